```python
import jax
import jax.numpy as jnp
from jax import lax
import numpy as np

D_MODEL = 1024
BATCH = 8
SEQ = 4096
DEPTH = 4

D_A = D_MODEL // 2
A_HEAD_DIM = 128
A_HEADS = D_A // A_HEAD_DIM
A_CHUNK = 64
D_B = D_MODEL // 4
B_BLOCKS = 4
B_BLOCK_DIM = D_B // B_BLOCKS
CONV_WIDTH = 4
LRU_C = 8.0
D_C = D_MODEL // 4
C_GROUPS = 4
C_GROUP_DIM = D_C // C_GROUPS
C_CHUNK = 128

D_MIX = D_A + D_B + D_C
SPLIT_SIZES = (D_A, D_A, D_A, D_A, D_B, D_B, D_C, D_C)
D_IN = sum(SPLIT_SIZES)
D_FF = ((8 * D_MODEL // 3 + 127) // 128) * 128
EPS = 1e-6

kernel_name = 'hybrid_hgrn2_rglru_sgu_macaron'


def rmsnorm(x, gain):
    x32 = x.astype(jnp.float32)
    y = x32 * lax.rsqrt(jnp.mean(x32 * x32, axis=-1, keepdims=True) + EPS)
    return (y * gain.astype(jnp.float32)).astype(x.dtype)


def group_rmsnorm(x, gain, n_groups):
    shp = x.shape
    x32 = x.astype(jnp.float32).reshape(shp[:-1] + (n_groups, shp[-1] // n_groups))
    y = x32 * lax.rsqrt(jnp.mean(x32 * x32, axis=-1, keepdims=True) + EPS)
    return y.reshape(shp) * gain.astype(jnp.float32)


def swiglu(x, w_gate, w_up, w_down):
    return (jax.nn.silu(x @ w_gate) * (x @ w_up)) @ w_down


def hgrn2(q, f_logit, i, g, lower_bound, norm_gain):
    bsz, seq, _ = q.shape
    n_chunks = seq // A_CHUNK
    f32 = jnp.float32
    q = jax.nn.silu(q.astype(f32))
    lb = lower_bound.astype(f32)
    forget = lb + (1.0 - lb) * jax.nn.sigmoid(f_logit.astype(f32))
    k = 1.0 - forget
    log_f = jnp.log(forget)

    def to_chunks(t):
        return t.reshape(bsz, n_chunks, A_CHUNK, A_HEADS, A_HEAD_DIM).transpose(1, 0, 3, 2, 4)

    qc, kc, vc = to_chunks(q), to_chunks(k), to_chunks(i.astype(f32))
    bc = jnp.cumsum(to_chunks(log_f), axis=3)
    causal = jnp.tril(jnp.ones((A_CHUNK, A_CHUNK), bool))[:, :, None]

    def chunk_step(state, inp):
        q_t, k_t, v_t, b_t = inp
        diff = jnp.where(causal, b_t[:, :, :, None, :] - b_t[:, :, None, :, :], -jnp.inf)
        scores = jnp.einsum('bhtk,bhtsk,bhsk->bhts', q_t, jnp.exp(diff), k_t)
        out = (jnp.einsum('bhts,bhsv->bhtv', scores, v_t)
               + jnp.einsum('bhtk,bhkv->bhtv', q_t * jnp.exp(b_t), state))
        b_end = b_t[:, :, -1:, :]
        state = (jnp.exp(b_end[:, :, 0, :, None]) * state
                 + jnp.einsum('bhsk,bhsv->bhkv', k_t * jnp.exp(b_end - b_t), v_t))
        return state, out

    state0 = jnp.zeros((bsz, A_HEADS, A_HEAD_DIM, A_HEAD_DIM), f32)
    _, o = lax.scan(chunk_step, state0, (qc, kc, vc, bc))
    o = o.transpose(1, 0, 3, 2, 4).reshape(bsz, seq, D_A)
    o = group_rmsnorm(o, norm_gain, A_HEADS) * jax.nn.silu(g.astype(f32))
    return o.astype(g.dtype)


def rglru(xb, gate, conv_w, conv_b, w_a, b_a, w_x, b_x, lam, norm_gain):
    bsz, seq, _ = xb.shape
    f32 = jnp.float32
    xp = jnp.pad(xb, ((0, 0), (CONV_WIDTH - 1, 0), (0, 0)))
    xc = conv_b + xp[:, 0:seq] * conv_w[0]
    for tap in range(1, CONV_WIDTH):
        xc = xc + xp[:, tap:tap + seq] * conv_w[tap]
    xh = xc.reshape(bsz, seq, B_BLOCKS, B_BLOCK_DIM)
    r = jax.nn.sigmoid((jnp.einsum('blhi,hij->blhj', xh, w_a) + b_a).astype(f32)).reshape(bsz, seq, D_B)
    gate_in = jax.nn.sigmoid((jnp.einsum('blhi,hij->blhj', xh, w_x) + b_x).astype(f32)).reshape(bsz, seq, D_B)
    log_a = -LRU_C * r * jax.nn.softplus(-lam.astype(f32))
    a = jnp.exp(log_a)
    mult = jnp.sqrt(-jnp.expm1(2.0 * log_a))
    u = mult * gate_in * xc.astype(f32)

    def combine(left, right):
        return right[0] * left[0], right[0] * left[1] + right[1]

    _, h = lax.associative_scan(combine, (a, u), axis=1)
    y = h * jax.nn.gelu(gate.astype(f32))
    return group_rmsnorm(y, norm_gain, B_BLOCKS).astype(xb.dtype)


def chunked_sgu(u_in, v_in, w_s, b_s, norm_gain):
    bsz, seq, _ = u_in.shape
    n_chunks = seq // C_CHUNK
    f32 = jnp.float32
    u = jax.nn.gelu(u_in.astype(f32))
    v = jax.nn.gelu(v_in.astype(f32)).reshape(bsz, n_chunks, C_CHUNK, C_GROUPS, C_GROUP_DIM)
    mu = jnp.mean(v, axis=-1, keepdims=True)
    var = jnp.mean(jnp.square(v - mu), axis=-1, keepdims=True)
    v = (v - mu) * lax.rsqrt(var + EPS)
    w = w_s.astype(f32) * jnp.tril(jnp.ones((C_CHUNK, C_CHUNK), f32))
    z = jnp.einsum('gts,bnsgc->bntgc', w, v) + b_s.astype(f32).T[:, :, None]
    y = u * z.reshape(bsz, seq, D_C)
    return group_rmsnorm(y, norm_gain, C_GROUPS).astype(u_in.dtype)


def _fwd_setup_inputs(seed: int = 0) -> dict:
    key = jax.random.key(seed)
    ks = jax.random.split(key, 32)
    f32 = jnp.float32

    def nrm(k, shape, scale):
        return jax.random.normal(k, shape, f32) * scale

    def gain(k, shape):
        return 1.0 + 0.05 * jax.random.normal(k, shape, f32)

    a0 = jax.random.uniform(ks[15], (DEPTH, D_B), f32, 0.9, 0.999)
    lam = jnp.log(a0) - jnp.log1p(-a0)
    return {
        'x': jax.random.normal(ks[0], (BATCH, SEQ, D_MODEL), f32),
        'ffn1_norm': gain(ks[1], (DEPTH, D_MODEL)),
        'ffn1_wg': nrm(ks[2], (DEPTH, D_MODEL, D_FF), D_MODEL ** -0.5),
        'ffn1_wu': nrm(ks[3], (DEPTH, D_MODEL, D_FF), D_MODEL ** -0.5),
        'ffn1_wd': nrm(ks[4], (DEPTH, D_FF, D_MODEL), D_FF ** -0.5),
        'mix_norm': gain(ks[5], (DEPTH, D_MODEL)),
        'w_in': nrm(ks[6], (DEPTH, D_MODEL, D_IN), D_MODEL ** -0.5),
        'hgrn_lb_logits': nrm(ks[7], (DEPTH, D_A), 0.5),
        'hgrn_norm': gain(ks[8], (DEPTH, D_A)),
        'conv_w': nrm(ks[9], (DEPTH, CONV_WIDTH, D_B), CONV_WIDTH ** -0.5),
        'conv_b': nrm(ks[10], (DEPTH, D_B), 0.02),
        'lru_wa': nrm(ks[11], (DEPTH, B_BLOCKS, B_BLOCK_DIM, B_BLOCK_DIM), B_BLOCK_DIM ** -0.5),
        'lru_ba': nrm(ks[12], (DEPTH, B_BLOCKS, B_BLOCK_DIM), 0.02),
        'lru_wx': nrm(ks[13], (DEPTH, B_BLOCKS, B_BLOCK_DIM, B_BLOCK_DIM), B_BLOCK_DIM ** -0.5),
        'lru_bx': nrm(ks[14], (DEPTH, B_BLOCKS, B_BLOCK_DIM), 0.02),
        'lru_lambda': lam,
        'lru_norm': gain(ks[16], (DEPTH, D_B)),
        'sgu_w': nrm(ks[17], (DEPTH, C_GROUPS, C_CHUNK, C_CHUNK), C_CHUNK ** -0.5),
        'sgu_b': gain(ks[18], (DEPTH, C_GROUPS, C_CHUNK)),
        'sgu_norm': gain(ks[19], (DEPTH, D_C)),
        'w_out': nrm(ks[20], (DEPTH, D_MIX, D_MODEL), D_MIX ** -0.5),
        'ffn2_norm': gain(ks[21], (DEPTH, D_MODEL)),
        'ffn2_wg': nrm(ks[22], (DEPTH, D_MODEL, D_FF), D_MODEL ** -0.5),
        'ffn2_wu': nrm(ks[23], (DEPTH, D_MODEL, D_FF), D_MODEL ** -0.5),
        'ffn2_wd': nrm(ks[24], (DEPTH, D_FF, D_MODEL), D_FF ** -0.5),
        'final_norm': gain(ks[25], (D_MODEL,)),
    }


def _fwd_reference(x, ffn1_norm, ffn1_wg, ffn1_wu, ffn1_wd, mix_norm, w_in, hgrn_lb_logits, hgrn_norm,
              conv_w, conv_b, lru_wa, lru_ba, lru_wx, lru_bx, lru_lambda, lru_norm,
              sgu_w, sgu_b, sgu_norm, w_out, ffn2_norm, ffn2_wg, ffn2_wu, ffn2_wd, final_norm):
    lb_soft = jax.nn.softmax(hgrn_lb_logits.astype(jnp.float32), axis=0)
    lower_bounds = jnp.cumsum(lb_soft, axis=0) - lb_soft[0]
    split_points = [int(p) for p in np.cumsum(SPLIT_SIZES)[:-1]]
    h = x
    for layer in range(DEPTH):
        h = h + 0.5 * swiglu(rmsnorm(h, ffn1_norm[layer]), ffn1_wg[layer], ffn1_wu[layer], ffn1_wd[layer])
        z = rmsnorm(h, mix_norm[layer]) @ w_in[layer]
        q, f_logit, i, g, xb, gate, u, v = jnp.split(z, split_points, axis=-1)
        out_a = hgrn2(q, f_logit, i, g, lower_bounds[layer], hgrn_norm[layer])
        out_b = rglru(xb, gate, conv_w[layer], conv_b[layer], lru_wa[layer], lru_ba[layer],
                      lru_wx[layer], lru_bx[layer], lru_lambda[layer], lru_norm[layer])
        out_c = chunked_sgu(u, v, sgu_w[layer], sgu_b[layer], sgu_norm[layer])
        h = h + jnp.concatenate([out_a, out_b, out_c], axis=-1) @ w_out[layer]
        h = h + 0.5 * swiglu(rmsnorm(h, ffn2_norm[layer]), ffn2_wg[layer], ffn2_wu[layer], ffn2_wd[layer])
    return rmsnorm(h, final_norm)


import jax as _jax
import jax.numpy as _jnp

TWIN_FORMAT = 'train_step'
FWD_PARAMS = ['x', 'ffn1_norm', 'ffn1_wg', 'ffn1_wu', 'ffn1_wd', 'mix_norm', 'w_in', 'hgrn_lb_logits', 'hgrn_norm', 'conv_w', 'conv_b', 'lru_wa', 'lru_ba', 'lru_wx', 'lru_bx', 'lru_lambda', 'lru_norm', 'sgu_w', 'sgu_b', 'sgu_norm', 'w_out', 'ffn2_norm', 'ffn2_wg', 'ffn2_wu', 'ffn2_wd', 'final_norm']
TWIN_WEIGHTS = ['ffn1_norm', 'ffn1_wg', 'ffn1_wu', 'ffn1_wd', 'mix_norm', 'w_in', 'hgrn_lb_logits', 'hgrn_norm', 'conv_w', 'conv_b', 'lru_wa', 'lru_ba', 'lru_wx', 'lru_bx', 'lru_lambda', 'lru_norm', 'sgu_w', 'sgu_b', 'sgu_norm', 'w_out', 'ffn2_norm', 'ffn2_wg', 'ffn2_wu', 'ffn2_wd', 'final_norm']
TWIN_DIFF_INPUT = 'x'
TWIN_INPUTS = ['x', 'ffn1_norm', 'ffn1_wg', 'ffn1_wu', 'ffn1_wd', 'mix_norm', 'w_in', 'hgrn_lb_logits', 'hgrn_norm', 'conv_w', 'conv_b', 'lru_wa', 'lru_ba', 'lru_wx', 'lru_bx', 'lru_lambda', 'lru_norm', 'sgu_w', 'sgu_b', 'sgu_norm', 'w_out', 'ffn2_norm', 'ffn2_wg', 'ffn2_wu', 'ffn2_wd', 'final_norm', 'loss_target', 'm_ffn1_norm', 'm_ffn1_wg', 'm_ffn1_wu', 'm_ffn1_wd', 'm_mix_norm', 'm_w_in', 'm_hgrn_lb_logits', 'm_hgrn_norm', 'm_conv_w', 'm_conv_b', 'm_lru_wa', 'm_lru_ba', 'm_lru_wx', 'm_lru_bx', 'm_lru_lambda', 'm_lru_norm', 'm_sgu_w', 'm_sgu_b', 'm_sgu_norm', 'm_w_out', 'm_ffn2_norm', 'm_ffn2_wg', 'm_ffn2_wu', 'm_ffn2_wd', 'm_final_norm', 'v_ffn1_norm', 'v_ffn1_wg', 'v_ffn1_wu', 'v_ffn1_wd', 'v_mix_norm', 'v_w_in', 'v_hgrn_lb_logits', 'v_hgrn_norm', 'v_conv_w', 'v_conv_b', 'v_lru_wa', 'v_lru_ba', 'v_lru_wx', 'v_lru_bx', 'v_lru_lambda', 'v_lru_norm', 'v_sgu_w', 'v_sgu_b', 'v_sgu_norm', 'v_w_out', 'v_ffn2_norm', 'v_ffn2_wg', 'v_ffn2_wu', 'v_ffn2_wd', 'v_final_norm']
TWIN_OUTPUTS = ['loss', 'grad_x', 'grad_ffn1_norm', 'grad_ffn1_wg', 'grad_ffn1_wu', 'grad_ffn1_wd', 'grad_mix_norm', 'grad_w_in', 'grad_hgrn_lb_logits', 'grad_hgrn_norm', 'grad_conv_w', 'grad_conv_b', 'grad_lru_wa', 'grad_lru_ba', 'grad_lru_wx', 'grad_lru_bx', 'grad_lru_lambda', 'grad_lru_norm', 'grad_sgu_w', 'grad_sgu_b', 'grad_sgu_norm', 'grad_w_out', 'grad_ffn2_norm', 'grad_ffn2_wg', 'grad_ffn2_wu', 'grad_ffn2_wd', 'grad_final_norm', 'delta_ffn1_norm', 'delta_ffn1_wg', 'delta_ffn1_wu', 'delta_ffn1_wd', 'delta_mix_norm', 'delta_w_in', 'delta_hgrn_lb_logits', 'delta_hgrn_norm', 'delta_conv_w', 'delta_conv_b', 'delta_lru_wa', 'delta_lru_ba', 'delta_lru_wx', 'delta_lru_bx', 'delta_lru_lambda', 'delta_lru_norm', 'delta_sgu_w', 'delta_sgu_b', 'delta_sgu_norm', 'delta_w_out', 'delta_ffn2_norm', 'delta_ffn2_wg', 'delta_ffn2_wu', 'delta_ffn2_wd', 'delta_final_norm', 'new_m_ffn1_norm', 'new_m_ffn1_wg', 'new_m_ffn1_wu', 'new_m_ffn1_wd', 'new_m_mix_norm', 'new_m_w_in', 'new_m_hgrn_lb_logits', 'new_m_hgrn_norm', 'new_m_conv_w', 'new_m_conv_b', 'new_m_lru_wa', 'new_m_lru_ba', 'new_m_lru_wx', 'new_m_lru_bx', 'new_m_lru_lambda', 'new_m_lru_norm', 'new_m_sgu_w', 'new_m_sgu_b', 'new_m_sgu_norm', 'new_m_w_out', 'new_m_ffn2_norm', 'new_m_ffn2_wg', 'new_m_ffn2_wu', 'new_m_ffn2_wd', 'new_m_final_norm', 'new_v_ffn1_norm', 'new_v_ffn1_wg', 'new_v_ffn1_wu', 'new_v_ffn1_wd', 'new_v_mix_norm', 'new_v_w_in', 'new_v_hgrn_lb_logits', 'new_v_hgrn_norm', 'new_v_conv_w', 'new_v_conv_b', 'new_v_lru_wa', 'new_v_lru_ba', 'new_v_lru_wx', 'new_v_lru_bx', 'new_v_lru_lambda', 'new_v_lru_norm', 'new_v_sgu_w', 'new_v_sgu_b', 'new_v_sgu_norm', 'new_v_w_out', 'new_v_ffn2_norm', 'new_v_ffn2_wg', 'new_v_ffn2_wu', 'new_v_ffn2_wd', 'new_v_final_norm']
TWIN_LEAF_KINDS = {'loss': 'loss', 'grad_x': 'grad_x', 'grad_ffn1_norm': 'grad_w', 'grad_ffn1_wg': 'grad_w', 'grad_ffn1_wu': 'grad_w', 'grad_ffn1_wd': 'grad_w', 'grad_mix_norm': 'grad_w', 'grad_w_in': 'grad_w', 'grad_hgrn_lb_logits': 'grad_w', 'grad_hgrn_norm': 'grad_w', 'grad_conv_w': 'grad_w', 'grad_conv_b': 'grad_w', 'grad_lru_wa': 'grad_w', 'grad_lru_ba': 'grad_w', 'grad_lru_wx': 'grad_w', 'grad_lru_bx': 'grad_w', 'grad_lru_lambda': 'grad_w', 'grad_lru_norm': 'grad_w', 'grad_sgu_w': 'grad_w', 'grad_sgu_b': 'grad_w', 'grad_sgu_norm': 'grad_w', 'grad_w_out': 'grad_w', 'grad_ffn2_norm': 'grad_w', 'grad_ffn2_wg': 'grad_w', 'grad_ffn2_wu': 'grad_w', 'grad_ffn2_wd': 'grad_w', 'grad_final_norm': 'grad_w', 'delta_ffn1_norm': 'delta_w', 'delta_ffn1_wg': 'delta_w', 'delta_ffn1_wu': 'delta_w', 'delta_ffn1_wd': 'delta_w', 'delta_mix_norm': 'delta_w', 'delta_w_in': 'delta_w', 'delta_hgrn_lb_logits': 'delta_w', 'delta_hgrn_norm': 'delta_w', 'delta_conv_w': 'delta_w', 'delta_conv_b': 'delta_w', 'delta_lru_wa': 'delta_w', 'delta_lru_ba': 'delta_w', 'delta_lru_wx': 'delta_w', 'delta_lru_bx': 'delta_w', 'delta_lru_lambda': 'delta_w', 'delta_lru_norm': 'delta_w', 'delta_sgu_w': 'delta_w', 'delta_sgu_b': 'delta_w', 'delta_sgu_norm': 'delta_w', 'delta_w_out': 'delta_w', 'delta_ffn2_norm': 'delta_w', 'delta_ffn2_wg': 'delta_w', 'delta_ffn2_wu': 'delta_w', 'delta_ffn2_wd': 'delta_w', 'delta_final_norm': 'delta_w', 'new_m_ffn1_norm': 'new_m', 'new_m_ffn1_wg': 'new_m', 'new_m_ffn1_wu': 'new_m', 'new_m_ffn1_wd': 'new_m', 'new_m_mix_norm': 'new_m', 'new_m_w_in': 'new_m', 'new_m_hgrn_lb_logits': 'new_m', 'new_m_hgrn_norm': 'new_m', 'new_m_conv_w': 'new_m', 'new_m_conv_b': 'new_m', 'new_m_lru_wa': 'new_m', 'new_m_lru_ba': 'new_m', 'new_m_lru_wx': 'new_m', 'new_m_lru_bx': 'new_m', 'new_m_lru_lambda': 'new_m', 'new_m_lru_norm': 'new_m', 'new_m_sgu_w': 'new_m', 'new_m_sgu_b': 'new_m', 'new_m_sgu_norm': 'new_m', 'new_m_w_out': 'new_m', 'new_m_ffn2_norm': 'new_m', 'new_m_ffn2_wg': 'new_m', 'new_m_ffn2_wu': 'new_m', 'new_m_ffn2_wd': 'new_m', 'new_m_final_norm': 'new_m', 'new_v_ffn1_norm': 'new_v', 'new_v_ffn1_wg': 'new_v', 'new_v_ffn1_wu': 'new_v', 'new_v_ffn1_wd': 'new_v', 'new_v_mix_norm': 'new_v', 'new_v_w_in': 'new_v', 'new_v_hgrn_lb_logits': 'new_v', 'new_v_hgrn_norm': 'new_v', 'new_v_conv_w': 'new_v', 'new_v_conv_b': 'new_v', 'new_v_lru_wa': 'new_v', 'new_v_lru_ba': 'new_v', 'new_v_lru_wx': 'new_v', 'new_v_lru_bx': 'new_v', 'new_v_lru_lambda': 'new_v', 'new_v_lru_norm': 'new_v', 'new_v_sgu_w': 'new_v', 'new_v_sgu_b': 'new_v', 'new_v_sgu_norm': 'new_v', 'new_v_w_out': 'new_v', 'new_v_ffn2_norm': 'new_v', 'new_v_ffn2_wg': 'new_v', 'new_v_ffn2_wu': 'new_v', 'new_v_ffn2_wd': 'new_v', 'new_v_final_norm': 'new_v'}


def _forward(args):
    return _fwd_reference(*[args[k] for k in FWD_PARAMS])


def _output_shape():
    out = _jax.eval_shape(lambda: _forward(_fwd_setup_inputs(0)))
    return out.shape, out.dtype

N_MICROBATCH = 1
ADAM_LR = 0.001
ADAM_B1 = 0.9
ADAM_B2 = 0.999
ADAM_EPS = 1e-08
ADAM_WD = 0.01
ADAM_STEP = 10
PER_EXAMPLE_BATCH_AXIS = {'x': 0, 'loss_target': 0}
SHARED_INPUTS = []
_WEIGHT_DTYPES = {'ffn1_norm': _jnp.float32, 'ffn1_wg': _jnp.float32, 'ffn1_wu': _jnp.float32, 'ffn1_wd': _jnp.float32, 'mix_norm': _jnp.float32, 'w_in': _jnp.float32, 'hgrn_lb_logits': _jnp.float32, 'hgrn_norm': _jnp.float32, 'conv_w': _jnp.float32, 'conv_b': _jnp.float32, 'lru_wa': _jnp.float32, 'lru_ba': _jnp.float32, 'lru_wx': _jnp.float32, 'lru_bx': _jnp.float32, 'lru_lambda': _jnp.float32, 'lru_norm': _jnp.float32, 'sgu_w': _jnp.float32, 'sgu_b': _jnp.float32, 'sgu_norm': _jnp.float32, 'w_out': _jnp.float32, 'ffn2_norm': _jnp.float32, 'ffn2_wg': _jnp.float32, 'ffn2_wu': _jnp.float32, 'ffn2_wd': _jnp.float32, 'final_norm': _jnp.float32}
MOMENT_SCALE = {'ffn1_norm': 8.390695e-02, 'ffn1_wg': 3.545852e-02, 'ffn1_wu': 3.437242e-02, 'ffn1_wd': 5.708422e-02, 'mix_norm': 1.739715e-01, 'w_in': 9.464307e-02, 'hgrn_lb_logits': 4.838065e-03, 'hgrn_norm': 8.332300e-02, 'conv_w': 1.659748e-01, 'conv_b': 6.578611e-01, 'lru_wa': 3.733939e-02, 'lru_ba': 3.461807e-02, 'lru_wx': 6.422794e-02, 'lru_bx': 5.056663e-02, 'lru_lambda': 7.190334e-02, 'lru_norm': 1.402123e-01, 'sgu_w': 5.378348e-02, 'sgu_b': 4.508203e-02, 'sgu_norm': 1.578841e-01, 'w_out': 1.276790e-01, 'ffn2_norm': 5.697479e-02, 'ffn2_wg': 2.453626e-02, 'ffn2_wu': 2.398270e-02, 'ffn2_wd': 3.979369e-02, 'final_norm': 3.207763e+01}


def _to_microbatches(a, axis):
    t = _jnp.moveaxis(a, axis, 0)
    t = t.reshape((N_MICROBATCH, t.shape[0] // N_MICROBATCH) + t.shape[1:])
    return _jnp.moveaxis(t, 1, axis + 1)


def setup_inputs(seed: int = 0) -> dict:
    inp = _fwd_setup_inputs(seed)
    key = _jax.random.fold_in(_jax.random.key(seed), 7919)
    shape, _ = _output_shape()
    out = dict(inp)
    out["loss_target"] = _jax.random.normal(_jax.random.fold_in(key, 0), shape, _jnp.float32)
    for i, name in enumerate(TWIN_WEIGHTS):
        w = inp[name].astype(_jnp.float32)
        if MOMENT_SCALE is None:
            s = _jnp.sqrt(_jnp.mean(_jnp.square(w)) + 1e-30)
        else:
            s = MOMENT_SCALE[name]
        km, kv = _jax.random.split(_jax.random.fold_in(key, i + 1))
        out[name] = w
        out["m_" + name] = s * _jax.random.normal(km, w.shape, _jnp.float32)
        out["v_" + name] = (s * s) * _jax.random.uniform(kv, w.shape, _jnp.float32, 0.5, 1.5)
    if N_MICROBATCH > 1:
        for name, axis in PER_EXAMPLE_BATCH_AXIS.items():
            out[name] = _to_microbatches(out[name], axis)
    return {'x': out['x'], 'ffn1_norm': out['ffn1_norm'], 'ffn1_wg': out['ffn1_wg'], 'ffn1_wu': out['ffn1_wu'], 'ffn1_wd': out['ffn1_wd'], 'mix_norm': out['mix_norm'], 'w_in': out['w_in'], 'hgrn_lb_logits': out['hgrn_lb_logits'], 'hgrn_norm': out['hgrn_norm'], 'conv_w': out['conv_w'], 'conv_b': out['conv_b'], 'lru_wa': out['lru_wa'], 'lru_ba': out['lru_ba'], 'lru_wx': out['lru_wx'], 'lru_bx': out['lru_bx'], 'lru_lambda': out['lru_lambda'], 'lru_norm': out['lru_norm'], 'sgu_w': out['sgu_w'], 'sgu_b': out['sgu_b'], 'sgu_norm': out['sgu_norm'], 'w_out': out['w_out'], 'ffn2_norm': out['ffn2_norm'], 'ffn2_wg': out['ffn2_wg'], 'ffn2_wu': out['ffn2_wu'], 'ffn2_wd': out['ffn2_wd'], 'final_norm': out['final_norm'], 'loss_target': out['loss_target'], 'm_ffn1_norm': out['m_ffn1_norm'], 'm_ffn1_wg': out['m_ffn1_wg'], 'm_ffn1_wu': out['m_ffn1_wu'], 'm_ffn1_wd': out['m_ffn1_wd'], 'm_mix_norm': out['m_mix_norm'], 'm_w_in': out['m_w_in'], 'm_hgrn_lb_logits': out['m_hgrn_lb_logits'], 'm_hgrn_norm': out['m_hgrn_norm'], 'm_conv_w': out['m_conv_w'], 'm_conv_b': out['m_conv_b'], 'm_lru_wa': out['m_lru_wa'], 'm_lru_ba': out['m_lru_ba'], 'm_lru_wx': out['m_lru_wx'], 'm_lru_bx': out['m_lru_bx'], 'm_lru_lambda': out['m_lru_lambda'], 'm_lru_norm': out['m_lru_norm'], 'm_sgu_w': out['m_sgu_w'], 'm_sgu_b': out['m_sgu_b'], 'm_sgu_norm': out['m_sgu_norm'], 'm_w_out': out['m_w_out'], 'm_ffn2_norm': out['m_ffn2_norm'], 'm_ffn2_wg': out['m_ffn2_wg'], 'm_ffn2_wu': out['m_ffn2_wu'], 'm_ffn2_wd': out['m_ffn2_wd'], 'm_final_norm': out['m_final_norm'], 'v_ffn1_norm': out['v_ffn1_norm'], 'v_ffn1_wg': out['v_ffn1_wg'], 'v_ffn1_wu': out['v_ffn1_wu'], 'v_ffn1_wd': out['v_ffn1_wd'], 'v_mix_norm': out['v_mix_norm'], 'v_w_in': out['v_w_in'], 'v_hgrn_lb_logits': out['v_hgrn_lb_logits'], 'v_hgrn_norm': out['v_hgrn_norm'], 'v_conv_w': out['v_conv_w'], 'v_conv_b': out['v_conv_b'], 'v_lru_wa': out['v_lru_wa'], 'v_lru_ba': out['v_lru_ba'], 'v_lru_wx': out['v_lru_wx'], 'v_lru_bx': out['v_lru_bx'], 'v_lru_lambda': out['v_lru_lambda'], 'v_lru_norm': out['v_lru_norm'], 'v_sgu_w': out['v_sgu_w'], 'v_sgu_b': out['v_sgu_b'], 'v_sgu_norm': out['v_sgu_norm'], 'v_w_out': out['v_w_out'], 'v_ffn2_norm': out['v_ffn2_norm'], 'v_ffn2_wg': out['v_ffn2_wg'], 'v_ffn2_wu': out['v_ffn2_wu'], 'v_ffn2_wd': out['v_ffn2_wd'], 'v_final_norm': out['v_final_norm']}


def _loss(weights, diff, rest, loss_target):
    with _jax.named_scope("forward"):
        args = {**rest, TWIN_DIFF_INPUT: diff, **{k: w.astype(_WEIGHT_DTYPES[k]) for k, w in weights.items()}}
        y = _forward(args)
    with _jax.named_scope("loss_head"):
        err = _jnp.square(y.astype(_jnp.float32) - loss_target)
        return 0.5 * _jnp.sum(_jnp.mean(err, axis=-1)) if err.ndim else 0.5 * err


def _adamw(w, g, m, v):
    m = ADAM_B1 * m + (1.0 - ADAM_B1) * g
    v = ADAM_B2 * v + (1.0 - ADAM_B2) * _jnp.square(g)
    m_hat = m / (1.0 - ADAM_B1 ** ADAM_STEP)
    v_hat = v / (1.0 - ADAM_B2 ** ADAM_STEP)
    delta = -ADAM_LR * (m_hat / (_jnp.sqrt(v_hat) + ADAM_EPS) + ADAM_WD * w)
    return delta, m, v


def reference(x, ffn1_norm, ffn1_wg, ffn1_wu, ffn1_wd, mix_norm, w_in, hgrn_lb_logits, hgrn_norm, conv_w, conv_b, lru_wa, lru_ba, lru_wx, lru_bx, lru_lambda, lru_norm, sgu_w, sgu_b, sgu_norm, w_out, ffn2_norm, ffn2_wg, ffn2_wu, ffn2_wd, final_norm, loss_target, m_ffn1_norm, m_ffn1_wg, m_ffn1_wu, m_ffn1_wd, m_mix_norm, m_w_in, m_hgrn_lb_logits, m_hgrn_norm, m_conv_w, m_conv_b, m_lru_wa, m_lru_ba, m_lru_wx, m_lru_bx, m_lru_lambda, m_lru_norm, m_sgu_w, m_sgu_b, m_sgu_norm, m_w_out, m_ffn2_norm, m_ffn2_wg, m_ffn2_wu, m_ffn2_wd, m_final_norm, v_ffn1_norm, v_ffn1_wg, v_ffn1_wu, v_ffn1_wd, v_mix_norm, v_w_in, v_hgrn_lb_logits, v_hgrn_norm, v_conv_w, v_conv_b, v_lru_wa, v_lru_ba, v_lru_wx, v_lru_bx, v_lru_lambda, v_lru_norm, v_sgu_w, v_sgu_b, v_sgu_norm, v_w_out, v_ffn2_norm, v_ffn2_wg, v_ffn2_wu, v_ffn2_wd, v_final_norm):
    given = dict(x=x, ffn1_norm=ffn1_norm, ffn1_wg=ffn1_wg, ffn1_wu=ffn1_wu, ffn1_wd=ffn1_wd, mix_norm=mix_norm, w_in=w_in, hgrn_lb_logits=hgrn_lb_logits, hgrn_norm=hgrn_norm, conv_w=conv_w, conv_b=conv_b, lru_wa=lru_wa, lru_ba=lru_ba, lru_wx=lru_wx, lru_bx=lru_bx, lru_lambda=lru_lambda, lru_norm=lru_norm, sgu_w=sgu_w, sgu_b=sgu_b, sgu_norm=sgu_norm, w_out=w_out, ffn2_norm=ffn2_norm, ffn2_wg=ffn2_wg, ffn2_wu=ffn2_wu, ffn2_wd=ffn2_wd, final_norm=final_norm, loss_target=loss_target, m_ffn1_norm=m_ffn1_norm, m_ffn1_wg=m_ffn1_wg, m_ffn1_wu=m_ffn1_wu, m_ffn1_wd=m_ffn1_wd, m_mix_norm=m_mix_norm, m_w_in=m_w_in, m_hgrn_lb_logits=m_hgrn_lb_logits, m_hgrn_norm=m_hgrn_norm, m_conv_w=m_conv_w, m_conv_b=m_conv_b, m_lru_wa=m_lru_wa, m_lru_ba=m_lru_ba, m_lru_wx=m_lru_wx, m_lru_bx=m_lru_bx, m_lru_lambda=m_lru_lambda, m_lru_norm=m_lru_norm, m_sgu_w=m_sgu_w, m_sgu_b=m_sgu_b, m_sgu_norm=m_sgu_norm, m_w_out=m_w_out, m_ffn2_norm=m_ffn2_norm, m_ffn2_wg=m_ffn2_wg, m_ffn2_wu=m_ffn2_wu, m_ffn2_wd=m_ffn2_wd, m_final_norm=m_final_norm, v_ffn1_norm=v_ffn1_norm, v_ffn1_wg=v_ffn1_wg, v_ffn1_wu=v_ffn1_wu, v_ffn1_wd=v_ffn1_wd, v_mix_norm=v_mix_norm, v_w_in=v_w_in, v_hgrn_lb_logits=v_hgrn_lb_logits, v_hgrn_norm=v_hgrn_norm, v_conv_w=v_conv_w, v_conv_b=v_conv_b, v_lru_wa=v_lru_wa, v_lru_ba=v_lru_ba, v_lru_wx=v_lru_wx, v_lru_bx=v_lru_bx, v_lru_lambda=v_lru_lambda, v_lru_norm=v_lru_norm, v_sgu_w=v_sgu_w, v_sgu_b=v_sgu_b, v_sgu_norm=v_sgu_norm, v_w_out=v_w_out, v_ffn2_norm=v_ffn2_norm, v_ffn2_wg=v_ffn2_wg, v_ffn2_wu=v_ffn2_wu, v_ffn2_wd=v_ffn2_wd, v_final_norm=v_final_norm)
    weights = {n: given[n] for n in TWIN_WEIGHTS}
    shared = {n: given[n] for n in SHARED_INPUTS}
    per_example = {n: given[n] for n in ['x']}
    grad_fn = _jax.value_and_grad(_loss, argnums=(0, 1))

    def one_microbatch(ex, loss_target):
        ex = dict(ex)
        diff = ex.pop(TWIN_DIFF_INPUT)
        return grad_fn(weights, diff, {**shared, **ex}, loss_target)

    if N_MICROBATCH == 1:
        loss, (grad_w, grad_x) = one_microbatch(per_example, given["loss_target"])
    else:
        def body(carry, xs):
            loss_sum, grad_sum = carry
            l_k, (gw_k, gx_k) = one_microbatch(xs[0], xs[1])
            with _jax.named_scope("update"):
                return (loss_sum + l_k, _jax.tree.map(_jnp.add, grad_sum, gw_k)), gx_k

        init = (_jnp.zeros((), _jnp.float32), _jax.tree.map(_jnp.zeros_like, weights))
        (loss, grad_w), grad_x = _jax.lax.scan(body, init, (per_example, given["loss_target"]))
    with _jax.named_scope("update"):
        delta_w, new_m, new_v = {}, {}, {}
        for n in TWIN_WEIGHTS:
            delta_w[n], new_m[n], new_v[n] = _adamw(weights[n], grad_w[n], given["m_" + n], given["v_" + n])
    return (loss, grad_x, *[grad_w[n] for n in TWIN_WEIGHTS], *[delta_w[n] for n in TWIN_WEIGHTS],
            *[new_m[n] for n in TWIN_WEIGHTS], *[new_v[n] for n in TWIN_WEIGHTS])
```

```python
import math

import numpy as np
import jax
import jax.numpy as jnp
from jax import lax
from jax.experimental import pallas as pl
from jax.experimental.pallas import tpu as pltpu

f32 = jnp.float32
bf16 = jnp.bfloat16
HI = lax.Precision.HIGHEST
MESH = pl.DeviceIdType.MESH

EPS = 1e-6
HEAD = 128
A_CHUNK = 64
A_SUB = 16
B_BLOCKS = 4
B_CHUNK = 256
CONV_WIDTH = 4
LRU_C = 8.0
C_GROUPS = 4
C_CHUNK = 128
N_CHIPS = 4
ADAM_LR, ADAM_B1, ADAM_B2, ADAM_EPS, ADAM_WD, ADAM_STEP = 0.001, 0.9, 0.999, 1e-08, 0.01, 10
VMEM_LIMIT = 56 * 1024 * 1024


def _cparams(n_axes):
    return pltpu.CompilerParams(dimension_semantics=("arbitrary",) * n_axes, vmem_limit_bytes=VMEM_LIMIT)


def _sds(shape, dtype):
    return jax.ShapeDtypeStruct(tuple(shape), dtype)


def _full(shape):
    n = len(shape)
    return pl.BlockSpec(tuple(shape), lambda *_: (0,) * n)


def _dot(a, b):
    return jnp.dot(a, b, preferred_element_type=f32)


def _dot_nt(a, b):
    return lax.dot_general(a, b, (((1,), (1,)), ((), ())), preferred_element_type=f32)


def _dot_tn(a, b):
    return lax.dot_general(a, b, (((0,), (0,)), ((), ())), preferred_element_type=f32)


def _silu(x):
    return x * jax.nn.sigmoid(x)


def _group_avg_matrix(n, group):
    idx = np.arange(n) // group
    return jnp.asarray((idx[:, None] == idx[None, :]).astype(np.float32) / group)


class _Place:
    def __init__(self):
        self.x, self.y, self.c = lax.axis_index("x"), lax.axis_index("y"), lax.axis_index("c")
        self.slot = 2 * self.x + self.y

    def peer(self, kind):
        x, y, c = self.x, self.y, self.c
        return {"sib": (x, y, 1 - c), "fx": (1 - x, y, c), "fy": (x, 1 - y, c), "fxy": (1 - x, 1 - y, c)}[kind]

    def peer_slot(self, kind):
        x, y = self.x, self.y
        return {"fx": 2 * (1 - x) + y, "fy": 2 * x + (1 - y), "fxy": 2 * (1 - x) + (1 - y)}[kind]


CHIP_KINDS = ("fx", "fy", "fxy")


def _exchange(name, ins, outs, remote, local=(), aliases=None):
    n_in, n_out, n_r, n_l = len(ins), len(outs), len(remote), len(local)

    def body(*refs):
        in_refs, out_refs = refs[:n_in], refs[n_in:n_in + n_out]
        send, recv, lsem = refs[n_in + n_out:]
        p = _Place()
        lcopies = []
        for t, (src, dst) in enumerate(local):
            cp = pltpu.make_async_copy(src(in_refs, out_refs, p), dst(in_refs, out_refs, p), lsem.at[t])
            cp.start()
            lcopies.append(cp)
        copies = []
        for t, (src, dst, kind) in enumerate(remote):
            cp = pltpu.make_async_remote_copy(
                src_ref=src(in_refs, out_refs, p), dst_ref=dst(in_refs, out_refs, p),
                send_sem=send.at[t], recv_sem=recv.at[t], device_id=p.peer(kind), device_id_type=MESH)
            cp.start()
            copies.append(cp)
        for cp in copies:
            cp.wait_recv()
        for cp in copies:
            cp.wait_send()
        for cp in lcopies:
            cp.wait()

    anyspec = pl.BlockSpec(memory_space=pl.ANY)
    res = pl.pallas_call(
        body, name=name, out_shape=tuple(outs),
        in_specs=[anyspec] * n_in, out_specs=tuple([anyspec] * n_out),
        scratch_shapes=[pltpu.SemaphoreType.DMA((n_r,)), pltpu.SemaphoreType.DMA((n_r,)),
                        pltpu.SemaphoreType.DMA((max(n_l, 1),))],
        input_output_aliases=aliases or {},
        compiler_params=pltpu.CompilerParams(has_side_effects=True),
    )(*ins)
    return list(res)


def _allgather_shards(shards):
    n = len(shards)
    half = shards[0].shape[0] // 2
    outs = [_sds((N_CHIPS,) + s.shape, s.dtype) for s in shards]
    remote, local = [], []
    for a in range(n):
        local.append((lambda i, o, p, a=a: i[a], lambda i, o, p, a=a: o[a].at[p.slot]))
        for kind in CHIP_KINDS:
            remote.append((lambda i, o, p, a=a: i[a].at[pl.ds(p.c * half, half)],
                           lambda i, o, p, a=a: o[a].at[p.slot, pl.ds(p.c * half, half)], kind))
    got = _exchange("allgather_ici", shards, outs, remote, local)
    remote2 = []
    for a in range(n):
        for kind in CHIP_KINDS:
            view = lambda i, o, p, a=a, kind=kind: o[a].at[p.peer_slot(kind), pl.ds(p.c * half, half)]
            remote2.append((view, view, "sib"))
    return _exchange("allgather_d2d", got, outs, remote2, aliases={a: a for a in range(n)})


def _ffn_fwd(h, gain, wg, wu, wd, layer, tm):
    T, D = h.shape
    nsh, F = wg.shape[0], wg.shape[3]
    nt = T // tm

    def body(h_ref, gain_ref, wg_ref, wu_ref, wd_ref, out_ref, gs_ref, us_ref, xn_ref, acc_ref):
        k = pl.program_id(1)

        @pl.when(k == 0)
        def _():
            hv = h_ref[...]
            r = lax.rsqrt(jnp.mean(hv * hv, axis=-1, keepdims=True) + EPS)
            xn_ref[...] = (hv * r * gain_ref[...]).astype(bf16)
            acc_ref[...] = jnp.zeros_like(acc_ref)

        xn = xn_ref[...]
        g = _dot(xn, wg_ref[...])
        u = _dot(xn, wu_ref[...])
        gs_ref[...] = g.astype(bf16)
        us_ref[...] = u.astype(bf16)
        a = (_silu(g) * u).astype(bf16)
        acc_ref[...] += _dot(a, wd_ref[...])

        @pl.when(k == nsh - 1)
        def _():
            out_ref[...] = h_ref[...] + 0.5 * acc_ref[...]

    wspec = pl.BlockSpec((None, None, D, F), lambda i, k: (k, layer, 0, 0))
    return pl.pallas_call(
        body, name="ffn_fwd", grid=(nt, nsh),
        in_specs=[pl.BlockSpec((tm, D), lambda i, k: (i, 0)), _full((1, D)), wspec, wspec,
                  pl.BlockSpec((None, None, F, D), lambda i, k: (k, layer, 0, 0))],
        out_specs=(pl.BlockSpec((tm, D), lambda i, k: (i, 0)),
                   pl.BlockSpec((None, tm, F), lambda i, k: (k, i, 0)),
                   pl.BlockSpec((None, tm, F), lambda i, k: (k, i, 0))),
        out_shape=(_sds((T, D), f32), _sds((nsh, T, F), bf16), _sds((nsh, T, F), bf16)),
        scratch_shapes=[pltpu.VMEM((tm, D), bf16), pltpu.VMEM((tm, D), f32)],
        compiler_params=_cparams(2),
    )(h, gain, wg, wu, wd)


def _ffn_bwd_dgrad(h, gain, dout, gs, us, wg, wu, wd, layer, tm):
    T, D = h.shape
    nsh, F = wg.shape[0], wg.shape[3]
    nt = T // tm

    def body(h_ref, gain_ref, dout_ref, gs_ref, us_ref, wg_ref, wu_ref, wd_ref,
             dh_ref, dgain_ref, dg_ref, du_ref, xn_ref, dob_ref, xh_ref, acc_ref):
        i, k = pl.program_id(0), pl.program_id(1)

        @pl.when((i == 0) & (k == 0))
        def _():
            dgain_ref[...] = jnp.zeros_like(dgain_ref)

        @pl.when(k == 0)
        def _():
            hv = h_ref[...]
            r = lax.rsqrt(jnp.mean(hv * hv, axis=-1, keepdims=True) + EPS)
            xh = hv * r
            xh_ref[...] = xh
            xn_ref[...] = (xh * gain_ref[...]).astype(bf16)
            dob_ref[...] = (0.5 * dout_ref[...]).astype(bf16)
            acc_ref[...] = jnp.zeros_like(acc_ref)

        da = _dot_nt(dob_ref[...], wd_ref[...])
        g = gs_ref[...].astype(f32)
        u = us_ref[...].astype(f32)
        sg = jax.nn.sigmoid(g)
        dg = (da * u * (sg * (1.0 + g * (1.0 - sg)))).astype(bf16)
        du = (da * (g * sg)).astype(bf16)
        dg_ref[...] = dg
        du_ref[...] = du
        acc_ref[...] += _dot_nt(dg, wg_ref[...]) + _dot_nt(du, wu_ref[...])

        @pl.when(k == nsh - 1)
        def _():
            hv = h_ref[...]
            r = lax.rsqrt(jnp.mean(hv * hv, axis=-1, keepdims=True) + EPS)
            xh = xh_ref[...]
            dxn = acc_ref[...]
            dgain_ref[...] += jnp.sum(dxn * xh, axis=0, keepdims=True)
            dxh = dxn * gain_ref[...]
            dh_ref[...] = dout_ref[...] + r * (dxh - xh * jnp.mean(dxh * xh, axis=-1, keepdims=True))

    tok = pl.BlockSpec((tm, D), lambda i, k: (i, 0))
    sav = pl.BlockSpec((None, tm, F), lambda i, k: (k, i, 0))
    wspec = pl.BlockSpec((None, None, D, F), lambda i, k: (k, layer, 0, 0))
    return pl.pallas_call(
        body, name="ffn_bwd_dgrad", grid=(nt, nsh),
        in_specs=[tok, _full((1, D)), tok, sav, sav, wspec, wspec,
                  pl.BlockSpec((None, None, F, D), lambda i, k: (k, layer, 0, 0))],
        out_specs=(tok, _full((1, D)), sav, sav, tok, tok),
        out_shape=(_sds((T, D), f32), _sds((1, D), f32), _sds((nsh, T, F), bf16), _sds((nsh, T, F), bf16),
                   _sds((T, D), bf16), _sds((T, D), bf16)),
        scratch_shapes=[pltpu.VMEM((tm, D), f32), pltpu.VMEM((tm, D), f32)],
        compiler_params=_cparams(2),
    )(h, gain, dout, gs, us, wg, wu, wd)


def _ffn_bwd_wgrad(xn, dob, gs, us, dg, du, acc_bufs, layer, n_layers, tm):
    T, D = xn.shape
    nsh, F = gs.shape[0], gs.shape[2]
    nt = T // tm

    def body(xn_ref, dob_ref, gs_ref, us_ref, dg_ref, du_ref, *rest):
        rest = rest[len(acc_bufs):]
        dwg_ref, dwu_ref, dwd_ref, ag_ref, au_ref, ad_ref = rest
        i = pl.program_id(1)

        @pl.when(i == 0)
        def _():
            ag_ref[...] = jnp.zeros_like(ag_ref)
            au_ref[...] = jnp.zeros_like(au_ref)
            ad_ref[...] = jnp.zeros_like(ad_ref)

        xn_v = xn_ref[...]
        ag_ref[...] += _dot_tn(xn_v, dg_ref[...])
        au_ref[...] += _dot_tn(xn_v, du_ref[...])
        g = gs_ref[...].astype(f32)
        a = (_silu(g) * us_ref[...].astype(f32)).astype(bf16)
        ad_ref[...] += _dot_tn(a, dob_ref[...])

        @pl.when(i == nt - 1)
        def _():
            dwg_ref[...] = ag_ref[...].astype(bf16)
            dwu_ref[...] = au_ref[...].astype(bf16)
            dwd_ref[...] = ad_ref[...].astype(bf16)

    tok = pl.BlockSpec((tm, D), lambda k, i: (i, 0))
    sav = pl.BlockSpec((None, tm, F), lambda k, i: (k, i, 0))
    anyspec = pl.BlockSpec(memory_space=pl.ANY)
    wspec = pl.BlockSpec((None, None, D, F), lambda k, i: (k, layer, 0, 0))
    wdspec = pl.BlockSpec((None, None, F, D), lambda k, i: (k, layer, 0, 0))
    n_acc = len(acc_bufs)
    return pl.pallas_call(
        body, name="ffn_bwd_wgrad", grid=(nsh, nt),
        in_specs=[tok, tok, sav, sav, sav, sav] + [anyspec] * n_acc,
        out_specs=(wspec, wspec, wdspec),
        out_shape=(_sds((nsh, n_layers, D, F), bf16), _sds((nsh, n_layers, D, F), bf16), _sds((nsh, n_layers, F, D), bf16)),
        scratch_shapes=[pltpu.VMEM((D, F), f32), pltpu.VMEM((D, F), f32), pltpu.VMEM((F, D), f32)],
        input_output_aliases={6 + j: j for j in range(n_acc)},
        compiler_params=_cparams(2),
    )(xn, dob, gs, us, dg, du, *acc_bufs)


def _proj_in_fwd(h, gain, w_in, layer, tm):
    T, D = h.shape
    nsh, N = w_in.shape[0], w_in.shape[3]
    nt = T // tm

    def body(h_ref, gain_ref, w_ref, z_ref, xn_ref):
        @pl.when(pl.program_id(1) == 0)
        def _():
            hv = h_ref[...]
            r = lax.rsqrt(jnp.mean(hv * hv, axis=-1, keepdims=True) + EPS)
            xn_ref[...] = (hv * r * gain_ref[...]).astype(bf16)

        z_ref[...] = _dot(xn_ref[...], w_ref[...])

    return pl.pallas_call(
        body, name="proj_in_fwd", grid=(nt, nsh),
        in_specs=[pl.BlockSpec((tm, D), lambda i, k: (i, 0)), _full((1, D)),
                  pl.BlockSpec((None, None, D, N), lambda i, k: (k, layer, 0, 0))],
        out_specs=pl.BlockSpec((tm, N), lambda i, k: (i, k)),
        out_shape=_sds((T, nsh * N), f32),
        scratch_shapes=[pltpu.VMEM((tm, D), bf16)],
        compiler_params=_cparams(2),
    )(h, gain, w_in)


def _proj_in_bwd_dgrad(h, gain, dres, dz, w_in, layer, tm):
    T, D = h.shape
    nsh, N = w_in.shape[0], w_in.shape[3]
    nt = T // tm

    def body(h_ref, gain_ref, dres_ref, dz_ref, w_ref, dh_ref, dgain_ref, xn_ref, acc_ref):
        i, k = pl.program_id(0), pl.program_id(1)

        @pl.when((i == 0) & (k == 0))
        def _():
            dgain_ref[...] = jnp.zeros_like(dgain_ref)

        @pl.when(k == 0)
        def _():
            acc_ref[...] = jnp.zeros_like(acc_ref)

        acc_ref[...] += _dot_nt(dz_ref[...].astype(bf16), w_ref[...])

        @pl.when(k == nsh - 1)
        def _():
            hv = h_ref[...]
            r = lax.rsqrt(jnp.mean(hv * hv, axis=-1, keepdims=True) + EPS)
            xh = hv * r
            xn_ref[...] = (xh * gain_ref[...]).astype(bf16)
            dxn = acc_ref[...]
            dgain_ref[...] += jnp.sum(dxn * xh, axis=0, keepdims=True)
            dxh = dxn * gain_ref[...]
            dh_ref[...] = dres_ref[...] + r * (dxh - xh * jnp.mean(dxh * xh, axis=-1, keepdims=True))

    tok = pl.BlockSpec((tm, D), lambda i, k: (i, 0))
    return pl.pallas_call(
        body, name="proj_in_bwd_dgrad", grid=(nt, nsh),
        in_specs=[tok, _full((1, D)), tok, pl.BlockSpec((tm, N), lambda i, k: (i, k)),
                  pl.BlockSpec((None, None, D, N), lambda i, k: (k, layer, 0, 0))],
        out_specs=(tok, _full((1, D)), tok),
        out_shape=(_sds((T, D), f32), _sds((1, D), f32), _sds((T, D), bf16)),
        scratch_shapes=[pltpu.VMEM((tm, D), f32)],
        compiler_params=_cparams(2),
    )(h, gain, dres, dz, w_in)


def _proj_in_bwd_wgrad(xn, dz, acc_buf, layer, n_layers, nsh, tm):
    T, D = xn.shape
    N = dz.shape[1] // nsh
    nt = T // tm

    def body(xn_ref, dz_ref, *rest):
        rest = rest[len(acc_buf):]
        dw_ref, acc_ref = rest
        i = pl.program_id(1)

        @pl.when(i == 0)
        def _():
            acc_ref[...] = jnp.zeros_like(acc_ref)

        acc_ref[...] += _dot_tn(xn_ref[...], dz_ref[...].astype(bf16))

        @pl.when(i == nt - 1)
        def _():
            dw_ref[...] = acc_ref[...].astype(bf16)

    n_acc = len(acc_buf)
    return pl.pallas_call(
        body, name="proj_in_bwd_wgrad", grid=(nsh, nt),
        in_specs=[pl.BlockSpec((tm, D), lambda k, i: (i, 0)), pl.BlockSpec((tm, N), lambda k, i: (i, k))]
        + [pl.BlockSpec(memory_space=pl.ANY)] * n_acc,
        out_specs=pl.BlockSpec((None, None, D, N), lambda k, i: (k, layer, 0, 0)),
        out_shape=_sds((nsh, n_layers, D, N), bf16),
        scratch_shapes=[pltpu.VMEM((D, N), f32)],
        input_output_aliases={2 + j: j for j in range(n_acc)},
        compiler_params=_cparams(2),
    )(xn, dz, *acc_buf)


def _proj_out_fwd(h, oa, ob, oc, w_out, layer, tm):
    T, D = h.shape
    nsh, R = w_out.shape[0], w_out.shape[2]
    da, db = oa.shape[1], ob.shape[1]
    nt = T // tm

    def body(h_ref, oa_ref, ob_ref, oc_ref, w_ref, out_ref):
        w = w_ref[...].reshape(nsh * R, D)
        out_ref[...] = (h_ref[...] + _dot(oa_ref[...], w[:da]) + _dot(ob_ref[...], w[da:da + db])
                        + _dot(oc_ref[...], w[da + db:]))

    def tok(n):
        return pl.BlockSpec((tm, n), lambda i: (i, 0))

    return pl.pallas_call(
        body, name="proj_out_fwd", grid=(nt,),
        in_specs=[tok(D), tok(da), tok(db), tok(oc.shape[1]),
                  pl.BlockSpec((nsh, None, R, D), lambda i: (0, layer, 0, 0))],
        out_specs=tok(D), out_shape=_sds((T, D), f32),
        compiler_params=_cparams(1),
    )(h, oa, ob, oc, w_out)


def _proj_out_bwd(dh, oa, ob, oc, w_out, acc_buf, layer, n_layers, tm):
    T, D = dh.shape
    nsh, R = w_out.shape[0], w_out.shape[2]
    da, db, dc = oa.shape[1], ob.shape[1], oc.shape[1]
    nt = T // tm

    def body(dh_ref, oa_ref, ob_ref, oc_ref, w_ref, *rest):
        rest = rest[len(acc_buf):]
        doa_ref, dob_ref, doc_ref, dw_ref, acc_ref = rest
        i = pl.program_id(0)

        @pl.when(i == 0)
        def _():
            acc_ref[...] = jnp.zeros_like(acc_ref)

        d = dh_ref[...].astype(bf16)
        w = w_ref[...].reshape(nsh * R, D)
        dm = _dot_nt(d, w)
        doa_ref[...] = dm[:, :da]
        dob_ref[...] = dm[:, da:da + db]
        doc_ref[...] = dm[:, da + db:]
        acc_ref[pl.ds(0, da), :] += _dot_tn(oa_ref[...], d)
        acc_ref[pl.ds(da, db), :] += _dot_tn(ob_ref[...], d)
        acc_ref[pl.ds(da + db, dc), :] += _dot_tn(oc_ref[...], d)

        @pl.when(i == nt - 1)
        def _():
            dw_ref[...] = acc_ref[...].astype(bf16).reshape(nsh, R, D)

    def tok(n):
        return pl.BlockSpec((tm, n), lambda i: (i, 0))

    n_acc = len(acc_buf)
    wspec = pl.BlockSpec((nsh, None, R, D), lambda i: (0, layer, 0, 0))
    return pl.pallas_call(
        body, name="proj_out_bwd", grid=(nt,),
        in_specs=[tok(D), tok(da), tok(db), tok(dc), wspec] + [pl.BlockSpec(memory_space=pl.ANY)] * n_acc,
        out_specs=(tok(da), tok(db), tok(dc), wspec),
        out_shape=(_sds((T, da), f32), _sds((T, db), f32), _sds((T, dc), f32), _sds((nsh, n_layers, R, D), bf16)),
        scratch_shapes=[pltpu.VMEM((nsh * R, D), f32)],
        input_output_aliases={5 + j: 3 + j for j in range(n_acc)},
        compiler_params=_cparams(1),
    )(dh, oa, ob, oc, w_out, *acc_buf)


def _head_sum(m, n_heads):
    parts = []
    for hd in range(n_heads):
        s = jnp.sum(m[:, hd * HEAD:(hd + 1) * HEAD], axis=-1, keepdims=True)
        parts.append(jnp.broadcast_to(s, (m.shape[0], HEAD)))
    return parts[0] if n_heads == 1 else jnp.concatenate(parts, axis=1)


def _hgrn_chunk(q, fl, iv, g, lb, gain, states, tri, n_heads):
    C = q.shape[0]
    qs = _silu(q)
    forget = lb + (1.0 - lb) * jax.nn.sigmoid(fl)
    kk = 1.0 - forget
    logf = jnp.log(forget)
    b = jnp.dot(tri, logf, precision=HI, preferred_element_type=f32)
    n_sub = C // A_SUB
    row = lax.broadcasted_iota(jnp.int32, (A_SUB, 1), 0)
    outs = []
    for blk in range(n_sub):
        lo = blk * A_SUB
        b_i, q_i, k_i, v_i = b[lo:lo + A_SUB], qs[lo:lo + A_SUB], kk[lo:lo + A_SUB], iv[lo:lo + A_SUB]
        acc = jnp.zeros_like(q_i)
        for s in range(A_SUB):
            e = jnp.exp(jnp.minimum(b_i - b_i[s:s + 1], 0.0))
            m = jnp.where(row >= s, q_i * e * k_i[s:s + 1], 0.0)
            acc = acc + _head_sum(m, n_heads) * v_i[s:s + 1]
        if blk > 0:
            piv = b_i[0:1]
            qt = (q_i * jnp.exp(b_i - piv)).astype(bf16)
            kt = (kk[:lo] * jnp.exp(piv - b[:lo])).astype(bf16)
            vb = iv[:lo].astype(bf16)
            parts = []
            for hd in range(n_heads):
                sl = slice(hd * HEAD, (hd + 1) * HEAD)
                sc = _dot_nt(qt[:, sl], kt[:, sl])
                parts.append(_dot(sc.astype(bf16), vb[:, sl]))
            acc = acc + (parts[0] if n_heads == 1 else jnp.concatenate(parts, axis=1))
        outs.append(acc)
    o = jnp.concatenate(outs, axis=0)
    qe = (qs * jnp.exp(b)).astype(bf16)
    b_end = b[C - 1:C]
    kd = (kk * jnp.exp(b_end - b)).astype(bf16)
    vb = iv.astype(bf16)
    dec = jnp.exp(b_end)
    inter, new_states = [], []
    for hd in range(n_heads):
        sl = slice(hd * HEAD, (hd + 1) * HEAD)
        st = states[hd]
        inter.append(_dot_nt(qe[:, sl], st.astype(bf16)))
        new_states.append(dec[:, sl] * st + _dot_tn(vb[:, sl], kd[:, sl]))
    o = o + (inter[0] if n_heads == 1 else jnp.concatenate(inter, axis=1))
    ms = _head_sum(o * o, n_heads) * (1.0 / HEAD)
    out = o * lax.rsqrt(ms + EPS) * gain * _silu(g)
    return out, tuple(new_states)


def _tri_matrix(n):
    return jnp.asarray(np.tril(np.ones((n, n), np.float32)))


def _hgrn_fwd(z, lb, gain, d_a):
    T = z.shape[0]
    C = A_CHUNK
    nc = T // C
    nh = d_a // HEAD
    tri = _tri_matrix(C)

    def body(q_ref, f_ref, i_ref, g_ref, lb_ref, gain_ref, tri_ref, out_ref, st_ref, carry_ref):
        @pl.when(pl.program_id(0) == 0)
        def _():
            carry_ref[...] = jnp.zeros_like(carry_ref)

        states = tuple(carry_ref[hd] for hd in range(nh))
        st_ref[...] = carry_ref[...]
        out, new_states = _hgrn_chunk(q_ref[...], f_ref[...], i_ref[...], g_ref[...], lb_ref[...], gain_ref[...],
                                      states, tri_ref[...], nh)
        out_ref[...] = out.astype(bf16)
        for hd in range(nh):
            carry_ref[hd] = new_states[hd]

    def col(j):
        return pl.BlockSpec((C, d_a), lambda c, j=j: (c, j))

    return pl.pallas_call(
        body, name="hgrn_fwd", grid=(nc,),
        in_specs=[col(0), col(1), col(2), col(3), _full((1, d_a)), _full((1, d_a)), _full((C, C))],
        out_specs=(pl.BlockSpec((C, d_a), lambda c: (c, 0)), pl.BlockSpec((None, nh, HEAD, HEAD), lambda c: (c, 0, 0, 0))),
        out_shape=(_sds((T, d_a), bf16), _sds((nc, nh, HEAD, HEAD), f32)),
        scratch_shapes=[pltpu.VMEM((nh, HEAD, HEAD), f32)],
        compiler_params=_cparams(1),
    )(z, z, z, z, lb, gain, tri)


def _hgrn_bwd(z, lb, gain, states, dout, d_a):
    T = z.shape[0]
    C = A_CHUNK
    nc = T // C
    nh = d_a // HEAD
    tri = _tri_matrix(C)

    def body(q_ref, f_ref, i_ref, g_ref, lb_ref, gain_ref, tri_ref, st_ref, do_ref,
             dz_ref, dlb_ref, dgain_ref, carry_ref):
        @pl.when(pl.program_id(0) == 0)
        def _():
            carry_ref[...] = jnp.zeros_like(carry_ref)
            dlb_ref[...] = jnp.zeros_like(dlb_ref)
            dgain_ref[...] = jnp.zeros_like(dgain_ref)

        tri_v = tri_ref[...]

        def fn(q, fl, iv, g, lbv, gv, sts):
            return _hgrn_chunk(q, fl, iv, g, lbv, gv, sts, tri_v, nh)

        states_in = tuple(st_ref[hd] for hd in range(nh))
        _, vjp = jax.vjp(fn, q_ref[...], f_ref[...], i_ref[...], g_ref[...], lb_ref[...], gain_ref[...], states_in)
        dstates = tuple(carry_ref[hd] for hd in range(nh))
        dq, df, di, dg, dlb, dgain, dst = vjp((do_ref[...], dstates))
        dz_ref[:, 0:d_a] = dq
        dz_ref[:, d_a:2 * d_a] = df
        dz_ref[:, 2 * d_a:3 * d_a] = di
        dz_ref[:, 3 * d_a:4 * d_a] = dg
        dlb_ref[...] += dlb
        dgain_ref[...] += dgain
        for hd in range(nh):
            carry_ref[hd] = dst[hd]

    def col(j):
        return pl.BlockSpec((C, d_a), lambda c, j=j: (nc - 1 - c, j))

    return pl.pallas_call(
        body, name="hgrn_bwd", grid=(nc,),
        in_specs=[col(0), col(1), col(2), col(3), _full((1, d_a)), _full((1, d_a)), _full((C, C)),
                  pl.BlockSpec((None, nh, HEAD, HEAD), lambda c: (nc - 1 - c, 0, 0, 0)),
                  pl.BlockSpec((C, d_a), lambda c: (nc - 1 - c, 0))],
        out_specs=(pl.BlockSpec((C, 4 * d_a), lambda c: (nc - 1 - c, 0)), _full((1, d_a)), _full((1, d_a))),
        out_shape=(_sds((T, 4 * d_a), f32), _sds((1, d_a), f32), _sds((1, d_a), f32)),
        scratch_shapes=[pltpu.VMEM((nh, HEAD, HEAD), f32)],
        compiler_params=_cparams(1),
    )(z, z, z, z, lb, gain, tri, states, dout)


def _one_minus_exp(x):
    series = -x * (1.0 + x * (0.5 + x * (1.0 / 6.0 + x * (1.0 / 24.0))))
    return jnp.where(x > -0.03, series, 1.0 - jnp.exp(x))


def _lru_pre(xc, wa, ba, wx, bx, lam):
    xb16 = xc.astype(bf16)
    r = jax.nn.sigmoid(_dot(xb16, wa.astype(bf16)) + ba)
    gi = jax.nn.sigmoid(_dot(xb16, wx.astype(bf16)) + bx)
    log_a = -LRU_C * r * jax.nn.softplus(-lam)
    a = jnp.exp(log_a)
    mult = jnp.sqrt(_one_minus_exp(2.0 * log_a))
    return a, mult * gi * xc


def _lru_post(h, gate, gain, avg):
    y = h * jax.nn.gelu(gate)
    ms = jnp.dot(y * y, avg, precision=HI, preferred_element_type=f32)
    return y * lax.rsqrt(ms + EPS) * gain


def _shift_down(x, d, prev):
    row = lax.broadcasted_iota(jnp.int32, x.shape, 0)
    return jnp.where(row >= d, pltpu.roll(x, d, 0), pltpu.roll(prev, d, 0))


def _shift_up(x, d, nxt):
    n = x.shape[0]
    row = lax.broadcasted_iota(jnp.int32, x.shape, 0)
    return jnp.where(row < n - d, pltpu.roll(x, n - d, 0), pltpu.roll(nxt, n - d, 0))


def _scan_rows(a, u, reverse):
    n = a.shape[0]
    row = lax.broadcasted_iota(jnp.int32, a.shape, 0)
    d = 1
    while d < n:
        shift, ok = (n - d, row < n - d) if reverse else (d, row >= d)
        su = jnp.where(ok, pltpu.roll(u, shift, 0), 0.0)
        sa = jnp.where(ok, pltpu.roll(a, shift, 0), 1.0)
        u = u + a * su
        a = a * sa
        d *= 2
    return a, u


def _conv(xb, xprev, cw, cb):
    xc = cb + cw[CONV_WIDTH - 1:CONV_WIDTH] * xb
    for d in range(1, CONV_WIDTH):
        xc = xc + cw[CONV_WIDTH - 1 - d:CONV_WIDTH - d] * _shift_down(xb, d, xprev)
    return xc


def _lru_fwd(z, col0, d_b, cw, cb, wa, ba, wx, bx, lam, gain, avg):
    T = z.shape[0]
    R = min(B_CHUNK, T)
    nr = T // R
    jb = col0 // d_b

    def body(xb_ref, gate_ref, cw_ref, cb_ref, wa_ref, ba_ref, wx_ref, bx_ref, lam_ref, gain_ref, avg_ref,
             out_ref, h_ref, xprev_ref, hprev_ref):
        @pl.when(pl.program_id(0) == 0)
        def _():
            xprev_ref[...] = jnp.zeros_like(xprev_ref)
            hprev_ref[...] = jnp.zeros_like(hprev_ref)

        xb = xb_ref[...]
        xc = _conv(xb, xprev_ref[...], cw_ref[...], cb_ref[...])
        a, u = _lru_pre(xc, wa_ref[...], ba_ref[...], wx_ref[...], bx_ref[...], lam_ref[...])
        acum, hl = _scan_rows(a, u, False)
        h = hl + acum * hprev_ref[R - 1:R, :]
        h_ref[...] = h
        out_ref[...] = _lru_post(h, gate_ref[...], gain_ref[...], avg_ref[...]).astype(bf16)
        xprev_ref[...] = xb
        hprev_ref[...] = h

    vec = _full((1, d_b))
    return pl.pallas_call(
        body, name="lru_fwd", grid=(nr,),
        in_specs=[pl.BlockSpec((R, d_b), lambda i: (i, jb)), pl.BlockSpec((R, d_b), lambda i: (i, jb + 1)),
                  _full((CONV_WIDTH, d_b)), vec, _full((d_b, d_b)), vec, _full((d_b, d_b)), vec, vec, vec, _full((d_b, d_b))],
        out_specs=(pl.BlockSpec((R, d_b), lambda i: (i, 0)), pl.BlockSpec((R, d_b), lambda i: (i, 0))),
        out_shape=(_sds((T, d_b), bf16), _sds((T, d_b), f32)),
        scratch_shapes=[pltpu.VMEM((R, d_b), f32), pltpu.VMEM((R, d_b), f32)],
        compiler_params=_cparams(1),
    )(z, z, cw, cb, wa, ba, wx, bx, lam, gain, avg)


def _lru_bwd(z, col0, d_b, hsave, dout, cw, cb, wa, ba, wx, bx, lam, gain, avg):
    T = z.shape[0]
    R = min(B_CHUNK, T)
    nr = T // R
    jb = col0 // d_b

    def body(xb_ref, xp_ref, gate_ref, h_ref, hp_ref, do_ref,
             cw_ref, cb_ref, wa_ref, ba_ref, wx_ref, bx_ref, lam_ref, gain_ref, avg_ref,
             dz_ref, dcw_ref, dcb_ref, dwa_ref, dba_ref, dwx_ref, dbx_ref, dlam_ref, dgain_ref,
             gfirst_ref, afirst_ref, dxcn_ref):
        step = pl.program_id(0)
        first_in_time = step == nr - 1

        @pl.when(step == 0)
        def _():
            for r in (dcw_ref, dcb_ref, dwa_ref, dba_ref, dwx_ref, dbx_ref, dlam_ref, dgain_ref,
                      gfirst_ref, afirst_ref, dxcn_ref):
                r[...] = jnp.zeros_like(r)

        xb = xb_ref[...]
        keep = jnp.where(first_in_time, 0.0, 1.0)
        xprev = xp_ref[...] * keep
        hprev = hp_ref[...] * keep
        cw = cw_ref[...]
        xc = _conv(xb, xprev, cw, cb_ref[...])
        (a, _), vjp_pre = jax.vjp(_lru_pre, xc, wa_ref[...], ba_ref[...], wx_ref[...], bx_ref[...], lam_ref[...])
        h = h_ref[...]
        avg = avg_ref[...]
        _, vjp_post = jax.vjp(lambda hh, gg, gn: _lru_post(hh, gg, gn, avg), h, gate_ref[...], gain_ref[...])
        dh, dgate, dgain = vjp_post(do_ref[...])
        a_next = _shift_up(a, 1, jnp.broadcast_to(afirst_ref[0:1, :], a.shape))
        acum, gl = _scan_rows(a_next, dh, True)
        gtot = gl + acum * gfirst_ref[0:1, :]
        da = gtot * _shift_down(h, 1, hprev)
        dxc, dwa, dba, dwx, dbx, dlam = vjp_pre((da, gtot))
        dxcn = dxcn_ref[...]
        dxb = cw[CONV_WIDTH - 1:CONV_WIDTH] * dxc
        dcw_ref[CONV_WIDTH - 1:CONV_WIDTH, :] += jnp.sum(dxc * xb, axis=0, keepdims=True)
        for d in range(1, CONV_WIDTH):
            tap = CONV_WIDTH - 1 - d
            dxb = dxb + cw[tap:tap + 1] * _shift_up(dxc, d, dxcn)
            dcw_ref[tap:tap + 1, :] += jnp.sum(dxc * _shift_down(xb, d, xprev), axis=0, keepdims=True)
        dz_ref[:, 0:d_b] = dxb
        dz_ref[:, d_b:2 * d_b] = dgate
        dcb_ref[...] += jnp.sum(dxc, axis=0, keepdims=True)
        dwa_ref[...] += dwa
        dba_ref[...] += dba
        dwx_ref[...] += dwx
        dbx_ref[...] += dbx
        dlam_ref[...] += dlam
        dgain_ref[...] += dgain
        gfirst_ref[...] = jnp.broadcast_to(gtot[0:1, :], gfirst_ref.shape)
        afirst_ref[...] = jnp.broadcast_to(a[0:1, :], afirst_ref.shape)
        dxcn_ref[...] = dxc

    vec = _full((1, d_b))
    mat = _full((d_b, d_b))

    def cur(j):
        return pl.BlockSpec((R, d_b), lambda i, j=j: (nr - 1 - i, j))

    def prev(j):
        return pl.BlockSpec((R, d_b), lambda i, j=j: (jnp.maximum(nr - 2 - i, 0), j))

    return pl.pallas_call(
        body, name="lru_bwd", grid=(nr,),
        in_specs=[cur(jb), prev(jb), cur(jb + 1), cur(0), prev(0), cur(0),
                  _full((CONV_WIDTH, d_b)), vec, mat, vec, mat, vec, vec, vec, mat],
        out_specs=(pl.BlockSpec((R, 2 * d_b), lambda i: (nr - 1 - i, 0)), _full((CONV_WIDTH, d_b)), vec, mat, vec, mat, vec, vec, vec),
        out_shape=(_sds((T, 2 * d_b), f32), _sds((CONV_WIDTH, d_b), f32), _sds((1, d_b), f32), _sds((d_b, d_b), f32),
                   _sds((1, d_b), f32), _sds((d_b, d_b), f32), _sds((1, d_b), f32), _sds((1, d_b), f32), _sds((1, d_b), f32)),
        scratch_shapes=[pltpu.VMEM((8, d_b), f32), pltpu.VMEM((8, d_b), f32), pltpu.VMEM((R, d_b), f32)],
        compiler_params=_cparams(1),
    )(z, z, z, hsave, hsave, dout, cw, cb, wa, ba, wx, bx, lam, gain, avg)


def _sgu_chunk(u_in, v_in, w, bexp, gain, avg, n_groups):
    C, d_c = u_in.shape
    gd = d_c // n_groups
    u = jax.nn.gelu(u_in)
    v = jax.nn.gelu(v_in)
    mu = jnp.dot(v, avg, precision=HI, preferred_element_type=f32)
    vc = v - mu
    var = jnp.dot(vc * vc, avg, precision=HI, preferred_element_type=f32)
    vh = (vc * lax.rsqrt(var + EPS)).astype(bf16)
    lane = lax.broadcasted_iota(jnp.int32, (1, d_c), 1)
    causal = lax.broadcasted_iota(jnp.int32, (C, C), 0) >= lax.broadcasted_iota(jnp.int32, (C, C), 1)
    zz = bexp
    for g in range(n_groups):
        wg = jnp.where(causal, w[g], 0.0).astype(bf16)
        zz = zz + jnp.where((lane >= g * gd) & (lane < (g + 1) * gd), _dot(wg, vh), 0.0)
    y = u * zz
    ms = jnp.dot(y * y, avg, precision=HI, preferred_element_type=f32)
    return y * lax.rsqrt(ms + EPS) * gain


def _sgu_fwd(z, col0, d_c, w, bexp, gain, avg):
    T = z.shape[0]
    C = C_CHUNK
    jb = col0 // d_c
    G = w.shape[0]

    def body(u_ref, v_ref, w_ref, b_ref, gain_ref, avg_ref, out_ref):
        out_ref[...] = _sgu_chunk(u_ref[...], v_ref[...], w_ref[...], b_ref[...], gain_ref[...], avg_ref[...], G).astype(bf16)

    return pl.pallas_call(
        body, name="sgu_fwd", grid=(T // C,),
        in_specs=[pl.BlockSpec((C, d_c), lambda i: (i, jb)), pl.BlockSpec((C, d_c), lambda i: (i, jb + 1)),
                  _full((G, C, C)), _full((C, d_c)), _full((1, d_c)), _full((d_c, d_c))],
        out_specs=pl.BlockSpec((C, d_c), lambda i: (i, 0)),
        out_shape=_sds((T, d_c), bf16),
        compiler_params=_cparams(1),
    )(z, z, w, bexp, gain, avg)


def _sgu_bwd(z, col0, d_c, dout, w, bexp, gain, avg):
    T = z.shape[0]
    C = C_CHUNK
    nc = T // C
    jb = col0 // d_c
    G = w.shape[0]
    gd = d_c // G

    def body(u_ref, v_ref, do_ref, w_ref, b_ref, gain_ref, avg_ref, dz_ref, dw_ref, db_ref, dgain_ref, dbexp_ref):
        step = pl.program_id(0)

        @pl.when(step == 0)
        def _():
            dw_ref[...] = jnp.zeros_like(dw_ref)
            dgain_ref[...] = jnp.zeros_like(dgain_ref)
            dbexp_ref[...] = jnp.zeros_like(dbexp_ref)

        avg = avg_ref[...]
        _, vjp = jax.vjp(lambda a, b, c, d, e: _sgu_chunk(a, b, c, d, e, avg, G),
                         u_ref[...], v_ref[...], w_ref[...], b_ref[...], gain_ref[...])
        du, dv, dw, dbexp, dgain = vjp(do_ref[...])
        dz_ref[:, 0:d_c] = du
        dz_ref[:, d_c:2 * d_c] = dv
        dw_ref[...] += dw
        dbexp_ref[...] += dbexp
        dgain_ref[...] += dgain

        @pl.when(step == nc - 1)
        def _():
            lane = lax.broadcasted_iota(jnp.int32, (1, d_c), 1)
            acc = dbexp_ref[...]
            for g in range(G):
                sel = jnp.where((lane >= g * gd) & (lane < (g + 1) * gd), acc, 0.0)
                db_ref[:, g:g + 1] = jnp.sum(sel, axis=1, keepdims=True)

    return pl.pallas_call(
        body, name="sgu_bwd", grid=(nc,),
        in_specs=[pl.BlockSpec((C, d_c), lambda i: (i, jb)), pl.BlockSpec((C, d_c), lambda i: (i, jb + 1)),
                  pl.BlockSpec((C, d_c), lambda i: (i, 0)),
                  _full((G, C, C)), _full((C, d_c)), _full((1, d_c)), _full((d_c, d_c))],
        out_specs=(pl.BlockSpec((C, 2 * d_c), lambda i: (i, 0)), _full((G, C, C)), _full((C, G)), _full((1, d_c))),
        out_shape=(_sds((T, 2 * d_c), f32), _sds((G, C, C), f32), _sds((C, G), f32), _sds((1, d_c), f32)),
        scratch_shapes=[pltpu.VMEM((C, d_c), f32)],
        compiler_params=_cparams(1),
    )(z, z, dout, w, bexp, gain, avg)


def _loss_head(h, gain, target, tm):
    T, D = h.shape
    nt = T // tm

    def body(h_ref, gain_ref, tgt_ref, dh_ref, loss_ref, dgain_ref):
        @pl.when(pl.program_id(0) == 0)
        def _():
            loss_ref[...] = jnp.zeros_like(loss_ref)
            dgain_ref[...] = jnp.zeros_like(dgain_ref)

        hv = h_ref[...]
        gain_v = gain_ref[...]
        r = lax.rsqrt(jnp.mean(hv * hv, axis=-1, keepdims=True) + EPS)
        xh = hv * r
        e = xh * gain_v - tgt_ref[...]
        loss_ref[...] += 0.5 * jnp.sum(jnp.mean(e * e, axis=-1, keepdims=True), axis=0, keepdims=True)
        dy = e * (1.0 / D)
        dgain_ref[...] += jnp.sum(dy * xh, axis=0, keepdims=True)
        dxh = dy * gain_v
        dh_ref[...] = r * (dxh - xh * jnp.mean(dxh * xh, axis=-1, keepdims=True))

    tok = pl.BlockSpec((tm, D), lambda i: (i, 0))
    return pl.pallas_call(
        body, name="loss_head", grid=(nt,),
        in_specs=[tok, _full((1, D)), tok],
        out_specs=(tok, _full((1, 128)), _full((1, D))),
        out_shape=(_sds((T, D), f32), _sds((1, 128), f32), _sds((1, D), f32)),
        compiler_params=_cparams(1),
    )(h, gain, target)


def _lower_bounds_fn(logits):
    n = logits.shape[0]
    mx = jnp.max(logits, axis=0, keepdims=True)
    ex = jnp.exp(logits - mx)
    soft = ex / jnp.sum(ex, axis=0, keepdims=True)
    rows = [jnp.zeros_like(soft[0:1])]
    for l in range(1, n):
        rows.append(rows[-1] + soft[l:l + 1])
    return jnp.concatenate(rows, axis=0)


def _lower_bounds(logits):
    def body(x_ref, o_ref):
        o_ref[...] = _lower_bounds_fn(x_ref[...])

    return pl.pallas_call(body, name="lower_bounds", out_shape=_sds(logits.shape, f32))(logits)


def _lower_bounds_bwd(logits, dlb):
    def body(x_ref, d_ref, o_ref):
        _, vjp = jax.vjp(_lower_bounds_fn, x_ref[...])
        o_ref[...] = vjp(d_ref[...])[0]

    return pl.pallas_call(body, name="lower_bounds_bwd", out_shape=_sds(logits.shape, f32))(logits, dlb)


def _adamw(w, g, m, v, rows_blk):
    R, Cc = w.shape
    rb = R if R <= rows_blk else math.gcd(R, rows_blk)

    def body(w_ref, g_ref, m_ref, v_ref, d_ref, nm_ref, nv_ref):
        gv = g_ref[...]
        m2 = ADAM_B1 * m_ref[...] + (1.0 - ADAM_B1) * gv
        v2 = ADAM_B2 * v_ref[...] + (1.0 - ADAM_B2) * (gv * gv)
        m_hat = m2 / (1.0 - ADAM_B1 ** ADAM_STEP)
        v_hat = v2 / (1.0 - ADAM_B2 ** ADAM_STEP)
        d_ref[...] = -ADAM_LR * (m_hat / (jnp.sqrt(v_hat) + ADAM_EPS) + ADAM_WD * w_ref[...])
        nm_ref[...] = m2
        nv_ref[...] = v2

    spec = pl.BlockSpec((rb, Cc), lambda i: (i, 0))
    return pl.pallas_call(
        body, name="adamw", grid=(R // rb,),
        in_specs=[spec] * 4, out_specs=(spec,) * 3, out_shape=(_sds((R, Cc), f32),) * 3,
        compiler_params=_cparams(1),
    )(w, g, m, v)


def _pair_sum(grads, recv, small, small_recv, c_arr):
    outs, n = [], len(grads)
    res = []
    for a in range(n):
        g, r = grads[a], recv[a]
        nsh, L, R, Cc = g.shape
        half = L // 2
        rb = R

        def body(c_ref, g_ref, r_ref, o_ref):
            o_ref[...] = (g_ref[...].astype(f32) + r_ref[...].astype(f32)).astype(bf16)

        gs = pltpu.PrefetchScalarGridSpec(
            num_scalar_prefetch=1, grid=(nsh, half, R // rb),
            in_specs=[pl.BlockSpec((None, None, rb, Cc), lambda s, l, i, c: (s, c[0] * half + l, i, 0)),
                      pl.BlockSpec((None, None, rb, Cc), lambda s, l, i, c: (s, l, i, 0))],
            out_specs=pl.BlockSpec((None, None, rb, Cc), lambda s, l, i, c: (s, l, i, 0)))
        res.append(pl.pallas_call(body, name="pair_sum", grid_spec=gs, out_shape=_sds((nsh, half, R, Cc), bf16),
                                  compiler_params=_cparams(3))(c_arr, g, r))

    def sbody(a_ref, b_ref, o_ref):
        o_ref[...] = a_ref[...] + b_ref[...]

    s = pl.pallas_call(sbody, name="pair_sum_small", out_shape=_sds(small.shape, f32))(small, small_recv)
    return res, s


def _chip_sum(hsum, recv, slot_arr, c_arr, n_layers):
    res = []
    for a in range(len(hsum)):
        hh, r = hsum[a], recv[a]
        nsh, half, R, Cc = hh.shape
        rb = R

        def body(s_ref, c_ref, h_ref, r_ref, o_ref):
            acc = h_ref[...].astype(f32)
            for j in range(N_CHIPS - 1):
                acc = acc + r_ref[j].astype(f32)
            o_ref[...] = acc

        gs = pltpu.PrefetchScalarGridSpec(
            num_scalar_prefetch=2, grid=(half, R // rb),
            in_specs=[pl.BlockSpec((None, None, rb, Cc), lambda l, i, s, c: (s[0], l, i, 0)),
                      pl.BlockSpec((N_CHIPS - 1, None, rb, Cc), lambda l, i, s, c: (0, l, i, 0))],
            out_specs=pl.BlockSpec((None, rb, Cc), lambda l, i, s, c: (c[0] * half + l, i, 0)))
        res.append(pl.pallas_call(body, name="chip_sum", grid_spec=gs, out_shape=_sds((n_layers, R, Cc), f32),
                                  compiler_params=_cparams(2))(slot_arr, c_arr, hh, r))
    return res


def _sum_slots(x):
    def body(x_ref, o_ref):
        acc = x_ref[0]
        for j in range(1, x.shape[0]):
            acc = acc + x_ref[j]
        o_ref[...] = acc

    return pl.pallas_call(body, name="sum_slots", out_shape=_sds(x.shape[1:], f32))(x)


def _blockdiag(w):
    nb, bd, _ = w.shape
    eye = jnp.eye(nb, dtype=w.dtype)
    return (eye[:, None, :, None] * w[:, :, None, :]).reshape(nb * bd, nb * bd)


def _blockdiag_extract(dense, nb):
    bd = dense.shape[0] // nb
    d4 = dense.reshape(nb, bd, nb, bd)
    return jnp.stack([d4[i, :, i, :] for i in range(nb)])


def _pack(arrays, multiple):
    flat = jnp.concatenate([a.reshape(-1).astype(f32) for a in arrays])
    pad = (-flat.shape[0]) % multiple
    return jnp.pad(flat, (0, pad))


def _unpack(flat, shapes):
    out, off = [], 0
    for s in shapes:
        n = int(np.prod(s))
        out.append(flat[off:off + n].reshape(s))
        off += n
    return out


BIG = ("ffn1_wg", "ffn1_wu", "ffn1_wd", "w_in", "w_out", "ffn2_wg", "ffn2_wu", "ffn2_wd")
SMALL = ("ffn1_norm", "mix_norm", "hgrn_lb_logits", "hgrn_norm", "conv_w", "conv_b", "lru_wa", "lru_ba", "lru_wx",
         "lru_bx", "lru_lambda", "lru_norm", "sgu_w", "sgu_b", "sgu_norm", "ffn2_norm", "final_norm")
WEIGHTS = ("ffn1_norm", "ffn1_wg", "ffn1_wu", "ffn1_wd", "mix_norm", "w_in", "hgrn_lb_logits", "hgrn_norm", "conv_w",
           "conv_b", "lru_wa", "lru_ba", "lru_wx", "lru_bx", "lru_lambda", "lru_norm", "sgu_w", "sgu_b", "sgu_norm",
           "w_out", "ffn2_norm", "ffn2_wg", "ffn2_wu", "ffn2_wd", "final_norm")


def kernel(x, ffn1_norm, ffn1_wg, ffn1_wu, ffn1_wd, mix_norm, w_in, hgrn_lb_logits, hgrn_norm, conv_w, conv_b, lru_wa, lru_ba, lru_wx, lru_bx, lru_lambda, lru_norm, sgu_w, sgu_b, sgu_norm, w_out, ffn2_norm, ffn2_wg, ffn2_wu, ffn2_wd, final_norm, loss_target, m_ffn1_norm, m_ffn1_wg, m_ffn1_wu, m_ffn1_wd, m_mix_norm, m_w_in, m_hgrn_lb_logits, m_hgrn_norm, m_conv_w, m_conv_b, m_lru_wa, m_lru_ba, m_lru_wx, m_lru_bx, m_lru_lambda, m_lru_norm, m_sgu_w, m_sgu_b, m_sgu_norm, m_w_out, m_ffn2_norm, m_ffn2_wg, m_ffn2_wu, m_ffn2_wd, m_final_norm, v_ffn1_norm, v_ffn1_wg, v_ffn1_wu, v_ffn1_wd, v_mix_norm, v_w_in, v_hgrn_lb_logits, v_hgrn_norm, v_conv_w, v_conv_b, v_lru_wa, v_lru_ba, v_lru_wx, v_lru_bx, v_lru_lambda, v_lru_norm, v_sgu_w, v_sgu_b, v_sgu_norm, v_w_out, v_ffn2_norm, v_ffn2_wg, v_ffn2_wu, v_ffn2_wd, v_final_norm):
    args = dict(locals())
    W = {n: args[n] for n in WEIGHTS}
    M = {n: args["m_" + n] for n in WEIGHTS}
    V = {n: args["v_" + n] for n in WEIGHTS}

    T, D = x.shape[1], x.shape[2]
    L = ffn1_norm.shape[0]
    d_a, d_b, d_c = hgrn_norm.shape[1], lru_norm.shape[1], sgu_norm.shape[1]
    col_b, col_c = 4 * d_a, 4 * d_a + 2 * d_b
    tm = 512 if T % 512 == 0 else T
    my_c = lax.axis_index("c")
    my_slot = 2 * lax.axis_index("x") + lax.axis_index("y")
    c_arr = jnp.reshape(my_c, (1,)).astype(jnp.int32)
    slot_arr = jnp.reshape(my_slot, (1,)).astype(jnp.int32)

    gathered = _allgather_shards([W[n].astype(bf16) for n in BIG] + [conv_w])
    G = dict(zip(BIG, gathered[:-1]))
    conv_full = jnp.transpose(gathered[-1], (1, 2, 0, 3)).reshape(L, CONV_WIDTH, d_b)

    lb = _lower_bounds(hgrn_lb_logits)
    avg_b = _group_avg_matrix(d_b, d_b // B_BLOCKS)
    avg_c = _group_avg_matrix(d_c, d_c // C_GROUPS)
    wa_dense = [_blockdiag(lru_wa[l]) for l in range(L)]
    wx_dense = [_blockdiag(lru_wx[l]) for l in range(L)]
    bexp = [jnp.repeat(sgu_b[l].T, d_c // C_GROUPS, axis=1) for l in range(L)]

    def lru_params(l):
        return (conv_full[l], conv_b[l][None], wa_dense[l], lru_ba[l].reshape(1, d_b), wx_dense[l],
                lru_bx[l].reshape(1, d_b), lru_lambda[l][None], lru_norm[l][None], avg_b)

    h = x.reshape(T, D)
    saved = []
    for l in range(L):
        s = {"h0": h}
        h, s["g1"], s["u1"] = _ffn_fwd(h, ffn1_norm[l][None], G["ffn1_wg"], G["ffn1_wu"], G["ffn1_wd"], l, tm)
        s["h1"] = h
        z = _proj_in_fwd(h, mix_norm[l][None], G["w_in"], l, tm)
        s["z"] = z
        s["oa"], s["states"] = _hgrn_fwd(z, lb[l][None], hgrn_norm[l][None], d_a)
        s["ob"], s["hl"] = _lru_fwd(z, col_b, d_b, *lru_params(l))
        s["oc"] = _sgu_fwd(z, col_c, d_c, sgu_w[l], bexp[l], sgu_norm[l][None], avg_c)
        h = _proj_out_fwd(h, s["oa"], s["ob"], s["oc"], G["w_out"], l, tm)
        s["h2"] = h
        h, s["g2"], s["u2"] = _ffn_fwd(h, ffn2_norm[l][None], G["ffn2_wg"], G["ffn2_wu"], G["ffn2_wd"], l, tm)
        saved.append(s)

    dh, loss_part, d_final = _loss_head(h, final_norm[None], loss_target.reshape(T, D), tm)
    loss = lax.psum(loss_part[0, 0], ("x", "y", "c"))

    bufs = {n: [] for n in ("ffn1", "ffn2", "w_in", "w_out")}
    small = {n: [None] * L for n in SMALL if n != "final_norm"}
    for l in reversed(range(L)):
        s = saved[l]
        dh, small["ffn2_norm"][l], dg, du, xn, dob = _ffn_bwd_dgrad(
            s["h2"], ffn2_norm[l][None], dh, s["g2"], s["u2"], G["ffn2_wg"], G["ffn2_wu"], G["ffn2_wd"], l, tm)
        bufs["ffn2"] = list(_ffn_bwd_wgrad(xn, dob, s["g2"], s["u2"], dg, du, bufs["ffn2"], l, L, tm))
        doa, dob_, doc, dwo = _proj_out_bwd(dh, s["oa"], s["ob"], s["oc"], G["w_out"], bufs["w_out"], l, L, tm)
        bufs["w_out"] = [dwo]
        dz_a, small["hgrn_lb_logits"][l], small["hgrn_norm"][l] = _hgrn_bwd(
            s["z"], lb[l][None], hgrn_norm[l][None], s["states"], doa, d_a)
        (dz_b, small["conv_w"][l], small["conv_b"][l], dwa, small["lru_ba"][l], dwx, small["lru_bx"][l],
         small["lru_lambda"][l], small["lru_norm"][l]) = _lru_bwd(s["z"], col_b, d_b, s["hl"], dob_, *lru_params(l))
        small["lru_wa"][l] = _blockdiag_extract(dwa, B_BLOCKS)
        small["lru_wx"][l] = _blockdiag_extract(dwx, B_BLOCKS)
        dz_c, small["sgu_w"][l], dsb, small["sgu_norm"][l] = _sgu_bwd(
            s["z"], col_c, d_c, doc, sgu_w[l], bexp[l], sgu_norm[l][None], avg_c)
        small["sgu_b"][l] = dsb.T
        dz = jnp.concatenate([dz_a, dz_b, dz_c], axis=1)
        dh, small["mix_norm"][l], xn = _proj_in_bwd_dgrad(s["h1"], mix_norm[l][None], dh, dz, G["w_in"], l, tm)
        bufs["w_in"] = [_proj_in_bwd_wgrad(xn, dz, bufs["w_in"], l, L, N_CHIPS, tm)]
        dh, small["ffn1_norm"][l], dg, du, xn, dob = _ffn_bwd_dgrad(
            s["h0"], ffn1_norm[l][None], dh, s["g1"], s["u1"], G["ffn1_wg"], G["ffn1_wu"], G["ffn1_wd"], l, tm)
        bufs["ffn1"] = list(_ffn_bwd_wgrad(xn, dob, s["g1"], s["u1"], dg, du, bufs["ffn1"], l, L, tm))
    grad_x = dh.reshape(x.shape)

    big_grads = [bufs["ffn1"][0], bufs["ffn1"][1], bufs["ffn1"][2], bufs["w_in"][0], bufs["w_out"][0],
                 bufs["ffn2"][0], bufs["ffn2"][1], bufs["ffn2"][2]]
    small_names = [n for n in SMALL]
    small_parts = [jnp.stack([jnp.reshape(v, (-1,)) for v in small[n]]) if n != "final_norm" else d_final for n in small_names]
    small_shapes = [p.shape for p in small_parts]
    packed = _pack(small_parts, 2 * 8 * 128).reshape(2, -1, 128)
    nb, half = len(big_grads), L // 2
    n_rows = packed.shape[1]

    remote = [(lambda i, o, p, a=a: i[a].at[:, pl.ds((1 - p.c) * half, half)], lambda i, o, p, a=a: o[a], "sib")
              for a in range(nb)]
    remote.append((lambda i, o, p: i[nb], lambda i, o, p: o[nb], "sib"))
    outs = [_sds((N_CHIPS, half) + g.shape[2:], bf16) for g in big_grads] + [_sds(packed.shape, f32)]
    recv = _exchange("grad_pair_d2d", big_grads + [packed], outs, remote)
    hsum, small_pair = _pair_sum(big_grads, recv[:nb], packed, recv[nb], c_arr)

    remote, local = [], []
    for a in range(nb):
        for j, kind in enumerate(CHIP_KINDS):
            remote.append((lambda i, o, p, a=a, kind=kind: i[a].at[p.peer_slot(kind)],
                           lambda i, o, p, a=a, j=j: o[a].at[j], kind))
    for kind in CHIP_KINDS:
        remote.append((lambda i, o, p: i[nb].at[p.c], lambda i, o, p: o[nb].at[p.slot], kind))
    local.append((lambda i, o, p: i[nb].at[p.c], lambda i, o, p: o[nb].at[p.slot]))
    outs = [_sds((N_CHIPS - 1, half) + g.shape[2:], bf16) for g in big_grads] + [_sds((N_CHIPS, n_rows, 128), f32)]
    recv = _exchange("grad_chip_ici", hsum + [small_pair], outs, remote, local)
    ssum = _chip_sum(hsum, recv[:nb], slot_arr, c_arr, L)
    small_half = _sum_slots(recv[nb])

    remote = [(lambda i, o, p, a=a: o[a].at[pl.ds(p.c * half, half)], lambda i, o, p, a=a: o[a].at[pl.ds(p.c * half, half)], "sib")
              for a in range(nb)]
    remote.append((lambda i, o, p: i[nb], lambda i, o, p: o[nb].at[p.c], "sib"))
    local = [(lambda i, o, p: i[nb], lambda i, o, p: o[nb].at[p.c])]
    outs = [_sds(sa.shape, f32) for sa in ssum] + [_sds(packed.shape, f32)]
    final = _exchange("grad_share_d2d", ssum + [small_half], outs, remote, local, aliases={a: a for a in range(nb)})
    grads = {n: final[a].reshape(W[n].shape) for a, n in enumerate(BIG)}
    small_tot = _unpack(final[nb].reshape(-1), small_shapes)
    for n, val in zip(small_names, small_tot):
        grads[n] = val
    grads["hgrn_lb_logits"] = _lower_bounds_bwd(hgrn_lb_logits, grads["hgrn_lb_logits"])
    shard_cols = conv_w.shape[2]
    grads["conv_w"] = lax.dynamic_slice_in_dim(grads["conv_w"].reshape(L, CONV_WIDTH, d_b), my_slot * shard_cols, shard_cols, axis=2)
    for n in SMALL:
        grads[n] = grads[n].reshape(W[n].shape)

    delta, new_m, new_v = {}, {}, {}
    for n in BIG:
        cols = W[n].shape[-1]
        d2, m2, v2 = _adamw(W[n].reshape(-1, cols), grads[n].reshape(-1, cols), M[n].reshape(-1, cols), V[n].reshape(-1, cols), 512)
        delta[n], new_m[n], new_v[n] = d2.reshape(W[n].shape), m2.reshape(W[n].shape), v2.reshape(W[n].shape)
    shapes = [W[n].shape for n in SMALL]
    packs = [_pack([src[n] for n in SMALL], 8 * 128).reshape(-1, 128) for src in (W, grads, M, V)]
    d2, m2, v2 = _adamw(*packs, 4096)
    for dst, val in ((delta, d2), (new_m, m2), (new_v, v2)):
        for n, piece in zip(SMALL, _unpack(val.reshape(-1), shapes)):
            dst[n] = piece

    return (loss, grad_x, *[grads[n] for n in WEIGHTS], *[delta[n] for n in WEIGHTS],
            *[new_m[n] for n in WEIGHTS], *[new_v[n] for n in WEIGHTS])
```

```python
import math

import numpy as np
import jax
import jax.numpy as jnp
from jax import lax
from jax.experimental import pallas as pl
from jax.experimental.pallas import tpu as pltpu

f32 = jnp.float32
bf16 = jnp.bfloat16
HI = lax.Precision.HIGHEST
MESH = pl.DeviceIdType.MESH

EPS = 1e-6
HEAD = 128
A_CHUNK = 64
A_SUB = 16
B_BLOCKS = 4
B_CHUNK = 256
CONV_WIDTH = 4
LRU_C = 8.0
C_GROUPS = 4
C_CHUNK = 128
N_CHIPS = 4
ADAM_LR, ADAM_B1, ADAM_B2, ADAM_EPS, ADAM_WD, ADAM_STEP = 0.001, 0.9, 0.999, 1e-08, 0.01, 10
VMEM_LIMIT = 56 * 1024 * 1024


def _cparams(n_axes):
    return pltpu.CompilerParams(dimension_semantics=("arbitrary",) * n_axes, vmem_limit_bytes=VMEM_LIMIT)


def _sds(shape, dtype):
    return jax.ShapeDtypeStruct(tuple(shape), dtype)


def _full(shape):
    n = len(shape)
    return pl.BlockSpec(tuple(shape), lambda *_: (0,) * n)


def _dot(a, b):
    return jnp.dot(a, b, preferred_element_type=f32)


def _dot_nt(a, b):
    return lax.dot_general(a, b, (((1,), (1,)), ((), ())), preferred_element_type=f32)


def _dot_tn(a, b):
    return lax.dot_general(a, b, (((0,), (0,)), ((), ())), preferred_element_type=f32)


def _silu(x):
    return x * jax.nn.sigmoid(x)


def _group_avg_matrix(n, group):
    idx = np.arange(n) // group
    return jnp.asarray((idx[:, None] == idx[None, :]).astype(np.float32) / group)


class _Place:
    def __init__(self):
        self.x, self.y, self.c = lax.axis_index("x"), lax.axis_index("y"), lax.axis_index("c")
        self.slot = 2 * self.x + self.y

    def peer(self, kind):
        x, y, c = self.x, self.y, self.c
        return {"sib": (x, y, 1 - c), "fx": (1 - x, y, c), "fy": (x, 1 - y, c), "fxy": (1 - x, 1 - y, c)}[kind]

    def peer_slot(self, kind):
        x, y = self.x, self.y
        return {"fx": 2 * (1 - x) + y, "fy": 2 * x + (1 - y), "fxy": 2 * (1 - x) + (1 - y)}[kind]


CHIP_KINDS = ("fx", "fy", "fxy")


def _exchange(name, ins, outs, remote, local=(), aliases=None):
    n_in, n_out, n_r, n_l = len(ins), len(outs), len(remote), len(local)

    def body(*refs):
        in_refs, out_refs = refs[:n_in], refs[n_in:n_in + n_out]
        send, recv, lsem = refs[n_in + n_out:]
        p = _Place()
        lcopies = []
        for t, (src, dst) in enumerate(local):
            cp = pltpu.make_async_copy(src(in_refs, out_refs, p), dst(in_refs, out_refs, p), lsem.at[t])
            cp.start()
            lcopies.append(cp)
        copies = []
        for t, (src, dst, kind) in enumerate(remote):
            cp = pltpu.make_async_remote_copy(
                src_ref=src(in_refs, out_refs, p), dst_ref=dst(in_refs, out_refs, p),
                send_sem=send.at[t], recv_sem=recv.at[t], device_id=p.peer(kind), device_id_type=MESH)
            cp.start()
            copies.append(cp)
        for cp in copies:
            cp.wait_recv()
        for cp in copies:
            cp.wait_send()
        for cp in lcopies:
            cp.wait()

    anyspec = pl.BlockSpec(memory_space=pl.ANY)
    res = pl.pallas_call(
        body, name=name, out_shape=tuple(outs),
        in_specs=[anyspec] * n_in, out_specs=tuple([anyspec] * n_out),
        scratch_shapes=[pltpu.SemaphoreType.DMA((n_r,)), pltpu.SemaphoreType.DMA((n_r,)),
                        pltpu.SemaphoreType.DMA((max(n_l, 1),))],
        input_output_aliases=aliases or {},
        compiler_params=pltpu.CompilerParams(has_side_effects=True),
    )(*ins)
    return list(res)


HBM_SPEC = pl.BlockSpec(memory_space=pltpu.HBM)
SEM_SPEC = pl.BlockSpec(memory_space=pltpu.SEMAPHORE)
ANY_SPEC = pl.BlockSpec(memory_space=pl.ANY)
DATAFLOW = pltpu.SideEffectType.DATAFLOW_SIDE_EFFECTING


def _in_hbm(a):
    return pltpu.with_memory_space_constraint(a, pltpu.HBM)


def _start_copies(name, srcs, lands, remote, after):
    n_s, n_l, n_r = len(srcs), len(lands), len(remote)

    def body(*refs):
        src_refs, land_refs = refs[:n_s], refs[n_s:n_s + n_l]
        send, recv = refs[n_s + n_l + 1], refs[n_s + n_l + 2]
        token = refs[-1]
        p = _Place()
        for t, (src, dst, kind) in enumerate(remote):
            pltpu.make_async_remote_copy(
                src_ref=src(src_refs, land_refs, p), dst_ref=dst(src_refs, land_refs, p),
                send_sem=send.at[t], recv_sem=recv.at[t], device_id=p.peer(kind), device_id_type=MESH).start()
        token[...] = jnp.zeros_like(token)

    thru = [pltpu.HBM(a.shape, a.dtype) for a in list(srcs) + list(lands)]
    res = pl.pallas_call(
        body, name=name,
        out_shape=(pltpu.SemaphoreType.DMA((n_r,)), pltpu.SemaphoreType.DMA((n_r,)), *thru, _sds((8, 128), f32)),
        in_specs=[HBM_SPEC] * (n_s + n_l) + [ANY_SPEC],
        out_specs=(SEM_SPEC, SEM_SPEC, *([HBM_SPEC] * (n_s + n_l)), pl.BlockSpec(memory_space=pltpu.VMEM)),
        input_output_aliases={i: 2 + i for i in range(n_s + n_l)},
        compiler_params=pltpu.CompilerParams(has_side_effects=DATAFLOW),
    )(*[_in_hbm(a) for a in srcs], *[_in_hbm(a) for a in lands], after)
    return res[0], res[1], list(res[2:2 + n_s]), list(res[2 + n_s:2 + n_s + n_l]), res[-1]


def _wait_copies(name, send, recv, srcs, lands, remote, after):
    n_s, n_l = len(srcs), len(lands)

    def body(*refs):
        src_refs, land_refs = refs[:n_s], refs[n_s:n_s + n_l]
        send_ref, recv_ref = refs[n_s + n_l], refs[n_s + n_l + 1]
        p = _Place()
        for t, (src, dst, kind) in enumerate(remote):
            cp = pltpu.make_async_remote_copy(
                src_ref=src(src_refs, land_refs, p), dst_ref=dst(src_refs, land_refs, p),
                send_sem=send_ref.at[t], recv_sem=recv_ref.at[t], device_id=p.peer(kind), device_id_type=MESH)
            cp.wait_send()
            cp.wait_recv()

    thru = [pltpu.HBM(a.shape, a.dtype) for a in list(srcs) + list(lands)]
    res = pl.pallas_call(
        body, name=name, out_shape=tuple(thru),
        in_specs=[HBM_SPEC] * (n_s + n_l) + [SEM_SPEC, SEM_SPEC, ANY_SPEC],
        out_specs=tuple([HBM_SPEC] * (n_s + n_l)),
        input_output_aliases={i: i for i in range(n_s + n_l)},
        compiler_params=pltpu.CompilerParams(has_side_effects=DATAFLOW),
    )(*srcs, *lands, send, recv, after)
    return list(res[n_s:])


def _half(ref, c):
    n2 = ref.shape[0] // 2
    return ref.at[pl.ds(c * n2, n2)]


def _gather_ici_plan(n):
    return [(lambda s, o, p, a=a: _half(s[a], p.c), lambda s, o, p, a=a: _half(o[a].at[p.slot], p.c), kind)
            for a in range(n) for kind in CHIP_KINDS]


def _gather_d2d(name, shards, lands):
    n = len(shards)
    remote = []
    for a in range(n):
        for kind in CHIP_KINDS:
            view = lambda i, o, p, a=a, kind=kind: _half(o[a].at[p.peer_slot(kind)], p.c)
            remote.append((view, view, "sib"))
    local = [(lambda i, o, p, a=a: i[n + a], lambda i, o, p, a=a: o[a].at[p.slot]) for a in range(n)]
    outs = [_sds(g.shape, g.dtype) for g in lands]
    return _exchange(name, list(lands) + list(shards), outs, remote, local, aliases={a: a for a in range(n)})


def _ffn_fwd(h, gain, wg, wu, wd, tm):
    T, D = h.shape
    nsh, F = wg.shape[0], wg.shape[2]
    nt = T // tm

    def body(h_ref, gain_ref, wg_ref, wu_ref, wd_ref, out_ref, gs_ref, us_ref, xn_ref, acc_ref):
        k = pl.program_id(1)

        @pl.when(k == 0)
        def _():
            hv = h_ref[...]
            r = lax.rsqrt(jnp.mean(hv * hv, axis=-1, keepdims=True) + EPS)
            xn_ref[...] = (hv * r * gain_ref[...]).astype(bf16)
            acc_ref[...] = jnp.zeros_like(acc_ref)

        xn = xn_ref[...]
        g = _dot(xn, wg_ref[...])
        u = _dot(xn, wu_ref[...])
        gs_ref[...] = g.astype(bf16)
        us_ref[...] = u.astype(bf16)
        a = (_silu(g) * u).astype(bf16)
        acc_ref[...] += _dot(a, wd_ref[...])

        @pl.when(k == nsh - 1)
        def _():
            out_ref[...] = h_ref[...] + 0.5 * acc_ref[...]

    wspec = pl.BlockSpec((None, D, F), lambda i, k: (k, 0, 0))
    return pl.pallas_call(
        body, name="ffn_fwd", grid=(nt, nsh),
        in_specs=[pl.BlockSpec((tm, D), lambda i, k: (i, 0)), _full((1, D)), wspec, wspec,
                  pl.BlockSpec((None, F, D), lambda i, k: (k, 0, 0))],
        out_specs=(pl.BlockSpec((tm, D), lambda i, k: (i, 0)),
                   pl.BlockSpec((None, tm, F), lambda i, k: (k, i, 0)),
                   pl.BlockSpec((None, tm, F), lambda i, k: (k, i, 0))),
        out_shape=(_sds((T, D), f32), _sds((nsh, T, F), bf16), _sds((nsh, T, F), bf16)),
        scratch_shapes=[pltpu.VMEM((tm, D), bf16), pltpu.VMEM((tm, D), f32)],
        compiler_params=_cparams(2),
    )(h, gain, wg, wu, wd)


def _ffn_bwd_dgrad(h, gain, dout, gs, us, wg, wu, wd, tm):
    T, D = h.shape
    nsh, F = wg.shape[0], wg.shape[2]
    nt = T // tm

    def body(h_ref, gain_ref, dout_ref, gs_ref, us_ref, wg_ref, wu_ref, wd_ref,
             dh_ref, dgain_ref, dg_ref, du_ref, xn_ref, dob_ref, xh_ref, acc_ref):
        i, k = pl.program_id(0), pl.program_id(1)

        @pl.when((i == 0) & (k == 0))
        def _():
            dgain_ref[...] = jnp.zeros_like(dgain_ref)

        @pl.when(k == 0)
        def _():
            hv = h_ref[...]
            r = lax.rsqrt(jnp.mean(hv * hv, axis=-1, keepdims=True) + EPS)
            xh = hv * r
            xh_ref[...] = xh
            xn_ref[...] = (xh * gain_ref[...]).astype(bf16)
            dob_ref[...] = (0.5 * dout_ref[...]).astype(bf16)
            acc_ref[...] = jnp.zeros_like(acc_ref)

        da = _dot_nt(dob_ref[...], wd_ref[...])
        g = gs_ref[...].astype(f32)
        u = us_ref[...].astype(f32)
        sg = jax.nn.sigmoid(g)
        dg = (da * u * (sg * (1.0 + g * (1.0 - sg)))).astype(bf16)
        du = (da * (g * sg)).astype(bf16)
        dg_ref[...] = dg
        du_ref[...] = du
        acc_ref[...] += _dot_nt(dg, wg_ref[...]) + _dot_nt(du, wu_ref[...])

        @pl.when(k == nsh - 1)
        def _():
            hv = h_ref[...]
            r = lax.rsqrt(jnp.mean(hv * hv, axis=-1, keepdims=True) + EPS)
            xh = xh_ref[...]
            dxn = acc_ref[...]
            dgain_ref[...] += jnp.sum(dxn * xh, axis=0, keepdims=True)
            dxh = dxn * gain_ref[...]
            dh_ref[...] = dout_ref[...] + r * (dxh - xh * jnp.mean(dxh * xh, axis=-1, keepdims=True))

    tok = pl.BlockSpec((tm, D), lambda i, k: (i, 0))
    sav = pl.BlockSpec((None, tm, F), lambda i, k: (k, i, 0))
    wspec = pl.BlockSpec((None, D, F), lambda i, k: (k, 0, 0))
    return pl.pallas_call(
        body, name="ffn_bwd_dgrad", grid=(nt, nsh),
        in_specs=[tok, _full((1, D)), tok, sav, sav, wspec, wspec,
                  pl.BlockSpec((None, F, D), lambda i, k: (k, 0, 0))],
        out_specs=(tok, _full((1, D)), sav, sav, tok, tok),
        out_shape=(_sds((T, D), f32), _sds((1, D), f32), _sds((nsh, T, F), bf16), _sds((nsh, T, F), bf16),
                   _sds((T, D), bf16), _sds((T, D), bf16)),
        scratch_shapes=[pltpu.VMEM((tm, D), f32), pltpu.VMEM((tm, D), f32)],
        compiler_params=_cparams(2),
    )(h, gain, dout, gs, us, wg, wu, wd)


def _ffn_bwd_wgrad(xn, dob, gs, us, dg, du, tm):
    T, D = xn.shape
    nsh, F = gs.shape[0], gs.shape[2]
    nt = T // tm

    def body(xn_ref, dob_ref, gs_ref, us_ref, dg_ref, du_ref, dwg_ref, dwu_ref, dwd_ref, ag_ref, au_ref, ad_ref):
        i = pl.program_id(1)

        @pl.when(i == 0)
        def _():
            ag_ref[...] = jnp.zeros_like(ag_ref)
            au_ref[...] = jnp.zeros_like(au_ref)
            ad_ref[...] = jnp.zeros_like(ad_ref)

        xn_v = xn_ref[...]
        ag_ref[...] += _dot_tn(xn_v, dg_ref[...])
        au_ref[...] += _dot_tn(xn_v, du_ref[...])
        g = gs_ref[...].astype(f32)
        a = (_silu(g) * us_ref[...].astype(f32)).astype(bf16)
        ad_ref[...] += _dot_tn(a, dob_ref[...])

        @pl.when(i == nt - 1)
        def _():
            dwg_ref[...] = ag_ref[...].astype(bf16)
            dwu_ref[...] = au_ref[...].astype(bf16)
            dwd_ref[...] = ad_ref[...].astype(bf16)

    tok = pl.BlockSpec((tm, D), lambda k, i: (i, 0))
    sav = pl.BlockSpec((None, tm, F), lambda k, i: (k, i, 0))
    wspec = pl.BlockSpec((None, D, F), lambda k, i: (k, 0, 0))
    wdspec = pl.BlockSpec((None, F, D), lambda k, i: (k, 0, 0))
    return pl.pallas_call(
        body, name="ffn_bwd_wgrad", grid=(nsh, nt),
        in_specs=[tok, tok, sav, sav, sav, sav],
        out_specs=(wspec, wspec, wdspec),
        out_shape=(_sds((nsh, D, F), bf16), _sds((nsh, D, F), bf16), _sds((nsh, F, D), bf16)),
        scratch_shapes=[pltpu.VMEM((D, F), f32), pltpu.VMEM((D, F), f32), pltpu.VMEM((F, D), f32)],
        compiler_params=_cparams(2),
    )(xn, dob, gs, us, dg, du)


def _proj_in_fwd(h, gain, w_in, tm):
    T, D = h.shape
    nsh, N = w_in.shape[0], w_in.shape[2]
    nt = T // tm

    def body(h_ref, gain_ref, w_ref, z_ref, xn_ref):
        @pl.when(pl.program_id(1) == 0)
        def _():
            hv = h_ref[...]
            r = lax.rsqrt(jnp.mean(hv * hv, axis=-1, keepdims=True) + EPS)
            xn_ref[...] = (hv * r * gain_ref[...]).astype(bf16)

        z_ref[...] = _dot(xn_ref[...], w_ref[...])

    return pl.pallas_call(
        body, name="proj_in_fwd", grid=(nt, nsh),
        in_specs=[pl.BlockSpec((tm, D), lambda i, k: (i, 0)), _full((1, D)),
                  pl.BlockSpec((None, D, N), lambda i, k: (k, 0, 0))],
        out_specs=pl.BlockSpec((tm, N), lambda i, k: (i, k)),
        out_shape=_sds((T, nsh * N), f32),
        scratch_shapes=[pltpu.VMEM((tm, D), bf16)],
        compiler_params=_cparams(2),
    )(h, gain, w_in)


def _proj_in_bwd_dgrad(h, gain, dres, dz, w_in, tm):
    T, D = h.shape
    nsh, N = w_in.shape[0], w_in.shape[2]
    nt = T // tm

    def body(h_ref, gain_ref, dres_ref, dz_ref, w_ref, dh_ref, dgain_ref, xn_ref, acc_ref):
        i, k = pl.program_id(0), pl.program_id(1)

        @pl.when((i == 0) & (k == 0))
        def _():
            dgain_ref[...] = jnp.zeros_like(dgain_ref)

        @pl.when(k == 0)
        def _():
            acc_ref[...] = jnp.zeros_like(acc_ref)

        acc_ref[...] += _dot_nt(dz_ref[...].astype(bf16), w_ref[...])

        @pl.when(k == nsh - 1)
        def _():
            hv = h_ref[...]
            r = lax.rsqrt(jnp.mean(hv * hv, axis=-1, keepdims=True) + EPS)
            xh = hv * r
            xn_ref[...] = (xh * gain_ref[...]).astype(bf16)
            dxn = acc_ref[...]
            dgain_ref[...] += jnp.sum(dxn * xh, axis=0, keepdims=True)
            dxh = dxn * gain_ref[...]
            dh_ref[...] = dres_ref[...] + r * (dxh - xh * jnp.mean(dxh * xh, axis=-1, keepdims=True))

    tok = pl.BlockSpec((tm, D), lambda i, k: (i, 0))
    return pl.pallas_call(
        body, name="proj_in_bwd_dgrad", grid=(nt, nsh),
        in_specs=[tok, _full((1, D)), tok, pl.BlockSpec((tm, N), lambda i, k: (i, k)),
                  pl.BlockSpec((None, D, N), lambda i, k: (k, 0, 0))],
        out_specs=(tok, _full((1, D)), tok),
        out_shape=(_sds((T, D), f32), _sds((1, D), f32), _sds((T, D), bf16)),
        scratch_shapes=[pltpu.VMEM((tm, D), f32)],
        compiler_params=_cparams(2),
    )(h, gain, dres, dz, w_in)


def _proj_in_bwd_wgrad(xn, dz, nsh, tm):
    T, D = xn.shape
    N = dz.shape[1] // nsh
    nt = T // tm

    def body(xn_ref, dz_ref, dw_ref, acc_ref):
        i = pl.program_id(1)

        @pl.when(i == 0)
        def _():
            acc_ref[...] = jnp.zeros_like(acc_ref)

        acc_ref[...] += _dot_tn(xn_ref[...], dz_ref[...].astype(bf16))

        @pl.when(i == nt - 1)
        def _():
            dw_ref[...] = acc_ref[...].astype(bf16)

    return pl.pallas_call(
        body, name="proj_in_bwd_wgrad", grid=(nsh, nt),
        in_specs=[pl.BlockSpec((tm, D), lambda k, i: (i, 0)), pl.BlockSpec((tm, N), lambda k, i: (i, k))],
        out_specs=pl.BlockSpec((None, D, N), lambda k, i: (k, 0, 0)),
        out_shape=_sds((nsh, D, N), bf16),
        scratch_shapes=[pltpu.VMEM((D, N), f32)],
        compiler_params=_cparams(2),
    )(xn, dz)


def _proj_out_fwd(h, oa, ob, oc, w_out, tm):
    T, D = h.shape
    nsh, R = w_out.shape[0], w_out.shape[1]
    da, db = oa.shape[1], ob.shape[1]
    nt = T // tm

    def body(h_ref, oa_ref, ob_ref, oc_ref, w_ref, out_ref):
        w = w_ref[...].reshape(nsh * R, D)
        out_ref[...] = (h_ref[...] + _dot(oa_ref[...], w[:da]) + _dot(ob_ref[...], w[da:da + db])
                        + _dot(oc_ref[...], w[da + db:]))

    def tok(n):
        return pl.BlockSpec((tm, n), lambda i: (i, 0))

    return pl.pallas_call(
        body, name="proj_out_fwd", grid=(nt,),
        in_specs=[tok(D), tok(da), tok(db), tok(oc.shape[1]), _full((nsh, R, D))],
        out_specs=tok(D), out_shape=_sds((T, D), f32),
        compiler_params=_cparams(1),
    )(h, oa, ob, oc, w_out)


def _proj_out_bwd(dh, oa, ob, oc, w_out, tm):
    T, D = dh.shape
    nsh, R = w_out.shape[0], w_out.shape[1]
    da, db, dc = oa.shape[1], ob.shape[1], oc.shape[1]
    nt = T // tm

    def body(dh_ref, oa_ref, ob_ref, oc_ref, w_ref, doa_ref, dob_ref, doc_ref, dw_ref, acc_ref):
        i = pl.program_id(0)

        @pl.when(i == 0)
        def _():
            acc_ref[...] = jnp.zeros_like(acc_ref)

        d = dh_ref[...].astype(bf16)
        w = w_ref[...].reshape(nsh * R, D)
        dm = _dot_nt(d, w)
        doa_ref[...] = dm[:, :da]
        dob_ref[...] = dm[:, da:da + db]
        doc_ref[...] = dm[:, da + db:]
        acc_ref[pl.ds(0, da), :] += _dot_tn(oa_ref[...], d)
        acc_ref[pl.ds(da, db), :] += _dot_tn(ob_ref[...], d)
        acc_ref[pl.ds(da + db, dc), :] += _dot_tn(oc_ref[...], d)

        @pl.when(i == nt - 1)
        def _():
            dw_ref[...] = acc_ref[...].astype(bf16).reshape(nsh, R, D)

    def tok(n):
        return pl.BlockSpec((tm, n), lambda i: (i, 0))

    wspec = _full((nsh, R, D))
    return pl.pallas_call(
        body, name="proj_out_bwd", grid=(nt,),
        in_specs=[tok(D), tok(da), tok(db), tok(dc), wspec],
        out_specs=(tok(da), tok(db), tok(dc), wspec),
        out_shape=(_sds((T, da), f32), _sds((T, db), f32), _sds((T, dc), f32), _sds((nsh, R, D), bf16)),
        scratch_shapes=[pltpu.VMEM((nsh * R, D), f32)],
        compiler_params=_cparams(1),
    )(dh, oa, ob, oc, w_out)


def _head_sum(m, n_heads):
    parts = []
    for hd in range(n_heads):
        s = jnp.sum(m[:, hd * HEAD:(hd + 1) * HEAD], axis=-1, keepdims=True)
        parts.append(jnp.broadcast_to(s, (m.shape[0], HEAD)))
    return parts[0] if n_heads == 1 else jnp.concatenate(parts, axis=1)


def _hgrn_chunk(q, fl, iv, g, lb, gain, states, tri, n_heads):
    C = q.shape[0]
    qs = _silu(q)
    forget = lb + (1.0 - lb) * jax.nn.sigmoid(fl)
    kk = 1.0 - forget
    logf = jnp.log(forget)
    b = jnp.dot(tri, logf, precision=HI, preferred_element_type=f32)
    n_sub = C // A_SUB
    row = lax.broadcasted_iota(jnp.int32, (A_SUB, 1), 0)
    outs = []
    for blk in range(n_sub):
        lo = blk * A_SUB
        b_i, q_i, k_i, v_i = b[lo:lo + A_SUB], qs[lo:lo + A_SUB], kk[lo:lo + A_SUB], iv[lo:lo + A_SUB]
        acc = jnp.zeros_like(q_i)
        for s in range(A_SUB):
            e = jnp.exp(jnp.minimum(b_i - b_i[s:s + 1], 0.0))
            m = jnp.where(row >= s, q_i * e * k_i[s:s + 1], 0.0)
            acc = acc + _head_sum(m, n_heads) * v_i[s:s + 1]
        if blk > 0:
            piv = b_i[0:1]
            qt = (q_i * jnp.exp(b_i - piv)).astype(bf16)
            kt = (kk[:lo] * jnp.exp(piv - b[:lo])).astype(bf16)
            vb = iv[:lo].astype(bf16)
            parts = []
            for hd in range(n_heads):
                sl = slice(hd * HEAD, (hd + 1) * HEAD)
                sc = _dot_nt(qt[:, sl], kt[:, sl])
                parts.append(_dot(sc.astype(bf16), vb[:, sl]))
            acc = acc + (parts[0] if n_heads == 1 else jnp.concatenate(parts, axis=1))
        outs.append(acc)
    o = jnp.concatenate(outs, axis=0)
    qe = (qs * jnp.exp(b)).astype(bf16)
    b_end = b[C - 1:C]
    kd = (kk * jnp.exp(b_end - b)).astype(bf16)
    vb = iv.astype(bf16)
    dec = jnp.exp(b_end)
    inter, new_states = [], []
    for hd in range(n_heads):
        sl = slice(hd * HEAD, (hd + 1) * HEAD)
        st = states[hd]
        inter.append(_dot_nt(qe[:, sl], st.astype(bf16)))
        new_states.append(dec[:, sl] * st + _dot_tn(vb[:, sl], kd[:, sl]))
    o = o + (inter[0] if n_heads == 1 else jnp.concatenate(inter, axis=1))
    ms = _head_sum(o * o, n_heads) * (1.0 / HEAD)
    out = o * lax.rsqrt(ms + EPS) * gain * _silu(g)
    return out, tuple(new_states)


def _tri_matrix(n):
    return jnp.asarray(np.tril(np.ones((n, n), np.float32)))


def _hgrn_fwd(z, lb, gain, d_a):
    T = z.shape[0]
    C = A_CHUNK
    nc = T // C
    nh = d_a // HEAD
    tri = _tri_matrix(C)

    def body(q_ref, f_ref, i_ref, g_ref, lb_ref, gain_ref, tri_ref, out_ref, st_ref, carry_ref):
        @pl.when(pl.program_id(0) == 0)
        def _():
            carry_ref[...] = jnp.zeros_like(carry_ref)

        states = tuple(carry_ref[hd] for hd in range(nh))
        st_ref[...] = carry_ref[...]
        out, new_states = _hgrn_chunk(q_ref[...], f_ref[...], i_ref[...], g_ref[...], lb_ref[...], gain_ref[...],
                                      states, tri_ref[...], nh)
        out_ref[...] = out.astype(bf16)
        for hd in range(nh):
            carry_ref[hd] = new_states[hd]

    def col(j):
        return pl.BlockSpec((C, d_a), lambda c, j=j: (c, j))

    return pl.pallas_call(
        body, name="hgrn_fwd", grid=(nc,),
        in_specs=[col(0), col(1), col(2), col(3), _full((1, d_a)), _full((1, d_a)), _full((C, C))],
        out_specs=(pl.BlockSpec((C, d_a), lambda c: (c, 0)), pl.BlockSpec((None, nh, HEAD, HEAD), lambda c: (c, 0, 0, 0))),
        out_shape=(_sds((T, d_a), bf16), _sds((nc, nh, HEAD, HEAD), f32)),
        scratch_shapes=[pltpu.VMEM((nh, HEAD, HEAD), f32)],
        compiler_params=_cparams(1),
    )(z, z, z, z, lb, gain, tri)


def _hgrn_bwd(z, lb, gain, states, dout, d_a):
    T = z.shape[0]
    C = A_CHUNK
    nc = T // C
    nh = d_a // HEAD
    tri = _tri_matrix(C)

    def body(q_ref, f_ref, i_ref, g_ref, lb_ref, gain_ref, tri_ref, st_ref, do_ref,
             dz_ref, dlb_ref, dgain_ref, carry_ref):
        @pl.when(pl.program_id(0) == 0)
        def _():
            carry_ref[...] = jnp.zeros_like(carry_ref)
            dlb_ref[...] = jnp.zeros_like(dlb_ref)
            dgain_ref[...] = jnp.zeros_like(dgain_ref)

        tri_v = tri_ref[...]

        def fn(q, fl, iv, g, lbv, gv, sts):
            return _hgrn_chunk(q, fl, iv, g, lbv, gv, sts, tri_v, nh)

        states_in = tuple(st_ref[hd] for hd in range(nh))
        _, vjp = jax.vjp(fn, q_ref[...], f_ref[...], i_ref[...], g_ref[...], lb_ref[...], gain_ref[...], states_in)
        dstates = tuple(carry_ref[hd] for hd in range(nh))
        dq, df, di, dg, dlb, dgain, dst = vjp((do_ref[...], dstates))
        dz_ref[:, 0:d_a] = dq
        dz_ref[:, d_a:2 * d_a] = df
        dz_ref[:, 2 * d_a:3 * d_a] = di
        dz_ref[:, 3 * d_a:4 * d_a] = dg
        dlb_ref[...] += dlb
        dgain_ref[...] += dgain
        for hd in range(nh):
            carry_ref[hd] = dst[hd]

    def col(j):
        return pl.BlockSpec((C, d_a), lambda c, j=j: (nc - 1 - c, j))

    return pl.pallas_call(
        body, name="hgrn_bwd", grid=(nc,),
        in_specs=[col(0), col(1), col(2), col(3), _full((1, d_a)), _full((1, d_a)), _full((C, C)),
                  pl.BlockSpec((None, nh, HEAD, HEAD), lambda c: (nc - 1 - c, 0, 0, 0)),
                  pl.BlockSpec((C, d_a), lambda c: (nc - 1 - c, 0))],
        out_specs=(pl.BlockSpec((C, 4 * d_a), lambda c: (nc - 1 - c, 0)), _full((1, d_a)), _full((1, d_a))),
        out_shape=(_sds(z.shape, f32), _sds((1, d_a), f32), _sds((1, d_a), f32)),
        scratch_shapes=[pltpu.VMEM((nh, HEAD, HEAD), f32)],
        compiler_params=_cparams(1),
    )(z, z, z, z, lb, gain, tri, states, dout)


def _one_minus_exp(x):
    series = -x * (1.0 + x * (0.5 + x * (1.0 / 6.0 + x * (1.0 / 24.0))))
    return jnp.where(x > -0.03, series, 1.0 - jnp.exp(x))


def _lru_pre(xc, wa, ba, wx, bx, lam):
    xb16 = xc.astype(bf16)
    r = jax.nn.sigmoid(_dot(xb16, wa.astype(bf16)) + ba)
    gi = jax.nn.sigmoid(_dot(xb16, wx.astype(bf16)) + bx)
    log_a = -LRU_C * r * jax.nn.softplus(-lam)
    a = jnp.exp(log_a)
    mult = jnp.sqrt(_one_minus_exp(2.0 * log_a))
    return a, mult * gi * xc


def _lru_post(h, gate, gain, avg):
    y = h * jax.nn.gelu(gate)
    ms = jnp.dot(y * y, avg, precision=HI, preferred_element_type=f32)
    return y * lax.rsqrt(ms + EPS) * gain


def _shift_down(x, d, prev):
    row = lax.broadcasted_iota(jnp.int32, x.shape, 0)
    return jnp.where(row >= d, pltpu.roll(x, d, 0), pltpu.roll(prev, d, 0))


def _shift_up(x, d, nxt):
    n = x.shape[0]
    row = lax.broadcasted_iota(jnp.int32, x.shape, 0)
    return jnp.where(row < n - d, pltpu.roll(x, n - d, 0), pltpu.roll(nxt, n - d, 0))


def _scan_rows(a, u, reverse):
    n = a.shape[0]
    row = lax.broadcasted_iota(jnp.int32, a.shape, 0)
    d = 1
    while d < n:
        shift, ok = (n - d, row < n - d) if reverse else (d, row >= d)
        su = jnp.where(ok, pltpu.roll(u, shift, 0), 0.0)
        sa = jnp.where(ok, pltpu.roll(a, shift, 0), 1.0)
        u = u + a * su
        a = a * sa
        d *= 2
    return a, u


def _conv(xb, xprev, cw, cb):
    xc = cb + cw[CONV_WIDTH - 1:CONV_WIDTH] * xb
    for d in range(1, CONV_WIDTH):
        xc = xc + cw[CONV_WIDTH - 1 - d:CONV_WIDTH - d] * _shift_down(xb, d, xprev)
    return xc


def _lru_fwd(z, col0, d_b, cw, cb, wa, ba, wx, bx, lam, gain, avg):
    T = z.shape[0]
    R = min(B_CHUNK, T)
    nr = T // R
    jb = col0 // d_b

    def body(xb_ref, gate_ref, cw_ref, cb_ref, wa_ref, ba_ref, wx_ref, bx_ref, lam_ref, gain_ref, avg_ref,
             out_ref, h_ref, xprev_ref, hprev_ref):
        @pl.when(pl.program_id(0) == 0)
        def _():
            xprev_ref[...] = jnp.zeros_like(xprev_ref)
            hprev_ref[...] = jnp.zeros_like(hprev_ref)

        xb = xb_ref[...]
        xc = _conv(xb, xprev_ref[...], cw_ref[...], cb_ref[...])
        a, u = _lru_pre(xc, wa_ref[...], ba_ref[...], wx_ref[...], bx_ref[...], lam_ref[...])
        acum, hl = _scan_rows(a, u, False)
        h = hl + acum * hprev_ref[R - 1:R, :]
        h_ref[...] = h
        out_ref[...] = _lru_post(h, gate_ref[...], gain_ref[...], avg_ref[...]).astype(bf16)
        xprev_ref[...] = xb
        hprev_ref[...] = h

    vec = _full((1, d_b))
    return pl.pallas_call(
        body, name="lru_fwd", grid=(nr,),
        in_specs=[pl.BlockSpec((R, d_b), lambda i: (i, jb)), pl.BlockSpec((R, d_b), lambda i: (i, jb + 1)),
                  _full((CONV_WIDTH, d_b)), vec, _full((d_b, d_b)), vec, _full((d_b, d_b)), vec, vec, vec, _full((d_b, d_b))],
        out_specs=(pl.BlockSpec((R, d_b), lambda i: (i, 0)), pl.BlockSpec((R, d_b), lambda i: (i, 0))),
        out_shape=(_sds((T, d_b), bf16), _sds((T, d_b), f32)),
        scratch_shapes=[pltpu.VMEM((R, d_b), f32), pltpu.VMEM((R, d_b), f32)],
        compiler_params=_cparams(1),
    )(z, z, cw, cb, wa, ba, wx, bx, lam, gain, avg)


def _lru_bwd(z, col0, d_b, hsave, dout, dz_buf, cw, cb, wa, ba, wx, bx, lam, gain, avg):
    T = z.shape[0]
    R = min(B_CHUNK, T)
    nr = T // R
    jb = col0 // d_b

    def body(xb_ref, xp_ref, gate_ref, h_ref, hp_ref, do_ref,
             cw_ref, cb_ref, wa_ref, ba_ref, wx_ref, bx_ref, lam_ref, gain_ref, avg_ref, dzin_ref,
             dz_ref, dcw_ref, dcb_ref, dwa_ref, dba_ref, dwx_ref, dbx_ref, dlam_ref, dgain_ref,
             gfirst_ref, afirst_ref, dxcn_ref):
        step = pl.program_id(0)
        first_in_time = step == nr - 1

        @pl.when(step == 0)
        def _():
            for r in (dcw_ref, dcb_ref, dwa_ref, dba_ref, dwx_ref, dbx_ref, dlam_ref, dgain_ref,
                      gfirst_ref, afirst_ref, dxcn_ref):
                r[...] = jnp.zeros_like(r)

        xb = xb_ref[...]
        keep = jnp.where(first_in_time, 0.0, 1.0)
        xprev = xp_ref[...] * keep
        hprev = hp_ref[...] * keep
        cw = cw_ref[...]
        xc = _conv(xb, xprev, cw, cb_ref[...])
        (a, _), vjp_pre = jax.vjp(_lru_pre, xc, wa_ref[...], ba_ref[...], wx_ref[...], bx_ref[...], lam_ref[...])
        h = h_ref[...]
        avg = avg_ref[...]
        _, vjp_post = jax.vjp(lambda hh, gg, gn: _lru_post(hh, gg, gn, avg), h, gate_ref[...], gain_ref[...])
        dh, dgate, dgain = vjp_post(do_ref[...])
        a_next = _shift_up(a, 1, jnp.broadcast_to(afirst_ref[0:1, :], a.shape))
        acum, gl = _scan_rows(a_next, dh, True)
        gtot = gl + acum * gfirst_ref[0:1, :]
        da = gtot * _shift_down(h, 1, hprev)
        dxc, dwa, dba, dwx, dbx, dlam = vjp_pre((da, gtot))
        dxcn = dxcn_ref[...]
        dxb = cw[CONV_WIDTH - 1:CONV_WIDTH] * dxc
        dcw_ref[CONV_WIDTH - 1:CONV_WIDTH, :] += jnp.sum(dxc * xb, axis=0, keepdims=True)
        for d in range(1, CONV_WIDTH):
            tap = CONV_WIDTH - 1 - d
            dxb = dxb + cw[tap:tap + 1] * _shift_up(dxc, d, dxcn)
            dcw_ref[tap:tap + 1, :] += jnp.sum(dxc * _shift_down(xb, d, xprev), axis=0, keepdims=True)
        dz_ref[:, 0:d_b] = dxb
        dz_ref[:, d_b:2 * d_b] = dgate
        dcb_ref[...] += jnp.sum(dxc, axis=0, keepdims=True)
        dwa_ref[...] += dwa
        dba_ref[...] += dba
        dwx_ref[...] += dwx
        dbx_ref[...] += dbx
        dlam_ref[...] += dlam
        dgain_ref[...] += dgain
        gfirst_ref[...] = jnp.broadcast_to(gtot[0:1, :], gfirst_ref.shape)
        afirst_ref[...] = jnp.broadcast_to(a[0:1, :], afirst_ref.shape)
        dxcn_ref[...] = dxc

    vec = _full((1, d_b))
    mat = _full((d_b, d_b))

    def cur(j):
        return pl.BlockSpec((R, d_b), lambda i, j=j: (nr - 1 - i, j))

    def prev(j):
        return pl.BlockSpec((R, d_b), lambda i, j=j: (jnp.maximum(nr - 2 - i, 0), j))

    return pl.pallas_call(
        body, name="lru_bwd", grid=(nr,),
        in_specs=[cur(jb), prev(jb), cur(jb + 1), cur(0), prev(0), cur(0),
                  _full((CONV_WIDTH, d_b)), vec, mat, vec, mat, vec, vec, vec, mat, ANY_SPEC],
        out_specs=(pl.BlockSpec((R, 2 * d_b), lambda i: (nr - 1 - i, col0 // (2 * d_b))), _full((CONV_WIDTH, d_b)), vec, mat, vec, mat, vec, vec, vec),
        out_shape=(_sds(dz_buf.shape, f32), _sds((CONV_WIDTH, d_b), f32), _sds((1, d_b), f32), _sds((d_b, d_b), f32),
                   _sds((1, d_b), f32), _sds((d_b, d_b), f32), _sds((1, d_b), f32), _sds((1, d_b), f32), _sds((1, d_b), f32)),
        scratch_shapes=[pltpu.VMEM((8, d_b), f32), pltpu.VMEM((8, d_b), f32), pltpu.VMEM((R, d_b), f32)],
        input_output_aliases={15: 0},
        compiler_params=_cparams(1),
    )(z, z, z, hsave, hsave, dout, cw, cb, wa, ba, wx, bx, lam, gain, avg, dz_buf)


def _sgu_chunk(u_in, v_in, w, bexp, gain, avg, n_groups):
    C, d_c = u_in.shape
    gd = d_c // n_groups
    u = jax.nn.gelu(u_in)
    v = jax.nn.gelu(v_in)
    mu = jnp.dot(v, avg, precision=HI, preferred_element_type=f32)
    vc = v - mu
    var = jnp.dot(vc * vc, avg, precision=HI, preferred_element_type=f32)
    vh = (vc * lax.rsqrt(var + EPS)).astype(bf16)
    lane = lax.broadcasted_iota(jnp.int32, (1, d_c), 1)
    causal = lax.broadcasted_iota(jnp.int32, (C, C), 0) >= lax.broadcasted_iota(jnp.int32, (C, C), 1)
    zz = bexp
    for g in range(n_groups):
        wg = jnp.where(causal, w[g], 0.0).astype(bf16)
        zz = zz + jnp.where((lane >= g * gd) & (lane < (g + 1) * gd), _dot(wg, vh), 0.0)
    y = u * zz
    ms = jnp.dot(y * y, avg, precision=HI, preferred_element_type=f32)
    return y * lax.rsqrt(ms + EPS) * gain


def _sgu_fwd(z, col0, d_c, w, bexp, gain, avg):
    T = z.shape[0]
    C = C_CHUNK
    jb = col0 // d_c
    G = w.shape[0]

    def body(u_ref, v_ref, w_ref, b_ref, gain_ref, avg_ref, out_ref):
        out_ref[...] = _sgu_chunk(u_ref[...], v_ref[...], w_ref[...], b_ref[...], gain_ref[...], avg_ref[...], G).astype(bf16)

    return pl.pallas_call(
        body, name="sgu_fwd", grid=(T // C,),
        in_specs=[pl.BlockSpec((C, d_c), lambda i: (i, jb)), pl.BlockSpec((C, d_c), lambda i: (i, jb + 1)),
                  _full((G, C, C)), _full((C, d_c)), _full((1, d_c)), _full((d_c, d_c))],
        out_specs=pl.BlockSpec((C, d_c), lambda i: (i, 0)),
        out_shape=_sds((T, d_c), bf16),
        compiler_params=_cparams(1),
    )(z, z, w, bexp, gain, avg)


def _sgu_bwd(z, col0, d_c, dout, dz_buf, w, bexp, gain, avg):
    T = z.shape[0]
    C = C_CHUNK
    nc = T // C
    jb = col0 // d_c
    G = w.shape[0]
    gd = d_c // G

    def body(u_ref, v_ref, do_ref, w_ref, b_ref, gain_ref, avg_ref, dzin_ref, dz_ref, dw_ref, db_ref, dgain_ref, dbexp_ref):
        step = pl.program_id(0)

        @pl.when(step == 0)
        def _():
            dw_ref[...] = jnp.zeros_like(dw_ref)
            dgain_ref[...] = jnp.zeros_like(dgain_ref)
            dbexp_ref[...] = jnp.zeros_like(dbexp_ref)

        avg = avg_ref[...]
        _, vjp = jax.vjp(lambda a, b, c, d, e: _sgu_chunk(a, b, c, d, e, avg, G),
                         u_ref[...], v_ref[...], w_ref[...], b_ref[...], gain_ref[...])
        du, dv, dw, dbexp, dgain = vjp(do_ref[...])
        dz_ref[:, 0:d_c] = du
        dz_ref[:, d_c:2 * d_c] = dv
        dw_ref[...] += dw
        dbexp_ref[...] += dbexp
        dgain_ref[...] += dgain

        @pl.when(step == nc - 1)
        def _():
            lane = lax.broadcasted_iota(jnp.int32, (1, d_c), 1)
            acc = dbexp_ref[...]
            for g in range(G):
                sel = jnp.where((lane >= g * gd) & (lane < (g + 1) * gd), acc, 0.0)
                db_ref[:, g:g + 1] = jnp.sum(sel, axis=1, keepdims=True)

    return pl.pallas_call(
        body, name="sgu_bwd", grid=(nc,),
        in_specs=[pl.BlockSpec((C, d_c), lambda i: (i, jb)), pl.BlockSpec((C, d_c), lambda i: (i, jb + 1)),
                  pl.BlockSpec((C, d_c), lambda i: (i, 0)),
                  _full((G, C, C)), _full((C, d_c)), _full((1, d_c)), _full((d_c, d_c)), ANY_SPEC],
        out_specs=(pl.BlockSpec((C, 2 * d_c), lambda i: (i, col0 // (2 * d_c))), _full((G, C, C)), _full((C, G)), _full((1, d_c))),
        out_shape=(_sds(dz_buf.shape, f32), _sds((G, C, C), f32), _sds((C, G), f32), _sds((1, d_c), f32)),
        scratch_shapes=[pltpu.VMEM((C, d_c), f32)],
        input_output_aliases={7: 0},
        compiler_params=_cparams(1),
    )(z, z, dout, w, bexp, gain, avg, dz_buf)


def _loss_head(h, gain, target, tm):
    T, D = h.shape
    nt = T // tm

    def body(h_ref, gain_ref, tgt_ref, dh_ref, loss_ref, dgain_ref):
        @pl.when(pl.program_id(0) == 0)
        def _():
            loss_ref[...] = jnp.zeros_like(loss_ref)
            dgain_ref[...] = jnp.zeros_like(dgain_ref)

        hv = h_ref[...]
        gain_v = gain_ref[...]
        r = lax.rsqrt(jnp.mean(hv * hv, axis=-1, keepdims=True) + EPS)
        xh = hv * r
        e = xh * gain_v - tgt_ref[...]
        loss_ref[...] += 0.5 * jnp.sum(jnp.mean(e * e, axis=-1, keepdims=True), axis=0, keepdims=True)
        dy = e * (1.0 / D)
        dgain_ref[...] += jnp.sum(dy * xh, axis=0, keepdims=True)
        dxh = dy * gain_v
        dh_ref[...] = r * (dxh - xh * jnp.mean(dxh * xh, axis=-1, keepdims=True))

    tok = pl.BlockSpec((tm, D), lambda i: (i, 0))
    return pl.pallas_call(
        body, name="loss_head", grid=(nt,),
        in_specs=[tok, _full((1, D)), tok],
        out_specs=(tok, _full((1, 128)), _full((1, D))),
        out_shape=(_sds((T, D), f32), _sds((1, 128), f32), _sds((1, D), f32)),
        compiler_params=_cparams(1),
    )(h, gain, target)


def _lower_bounds_fn(logits):
    n = logits.shape[0]
    mx = jnp.max(logits, axis=0, keepdims=True)
    ex = jnp.exp(logits - mx)
    soft = ex / jnp.sum(ex, axis=0, keepdims=True)
    rows = [jnp.zeros_like(soft[0:1])]
    for l in range(1, n):
        rows.append(rows[-1] + soft[l:l + 1])
    return jnp.concatenate(rows, axis=0)


def _lower_bounds(logits):
    def body(x_ref, o_ref):
        o_ref[...] = _lower_bounds_fn(x_ref[...])

    return pl.pallas_call(body, name="lower_bounds", out_shape=_sds(logits.shape, f32))(logits)


def _lower_bounds_bwd(logits, dlb):
    def body(x_ref, d_ref, o_ref):
        _, vjp = jax.vjp(_lower_bounds_fn, x_ref[...])
        o_ref[...] = vjp(d_ref[...])[0]

    return pl.pallas_call(body, name="lower_bounds_bwd", out_shape=_sds(logits.shape, f32))(logits, dlb)


def _adamw(w, g, m, v, rows_blk):
    R, Cc = w.shape
    rb = R if R <= rows_blk else math.gcd(R, rows_blk)

    def body(w_ref, g_ref, m_ref, v_ref, d_ref, nm_ref, nv_ref):
        gv = g_ref[...]
        m2 = ADAM_B1 * m_ref[...] + (1.0 - ADAM_B1) * gv
        v2 = ADAM_B2 * v_ref[...] + (1.0 - ADAM_B2) * (gv * gv)
        m_hat = m2 / (1.0 - ADAM_B1 ** ADAM_STEP)
        v_hat = v2 / (1.0 - ADAM_B2 ** ADAM_STEP)
        d_ref[...] = -ADAM_LR * (m_hat / (jnp.sqrt(v_hat) + ADAM_EPS) + ADAM_WD * w_ref[...])
        nm_ref[...] = m2
        nv_ref[...] = v2

    spec = pl.BlockSpec((rb, Cc), lambda i: (i, 0))
    return pl.pallas_call(
        body, name="adamw", grid=(R // rb,),
        in_specs=[spec] * 4, out_specs=(spec,) * 3, out_shape=(_sds((R, Cc), f32),) * 3,
        compiler_params=_cparams(1),
    )(w, g, m, v)


def _pair_sum(grads, recv, c_arr):
    res = []
    for g, r in zip(grads, recv):
        nsh, R, Cc = g.shape
        r2 = R // 2

        def body(c_ref, g_ref, r_ref, o_ref):
            o_ref[...] = (g_ref[...].astype(f32) + r_ref[...].astype(f32)).astype(bf16)

        gs = pltpu.PrefetchScalarGridSpec(
            num_scalar_prefetch=1, grid=(nsh,),
            in_specs=[pl.BlockSpec((None, r2, Cc), lambda s, c: (s, c[0], 0)),
                      pl.BlockSpec((None, r2, Cc), lambda s, c: (s, 0, 0))],
            out_specs=pl.BlockSpec((None, r2, Cc), lambda s, c: (s, 0, 0)))
        res.append(pl.pallas_call(body, name="pair_sum", grid_spec=gs, out_shape=_sds((nsh, r2, Cc), bf16),
                                  compiler_params=_cparams(1))(c_arr, g, r))
    return res


def _add(a, b):
    def body(a_ref, b_ref, o_ref):
        o_ref[...] = a_ref[...] + b_ref[...]

    return pl.pallas_call(body, name="pair_sum_small", out_shape=_sds(a.shape, f32))(a, b)


def _chip_sum(hsum, recv, bufs, slot_arr, c_arr, layer, n_layers):
    res = []
    for a in range(len(hsum)):
        hh, r = hsum[a], recv[a]
        nsh, r2, Cc = hh.shape
        prev = [bufs[a]] if bufs else []

        def body(s_ref, c_ref, h_ref, r_ref, *rest):
            o_ref = rest[-1]
            acc = h_ref[...].astype(f32)
            for j in range(N_CHIPS - 1):
                acc = acc + r_ref[j].astype(f32)
            o_ref[...] = acc

        gs = pltpu.PrefetchScalarGridSpec(
            num_scalar_prefetch=2, grid=(1,),
            in_specs=[pl.BlockSpec((None, r2, Cc), lambda i, s, c: (s[0], 0, 0)),
                      pl.BlockSpec((N_CHIPS - 1, r2, Cc), lambda i, s, c: (0, 0, 0))] + [ANY_SPEC] * len(prev),
            out_specs=pl.BlockSpec((None, r2, Cc), lambda i, s, c: (layer, c[0], 0)))
        res.append(pl.pallas_call(body, name="chip_sum", grid_spec=gs, out_shape=_sds((n_layers, 2 * r2, Cc), f32),
                                  input_output_aliases={4: 0} if prev else {},
                                  compiler_params=_cparams(1))(slot_arr, c_arr, hh, r, *prev))
    return res


def _sum_slots(x):
    def body(x_ref, o_ref):
        acc = x_ref[0]
        for j in range(1, x.shape[0]):
            acc = acc + x_ref[j]
        o_ref[...] = acc

    return pl.pallas_call(body, name="sum_slots", out_shape=_sds(x.shape[1:], f32))(x)


def _blockdiag(w):
    nb, bd, _ = w.shape
    eye = jnp.eye(nb, dtype=w.dtype)
    return (eye[:, None, :, None] * w[:, :, None, :]).reshape(nb * bd, nb * bd)


def _blockdiag_extract(dense, nb):
    bd = dense.shape[0] // nb
    d4 = dense.reshape(nb, bd, nb, bd)
    return jnp.stack([d4[i, :, i, :] for i in range(nb)])


def _pack(arrays, multiple):
    flat = jnp.concatenate([a.reshape(-1).astype(f32) for a in arrays])
    pad = (-flat.shape[0]) % multiple
    return jnp.pad(flat, (0, pad))


def _unpack(flat, shapes):
    out, off = [], 0
    for s in shapes:
        n = int(np.prod(s))
        out.append(flat[off:off + n].reshape(s))
        off += n
    return out


BIG = ("ffn1_wg", "ffn1_wu", "ffn1_wd", "w_in", "w_out", "ffn2_wg", "ffn2_wu", "ffn2_wd")
SMALL = ("ffn1_norm", "mix_norm", "hgrn_lb_logits", "hgrn_norm", "conv_w", "conv_b", "lru_wa", "lru_ba", "lru_wx",
         "lru_bx", "lru_lambda", "lru_norm", "sgu_w", "sgu_b", "sgu_norm", "ffn2_norm", "final_norm")
WEIGHTS = ("ffn1_norm", "ffn1_wg", "ffn1_wu", "ffn1_wd", "mix_norm", "w_in", "hgrn_lb_logits", "hgrn_norm", "conv_w",
           "conv_b", "lru_wa", "lru_ba", "lru_wx", "lru_bx", "lru_lambda", "lru_norm", "sgu_w", "sgu_b", "sgu_norm",
           "w_out", "ffn2_norm", "ffn2_wg", "ffn2_wu", "ffn2_wd", "final_norm")


def kernel(x, ffn1_norm, ffn1_wg, ffn1_wu, ffn1_wd, mix_norm, w_in, hgrn_lb_logits, hgrn_norm, conv_w, conv_b, lru_wa, lru_ba, lru_wx, lru_bx, lru_lambda, lru_norm, sgu_w, sgu_b, sgu_norm, w_out, ffn2_norm, ffn2_wg, ffn2_wu, ffn2_wd, final_norm, loss_target, m_ffn1_norm, m_ffn1_wg, m_ffn1_wu, m_ffn1_wd, m_mix_norm, m_w_in, m_hgrn_lb_logits, m_hgrn_norm, m_conv_w, m_conv_b, m_lru_wa, m_lru_ba, m_lru_wx, m_lru_bx, m_lru_lambda, m_lru_norm, m_sgu_w, m_sgu_b, m_sgu_norm, m_w_out, m_ffn2_norm, m_ffn2_wg, m_ffn2_wu, m_ffn2_wd, m_final_norm, v_ffn1_norm, v_ffn1_wg, v_ffn1_wu, v_ffn1_wd, v_mix_norm, v_w_in, v_hgrn_lb_logits, v_hgrn_norm, v_conv_w, v_conv_b, v_lru_wa, v_lru_ba, v_lru_wx, v_lru_bx, v_lru_lambda, v_lru_norm, v_sgu_w, v_sgu_b, v_sgu_norm, v_w_out, v_ffn2_norm, v_ffn2_wg, v_ffn2_wu, v_ffn2_wd, v_final_norm):
    args = dict(locals())
    W = {n: args[n] for n in WEIGHTS}
    M = {n: args["m_" + n] for n in WEIGHTS}
    V = {n: args["v_" + n] for n in WEIGHTS}

    T, D = x.shape[1], x.shape[2]
    L = ffn1_norm.shape[0]
    d_a, d_b, d_c = hgrn_norm.shape[1], lru_norm.shape[1], sgu_norm.shape[1]
    col_b, col_c = 4 * d_a, 4 * d_a + 2 * d_b
    tm = 512 if T % 512 == 0 else T
    my_c = lax.axis_index("c")
    my_slot = 2 * lax.axis_index("x") + lax.axis_index("y")
    c_arr = jnp.reshape(my_c, (1,)).astype(jnp.int32)
    slot_arr = jnp.reshape(my_slot, (1,)).astype(jnp.int32)

    nb = len(BIG)
    shards = {n: W[n].astype(bf16) for n in BIG}
    gplan = _gather_ici_plan(nb)

    def layer_shards(l):
        return [shards[n][l] for n in BIG]

    sh0 = layer_shards(0) + [conv_w]
    got = _exchange("gather0_ici", sh0, [_sds((N_CHIPS,) + s.shape, s.dtype) for s in sh0], _gather_ici_plan(nb + 1))
    got = _gather_d2d("gather0_d2d", sh0, got)
    G = [None] * L
    G[0] = dict(zip(BIG, got[:nb]))
    conv_full = jnp.transpose(got[nb], (1, 2, 0, 3)).reshape(L, CONV_WIDTH, d_b)

    def start_gather(l, after):
        sh = layer_shards(l)
        lands = [lax.empty((N_CHIPS,) + s.shape, bf16) for s in sh]
        return _start_copies(f"gather_start_{l}", sh, lands, gplan, after)

    def finish_gather(l, pending, after):
        send, recv, sh, lands, _ = pending
        lands = _wait_copies(f"gather_wait_{l}", send, recv, sh, lands, gplan, after)
        return dict(zip(BIG, _gather_d2d("gather_d2d", sh, lands)))

    lb = _lower_bounds(hgrn_lb_logits)
    avg_b = _group_avg_matrix(d_b, d_b // B_BLOCKS)
    avg_c = _group_avg_matrix(d_c, d_c // C_GROUPS)
    wa_dense = [_blockdiag(lru_wa[l]) for l in range(L)]
    wx_dense = [_blockdiag(lru_wx[l]) for l in range(L)]
    bexp = [jnp.repeat(sgu_b[l].T, d_c // C_GROUPS, axis=1) for l in range(L)]

    def lru_params(l):
        return (conv_full[l], conv_b[l][None], wa_dense[l], lru_ba[l].reshape(1, d_b), wx_dense[l],
                lru_bx[l].reshape(1, d_b), lru_lambda[l][None], lru_norm[l][None], avg_b)

    h = x.reshape(T, D)
    saved = []
    for l in range(L):
        s = {"h0": h}
        gain1 = ffn1_norm[l][None]
        pending = None
        if l + 1 < L:
            pending = start_gather(l + 1, G[0]["ffn1_wg"] if l == 0 else h)
            gain1 = gain1 + pending[4][0:1, 0:1]
        g = G[l]
        h, s["g1"], s["u1"] = _ffn_fwd(h, gain1, g["ffn1_wg"], g["ffn1_wu"], g["ffn1_wd"], tm)
        s["h1"] = h
        z = _proj_in_fwd(h, mix_norm[l][None], g["w_in"], tm)
        s["z"] = z
        s["oa"], s["states"] = _hgrn_fwd(z, lb[l][None], hgrn_norm[l][None], d_a)
        s["ob"], s["hl"] = _lru_fwd(z, col_b, d_b, *lru_params(l))
        s["oc"] = _sgu_fwd(z, col_c, d_c, sgu_w[l], bexp[l], sgu_norm[l][None], avg_c)
        h = _proj_out_fwd(h, s["oa"], s["ob"], s["oc"], g["w_out"], tm)
        s["h2"] = h
        h, s["g2"], s["u2"] = _ffn_fwd(h, ffn2_norm[l][None], g["ffn2_wg"], g["ffn2_wu"], g["ffn2_wd"], tm)
        saved.append(s)
        if pending is not None:
            G[l + 1] = finish_gather(l + 1, pending, h)

    dh, loss_part, d_final = _loss_head(h, final_norm[None], loss_target.reshape(T, D), tm)
    loss = lax.psum(loss_part[0, 0], ("x", "y", "c"))

    def pair_views(n_big):
        r = [(lambda i, o, p, a=a: i[a].at[:, pl.ds((1 - p.c) * (i[a].shape[1] // 2), i[a].shape[1] // 2)],
              lambda i, o, p, a=a: o[a], "sib") for a in range(n_big)]
        return r

    chip_plan = [(lambda s_, o, p, a=a, kind=kind: s_[a].at[p.peer_slot(kind)], lambda s_, o, p, a=a, j=j: o[a].at[j], kind)
                 for a in range(nb) for j, kind in enumerate(CHIP_KINDS)]

    def share(l, sbufs, extra_in=(), extra_out=(), extra_remote=(), extra_local=()):
        remote = [(lambda i, o, p, a=a: _half(o[a].at[l], p.c), lambda i, o, p, a=a: _half(o[a].at[l], p.c), "sib")
                  for a in range(nb)]
        outs = [_sds(sa.shape, f32) for sa in sbufs] + list(extra_out)
        return _exchange("grad_share_d2d", list(sbufs) + list(extra_in), outs, remote + list(extra_remote),
                         list(extra_local), aliases={a: a for a in range(nb)})

    small = {n: [None] * L for n in SMALL if n != "final_norm"}
    sbufs, chip_pending = [], None
    for l in reversed(range(L)):
        s, g = saved[l], G[l]
        gain2 = ffn2_norm[l][None]
        if chip_pending is not None:
            gain2 = gain2 + chip_pending[0][4][0:1, 0:1]
        dh, small["ffn2_norm"][l], dg, du, xn, dob = _ffn_bwd_dgrad(
            s["h2"], gain2, dh, s["g2"], s["u2"], g["ffn2_wg"], g["ffn2_wu"], g["ffn2_wd"], tm)
        dwg2, dwu2, dwd2 = _ffn_bwd_wgrad(xn, dob, s["g2"], s["u2"], dg, du, tm)
        doa, dob_, doc, dwo = _proj_out_bwd(dh, s["oa"], s["ob"], s["oc"], g["w_out"], tm)
        dz, small["hgrn_lb_logits"][l], small["hgrn_norm"][l] = _hgrn_bwd(
            s["z"], lb[l][None], hgrn_norm[l][None], s["states"], doa, d_a)
        (dz, small["conv_w"][l], small["conv_b"][l], dwa, small["lru_ba"][l], dwx, small["lru_bx"][l],
         small["lru_lambda"][l], small["lru_norm"][l]) = _lru_bwd(s["z"], col_b, d_b, s["hl"], dob_, dz, *lru_params(l))
        small["lru_wa"][l] = _blockdiag_extract(dwa, B_BLOCKS)
        small["lru_wx"][l] = _blockdiag_extract(dwx, B_BLOCKS)
        dz, small["sgu_w"][l], dsb, small["sgu_norm"][l] = _sgu_bwd(
            s["z"], col_c, d_c, doc, dz, sgu_w[l], bexp[l], sgu_norm[l][None], avg_c)
        small["sgu_b"][l] = dsb.T
        dh, small["mix_norm"][l], xn = _proj_in_bwd_dgrad(s["h1"], mix_norm[l][None], dh, dz, g["w_in"], tm)
        dwi = _proj_in_bwd_wgrad(xn, dz, N_CHIPS, tm)
        dh, small["ffn1_norm"][l], dg, du, xn, dob = _ffn_bwd_dgrad(
            s["h0"], ffn1_norm[l][None], dh, s["g1"], s["u1"], g["ffn1_wg"], g["ffn1_wu"], g["ffn1_wd"], tm)
        dwg1, dwu1, dwd1 = _ffn_bwd_wgrad(xn, dob, s["g1"], s["u1"], dg, du, tm)
        layer_grads = [dwg1, dwu1, dwd1, dwi, dwo, dwg2, dwu2, dwd2]

        if chip_pending is not None:
            (send, recv, hs, lands, _), hsum_prev = chip_pending
            lands = _wait_copies(f"grad_chip_wait_{l + 1}", send, recv, hs, lands, chip_plan, dwg1)
            sbufs = _chip_sum(hsum_prev, lands, sbufs, slot_arr, c_arr, l + 1, L)
            sbufs = share(l + 1, sbufs)
            chip_pending = None
        if l > 0:
            outs = [_sds((N_CHIPS, gr.shape[1] // 2, gr.shape[2]), bf16) for gr in layer_grads]
            recv_a = _exchange("grad_pair_d2d", layer_grads, outs, pair_views(nb))
            hsum = _pair_sum(layer_grads, recv_a, c_arr)
            lands = [lax.empty((N_CHIPS - 1,) + hh.shape[1:], bf16) for hh in hsum]
            chip_pending = (_start_copies(f"grad_chip_start_{l}", hsum, lands, chip_plan, hsum[0]), hsum)
    grad_x = dh.reshape(x.shape)

    small_names = [n for n in SMALL]
    small_parts = [jnp.stack([jnp.reshape(v, (-1,)) for v in small[n]]) if n != "final_norm" else d_final for n in small_names]
    small_shapes = [p.shape for p in small_parts]
    packed = _pack(small_parts, 2 * 8 * 128).reshape(2, -1, 128)
    n_rows = packed.shape[1]
    outs = [_sds((N_CHIPS, gr.shape[1] // 2, gr.shape[2]), bf16) for gr in layer_grads] + [_sds(packed.shape, f32)]
    recv_a = _exchange("grad_pair_d2d_last", layer_grads + [packed], outs,
                       pair_views(nb) + [(lambda i, o, p: i[nb], lambda i, o, p: o[nb], "sib")])
    hsum = _pair_sum(layer_grads, recv_a[:nb], c_arr)
    small_pair = _add(packed, recv_a[nb])
    remote = list(chip_plan) + [(lambda i, o, p: i[nb].at[p.c], lambda i, o, p: o[nb].at[p.slot], kind) for kind in CHIP_KINDS]
    local = [(lambda i, o, p: i[nb].at[p.c], lambda i, o, p: o[nb].at[p.slot])]
    outs = [_sds((N_CHIPS - 1,) + hh.shape[1:], bf16) for hh in hsum] + [_sds((N_CHIPS, n_rows, 128), f32)]
    recv_b = _exchange("grad_chip_ici_last", hsum + [small_pair], outs, remote, local)
    sbufs = _chip_sum(hsum, recv_b[:nb], sbufs, slot_arr, c_arr, 0, L)
    small_half = _sum_slots(recv_b[nb])
    final = share(0, sbufs, extra_in=[small_half], extra_out=[_sds(packed.shape, f32)],
                  extra_remote=[(lambda i, o, p: i[nb], lambda i, o, p: o[nb].at[p.c], "sib")],
                  extra_local=[(lambda i, o, p: i[nb], lambda i, o, p: o[nb].at[p.c])])
    grads = {n: final[a].reshape(W[n].shape) for a, n in enumerate(BIG)}
    small_tot = _unpack(final[nb].reshape(-1), small_shapes)
    for n, val in zip(small_names, small_tot):
        grads[n] = val
    grads["hgrn_lb_logits"] = _lower_bounds_bwd(hgrn_lb_logits, grads["hgrn_lb_logits"])
    shard_cols = conv_w.shape[2]
    grads["conv_w"] = lax.dynamic_slice_in_dim(grads["conv_w"].reshape(L, CONV_WIDTH, d_b), my_slot * shard_cols, shard_cols, axis=2)
    for n in SMALL:
        grads[n] = grads[n].reshape(W[n].shape)

    delta, new_m, new_v = {}, {}, {}
    for n in BIG:
        cols = W[n].shape[-1]
        d2, m2, v2 = _adamw(W[n].reshape(-1, cols), grads[n].reshape(-1, cols), M[n].reshape(-1, cols), V[n].reshape(-1, cols), 512)
        delta[n], new_m[n], new_v[n] = d2.reshape(W[n].shape), m2.reshape(W[n].shape), v2.reshape(W[n].shape)
    shapes = [W[n].shape for n in SMALL]
    packs = [_pack([src[n] for n in SMALL], 8 * 128).reshape(-1, 128) for src in (W, grads, M, V)]
    d2, m2, v2 = _adamw(*packs, 4096)
    for dst, val in ((delta, d2), (new_m, m2), (new_v, v2)):
        for n, piece in zip(SMALL, _unpack(val.reshape(-1), shapes)):
            dst[n] = piece

    return (loss, grad_x, *[grads[n] for n in WEIGHTS], *[delta[n] for n in WEIGHTS],
            *[new_m[n] for n in WEIGHTS], *[new_v[n] for n in WEIGHTS])
```

```python
import math

import numpy as np
import jax
import jax.numpy as jnp
from jax import lax
from jax.experimental import pallas as pl
from jax.experimental.pallas import tpu as pltpu

f32 = jnp.float32
bf16 = jnp.bfloat16
HI = lax.Precision.HIGHEST
MESH = pl.DeviceIdType.MESH

EPS = 1e-6
HEAD = 128
A_CHUNK = 64
A_SUB = 16
B_BLOCKS = 4
B_CHUNK = 256
CONV_WIDTH = 4
LRU_C = 8.0
C_GROUPS = 4
C_CHUNK = 128
N_CHIPS = 4
ADAM_LR, ADAM_B1, ADAM_B2, ADAM_EPS, ADAM_WD, ADAM_STEP = 0.001, 0.9, 0.999, 1e-08, 0.01, 10
VMEM_LIMIT = 56 * 1024 * 1024


def _cparams(n_axes):
    return pltpu.CompilerParams(dimension_semantics=("arbitrary",) * n_axes, vmem_limit_bytes=VMEM_LIMIT)


def _sds(shape, dtype):
    return jax.ShapeDtypeStruct(tuple(shape), dtype)


def _full(shape):
    n = len(shape)
    return pl.BlockSpec(tuple(shape), lambda *_: (0,) * n)


def _dot(a, b):
    return jnp.dot(a, b, preferred_element_type=f32)


def _dot_nt(a, b):
    return lax.dot_general(a, b, (((1,), (1,)), ((), ())), preferred_element_type=f32)


def _dot_tn(a, b):
    return lax.dot_general(a, b, (((0,), (0,)), ((), ())), preferred_element_type=f32)


def _silu(x):
    return x * jax.nn.sigmoid(x)


def _group_avg_matrix(n, group):
    idx = np.arange(n) // group
    return jnp.asarray((idx[:, None] == idx[None, :]).astype(np.float32) / group)


class _Place:
    def __init__(self):
        self.x, self.y, self.c = lax.axis_index("x"), lax.axis_index("y"), lax.axis_index("c")
        self.slot = 2 * self.x + self.y

    def peer(self, kind):
        x, y, c = self.x, self.y, self.c
        return {"sib": (x, y, 1 - c), "fx": (1 - x, y, c), "fy": (x, 1 - y, c), "fxy": (1 - x, 1 - y, c)}[kind]

    def peer_slot(self, kind):
        x, y = self.x, self.y
        return {"fx": 2 * (1 - x) + y, "fy": 2 * x + (1 - y), "fxy": 2 * (1 - x) + (1 - y)}[kind]


CHIP_KINDS = ("fx", "fy", "fxy")


def _exchange(name, ins, outs, remote, local=(), aliases=None):
    n_in, n_out, n_r, n_l = len(ins), len(outs), len(remote), len(local)

    def body(*refs):
        in_refs, out_refs = refs[:n_in], refs[n_in:n_in + n_out]
        send, recv, lsem = refs[n_in + n_out:]
        p = _Place()
        lcopies = []
        for t, (src, dst) in enumerate(local):
            cp = pltpu.make_async_copy(src(in_refs, out_refs, p), dst(in_refs, out_refs, p), lsem.at[t])
            cp.start()
            lcopies.append(cp)
        copies = []
        for t, (src, dst, kind) in enumerate(remote):
            cp = pltpu.make_async_remote_copy(
                src_ref=src(in_refs, out_refs, p), dst_ref=dst(in_refs, out_refs, p),
                send_sem=send.at[t], recv_sem=recv.at[t], device_id=p.peer(kind), device_id_type=MESH)
            cp.start()
            copies.append(cp)
        for cp in copies:
            cp.wait_recv()
        for cp in copies:
            cp.wait_send()
        for cp in lcopies:
            cp.wait()

    anyspec = pl.BlockSpec(memory_space=pl.ANY)
    res = pl.pallas_call(
        body, name=name, out_shape=tuple(outs),
        in_specs=[anyspec] * n_in, out_specs=tuple([anyspec] * n_out),
        scratch_shapes=[pltpu.SemaphoreType.DMA((n_r,)), pltpu.SemaphoreType.DMA((n_r,)),
                        pltpu.SemaphoreType.DMA((max(n_l, 1),))],
        input_output_aliases=aliases or {},
        compiler_params=pltpu.CompilerParams(has_side_effects=True),
    )(*ins)
    return list(res)


HBM_SPEC = pl.BlockSpec(memory_space=pltpu.HBM)
SEM_SPEC = pl.BlockSpec(memory_space=pltpu.SEMAPHORE)
ANY_SPEC = pl.BlockSpec(memory_space=pl.ANY)
DATAFLOW = pltpu.SideEffectType.DATAFLOW_SIDE_EFFECTING


def _in_hbm(a):
    return pltpu.with_memory_space_constraint(a, pltpu.HBM)


def _start_copies(name, srcs, lands, remote, after):
    n_s, n_l, n_r = len(srcs), len(lands), len(remote)

    def body(*refs):
        src_refs, land_refs = refs[:n_s], refs[n_s:n_s + n_l]
        send, recv = refs[n_s + n_l + 1], refs[n_s + n_l + 2]
        token = refs[-1]
        p = _Place()
        for t, (src, dst, kind) in enumerate(remote):
            pltpu.make_async_remote_copy(
                src_ref=src(src_refs, land_refs, p), dst_ref=dst(src_refs, land_refs, p),
                send_sem=send.at[t], recv_sem=recv.at[t], device_id=p.peer(kind), device_id_type=MESH).start()
        token[...] = jnp.zeros_like(token)

    thru = [pltpu.HBM(a.shape, a.dtype) for a in list(srcs) + list(lands)]
    res = pl.pallas_call(
        body, name=name,
        out_shape=(pltpu.SemaphoreType.DMA((n_r,)), pltpu.SemaphoreType.DMA((n_r,)), *thru, _sds((8, 128), f32)),
        in_specs=[HBM_SPEC] * (n_s + n_l) + [ANY_SPEC],
        out_specs=(SEM_SPEC, SEM_SPEC, *([HBM_SPEC] * (n_s + n_l)), pl.BlockSpec(memory_space=pltpu.VMEM)),
        input_output_aliases={i: 2 + i for i in range(n_s + n_l)},
        compiler_params=pltpu.CompilerParams(has_side_effects=DATAFLOW),
    )(*[_in_hbm(a) for a in srcs], *[_in_hbm(a) for a in lands], after)
    return res[0], res[1], list(res[2:2 + n_s]), list(res[2 + n_s:2 + n_s + n_l]), res[-1]


def _wait_copies(name, send, recv, srcs, lands, remote, after):
    n_s, n_l = len(srcs), len(lands)

    def body(*refs):
        src_refs, land_refs = refs[:n_s], refs[n_s:n_s + n_l]
        send_ref, recv_ref = refs[n_s + n_l], refs[n_s + n_l + 1]
        p = _Place()
        for t, (src, dst, kind) in enumerate(remote):
            cp = pltpu.make_async_remote_copy(
                src_ref=src(src_refs, land_refs, p), dst_ref=dst(src_refs, land_refs, p),
                send_sem=send_ref.at[t], recv_sem=recv_ref.at[t], device_id=p.peer(kind), device_id_type=MESH)
            cp.wait_send()
            cp.wait_recv()

    thru = [pltpu.HBM(a.shape, a.dtype) for a in list(srcs) + list(lands)]
    res = pl.pallas_call(
        body, name=name, out_shape=tuple(thru),
        in_specs=[HBM_SPEC] * (n_s + n_l) + [SEM_SPEC, SEM_SPEC, ANY_SPEC],
        out_specs=tuple([HBM_SPEC] * (n_s + n_l)),
        input_output_aliases={i: i for i in range(n_s + n_l)},
        compiler_params=pltpu.CompilerParams(has_side_effects=DATAFLOW),
    )(*srcs, *lands, send, recv, after)
    return list(res[n_s:])


def _half(ref, c):
    n2 = ref.shape[0] // 2
    return ref.at[pl.ds(c * n2, n2)]


def _gather_ici_plan(n):
    def view(a):
        return lambda s, o, p: _half(o[a].at[p.slot], p.c)

    return [(view(a), view(a), kind) for a in range(n) for kind in CHIP_KINDS]


def _gather_d2d(name, lands):
    n = len(lands)
    remote = []
    for a in range(n):
        for kind in CHIP_KINDS:
            view = lambda i, o, p, a=a, kind=kind: _half(o[a].at[p.peer_slot(kind)], p.c)
            remote.append((view, view, "sib"))
    outs = [_sds(g.shape, g.dtype) for g in lands]
    return _exchange(name, list(lands), outs, remote, aliases={a: a for a in range(n)})


def _cast_place(weights, layer, slot_arr, n_steps=4):
    def body(s_ref, *refs):
        n = len(refs) // 2
        for a in range(n):
            refs[n + a][...] = refs[a][...].astype(bf16)

    in_specs, out_specs, out_shape = [], [], []
    for w in weights:
        _, R, Cc = w.shape
        rt = R // n_steps
        in_specs.append(pl.BlockSpec((None, rt, Cc), lambda i, s: (layer, i, 0)))
        out_specs.append(pl.BlockSpec((None, rt, Cc), lambda i, s: (s[0], i, 0)))
        out_shape.append(_sds((N_CHIPS, R, Cc), bf16))
    gs = pltpu.PrefetchScalarGridSpec(num_scalar_prefetch=1, grid=(n_steps,), in_specs=in_specs, out_specs=tuple(out_specs))
    return list(pl.pallas_call(body, name="cast_place", grid_spec=gs, out_shape=tuple(out_shape),
                               compiler_params=_cparams(1))(slot_arr, *weights))


def _ffn_fwd(h, gain, wg, wu, wd, tm):
    T, D = h.shape
    nsh, F = wg.shape[0], wg.shape[2]
    nt = T // tm

    def body(h_ref, gain_ref, wg_ref, wu_ref, wd_ref, out_ref, gs_ref, us_ref, xn_ref, acc_ref):
        k = pl.program_id(1)

        @pl.when(k == 0)
        def _():
            hv = h_ref[...]
            r = lax.rsqrt(jnp.mean(hv * hv, axis=-1, keepdims=True) + EPS)
            xn_ref[...] = (hv * r * gain_ref[...]).astype(bf16)
            acc_ref[...] = jnp.zeros_like(acc_ref)

        xn = xn_ref[...]
        g = _dot(xn, wg_ref[...])
        u = _dot(xn, wu_ref[...])
        gs_ref[...] = g.astype(bf16)
        us_ref[...] = u.astype(bf16)
        a = (_silu(g) * u).astype(bf16)
        acc_ref[...] += _dot(a, wd_ref[...])

        @pl.when(k == nsh - 1)
        def _():
            out_ref[...] = h_ref[...] + 0.5 * acc_ref[...]

    wspec = pl.BlockSpec((None, D, F), lambda i, k: (k, 0, 0))
    return pl.pallas_call(
        body, name="ffn_fwd", grid=(nt, nsh),
        in_specs=[pl.BlockSpec((tm, D), lambda i, k: (i, 0)), _full((1, D)), wspec, wspec,
                  pl.BlockSpec((None, F, D), lambda i, k: (k, 0, 0))],
        out_specs=(pl.BlockSpec((tm, D), lambda i, k: (i, 0)),
                   pl.BlockSpec((None, tm, F), lambda i, k: (k, i, 0)),
                   pl.BlockSpec((None, tm, F), lambda i, k: (k, i, 0))),
        out_shape=(_sds((T, D), f32), _sds((nsh, T, F), bf16), _sds((nsh, T, F), bf16)),
        scratch_shapes=[pltpu.VMEM((tm, D), bf16), pltpu.VMEM((tm, D), f32)],
        compiler_params=_cparams(2),
    )(h, gain, wg, wu, wd)


def _ffn_bwd_dgrad(h, gain, dout, gs, us, wg, wu, wd, tm):
    T, D = h.shape
    nsh, F = wg.shape[0], wg.shape[2]
    nt = T // tm

    def body(h_ref, gain_ref, dout_ref, gs_ref, us_ref, wg_ref, wu_ref, wd_ref,
             dh_ref, dgain_ref, dg_ref, du_ref, xn_ref, dob_ref, xh_ref, acc_ref):
        i, k = pl.program_id(0), pl.program_id(1)

        @pl.when((i == 0) & (k == 0))
        def _():
            dgain_ref[...] = jnp.zeros_like(dgain_ref)

        @pl.when(k == 0)
        def _():
            hv = h_ref[...]
            r = lax.rsqrt(jnp.mean(hv * hv, axis=-1, keepdims=True) + EPS)
            xh = hv * r
            xh_ref[...] = xh
            xn_ref[...] = (xh * gain_ref[...]).astype(bf16)
            dob_ref[...] = (0.5 * dout_ref[...]).astype(bf16)
            acc_ref[...] = jnp.zeros_like(acc_ref)

        da = _dot_nt(dob_ref[...], wd_ref[...])
        g = gs_ref[...].astype(f32)
        u = us_ref[...].astype(f32)
        sg = jax.nn.sigmoid(g)
        dg = (da * u * (sg * (1.0 + g * (1.0 - sg)))).astype(bf16)
        du = (da * (g * sg)).astype(bf16)
        dg_ref[...] = dg
        du_ref[...] = du
        acc_ref[...] += _dot_nt(dg, wg_ref[...]) + _dot_nt(du, wu_ref[...])

        @pl.when(k == nsh - 1)
        def _():
            hv = h_ref[...]
            r = lax.rsqrt(jnp.mean(hv * hv, axis=-1, keepdims=True) + EPS)
            xh = xh_ref[...]
            dxn = acc_ref[...]
            dgain_ref[...] += jnp.sum(dxn * xh, axis=0, keepdims=True)
            dxh = dxn * gain_ref[...]
            dh_ref[...] = dout_ref[...] + r * (dxh - xh * jnp.mean(dxh * xh, axis=-1, keepdims=True))

    tok = pl.BlockSpec((tm, D), lambda i, k: (i, 0))
    sav = pl.BlockSpec((None, tm, F), lambda i, k: (k, i, 0))
    wspec = pl.BlockSpec((None, D, F), lambda i, k: (k, 0, 0))
    return pl.pallas_call(
        body, name="ffn_bwd_dgrad", grid=(nt, nsh),
        in_specs=[tok, _full((1, D)), tok, sav, sav, wspec, wspec,
                  pl.BlockSpec((None, F, D), lambda i, k: (k, 0, 0))],
        out_specs=(tok, _full((1, D)), sav, sav, tok, tok),
        out_shape=(_sds((T, D), f32), _sds((1, D), f32), _sds((nsh, T, F), bf16), _sds((nsh, T, F), bf16),
                   _sds((T, D), bf16), _sds((T, D), bf16)),
        scratch_shapes=[pltpu.VMEM((tm, D), f32), pltpu.VMEM((tm, D), f32)],
        compiler_params=_cparams(2),
    )(h, gain, dout, gs, us, wg, wu, wd)


def _ffn_bwd_wgrad(xn, dob, gs, us, dg, du, tm):
    T, D = xn.shape
    nsh, F = gs.shape[0], gs.shape[2]
    nt = T // tm

    def body(xn_ref, dob_ref, gs_ref, us_ref, dg_ref, du_ref, dwg_ref, dwu_ref, dwd_ref, ag_ref, au_ref, ad_ref):
        i = pl.program_id(1)

        @pl.when(i == 0)
        def _():
            ag_ref[...] = jnp.zeros_like(ag_ref)
            au_ref[...] = jnp.zeros_like(au_ref)
            ad_ref[...] = jnp.zeros_like(ad_ref)

        xn_v = xn_ref[...]
        ag_ref[...] += _dot_tn(xn_v, dg_ref[...])
        au_ref[...] += _dot_tn(xn_v, du_ref[...])
        g = gs_ref[...].astype(f32)
        a = (_silu(g) * us_ref[...].astype(f32)).astype(bf16)
        ad_ref[...] += _dot_tn(a, dob_ref[...])

        @pl.when(i == nt - 1)
        def _():
            dwg_ref[...] = ag_ref[...].astype(bf16)
            dwu_ref[...] = au_ref[...].astype(bf16)
            dwd_ref[...] = ad_ref[...].astype(bf16)

    tok = pl.BlockSpec((tm, D), lambda k, i: (i, 0))
    sav = pl.BlockSpec((None, tm, F), lambda k, i: (k, i, 0))
    wspec = pl.BlockSpec((None, D, F), lambda k, i: (k, 0, 0))
    wdspec = pl.BlockSpec((None, F, D), lambda k, i: (k, 0, 0))
    return pl.pallas_call(
        body, name="ffn_bwd_wgrad", grid=(nsh, nt),
        in_specs=[tok, tok, sav, sav, sav, sav],
        out_specs=(wspec, wspec, wdspec),
        out_shape=(_sds((nsh, D, F), bf16), _sds((nsh, D, F), bf16), _sds((nsh, F, D), bf16)),
        scratch_shapes=[pltpu.VMEM((D, F), f32), pltpu.VMEM((D, F), f32), pltpu.VMEM((F, D), f32)],
        compiler_params=_cparams(2),
    )(xn, dob, gs, us, dg, du)


def _proj_in_fwd(h, gain, w_in, tm):
    T, D = h.shape
    nsh, N = w_in.shape[0], w_in.shape[2]
    nt = T // tm

    def body(h_ref, gain_ref, w_ref, z_ref, xn_ref):
        @pl.when(pl.program_id(1) == 0)
        def _():
            hv = h_ref[...]
            r = lax.rsqrt(jnp.mean(hv * hv, axis=-1, keepdims=True) + EPS)
            xn_ref[...] = (hv * r * gain_ref[...]).astype(bf16)

        z_ref[...] = _dot(xn_ref[...], w_ref[...])

    return pl.pallas_call(
        body, name="proj_in_fwd", grid=(nt, nsh),
        in_specs=[pl.BlockSpec((tm, D), lambda i, k: (i, 0)), _full((1, D)),
                  pl.BlockSpec((None, D, N), lambda i, k: (k, 0, 0))],
        out_specs=pl.BlockSpec((tm, N), lambda i, k: (i, k)),
        out_shape=_sds((T, nsh * N), f32),
        scratch_shapes=[pltpu.VMEM((tm, D), bf16)],
        compiler_params=_cparams(2),
    )(h, gain, w_in)


def _proj_in_bwd_dgrad(h, gain, dres, dz, w_in, tm):
    T, D = h.shape
    nsh, N = w_in.shape[0], w_in.shape[2]
    nt = T // tm

    def body(h_ref, gain_ref, dres_ref, dz_ref, w_ref, dh_ref, dgain_ref, xn_ref, acc_ref):
        i, k = pl.program_id(0), pl.program_id(1)

        @pl.when((i == 0) & (k == 0))
        def _():
            dgain_ref[...] = jnp.zeros_like(dgain_ref)

        @pl.when(k == 0)
        def _():
            acc_ref[...] = jnp.zeros_like(acc_ref)

        acc_ref[...] += _dot_nt(dz_ref[...].astype(bf16), w_ref[...])

        @pl.when(k == nsh - 1)
        def _():
            hv = h_ref[...]
            r = lax.rsqrt(jnp.mean(hv * hv, axis=-1, keepdims=True) + EPS)
            xh = hv * r
            xn_ref[...] = (xh * gain_ref[...]).astype(bf16)
            dxn = acc_ref[...]
            dgain_ref[...] += jnp.sum(dxn * xh, axis=0, keepdims=True)
            dxh = dxn * gain_ref[...]
            dh_ref[...] = dres_ref[...] + r * (dxh - xh * jnp.mean(dxh * xh, axis=-1, keepdims=True))

    tok = pl.BlockSpec((tm, D), lambda i, k: (i, 0))
    return pl.pallas_call(
        body, name="proj_in_bwd_dgrad", grid=(nt, nsh),
        in_specs=[tok, _full((1, D)), tok, pl.BlockSpec((tm, N), lambda i, k: (i, k)),
                  pl.BlockSpec((None, D, N), lambda i, k: (k, 0, 0))],
        out_specs=(tok, _full((1, D)), tok),
        out_shape=(_sds((T, D), f32), _sds((1, D), f32), _sds((T, D), bf16)),
        scratch_shapes=[pltpu.VMEM((tm, D), f32)],
        compiler_params=_cparams(2),
    )(h, gain, dres, dz, w_in)


def _proj_in_bwd_wgrad(xn, dz, nsh, tm):
    T, D = xn.shape
    N = dz.shape[1] // nsh
    nt = T // tm

    def body(xn_ref, dz_ref, dw_ref, acc_ref):
        i = pl.program_id(1)

        @pl.when(i == 0)
        def _():
            acc_ref[...] = jnp.zeros_like(acc_ref)

        acc_ref[...] += _dot_tn(xn_ref[...], dz_ref[...].astype(bf16))

        @pl.when(i == nt - 1)
        def _():
            dw_ref[...] = acc_ref[...].astype(bf16)

    return pl.pallas_call(
        body, name="proj_in_bwd_wgrad", grid=(nsh, nt),
        in_specs=[pl.BlockSpec((tm, D), lambda k, i: (i, 0)), pl.BlockSpec((tm, N), lambda k, i: (i, k))],
        out_specs=pl.BlockSpec((None, D, N), lambda k, i: (k, 0, 0)),
        out_shape=_sds((nsh, D, N), bf16),
        scratch_shapes=[pltpu.VMEM((D, N), f32)],
        compiler_params=_cparams(2),
    )(xn, dz)


def _proj_out_fwd(h, oa, ob, oc, w_out, tm):
    T, D = h.shape
    nsh, R = w_out.shape[0], w_out.shape[1]
    da, db = oa.shape[1], ob.shape[1]
    nt = T // tm

    def body(h_ref, oa_ref, ob_ref, oc_ref, w_ref, out_ref):
        w = w_ref[...].reshape(nsh * R, D)
        out_ref[...] = (h_ref[...] + _dot(oa_ref[...], w[:da]) + _dot(ob_ref[...], w[da:da + db])
                        + _dot(oc_ref[...], w[da + db:]))

    def tok(n):
        return pl.BlockSpec((tm, n), lambda i: (i, 0))

    return pl.pallas_call(
        body, name="proj_out_fwd", grid=(nt,),
        in_specs=[tok(D), tok(da), tok(db), tok(oc.shape[1]), _full((nsh, R, D))],
        out_specs=tok(D), out_shape=_sds((T, D), f32),
        compiler_params=_cparams(1),
    )(h, oa, ob, oc, w_out)


def _proj_out_bwd(dh, oa, ob, oc, w_out, tm):
    T, D = dh.shape
    nsh, R = w_out.shape[0], w_out.shape[1]
    da, db, dc = oa.shape[1], ob.shape[1], oc.shape[1]
    nt = T // tm

    def body(dh_ref, oa_ref, ob_ref, oc_ref, w_ref, doa_ref, dob_ref, doc_ref, dw_ref, acc_ref):
        i = pl.program_id(0)

        @pl.when(i == 0)
        def _():
            acc_ref[...] = jnp.zeros_like(acc_ref)

        d = dh_ref[...].astype(bf16)
        w = w_ref[...].reshape(nsh * R, D)
        dm = _dot_nt(d, w)
        doa_ref[...] = dm[:, :da]
        dob_ref[...] = dm[:, da:da + db]
        doc_ref[...] = dm[:, da + db:]
        acc_ref[pl.ds(0, da), :] += _dot_tn(oa_ref[...], d)
        acc_ref[pl.ds(da, db), :] += _dot_tn(ob_ref[...], d)
        acc_ref[pl.ds(da + db, dc), :] += _dot_tn(oc_ref[...], d)

        @pl.when(i == nt - 1)
        def _():
            dw_ref[...] = acc_ref[...].astype(bf16).reshape(nsh, R, D)

    def tok(n):
        return pl.BlockSpec((tm, n), lambda i: (i, 0))

    wspec = _full((nsh, R, D))
    return pl.pallas_call(
        body, name="proj_out_bwd", grid=(nt,),
        in_specs=[tok(D), tok(da), tok(db), tok(dc), wspec],
        out_specs=(tok(da), tok(db), tok(dc), wspec),
        out_shape=(_sds((T, da), f32), _sds((T, db), f32), _sds((T, dc), f32), _sds((nsh, R, D), bf16)),
        scratch_shapes=[pltpu.VMEM((nsh * R, D), f32)],
        compiler_params=_cparams(1),
    )(dh, oa, ob, oc, w_out)


def _head_sum(m, n_heads):
    parts = []
    for hd in range(n_heads):
        s = jnp.sum(m[:, hd * HEAD:(hd + 1) * HEAD], axis=-1, keepdims=True)
        parts.append(jnp.broadcast_to(s, (m.shape[0], HEAD)))
    return parts[0] if n_heads == 1 else jnp.concatenate(parts, axis=1)


def _hgrn_chunk(q, fl, iv, g, lb, gain, states, tri, n_heads):
    C = q.shape[0]
    qs = _silu(q)
    forget = lb + (1.0 - lb) * jax.nn.sigmoid(fl)
    kk = 1.0 - forget
    logf = jnp.log(forget)
    b = jnp.dot(tri, logf, precision=HI, preferred_element_type=f32)
    n_sub = C // A_SUB
    row = lax.broadcasted_iota(jnp.int32, (A_SUB, 1), 0)
    outs = []
    for blk in range(n_sub):
        lo = blk * A_SUB
        b_i, q_i, k_i, v_i = b[lo:lo + A_SUB], qs[lo:lo + A_SUB], kk[lo:lo + A_SUB], iv[lo:lo + A_SUB]
        acc = jnp.zeros_like(q_i)
        for s in range(A_SUB):
            e = jnp.exp(jnp.minimum(b_i - b_i[s:s + 1], 0.0))
            m = jnp.where(row >= s, q_i * e * k_i[s:s + 1], 0.0)
            acc = acc + _head_sum(m, n_heads) * v_i[s:s + 1]
        if blk > 0:
            piv = b_i[0:1]
            qt = (q_i * jnp.exp(b_i - piv)).astype(bf16)
            kt = (kk[:lo] * jnp.exp(piv - b[:lo])).astype(bf16)
            vb = iv[:lo].astype(bf16)
            parts = []
            for hd in range(n_heads):
                sl = slice(hd * HEAD, (hd + 1) * HEAD)
                sc = _dot_nt(qt[:, sl], kt[:, sl])
                parts.append(_dot(sc.astype(bf16), vb[:, sl]))
            acc = acc + (parts[0] if n_heads == 1 else jnp.concatenate(parts, axis=1))
        outs.append(acc)
    o = jnp.concatenate(outs, axis=0)
    qe = (qs * jnp.exp(b)).astype(bf16)
    b_end = b[C - 1:C]
    kd = (kk * jnp.exp(b_end - b)).astype(bf16)
    vb = iv.astype(bf16)
    dec = jnp.exp(b_end)
    inter, new_states = [], []
    for hd in range(n_heads):
        sl = slice(hd * HEAD, (hd + 1) * HEAD)
        st = states[hd]
        inter.append(_dot_nt(qe[:, sl], st.astype(bf16)))
        new_states.append(dec[:, sl] * st + _dot_tn(vb[:, sl], kd[:, sl]))
    o = o + (inter[0] if n_heads == 1 else jnp.concatenate(inter, axis=1))
    ms = _head_sum(o * o, n_heads) * (1.0 / HEAD)
    out = o * lax.rsqrt(ms + EPS) * gain * _silu(g)
    return out, tuple(new_states)


def _tri_matrix(n):
    return jnp.asarray(np.tril(np.ones((n, n), np.float32)))


def _hgrn_fwd(z, lb, gain, d_a):
    T = z.shape[0]
    C = A_CHUNK
    nc = T // C
    nh = d_a // HEAD
    tri = _tri_matrix(C)

    def body(q_ref, f_ref, i_ref, g_ref, lb_ref, gain_ref, tri_ref, out_ref, st_ref, carry_ref):
        @pl.when(pl.program_id(0) == 0)
        def _():
            carry_ref[...] = jnp.zeros_like(carry_ref)

        states = tuple(carry_ref[hd] for hd in range(nh))
        st_ref[...] = carry_ref[...]
        out, new_states = _hgrn_chunk(q_ref[...], f_ref[...], i_ref[...], g_ref[...], lb_ref[...], gain_ref[...],
                                      states, tri_ref[...], nh)
        out_ref[...] = out.astype(bf16)
        for hd in range(nh):
            carry_ref[hd] = new_states[hd]

    def col(j):
        return pl.BlockSpec((C, d_a), lambda c, j=j: (c, j))

    return pl.pallas_call(
        body, name="hgrn_fwd", grid=(nc,),
        in_specs=[col(0), col(1), col(2), col(3), _full((1, d_a)), _full((1, d_a)), _full((C, C))],
        out_specs=(pl.BlockSpec((C, d_a), lambda c: (c, 0)), pl.BlockSpec((None, nh, HEAD, HEAD), lambda c: (c, 0, 0, 0))),
        out_shape=(_sds((T, d_a), bf16), _sds((nc, nh, HEAD, HEAD), f32)),
        scratch_shapes=[pltpu.VMEM((nh, HEAD, HEAD), f32)],
        compiler_params=_cparams(1),
    )(z, z, z, z, lb, gain, tri)


def _hgrn_bwd(z, lb, gain, states, dout, d_a):
    T = z.shape[0]
    C = A_CHUNK
    nc = T // C
    nh = d_a // HEAD
    tri = _tri_matrix(C)

    def body(q_ref, f_ref, i_ref, g_ref, lb_ref, gain_ref, tri_ref, st_ref, do_ref,
             dz_ref, dlb_ref, dgain_ref, carry_ref):
        @pl.when(pl.program_id(0) == 0)
        def _():
            carry_ref[...] = jnp.zeros_like(carry_ref)
            dlb_ref[...] = jnp.zeros_like(dlb_ref)
            dgain_ref[...] = jnp.zeros_like(dgain_ref)

        tri_v = tri_ref[...]

        def fn(q, fl, iv, g, lbv, gv, sts):
            return _hgrn_chunk(q, fl, iv, g, lbv, gv, sts, tri_v, nh)

        states_in = tuple(st_ref[hd] for hd in range(nh))
        _, vjp = jax.vjp(fn, q_ref[...], f_ref[...], i_ref[...], g_ref[...], lb_ref[...], gain_ref[...], states_in)
        dstates = tuple(carry_ref[hd] for hd in range(nh))
        dq, df, di, dg, dlb, dgain, dst = vjp((do_ref[...], dstates))
        dz_ref[:, 0:d_a] = dq
        dz_ref[:, d_a:2 * d_a] = df
        dz_ref[:, 2 * d_a:3 * d_a] = di
        dz_ref[:, 3 * d_a:4 * d_a] = dg
        dlb_ref[...] += dlb
        dgain_ref[...] += dgain
        for hd in range(nh):
            carry_ref[hd] = dst[hd]

    def col(j):
        return pl.BlockSpec((C, d_a), lambda c, j=j: (nc - 1 - c, j))

    return pl.pallas_call(
        body, name="hgrn_bwd", grid=(nc,),
        in_specs=[col(0), col(1), col(2), col(3), _full((1, d_a)), _full((1, d_a)), _full((C, C)),
                  pl.BlockSpec((None, nh, HEAD, HEAD), lambda c: (nc - 1 - c, 0, 0, 0)),
                  pl.BlockSpec((C, d_a), lambda c: (nc - 1 - c, 0))],
        out_specs=(pl.BlockSpec((C, 4 * d_a), lambda c: (nc - 1 - c, 0)), _full((1, d_a)), _full((1, d_a))),
        out_shape=(_sds(z.shape, f32), _sds((1, d_a), f32), _sds((1, d_a), f32)),
        scratch_shapes=[pltpu.VMEM((nh, HEAD, HEAD), f32)],
        compiler_params=_cparams(1),
    )(z, z, z, z, lb, gain, tri, states, dout)


def _one_minus_exp(x):
    series = -x * (1.0 + x * (0.5 + x * (1.0 / 6.0 + x * (1.0 / 24.0))))
    return jnp.where(x > -0.03, series, 1.0 - jnp.exp(x))


def _lru_pre(xc, wa, ba, wx, bx, lam):
    xb16 = xc.astype(bf16)
    r = jax.nn.sigmoid(_dot(xb16, wa.astype(bf16)) + ba)
    gi = jax.nn.sigmoid(_dot(xb16, wx.astype(bf16)) + bx)
    log_a = -LRU_C * r * jax.nn.softplus(-lam)
    a = jnp.exp(log_a)
    mult = jnp.sqrt(_one_minus_exp(2.0 * log_a))
    return a, mult * gi * xc


def _lru_post(h, gate, gain, avg):
    y = h * jax.nn.gelu(gate)
    ms = jnp.dot(y * y, avg, precision=HI, preferred_element_type=f32)
    return y * lax.rsqrt(ms + EPS) * gain


def _shift_down(x, d, prev):
    row = lax.broadcasted_iota(jnp.int32, x.shape, 0)
    return jnp.where(row >= d, pltpu.roll(x, d, 0), pltpu.roll(prev, d, 0))


def _shift_up(x, d, nxt):
    n = x.shape[0]
    row = lax.broadcasted_iota(jnp.int32, x.shape, 0)
    return jnp.where(row < n - d, pltpu.roll(x, n - d, 0), pltpu.roll(nxt, n - d, 0))


def _scan_rows(a, u, reverse):
    n = a.shape[0]
    row = lax.broadcasted_iota(jnp.int32, a.shape, 0)
    d = 1
    while d < n:
        shift, ok = (n - d, row < n - d) if reverse else (d, row >= d)
        su = jnp.where(ok, pltpu.roll(u, shift, 0), 0.0)
        sa = jnp.where(ok, pltpu.roll(a, shift, 0), 1.0)
        u = u + a * su
        a = a * sa
        d *= 2
    return a, u


def _conv(xb, xprev, cw, cb):
    xc = cb + cw[CONV_WIDTH - 1:CONV_WIDTH] * xb
    for d in range(1, CONV_WIDTH):
        xc = xc + cw[CONV_WIDTH - 1 - d:CONV_WIDTH - d] * _shift_down(xb, d, xprev)
    return xc


def _lru_fwd(z, col0, d_b, cw, cb, wa, ba, wx, bx, lam, gain, avg):
    T = z.shape[0]
    R = min(B_CHUNK, T)
    nr = T // R
    jb = col0 // d_b

    def body(xb_ref, gate_ref, cw_ref, cb_ref, wa_ref, ba_ref, wx_ref, bx_ref, lam_ref, gain_ref, avg_ref,
             out_ref, h_ref, xprev_ref, hprev_ref):
        @pl.when(pl.program_id(0) == 0)
        def _():
            xprev_ref[...] = jnp.zeros_like(xprev_ref)
            hprev_ref[...] = jnp.zeros_like(hprev_ref)

        xb = xb_ref[...]
        xc = _conv(xb, xprev_ref[...], cw_ref[...], cb_ref[...])
        a, u = _lru_pre(xc, wa_ref[...], ba_ref[...], wx_ref[...], bx_ref[...], lam_ref[...])
        acum, hl = _scan_rows(a, u, False)
        h = hl + acum * hprev_ref[R - 1:R, :]
        h_ref[...] = h
        out_ref[...] = _lru_post(h, gate_ref[...], gain_ref[...], avg_ref[...]).astype(bf16)
        xprev_ref[...] = xb
        hprev_ref[...] = h

    vec = _full((1, d_b))
    return pl.pallas_call(
        body, name="lru_fwd", grid=(nr,),
        in_specs=[pl.BlockSpec((R, d_b), lambda i: (i, jb)), pl.BlockSpec((R, d_b), lambda i: (i, jb + 1)),
                  _full((CONV_WIDTH, d_b)), vec, _full((d_b, d_b)), vec, _full((d_b, d_b)), vec, vec, vec, _full((d_b, d_b))],
        out_specs=(pl.BlockSpec((R, d_b), lambda i: (i, 0)), pl.BlockSpec((R, d_b), lambda i: (i, 0))),
        out_shape=(_sds((T, d_b), bf16), _sds((T, d_b), f32)),
        scratch_shapes=[pltpu.VMEM((R, d_b), f32), pltpu.VMEM((R, d_b), f32)],
        compiler_params=_cparams(1),
    )(z, z, cw, cb, wa, ba, wx, bx, lam, gain, avg)


def _lru_bwd(z, col0, d_b, hsave, dout, dz_buf, cw, cb, wa, ba, wx, bx, lam, gain, avg):
    T = z.shape[0]
    R = min(B_CHUNK, T)
    nr = T // R
    jb = col0 // d_b

    def body(xb_ref, xp_ref, gate_ref, h_ref, hp_ref, do_ref,
             cw_ref, cb_ref, wa_ref, ba_ref, wx_ref, bx_ref, lam_ref, gain_ref, avg_ref, dzin_ref,
             dz_ref, dcw_ref, dcb_ref, dwa_ref, dba_ref, dwx_ref, dbx_ref, dlam_ref, dgain_ref,
             gfirst_ref, afirst_ref, dxcn_ref):
        step = pl.program_id(0)
        first_in_time = step == nr - 1

        @pl.when(step == 0)
        def _():
            for r in (dcw_ref, dcb_ref, dwa_ref, dba_ref, dwx_ref, dbx_ref, dlam_ref, dgain_ref,
                      gfirst_ref, afirst_ref, dxcn_ref):
                r[...] = jnp.zeros_like(r)

        xb = xb_ref[...]
        keep = jnp.where(first_in_time, 0.0, 1.0)
        xprev = xp_ref[...] * keep
        hprev = hp_ref[...] * keep
        cw = cw_ref[...]
        xc = _conv(xb, xprev, cw, cb_ref[...])
        (a, _), vjp_pre = jax.vjp(_lru_pre, xc, wa_ref[...], ba_ref[...], wx_ref[...], bx_ref[...], lam_ref[...])
        h = h_ref[...]
        avg = avg_ref[...]
        _, vjp_post = jax.vjp(lambda hh, gg, gn: _lru_post(hh, gg, gn, avg), h, gate_ref[...], gain_ref[...])
        dh, dgate, dgain = vjp_post(do_ref[...])
        a_next = _shift_up(a, 1, jnp.broadcast_to(afirst_ref[0:1, :], a.shape))
        acum, gl = _scan_rows(a_next, dh, True)
        gtot = gl + acum * gfirst_ref[0:1, :]
        da = gtot * _shift_down(h, 1, hprev)
        dxc, dwa, dba, dwx, dbx, dlam = vjp_pre((da, gtot))
        dxcn = dxcn_ref[...]
        dxb = cw[CONV_WIDTH - 1:CONV_WIDTH] * dxc
        dcw_ref[CONV_WIDTH - 1:CONV_WIDTH, :] += jnp.sum(dxc * xb, axis=0, keepdims=True)
        for d in range(1, CONV_WIDTH):
            tap = CONV_WIDTH - 1 - d
            dxb = dxb + cw[tap:tap + 1] * _shift_up(dxc, d, dxcn)
            dcw_ref[tap:tap + 1, :] += jnp.sum(dxc * _shift_down(xb, d, xprev), axis=0, keepdims=True)
        dz_ref[:, 0:d_b] = dxb
        dz_ref[:, d_b:2 * d_b] = dgate
        dcb_ref[...] += jnp.sum(dxc, axis=0, keepdims=True)
        dwa_ref[...] += dwa
        dba_ref[...] += dba
        dwx_ref[...] += dwx
        dbx_ref[...] += dbx
        dlam_ref[...] += dlam
        dgain_ref[...] += dgain
        gfirst_ref[...] = jnp.broadcast_to(gtot[0:1, :], gfirst_ref.shape)
        afirst_ref[...] = jnp.broadcast_to(a[0:1, :], afirst_ref.shape)
        dxcn_ref[...] = dxc

    vec = _full((1, d_b))
    mat = _full((d_b, d_b))

    def cur(j):
        return pl.BlockSpec((R, d_b), lambda i, j=j: (nr - 1 - i, j))

    def prev(j):
        return pl.BlockSpec((R, d_b), lambda i, j=j: (jnp.maximum(nr - 2 - i, 0), j))

    return pl.pallas_call(
        body, name="lru_bwd", grid=(nr,),
        in_specs=[cur(jb), prev(jb), cur(jb + 1), cur(0), prev(0), cur(0),
                  _full((CONV_WIDTH, d_b)), vec, mat, vec, mat, vec, vec, vec, mat, ANY_SPEC],
        out_specs=(pl.BlockSpec((R, 2 * d_b), lambda i: (nr - 1 - i, col0 // (2 * d_b))), _full((CONV_WIDTH, d_b)), vec, mat, vec, mat, vec, vec, vec),
        out_shape=(_sds(dz_buf.shape, f32), _sds((CONV_WIDTH, d_b), f32), _sds((1, d_b), f32), _sds((d_b, d_b), f32),
                   _sds((1, d_b), f32), _sds((d_b, d_b), f32), _sds((1, d_b), f32), _sds((1, d_b), f32), _sds((1, d_b), f32)),
        scratch_shapes=[pltpu.VMEM((8, d_b), f32), pltpu.VMEM((8, d_b), f32), pltpu.VMEM((R, d_b), f32)],
        input_output_aliases={15: 0},
        compiler_params=_cparams(1),
    )(z, z, z, hsave, hsave, dout, cw, cb, wa, ba, wx, bx, lam, gain, avg, dz_buf)


def _sgu_chunk(u_in, v_in, w, bexp, gain, avg, n_groups):
    C, d_c = u_in.shape
    gd = d_c // n_groups
    u = jax.nn.gelu(u_in)
    v = jax.nn.gelu(v_in)
    mu = jnp.dot(v, avg, precision=HI, preferred_element_type=f32)
    vc = v - mu
    var = jnp.dot(vc * vc, avg, precision=HI, preferred_element_type=f32)
    vh = (vc * lax.rsqrt(var + EPS)).astype(bf16)
    lane = lax.broadcasted_iota(jnp.int32, (1, d_c), 1)
    causal = lax.broadcasted_iota(jnp.int32, (C, C), 0) >= lax.broadcasted_iota(jnp.int32, (C, C), 1)
    zz = bexp
    for g in range(n_groups):
        wg = jnp.where(causal, w[g], 0.0).astype(bf16)
        zz = zz + jnp.where((lane >= g * gd) & (lane < (g + 1) * gd), _dot(wg, vh), 0.0)
    y = u * zz
    ms = jnp.dot(y * y, avg, precision=HI, preferred_element_type=f32)
    return y * lax.rsqrt(ms + EPS) * gain


def _sgu_fwd(z, col0, d_c, w, bexp, gain, avg):
    T = z.shape[0]
    C = C_CHUNK
    jb = col0 // d_c
    G = w.shape[0]

    def body(u_ref, v_ref, w_ref, b_ref, gain_ref, avg_ref, out_ref):
        out_ref[...] = _sgu_chunk(u_ref[...], v_ref[...], w_ref[...], b_ref[...], gain_ref[...], avg_ref[...], G).astype(bf16)

    return pl.pallas_call(
        body, name="sgu_fwd", grid=(T // C,),
        in_specs=[pl.BlockSpec((C, d_c), lambda i: (i, jb)), pl.BlockSpec((C, d_c), lambda i: (i, jb + 1)),
                  _full((G, C, C)), _full((C, d_c)), _full((1, d_c)), _full((d_c, d_c))],
        out_specs=pl.BlockSpec((C, d_c), lambda i: (i, 0)),
        out_shape=_sds((T, d_c), bf16),
        compiler_params=_cparams(1),
    )(z, z, w, bexp, gain, avg)


def _sgu_bwd(z, col0, d_c, dout, dz_buf, w, bexp, gain, avg):
    T = z.shape[0]
    C = C_CHUNK
    nc = T // C
    jb = col0 // d_c
    G = w.shape[0]
    gd = d_c // G

    def body(u_ref, v_ref, do_ref, w_ref, b_ref, gain_ref, avg_ref, dzin_ref, dz_ref, dw_ref, db_ref, dgain_ref, dbexp_ref):
        step = pl.program_id(0)

        @pl.when(step == 0)
        def _():
            dw_ref[...] = jnp.zeros_like(dw_ref)
            dgain_ref[...] = jnp.zeros_like(dgain_ref)
            dbexp_ref[...] = jnp.zeros_like(dbexp_ref)

        avg = avg_ref[...]
        _, vjp = jax.vjp(lambda a, b, c, d, e: _sgu_chunk(a, b, c, d, e, avg, G),
                         u_ref[...], v_ref[...], w_ref[...], b_ref[...], gain_ref[...])
        du, dv, dw, dbexp, dgain = vjp(do_ref[...])
        dz_ref[:, 0:d_c] = du
        dz_ref[:, d_c:2 * d_c] = dv
        dw_ref[...] += dw
        dbexp_ref[...] += dbexp
        dgain_ref[...] += dgain

        @pl.when(step == nc - 1)
        def _():
            lane = lax.broadcasted_iota(jnp.int32, (1, d_c), 1)
            acc = dbexp_ref[...]
            for g in range(G):
                sel = jnp.where((lane >= g * gd) & (lane < (g + 1) * gd), acc, 0.0)
                db_ref[:, g:g + 1] = jnp.sum(sel, axis=1, keepdims=True)

    return pl.pallas_call(
        body, name="sgu_bwd", grid=(nc,),
        in_specs=[pl.BlockSpec((C, d_c), lambda i: (i, jb)), pl.BlockSpec((C, d_c), lambda i: (i, jb + 1)),
                  pl.BlockSpec((C, d_c), lambda i: (i, 0)),
                  _full((G, C, C)), _full((C, d_c)), _full((1, d_c)), _full((d_c, d_c)), ANY_SPEC],
        out_specs=(pl.BlockSpec((C, 2 * d_c), lambda i: (i, col0 // (2 * d_c))), _full((G, C, C)), _full((C, G)), _full((1, d_c))),
        out_shape=(_sds(dz_buf.shape, f32), _sds((G, C, C), f32), _sds((C, G), f32), _sds((1, d_c), f32)),
        scratch_shapes=[pltpu.VMEM((C, d_c), f32)],
        input_output_aliases={7: 0},
        compiler_params=_cparams(1),
    )(z, z, dout, w, bexp, gain, avg, dz_buf)


def _loss_head(h, gain, target, tm):
    T, D = h.shape
    nt = T // tm

    def body(h_ref, gain_ref, tgt_ref, dh_ref, loss_ref, dgain_ref):
        @pl.when(pl.program_id(0) == 0)
        def _():
            loss_ref[...] = jnp.zeros_like(loss_ref)
            dgain_ref[...] = jnp.zeros_like(dgain_ref)

        hv = h_ref[...]
        gain_v = gain_ref[...]
        r = lax.rsqrt(jnp.mean(hv * hv, axis=-1, keepdims=True) + EPS)
        xh = hv * r
        e = xh * gain_v - tgt_ref[...]
        loss_ref[...] += 0.5 * jnp.sum(jnp.mean(e * e, axis=-1, keepdims=True), axis=0, keepdims=True)
        dy = e * (1.0 / D)
        dgain_ref[...] += jnp.sum(dy * xh, axis=0, keepdims=True)
        dxh = dy * gain_v
        dh_ref[...] = r * (dxh - xh * jnp.mean(dxh * xh, axis=-1, keepdims=True))

    tok = pl.BlockSpec((tm, D), lambda i: (i, 0))
    return pl.pallas_call(
        body, name="loss_head", grid=(nt,),
        in_specs=[tok, _full((1, D)), tok],
        out_specs=(tok, _full((1, 128)), _full((1, D))),
        out_shape=(_sds((T, D), f32), _sds((1, 128), f32), _sds((1, D), f32)),
        compiler_params=_cparams(1),
    )(h, gain, target)


def _lower_bounds_fn(logits):
    n = logits.shape[0]
    mx = jnp.max(logits, axis=0, keepdims=True)
    ex = jnp.exp(logits - mx)
    soft = ex / jnp.sum(ex, axis=0, keepdims=True)
    rows = [jnp.zeros_like(soft[0:1])]
    for l in range(1, n):
        rows.append(rows[-1] + soft[l:l + 1])
    return jnp.concatenate(rows, axis=0)


def _lower_bounds(logits):
    def body(x_ref, o_ref):
        o_ref[...] = _lower_bounds_fn(x_ref[...])

    return pl.pallas_call(body, name="lower_bounds", out_shape=_sds(logits.shape, f32))(logits)


def _lower_bounds_bwd(logits, dlb):
    def body(x_ref, d_ref, o_ref):
        _, vjp = jax.vjp(_lower_bounds_fn, x_ref[...])
        o_ref[...] = vjp(d_ref[...])[0]

    return pl.pallas_call(body, name="lower_bounds_bwd", out_shape=_sds(logits.shape, f32))(logits, dlb)


def _adamw(w, g, m, v, rows_blk):
    R, Cc = w.shape
    rb = R if R <= rows_blk else math.gcd(R, rows_blk)

    def body(w_ref, g_ref, m_ref, v_ref, d_ref, nm_ref, nv_ref):
        gv = g_ref[...]
        m2 = ADAM_B1 * m_ref[...] + (1.0 - ADAM_B1) * gv
        v2 = ADAM_B2 * v_ref[...] + (1.0 - ADAM_B2) * (gv * gv)
        m_hat = m2 / (1.0 - ADAM_B1 ** ADAM_STEP)
        v_hat = v2 / (1.0 - ADAM_B2 ** ADAM_STEP)
        d_ref[...] = -ADAM_LR * (m_hat / (jnp.sqrt(v_hat) + ADAM_EPS) + ADAM_WD * w_ref[...])
        nm_ref[...] = m2
        nv_ref[...] = v2

    spec = pl.BlockSpec((rb, Cc), lambda i: (i, 0))
    return pl.pallas_call(
        body, name="adamw", grid=(R // rb,),
        in_specs=[spec] * 4, out_specs=(spec,) * 3, out_shape=(_sds((R, Cc), f32),) * 3,
        compiler_params=_cparams(1),
    )(w, g, m, v)


def _pair_sum(grads, recv, c_arr):
    n = len(grads)
    nsh = grads[0].shape[0]

    def body(c_ref, *refs):
        for a in range(n):
            refs[2 * n + a][...] = (refs[a][...].astype(f32) + refs[n + a][...].astype(f32)).astype(bf16)

    g_specs, r_specs, out_shape = [], [], []
    for g in grads:
        _, R, Cc = g.shape
        r2 = R // 2
        g_specs.append(pl.BlockSpec((None, r2, Cc), lambda s, c: (s, c[0], 0)))
        r_specs.append(pl.BlockSpec((None, r2, Cc), lambda s, c: (s, 0, 0)))
        out_shape.append(_sds((nsh, r2, Cc), bf16))
    gs = pltpu.PrefetchScalarGridSpec(num_scalar_prefetch=1, grid=(nsh,), in_specs=g_specs + r_specs, out_specs=tuple(r_specs))
    return list(pl.pallas_call(body, name="pair_sum", grid_spec=gs, out_shape=tuple(out_shape),
                               compiler_params=_cparams(1))(c_arr, *grads, *recv))


def _add(a, b):
    def body(a_ref, b_ref, o_ref):
        o_ref[...] = a_ref[...] + b_ref[...]

    return pl.pallas_call(body, name="pair_sum_small", out_shape=_sds(a.shape, f32))(a, b)


def _chip_sum(hsum, recv, bufs, slot_arr, c_arr, layer, n_layers):
    n = len(hsum)
    prev = list(bufs)
    steps = 2

    def body(s_ref, c_ref, *refs):
        outs = refs[len(refs) - n:]
        for a in range(n):
            acc = refs[a][...].astype(f32)
            for j in range(N_CHIPS - 1):
                acc = acc + refs[n + a][j].astype(f32)
            outs[a][...] = acc

    h_specs, r_specs, o_specs, out_shape = [], [], [], []
    for hh in hsum:
        _, r2, Cc = hh.shape
        rt = r2 // steps
        h_specs.append(pl.BlockSpec((None, rt, Cc), lambda i, s, c: (s[0], i, 0)))
        r_specs.append(pl.BlockSpec((N_CHIPS - 1, rt, Cc), lambda i, s, c: (0, i, 0)))
        o_specs.append(pl.BlockSpec((None, rt, Cc), lambda i, s, c: (layer, c[0] * steps + i, 0)))
        out_shape.append(_sds((n_layers, 2 * r2, Cc), f32))
    gs = pltpu.PrefetchScalarGridSpec(num_scalar_prefetch=2, grid=(steps,),
                                      in_specs=h_specs + r_specs + [ANY_SPEC] * len(prev), out_specs=tuple(o_specs))
    return list(pl.pallas_call(body, name="chip_sum", grid_spec=gs, out_shape=tuple(out_shape),
                               input_output_aliases={2 + 2 * n + a: a for a in range(len(prev))},
                               compiler_params=_cparams(1))(slot_arr, c_arr, *hsum, *recv, *prev))


def _sum_slots(x):
    def body(x_ref, o_ref):
        acc = x_ref[0]
        for j in range(1, x.shape[0]):
            acc = acc + x_ref[j]
        o_ref[...] = acc

    return pl.pallas_call(body, name="sum_slots", out_shape=_sds(x.shape[1:], f32))(x)


def _blockdiag(w):
    nb, bd, _ = w.shape
    eye = jnp.eye(nb, dtype=w.dtype)
    return (eye[:, None, :, None] * w[:, :, None, :]).reshape(nb * bd, nb * bd)


def _blockdiag_extract(dense, nb):
    bd = dense.shape[0] // nb
    d4 = dense.reshape(nb, bd, nb, bd)
    return jnp.stack([d4[i, :, i, :] for i in range(nb)])


def _pack(arrays, multiple):
    flat = jnp.concatenate([a.reshape(-1).astype(f32) for a in arrays])
    pad = (-flat.shape[0]) % multiple
    return jnp.pad(flat, (0, pad))


def _unpack(flat, shapes):
    out, off = [], 0
    for s in shapes:
        n = int(np.prod(s))
        out.append(flat[off:off + n].reshape(s))
        off += n
    return out


BIG = ("ffn1_wg", "ffn1_wu", "ffn1_wd", "w_in", "w_out", "ffn2_wg", "ffn2_wu", "ffn2_wd")
SMALL = ("ffn1_norm", "mix_norm", "hgrn_lb_logits", "hgrn_norm", "conv_w", "conv_b", "lru_wa", "lru_ba", "lru_wx",
         "lru_bx", "lru_lambda", "lru_norm", "sgu_w", "sgu_b", "sgu_norm", "ffn2_norm", "final_norm")
WEIGHTS = ("ffn1_norm", "ffn1_wg", "ffn1_wu", "ffn1_wd", "mix_norm", "w_in", "hgrn_lb_logits", "hgrn_norm", "conv_w",
           "conv_b", "lru_wa", "lru_ba", "lru_wx", "lru_bx", "lru_lambda", "lru_norm", "sgu_w", "sgu_b", "sgu_norm",
           "w_out", "ffn2_norm", "ffn2_wg", "ffn2_wu", "ffn2_wd", "final_norm")


def kernel(x, ffn1_norm, ffn1_wg, ffn1_wu, ffn1_wd, mix_norm, w_in, hgrn_lb_logits, hgrn_norm, conv_w, conv_b, lru_wa, lru_ba, lru_wx, lru_bx, lru_lambda, lru_norm, sgu_w, sgu_b, sgu_norm, w_out, ffn2_norm, ffn2_wg, ffn2_wu, ffn2_wd, final_norm, loss_target, m_ffn1_norm, m_ffn1_wg, m_ffn1_wu, m_ffn1_wd, m_mix_norm, m_w_in, m_hgrn_lb_logits, m_hgrn_norm, m_conv_w, m_conv_b, m_lru_wa, m_lru_ba, m_lru_wx, m_lru_bx, m_lru_lambda, m_lru_norm, m_sgu_w, m_sgu_b, m_sgu_norm, m_w_out, m_ffn2_norm, m_ffn2_wg, m_ffn2_wu, m_ffn2_wd, m_final_norm, v_ffn1_norm, v_ffn1_wg, v_ffn1_wu, v_ffn1_wd, v_mix_norm, v_w_in, v_hgrn_lb_logits, v_hgrn_norm, v_conv_w, v_conv_b, v_lru_wa, v_lru_ba, v_lru_wx, v_lru_bx, v_lru_lambda, v_lru_norm, v_sgu_w, v_sgu_b, v_sgu_norm, v_w_out, v_ffn2_norm, v_ffn2_wg, v_ffn2_wu, v_ffn2_wd, v_final_norm):
    args = dict(locals())
    W = {n: args[n] for n in WEIGHTS}
    M = {n: args["m_" + n] for n in WEIGHTS}
    V = {n: args["v_" + n] for n in WEIGHTS}

    T, D = x.shape[1], x.shape[2]
    L = ffn1_norm.shape[0]
    d_a, d_b, d_c = hgrn_norm.shape[1], lru_norm.shape[1], sgu_norm.shape[1]
    col_b, col_c = 4 * d_a, 4 * d_a + 2 * d_b
    tm = 512 if T % 512 == 0 else T
    my_c = lax.axis_index("c")
    my_slot = 2 * lax.axis_index("x") + lax.axis_index("y")
    c_arr = jnp.reshape(my_c, (1,)).astype(jnp.int32)
    slot_arr = jnp.reshape(my_slot, (1,)).astype(jnp.int32)

    nb = len(BIG)
    gplan = _gather_ici_plan(nb)
    place_steps = 4 if all(W[n].shape[1] % 64 == 0 for n in BIG) else 2

    def placed(l):
        return _cast_place([W[n] for n in BIG], l, slot_arr, place_steps)

    conv_land = lax.dynamic_update_slice_in_dim(jnp.zeros((N_CHIPS,) + conv_w.shape, f32), conv_w[None], my_slot, axis=0)
    lands0 = placed(0) + [conv_land]
    got = _exchange("gather0_ici", lands0, [_sds(a.shape, a.dtype) for a in lands0], _gather_ici_plan(nb + 1),
                    aliases={a: a for a in range(nb + 1)})
    got = _gather_d2d("gather0_d2d", got)
    G = [None] * L
    G[0] = dict(zip(BIG, got[:nb]))
    conv_full = jnp.transpose(got[nb], (1, 2, 0, 3)).reshape(L, CONV_WIDTH, d_b)

    def start_gather(l, after):
        return _start_copies(f"gather_start_{l}", [], placed(l), gplan, after)

    def finish_gather(l, pending, after):
        send, recv, _, lands, _ = pending
        lands = _wait_copies(f"gather_wait_{l}", send, recv, [], lands, gplan, after)
        return dict(zip(BIG, _gather_d2d("gather_d2d", lands)))

    lb = _lower_bounds(hgrn_lb_logits)
    avg_b = _group_avg_matrix(d_b, d_b // B_BLOCKS)
    avg_c = _group_avg_matrix(d_c, d_c // C_GROUPS)
    wa_dense = [_blockdiag(lru_wa[l]) for l in range(L)]
    wx_dense = [_blockdiag(lru_wx[l]) for l in range(L)]
    bexp = [jnp.repeat(sgu_b[l].T, d_c // C_GROUPS, axis=1) for l in range(L)]

    def lru_params(l):
        return (conv_full[l], conv_b[l][None], wa_dense[l], lru_ba[l].reshape(1, d_b), wx_dense[l],
                lru_bx[l].reshape(1, d_b), lru_lambda[l][None], lru_norm[l][None], avg_b)

    h = x.reshape(T, D)
    saved = []
    for l in range(L):
        s = {"h0": h}
        gain1 = ffn1_norm[l][None]
        pending = None
        if l + 1 < L:
            pending = start_gather(l + 1, G[0]["ffn1_wg"] if l == 0 else h)
            gain1 = gain1 + pending[4][0:1, 0:1]
        g = G[l]
        h, s["g1"], s["u1"] = _ffn_fwd(h, gain1, g["ffn1_wg"], g["ffn1_wu"], g["ffn1_wd"], tm)
        s["h1"] = h
        z = _proj_in_fwd(h, mix_norm[l][None], g["w_in"], tm)
        s["z"] = z
        s["oa"], s["states"] = _hgrn_fwd(z, lb[l][None], hgrn_norm[l][None], d_a)
        s["ob"], s["hl"] = _lru_fwd(z, col_b, d_b, *lru_params(l))
        s["oc"] = _sgu_fwd(z, col_c, d_c, sgu_w[l], bexp[l], sgu_norm[l][None], avg_c)
        h = _proj_out_fwd(h, s["oa"], s["ob"], s["oc"], g["w_out"], tm)
        s["h2"] = h
        h, s["g2"], s["u2"] = _ffn_fwd(h, ffn2_norm[l][None], g["ffn2_wg"], g["ffn2_wu"], g["ffn2_wd"], tm)
        saved.append(s)
        if pending is not None:
            G[l + 1] = finish_gather(l + 1, pending, h)

    dh, loss_part, d_final = _loss_head(h, final_norm[None], loss_target.reshape(T, D), tm)
    loss = lax.psum(loss_part[0, 0], ("x", "y", "c"))

    def pair_views(n_big):
        r = [(lambda i, o, p, a=a: i[a].at[:, pl.ds((1 - p.c) * (i[a].shape[1] // 2), i[a].shape[1] // 2)],
              lambda i, o, p, a=a: o[a], "sib") for a in range(n_big)]
        return r

    chip_plan = [(lambda s_, o, p, a=a, kind=kind: s_[a].at[p.peer_slot(kind)], lambda s_, o, p, a=a, j=j: o[a].at[j], kind)
                 for a in range(nb) for j, kind in enumerate(CHIP_KINDS)]

    def share(l, sbufs, extra_in=(), extra_out=(), extra_remote=(), extra_local=()):
        remote = [(lambda i, o, p, a=a: _half(o[a].at[l], p.c), lambda i, o, p, a=a: _half(o[a].at[l], p.c), "sib")
                  for a in range(nb)]
        outs = [_sds(sa.shape, f32) for sa in sbufs] + list(extra_out)
        return _exchange("grad_share_d2d", list(sbufs) + list(extra_in), outs, remote + list(extra_remote),
                         list(extra_local), aliases={a: a for a in range(nb)})

    small = {n: [None] * L for n in SMALL if n != "final_norm"}
    sbufs, chip_pending = [], None
    for l in reversed(range(L)):
        s, g = saved[l], G[l]
        gain2 = ffn2_norm[l][None]
        if chip_pending is not None:
            gain2 = gain2 + chip_pending[0][4][0:1, 0:1]
        dh, small["ffn2_norm"][l], dg, du, xn, dob = _ffn_bwd_dgrad(
            s["h2"], gain2, dh, s["g2"], s["u2"], g["ffn2_wg"], g["ffn2_wu"], g["ffn2_wd"], tm)
        dwg2, dwu2, dwd2 = _ffn_bwd_wgrad(xn, dob, s["g2"], s["u2"], dg, du, tm)
        doa, dob_, doc, dwo = _proj_out_bwd(dh, s["oa"], s["ob"], s["oc"], g["w_out"], tm)
        dz, small["hgrn_lb_logits"][l], small["hgrn_norm"][l] = _hgrn_bwd(
            s["z"], lb[l][None], hgrn_norm[l][None], s["states"], doa, d_a)
        (dz, small["conv_w"][l], small["conv_b"][l], dwa, small["lru_ba"][l], dwx, small["lru_bx"][l],
         small["lru_lambda"][l], small["lru_norm"][l]) = _lru_bwd(s["z"], col_b, d_b, s["hl"], dob_, dz, *lru_params(l))
        small["lru_wa"][l] = _blockdiag_extract(dwa, B_BLOCKS)
        small["lru_wx"][l] = _blockdiag_extract(dwx, B_BLOCKS)
        dz, small["sgu_w"][l], dsb, small["sgu_norm"][l] = _sgu_bwd(
            s["z"], col_c, d_c, doc, dz, sgu_w[l], bexp[l], sgu_norm[l][None], avg_c)
        small["sgu_b"][l] = dsb.T
        dh, small["mix_norm"][l], xn = _proj_in_bwd_dgrad(s["h1"], mix_norm[l][None], dh, dz, g["w_in"], tm)
        dwi = _proj_in_bwd_wgrad(xn, dz, N_CHIPS, tm)
        dh, small["ffn1_norm"][l], dg, du, xn, dob = _ffn_bwd_dgrad(
            s["h0"], ffn1_norm[l][None], dh, s["g1"], s["u1"], g["ffn1_wg"], g["ffn1_wu"], g["ffn1_wd"], tm)
        dwg1, dwu1, dwd1 = _ffn_bwd_wgrad(xn, dob, s["g1"], s["u1"], dg, du, tm)
        layer_grads = [dwg1, dwu1, dwd1, dwi, dwo, dwg2, dwu2, dwd2]

        if chip_pending is not None:
            (send, recv, hs, lands, _), hsum_prev = chip_pending
            lands = _wait_copies(f"grad_chip_wait_{l + 1}", send, recv, hs, lands, chip_plan, dwg1)
            sbufs = _chip_sum(hsum_prev, lands, sbufs, slot_arr, c_arr, l + 1, L)
            sbufs = share(l + 1, sbufs)
            chip_pending = None
        if l > 0:
            outs = [_sds((N_CHIPS, gr.shape[1] // 2, gr.shape[2]), bf16) for gr in layer_grads]
            recv_a = _exchange("grad_pair_d2d", layer_grads, outs, pair_views(nb))
            hsum = _pair_sum(layer_grads, recv_a, c_arr)
            lands = [lax.empty((N_CHIPS - 1,) + hh.shape[1:], bf16) for hh in hsum]
            chip_pending = (_start_copies(f"grad_chip_start_{l}", hsum, lands, chip_plan, hsum[0]), hsum)
    grad_x = dh.reshape(x.shape)

    small_names = [n for n in SMALL]
    small_parts = [jnp.stack([jnp.reshape(v, (-1,)) for v in small[n]]) if n != "final_norm" else d_final for n in small_names]
    small_shapes = [p.shape for p in small_parts]
    packed = _pack(small_parts, 2 * 8 * 128).reshape(2, -1, 128)
    n_rows = packed.shape[1]
    outs = [_sds((N_CHIPS, gr.shape[1] // 2, gr.shape[2]), bf16) for gr in layer_grads] + [_sds(packed.shape, f32)]
    recv_a = _exchange("grad_pair_d2d_last", layer_grads + [packed], outs,
                       pair_views(nb) + [(lambda i, o, p: i[nb], lambda i, o, p: o[nb], "sib")])
    hsum = _pair_sum(layer_grads, recv_a[:nb], c_arr)
    small_pair = _add(packed, recv_a[nb])
    remote = list(chip_plan) + [(lambda i, o, p: i[nb].at[p.c], lambda i, o, p: o[nb].at[p.slot], kind) for kind in CHIP_KINDS]
    local = [(lambda i, o, p: i[nb].at[p.c], lambda i, o, p: o[nb].at[p.slot])]
    outs = [_sds((N_CHIPS - 1,) + hh.shape[1:], bf16) for hh in hsum] + [_sds((N_CHIPS, n_rows, 128), f32)]
    recv_b = _exchange("grad_chip_ici_last", hsum + [small_pair], outs, remote, local)
    sbufs = _chip_sum(hsum, recv_b[:nb], sbufs, slot_arr, c_arr, 0, L)
    small_half = _sum_slots(recv_b[nb])
    final = share(0, sbufs, extra_in=[small_half], extra_out=[_sds(packed.shape, f32)],
                  extra_remote=[(lambda i, o, p: i[nb], lambda i, o, p: o[nb].at[p.c], "sib")],
                  extra_local=[(lambda i, o, p: i[nb], lambda i, o, p: o[nb].at[p.c])])
    grads = {n: final[a].reshape(W[n].shape) for a, n in enumerate(BIG)}
    small_tot = _unpack(final[nb].reshape(-1), small_shapes)
    for n, val in zip(small_names, small_tot):
        grads[n] = val
    grads["hgrn_lb_logits"] = _lower_bounds_bwd(hgrn_lb_logits, grads["hgrn_lb_logits"])
    shard_cols = conv_w.shape[2]
    grads["conv_w"] = lax.dynamic_slice_in_dim(grads["conv_w"].reshape(L, CONV_WIDTH, d_b), my_slot * shard_cols, shard_cols, axis=2)
    for n in SMALL:
        grads[n] = grads[n].reshape(W[n].shape)

    delta, new_m, new_v = {}, {}, {}
    for n in BIG:
        cols = W[n].shape[-1]
        d2, m2, v2 = _adamw(W[n].reshape(-1, cols), grads[n].reshape(-1, cols), M[n].reshape(-1, cols), V[n].reshape(-1, cols), 512)
        delta[n], new_m[n], new_v[n] = d2.reshape(W[n].shape), m2.reshape(W[n].shape), v2.reshape(W[n].shape)
    shapes = [W[n].shape for n in SMALL]
    packs = [_pack([src[n] for n in SMALL], 8 * 128).reshape(-1, 128) for src in (W, grads, M, V)]
    d2, m2, v2 = _adamw(*packs, 4096)
    for dst, val in ((delta, d2), (new_m, m2), (new_v, v2)):
        for n, piece in zip(SMALL, _unpack(val.reshape(-1), shapes)):
            dst[n] = piece

    return (loss, grad_x, *[grads[n] for n in WEIGHTS], *[delta[n] for n in WEIGHTS],
            *[new_m[n] for n in WEIGHTS], *[new_v[n] for n in WEIGHTS])
```

```python
import math

import numpy as np
import jax
import jax.numpy as jnp
from jax import lax
from jax.experimental import pallas as pl
from jax.experimental.pallas import tpu as pltpu

f32 = jnp.float32
bf16 = jnp.bfloat16
HI = lax.Precision.HIGHEST
MESH = pl.DeviceIdType.MESH

EPS = 1e-6
HEAD = 128
A_CHUNK = 64
A_SUB = 16
A_INNER = 2
SUBLANES = 8
B_BLOCKS = 4
B_CHUNK = 256
CONV_WIDTH = 4
LRU_C = 8.0
C_GROUPS = 4
C_CHUNK = 128
C_INNER = 4
N_CHIPS = 4
ADAM_LR, ADAM_B1, ADAM_B2, ADAM_EPS, ADAM_WD, ADAM_STEP = 0.001, 0.9, 0.999, 1e-08, 0.01, 10
VMEM_LIMIT = 56 * 1024 * 1024


def _cparams(n_axes):
    return pltpu.CompilerParams(dimension_semantics=("arbitrary",) * n_axes, vmem_limit_bytes=VMEM_LIMIT)


def _sds(shape, dtype):
    return jax.ShapeDtypeStruct(tuple(shape), dtype)


def _full(shape):
    n = len(shape)
    return pl.BlockSpec(tuple(shape), lambda *_: (0,) * n)


def _dot(a, b):
    return jnp.dot(a, b, preferred_element_type=f32)


def _dot_nt(a, b):
    return lax.dot_general(a, b, (((1,), (1,)), ((), ())), preferred_element_type=f32)


def _dot_tn(a, b):
    return lax.dot_general(a, b, (((0,), (0,)), ((), ())), preferred_element_type=f32)


def _silu(x):
    return x * jax.nn.sigmoid(x)


def _group_avg_matrix(n, group):
    idx = np.arange(n) // group
    return jnp.asarray((idx[:, None] == idx[None, :]).astype(np.float32) / group)


class _Place:
    def __init__(self):
        self.x, self.y, self.c = lax.axis_index("x"), lax.axis_index("y"), lax.axis_index("c")
        self.slot = 2 * self.x + self.y

    def peer(self, kind):
        x, y, c = self.x, self.y, self.c
        return {"sib": (x, y, 1 - c), "fx": (1 - x, y, c), "fy": (x, 1 - y, c), "fxy": (1 - x, 1 - y, c)}[kind]

    def peer_slot(self, kind):
        x, y = self.x, self.y
        return {"fx": 2 * (1 - x) + y, "fy": 2 * x + (1 - y), "fxy": 2 * (1 - x) + (1 - y)}[kind]


CHIP_KINDS = ("fx", "fy", "fxy")


def _exchange(name, ins, outs, remote, local=(), aliases=None):
    n_in, n_out, n_r, n_l = len(ins), len(outs), len(remote), len(local)

    def body(*refs):
        in_refs, out_refs = refs[:n_in], refs[n_in:n_in + n_out]
        send, recv, lsem = refs[n_in + n_out:]
        p = _Place()
        lcopies = []
        for t, (src, dst) in enumerate(local):
            cp = pltpu.make_async_copy(src(in_refs, out_refs, p), dst(in_refs, out_refs, p), lsem.at[t])
            cp.start()
            lcopies.append(cp)
        copies = []
        for t, (src, dst, kind) in enumerate(remote):
            cp = pltpu.make_async_remote_copy(
                src_ref=src(in_refs, out_refs, p), dst_ref=dst(in_refs, out_refs, p),
                send_sem=send.at[t], recv_sem=recv.at[t], device_id=p.peer(kind), device_id_type=MESH)
            cp.start()
            copies.append(cp)
        for cp in copies:
            cp.wait_recv()
        for cp in copies:
            cp.wait_send()
        for cp in lcopies:
            cp.wait()

    anyspec = pl.BlockSpec(memory_space=pl.ANY)
    res = pl.pallas_call(
        body, name=name, out_shape=tuple(outs),
        in_specs=[anyspec] * n_in, out_specs=tuple([anyspec] * n_out),
        scratch_shapes=[pltpu.SemaphoreType.DMA((n_r,)), pltpu.SemaphoreType.DMA((n_r,)),
                        pltpu.SemaphoreType.DMA((max(n_l, 1),))],
        input_output_aliases=aliases or {},
        compiler_params=pltpu.CompilerParams(has_side_effects=True),
    )(*ins)
    return list(res)


HBM_SPEC = pl.BlockSpec(memory_space=pltpu.HBM)
SEM_SPEC = pl.BlockSpec(memory_space=pltpu.SEMAPHORE)
ANY_SPEC = pl.BlockSpec(memory_space=pl.ANY)
DATAFLOW = pltpu.SideEffectType.DATAFLOW_SIDE_EFFECTING


def _in_hbm(a):
    return pltpu.with_memory_space_constraint(a, pltpu.HBM)


def _start_copies(name, srcs, lands, remote, after):
    n_s, n_l, n_r = len(srcs), len(lands), len(remote)

    def body(*refs):
        src_refs, land_refs = refs[:n_s], refs[n_s:n_s + n_l]
        send, recv = refs[n_s + n_l + 1], refs[n_s + n_l + 2]
        token = refs[-1]
        p = _Place()
        for t, (src, dst, kind) in enumerate(remote):
            pltpu.make_async_remote_copy(
                src_ref=src(src_refs, land_refs, p), dst_ref=dst(src_refs, land_refs, p),
                send_sem=send.at[t], recv_sem=recv.at[t], device_id=p.peer(kind), device_id_type=MESH).start()
        token[...] = jnp.zeros_like(token)

    thru = [pltpu.HBM(a.shape, a.dtype) for a in list(srcs) + list(lands)]
    res = pl.pallas_call(
        body, name=name,
        out_shape=(pltpu.SemaphoreType.DMA((n_r,)), pltpu.SemaphoreType.DMA((n_r,)), *thru, _sds((8, 128), f32)),
        in_specs=[HBM_SPEC] * (n_s + n_l) + [ANY_SPEC],
        out_specs=(SEM_SPEC, SEM_SPEC, *([HBM_SPEC] * (n_s + n_l)), pl.BlockSpec(memory_space=pltpu.VMEM)),
        input_output_aliases={i: 2 + i for i in range(n_s + n_l)},
        compiler_params=pltpu.CompilerParams(has_side_effects=DATAFLOW),
    )(*[_in_hbm(a) for a in srcs], *[_in_hbm(a) for a in lands], after)
    return res[0], res[1], list(res[2:2 + n_s]), list(res[2 + n_s:2 + n_s + n_l]), res[-1]


def _wait_copies(name, send, recv, srcs, lands, remote, after):
    n_s, n_l = len(srcs), len(lands)

    def body(*refs):
        src_refs, land_refs = refs[:n_s], refs[n_s:n_s + n_l]
        send_ref, recv_ref = refs[n_s + n_l], refs[n_s + n_l + 1]
        p = _Place()
        for t, (src, dst, kind) in enumerate(remote):
            cp = pltpu.make_async_remote_copy(
                src_ref=src(src_refs, land_refs, p), dst_ref=dst(src_refs, land_refs, p),
                send_sem=send_ref.at[t], recv_sem=recv_ref.at[t], device_id=p.peer(kind), device_id_type=MESH)
            cp.wait_send()
            cp.wait_recv()

    thru = [pltpu.HBM(a.shape, a.dtype) for a in list(srcs) + list(lands)]
    res = pl.pallas_call(
        body, name=name, out_shape=tuple(thru),
        in_specs=[HBM_SPEC] * (n_s + n_l) + [SEM_SPEC, SEM_SPEC, ANY_SPEC],
        out_specs=tuple([HBM_SPEC] * (n_s + n_l)),
        input_output_aliases={i: i for i in range(n_s + n_l)},
        compiler_params=pltpu.CompilerParams(has_side_effects=DATAFLOW),
    )(*srcs, *lands, send, recv, after)
    return list(res[n_s:])


def _half(ref, c):
    n2 = ref.shape[0] // 2
    return ref.at[pl.ds(c * n2, n2)]


def _gather_ici_plan(n):
    def view(a):
        return lambda s, o, p: _half(o[a].at[p.slot], p.c)

    return [(view(a), view(a), kind) for a in range(n) for kind in CHIP_KINDS]


def _gather_d2d(name, lands):
    n = len(lands)
    remote = []
    for a in range(n):
        for kind in CHIP_KINDS:
            view = lambda i, o, p, a=a, kind=kind: _half(o[a].at[p.peer_slot(kind)], p.c)
            remote.append((view, view, "sib"))
    outs = [_sds(g.shape, g.dtype) for g in lands]
    return _exchange(name, list(lands), outs, remote, aliases={a: a for a in range(n)})


def _cast_place(weights, layer, slot_arr, n_steps=4):
    def body(s_ref, *refs):
        n = len(refs) // 2
        for a in range(n):
            refs[n + a][...] = refs[a][...].astype(bf16)

    in_specs, out_specs, out_shape = [], [], []
    for w in weights:
        _, R, Cc = w.shape
        rt = R // n_steps
        in_specs.append(pl.BlockSpec((None, rt, Cc), lambda i, s: (layer, i, 0)))
        out_specs.append(pl.BlockSpec((None, rt, Cc), lambda i, s: (s[0], i, 0)))
        out_shape.append(_sds((N_CHIPS, R, Cc), bf16))
    gs = pltpu.PrefetchScalarGridSpec(num_scalar_prefetch=1, grid=(n_steps,), in_specs=in_specs, out_specs=tuple(out_specs))
    return list(pl.pallas_call(body, name="cast_place", grid_spec=gs, out_shape=tuple(out_shape),
                               compiler_params=_cparams(1))(slot_arr, *weights))


def _ffn_fwd(h, gain, wg, wu, wd, tm):
    T, D = h.shape
    nsh, F = wg.shape[0], wg.shape[2]
    nt = T // tm

    def body(h_ref, gain_ref, wg_ref, wu_ref, wd_ref, out_ref, gs_ref, us_ref, xn_ref, acc_ref):
        k = pl.program_id(1)

        @pl.when(k == 0)
        def _():
            hv = h_ref[...]
            r = lax.rsqrt(jnp.mean(hv * hv, axis=-1, keepdims=True) + EPS)
            xn_ref[...] = (hv * r * gain_ref[...]).astype(bf16)
            acc_ref[...] = jnp.zeros_like(acc_ref)

        xn = xn_ref[...]
        g = _dot(xn, wg_ref[...])
        u = _dot(xn, wu_ref[...])
        gs_ref[...] = g.astype(bf16)
        us_ref[...] = u.astype(bf16)
        a = (_silu(g) * u).astype(bf16)
        acc_ref[...] += _dot(a, wd_ref[...])

        @pl.when(k == nsh - 1)
        def _():
            out_ref[...] = h_ref[...] + 0.5 * acc_ref[...]

    wspec = pl.BlockSpec((None, D, F), lambda i, k: (k, 0, 0))
    return pl.pallas_call(
        body, name="ffn_fwd", grid=(nt, nsh),
        in_specs=[pl.BlockSpec((tm, D), lambda i, k: (i, 0)), _full((1, D)), wspec, wspec,
                  pl.BlockSpec((None, F, D), lambda i, k: (k, 0, 0))],
        out_specs=(pl.BlockSpec((tm, D), lambda i, k: (i, 0)),
                   pl.BlockSpec((None, tm, F), lambda i, k: (k, i, 0)),
                   pl.BlockSpec((None, tm, F), lambda i, k: (k, i, 0))),
        out_shape=(_sds((T, D), f32), _sds((nsh, T, F), bf16), _sds((nsh, T, F), bf16)),
        scratch_shapes=[pltpu.VMEM((tm, D), bf16), pltpu.VMEM((tm, D), f32)],
        compiler_params=_cparams(2),
    )(h, gain, wg, wu, wd)


def _ffn_bwd_dgrad(h, gain, dout, gs, us, wg, wu, wd, tm):
    T, D = h.shape
    nsh, F = wg.shape[0], wg.shape[2]
    nt = T // tm

    def body(h_ref, gain_ref, dout_ref, gs_ref, us_ref, wg_ref, wu_ref, wd_ref,
             dh_ref, dgain_ref, dg_ref, du_ref, xn_ref, dob_ref, xh_ref, acc_ref):
        i, k = pl.program_id(0), pl.program_id(1)

        @pl.when((i == 0) & (k == 0))
        def _():
            dgain_ref[...] = jnp.zeros_like(dgain_ref)

        @pl.when(k == 0)
        def _():
            hv = h_ref[...]
            r = lax.rsqrt(jnp.mean(hv * hv, axis=-1, keepdims=True) + EPS)
            xh = hv * r
            xh_ref[...] = xh
            xn_ref[...] = (xh * gain_ref[...]).astype(bf16)
            dob_ref[...] = (0.5 * dout_ref[...]).astype(bf16)
            acc_ref[...] = jnp.zeros_like(acc_ref)

        da = _dot_nt(dob_ref[...], wd_ref[...])
        g = gs_ref[...].astype(f32)
        u = us_ref[...].astype(f32)
        sg = jax.nn.sigmoid(g)
        dg = (da * u * (sg * (1.0 + g * (1.0 - sg)))).astype(bf16)
        du = (da * (g * sg)).astype(bf16)
        dg_ref[...] = dg
        du_ref[...] = du
        acc_ref[...] += _dot_nt(dg, wg_ref[...]) + _dot_nt(du, wu_ref[...])

        @pl.when(k == nsh - 1)
        def _():
            hv = h_ref[...]
            r = lax.rsqrt(jnp.mean(hv * hv, axis=-1, keepdims=True) + EPS)
            xh = xh_ref[...]
            dxn = acc_ref[...]
            dgain_ref[...] += jnp.sum(dxn * xh, axis=0, keepdims=True)
            dxh = dxn * gain_ref[...]
            dh_ref[...] = dout_ref[...] + r * (dxh - xh * jnp.mean(dxh * xh, axis=-1, keepdims=True))

    tok = pl.BlockSpec((tm, D), lambda i, k: (i, 0))
    sav = pl.BlockSpec((None, tm, F), lambda i, k: (k, i, 0))
    wspec = pl.BlockSpec((None, D, F), lambda i, k: (k, 0, 0))
    return pl.pallas_call(
        body, name="ffn_bwd_dgrad", grid=(nt, nsh),
        in_specs=[tok, _full((1, D)), tok, sav, sav, wspec, wspec,
                  pl.BlockSpec((None, F, D), lambda i, k: (k, 0, 0))],
        out_specs=(tok, _full((1, D)), sav, sav, tok, tok),
        out_shape=(_sds((T, D), f32), _sds((1, D), f32), _sds((nsh, T, F), bf16), _sds((nsh, T, F), bf16),
                   _sds((T, D), bf16), _sds((T, D), bf16)),
        scratch_shapes=[pltpu.VMEM((tm, D), f32), pltpu.VMEM((tm, D), f32)],
        compiler_params=_cparams(2),
    )(h, gain, dout, gs, us, wg, wu, wd)


def _ffn_bwd_wgrad(xn, dob, gs, us, dg, du, tm):
    T, D = xn.shape
    nsh, F = gs.shape[0], gs.shape[2]
    nt = T // tm

    def body(xn_ref, dob_ref, gs_ref, us_ref, dg_ref, du_ref, dwg_ref, dwu_ref, dwd_ref, ag_ref, au_ref, ad_ref):
        i = pl.program_id(1)

        @pl.when(i == 0)
        def _():
            ag_ref[...] = jnp.zeros_like(ag_ref)
            au_ref[...] = jnp.zeros_like(au_ref)
            ad_ref[...] = jnp.zeros_like(ad_ref)

        xn_v = xn_ref[...]
        ag_ref[...] += _dot_tn(xn_v, dg_ref[...])
        au_ref[...] += _dot_tn(xn_v, du_ref[...])
        g = gs_ref[...].astype(f32)
        a = (_silu(g) * us_ref[...].astype(f32)).astype(bf16)
        ad_ref[...] += _dot_tn(a, dob_ref[...])

        @pl.when(i == nt - 1)
        def _():
            dwg_ref[...] = ag_ref[...].astype(bf16)
            dwu_ref[...] = au_ref[...].astype(bf16)
            dwd_ref[...] = ad_ref[...].astype(bf16)

    tok = pl.BlockSpec((tm, D), lambda k, i: (i, 0))
    sav = pl.BlockSpec((None, tm, F), lambda k, i: (k, i, 0))
    wspec = pl.BlockSpec((None, D, F), lambda k, i: (k, 0, 0))
    wdspec = pl.BlockSpec((None, F, D), lambda k, i: (k, 0, 0))
    return pl.pallas_call(
        body, name="ffn_bwd_wgrad", grid=(nsh, nt),
        in_specs=[tok, tok, sav, sav, sav, sav],
        out_specs=(wspec, wspec, wdspec),
        out_shape=(_sds((nsh, D, F), bf16), _sds((nsh, D, F), bf16), _sds((nsh, F, D), bf16)),
        scratch_shapes=[pltpu.VMEM((D, F), f32), pltpu.VMEM((D, F), f32), pltpu.VMEM((F, D), f32)],
        compiler_params=_cparams(2),
    )(xn, dob, gs, us, dg, du)


def _proj_in_fwd(h, gain, w_in, tm):
    T, D = h.shape
    nsh, N = w_in.shape[0], w_in.shape[2]
    nt = T // tm

    def body(h_ref, gain_ref, w_ref, z_ref, xn_ref):
        @pl.when(pl.program_id(1) == 0)
        def _():
            hv = h_ref[...]
            r = lax.rsqrt(jnp.mean(hv * hv, axis=-1, keepdims=True) + EPS)
            xn_ref[...] = (hv * r * gain_ref[...]).astype(bf16)

        z_ref[...] = _dot(xn_ref[...], w_ref[...])

    return pl.pallas_call(
        body, name="proj_in_fwd", grid=(nt, nsh),
        in_specs=[pl.BlockSpec((tm, D), lambda i, k: (i, 0)), _full((1, D)),
                  pl.BlockSpec((None, D, N), lambda i, k: (k, 0, 0))],
        out_specs=pl.BlockSpec((tm, N), lambda i, k: (i, k)),
        out_shape=_sds((T, nsh * N), f32),
        scratch_shapes=[pltpu.VMEM((tm, D), bf16)],
        compiler_params=_cparams(2),
    )(h, gain, w_in)


def _proj_in_bwd_dgrad(h, gain, dres, dz, w_in, tm):
    T, D = h.shape
    nsh, N = w_in.shape[0], w_in.shape[2]
    nt = T // tm

    def body(h_ref, gain_ref, dres_ref, dz_ref, w_ref, dh_ref, dgain_ref, xn_ref, acc_ref):
        i, k = pl.program_id(0), pl.program_id(1)

        @pl.when((i == 0) & (k == 0))
        def _():
            dgain_ref[...] = jnp.zeros_like(dgain_ref)

        @pl.when(k == 0)
        def _():
            acc_ref[...] = jnp.zeros_like(acc_ref)

        acc_ref[...] += _dot_nt(dz_ref[...], w_ref[...])

        @pl.when(k == nsh - 1)
        def _():
            hv = h_ref[...]
            r = lax.rsqrt(jnp.mean(hv * hv, axis=-1, keepdims=True) + EPS)
            xh = hv * r
            xn_ref[...] = (xh * gain_ref[...]).astype(bf16)
            dxn = acc_ref[...]
            dgain_ref[...] += jnp.sum(dxn * xh, axis=0, keepdims=True)
            dxh = dxn * gain_ref[...]
            dh_ref[...] = dres_ref[...] + r * (dxh - xh * jnp.mean(dxh * xh, axis=-1, keepdims=True))

    tok = pl.BlockSpec((tm, D), lambda i, k: (i, 0))
    return pl.pallas_call(
        body, name="proj_in_bwd_dgrad", grid=(nt, nsh),
        in_specs=[tok, _full((1, D)), tok, pl.BlockSpec((tm, N), lambda i, k: (i, k)),
                  pl.BlockSpec((None, D, N), lambda i, k: (k, 0, 0))],
        out_specs=(tok, _full((1, D)), tok),
        out_shape=(_sds((T, D), f32), _sds((1, D), f32), _sds((T, D), bf16)),
        scratch_shapes=[pltpu.VMEM((tm, D), f32)],
        compiler_params=_cparams(2),
    )(h, gain, dres, dz, w_in)


def _proj_in_bwd_wgrad(xn, dz, nsh, tm):
    T, D = xn.shape
    N = dz.shape[1] // nsh
    nt = T // tm

    def body(xn_ref, dz_ref, dw_ref, acc_ref):
        i = pl.program_id(1)

        @pl.when(i == 0)
        def _():
            acc_ref[...] = jnp.zeros_like(acc_ref)

        acc_ref[...] += _dot_tn(xn_ref[...], dz_ref[...])

        @pl.when(i == nt - 1)
        def _():
            dw_ref[...] = acc_ref[...].astype(bf16)

    return pl.pallas_call(
        body, name="proj_in_bwd_wgrad", grid=(nsh, nt),
        in_specs=[pl.BlockSpec((tm, D), lambda k, i: (i, 0)), pl.BlockSpec((tm, N), lambda k, i: (i, k))],
        out_specs=pl.BlockSpec((None, D, N), lambda k, i: (k, 0, 0)),
        out_shape=_sds((nsh, D, N), bf16),
        scratch_shapes=[pltpu.VMEM((D, N), f32)],
        compiler_params=_cparams(2),
    )(xn, dz)


def _proj_out_fwd(h, oa, ob, oc, w_out, tm):
    T, D = h.shape
    nsh, R = w_out.shape[0], w_out.shape[1]
    da, db = oa.shape[1], ob.shape[1]
    nt = T // tm

    def body(h_ref, oa_ref, ob_ref, oc_ref, w_ref, out_ref):
        w = w_ref[...].reshape(nsh * R, D)
        out_ref[...] = (h_ref[...] + _dot(oa_ref[...], w[:da]) + _dot(ob_ref[...], w[da:da + db])
                        + _dot(oc_ref[...], w[da + db:]))

    def tok(n):
        return pl.BlockSpec((tm, n), lambda i: (i, 0))

    return pl.pallas_call(
        body, name="proj_out_fwd", grid=(nt,),
        in_specs=[tok(D), tok(da), tok(db), tok(oc.shape[1]), _full((nsh, R, D))],
        out_specs=tok(D), out_shape=_sds((T, D), f32),
        compiler_params=_cparams(1),
    )(h, oa, ob, oc, w_out)


def _proj_out_bwd(dh, oa, ob, oc, w_out, tm):
    T, D = dh.shape
    nsh, R = w_out.shape[0], w_out.shape[1]
    da, db, dc = oa.shape[1], ob.shape[1], oc.shape[1]
    nt = T // tm

    def body(dh_ref, oa_ref, ob_ref, oc_ref, w_ref, doa_ref, dob_ref, doc_ref, dw_ref, acc_ref):
        i = pl.program_id(0)

        @pl.when(i == 0)
        def _():
            acc_ref[...] = jnp.zeros_like(acc_ref)

        d = dh_ref[...].astype(bf16)
        w = w_ref[...].reshape(nsh * R, D)
        dm = _dot_nt(d, w)
        doa_ref[...] = dm[:, :da]
        dob_ref[...] = dm[:, da:da + db]
        doc_ref[...] = dm[:, da + db:]
        acc_ref[pl.ds(0, da), :] += _dot_tn(oa_ref[...], d)
        acc_ref[pl.ds(da, db), :] += _dot_tn(ob_ref[...], d)
        acc_ref[pl.ds(da + db, dc), :] += _dot_tn(oc_ref[...], d)

        @pl.when(i == nt - 1)
        def _():
            dw_ref[...] = acc_ref[...].astype(bf16).reshape(nsh, R, D)

    def tok(n):
        return pl.BlockSpec((tm, n), lambda i: (i, 0))

    wspec = _full((nsh, R, D))
    return pl.pallas_call(
        body, name="proj_out_bwd", grid=(nt,),
        in_specs=[tok(D), tok(da), tok(db), tok(dc), wspec],
        out_specs=(tok(da), tok(db), tok(dc), wspec),
        out_shape=(_sds((T, da), f32), _sds((T, db), f32), _sds((T, dc), f32), _sds((nsh, R, D), bf16)),
        scratch_shapes=[pltpu.VMEM((nsh * R, D), f32)],
        compiler_params=_cparams(1),
    )(dh, oa, ob, oc, w_out)


def _head_sum(m, n_heads):
    parts = []
    for hd in range(n_heads):
        s = jnp.sum(m[:, hd * HEAD:(hd + 1) * HEAD], axis=-1, keepdims=True)
        parts.append(jnp.broadcast_to(s, (m.shape[0], HEAD)))
    return parts[0] if n_heads == 1 else jnp.concatenate(parts, axis=1)


def _cat(parts, axis):
    return parts[0] if len(parts) == 1 else jnp.concatenate(parts, axis=axis)


def _hgrn_block(q, fl, iv, lb, states, tri, n_heads, n_inner):
    C = q.shape[0] // n_inner
    qs = _silu(q)
    forget = lb + (1.0 - lb) * jax.nn.sigmoid(fl)
    kk = 1.0 - forget
    logf = jnp.log(forget)
    b = jnp.dot(tri, logf, precision=HI, preferred_element_type=f32)
    vb = iv.astype(bf16)
    heads = [slice(hd * HEAD, (hd + 1) * HEAD) for hd in range(n_heads)]
    n_sub = C // A_SUB

    off, qe, kd, dec = {}, [], [], []
    for j in range(n_inner):
        c0 = j * C
        for blk in range(1, n_sub):
            lo = c0 + blk * A_SUB
            piv = b[lo:lo + 1]
            qt = (qs[lo:lo + A_SUB] * jnp.exp(b[lo:lo + A_SUB] - piv)).astype(bf16)
            kt = (kk[c0:lo] * jnp.exp(piv - b[c0:lo])).astype(bf16)
            parts = []
            for sl in heads:
                sc = _dot_nt(qt[:, sl], kt[:, sl])
                parts.append(_dot(sc.astype(bf16), vb[c0:lo, sl]))
            off[(j, blk)] = _cat(parts, 1)
        bj = b[c0:c0 + C]
        b_end = bj[C - 1:C]
        qe.append((qs[c0:c0 + C] * jnp.exp(bj)).astype(bf16))
        kd.append((kk[c0:c0 + C] * jnp.exp(b_end - bj)).astype(bf16))
        dec.append(jnp.exp(b_end))

    outs = []
    for j in range(n_inner):
        for blk in range(n_sub):
            lo = j * C + blk * A_SUB
            groups = [off[(j, blk)][r0:r0 + SUBLANES] if blk > 0 else None for r0 in range(0, A_SUB, SUBLANES)]
            for s in range(A_SUB):
                first = (s // SUBLANES) * SUBLANES
                n_rows = A_SUB - first
                row = lax.broadcasted_iota(jnp.int32, (n_rows, 1), 0) + first
                gate = jnp.where(row >= s, 0.0, -1e30)
                r = slice(lo + first, lo + A_SUB)
                m = qs[r] * jnp.exp((b[r] - b[lo + s:lo + s + 1]) + gate) * kk[lo + s:lo + s + 1]
                term = _head_sum(m, n_heads) * iv[lo + s:lo + s + 1]
                for gi in range(first // SUBLANES, A_SUB // SUBLANES):
                    piece = term[gi * SUBLANES - first:(gi + 1) * SUBLANES - first]
                    groups[gi] = piece if groups[gi] is None else groups[gi] + piece
            outs.extend(groups)
    o = jnp.concatenate(outs, axis=0)

    inter = []
    states = list(states)
    for j in range(n_inner):
        c0 = j * C
        parts = []
        for hd, sl in enumerate(heads):
            st = states[hd]
            parts.append(_dot_nt(qe[j][:, sl], st.astype(bf16)))
            states[hd] = dec[j][:, sl] * st + _dot_tn(vb[c0:c0 + C, sl], kd[j][:, sl])
        inter.append(_cat(parts, 1))
    return o + _cat(inter, 0), tuple(states)


def _hgrn_gate(o, g, gain, n_heads):
    ms = _head_sum(o * o, n_heads) * (1.0 / HEAD)
    return o * lax.rsqrt(ms + EPS) * gain * _silu(g)


def _tri_matrix(c, n_inner):
    idx = np.arange(c * n_inner)
    same = (idx[:, None] // c) == (idx[None, :] // c)
    return jnp.asarray((same & (idx[:, None] >= idx[None, :])).astype(np.float32))


def _hgrn_fwd(z, lb, gain, d_a):
    T = z.shape[0]
    C = A_CHUNK * A_INNER
    nc = T // C
    nh = d_a // HEAD
    tri = _tri_matrix(A_CHUNK, A_INNER)

    def body(q_ref, f_ref, i_ref, g_ref, lb_ref, gain_ref, tri_ref, out_ref, o_ref, st_ref, carry_ref):
        @pl.when(pl.program_id(0) == 0)
        def _():
            carry_ref[...] = jnp.zeros_like(carry_ref)

        states = tuple(carry_ref[hd] for hd in range(nh))
        st_ref[...] = carry_ref[...]
        o, new_states = _hgrn_block(q_ref[...], f_ref[...], i_ref[...], lb_ref[...], states, tri_ref[...], nh, A_INNER)
        o_ref[...] = o
        out_ref[...] = _hgrn_gate(o, g_ref[...], gain_ref[...], nh).astype(bf16)
        for hd in range(nh):
            carry_ref[hd] = new_states[hd]

    def col(j):
        return pl.BlockSpec((C, d_a), lambda c, j=j: (c, j))

    tok = pl.BlockSpec((C, d_a), lambda c: (c, 0))
    return pl.pallas_call(
        body, name="hgrn_fwd", grid=(nc,),
        in_specs=[col(0), col(1), col(2), col(3), _full((1, d_a)), _full((1, d_a)), _full((C, C))],
        out_specs=(tok, tok, pl.BlockSpec((None, nh, HEAD, HEAD), lambda c: (c, 0, 0, 0))),
        out_shape=(_sds((T, d_a), bf16), _sds((T, d_a), f32), _sds((nc, nh, HEAD, HEAD), f32)),
        scratch_shapes=[pltpu.VMEM((nh, HEAD, HEAD), f32)],
        compiler_params=_cparams(1),
    )(z, z, z, z, lb, gain, tri)


def _hgrn_bwd(z, lb, gain, o_pre, states, dout, d_a):
    T = z.shape[0]
    C = A_CHUNK * A_INNER
    nc = T // C
    nh = d_a // HEAD
    tri = _tri_matrix(A_CHUNK, A_INNER)

    def body(q_ref, f_ref, i_ref, g_ref, lb_ref, gain_ref, tri_ref, o_ref, st_ref, do_ref,
             dz_ref, dlb_ref, dgain_ref, carry_ref):
        @pl.when(pl.program_id(0) == 0)
        def _():
            carry_ref[...] = jnp.zeros_like(carry_ref)
            dlb_ref[...] = jnp.zeros_like(dlb_ref)
            dgain_ref[...] = jnp.zeros_like(dgain_ref)

        _, vjp_gate = jax.vjp(lambda o, g, gv: _hgrn_gate(o, g, gv, nh), o_ref[...], g_ref[...], gain_ref[...])
        d_o, dg, dgain = vjp_gate(do_ref[...])
        tri_v = tri_ref[...]

        def fn(q, fl, iv, lbv, sts):
            return _hgrn_block(q, fl, iv, lbv, sts, tri_v, nh, A_INNER)

        states_in = tuple(st_ref[hd] for hd in range(nh))
        _, vjp = jax.vjp(fn, q_ref[...], f_ref[...], i_ref[...], lb_ref[...], states_in)
        dstates = tuple(carry_ref[hd] for hd in range(nh))
        dq, df, di, dlb, dst = vjp((d_o, dstates))
        dz_ref[:, 0:d_a] = dq.astype(bf16)
        dz_ref[:, d_a:2 * d_a] = df.astype(bf16)
        dz_ref[:, 2 * d_a:3 * d_a] = di.astype(bf16)
        dz_ref[:, 3 * d_a:4 * d_a] = dg.astype(bf16)
        dlb_ref[...] += dlb
        dgain_ref[...] += dgain
        for hd in range(nh):
            carry_ref[hd] = dst[hd]

    def col(j):
        return pl.BlockSpec((C, d_a), lambda c, j=j: (nc - 1 - c, j))

    tok = pl.BlockSpec((C, d_a), lambda c: (nc - 1 - c, 0))
    return pl.pallas_call(
        body, name="hgrn_bwd", grid=(nc,),
        in_specs=[col(0), col(1), col(2), col(3), _full((1, d_a)), _full((1, d_a)), _full((C, C)), tok,
                  pl.BlockSpec((None, nh, HEAD, HEAD), lambda c: (nc - 1 - c, 0, 0, 0)), tok],
        out_specs=(pl.BlockSpec((C, 4 * d_a), lambda c: (nc - 1 - c, 0)), _full((1, d_a)), _full((1, d_a))),
        out_shape=(_sds(z.shape, bf16), _sds((1, d_a), f32), _sds((1, d_a), f32)),
        scratch_shapes=[pltpu.VMEM((nh, HEAD, HEAD), f32)],
        compiler_params=_cparams(1),
    )(z, z, z, z, lb, gain, tri, o_pre, states, dout)


def _one_minus_exp(x):
    series = -x * (1.0 + x * (0.5 + x * (1.0 / 6.0 + x * (1.0 / 24.0))))
    return jnp.where(x > -0.03, series, 1.0 - jnp.exp(x))


def _lru_pre(xc, wa, ba, wx, bx, lam):
    xb16 = xc.astype(bf16)
    r = jax.nn.sigmoid(_dot(xb16, wa.astype(bf16)) + ba)
    gi = jax.nn.sigmoid(_dot(xb16, wx.astype(bf16)) + bx)
    log_a = -LRU_C * r * jax.nn.softplus(-lam)
    a = jnp.exp(log_a)
    mult = jnp.sqrt(_one_minus_exp(2.0 * log_a))
    return a, mult * gi * xc


def _lru_post(h, gate, gain, avg):
    y = h * jax.nn.gelu(gate)
    ms = _group_mean(y * y, avg)
    return y * lax.rsqrt(ms + EPS) * gain


def _shift_down(x, d, prev):
    row = lax.broadcasted_iota(jnp.int32, x.shape, 0)
    return jnp.where(row >= d, pltpu.roll(x, d, 0), pltpu.roll(prev, d, 0))


def _shift_up(x, d, nxt):
    n = x.shape[0]
    row = lax.broadcasted_iota(jnp.int32, x.shape, 0)
    return jnp.where(row < n - d, pltpu.roll(x, n - d, 0), pltpu.roll(nxt, n - d, 0))


def _scan_rows(a, u, reverse):
    n = a.shape[0]
    row = lax.broadcasted_iota(jnp.int32, a.shape, 0)
    d = 1
    while d < n:
        shift, ok = (n - d, row < n - d) if reverse else (d, row >= d)
        su = jnp.where(ok, pltpu.roll(u, shift, 0), 0.0)
        sa = jnp.where(ok, pltpu.roll(a, shift, 0), 1.0)
        u = u + a * su
        a = a * sa
        d *= 2
    return a, u


def _conv(xb, xprev, cw, cb):
    xc = cb + cw[CONV_WIDTH - 1:CONV_WIDTH] * xb
    for d in range(1, CONV_WIDTH):
        xc = xc + cw[CONV_WIDTH - 1 - d:CONV_WIDTH - d] * _shift_down(xb, d, xprev)
    return xc


def _lru_fwd(z, col0, d_b, cw, cb, wa, ba, wx, bx, lam, gain, avg):
    T = z.shape[0]
    R = min(B_CHUNK, T)
    nr = T // R
    jb = col0 // d_b

    def body(xb_ref, gate_ref, cw_ref, cb_ref, wa_ref, ba_ref, wx_ref, bx_ref, lam_ref, gain_ref, avg_ref,
             out_ref, h_ref, xprev_ref, hprev_ref):
        @pl.when(pl.program_id(0) == 0)
        def _():
            xprev_ref[...] = jnp.zeros_like(xprev_ref)
            hprev_ref[...] = jnp.zeros_like(hprev_ref)

        xb = xb_ref[...]
        xc = _conv(xb, xprev_ref[...], cw_ref[...], cb_ref[...])
        a, u = _lru_pre(xc, wa_ref[...], ba_ref[...], wx_ref[...], bx_ref[...], lam_ref[...])
        acum, hl = _scan_rows(a, u, False)
        h = hl + acum * hprev_ref[R - 1:R, :]
        h_ref[...] = h
        out_ref[...] = _lru_post(h, gate_ref[...], gain_ref[...], avg_ref[...]).astype(bf16)
        xprev_ref[...] = xb
        hprev_ref[...] = h

    vec = _full((1, d_b))
    return pl.pallas_call(
        body, name="lru_fwd", grid=(nr,),
        in_specs=[pl.BlockSpec((R, d_b), lambda i: (i, jb)), pl.BlockSpec((R, d_b), lambda i: (i, jb + 1)),
                  _full((CONV_WIDTH, d_b)), vec, _full((d_b, d_b)), vec, _full((d_b, d_b)), vec, vec, vec, _full((d_b, d_b))],
        out_specs=(pl.BlockSpec((R, d_b), lambda i: (i, 0)), pl.BlockSpec((R, d_b), lambda i: (i, 0))),
        out_shape=(_sds((T, d_b), bf16), _sds((T, d_b), f32)),
        scratch_shapes=[pltpu.VMEM((R, d_b), f32), pltpu.VMEM((R, d_b), f32)],
        compiler_params=_cparams(1),
    )(z, z, cw, cb, wa, ba, wx, bx, lam, gain, avg)


def _lru_bwd(z, col0, d_b, hsave, dout, dz_buf, cw, cb, wa, ba, wx, bx, lam, gain, avg):
    T = z.shape[0]
    R = min(B_CHUNK, T)
    nr = T // R
    jb = col0 // d_b

    def body(xb_ref, xp_ref, gate_ref, h_ref, hp_ref, do_ref,
             cw_ref, cb_ref, wa_ref, ba_ref, wx_ref, bx_ref, lam_ref, gain_ref, avg_ref, dzin_ref,
             dz_ref, dcw_ref, dcb_ref, dwa_ref, dba_ref, dwx_ref, dbx_ref, dlam_ref, dgain_ref,
             gfirst_ref, afirst_ref, dxcn_ref):
        step = pl.program_id(0)
        first_in_time = step == nr - 1

        @pl.when(step == 0)
        def _():
            for r in (dcw_ref, dcb_ref, dwa_ref, dba_ref, dwx_ref, dbx_ref, dlam_ref, dgain_ref,
                      gfirst_ref, afirst_ref, dxcn_ref):
                r[...] = jnp.zeros_like(r)

        xb = xb_ref[...]
        keep = jnp.where(first_in_time, 0.0, 1.0)
        xprev = xp_ref[...] * keep
        hprev = hp_ref[...] * keep
        cw = cw_ref[...]
        xc = _conv(xb, xprev, cw, cb_ref[...])
        (a, _), vjp_pre = jax.vjp(_lru_pre, xc, wa_ref[...], ba_ref[...], wx_ref[...], bx_ref[...], lam_ref[...])
        h = h_ref[...]
        avg = avg_ref[...]
        _, vjp_post = jax.vjp(lambda hh, gg, gn: _lru_post(hh, gg, gn, avg), h, gate_ref[...], gain_ref[...])
        dh, dgate, dgain = vjp_post(do_ref[...])
        a_next = _shift_up(a, 1, jnp.broadcast_to(afirst_ref[0:1, :], a.shape))
        acum, gl = _scan_rows(a_next, dh, True)
        gtot = gl + acum * gfirst_ref[0:1, :]
        da = gtot * _shift_down(h, 1, hprev)
        dxc, dwa, dba, dwx, dbx, dlam = vjp_pre((da, gtot))
        dxcn = dxcn_ref[...]
        dxb = cw[CONV_WIDTH - 1:CONV_WIDTH] * dxc
        dcw_ref[CONV_WIDTH - 1:CONV_WIDTH, :] += jnp.sum(dxc * xb, axis=0, keepdims=True)
        for d in range(1, CONV_WIDTH):
            tap = CONV_WIDTH - 1 - d
            dxb = dxb + cw[tap:tap + 1] * _shift_up(dxc, d, dxcn)
            dcw_ref[tap:tap + 1, :] += jnp.sum(dxc * _shift_down(xb, d, xprev), axis=0, keepdims=True)
        dz_ref[:, 0:d_b] = dxb.astype(bf16)
        dz_ref[:, d_b:2 * d_b] = dgate.astype(bf16)
        dcb_ref[...] += jnp.sum(dxc, axis=0, keepdims=True)
        dwa_ref[...] += dwa
        dba_ref[...] += dba
        dwx_ref[...] += dwx
        dbx_ref[...] += dbx
        dlam_ref[...] += dlam
        dgain_ref[...] += dgain
        gfirst_ref[...] = jnp.broadcast_to(gtot[0:1, :], gfirst_ref.shape)
        afirst_ref[...] = jnp.broadcast_to(a[0:1, :], afirst_ref.shape)
        dxcn_ref[...] = dxc

    vec = _full((1, d_b))
    mat = _full((d_b, d_b))

    def cur(j):
        return pl.BlockSpec((R, d_b), lambda i, j=j: (nr - 1 - i, j))

    def prev(j):
        return pl.BlockSpec((R, d_b), lambda i, j=j: (jnp.maximum(nr - 2 - i, 0), j))

    return pl.pallas_call(
        body, name="lru_bwd", grid=(nr,),
        in_specs=[cur(jb), prev(jb), cur(jb + 1), cur(0), prev(0), cur(0),
                  _full((CONV_WIDTH, d_b)), vec, mat, vec, mat, vec, vec, vec, mat, ANY_SPEC],
        out_specs=(pl.BlockSpec((R, 2 * d_b), lambda i: (nr - 1 - i, col0 // (2 * d_b))), _full((CONV_WIDTH, d_b)), vec, mat, vec, mat, vec, vec, vec),
        out_shape=(_sds(dz_buf.shape, bf16), _sds((CONV_WIDTH, d_b), f32), _sds((1, d_b), f32), _sds((d_b, d_b), f32),
                   _sds((1, d_b), f32), _sds((d_b, d_b), f32), _sds((1, d_b), f32), _sds((1, d_b), f32), _sds((1, d_b), f32)),
        scratch_shapes=[pltpu.VMEM((8, d_b), f32), pltpu.VMEM((8, d_b), f32), pltpu.VMEM((R, d_b), f32)],
        input_output_aliases={15: 0},
        compiler_params=_cparams(1),
    )(z, z, z, hsave, hsave, dout, cw, cb, wa, ba, wx, bx, lam, gain, avg, dz_buf)


def _two_pass(x, m16):
    hi = x.astype(bf16)
    lo = (x - hi.astype(f32)).astype(bf16)
    return _dot(hi, m16) + _dot(lo, m16)


@jax.custom_vjp
def _group_mean(x, avg):
    return _two_pass(x, avg.astype(bf16))


def _group_mean_fwd(x, avg):
    return _group_mean(x, avg), avg


def _group_mean_bwd(avg, ct):
    return _two_pass(ct, avg.astype(bf16)), jnp.zeros_like(avg)


_group_mean.defvjp(_group_mean_fwd, _group_mean_bwd)


def _sgu_chunk(u_in, v_in, w, bexp, gain, avg, n_groups):
    C, d_c = u_in.shape
    gd = d_c // n_groups
    u = jax.nn.gelu(u_in)
    v = jax.nn.gelu(v_in)
    mu = _group_mean(v, avg)
    vc = v - mu
    var = _group_mean(vc * vc, avg)
    vh = (vc * lax.rsqrt(var + EPS)).astype(bf16)
    lane = lax.broadcasted_iota(jnp.int32, (1, d_c), 1)
    causal = lax.broadcasted_iota(jnp.int32, (C, C), 0) >= lax.broadcasted_iota(jnp.int32, (C, C), 1)
    zz = bexp
    for g in range(n_groups):
        wg = jnp.where(causal, w[g], 0.0).astype(bf16)
        zz = zz + jnp.where((lane >= g * gd) & (lane < (g + 1) * gd), _dot(wg, vh), 0.0)
    y = u * zz
    ms = _group_mean(y * y, avg)
    return y * lax.rsqrt(ms + EPS) * gain


def _sgu_inner(T):
    return C_INNER if T % (C_CHUNK * C_INNER) == 0 else 1


def _sgu_fwd(z, col0, d_c, w, bexp, gain, avg):
    T = z.shape[0]
    C = C_CHUNK
    n_in = _sgu_inner(T)
    R = C * n_in
    jb = col0 // d_c
    G = w.shape[0]

    def body(u_ref, v_ref, w_ref, b_ref, gain_ref, avg_ref, out_ref):
        w_v, b_v, gain_v, avg = w_ref[...], b_ref[...], gain_ref[...], avg_ref[...]
        for j in range(n_in):
            rows = pl.ds(j * C, C)
            out_ref[rows, :] = _sgu_chunk(u_ref[rows, :], v_ref[rows, :], w_v, b_v, gain_v, avg, G).astype(bf16)

    return pl.pallas_call(
        body, name="sgu_fwd", grid=(T // R,),
        in_specs=[pl.BlockSpec((R, d_c), lambda i: (i, jb)), pl.BlockSpec((R, d_c), lambda i: (i, jb + 1)),
                  _full((G, C, C)), _full((C, d_c)), _full((1, d_c)), _full((d_c, d_c))],
        out_specs=pl.BlockSpec((R, d_c), lambda i: (i, 0)),
        out_shape=_sds((T, d_c), bf16),
        compiler_params=_cparams(1),
    )(z, z, w, bexp, gain, avg)


def _sgu_bwd(z, col0, d_c, dout, dz_buf, w, bexp, gain, avg):
    T = z.shape[0]
    C = C_CHUNK
    n_in = _sgu_inner(T)
    R = C * n_in
    nc = T // R
    jb = col0 // d_c
    G = w.shape[0]
    gd = d_c // G

    def body(u_ref, v_ref, do_ref, w_ref, b_ref, gain_ref, avg_ref, dzin_ref, dz_ref, dw_ref, db_ref, dgain_ref, dbexp_ref):
        step = pl.program_id(0)

        @pl.when(step == 0)
        def _():
            dw_ref[...] = jnp.zeros_like(dw_ref)
            dgain_ref[...] = jnp.zeros_like(dgain_ref)
            dbexp_ref[...] = jnp.zeros_like(dbexp_ref)

        avg, w_v, b_v, gain_v = avg_ref[...], w_ref[...], b_ref[...], gain_ref[...]
        dw = dbexp = dgain = None
        for j in range(n_in):
            rows = pl.ds(j * C, C)
            _, vjp = jax.vjp(lambda a, b, c, d, e: _sgu_chunk(a, b, c, d, e, avg, G),
                             u_ref[rows, :], v_ref[rows, :], w_v, b_v, gain_v)
            du, dv, dw_j, dbexp_j, dgain_j = vjp(do_ref[rows, :])
            dz_ref[rows, 0:d_c] = du.astype(bf16)
            dz_ref[rows, d_c:2 * d_c] = dv.astype(bf16)
            dw = dw_j if dw is None else dw + dw_j
            dbexp = dbexp_j if dbexp is None else dbexp + dbexp_j
            dgain = dgain_j if dgain is None else dgain + dgain_j
        dw_ref[...] += dw
        dbexp_ref[...] += dbexp
        dgain_ref[...] += dgain

        @pl.when(step == nc - 1)
        def _():
            lane = lax.broadcasted_iota(jnp.int32, (1, d_c), 1)
            acc = dbexp_ref[...]
            for g in range(G):
                sel = jnp.where((lane >= g * gd) & (lane < (g + 1) * gd), acc, 0.0)
                db_ref[:, g:g + 1] = jnp.sum(sel, axis=1, keepdims=True)

    return pl.pallas_call(
        body, name="sgu_bwd", grid=(nc,),
        in_specs=[pl.BlockSpec((R, d_c), lambda i: (i, jb)), pl.BlockSpec((R, d_c), lambda i: (i, jb + 1)),
                  pl.BlockSpec((R, d_c), lambda i: (i, 0)),
                  _full((G, C, C)), _full((C, d_c)), _full((1, d_c)), _full((d_c, d_c)), ANY_SPEC],
        out_specs=(pl.BlockSpec((R, 2 * d_c), lambda i: (i, col0 // (2 * d_c))), _full((G, C, C)), _full((C, G)), _full((1, d_c))),
        out_shape=(_sds(dz_buf.shape, bf16), _sds((G, C, C), f32), _sds((C, G), f32), _sds((1, d_c), f32)),
        scratch_shapes=[pltpu.VMEM((C, d_c), f32)],
        input_output_aliases={7: 0},
        compiler_params=_cparams(1),
    )(z, z, dout, w, bexp, gain, avg, dz_buf)


def _loss_head(h, gain, target, tm):
    T, D = h.shape
    nt = T // tm

    def body(h_ref, gain_ref, tgt_ref, dh_ref, loss_ref, dgain_ref):
        @pl.when(pl.program_id(0) == 0)
        def _():
            loss_ref[...] = jnp.zeros_like(loss_ref)
            dgain_ref[...] = jnp.zeros_like(dgain_ref)

        hv = h_ref[...]
        gain_v = gain_ref[...]
        r = lax.rsqrt(jnp.mean(hv * hv, axis=-1, keepdims=True) + EPS)
        xh = hv * r
        e = xh * gain_v - tgt_ref[...]
        loss_ref[...] += 0.5 * jnp.sum(jnp.mean(e * e, axis=-1, keepdims=True), axis=0, keepdims=True)
        dy = e * (1.0 / D)
        dgain_ref[...] += jnp.sum(dy * xh, axis=0, keepdims=True)
        dxh = dy * gain_v
        dh_ref[...] = r * (dxh - xh * jnp.mean(dxh * xh, axis=-1, keepdims=True))

    tok = pl.BlockSpec((tm, D), lambda i: (i, 0))
    return pl.pallas_call(
        body, name="loss_head", grid=(nt,),
        in_specs=[tok, _full((1, D)), tok],
        out_specs=(tok, _full((1, 128)), _full((1, D))),
        out_shape=(_sds((T, D), f32), _sds((1, 128), f32), _sds((1, D), f32)),
        compiler_params=_cparams(1),
    )(h, gain, target)


def _lower_bounds_fn(logits):
    n = logits.shape[0]
    mx = jnp.max(logits, axis=0, keepdims=True)
    ex = jnp.exp(logits - mx)
    soft = ex / jnp.sum(ex, axis=0, keepdims=True)
    rows = [jnp.zeros_like(soft[0:1])]
    for l in range(1, n):
        rows.append(rows[-1] + soft[l:l + 1])
    return jnp.concatenate(rows, axis=0)


def _lower_bounds(logits):
    def body(x_ref, o_ref):
        o_ref[...] = _lower_bounds_fn(x_ref[...])

    return pl.pallas_call(body, name="lower_bounds", out_shape=_sds(logits.shape, f32))(logits)


def _lower_bounds_bwd(logits, dlb):
    def body(x_ref, d_ref, o_ref):
        _, vjp = jax.vjp(_lower_bounds_fn, x_ref[...])
        o_ref[...] = vjp(d_ref[...])[0]

    return pl.pallas_call(body, name="lower_bounds_bwd", out_shape=_sds(logits.shape, f32))(logits, dlb)


def _adamw(w, g, m, v, rows_blk):
    R, Cc = w.shape
    rb = R if R <= rows_blk else math.gcd(R, rows_blk)

    def body(w_ref, g_ref, m_ref, v_ref, d_ref, nm_ref, nv_ref):
        gv = g_ref[...]
        m2 = ADAM_B1 * m_ref[...] + (1.0 - ADAM_B1) * gv
        v2 = ADAM_B2 * v_ref[...] + (1.0 - ADAM_B2) * (gv * gv)
        m_hat = m2 / (1.0 - ADAM_B1 ** ADAM_STEP)
        v_hat = v2 / (1.0 - ADAM_B2 ** ADAM_STEP)
        d_ref[...] = -ADAM_LR * (m_hat / (jnp.sqrt(v_hat) + ADAM_EPS) + ADAM_WD * w_ref[...])
        nm_ref[...] = m2
        nv_ref[...] = v2

    spec = pl.BlockSpec((rb, Cc), lambda i: (i, 0))
    return pl.pallas_call(
        body, name="adamw", grid=(R // rb,),
        in_specs=[spec] * 4, out_specs=(spec,) * 3, out_shape=(_sds((R, Cc), f32),) * 3,
        compiler_params=_cparams(1),
    )(w, g, m, v)


def _pair_sum(grads, recv, c_arr):
    n = len(grads)
    nsh = grads[0].shape[0]

    def body(c_ref, *refs):
        for a in range(n):
            refs[2 * n + a][...] = (refs[a][...].astype(f32) + refs[n + a][...].astype(f32)).astype(bf16)

    g_specs, r_specs, out_shape = [], [], []
    for g in grads:
        _, R, Cc = g.shape
        r2 = R // 2
        g_specs.append(pl.BlockSpec((None, r2, Cc), lambda s, c: (s, c[0], 0)))
        r_specs.append(pl.BlockSpec((None, r2, Cc), lambda s, c: (s, 0, 0)))
        out_shape.append(_sds((nsh, r2, Cc), bf16))
    gs = pltpu.PrefetchScalarGridSpec(num_scalar_prefetch=1, grid=(nsh,), in_specs=g_specs + r_specs, out_specs=tuple(r_specs))
    return list(pl.pallas_call(body, name="pair_sum", grid_spec=gs, out_shape=tuple(out_shape),
                               compiler_params=_cparams(1))(c_arr, *grads, *recv))


def _add(a, b):
    def body(a_ref, b_ref, o_ref):
        o_ref[...] = a_ref[...] + b_ref[...]

    return pl.pallas_call(body, name="pair_sum_small", out_shape=_sds(a.shape, f32))(a, b)


def _chip_sum(hsum, recv, bufs, slot_arr, c_arr, layer, n_layers):
    n = len(hsum)
    prev = list(bufs)
    steps = 2

    def body(s_ref, c_ref, *refs):
        outs = refs[len(refs) - n:]
        for a in range(n):
            acc = refs[a][...].astype(f32)
            for j in range(N_CHIPS - 1):
                acc = acc + refs[n + a][j].astype(f32)
            outs[a][...] = acc

    h_specs, r_specs, o_specs, out_shape = [], [], [], []
    for hh in hsum:
        _, r2, Cc = hh.shape
        rt = r2 // steps
        h_specs.append(pl.BlockSpec((None, rt, Cc), lambda i, s, c: (s[0], i, 0)))
        r_specs.append(pl.BlockSpec((N_CHIPS - 1, rt, Cc), lambda i, s, c: (0, i, 0)))
        o_specs.append(pl.BlockSpec((None, rt, Cc), lambda i, s, c: (layer, c[0] * steps + i, 0)))
        out_shape.append(_sds((n_layers, 2 * r2, Cc), f32))
    gs = pltpu.PrefetchScalarGridSpec(num_scalar_prefetch=2, grid=(steps,),
                                      in_specs=h_specs + r_specs + [ANY_SPEC] * len(prev), out_specs=tuple(o_specs))
    return list(pl.pallas_call(body, name="chip_sum", grid_spec=gs, out_shape=tuple(out_shape),
                               input_output_aliases={2 + 2 * n + a: a for a in range(len(prev))},
                               compiler_params=_cparams(1))(slot_arr, c_arr, *hsum, *recv, *prev))


def _sum_slots(x):
    def body(x_ref, o_ref):
        acc = x_ref[0]
        for j in range(1, x.shape[0]):
            acc = acc + x_ref[j]
        o_ref[...] = acc

    return pl.pallas_call(body, name="sum_slots", out_shape=_sds(x.shape[1:], f32))(x)


def _blockdiag(w):
    nb, bd, _ = w.shape
    eye = jnp.eye(nb, dtype=w.dtype)
    return (eye[:, None, :, None] * w[:, :, None, :]).reshape(nb * bd, nb * bd)


def _blockdiag_extract(dense, nb):
    bd = dense.shape[0] // nb
    d4 = dense.reshape(nb, bd, nb, bd)
    return jnp.stack([d4[i, :, i, :] for i in range(nb)])


def _pack(arrays, multiple):
    flat = jnp.concatenate([a.reshape(-1).astype(f32) for a in arrays])
    pad = (-flat.shape[0]) % multiple
    return jnp.pad(flat, (0, pad))


def _unpack(flat, shapes):
    out, off = [], 0
    for s in shapes:
        n = int(np.prod(s))
        out.append(flat[off:off + n].reshape(s))
        off += n
    return out


BIG = ("ffn1_wg", "ffn1_wu", "ffn1_wd", "w_in", "w_out", "ffn2_wg", "ffn2_wu", "ffn2_wd")
SMALL = ("ffn1_norm", "mix_norm", "hgrn_lb_logits", "hgrn_norm", "conv_w", "conv_b", "lru_wa", "lru_ba", "lru_wx",
         "lru_bx", "lru_lambda", "lru_norm", "sgu_w", "sgu_b", "sgu_norm", "ffn2_norm", "final_norm")
WEIGHTS = ("ffn1_norm", "ffn1_wg", "ffn1_wu", "ffn1_wd", "mix_norm", "w_in", "hgrn_lb_logits", "hgrn_norm", "conv_w",
           "conv_b", "lru_wa", "lru_ba", "lru_wx", "lru_bx", "lru_lambda", "lru_norm", "sgu_w", "sgu_b", "sgu_norm",
           "w_out", "ffn2_norm", "ffn2_wg", "ffn2_wu", "ffn2_wd", "final_norm")


def kernel(x, ffn1_norm, ffn1_wg, ffn1_wu, ffn1_wd, mix_norm, w_in, hgrn_lb_logits, hgrn_norm, conv_w, conv_b, lru_wa, lru_ba, lru_wx, lru_bx, lru_lambda, lru_norm, sgu_w, sgu_b, sgu_norm, w_out, ffn2_norm, ffn2_wg, ffn2_wu, ffn2_wd, final_norm, loss_target, m_ffn1_norm, m_ffn1_wg, m_ffn1_wu, m_ffn1_wd, m_mix_norm, m_w_in, m_hgrn_lb_logits, m_hgrn_norm, m_conv_w, m_conv_b, m_lru_wa, m_lru_ba, m_lru_wx, m_lru_bx, m_lru_lambda, m_lru_norm, m_sgu_w, m_sgu_b, m_sgu_norm, m_w_out, m_ffn2_norm, m_ffn2_wg, m_ffn2_wu, m_ffn2_wd, m_final_norm, v_ffn1_norm, v_ffn1_wg, v_ffn1_wu, v_ffn1_wd, v_mix_norm, v_w_in, v_hgrn_lb_logits, v_hgrn_norm, v_conv_w, v_conv_b, v_lru_wa, v_lru_ba, v_lru_wx, v_lru_bx, v_lru_lambda, v_lru_norm, v_sgu_w, v_sgu_b, v_sgu_norm, v_w_out, v_ffn2_norm, v_ffn2_wg, v_ffn2_wu, v_ffn2_wd, v_final_norm):
    args = dict(locals())
    W = {n: args[n] for n in WEIGHTS}
    M = {n: args["m_" + n] for n in WEIGHTS}
    V = {n: args["v_" + n] for n in WEIGHTS}

    T, D = x.shape[1], x.shape[2]
    L = ffn1_norm.shape[0]
    d_a, d_b, d_c = hgrn_norm.shape[1], lru_norm.shape[1], sgu_norm.shape[1]
    col_b, col_c = 4 * d_a, 4 * d_a + 2 * d_b
    tm = 512 if T % 512 == 0 else T
    my_c = lax.axis_index("c")
    my_slot = 2 * lax.axis_index("x") + lax.axis_index("y")
    c_arr = jnp.reshape(my_c, (1,)).astype(jnp.int32)
    slot_arr = jnp.reshape(my_slot, (1,)).astype(jnp.int32)

    nb = len(BIG)
    gplan = _gather_ici_plan(nb)
    place_steps = 4 if all(W[n].shape[1] % 64 == 0 for n in BIG) else 2

    def placed(l):
        return _cast_place([W[n] for n in BIG], l, slot_arr, place_steps)

    conv_land = lax.dynamic_update_slice_in_dim(jnp.zeros((N_CHIPS,) + conv_w.shape, f32), conv_w[None], my_slot, axis=0)
    lands0 = placed(0)
    n_first = 3
    first = lands0[:n_first] + [conv_land]
    got = _exchange("gather0_ici", first, [_sds(a.shape, a.dtype) for a in first], _gather_ici_plan(n_first + 1),
                    aliases={a: a for a in range(n_first + 1)})
    got = _gather_d2d("gather0_d2d", got)
    G = [None] * L
    G[0] = dict(zip(BIG[:n_first], got[:n_first]))
    conv_full = jnp.transpose(got[n_first], (1, 2, 0, 3)).reshape(L, CONV_WIDTH, d_b)
    rest_plan = _gather_ici_plan(nb - n_first)
    rest_pending = _start_copies("gather_start_0", [], lands0[n_first:], rest_plan, got[0])

    def start_gather(l, after):
        return _start_copies(f"gather_start_{l}", [], placed(l), gplan, after)

    def finish_gather(l, pending, after):
        send, recv, _, lands, _ = pending
        lands = _wait_copies(f"gather_wait_{l}", send, recv, [], lands, gplan, after)
        return dict(zip(BIG, _gather_d2d("gather_d2d", lands)))

    lb = _lower_bounds(hgrn_lb_logits)
    avg_b = _group_avg_matrix(d_b, d_b // B_BLOCKS)
    avg_c = _group_avg_matrix(d_c, d_c // C_GROUPS)
    wa_dense = [_blockdiag(lru_wa[l]) for l in range(L)]
    wx_dense = [_blockdiag(lru_wx[l]) for l in range(L)]
    bexp = [jnp.repeat(sgu_b[l].T, d_c // C_GROUPS, axis=1) for l in range(L)]

    def lru_params(l):
        return (conv_full[l], conv_b[l][None], wa_dense[l], lru_ba[l].reshape(1, d_b), wx_dense[l],
                lru_bx[l].reshape(1, d_b), lru_lambda[l][None], lru_norm[l][None], avg_b)

    h = x.reshape(T, D)
    saved = []
    for l in range(L):
        s = {"h0": h}
        gain1, gain_mix = ffn1_norm[l][None], mix_norm[l][None]
        pending = None
        if l == 0:
            gain1 = gain1 + rest_pending[4][0:1, 0:1]
        elif l + 1 < L:
            pending = start_gather(l + 1, h)
            gain1 = gain1 + pending[4][0:1, 0:1]
        g = G[l]
        h, s["g1"], s["u1"] = _ffn_fwd(h, gain1, g["ffn1_wg"], g["ffn1_wu"], g["ffn1_wd"], tm)
        s["h1"] = h
        if l == 0:
            send, recv, _, lands, _ = rest_pending
            lands = _wait_copies("gather_wait_0", send, recv, [], lands, rest_plan, h)
            g.update(zip(BIG[n_first:], _gather_d2d("gather_d2d", lands)))
            if L > 1:
                pending = start_gather(1, g["w_in"])
                gain_mix = gain_mix + pending[4][0:1, 0:1]
        z = _proj_in_fwd(h, gain_mix, g["w_in"], tm)
        s["z"] = z
        s["oa"], s["o_pre"], s["states"] = _hgrn_fwd(z, lb[l][None], hgrn_norm[l][None], d_a)
        s["ob"], s["hl"] = _lru_fwd(z, col_b, d_b, *lru_params(l))
        s["oc"] = _sgu_fwd(z, col_c, d_c, sgu_w[l], bexp[l], sgu_norm[l][None], avg_c)
        h = _proj_out_fwd(h, s["oa"], s["ob"], s["oc"], g["w_out"], tm)
        s["h2"] = h
        h, s["g2"], s["u2"] = _ffn_fwd(h, ffn2_norm[l][None], g["ffn2_wg"], g["ffn2_wu"], g["ffn2_wd"], tm)
        saved.append(s)
        if pending is not None:
            G[l + 1] = finish_gather(l + 1, pending, h)

    dh, loss_part, d_final = _loss_head(h, final_norm[None], loss_target.reshape(T, D), tm)
    loss = lax.psum(loss_part[0, 0], ("x", "y", "c"))

    def pair_views(n_big):
        r = [(lambda i, o, p, a=a: i[a].at[:, pl.ds((1 - p.c) * (i[a].shape[1] // 2), i[a].shape[1] // 2)],
              lambda i, o, p, a=a: o[a], "sib") for a in range(n_big)]
        return r

    def chip_plan_for(n):
        return [(lambda s_, o, p, a=a, kind=kind: s_[a].at[p.peer_slot(kind)], lambda s_, o, p, a=a, j=j: o[a].at[j], kind)
                for a in range(n) for j, kind in enumerate(CHIP_KINDS)]

    chip_plan = chip_plan_for(nb)
    sbufs = {n: None for n in BIG}

    def pair_phase(arrs, extra=None):
        n = len(arrs)
        ins, remote = list(arrs), pair_views(n)
        outs = [_sds((N_CHIPS, a.shape[1] // 2, a.shape[2]), bf16) for a in arrs]
        if extra is not None:
            ins.append(extra)
            outs.append(_sds(extra.shape, f32))
            remote = remote + [(lambda i, o, p: i[n], lambda i, o, p: o[n], "sib")]
        recv = _exchange("grad_pair_d2d", ins, outs, remote)
        return _pair_sum(arrs, recv[:n], c_arr), (None if extra is None else _add(extra, recv[n]))

    def chip_sum_into(names, hs, lands, l):
        prev = [sbufs[n] for n in names] if sbufs[names[0]] is not None else []
        for n, buf in zip(names, _chip_sum(hs, lands, prev, slot_arr, c_arr, l, L)):
            sbufs[n] = buf

    def share(l, extra_in=(), extra_out=(), extra_remote=(), extra_local=()):
        remote = [(lambda i, o, p, a=a: _half(o[a].at[l], p.c), lambda i, o, p, a=a: _half(o[a].at[l], p.c), "sib")
                  for a in range(nb)]
        outs = [_sds(sbufs[n].shape, f32) for n in BIG] + list(extra_out)
        res = _exchange("grad_share_d2d", [sbufs[n] for n in BIG] + list(extra_in), outs, remote + list(extra_remote),
                        list(extra_local), aliases={a: a for a in range(nb)})
        for n, buf in zip(BIG, res[:nb]):
            sbufs[n] = buf
        return res[nb:]

    small = {n: [None] * L for n in SMALL if n != "final_norm"}
    chip_pending = early = None
    early_names = ("w_in", "w_out", "ffn2_wg", "ffn2_wu", "ffn2_wd")
    for l in reversed(range(L)):
        s, g = saved[l], G[l]
        gain2 = ffn2_norm[l][None]
        if chip_pending is not None:
            gain2 = gain2 + chip_pending[0][4][0:1, 0:1]
        dh, small["ffn2_norm"][l], dg, du, xn, dob = _ffn_bwd_dgrad(
            s["h2"], gain2, dh, s["g2"], s["u2"], g["ffn2_wg"], g["ffn2_wu"], g["ffn2_wd"], tm)
        dwg2, dwu2, dwd2 = _ffn_bwd_wgrad(xn, dob, s["g2"], s["u2"], dg, du, tm)
        doa, dob_, doc, dwo = _proj_out_bwd(dh, s["oa"], s["ob"], s["oc"], g["w_out"], tm)
        dz, small["hgrn_lb_logits"][l], small["hgrn_norm"][l] = _hgrn_bwd(
            s["z"], lb[l][None], hgrn_norm[l][None], s["o_pre"], s["states"], doa, d_a)
        (dz, small["conv_w"][l], small["conv_b"][l], dwa, small["lru_ba"][l], dwx, small["lru_bx"][l],
         small["lru_lambda"][l], small["lru_norm"][l]) = _lru_bwd(s["z"], col_b, d_b, s["hl"], dob_, dz, *lru_params(l))
        small["lru_wa"][l] = _blockdiag_extract(dwa, B_BLOCKS)
        small["lru_wx"][l] = _blockdiag_extract(dwx, B_BLOCKS)
        dz, small["sgu_w"][l], dsb, small["sgu_norm"][l] = _sgu_bwd(
            s["z"], col_c, d_c, doc, dz, sgu_w[l], bexp[l], sgu_norm[l][None], avg_c)
        small["sgu_b"][l] = dsb.T
        dh, small["mix_norm"][l], xn = _proj_in_bwd_dgrad(s["h1"], mix_norm[l][None], dh, dz, g["w_in"], tm)
        dwi = _proj_in_bwd_wgrad(xn, dz, N_CHIPS, tm)
        gain1 = ffn1_norm[l][None]
        if l == 0:
            hs_e, _ = pair_phase([dwi, dwo, dwg2, dwu2, dwd2])
            lands = [lax.empty((N_CHIPS - 1,) + hh.shape[1:], bf16) for hh in hs_e]
            early = (_start_copies("grad_chip_start_0", hs_e, lands, chip_plan_for(len(hs_e)), hs_e[0]), hs_e)
            gain1 = gain1 + early[0][4][0:1, 0:1]
        dh, small["ffn1_norm"][l], dg, du, xn, dob = _ffn_bwd_dgrad(
            s["h0"], gain1, dh, s["g1"], s["u1"], g["ffn1_wg"], g["ffn1_wu"], g["ffn1_wd"], tm)
        dwg1, dwu1, dwd1 = _ffn_bwd_wgrad(xn, dob, s["g1"], s["u1"], dg, du, tm)
        layer_grads = [dwg1, dwu1, dwd1, dwi, dwo, dwg2, dwu2, dwd2]

        if chip_pending is not None:
            (send, recv, hs, lands, _), hsum_prev = chip_pending
            lands = _wait_copies(f"grad_chip_wait_{l + 1}", send, recv, hs, lands, chip_plan, dwg1)
            chip_sum_into(BIG, hsum_prev, lands, l + 1)
            share(l + 1)
            chip_pending = None
        if l > 0:
            hsum, _ = pair_phase(layer_grads)
            lands = [lax.empty((N_CHIPS - 1,) + hh.shape[1:], bf16) for hh in hsum]
            chip_pending = (_start_copies(f"grad_chip_start_{l}", hsum, lands, chip_plan, hsum[0]), hsum)
    grad_x = dh.reshape(x.shape)

    (send, recv, hs, lands, _), hs_e = early
    lands = _wait_copies("grad_chip_wait_0", send, recv, hs, lands, chip_plan_for(len(hs_e)), dwg1)
    chip_sum_into(early_names, hs_e, lands, 0)
    small_names = [n for n in SMALL]
    small_parts = [jnp.stack([jnp.reshape(v, (-1,)) for v in small[n]]) if n != "final_norm" else d_final for n in small_names]
    small_shapes = [p.shape for p in small_parts]
    packed = _pack(small_parts, 2 * 8 * 128).reshape(2, -1, 128)
    n_rows = packed.shape[1]
    late = [dwg1, dwu1, dwd1]
    nl = len(late)
    hsum, small_pair = pair_phase(late, packed)
    remote = chip_plan_for(nl) + [(lambda i, o, p: i[nl].at[p.c], lambda i, o, p: o[nl].at[p.slot], kind) for kind in CHIP_KINDS]
    local = [(lambda i, o, p: i[nl].at[p.c], lambda i, o, p: o[nl].at[p.slot])]
    outs = [_sds((N_CHIPS - 1,) + hh.shape[1:], bf16) for hh in hsum] + [_sds((N_CHIPS, n_rows, 128), f32)]
    recv_b = _exchange("grad_chip_ici_last", hsum + [small_pair], outs, remote, local)
    chip_sum_into(BIG[:nl], hsum, recv_b[:nl], 0)
    small_half = _sum_slots(recv_b[nl])
    (small_all,) = share(0, extra_in=[small_half], extra_out=[_sds(packed.shape, f32)],
                         extra_remote=[(lambda i, o, p: i[nb], lambda i, o, p: o[nb].at[p.c], "sib")],
                         extra_local=[(lambda i, o, p: i[nb], lambda i, o, p: o[nb].at[p.c])])
    grads = {n: sbufs[n].reshape(W[n].shape) for n in BIG}
    small_tot = _unpack(small_all.reshape(-1), small_shapes)
    for n, val in zip(small_names, small_tot):
        grads[n] = val
    grads["hgrn_lb_logits"] = _lower_bounds_bwd(hgrn_lb_logits, grads["hgrn_lb_logits"])
    shard_cols = conv_w.shape[2]
    grads["conv_w"] = lax.dynamic_slice_in_dim(grads["conv_w"].reshape(L, CONV_WIDTH, d_b), my_slot * shard_cols, shard_cols, axis=2)
    for n in SMALL:
        grads[n] = grads[n].reshape(W[n].shape)

    delta, new_m, new_v = {}, {}, {}
    for n in BIG:
        cols = W[n].shape[-1]
        d2, m2, v2 = _adamw(W[n].reshape(-1, cols), grads[n].reshape(-1, cols), M[n].reshape(-1, cols), V[n].reshape(-1, cols), 512)
        delta[n], new_m[n], new_v[n] = d2.reshape(W[n].shape), m2.reshape(W[n].shape), v2.reshape(W[n].shape)
    shapes = [W[n].shape for n in SMALL]
    packs = [_pack([src[n] for n in SMALL], 8 * 128).reshape(-1, 128) for src in (W, grads, M, V)]
    d2, m2, v2 = _adamw(*packs, 4096)
    for dst, val in ((delta, d2), (new_m, m2), (new_v, v2)):
        for n, piece in zip(SMALL, _unpack(val.reshape(-1), shapes)):
            dst[n] = piece

    return (loss, grad_x, *[grads[n] for n in WEIGHTS], *[delta[n] for n in WEIGHTS],
            *[new_m[n] for n in WEIGHTS], *[new_v[n] for n in WEIGHTS])
```

```python
import math

import numpy as np
import jax
import jax.numpy as jnp
from jax import lax
from jax.experimental import pallas as pl
from jax.experimental.pallas import tpu as pltpu

f32 = jnp.float32
bf16 = jnp.bfloat16
HI = lax.Precision.HIGHEST
MESH = pl.DeviceIdType.MESH

EPS = 1e-6
HEAD = 128
A_CHUNK = 64
A_SUB = 16
A_INNER = 2
SUBLANES = 8
B_BLOCKS = 4
B_CHUNK = 256
CONV_WIDTH = 4
LRU_C = 8.0
C_GROUPS = 4
C_CHUNK = 128
C_INNER = 4
N_CHIPS = 4
ADAM_LR, ADAM_B1, ADAM_B2, ADAM_EPS, ADAM_WD, ADAM_STEP = 0.001, 0.9, 0.999, 1e-08, 0.01, 10
VMEM_LIMIT = 56 * 1024 * 1024


def _cparams(n_axes):
    return pltpu.CompilerParams(dimension_semantics=("arbitrary",) * n_axes, vmem_limit_bytes=VMEM_LIMIT)


def _sds(shape, dtype):
    return jax.ShapeDtypeStruct(tuple(shape), dtype)


def _full(shape):
    n = len(shape)
    return pl.BlockSpec(tuple(shape), lambda *_: (0,) * n)


def _resident(shape):
    n = len(shape)
    return pl.BlockSpec(tuple(shape), lambda *_: (0,) * n, pipeline_mode=pl.Buffered(1))


def _dot(a, b):
    return jnp.dot(a, b, preferred_element_type=f32)


def _dot_nt(a, b):
    return lax.dot_general(a, b, (((1,), (1,)), ((), ())), preferred_element_type=f32)


def _dot_tn(a, b):
    return lax.dot_general(a, b, (((0,), (0,)), ((), ())), preferred_element_type=f32)


def _silu(x):
    return x * jax.nn.sigmoid(x)


def _group_avg_matrix(n, group):
    idx = np.arange(n) // group
    return jnp.asarray((idx[:, None] == idx[None, :]).astype(np.float32) / group)


class _Place:
    def __init__(self):
        self.x, self.y, self.c = lax.axis_index("x"), lax.axis_index("y"), lax.axis_index("c")
        self.slot = 2 * self.x + self.y

    def peer(self, kind):
        x, y, c = self.x, self.y, self.c
        return {"sib": (x, y, 1 - c), "fx": (1 - x, y, c), "fy": (x, 1 - y, c), "fxy": (1 - x, 1 - y, c)}[kind]

    def peer_slot(self, kind):
        x, y = self.x, self.y
        return {"fx": 2 * (1 - x) + y, "fy": 2 * x + (1 - y), "fxy": 2 * (1 - x) + (1 - y)}[kind]


CHIP_KINDS = ("fx", "fy", "fxy")


def _exchange(name, ins, outs, remote, local=(), aliases=None):
    n_in, n_out, n_r, n_l = len(ins), len(outs), len(remote), len(local)

    def body(*refs):
        in_refs, out_refs = refs[:n_in], refs[n_in:n_in + n_out]
        send, recv, lsem = refs[n_in + n_out:]
        p = _Place()
        lcopies = []
        for t, (src, dst) in enumerate(local):
            cp = pltpu.make_async_copy(src(in_refs, out_refs, p), dst(in_refs, out_refs, p), lsem.at[t])
            cp.start()
            lcopies.append(cp)
        copies = []
        for t, (src, dst, kind) in enumerate(remote):
            cp = pltpu.make_async_remote_copy(
                src_ref=src(in_refs, out_refs, p), dst_ref=dst(in_refs, out_refs, p),
                send_sem=send.at[t], recv_sem=recv.at[t], device_id=p.peer(kind), device_id_type=MESH)
            cp.start()
            copies.append(cp)
        for cp in copies:
            cp.wait_recv()
        for cp in copies:
            cp.wait_send()
        for cp in lcopies:
            cp.wait()

    anyspec = pl.BlockSpec(memory_space=pl.ANY)
    res = pl.pallas_call(
        body, name=name, out_shape=tuple(outs),
        in_specs=[anyspec] * n_in, out_specs=tuple([anyspec] * n_out),
        scratch_shapes=[pltpu.SemaphoreType.DMA((n_r,)), pltpu.SemaphoreType.DMA((n_r,)),
                        pltpu.SemaphoreType.DMA((max(n_l, 1),))],
        input_output_aliases=aliases or {},
        compiler_params=pltpu.CompilerParams(has_side_effects=True),
    )(*ins)
    return list(res)


HBM_SPEC = pl.BlockSpec(memory_space=pltpu.HBM)
SEM_SPEC = pl.BlockSpec(memory_space=pltpu.SEMAPHORE)
ANY_SPEC = pl.BlockSpec(memory_space=pl.ANY)
DATAFLOW = pltpu.SideEffectType.DATAFLOW_SIDE_EFFECTING


def _in_hbm(a):
    return pltpu.with_memory_space_constraint(a, pltpu.HBM)


def _start_copies(name, srcs, lands, remote, after):
    n_s, n_l, n_r = len(srcs), len(lands), len(remote)

    def body(*refs):
        src_refs, land_refs = refs[:n_s], refs[n_s:n_s + n_l]
        send, recv = refs[n_s + n_l + 1], refs[n_s + n_l + 2]
        token = refs[-1]
        p = _Place()
        for t, (src, dst, kind) in enumerate(remote):
            pltpu.make_async_remote_copy(
                src_ref=src(src_refs, land_refs, p), dst_ref=dst(src_refs, land_refs, p),
                send_sem=send.at[t], recv_sem=recv.at[t], device_id=p.peer(kind), device_id_type=MESH).start()
        token[...] = jnp.zeros_like(token)

    thru = [pltpu.HBM(a.shape, a.dtype) for a in list(srcs) + list(lands)]
    res = pl.pallas_call(
        body, name=name,
        out_shape=(pltpu.SemaphoreType.DMA((n_r,)), pltpu.SemaphoreType.DMA((n_r,)), *thru, _sds((8, 128), f32)),
        in_specs=[HBM_SPEC] * (n_s + n_l) + [ANY_SPEC],
        out_specs=(SEM_SPEC, SEM_SPEC, *([HBM_SPEC] * (n_s + n_l)), pl.BlockSpec(memory_space=pltpu.VMEM)),
        input_output_aliases={i: 2 + i for i in range(n_s + n_l)},
        compiler_params=pltpu.CompilerParams(has_side_effects=DATAFLOW),
    )(*[_in_hbm(a) for a in srcs], *[_in_hbm(a) for a in lands], after)
    return res[0], res[1], list(res[2:2 + n_s]), list(res[2 + n_s:2 + n_s + n_l]), res[-1]


def _wait_copies(name, send, recv, srcs, lands, remote, after):
    n_s, n_l = len(srcs), len(lands)

    def body(*refs):
        src_refs, land_refs = refs[:n_s], refs[n_s:n_s + n_l]
        send_ref, recv_ref = refs[n_s + n_l], refs[n_s + n_l + 1]
        p = _Place()
        for t, (src, dst, kind) in enumerate(remote):
            cp = pltpu.make_async_remote_copy(
                src_ref=src(src_refs, land_refs, p), dst_ref=dst(src_refs, land_refs, p),
                send_sem=send_ref.at[t], recv_sem=recv_ref.at[t], device_id=p.peer(kind), device_id_type=MESH)
            cp.wait_send()
            cp.wait_recv()

    thru = [pltpu.HBM(a.shape, a.dtype) for a in list(srcs) + list(lands)]
    res = pl.pallas_call(
        body, name=name, out_shape=tuple(thru),
        in_specs=[HBM_SPEC] * (n_s + n_l) + [SEM_SPEC, SEM_SPEC, ANY_SPEC],
        out_specs=tuple([HBM_SPEC] * (n_s + n_l)),
        input_output_aliases={i: i for i in range(n_s + n_l)},
        compiler_params=pltpu.CompilerParams(has_side_effects=DATAFLOW),
    )(*srcs, *lands, send, recv, after)
    return list(res[n_s:])


def _half(ref, c):
    n2 = ref.shape[0] // 2
    return ref.at[pl.ds(c * n2, n2)]


def _gather_ici_plan(n):
    def view(a):
        return lambda s, o, p: _half(o[a].at[p.slot], p.c)

    return [(view(a), view(a), kind) for a in range(n) for kind in CHIP_KINDS]


def _gather_d2d(name, lands):
    n = len(lands)
    remote = []
    for a in range(n):
        for kind in CHIP_KINDS:
            view = lambda i, o, p, a=a, kind=kind: _half(o[a].at[p.peer_slot(kind)], p.c)
            remote.append((view, view, "sib"))
    outs = [_sds(g.shape, g.dtype) for g in lands]
    return _exchange(name, list(lands), outs, remote, aliases={a: a for a in range(n)})


def _cast_place(weights, layer, slot_arr, n_steps=4):
    def body(s_ref, *refs):
        n = len(refs) // 2
        for a in range(n):
            refs[n + a][...] = refs[a][...].astype(bf16)

    in_specs, out_specs, out_shape = [], [], []
    for w in weights:
        _, R, Cc = w.shape
        rt = R // n_steps
        in_specs.append(pl.BlockSpec((None, rt, Cc), lambda i, s: (layer, i, 0)))
        out_specs.append(pl.BlockSpec((None, rt, Cc), lambda i, s: (s[0], i, 0)))
        out_shape.append(_sds((N_CHIPS, R, Cc), bf16))
    gs = pltpu.PrefetchScalarGridSpec(num_scalar_prefetch=1, grid=(n_steps,), in_specs=in_specs, out_specs=tuple(out_specs))
    return list(pl.pallas_call(body, name="cast_place", grid_spec=gs, out_shape=tuple(out_shape),
                               compiler_params=_cparams(1))(slot_arr, *weights))


def _ffn_fwd(h, gain, wg, wu, wd, tm):
    T, D = h.shape
    nsh, F = wg.shape[0], wg.shape[2]
    nt = T // tm

    def body(h_ref, gain_ref, wg_ref, wu_ref, wd_ref, out_ref, gs_ref, us_ref):
        hv = h_ref[...]
        r = lax.rsqrt(jnp.mean(hv * hv, axis=-1, keepdims=True) + EPS)
        xn = (hv * r * gain_ref[...]).astype(bf16)
        acc = None
        for k in range(nsh):
            g = _dot(xn, wg_ref[k])
            u = _dot(xn, wu_ref[k])
            gs_ref[k] = g.astype(bf16)
            us_ref[k] = u.astype(bf16)
            part = _dot((_silu(g) * u).astype(bf16), wd_ref[k])
            acc = part if acc is None else acc + part
        out_ref[...] = hv + 0.5 * acc

    sav = pl.BlockSpec((nsh, tm, F), lambda i: (0, i, 0))
    return pl.pallas_call(
        body, name="ffn_fwd", grid=(nt,),
        in_specs=[pl.BlockSpec((tm, D), lambda i: (i, 0)), _full((1, D)), _resident((nsh, D, F)), _resident((nsh, D, F)),
                  _resident((nsh, F, D))],
        out_specs=(pl.BlockSpec((tm, D), lambda i: (i, 0)), sav, sav),
        out_shape=(_sds((T, D), f32), _sds((nsh, T, F), bf16), _sds((nsh, T, F), bf16)),
        compiler_params=_cparams(1),
    )(h, gain, wg, wu, wd)


def _ffn_bwd_dgrad(h, gain, dout, gs, us, wg, wu, wd, tm):
    T, D = h.shape
    nsh, F = wg.shape[0], wg.shape[2]
    nt = T // tm

    def body(h_ref, gain_ref, dout_ref, gs_ref, us_ref, wg_ref, wu_ref, wd_ref,
             dh_ref, dgain_ref, dg_ref, du_ref, xn_ref, dob_ref):
        @pl.when(pl.program_id(0) == 0)
        def _():
            dgain_ref[...] = jnp.zeros_like(dgain_ref)

        hv = h_ref[...]
        r = lax.rsqrt(jnp.mean(hv * hv, axis=-1, keepdims=True) + EPS)
        xh = hv * r
        xn_ref[...] = (xh * gain_ref[...]).astype(bf16)
        dv = dout_ref[...]
        dob = (0.5 * dv).astype(bf16)
        dob_ref[...] = dob
        dxn = None
        for k in range(nsh):
            da = _dot_nt(dob, wd_ref[k])
            g = gs_ref[k].astype(f32)
            u = us_ref[k].astype(f32)
            sg = jax.nn.sigmoid(g)
            dg = (da * u * (sg * (1.0 + g * (1.0 - sg)))).astype(bf16)
            du = (da * (g * sg)).astype(bf16)
            dg_ref[k] = dg
            du_ref[k] = du
            part = _dot_nt(dg, wg_ref[k]) + _dot_nt(du, wu_ref[k])
            dxn = part if dxn is None else dxn + part
        dgain_ref[...] += jnp.sum(dxn * xh, axis=0, keepdims=True)
        dxh = dxn * gain_ref[...]
        dh_ref[...] = dv + r * (dxh - xh * jnp.mean(dxh * xh, axis=-1, keepdims=True))

    tok = pl.BlockSpec((tm, D), lambda i: (i, 0))
    sav = pl.BlockSpec((nsh, tm, F), lambda i: (0, i, 0))
    return pl.pallas_call(
        body, name="ffn_bwd_dgrad", grid=(nt,),
        in_specs=[tok, _full((1, D)), tok, sav, sav, _resident((nsh, D, F)), _resident((nsh, D, F)), _resident((nsh, F, D))],
        out_specs=(tok, _full((1, D)), sav, sav, tok, tok),
        out_shape=(_sds((T, D), f32), _sds((1, D), f32), _sds((nsh, T, F), bf16), _sds((nsh, T, F), bf16),
                   _sds((T, D), bf16), _sds((T, D), bf16)),
        compiler_params=_cparams(1),
    )(h, gain, dout, gs, us, wg, wu, wd)


def _ffn_bwd_wgrad(xn, dob, gs, us, dg, du, tm):
    T, D = xn.shape
    nsh, F = gs.shape[0], gs.shape[2]
    nt = T // tm

    def body(xn_ref, dob_ref, gs_ref, us_ref, dg_ref, du_ref, dwg_ref, dwu_ref, dwd_ref, ag_ref, au_ref, ad_ref):
        i = pl.program_id(1)

        @pl.when(i == 0)
        def _():
            ag_ref[...] = jnp.zeros_like(ag_ref)
            au_ref[...] = jnp.zeros_like(au_ref)
            ad_ref[...] = jnp.zeros_like(ad_ref)

        xn_v = xn_ref[...]
        ag_ref[...] += _dot_tn(xn_v, dg_ref[...])
        au_ref[...] += _dot_tn(xn_v, du_ref[...])
        g = gs_ref[...].astype(f32)
        a = (_silu(g) * us_ref[...].astype(f32)).astype(bf16)
        ad_ref[...] += _dot_tn(a, dob_ref[...])

        @pl.when(i == nt - 1)
        def _():
            dwg_ref[...] = ag_ref[...].astype(bf16)
            dwu_ref[...] = au_ref[...].astype(bf16)
            dwd_ref[...] = ad_ref[...].astype(bf16)

    tok = pl.BlockSpec((tm, D), lambda k, i: (i, 0))
    sav = pl.BlockSpec((None, tm, F), lambda k, i: (k, i, 0))
    wspec = pl.BlockSpec((None, D, F), lambda k, i: (k, 0, 0))
    wdspec = pl.BlockSpec((None, F, D), lambda k, i: (k, 0, 0))
    return pl.pallas_call(
        body, name="ffn_bwd_wgrad", grid=(nsh, nt),
        in_specs=[tok, tok, sav, sav, sav, sav],
        out_specs=(wspec, wspec, wdspec),
        out_shape=(_sds((nsh, D, F), bf16), _sds((nsh, D, F), bf16), _sds((nsh, F, D), bf16)),
        scratch_shapes=[pltpu.VMEM((D, F), f32), pltpu.VMEM((D, F), f32), pltpu.VMEM((F, D), f32)],
        compiler_params=_cparams(2),
    )(xn, dob, gs, us, dg, du)


def _proj_in_fwd(h, gain, w_in, tm):
    T, D = h.shape
    nsh, N = w_in.shape[0], w_in.shape[2]
    nt = T // tm

    def body(h_ref, gain_ref, w_ref, z_ref):
        hv = h_ref[...]
        r = lax.rsqrt(jnp.mean(hv * hv, axis=-1, keepdims=True) + EPS)
        xn = (hv * r * gain_ref[...]).astype(bf16)
        for k in range(nsh):
            z_ref[:, k * N:(k + 1) * N] = _dot(xn, w_ref[k])

    return pl.pallas_call(
        body, name="proj_in_fwd", grid=(nt,),
        in_specs=[pl.BlockSpec((tm, D), lambda i: (i, 0)), _full((1, D)), _full((nsh, D, N))],
        out_specs=pl.BlockSpec((tm, nsh * N), lambda i: (i, 0)),
        out_shape=_sds((T, nsh * N), f32),
        compiler_params=_cparams(1),
    )(h, gain, w_in)


def _proj_in_bwd_dgrad(h, gain, dres, dz, w_in, tm):
    T, D = h.shape
    nsh, N = w_in.shape[0], w_in.shape[2]
    nt = T // tm

    def body(h_ref, gain_ref, dres_ref, dz_ref, w_ref, dh_ref, dgain_ref, xn_ref):
        @pl.when(pl.program_id(0) == 0)
        def _():
            dgain_ref[...] = jnp.zeros_like(dgain_ref)

        dxn = _dot_nt(dz_ref[:, 0:N], w_ref[0])
        for k in range(1, nsh):
            dxn = dxn + _dot_nt(dz_ref[:, k * N:(k + 1) * N], w_ref[k])
        hv = h_ref[...]
        r = lax.rsqrt(jnp.mean(hv * hv, axis=-1, keepdims=True) + EPS)
        xh = hv * r
        xn_ref[...] = (xh * gain_ref[...]).astype(bf16)
        dgain_ref[...] += jnp.sum(dxn * xh, axis=0, keepdims=True)
        dxh = dxn * gain_ref[...]
        dh_ref[...] = dres_ref[...] + r * (dxh - xh * jnp.mean(dxh * xh, axis=-1, keepdims=True))

    tok = pl.BlockSpec((tm, D), lambda i: (i, 0))
    return pl.pallas_call(
        body, name="proj_in_bwd_dgrad", grid=(nt,),
        in_specs=[tok, _full((1, D)), tok, pl.BlockSpec((tm, nsh * N), lambda i: (i, 0)), _full((nsh, D, N))],
        out_specs=(tok, _full((1, D)), tok),
        out_shape=(_sds((T, D), f32), _sds((1, D), f32), _sds((T, D), bf16)),
        compiler_params=_cparams(1),
    )(h, gain, dres, dz, w_in)


def _proj_in_bwd_wgrad(xn, dz, nsh):
    T, D = xn.shape
    N = dz.shape[1] // nsh

    def body(xn_ref, dz_ref, dw_ref):
        dw_ref[...] = _dot_tn(xn_ref[...], dz_ref[...]).astype(bf16)

    return pl.pallas_call(
        body, name="proj_in_bwd_wgrad", grid=(nsh,),
        in_specs=[_full((T, D)), pl.BlockSpec((T, N), lambda k: (0, k))],
        out_specs=pl.BlockSpec((None, D, N), lambda k: (k, 0, 0)),
        out_shape=_sds((nsh, D, N), bf16),
        compiler_params=_cparams(1),
    )(xn, dz)


def _proj_out_fwd(h, oa, ob, oc, w_out, tm):
    T, D = h.shape
    nsh, R = w_out.shape[0], w_out.shape[1]
    da, db = oa.shape[1], ob.shape[1]
    nt = T // tm

    def body(h_ref, oa_ref, ob_ref, oc_ref, w_ref, out_ref):
        w = w_ref[...].reshape(nsh * R, D)
        out_ref[...] = (h_ref[...] + _dot(oa_ref[...], w[:da]) + _dot(ob_ref[...], w[da:da + db])
                        + _dot(oc_ref[...], w[da + db:]))

    def tok(n):
        return pl.BlockSpec((tm, n), lambda i: (i, 0))

    return pl.pallas_call(
        body, name="proj_out_fwd", grid=(nt,),
        in_specs=[tok(D), tok(da), tok(db), tok(oc.shape[1]), _full((nsh, R, D))],
        out_specs=tok(D), out_shape=_sds((T, D), f32),
        compiler_params=_cparams(1),
    )(h, oa, ob, oc, w_out)


def _proj_out_bwd(dh, oa, ob, oc, w_out, tm):
    T, D = dh.shape
    nsh, R = w_out.shape[0], w_out.shape[1]
    da, db, dc = oa.shape[1], ob.shape[1], oc.shape[1]
    nt = T // tm

    def body(dh_ref, oa_ref, ob_ref, oc_ref, w_ref, doa_ref, dob_ref, doc_ref, dw_ref, acc_ref):
        i = pl.program_id(0)

        @pl.when(i == 0)
        def _():
            acc_ref[...] = jnp.zeros_like(acc_ref)

        d = dh_ref[...].astype(bf16)
        w = w_ref[...].reshape(nsh * R, D)
        dm = _dot_nt(d, w)
        doa_ref[...] = dm[:, :da]
        dob_ref[...] = dm[:, da:da + db]
        doc_ref[...] = dm[:, da + db:]
        acc_ref[pl.ds(0, da), :] += _dot_tn(oa_ref[...], d)
        acc_ref[pl.ds(da, db), :] += _dot_tn(ob_ref[...], d)
        acc_ref[pl.ds(da + db, dc), :] += _dot_tn(oc_ref[...], d)

        @pl.when(i == nt - 1)
        def _():
            dw_ref[...] = acc_ref[...].astype(bf16).reshape(nsh, R, D)

    def tok(n):
        return pl.BlockSpec((tm, n), lambda i: (i, 0))

    wspec = _full((nsh, R, D))
    return pl.pallas_call(
        body, name="proj_out_bwd", grid=(nt,),
        in_specs=[tok(D), tok(da), tok(db), tok(dc), wspec],
        out_specs=(tok(da), tok(db), tok(dc), wspec),
        out_shape=(_sds((T, da), f32), _sds((T, db), f32), _sds((T, dc), f32), _sds((nsh, R, D), bf16)),
        scratch_shapes=[pltpu.VMEM((nsh * R, D), f32)],
        compiler_params=_cparams(1),
    )(dh, oa, ob, oc, w_out)


def _head_sum(m, n_heads):
    parts = []
    for hd in range(n_heads):
        s = jnp.sum(m[:, hd * HEAD:(hd + 1) * HEAD], axis=-1, keepdims=True)
        parts.append(jnp.broadcast_to(s, (m.shape[0], HEAD)))
    return parts[0] if n_heads == 1 else jnp.concatenate(parts, axis=1)


def _cat(parts, axis):
    return parts[0] if len(parts) == 1 else jnp.concatenate(parts, axis=axis)


def _hgrn_block(q, fl, iv, lb, states, tri, n_heads, n_inner):
    C = q.shape[0] // n_inner
    qs = _silu(q)
    forget = lb + (1.0 - lb) * jax.nn.sigmoid(fl)
    kk = 1.0 - forget
    logf = jnp.log(forget)
    b = jnp.dot(tri, logf, precision=HI, preferred_element_type=f32)
    vb = iv.astype(bf16)
    heads = [slice(hd * HEAD, (hd + 1) * HEAD) for hd in range(n_heads)]
    n_sub = C // A_SUB

    off, qe, kd, dec = {}, [], [], []
    for j in range(n_inner):
        c0 = j * C
        for blk in range(1, n_sub):
            lo = c0 + blk * A_SUB
            piv = b[lo:lo + 1]
            qt = (qs[lo:lo + A_SUB] * jnp.exp(b[lo:lo + A_SUB] - piv)).astype(bf16)
            kt = (kk[c0:lo] * jnp.exp(piv - b[c0:lo])).astype(bf16)
            parts = []
            for sl in heads:
                sc = _dot_nt(qt[:, sl], kt[:, sl])
                parts.append(_dot(sc.astype(bf16), vb[c0:lo, sl]))
            off[(j, blk)] = _cat(parts, 1)
        bj = b[c0:c0 + C]
        b_end = bj[C - 1:C]
        qe.append((qs[c0:c0 + C] * jnp.exp(bj)).astype(bf16))
        kd.append((kk[c0:c0 + C] * jnp.exp(b_end - bj)).astype(bf16))
        dec.append(jnp.exp(b_end))

    outs = []
    for j in range(n_inner):
        for blk in range(n_sub):
            lo = j * C + blk * A_SUB
            groups = [off[(j, blk)][r0:r0 + SUBLANES] if blk > 0 else None for r0 in range(0, A_SUB, SUBLANES)]
            for s in range(A_SUB):
                first = (s // SUBLANES) * SUBLANES
                n_rows = A_SUB - first
                row = lax.broadcasted_iota(jnp.int32, (n_rows, 1), 0) + first
                gate = jnp.where(row >= s, 0.0, -1e30)
                r = slice(lo + first, lo + A_SUB)
                m = qs[r] * jnp.exp((b[r] - b[lo + s:lo + s + 1]) + gate) * kk[lo + s:lo + s + 1]
                term = _head_sum(m, n_heads) * iv[lo + s:lo + s + 1]
                for gi in range(first // SUBLANES, A_SUB // SUBLANES):
                    piece = term[gi * SUBLANES - first:(gi + 1) * SUBLANES - first]
                    groups[gi] = piece if groups[gi] is None else groups[gi] + piece
            outs.extend(groups)
    o = jnp.concatenate(outs, axis=0)

    inter = []
    states = list(states)
    for j in range(n_inner):
        c0 = j * C
        parts = []
        for hd, sl in enumerate(heads):
            st = states[hd]
            parts.append(_dot_nt(qe[j][:, sl], st.astype(bf16)))
            states[hd] = dec[j][:, sl] * st + _dot_tn(vb[c0:c0 + C, sl], kd[j][:, sl])
        inter.append(_cat(parts, 1))
    return o + _cat(inter, 0), tuple(states)


def _hgrn_gate(o, g, gain, n_heads):
    ms = _head_sum(o * o, n_heads) * (1.0 / HEAD)
    return o * lax.rsqrt(ms + EPS) * gain * _silu(g)


def _tri_matrix(c, n_inner):
    idx = np.arange(c * n_inner)
    same = (idx[:, None] // c) == (idx[None, :] // c)
    return jnp.asarray((same & (idx[:, None] >= idx[None, :])).astype(np.float32))


def _hgrn_fwd(z, lb, gain, d_a):
    T = z.shape[0]
    C = A_CHUNK * A_INNER
    nc = T // C
    nh = d_a // HEAD
    tri = _tri_matrix(A_CHUNK, A_INNER)

    def body(q_ref, f_ref, i_ref, g_ref, lb_ref, gain_ref, tri_ref, out_ref, o_ref, st_ref, carry_ref):
        @pl.when(pl.program_id(0) == 0)
        def _():
            carry_ref[...] = jnp.zeros_like(carry_ref)

        states = tuple(carry_ref[hd] for hd in range(nh))
        st_ref[...] = carry_ref[...]
        o, new_states = _hgrn_block(q_ref[...], f_ref[...], i_ref[...], lb_ref[...], states, tri_ref[...], nh, A_INNER)
        o_ref[...] = o
        out_ref[...] = _hgrn_gate(o, g_ref[...], gain_ref[...], nh).astype(bf16)
        for hd in range(nh):
            carry_ref[hd] = new_states[hd]

    def col(j):
        return pl.BlockSpec((C, d_a), lambda c, j=j: (c, j))

    tok = pl.BlockSpec((C, d_a), lambda c: (c, 0))
    return pl.pallas_call(
        body, name="hgrn_fwd", grid=(nc,),
        in_specs=[col(0), col(1), col(2), col(3), _full((1, d_a)), _full((1, d_a)), _full((C, C))],
        out_specs=(tok, tok, pl.BlockSpec((None, nh, HEAD, HEAD), lambda c: (c, 0, 0, 0))),
        out_shape=(_sds((T, d_a), bf16), _sds((T, d_a), f32), _sds((nc, nh, HEAD, HEAD), f32)),
        scratch_shapes=[pltpu.VMEM((nh, HEAD, HEAD), f32)],
        compiler_params=_cparams(1),
    )(z, z, z, z, lb, gain, tri)


def _hgrn_bwd(z, lb, gain, o_pre, states, dout, d_a):
    T = z.shape[0]
    C = A_CHUNK * A_INNER
    nc = T // C
    nh = d_a // HEAD
    tri = _tri_matrix(A_CHUNK, A_INNER)

    def body(q_ref, f_ref, i_ref, g_ref, lb_ref, gain_ref, tri_ref, o_ref, st_ref, do_ref,
             dz_ref, dlb_ref, dgain_ref, carry_ref):
        @pl.when(pl.program_id(0) == 0)
        def _():
            carry_ref[...] = jnp.zeros_like(carry_ref)
            dlb_ref[...] = jnp.zeros_like(dlb_ref)
            dgain_ref[...] = jnp.zeros_like(dgain_ref)

        _, vjp_gate = jax.vjp(lambda o, g, gv: _hgrn_gate(o, g, gv, nh), o_ref[...], g_ref[...], gain_ref[...])
        d_o, dg, dgain = vjp_gate(do_ref[...])
        tri_v = tri_ref[...]

        def fn(q, fl, iv, lbv, sts):
            return _hgrn_block(q, fl, iv, lbv, sts, tri_v, nh, A_INNER)

        states_in = tuple(st_ref[hd] for hd in range(nh))
        _, vjp = jax.vjp(fn, q_ref[...], f_ref[...], i_ref[...], lb_ref[...], states_in)
        dstates = tuple(carry_ref[hd] for hd in range(nh))
        dq, df, di, dlb, dst = vjp((d_o, dstates))
        dz_ref[:, 0:d_a] = dq.astype(bf16)
        dz_ref[:, d_a:2 * d_a] = df.astype(bf16)
        dz_ref[:, 2 * d_a:3 * d_a] = di.astype(bf16)
        dz_ref[:, 3 * d_a:4 * d_a] = dg.astype(bf16)
        dlb_ref[...] += dlb
        dgain_ref[...] += dgain
        for hd in range(nh):
            carry_ref[hd] = dst[hd]

    def col(j):
        return pl.BlockSpec((C, d_a), lambda c, j=j: (nc - 1 - c, j))

    tok = pl.BlockSpec((C, d_a), lambda c: (nc - 1 - c, 0))
    return pl.pallas_call(
        body, name="hgrn_bwd", grid=(nc,),
        in_specs=[col(0), col(1), col(2), col(3), _full((1, d_a)), _full((1, d_a)), _full((C, C)), tok,
                  pl.BlockSpec((None, nh, HEAD, HEAD), lambda c: (nc - 1 - c, 0, 0, 0)), tok],
        out_specs=(pl.BlockSpec((C, 4 * d_a), lambda c: (nc - 1 - c, 0)), _full((1, d_a)), _full((1, d_a))),
        out_shape=(_sds(z.shape, bf16), _sds((1, d_a), f32), _sds((1, d_a), f32)),
        scratch_shapes=[pltpu.VMEM((nh, HEAD, HEAD), f32)],
        compiler_params=_cparams(1),
    )(z, z, z, z, lb, gain, tri, o_pre, states, dout)


def _one_minus_exp(x):
    series = -x * (1.0 + x * (0.5 + x * (1.0 / 6.0 + x * (1.0 / 24.0))))
    return jnp.where(x > -0.03, series, 1.0 - jnp.exp(x))


def _lru_pre(xc, wa, ba, wx, bx, lam):
    xb16 = xc.astype(bf16)
    r = jax.nn.sigmoid(_dot(xb16, wa.astype(bf16)) + ba)
    gi = jax.nn.sigmoid(_dot(xb16, wx.astype(bf16)) + bx)
    log_a = -LRU_C * r * jax.nn.softplus(-lam)
    a = jnp.exp(log_a)
    mult = jnp.sqrt(_one_minus_exp(2.0 * log_a))
    return a, mult * gi * xc


def _lru_post(h, gate, gain, avg):
    y = h * jax.nn.gelu(gate)
    ms = _group_mean(y * y, avg)
    return y * lax.rsqrt(ms + EPS) * gain


def _shift_down(x, d, prev):
    row = lax.broadcasted_iota(jnp.int32, x.shape, 0)
    return jnp.where(row >= d, pltpu.roll(x, d, 0), pltpu.roll(prev, d, 0))


def _shift_up(x, d, nxt):
    n = x.shape[0]
    row = lax.broadcasted_iota(jnp.int32, x.shape, 0)
    return jnp.where(row < n - d, pltpu.roll(x, n - d, 0), pltpu.roll(nxt, n - d, 0))


def _scan_rows(a, u, reverse):
    n = a.shape[0]
    row = lax.broadcasted_iota(jnp.int32, a.shape, 0)
    d = 1
    while d < n:
        shift, ok = (n - d, row < n - d) if reverse else (d, row >= d)
        su = jnp.where(ok, pltpu.roll(u, shift, 0), 0.0)
        sa = jnp.where(ok, pltpu.roll(a, shift, 0), 1.0)
        u = u + a * su
        a = a * sa
        d *= 2
    return a, u


def _conv(xb, xprev, cw, cb):
    xc = cb + cw[CONV_WIDTH - 1:CONV_WIDTH] * xb
    for d in range(1, CONV_WIDTH):
        xc = xc + cw[CONV_WIDTH - 1 - d:CONV_WIDTH - d] * _shift_down(xb, d, xprev)
    return xc


def _lru_fwd(z, col0, d_b, cw, cb, wa, ba, wx, bx, lam, gain, avg):
    T = z.shape[0]
    R = min(B_CHUNK, T)
    nr = T // R
    jb = col0 // d_b

    def body(xb_ref, gate_ref, cw_ref, cb_ref, wa_ref, ba_ref, wx_ref, bx_ref, lam_ref, gain_ref, avg_ref,
             out_ref, h_ref, xprev_ref, hprev_ref):
        @pl.when(pl.program_id(0) == 0)
        def _():
            xprev_ref[...] = jnp.zeros_like(xprev_ref)
            hprev_ref[...] = jnp.zeros_like(hprev_ref)

        xb = xb_ref[...]
        xc = _conv(xb, xprev_ref[...], cw_ref[...], cb_ref[...])
        a, u = _lru_pre(xc, wa_ref[...], ba_ref[...], wx_ref[...], bx_ref[...], lam_ref[...])
        acum, hl = _scan_rows(a, u, False)
        h = hl + acum * hprev_ref[R - 1:R, :]
        h_ref[...] = h
        out_ref[...] = _lru_post(h, gate_ref[...], gain_ref[...], avg_ref[...]).astype(bf16)
        xprev_ref[...] = xb
        hprev_ref[...] = h

    vec = _full((1, d_b))
    return pl.pallas_call(
        body, name="lru_fwd", grid=(nr,),
        in_specs=[pl.BlockSpec((R, d_b), lambda i: (i, jb)), pl.BlockSpec((R, d_b), lambda i: (i, jb + 1)),
                  _full((CONV_WIDTH, d_b)), vec, _full((d_b, d_b)), vec, _full((d_b, d_b)), vec, vec, vec, _full((d_b, d_b))],
        out_specs=(pl.BlockSpec((R, d_b), lambda i: (i, 0)), pl.BlockSpec((R, d_b), lambda i: (i, 0))),
        out_shape=(_sds((T, d_b), bf16), _sds((T, d_b), f32)),
        scratch_shapes=[pltpu.VMEM((R, d_b), f32), pltpu.VMEM((R, d_b), f32)],
        compiler_params=_cparams(1),
    )(z, z, cw, cb, wa, ba, wx, bx, lam, gain, avg)


def _lru_bwd(z, col0, d_b, hsave, dout, dz_buf, cw, cb, wa, ba, wx, bx, lam, gain, avg):
    T = z.shape[0]
    R = min(B_CHUNK, T)
    nr = T // R
    jb = col0 // d_b

    def body(xb_ref, xp_ref, gate_ref, h_ref, hp_ref, do_ref,
             cw_ref, cb_ref, wa_ref, ba_ref, wx_ref, bx_ref, lam_ref, gain_ref, avg_ref, dzin_ref,
             dz_ref, dcw_ref, dcb_ref, dwa_ref, dba_ref, dwx_ref, dbx_ref, dlam_ref, dgain_ref,
             gfirst_ref, afirst_ref, dxcn_ref):
        step = pl.program_id(0)
        first_in_time = step == nr - 1

        @pl.when(step == 0)
        def _():
            for r in (dcw_ref, dcb_ref, dwa_ref, dba_ref, dwx_ref, dbx_ref, dlam_ref, dgain_ref,
                      gfirst_ref, afirst_ref, dxcn_ref):
                r[...] = jnp.zeros_like(r)

        xb = xb_ref[...]
        keep = jnp.where(first_in_time, 0.0, 1.0)
        xprev = xp_ref[...] * keep
        hprev = hp_ref[...] * keep
        cw = cw_ref[...]
        xc = _conv(xb, xprev, cw, cb_ref[...])
        (a, _), vjp_pre = jax.vjp(_lru_pre, xc, wa_ref[...], ba_ref[...], wx_ref[...], bx_ref[...], lam_ref[...])
        h = h_ref[...]
        avg = avg_ref[...]
        _, vjp_post = jax.vjp(lambda hh, gg, gn: _lru_post(hh, gg, gn, avg), h, gate_ref[...], gain_ref[...])
        dh, dgate, dgain = vjp_post(do_ref[...])
        a_next = _shift_up(a, 1, jnp.broadcast_to(afirst_ref[0:1, :], a.shape))
        acum, gl = _scan_rows(a_next, dh, True)
        gtot = gl + acum * gfirst_ref[0:1, :]
        da = gtot * _shift_down(h, 1, hprev)
        dxc, dwa, dba, dwx, dbx, dlam = vjp_pre((da, gtot))
        dxcn = dxcn_ref[...]
        dxb = cw[CONV_WIDTH - 1:CONV_WIDTH] * dxc
        dcw_ref[CONV_WIDTH - 1:CONV_WIDTH, :] += jnp.sum(dxc * xb, axis=0, keepdims=True)
        for d in range(1, CONV_WIDTH):
            tap = CONV_WIDTH - 1 - d
            dxb = dxb + cw[tap:tap + 1] * _shift_up(dxc, d, dxcn)
            dcw_ref[tap:tap + 1, :] += jnp.sum(dxc * _shift_down(xb, d, xprev), axis=0, keepdims=True)
        dz_ref[:, 0:d_b] = dxb.astype(bf16)
        dz_ref[:, d_b:2 * d_b] = dgate.astype(bf16)
        dcb_ref[...] += jnp.sum(dxc, axis=0, keepdims=True)
        dwa_ref[...] += dwa
        dba_ref[...] += dba
        dwx_ref[...] += dwx
        dbx_ref[...] += dbx
        dlam_ref[...] += dlam
        dgain_ref[...] += dgain
        gfirst_ref[...] = jnp.broadcast_to(gtot[0:1, :], gfirst_ref.shape)
        afirst_ref[...] = jnp.broadcast_to(a[0:1, :], afirst_ref.shape)
        dxcn_ref[...] = dxc

    vec = _full((1, d_b))
    mat = _full((d_b, d_b))

    def cur(j):
        return pl.BlockSpec((R, d_b), lambda i, j=j: (nr - 1 - i, j))

    def prev(j):
        return pl.BlockSpec((R, d_b), lambda i, j=j: (jnp.maximum(nr - 2 - i, 0), j))

    return pl.pallas_call(
        body, name="lru_bwd", grid=(nr,),
        in_specs=[cur(jb), prev(jb), cur(jb + 1), cur(0), prev(0), cur(0),
                  _full((CONV_WIDTH, d_b)), vec, mat, vec, mat, vec, vec, vec, mat, ANY_SPEC],
        out_specs=(pl.BlockSpec((R, 2 * d_b), lambda i: (nr - 1 - i, col0 // (2 * d_b))), _full((CONV_WIDTH, d_b)), vec, mat, vec, mat, vec, vec, vec),
        out_shape=(_sds(dz_buf.shape, bf16), _sds((CONV_WIDTH, d_b), f32), _sds((1, d_b), f32), _sds((d_b, d_b), f32),
                   _sds((1, d_b), f32), _sds((d_b, d_b), f32), _sds((1, d_b), f32), _sds((1, d_b), f32), _sds((1, d_b), f32)),
        scratch_shapes=[pltpu.VMEM((8, d_b), f32), pltpu.VMEM((8, d_b), f32), pltpu.VMEM((R, d_b), f32)],
        input_output_aliases={15: 0},
        compiler_params=_cparams(1),
    )(z, z, z, hsave, hsave, dout, cw, cb, wa, ba, wx, bx, lam, gain, avg, dz_buf)


def _two_pass(x, m16):
    hi = x.astype(bf16)
    lo = (x - hi.astype(f32)).astype(bf16)
    return _dot(hi, m16) + _dot(lo, m16)


@jax.custom_vjp
def _group_mean(x, avg):
    return _two_pass(x, avg.astype(bf16))


def _group_mean_fwd(x, avg):
    return _group_mean(x, avg), avg


def _group_mean_bwd(avg, ct):
    return _two_pass(ct, avg.astype(bf16)), jnp.zeros_like(avg)


_group_mean.defvjp(_group_mean_fwd, _group_mean_bwd)


def _sgu_chunk(u_in, v_in, w, bexp, gain, avg, n_groups):
    C, d_c = u_in.shape
    gd = d_c // n_groups
    u = jax.nn.gelu(u_in)
    v = jax.nn.gelu(v_in)
    mu = _group_mean(v, avg)
    vc = v - mu
    var = _group_mean(vc * vc, avg)
    vh = (vc * lax.rsqrt(var + EPS)).astype(bf16)
    lane = lax.broadcasted_iota(jnp.int32, (1, d_c), 1)
    causal = lax.broadcasted_iota(jnp.int32, (C, C), 0) >= lax.broadcasted_iota(jnp.int32, (C, C), 1)
    zz = bexp
    for g in range(n_groups):
        wg = jnp.where(causal, w[g], 0.0).astype(bf16)
        zz = zz + jnp.where((lane >= g * gd) & (lane < (g + 1) * gd), _dot(wg, vh), 0.0)
    y = u * zz
    ms = _group_mean(y * y, avg)
    return y * lax.rsqrt(ms + EPS) * gain


def _sgu_inner(T):
    return C_INNER if T % (C_CHUNK * C_INNER) == 0 else 1


def _sgu_fwd(z, col0, d_c, w, bexp, gain, avg):
    T = z.shape[0]
    C = C_CHUNK
    n_in = _sgu_inner(T)
    R = C * n_in
    jb = col0 // d_c
    G = w.shape[0]

    def body(u_ref, v_ref, w_ref, b_ref, gain_ref, avg_ref, out_ref):
        w_v, b_v, gain_v, avg = w_ref[...], b_ref[...], gain_ref[...], avg_ref[...]
        for j in range(n_in):
            rows = pl.ds(j * C, C)
            out_ref[rows, :] = _sgu_chunk(u_ref[rows, :], v_ref[rows, :], w_v, b_v, gain_v, avg, G).astype(bf16)

    return pl.pallas_call(
        body, name="sgu_fwd", grid=(T // R,),
        in_specs=[pl.BlockSpec((R, d_c), lambda i: (i, jb)), pl.BlockSpec((R, d_c), lambda i: (i, jb + 1)),
                  _full((G, C, C)), _full((C, d_c)), _full((1, d_c)), _full((d_c, d_c))],
        out_specs=pl.BlockSpec((R, d_c), lambda i: (i, 0)),
        out_shape=_sds((T, d_c), bf16),
        compiler_params=_cparams(1),
    )(z, z, w, bexp, gain, avg)


def _sgu_bwd(z, col0, d_c, dout, dz_buf, w, bexp, gain, avg):
    T = z.shape[0]
    C = C_CHUNK
    n_in = _sgu_inner(T)
    R = C * n_in
    nc = T // R
    jb = col0 // d_c
    G = w.shape[0]
    gd = d_c // G

    def body(u_ref, v_ref, do_ref, w_ref, b_ref, gain_ref, avg_ref, dzin_ref, dz_ref, dw_ref, db_ref, dgain_ref, dbexp_ref):
        step = pl.program_id(0)

        @pl.when(step == 0)
        def _():
            dw_ref[...] = jnp.zeros_like(dw_ref)
            dgain_ref[...] = jnp.zeros_like(dgain_ref)
            dbexp_ref[...] = jnp.zeros_like(dbexp_ref)

        avg, w_v, b_v, gain_v = avg_ref[...], w_ref[...], b_ref[...], gain_ref[...]
        dw = dbexp = dgain = None
        for j in range(n_in):
            rows = pl.ds(j * C, C)
            _, vjp = jax.vjp(lambda a, b, c, d, e: _sgu_chunk(a, b, c, d, e, avg, G),
                             u_ref[rows, :], v_ref[rows, :], w_v, b_v, gain_v)
            du, dv, dw_j, dbexp_j, dgain_j = vjp(do_ref[rows, :])
            dz_ref[rows, 0:d_c] = du.astype(bf16)
            dz_ref[rows, d_c:2 * d_c] = dv.astype(bf16)
            dw = dw_j if dw is None else dw + dw_j
            dbexp = dbexp_j if dbexp is None else dbexp + dbexp_j
            dgain = dgain_j if dgain is None else dgain + dgain_j
        dw_ref[...] += dw
        dbexp_ref[...] += dbexp
        dgain_ref[...] += dgain

        @pl.when(step == nc - 1)
        def _():
            lane = lax.broadcasted_iota(jnp.int32, (1, d_c), 1)
            acc = dbexp_ref[...]
            for g in range(G):
                sel = jnp.where((lane >= g * gd) & (lane < (g + 1) * gd), acc, 0.0)
                db_ref[:, g:g + 1] = jnp.sum(sel, axis=1, keepdims=True)

    return pl.pallas_call(
        body, name="sgu_bwd", grid=(nc,),
        in_specs=[pl.BlockSpec((R, d_c), lambda i: (i, jb)), pl.BlockSpec((R, d_c), lambda i: (i, jb + 1)),
                  pl.BlockSpec((R, d_c), lambda i: (i, 0)),
                  _full((G, C, C)), _full((C, d_c)), _full((1, d_c)), _full((d_c, d_c)), ANY_SPEC],
        out_specs=(pl.BlockSpec((R, 2 * d_c), lambda i: (i, col0 // (2 * d_c))), _full((G, C, C)), _full((C, G)), _full((1, d_c))),
        out_shape=(_sds(dz_buf.shape, bf16), _sds((G, C, C), f32), _sds((C, G), f32), _sds((1, d_c), f32)),
        scratch_shapes=[pltpu.VMEM((C, d_c), f32)],
        input_output_aliases={7: 0},
        compiler_params=_cparams(1),
    )(z, z, dout, w, bexp, gain, avg, dz_buf)


def _loss_head(h, gain, target, tm):
    T, D = h.shape
    nt = T // tm

    def body(h_ref, gain_ref, tgt_ref, dh_ref, loss_ref, dgain_ref):
        @pl.when(pl.program_id(0) == 0)
        def _():
            loss_ref[...] = jnp.zeros_like(loss_ref)
            dgain_ref[...] = jnp.zeros_like(dgain_ref)

        hv = h_ref[...]
        gain_v = gain_ref[...]
        r = lax.rsqrt(jnp.mean(hv * hv, axis=-1, keepdims=True) + EPS)
        xh = hv * r
        e = xh * gain_v - tgt_ref[...]
        loss_ref[...] += 0.5 * jnp.sum(jnp.mean(e * e, axis=-1, keepdims=True), axis=0, keepdims=True)
        dy = e * (1.0 / D)
        dgain_ref[...] += jnp.sum(dy * xh, axis=0, keepdims=True)
        dxh = dy * gain_v
        dh_ref[...] = r * (dxh - xh * jnp.mean(dxh * xh, axis=-1, keepdims=True))

    tok = pl.BlockSpec((tm, D), lambda i: (i, 0))
    return pl.pallas_call(
        body, name="loss_head", grid=(nt,),
        in_specs=[tok, _full((1, D)), tok],
        out_specs=(tok, _full((1, 128)), _full((1, D))),
        out_shape=(_sds((T, D), f32), _sds((1, 128), f32), _sds((1, D), f32)),
        compiler_params=_cparams(1),
    )(h, gain, target)


def _lower_bounds_fn(logits):
    n = logits.shape[0]
    mx = jnp.max(logits, axis=0, keepdims=True)
    ex = jnp.exp(logits - mx)
    soft = ex / jnp.sum(ex, axis=0, keepdims=True)
    rows = [jnp.zeros_like(soft[0:1])]
    for l in range(1, n):
        rows.append(rows[-1] + soft[l:l + 1])
    return jnp.concatenate(rows, axis=0)


def _lower_bounds(logits):
    def body(x_ref, o_ref):
        o_ref[...] = _lower_bounds_fn(x_ref[...])

    return pl.pallas_call(body, name="lower_bounds", out_shape=_sds(logits.shape, f32))(logits)


def _lower_bounds_bwd(logits, dlb):
    def body(x_ref, d_ref, o_ref):
        _, vjp = jax.vjp(_lower_bounds_fn, x_ref[...])
        o_ref[...] = vjp(d_ref[...])[0]

    return pl.pallas_call(body, name="lower_bounds_bwd", out_shape=_sds(logits.shape, f32))(logits, dlb)


def _adamw(w, g, m, v, rows_blk):
    R, Cc = w.shape
    rb = R if R <= rows_blk else math.gcd(R, rows_blk)

    def body(w_ref, g_ref, m_ref, v_ref, d_ref, nm_ref, nv_ref):
        gv = g_ref[...]
        m2 = ADAM_B1 * m_ref[...] + (1.0 - ADAM_B1) * gv
        v2 = ADAM_B2 * v_ref[...] + (1.0 - ADAM_B2) * (gv * gv)
        m_hat = m2 / (1.0 - ADAM_B1 ** ADAM_STEP)
        v_hat = v2 / (1.0 - ADAM_B2 ** ADAM_STEP)
        d_ref[...] = -ADAM_LR * (m_hat / (jnp.sqrt(v_hat) + ADAM_EPS) + ADAM_WD * w_ref[...])
        nm_ref[...] = m2
        nv_ref[...] = v2

    spec = pl.BlockSpec((rb, Cc), lambda i: (i, 0))
    return pl.pallas_call(
        body, name="adamw", grid=(R // rb,),
        in_specs=[spec] * 4, out_specs=(spec,) * 3, out_shape=(_sds((R, Cc), f32),) * 3,
        compiler_params=_cparams(1),
    )(w, g, m, v)


def _pair_sum(grads, recv, c_arr):
    n = len(grads)
    nsh = grads[0].shape[0]

    def body(c_ref, *refs):
        for a in range(n):
            refs[2 * n + a][...] = (refs[a][...].astype(f32) + refs[n + a][...].astype(f32)).astype(bf16)

    g_specs, r_specs, out_shape = [], [], []
    for g in grads:
        _, R, Cc = g.shape
        r2 = R // 2
        g_specs.append(pl.BlockSpec((None, r2, Cc), lambda s, c: (s, c[0], 0)))
        r_specs.append(pl.BlockSpec((None, r2, Cc), lambda s, c: (s, 0, 0)))
        out_shape.append(_sds((nsh, r2, Cc), bf16))
    gs = pltpu.PrefetchScalarGridSpec(num_scalar_prefetch=1, grid=(nsh,), in_specs=g_specs + r_specs, out_specs=tuple(r_specs))
    return list(pl.pallas_call(body, name="pair_sum", grid_spec=gs, out_shape=tuple(out_shape),
                               compiler_params=_cparams(1))(c_arr, *grads, *recv))


def _add(a, b):
    def body(a_ref, b_ref, o_ref):
        o_ref[...] = a_ref[...] + b_ref[...]

    return pl.pallas_call(body, name="pair_sum_small", out_shape=_sds(a.shape, f32))(a, b)


def _chip_sum(hsum, recv, bufs, slot_arr, c_arr, layer, n_layers):
    n = len(hsum)
    prev = list(bufs)
    steps = 2

    def body(s_ref, c_ref, *refs):
        outs = refs[len(refs) - n:]
        for a in range(n):
            acc = refs[a][...].astype(f32)
            for j in range(N_CHIPS - 1):
                acc = acc + refs[n + a][j].astype(f32)
            outs[a][...] = acc

    h_specs, r_specs, o_specs, out_shape = [], [], [], []
    for hh in hsum:
        _, r2, Cc = hh.shape
        rt = r2 // steps
        h_specs.append(pl.BlockSpec((None, rt, Cc), lambda i, s, c: (s[0], i, 0)))
        r_specs.append(pl.BlockSpec((N_CHIPS - 1, rt, Cc), lambda i, s, c: (0, i, 0)))
        o_specs.append(pl.BlockSpec((None, rt, Cc), lambda i, s, c: (layer, c[0] * steps + i, 0)))
        out_shape.append(_sds((n_layers, 2 * r2, Cc), f32))
    gs = pltpu.PrefetchScalarGridSpec(num_scalar_prefetch=2, grid=(steps,),
                                      in_specs=h_specs + r_specs + [ANY_SPEC] * len(prev), out_specs=tuple(o_specs))
    return list(pl.pallas_call(body, name="chip_sum", grid_spec=gs, out_shape=tuple(out_shape),
                               input_output_aliases={2 + 2 * n + a: a for a in range(len(prev))},
                               compiler_params=_cparams(1))(slot_arr, c_arr, *hsum, *recv, *prev))


def _sum_slots(x):
    def body(x_ref, o_ref):
        acc = x_ref[0]
        for j in range(1, x.shape[0]):
            acc = acc + x_ref[j]
        o_ref[...] = acc

    return pl.pallas_call(body, name="sum_slots", out_shape=_sds(x.shape[1:], f32))(x)


def _blockdiag(w):
    nb, bd, _ = w.shape
    eye = jnp.eye(nb, dtype=w.dtype)
    return (eye[:, None, :, None] * w[:, :, None, :]).reshape(nb * bd, nb * bd)


def _blockdiag_extract(dense, nb):
    bd = dense.shape[0] // nb
    d4 = dense.reshape(nb, bd, nb, bd)
    return jnp.stack([d4[i, :, i, :] for i in range(nb)])


def _pack(arrays, multiple):
    flat = jnp.concatenate([a.reshape(-1).astype(f32) for a in arrays])
    pad = (-flat.shape[0]) % multiple
    return jnp.pad(flat, (0, pad))


def _unpack(flat, shapes):
    out, off = [], 0
    for s in shapes:
        n = int(np.prod(s))
        out.append(flat[off:off + n].reshape(s))
        off += n
    return out


BIG = ("ffn1_wg", "ffn1_wu", "ffn1_wd", "w_in", "w_out", "ffn2_wg", "ffn2_wu", "ffn2_wd")
SMALL = ("ffn1_norm", "mix_norm", "hgrn_lb_logits", "hgrn_norm", "conv_w", "conv_b", "lru_wa", "lru_ba", "lru_wx",
         "lru_bx", "lru_lambda", "lru_norm", "sgu_w", "sgu_b", "sgu_norm", "ffn2_norm", "final_norm")
WEIGHTS = ("ffn1_norm", "ffn1_wg", "ffn1_wu", "ffn1_wd", "mix_norm", "w_in", "hgrn_lb_logits", "hgrn_norm", "conv_w",
           "conv_b", "lru_wa", "lru_ba", "lru_wx", "lru_bx", "lru_lambda", "lru_norm", "sgu_w", "sgu_b", "sgu_norm",
           "w_out", "ffn2_norm", "ffn2_wg", "ffn2_wu", "ffn2_wd", "final_norm")


def kernel(x, ffn1_norm, ffn1_wg, ffn1_wu, ffn1_wd, mix_norm, w_in, hgrn_lb_logits, hgrn_norm, conv_w, conv_b, lru_wa, lru_ba, lru_wx, lru_bx, lru_lambda, lru_norm, sgu_w, sgu_b, sgu_norm, w_out, ffn2_norm, ffn2_wg, ffn2_wu, ffn2_wd, final_norm, loss_target, m_ffn1_norm, m_ffn1_wg, m_ffn1_wu, m_ffn1_wd, m_mix_norm, m_w_in, m_hgrn_lb_logits, m_hgrn_norm, m_conv_w, m_conv_b, m_lru_wa, m_lru_ba, m_lru_wx, m_lru_bx, m_lru_lambda, m_lru_norm, m_sgu_w, m_sgu_b, m_sgu_norm, m_w_out, m_ffn2_norm, m_ffn2_wg, m_ffn2_wu, m_ffn2_wd, m_final_norm, v_ffn1_norm, v_ffn1_wg, v_ffn1_wu, v_ffn1_wd, v_mix_norm, v_w_in, v_hgrn_lb_logits, v_hgrn_norm, v_conv_w, v_conv_b, v_lru_wa, v_lru_ba, v_lru_wx, v_lru_bx, v_lru_lambda, v_lru_norm, v_sgu_w, v_sgu_b, v_sgu_norm, v_w_out, v_ffn2_norm, v_ffn2_wg, v_ffn2_wu, v_ffn2_wd, v_final_norm):
    args = dict(locals())
    W = {n: args[n] for n in WEIGHTS}
    M = {n: args["m_" + n] for n in WEIGHTS}
    V = {n: args["v_" + n] for n in WEIGHTS}

    T, D = x.shape[1], x.shape[2]
    L = ffn1_norm.shape[0]
    d_a, d_b, d_c = hgrn_norm.shape[1], lru_norm.shape[1], sgu_norm.shape[1]
    col_b, col_c = 4 * d_a, 4 * d_a + 2 * d_b
    tm = 512 if T % 512 == 0 else T
    tm_w = 1024 if T % 1024 == 0 else tm
    tm_d = 256 if T % 256 == 0 else tm
    my_c = lax.axis_index("c")
    my_slot = 2 * lax.axis_index("x") + lax.axis_index("y")
    c_arr = jnp.reshape(my_c, (1,)).astype(jnp.int32)
    slot_arr = jnp.reshape(my_slot, (1,)).astype(jnp.int32)

    nb = len(BIG)
    gplan = _gather_ici_plan(nb)
    place_steps = 4 if all(W[n].shape[1] % 64 == 0 for n in BIG) else 2

    def placed(l):
        return _cast_place([W[n] for n in BIG], l, slot_arr, place_steps)

    conv_land = lax.dynamic_update_slice_in_dim(jnp.zeros((N_CHIPS,) + conv_w.shape, f32), conv_w[None], my_slot, axis=0)
    lands0 = placed(0)
    n_first = 3
    first = lands0[:n_first] + [conv_land]
    got = _exchange("gather0_ici", first, [_sds(a.shape, a.dtype) for a in first], _gather_ici_plan(n_first + 1),
                    aliases={a: a for a in range(n_first + 1)})
    got = _gather_d2d("gather0_d2d", got)
    G = [None] * L
    G[0] = dict(zip(BIG[:n_first], got[:n_first]))
    conv_full = jnp.transpose(got[n_first], (1, 2, 0, 3)).reshape(L, CONV_WIDTH, d_b)
    rest_plan = _gather_ici_plan(nb - n_first)
    rest_pending = _start_copies("gather_start_0", [], lands0[n_first:], rest_plan, got[0])

    def start_gather(l, after):
        return _start_copies(f"gather_start_{l}", [], placed(l), gplan, after)

    def finish_gather(l, pending, after):
        send, recv, _, lands, _ = pending
        lands = _wait_copies(f"gather_wait_{l}", send, recv, [], lands, gplan, after)
        return dict(zip(BIG, _gather_d2d("gather_d2d", lands)))

    lb = _lower_bounds(hgrn_lb_logits)
    avg_b = _group_avg_matrix(d_b, d_b // B_BLOCKS)
    avg_c = _group_avg_matrix(d_c, d_c // C_GROUPS)
    wa_dense = [_blockdiag(lru_wa[l]) for l in range(L)]
    wx_dense = [_blockdiag(lru_wx[l]) for l in range(L)]
    bexp = [jnp.repeat(sgu_b[l].T, d_c // C_GROUPS, axis=1) for l in range(L)]

    def lru_params(l):
        return (conv_full[l], conv_b[l][None], wa_dense[l], lru_ba[l].reshape(1, d_b), wx_dense[l],
                lru_bx[l].reshape(1, d_b), lru_lambda[l][None], lru_norm[l][None], avg_b)

    h = x.reshape(T, D)
    saved = []
    for l in range(L):
        s = {"h0": h}
        gain1, gain_mix = ffn1_norm[l][None], mix_norm[l][None]
        pending = None
        if l == 0:
            gain1 = gain1 + rest_pending[4][0:1, 0:1]
        elif l + 1 < L:
            pending = start_gather(l + 1, h)
            gain1 = gain1 + pending[4][0:1, 0:1]
        g = G[l]
        h, s["g1"], s["u1"] = _ffn_fwd(h, gain1, g["ffn1_wg"], g["ffn1_wu"], g["ffn1_wd"], tm)
        s["h1"] = h
        if l == 0:
            send, recv, _, lands, _ = rest_pending
            lands = _wait_copies("gather_wait_0", send, recv, [], lands, rest_plan, h)
            g.update(zip(BIG[n_first:], _gather_d2d("gather_d2d", lands)))
            if L > 1:
                pending = start_gather(1, g["w_in"])
                gain_mix = gain_mix + pending[4][0:1, 0:1]
        z = _proj_in_fwd(h, gain_mix, g["w_in"], tm)
        s["z"] = z
        s["oa"], s["o_pre"], s["states"] = _hgrn_fwd(z, lb[l][None], hgrn_norm[l][None], d_a)
        s["ob"], s["hl"] = _lru_fwd(z, col_b, d_b, *lru_params(l))
        s["oc"] = _sgu_fwd(z, col_c, d_c, sgu_w[l], bexp[l], sgu_norm[l][None], avg_c)
        h = _proj_out_fwd(h, s["oa"], s["ob"], s["oc"], g["w_out"], tm)
        s["h2"] = h
        h, s["g2"], s["u2"] = _ffn_fwd(h, ffn2_norm[l][None], g["ffn2_wg"], g["ffn2_wu"], g["ffn2_wd"], tm)
        saved.append(s)
        if pending is not None:
            G[l + 1] = finish_gather(l + 1, pending, h)

    dh, loss_part, d_final = _loss_head(h, final_norm[None], loss_target.reshape(T, D), tm)
    loss = lax.psum(loss_part[0, 0], ("x", "y", "c"))

    def pair_views(n_big):
        r = [(lambda i, o, p, a=a: i[a].at[:, pl.ds((1 - p.c) * (i[a].shape[1] // 2), i[a].shape[1] // 2)],
              lambda i, o, p, a=a: o[a], "sib") for a in range(n_big)]
        return r

    def chip_plan_for(n):
        return [(lambda s_, o, p, a=a, kind=kind: s_[a].at[p.peer_slot(kind)], lambda s_, o, p, a=a, j=j: o[a].at[j], kind)
                for a in range(n) for j, kind in enumerate(CHIP_KINDS)]

    chip_plan = chip_plan_for(nb)
    sbufs = {n: None for n in BIG}

    def pair_phase(arrs, extra=None):
        n = len(arrs)
        ins, remote = list(arrs), pair_views(n)
        outs = [_sds((N_CHIPS, a.shape[1] // 2, a.shape[2]), bf16) for a in arrs]
        if extra is not None:
            ins.append(extra)
            outs.append(_sds(extra.shape, f32))
            remote = remote + [(lambda i, o, p: i[n], lambda i, o, p: o[n], "sib")]
        recv = _exchange("grad_pair_d2d", ins, outs, remote)
        return _pair_sum(arrs, recv[:n], c_arr), (None if extra is None else _add(extra, recv[n]))

    def chip_sum_into(names, hs, lands, l):
        prev = [sbufs[n] for n in names] if sbufs[names[0]] is not None else []
        for n, buf in zip(names, _chip_sum(hs, lands, prev, slot_arr, c_arr, l, L)):
            sbufs[n] = buf

    def share(l, extra_in=(), extra_out=(), extra_remote=(), extra_local=()):
        remote = [(lambda i, o, p, a=a: _half(o[a].at[l], p.c), lambda i, o, p, a=a: _half(o[a].at[l], p.c), "sib")
                  for a in range(nb)]
        outs = [_sds(sbufs[n].shape, f32) for n in BIG] + list(extra_out)
        res = _exchange("grad_share_d2d", [sbufs[n] for n in BIG] + list(extra_in), outs, remote + list(extra_remote),
                        list(extra_local), aliases={a: a for a in range(nb)})
        for n, buf in zip(BIG, res[:nb]):
            sbufs[n] = buf
        return res[nb:]

    small = {n: [None] * L for n in SMALL if n != "final_norm"}
    chip_pending = early = None
    early_names = ("w_in", "w_out", "ffn2_wg", "ffn2_wu", "ffn2_wd")
    for l in reversed(range(L)):
        s, g = saved[l], G[l]
        gain2 = ffn2_norm[l][None]
        if chip_pending is not None:
            gain2 = gain2 + chip_pending[0][4][0:1, 0:1]
        dh, small["ffn2_norm"][l], dg, du, xn, dob = _ffn_bwd_dgrad(
            s["h2"], gain2, dh, s["g2"], s["u2"], g["ffn2_wg"], g["ffn2_wu"], g["ffn2_wd"], tm_d)
        dwg2, dwu2, dwd2 = _ffn_bwd_wgrad(xn, dob, s["g2"], s["u2"], dg, du, tm_w)
        doa, dob_, doc, dwo = _proj_out_bwd(dh, s["oa"], s["ob"], s["oc"], g["w_out"], tm)
        dz, small["hgrn_lb_logits"][l], small["hgrn_norm"][l] = _hgrn_bwd(
            s["z"], lb[l][None], hgrn_norm[l][None], s["o_pre"], s["states"], doa, d_a)
        (dz, small["conv_w"][l], small["conv_b"][l], dwa, small["lru_ba"][l], dwx, small["lru_bx"][l],
         small["lru_lambda"][l], small["lru_norm"][l]) = _lru_bwd(s["z"], col_b, d_b, s["hl"], dob_, dz, *lru_params(l))
        small["lru_wa"][l] = _blockdiag_extract(dwa, B_BLOCKS)
        small["lru_wx"][l] = _blockdiag_extract(dwx, B_BLOCKS)
        dz, small["sgu_w"][l], dsb, small["sgu_norm"][l] = _sgu_bwd(
            s["z"], col_c, d_c, doc, dz, sgu_w[l], bexp[l], sgu_norm[l][None], avg_c)
        small["sgu_b"][l] = dsb.T
        dh, small["mix_norm"][l], xn = _proj_in_bwd_dgrad(s["h1"], mix_norm[l][None], dh, dz, g["w_in"], tm)
        dwi = _proj_in_bwd_wgrad(xn, dz, N_CHIPS)
        gain1 = ffn1_norm[l][None]
        if l == 0:
            hs_e, _ = pair_phase([dwi, dwo, dwg2, dwu2, dwd2])
            lands = [lax.empty((N_CHIPS - 1,) + hh.shape[1:], bf16) for hh in hs_e]
            early = (_start_copies("grad_chip_start_0", hs_e, lands, chip_plan_for(len(hs_e)), hs_e[0]), hs_e)
            gain1 = gain1 + early[0][4][0:1, 0:1]
        dh, small["ffn1_norm"][l], dg, du, xn, dob = _ffn_bwd_dgrad(
            s["h0"], gain1, dh, s["g1"], s["u1"], g["ffn1_wg"], g["ffn1_wu"], g["ffn1_wd"], tm_d)
        dwg1, dwu1, dwd1 = _ffn_bwd_wgrad(xn, dob, s["g1"], s["u1"], dg, du, tm_w)
        layer_grads = [dwg1, dwu1, dwd1, dwi, dwo, dwg2, dwu2, dwd2]

        if chip_pending is not None:
            (send, recv, hs, lands, _), hsum_prev = chip_pending
            lands = _wait_copies(f"grad_chip_wait_{l + 1}", send, recv, hs, lands, chip_plan, dwg1)
            chip_sum_into(BIG, hsum_prev, lands, l + 1)
            share(l + 1)
            chip_pending = None
        if l > 0:
            hsum, _ = pair_phase(layer_grads)
            lands = [lax.empty((N_CHIPS - 1,) + hh.shape[1:], bf16) for hh in hsum]
            chip_pending = (_start_copies(f"grad_chip_start_{l}", hsum, lands, chip_plan, hsum[0]), hsum)
    grad_x = dh.reshape(x.shape)

    (send, recv, hs, lands, _), hs_e = early
    lands = _wait_copies("grad_chip_wait_0", send, recv, hs, lands, chip_plan_for(len(hs_e)), dwg1)
    chip_sum_into(early_names, hs_e, lands, 0)
    small_names = [n for n in SMALL]
    small_parts = [jnp.stack([jnp.reshape(v, (-1,)) for v in small[n]]) if n != "final_norm" else d_final for n in small_names]
    small_shapes = [p.shape for p in small_parts]
    packed = _pack(small_parts, 2 * 8 * 128).reshape(2, -1, 128)
    n_rows = packed.shape[1]
    late = [dwg1, dwu1, dwd1]
    nl = len(late)
    hsum, small_pair = pair_phase(late, packed)
    remote = chip_plan_for(nl) + [(lambda i, o, p: i[nl].at[p.c], lambda i, o, p: o[nl].at[p.slot], kind) for kind in CHIP_KINDS]
    local = [(lambda i, o, p: i[nl].at[p.c], lambda i, o, p: o[nl].at[p.slot])]
    outs = [_sds((N_CHIPS - 1,) + hh.shape[1:], bf16) for hh in hsum] + [_sds((N_CHIPS, n_rows, 128), f32)]
    recv_b = _exchange("grad_chip_ici_last", hsum + [small_pair], outs, remote, local)
    chip_sum_into(BIG[:nl], hsum, recv_b[:nl], 0)
    small_half = _sum_slots(recv_b[nl])
    (small_all,) = share(0, extra_in=[small_half], extra_out=[_sds(packed.shape, f32)],
                         extra_remote=[(lambda i, o, p: i[nb], lambda i, o, p: o[nb].at[p.c], "sib")],
                         extra_local=[(lambda i, o, p: i[nb], lambda i, o, p: o[nb].at[p.c])])
    grads = {n: sbufs[n].reshape(W[n].shape) for n in BIG}
    small_tot = _unpack(small_all.reshape(-1), small_shapes)
    for n, val in zip(small_names, small_tot):
        grads[n] = val
    grads["hgrn_lb_logits"] = _lower_bounds_bwd(hgrn_lb_logits, grads["hgrn_lb_logits"])
    shard_cols = conv_w.shape[2]
    grads["conv_w"] = lax.dynamic_slice_in_dim(grads["conv_w"].reshape(L, CONV_WIDTH, d_b), my_slot * shard_cols, shard_cols, axis=2)
    for n in SMALL:
        grads[n] = grads[n].reshape(W[n].shape)

    delta, new_m, new_v = {}, {}, {}
    for n in BIG:
        cols = W[n].shape[-1]
        d2, m2, v2 = _adamw(W[n].reshape(-1, cols), grads[n].reshape(-1, cols), M[n].reshape(-1, cols), V[n].reshape(-1, cols), 512)
        delta[n], new_m[n], new_v[n] = d2.reshape(W[n].shape), m2.reshape(W[n].shape), v2.reshape(W[n].shape)
    shapes = [W[n].shape for n in SMALL]
    packs = [_pack([src[n] for n in SMALL], 8 * 128).reshape(-1, 128) for src in (W, grads, M, V)]
    d2, m2, v2 = _adamw(*packs, 4096)
    for dst, val in ((delta, d2), (new_m, m2), (new_v, v2)):
        for n, piece in zip(SMALL, _unpack(val.reshape(-1), shapes)):
            dst[n] = piece

    return (loss, grad_x, *[grads[n] for n in WEIGHTS], *[delta[n] for n in WEIGHTS],
            *[new_m[n] for n in WEIGHTS], *[new_v[n] for n in WEIGHTS])
```

```python
import math

import numpy as np
import jax
import jax.numpy as jnp
from jax import lax
from jax.experimental import pallas as pl
from jax.experimental.pallas import tpu as pltpu

f32 = jnp.float32
bf16 = jnp.bfloat16
HI = lax.Precision.HIGHEST
MESH = pl.DeviceIdType.MESH

EPS = 1e-6
HEAD = 128
A_CHUNK = 64
A_SUB = 16
A_INNER = 2
SUBLANES = 8
B_BLOCKS = 4
B_CHUNK = 256
CONV_WIDTH = 4
LRU_C = 8.0
C_GROUPS = 4
C_CHUNK = 128
C_INNER = 4
N_CHIPS = 4
ADAM_LR, ADAM_B1, ADAM_B2, ADAM_EPS, ADAM_WD, ADAM_STEP = 0.001, 0.9, 0.999, 1e-08, 0.01, 10
VMEM_LIMIT = 56 * 1024 * 1024


def _cparams(n_axes):
    return pltpu.CompilerParams(dimension_semantics=("arbitrary",) * n_axes, vmem_limit_bytes=VMEM_LIMIT)


def _sds(shape, dtype):
    return jax.ShapeDtypeStruct(tuple(shape), dtype)


def _full(shape):
    n = len(shape)
    return pl.BlockSpec(tuple(shape), lambda *_: (0,) * n)


def _resident(shape):
    n = len(shape)
    return pl.BlockSpec(tuple(shape), lambda *_: (0,) * n, pipeline_mode=pl.Buffered(1))


def _dot(a, b):
    return jnp.dot(a, b, preferred_element_type=f32)


def _dot_nt(a, b):
    return lax.dot_general(a, b, (((1,), (1,)), ((), ())), preferred_element_type=f32)


def _dot_tn(a, b):
    return lax.dot_general(a, b, (((0,), (0,)), ((), ())), preferred_element_type=f32)


def _silu(x):
    return x * jax.nn.sigmoid(x)


def _group_avg_matrix(n, group):
    idx = np.arange(n) // group
    return jnp.asarray((idx[:, None] == idx[None, :]).astype(np.float32) / group)


class _Place:
    def __init__(self):
        self.x, self.y, self.c = lax.axis_index("x"), lax.axis_index("y"), lax.axis_index("c")
        self.slot = 2 * self.x + self.y

    def peer(self, kind):
        x, y, c = self.x, self.y, self.c
        return {"sib": (x, y, 1 - c), "fx": (1 - x, y, c), "fy": (x, 1 - y, c), "fxy": (1 - x, 1 - y, c)}[kind]

    def peer_slot(self, kind):
        x, y = self.x, self.y
        return {"fx": 2 * (1 - x) + y, "fy": 2 * x + (1 - y), "fxy": 2 * (1 - x) + (1 - y)}[kind]


CHIP_KINDS = ("fx", "fy", "fxy")


def _exchange(name, ins, outs, remote, local=(), aliases=None):
    n_in, n_out, n_r, n_l = len(ins), len(outs), len(remote), len(local)

    def body(*refs):
        in_refs, out_refs = refs[:n_in], refs[n_in:n_in + n_out]
        send, recv, lsem = refs[n_in + n_out:]
        p = _Place()
        lcopies = []
        for t, (src, dst) in enumerate(local):
            cp = pltpu.make_async_copy(src(in_refs, out_refs, p), dst(in_refs, out_refs, p), lsem.at[t])
            cp.start()
            lcopies.append(cp)
        copies = []
        for t, (src, dst, kind) in enumerate(remote):
            cp = pltpu.make_async_remote_copy(
                src_ref=src(in_refs, out_refs, p), dst_ref=dst(in_refs, out_refs, p),
                send_sem=send.at[t], recv_sem=recv.at[t], device_id=p.peer(kind), device_id_type=MESH)
            cp.start()
            copies.append(cp)
        for cp in copies:
            cp.wait_recv()
        for cp in copies:
            cp.wait_send()
        for cp in lcopies:
            cp.wait()

    anyspec = pl.BlockSpec(memory_space=pl.ANY)
    res = pl.pallas_call(
        body, name=name, out_shape=tuple(outs),
        in_specs=[anyspec] * n_in, out_specs=tuple([anyspec] * n_out),
        scratch_shapes=[pltpu.SemaphoreType.DMA((n_r,)), pltpu.SemaphoreType.DMA((n_r,)),
                        pltpu.SemaphoreType.DMA((max(n_l, 1),))],
        input_output_aliases=aliases or {},
        compiler_params=pltpu.CompilerParams(has_side_effects=True),
    )(*ins)
    return list(res)


HBM_SPEC = pl.BlockSpec(memory_space=pltpu.HBM)
SEM_SPEC = pl.BlockSpec(memory_space=pltpu.SEMAPHORE)
ANY_SPEC = pl.BlockSpec(memory_space=pl.ANY)
DATAFLOW = pltpu.SideEffectType.DATAFLOW_SIDE_EFFECTING


def _in_hbm(a):
    return pltpu.with_memory_space_constraint(a, pltpu.HBM)


def _start_copies(name, srcs, lands, remote, after):
    n_s, n_l, n_r = len(srcs), len(lands), len(remote)

    def body(*refs):
        src_refs, land_refs = refs[:n_s], refs[n_s:n_s + n_l]
        send, recv = refs[n_s + n_l + 1], refs[n_s + n_l + 2]
        token = refs[-1]
        p = _Place()
        for t, (src, dst, kind) in enumerate(remote):
            pltpu.make_async_remote_copy(
                src_ref=src(src_refs, land_refs, p), dst_ref=dst(src_refs, land_refs, p),
                send_sem=send.at[t], recv_sem=recv.at[t], device_id=p.peer(kind), device_id_type=MESH).start()
        token[...] = jnp.zeros_like(token)

    thru = [pltpu.HBM(a.shape, a.dtype) for a in list(srcs) + list(lands)]
    res = pl.pallas_call(
        body, name=name,
        out_shape=(pltpu.SemaphoreType.DMA((n_r,)), pltpu.SemaphoreType.DMA((n_r,)), *thru, _sds((8, 128), f32)),
        in_specs=[HBM_SPEC] * (n_s + n_l) + [ANY_SPEC],
        out_specs=(SEM_SPEC, SEM_SPEC, *([HBM_SPEC] * (n_s + n_l)), pl.BlockSpec(memory_space=pltpu.VMEM)),
        input_output_aliases={i: 2 + i for i in range(n_s + n_l)},
        compiler_params=pltpu.CompilerParams(has_side_effects=DATAFLOW),
    )(*[_in_hbm(a) for a in srcs], *[_in_hbm(a) for a in lands], after)
    return res[0], res[1], list(res[2:2 + n_s]), list(res[2 + n_s:2 + n_s + n_l]), res[-1]


def _wait_copies(name, send, recv, srcs, lands, remote, after):
    n_s, n_l = len(srcs), len(lands)

    def body(*refs):
        src_refs, land_refs = refs[:n_s], refs[n_s:n_s + n_l]
        send_ref, recv_ref = refs[n_s + n_l], refs[n_s + n_l + 1]
        p = _Place()
        for t, (src, dst, kind) in enumerate(remote):
            cp = pltpu.make_async_remote_copy(
                src_ref=src(src_refs, land_refs, p), dst_ref=dst(src_refs, land_refs, p),
                send_sem=send_ref.at[t], recv_sem=recv_ref.at[t], device_id=p.peer(kind), device_id_type=MESH)
            cp.wait_send()
            cp.wait_recv()

    thru = [pltpu.HBM(a.shape, a.dtype) for a in list(srcs) + list(lands)]
    res = pl.pallas_call(
        body, name=name, out_shape=tuple(thru),
        in_specs=[HBM_SPEC] * (n_s + n_l) + [SEM_SPEC, SEM_SPEC, ANY_SPEC],
        out_specs=tuple([HBM_SPEC] * (n_s + n_l)),
        input_output_aliases={i: i for i in range(n_s + n_l)},
        compiler_params=pltpu.CompilerParams(has_side_effects=DATAFLOW),
    )(*srcs, *lands, send, recv, after)
    return list(res[n_s:])


def _half(ref, c):
    n2 = ref.shape[0] // 2
    return ref.at[pl.ds(c * n2, n2)]


def _gather_ici_plan(n):
    def view(a):
        return lambda s, o, p: _half(o[a].at[p.slot], p.c)

    return [(view(a), view(a), kind) for a in range(n) for kind in CHIP_KINDS]


def _gather_d2d_plan(n):
    remote = []
    for a in range(n):
        for kind in CHIP_KINDS:
            view = lambda i, o, p, a=a, kind=kind: _half(o[a].at[p.peer_slot(kind)], p.c)
            remote.append((view, view, "sib"))
    return remote


def _gather_d2d(name, lands):
    n = len(lands)
    outs = [_sds(g.shape, g.dtype) for g in lands]
    return _exchange(name, list(lands), outs, _gather_d2d_plan(n), aliases={a: a for a in range(n)})


def _cast_place(weights, layer, slot_arr, n_steps=4):
    def body(s_ref, *refs):
        n = len(refs) // 2
        for a in range(n):
            refs[n + a][...] = refs[a][...].astype(bf16)

    in_specs, out_specs, out_shape = [], [], []
    for w in weights:
        _, R, Cc = w.shape
        rt = R // n_steps
        in_specs.append(pl.BlockSpec((None, rt, Cc), lambda i, s: (layer, i, 0)))
        out_specs.append(pl.BlockSpec((None, rt, Cc), lambda i, s: (s[0], i, 0)))
        out_shape.append(_sds((N_CHIPS, R, Cc), bf16))
    gs = pltpu.PrefetchScalarGridSpec(num_scalar_prefetch=1, grid=(n_steps,), in_specs=in_specs, out_specs=tuple(out_specs))
    return list(pl.pallas_call(body, name="cast_place", grid_spec=gs, out_shape=tuple(out_shape),
                               compiler_params=_cparams(1))(slot_arr, *weights))


def _ffn_fwd(h, gain, wg, wu, wd, tm):
    T, D = h.shape
    nsh, F = wg.shape[0], wg.shape[2]
    nt = T // tm

    def body(h_ref, gain_ref, wg_ref, wu_ref, wd_ref, out_ref, gs_ref, us_ref):
        hv = h_ref[...]
        r = lax.rsqrt(jnp.mean(hv * hv, axis=-1, keepdims=True) + EPS)
        xn = (hv * r * gain_ref[...]).astype(bf16)
        acc = None
        for k in range(nsh):
            g = _dot(xn, wg_ref[k])
            u = _dot(xn, wu_ref[k])
            gs_ref[k] = g.astype(bf16)
            us_ref[k] = u.astype(bf16)
            part = _dot((_silu(g) * u).astype(bf16), wd_ref[k])
            acc = part if acc is None else acc + part
        out_ref[...] = hv + 0.5 * acc

    sav = pl.BlockSpec((nsh, tm, F), lambda i: (0, i, 0))
    return pl.pallas_call(
        body, name="ffn_fwd", grid=(nt,),
        in_specs=[pl.BlockSpec((tm, D), lambda i: (i, 0)), _full((1, D)), _resident((nsh, D, F)), _resident((nsh, D, F)),
                  _resident((nsh, F, D))],
        out_specs=(pl.BlockSpec((tm, D), lambda i: (i, 0)), sav, sav),
        out_shape=(_sds((T, D), f32), _sds((nsh, T, F), bf16), _sds((nsh, T, F), bf16)),
        compiler_params=_cparams(1),
    )(h, gain, wg, wu, wd)


def _ffn_bwd_dgrad(h, gain, dout, gs, us, wg, wu, wd, tm):
    T, D = h.shape
    nsh, F = wg.shape[0], wg.shape[2]
    nt = T // tm

    def body(h_ref, gain_ref, dout_ref, gs_ref, us_ref, wg_ref, wu_ref, wd_ref,
             dh_ref, dgain_ref, dg_ref, du_ref, xn_ref, dob_ref):
        @pl.when(pl.program_id(0) == 0)
        def _():
            dgain_ref[...] = jnp.zeros_like(dgain_ref)

        hv = h_ref[...]
        r = lax.rsqrt(jnp.mean(hv * hv, axis=-1, keepdims=True) + EPS)
        xh = hv * r
        xn_ref[...] = (xh * gain_ref[...]).astype(bf16)
        dv = dout_ref[...]
        dob = (0.5 * dv).astype(bf16)
        dob_ref[...] = dob
        dxn = None
        for k in range(nsh):
            da = _dot_nt(dob, wd_ref[k])
            g = gs_ref[k].astype(f32)
            u = us_ref[k].astype(f32)
            sg = jax.nn.sigmoid(g)
            dg = (da * u * (sg * (1.0 + g * (1.0 - sg)))).astype(bf16)
            du = (da * (g * sg)).astype(bf16)
            dg_ref[k] = dg
            du_ref[k] = du
            part = _dot_nt(dg, wg_ref[k]) + _dot_nt(du, wu_ref[k])
            dxn = part if dxn is None else dxn + part
        dgain_ref[...] += jnp.sum(dxn * xh, axis=0, keepdims=True)
        dxh = dxn * gain_ref[...]
        dh_ref[...] = dv + r * (dxh - xh * jnp.mean(dxh * xh, axis=-1, keepdims=True))

    tok = pl.BlockSpec((tm, D), lambda i: (i, 0))
    sav = pl.BlockSpec((nsh, tm, F), lambda i: (0, i, 0))
    return pl.pallas_call(
        body, name="ffn_bwd_dgrad", grid=(nt,),
        in_specs=[tok, _full((1, D)), tok, sav, sav, _resident((nsh, D, F)), _resident((nsh, D, F)), _resident((nsh, F, D))],
        out_specs=(tok, _full((1, D)), sav, sav, tok, tok),
        out_shape=(_sds((T, D), f32), _sds((1, D), f32), _sds((nsh, T, F), bf16), _sds((nsh, T, F), bf16),
                   _sds((T, D), bf16), _sds((T, D), bf16)),
        compiler_params=_cparams(1),
    )(h, gain, dout, gs, us, wg, wu, wd)


def _ffn_bwd_wgrad(xn, dob, gs, us, dg, du, tm):
    T, D = xn.shape
    nsh, F = gs.shape[0], gs.shape[2]
    nt = T // tm

    def body(xn_ref, dob_ref, gs_ref, us_ref, dg_ref, du_ref, dwg_ref, dwu_ref, dwd_ref, ag_ref, au_ref, ad_ref):
        i = pl.program_id(1)

        @pl.when(i == 0)
        def _():
            ag_ref[...] = jnp.zeros_like(ag_ref)
            au_ref[...] = jnp.zeros_like(au_ref)
            ad_ref[...] = jnp.zeros_like(ad_ref)

        xn_v = xn_ref[...]
        ag_ref[...] += _dot_tn(xn_v, dg_ref[...])
        au_ref[...] += _dot_tn(xn_v, du_ref[...])
        g = gs_ref[...].astype(f32)
        a = (_silu(g) * us_ref[...].astype(f32)).astype(bf16)
        ad_ref[...] += _dot_tn(a, dob_ref[...])

        @pl.when(i == nt - 1)
        def _():
            dwg_ref[...] = ag_ref[...].astype(bf16)
            dwu_ref[...] = au_ref[...].astype(bf16)
            dwd_ref[...] = ad_ref[...].astype(bf16)

    tok = pl.BlockSpec((tm, D), lambda k, i: (i, 0))
    sav = pl.BlockSpec((None, tm, F), lambda k, i: (k, i, 0))
    wspec = pl.BlockSpec((None, D, F), lambda k, i: (k, 0, 0))
    wdspec = pl.BlockSpec((None, F, D), lambda k, i: (k, 0, 0))
    return pl.pallas_call(
        body, name="ffn_bwd_wgrad", grid=(nsh, nt),
        in_specs=[tok, tok, sav, sav, sav, sav],
        out_specs=(wspec, wspec, wdspec),
        out_shape=(_sds((nsh, D, F), bf16), _sds((nsh, D, F), bf16), _sds((nsh, F, D), bf16)),
        scratch_shapes=[pltpu.VMEM((D, F), f32), pltpu.VMEM((D, F), f32), pltpu.VMEM((F, D), f32)],
        compiler_params=_cparams(2),
    )(xn, dob, gs, us, dg, du)


def _proj_in_fwd(h, gain, w_in, tm):
    T, D = h.shape
    nsh, N = w_in.shape[0], w_in.shape[2]
    nt = T // tm

    def body(h_ref, gain_ref, w_ref, z_ref):
        hv = h_ref[...]
        r = lax.rsqrt(jnp.mean(hv * hv, axis=-1, keepdims=True) + EPS)
        xn = (hv * r * gain_ref[...]).astype(bf16)
        for k in range(nsh):
            z_ref[:, k * N:(k + 1) * N] = _dot(xn, w_ref[k])

    return pl.pallas_call(
        body, name="proj_in_fwd", grid=(nt,),
        in_specs=[pl.BlockSpec((tm, D), lambda i: (i, 0)), _full((1, D)), _full((nsh, D, N))],
        out_specs=pl.BlockSpec((tm, nsh * N), lambda i: (i, 0)),
        out_shape=_sds((T, nsh * N), f32),
        compiler_params=_cparams(1),
    )(h, gain, w_in)


def _proj_in_bwd_dgrad(h, gain, dres, dz, w_in, tm):
    T, D = h.shape
    nsh, N = w_in.shape[0], w_in.shape[2]
    nt = T // tm

    def body(h_ref, gain_ref, dres_ref, dz_ref, w_ref, dh_ref, dgain_ref, xn_ref):
        @pl.when(pl.program_id(0) == 0)
        def _():
            dgain_ref[...] = jnp.zeros_like(dgain_ref)

        dxn = _dot_nt(dz_ref[:, 0:N], w_ref[0])
        for k in range(1, nsh):
            dxn = dxn + _dot_nt(dz_ref[:, k * N:(k + 1) * N], w_ref[k])
        hv = h_ref[...]
        r = lax.rsqrt(jnp.mean(hv * hv, axis=-1, keepdims=True) + EPS)
        xh = hv * r
        xn_ref[...] = (xh * gain_ref[...]).astype(bf16)
        dgain_ref[...] += jnp.sum(dxn * xh, axis=0, keepdims=True)
        dxh = dxn * gain_ref[...]
        dh_ref[...] = dres_ref[...] + r * (dxh - xh * jnp.mean(dxh * xh, axis=-1, keepdims=True))

    tok = pl.BlockSpec((tm, D), lambda i: (i, 0))
    return pl.pallas_call(
        body, name="proj_in_bwd_dgrad", grid=(nt,),
        in_specs=[tok, _full((1, D)), tok, pl.BlockSpec((tm, nsh * N), lambda i: (i, 0)), _full((nsh, D, N))],
        out_specs=(tok, _full((1, D)), tok),
        out_shape=(_sds((T, D), f32), _sds((1, D), f32), _sds((T, D), bf16)),
        compiler_params=_cparams(1),
    )(h, gain, dres, dz, w_in)


def _proj_in_bwd_wgrad(xn, dz, nsh):
    T, D = xn.shape
    N = dz.shape[1] // nsh

    def body(xn_ref, dz_ref, dw_ref):
        dw_ref[...] = _dot_tn(xn_ref[...], dz_ref[...]).astype(bf16)

    return pl.pallas_call(
        body, name="proj_in_bwd_wgrad", grid=(nsh,),
        in_specs=[_full((T, D)), pl.BlockSpec((T, N), lambda k: (0, k))],
        out_specs=pl.BlockSpec((None, D, N), lambda k: (k, 0, 0)),
        out_shape=_sds((nsh, D, N), bf16),
        compiler_params=_cparams(1),
    )(xn, dz)


def _proj_out_fwd(h, oa, ob, oc, w_out, tm):
    T, D = h.shape
    nsh, R = w_out.shape[0], w_out.shape[1]
    da, db = oa.shape[1], ob.shape[1]
    nt = T // tm

    def body(h_ref, oa_ref, ob_ref, oc_ref, w_ref, out_ref):
        w = w_ref[...].reshape(nsh * R, D)
        out_ref[...] = (h_ref[...] + _dot(oa_ref[...], w[:da]) + _dot(ob_ref[...], w[da:da + db])
                        + _dot(oc_ref[...], w[da + db:]))

    def tok(n):
        return pl.BlockSpec((tm, n), lambda i: (i, 0))

    return pl.pallas_call(
        body, name="proj_out_fwd", grid=(nt,),
        in_specs=[tok(D), tok(da), tok(db), tok(oc.shape[1]), _full((nsh, R, D))],
        out_specs=tok(D), out_shape=_sds((T, D), f32),
        compiler_params=_cparams(1),
    )(h, oa, ob, oc, w_out)


def _proj_out_bwd(dh, oa, ob, oc, w_out, tm):
    T, D = dh.shape
    nsh, R = w_out.shape[0], w_out.shape[1]
    da, db, dc = oa.shape[1], ob.shape[1], oc.shape[1]
    nt = T // tm

    def body(dh_ref, oa_ref, ob_ref, oc_ref, w_ref, doa_ref, dob_ref, doc_ref, dw_ref, acc_ref):
        i = pl.program_id(0)

        @pl.when(i == 0)
        def _():
            acc_ref[...] = jnp.zeros_like(acc_ref)

        d = dh_ref[...].astype(bf16)
        w = w_ref[...].reshape(nsh * R, D)
        dm = _dot_nt(d, w)
        doa_ref[...] = dm[:, :da]
        dob_ref[...] = dm[:, da:da + db]
        doc_ref[...] = dm[:, da + db:]
        acc_ref[pl.ds(0, da), :] += _dot_tn(oa_ref[...], d)
        acc_ref[pl.ds(da, db), :] += _dot_tn(ob_ref[...], d)
        acc_ref[pl.ds(da + db, dc), :] += _dot_tn(oc_ref[...], d)

        @pl.when(i == nt - 1)
        def _():
            dw_ref[...] = acc_ref[...].astype(bf16).reshape(nsh, R, D)

    def tok(n):
        return pl.BlockSpec((tm, n), lambda i: (i, 0))

    wspec = _full((nsh, R, D))
    return pl.pallas_call(
        body, name="proj_out_bwd", grid=(nt,),
        in_specs=[tok(D), tok(da), tok(db), tok(dc), wspec],
        out_specs=(tok(da), tok(db), tok(dc), wspec),
        out_shape=(_sds((T, da), f32), _sds((T, db), f32), _sds((T, dc), f32), _sds((nsh, R, D), bf16)),
        scratch_shapes=[pltpu.VMEM((nsh * R, D), f32)],
        compiler_params=_cparams(1),
    )(dh, oa, ob, oc, w_out)


def _head_sum(m, n_heads):
    parts = []
    for hd in range(n_heads):
        s = jnp.sum(m[:, hd * HEAD:(hd + 1) * HEAD], axis=-1, keepdims=True)
        parts.append(jnp.broadcast_to(s, (m.shape[0], HEAD)))
    return parts[0] if n_heads == 1 else jnp.concatenate(parts, axis=1)


def _cat(parts, axis):
    return parts[0] if len(parts) == 1 else jnp.concatenate(parts, axis=axis)


def _hgrn_block(q, fl, iv, lb, states, tri, n_heads, n_inner):
    C = q.shape[0] // n_inner
    qs = _silu(q)
    forget = lb + (1.0 - lb) * jax.nn.sigmoid(fl)
    kk = 1.0 - forget
    logf = jnp.log(forget)
    b = jnp.dot(tri, logf, precision=HI, preferred_element_type=f32)
    vb = iv.astype(bf16)
    heads = [slice(hd * HEAD, (hd + 1) * HEAD) for hd in range(n_heads)]
    n_sub = C // A_SUB

    off, qe, kd, dec = {}, [], [], []
    for j in range(n_inner):
        c0 = j * C
        for blk in range(1, n_sub):
            lo = c0 + blk * A_SUB
            piv = b[lo:lo + 1]
            qt = (qs[lo:lo + A_SUB] * jnp.exp(b[lo:lo + A_SUB] - piv)).astype(bf16)
            kt = (kk[c0:lo] * jnp.exp(piv - b[c0:lo])).astype(bf16)
            parts = []
            for sl in heads:
                sc = _dot_nt(qt[:, sl], kt[:, sl])
                parts.append(_dot(sc.astype(bf16), vb[c0:lo, sl]))
            off[(j, blk)] = _cat(parts, 1)
        bj = b[c0:c0 + C]
        b_end = bj[C - 1:C]
        qe.append((qs[c0:c0 + C] * jnp.exp(bj)).astype(bf16))
        kd.append((kk[c0:c0 + C] * jnp.exp(b_end - bj)).astype(bf16))
        dec.append(jnp.exp(b_end))

    outs = []
    for j in range(n_inner):
        for blk in range(n_sub):
            lo = j * C + blk * A_SUB
            groups = [off[(j, blk)][r0:r0 + SUBLANES] if blk > 0 else None for r0 in range(0, A_SUB, SUBLANES)]
            for s in range(A_SUB):
                first = (s // SUBLANES) * SUBLANES
                n_rows = A_SUB - first
                row = lax.broadcasted_iota(jnp.int32, (n_rows, 1), 0) + first
                gate = jnp.where(row >= s, 0.0, -1e30)
                r = slice(lo + first, lo + A_SUB)
                m = qs[r] * jnp.exp((b[r] - b[lo + s:lo + s + 1]) + gate) * kk[lo + s:lo + s + 1]
                term = _head_sum(m, n_heads) * iv[lo + s:lo + s + 1]
                for gi in range(first // SUBLANES, A_SUB // SUBLANES):
                    piece = term[gi * SUBLANES - first:(gi + 1) * SUBLANES - first]
                    groups[gi] = piece if groups[gi] is None else groups[gi] + piece
            outs.extend(groups)
    o = jnp.concatenate(outs, axis=0)

    inter = []
    states = list(states)
    for j in range(n_inner):
        c0 = j * C
        parts = []
        for hd, sl in enumerate(heads):
            st = states[hd]
            parts.append(_dot_nt(qe[j][:, sl], st.astype(bf16)))
            states[hd] = dec[j][:, sl] * st + _dot_tn(vb[c0:c0 + C, sl], kd[j][:, sl])
        inter.append(_cat(parts, 1))
    return o + _cat(inter, 0), tuple(states)


def _hgrn_gate(o, g, gain, n_heads):
    ms = _head_sum(o * o, n_heads) * (1.0 / HEAD)
    return o * lax.rsqrt(ms + EPS) * gain * _silu(g)


def _tri_matrix(c, n_inner):
    idx = np.arange(c * n_inner)
    same = (idx[:, None] // c) == (idx[None, :] // c)
    return jnp.asarray((same & (idx[:, None] >= idx[None, :])).astype(np.float32))


def _hgrn_fwd(z, lb, gain, d_a):
    T = z.shape[0]
    C = A_CHUNK * A_INNER
    nc = T // C
    nh = d_a // HEAD
    tri = _tri_matrix(A_CHUNK, A_INNER)

    def body(q_ref, f_ref, i_ref, g_ref, lb_ref, gain_ref, tri_ref, out_ref, o_ref, st_ref, carry_ref):
        @pl.when(pl.program_id(0) == 0)
        def _():
            carry_ref[...] = jnp.zeros_like(carry_ref)

        states = tuple(carry_ref[hd] for hd in range(nh))
        st_ref[...] = carry_ref[...]
        o, new_states = _hgrn_block(q_ref[...], f_ref[...], i_ref[...], lb_ref[...], states, tri_ref[...], nh, A_INNER)
        o_ref[...] = o
        out_ref[...] = _hgrn_gate(o, g_ref[...], gain_ref[...], nh).astype(bf16)
        for hd in range(nh):
            carry_ref[hd] = new_states[hd]

    def col(j):
        return pl.BlockSpec((C, d_a), lambda c, j=j: (c, j))

    tok = pl.BlockSpec((C, d_a), lambda c: (c, 0))
    return pl.pallas_call(
        body, name="hgrn_fwd", grid=(nc,),
        in_specs=[col(0), col(1), col(2), col(3), _full((1, d_a)), _full((1, d_a)), _full((C, C))],
        out_specs=(tok, tok, pl.BlockSpec((None, nh, HEAD, HEAD), lambda c: (c, 0, 0, 0))),
        out_shape=(_sds((T, d_a), bf16), _sds((T, d_a), f32), _sds((nc, nh, HEAD, HEAD), f32)),
        scratch_shapes=[pltpu.VMEM((nh, HEAD, HEAD), f32)],
        compiler_params=_cparams(1),
    )(z, z, z, z, lb, gain, tri)


def _hgrn_bwd(z, lb, gain, o_pre, states, dout, d_a):
    T = z.shape[0]
    C = A_CHUNK * A_INNER
    nc = T // C
    nh = d_a // HEAD
    tri = _tri_matrix(A_CHUNK, A_INNER)

    def body(q_ref, f_ref, i_ref, g_ref, lb_ref, gain_ref, tri_ref, o_ref, st_ref, do_ref,
             dz_ref, dlb_ref, dgain_ref, carry_ref):
        @pl.when(pl.program_id(0) == 0)
        def _():
            carry_ref[...] = jnp.zeros_like(carry_ref)
            dlb_ref[...] = jnp.zeros_like(dlb_ref)
            dgain_ref[...] = jnp.zeros_like(dgain_ref)

        _, vjp_gate = jax.vjp(lambda o, g, gv: _hgrn_gate(o, g, gv, nh), o_ref[...], g_ref[...], gain_ref[...])
        d_o, dg, dgain = vjp_gate(do_ref[...])
        tri_v = tri_ref[...]

        def fn(q, fl, iv, lbv, sts):
            return _hgrn_block(q, fl, iv, lbv, sts, tri_v, nh, A_INNER)

        states_in = tuple(st_ref[hd] for hd in range(nh))
        _, vjp = jax.vjp(fn, q_ref[...], f_ref[...], i_ref[...], lb_ref[...], states_in)
        dstates = tuple(carry_ref[hd] for hd in range(nh))
        dq, df, di, dlb, dst = vjp((d_o, dstates))
        dz_ref[:, 0:d_a] = dq.astype(bf16)
        dz_ref[:, d_a:2 * d_a] = df.astype(bf16)
        dz_ref[:, 2 * d_a:3 * d_a] = di.astype(bf16)
        dz_ref[:, 3 * d_a:4 * d_a] = dg.astype(bf16)
        dlb_ref[...] += dlb
        dgain_ref[...] += dgain
        for hd in range(nh):
            carry_ref[hd] = dst[hd]

    def col(j):
        return pl.BlockSpec((C, d_a), lambda c, j=j: (nc - 1 - c, j))

    tok = pl.BlockSpec((C, d_a), lambda c: (nc - 1 - c, 0))
    return pl.pallas_call(
        body, name="hgrn_bwd", grid=(nc,),
        in_specs=[col(0), col(1), col(2), col(3), _full((1, d_a)), _full((1, d_a)), _full((C, C)), tok,
                  pl.BlockSpec((None, nh, HEAD, HEAD), lambda c: (nc - 1 - c, 0, 0, 0)), tok],
        out_specs=(pl.BlockSpec((C, 4 * d_a), lambda c: (nc - 1 - c, 0)), _full((1, d_a)), _full((1, d_a))),
        out_shape=(_sds(z.shape, bf16), _sds((1, d_a), f32), _sds((1, d_a), f32)),
        scratch_shapes=[pltpu.VMEM((nh, HEAD, HEAD), f32)],
        compiler_params=_cparams(1),
    )(z, z, z, z, lb, gain, tri, o_pre, states, dout)


def _one_minus_exp(x):
    series = -x * (1.0 + x * (0.5 + x * (1.0 / 6.0 + x * (1.0 / 24.0))))
    return jnp.where(x > -0.03, series, 1.0 - jnp.exp(x))


def _lru_pre(xc, wa, ba, wx, bx, lam):
    xb16 = xc.astype(bf16)
    r = jax.nn.sigmoid(_dot(xb16, wa.astype(bf16)) + ba)
    gi = jax.nn.sigmoid(_dot(xb16, wx.astype(bf16)) + bx)
    log_a = -LRU_C * r * jax.nn.softplus(-lam)
    a = jnp.exp(log_a)
    mult = jnp.sqrt(_one_minus_exp(2.0 * log_a))
    return a, mult * gi * xc


def _lru_post(h, gate, gain, avg):
    y = h * jax.nn.gelu(gate)
    ms = _group_mean(y * y, avg)
    return y * lax.rsqrt(ms + EPS) * gain


def _shift_down(x, d, prev):
    row = lax.broadcasted_iota(jnp.int32, x.shape, 0)
    return jnp.where(row >= d, pltpu.roll(x, d, 0), pltpu.roll(prev, d, 0))


def _shift_up(x, d, nxt):
    n = x.shape[0]
    row = lax.broadcasted_iota(jnp.int32, x.shape, 0)
    return jnp.where(row < n - d, pltpu.roll(x, n - d, 0), pltpu.roll(nxt, n - d, 0))


def _scan_rows(a, u, reverse):
    n = a.shape[0]
    row = lax.broadcasted_iota(jnp.int32, a.shape, 0)
    d = 1
    while d < n:
        shift, ok = (n - d, row < n - d) if reverse else (d, row >= d)
        su = jnp.where(ok, pltpu.roll(u, shift, 0), 0.0)
        sa = jnp.where(ok, pltpu.roll(a, shift, 0), 1.0)
        u = u + a * su
        a = a * sa
        d *= 2
    return a, u


def _conv(xb, xprev, cw, cb):
    xc = cb + cw[CONV_WIDTH - 1:CONV_WIDTH] * xb
    for d in range(1, CONV_WIDTH):
        xc = xc + cw[CONV_WIDTH - 1 - d:CONV_WIDTH - d] * _shift_down(xb, d, xprev)
    return xc


def _lru_fwd(z, col0, d_b, cw, cb, wa, ba, wx, bx, lam, gain, avg):
    T = z.shape[0]
    R = min(B_CHUNK, T)
    nr = T // R
    jb = col0 // d_b

    def body(xb_ref, gate_ref, cw_ref, cb_ref, wa_ref, ba_ref, wx_ref, bx_ref, lam_ref, gain_ref, avg_ref,
             out_ref, h_ref, xprev_ref, hprev_ref):
        @pl.when(pl.program_id(0) == 0)
        def _():
            xprev_ref[...] = jnp.zeros_like(xprev_ref)
            hprev_ref[...] = jnp.zeros_like(hprev_ref)

        xb = xb_ref[...]
        xc = _conv(xb, xprev_ref[...], cw_ref[...], cb_ref[...])
        a, u = _lru_pre(xc, wa_ref[...], ba_ref[...], wx_ref[...], bx_ref[...], lam_ref[...])
        acum, hl = _scan_rows(a, u, False)
        h = hl + acum * hprev_ref[R - 1:R, :]
        h_ref[...] = h
        out_ref[...] = _lru_post(h, gate_ref[...], gain_ref[...], avg_ref[...]).astype(bf16)
        xprev_ref[...] = xb
        hprev_ref[...] = h

    vec = _full((1, d_b))
    return pl.pallas_call(
        body, name="lru_fwd", grid=(nr,),
        in_specs=[pl.BlockSpec((R, d_b), lambda i: (i, jb)), pl.BlockSpec((R, d_b), lambda i: (i, jb + 1)),
                  _full((CONV_WIDTH, d_b)), vec, _full((d_b, d_b)), vec, _full((d_b, d_b)), vec, vec, vec, _full((d_b, d_b))],
        out_specs=(pl.BlockSpec((R, d_b), lambda i: (i, 0)), pl.BlockSpec((R, d_b), lambda i: (i, 0))),
        out_shape=(_sds((T, d_b), bf16), _sds((T, d_b), f32)),
        scratch_shapes=[pltpu.VMEM((R, d_b), f32), pltpu.VMEM((R, d_b), f32)],
        compiler_params=_cparams(1),
    )(z, z, cw, cb, wa, ba, wx, bx, lam, gain, avg)


def _lru_bwd(z, col0, d_b, hsave, dout, dz_buf, cw, cb, wa, ba, wx, bx, lam, gain, avg):
    T = z.shape[0]
    R = min(B_CHUNK, T)
    nr = T // R
    jb = col0 // d_b

    def body(xb_ref, xp_ref, gate_ref, h_ref, hp_ref, do_ref,
             cw_ref, cb_ref, wa_ref, ba_ref, wx_ref, bx_ref, lam_ref, gain_ref, avg_ref, dzin_ref,
             dz_ref, dcw_ref, dcb_ref, dwa_ref, dba_ref, dwx_ref, dbx_ref, dlam_ref, dgain_ref,
             gfirst_ref, afirst_ref, dxcn_ref):
        step = pl.program_id(0)
        first_in_time = step == nr - 1

        @pl.when(step == 0)
        def _():
            for r in (dcw_ref, dcb_ref, dwa_ref, dba_ref, dwx_ref, dbx_ref, dlam_ref, dgain_ref,
                      gfirst_ref, afirst_ref, dxcn_ref):
                r[...] = jnp.zeros_like(r)

        xb = xb_ref[...]
        keep = jnp.where(first_in_time, 0.0, 1.0)
        xprev = xp_ref[...] * keep
        hprev = hp_ref[...] * keep
        cw = cw_ref[...]
        xc = _conv(xb, xprev, cw, cb_ref[...])
        (a, _), vjp_pre = jax.vjp(_lru_pre, xc, wa_ref[...], ba_ref[...], wx_ref[...], bx_ref[...], lam_ref[...])
        h = h_ref[...]
        avg = avg_ref[...]
        _, vjp_post = jax.vjp(lambda hh, gg, gn: _lru_post(hh, gg, gn, avg), h, gate_ref[...], gain_ref[...])
        dh, dgate, dgain = vjp_post(do_ref[...])
        a_next = _shift_up(a, 1, jnp.broadcast_to(afirst_ref[0:1, :], a.shape))
        acum, gl = _scan_rows(a_next, dh, True)
        gtot = gl + acum * gfirst_ref[0:1, :]
        da = gtot * _shift_down(h, 1, hprev)
        dxc, dwa, dba, dwx, dbx, dlam = vjp_pre((da, gtot))
        dxcn = dxcn_ref[...]
        dxb = cw[CONV_WIDTH - 1:CONV_WIDTH] * dxc
        dcw_ref[CONV_WIDTH - 1:CONV_WIDTH, :] += jnp.sum(dxc * xb, axis=0, keepdims=True)
        for d in range(1, CONV_WIDTH):
            tap = CONV_WIDTH - 1 - d
            dxb = dxb + cw[tap:tap + 1] * _shift_up(dxc, d, dxcn)
            dcw_ref[tap:tap + 1, :] += jnp.sum(dxc * _shift_down(xb, d, xprev), axis=0, keepdims=True)
        dz_ref[:, 0:d_b] = dxb.astype(bf16)
        dz_ref[:, d_b:2 * d_b] = dgate.astype(bf16)
        dcb_ref[...] += jnp.sum(dxc, axis=0, keepdims=True)
        dwa_ref[...] += dwa
        dba_ref[...] += dba
        dwx_ref[...] += dwx
        dbx_ref[...] += dbx
        dlam_ref[...] += dlam
        dgain_ref[...] += dgain
        gfirst_ref[...] = jnp.broadcast_to(gtot[0:1, :], gfirst_ref.shape)
        afirst_ref[...] = jnp.broadcast_to(a[0:1, :], afirst_ref.shape)
        dxcn_ref[...] = dxc

    vec = _full((1, d_b))
    mat = _full((d_b, d_b))

    def cur(j):
        return pl.BlockSpec((R, d_b), lambda i, j=j: (nr - 1 - i, j))

    def prev(j):
        return pl.BlockSpec((R, d_b), lambda i, j=j: (jnp.maximum(nr - 2 - i, 0), j))

    return pl.pallas_call(
        body, name="lru_bwd", grid=(nr,),
        in_specs=[cur(jb), prev(jb), cur(jb + 1), cur(0), prev(0), cur(0),
                  _full((CONV_WIDTH, d_b)), vec, mat, vec, mat, vec, vec, vec, mat, ANY_SPEC],
        out_specs=(pl.BlockSpec((R, 2 * d_b), lambda i: (nr - 1 - i, col0 // (2 * d_b))), _full((CONV_WIDTH, d_b)), vec, mat, vec, mat, vec, vec, vec),
        out_shape=(_sds(dz_buf.shape, bf16), _sds((CONV_WIDTH, d_b), f32), _sds((1, d_b), f32), _sds((d_b, d_b), f32),
                   _sds((1, d_b), f32), _sds((d_b, d_b), f32), _sds((1, d_b), f32), _sds((1, d_b), f32), _sds((1, d_b), f32)),
        scratch_shapes=[pltpu.VMEM((8, d_b), f32), pltpu.VMEM((8, d_b), f32), pltpu.VMEM((R, d_b), f32)],
        input_output_aliases={15: 0},
        compiler_params=_cparams(1),
    )(z, z, z, hsave, hsave, dout, cw, cb, wa, ba, wx, bx, lam, gain, avg, dz_buf)


def _two_pass(x, m16):
    hi = x.astype(bf16)
    lo = (x - hi.astype(f32)).astype(bf16)
    return _dot(hi, m16) + _dot(lo, m16)


@jax.custom_vjp
def _group_mean(x, avg):
    return _two_pass(x, avg.astype(bf16))


def _group_mean_fwd(x, avg):
    return _group_mean(x, avg), avg


def _group_mean_bwd(avg, ct):
    return _two_pass(ct, avg.astype(bf16)), jnp.zeros_like(avg)


_group_mean.defvjp(_group_mean_fwd, _group_mean_bwd)


def _sgu_chunk(u_in, v_in, w, bexp, gain, avg, n_groups):
    C, d_c = u_in.shape
    gd = d_c // n_groups
    u = jax.nn.gelu(u_in)
    v = jax.nn.gelu(v_in)
    mu = _group_mean(v, avg)
    vc = v - mu
    var = _group_mean(vc * vc, avg)
    vh = (vc * lax.rsqrt(var + EPS)).astype(bf16)
    lane = lax.broadcasted_iota(jnp.int32, (1, d_c), 1)
    causal = lax.broadcasted_iota(jnp.int32, (C, C), 0) >= lax.broadcasted_iota(jnp.int32, (C, C), 1)
    zz = bexp
    for g in range(n_groups):
        wg = jnp.where(causal, w[g], 0.0).astype(bf16)
        zz = zz + jnp.where((lane >= g * gd) & (lane < (g + 1) * gd), _dot(wg, vh), 0.0)
    y = u * zz
    ms = _group_mean(y * y, avg)
    return y * lax.rsqrt(ms + EPS) * gain


def _sgu_inner(T):
    return C_INNER if T % (C_CHUNK * C_INNER) == 0 else 1


def _sgu_fwd(z, col0, d_c, w, bexp, gain, avg):
    T = z.shape[0]
    C = C_CHUNK
    n_in = _sgu_inner(T)
    R = C * n_in
    jb = col0 // d_c
    G = w.shape[0]

    def body(u_ref, v_ref, w_ref, b_ref, gain_ref, avg_ref, out_ref):
        w_v, b_v, gain_v, avg = w_ref[...], b_ref[...], gain_ref[...], avg_ref[...]
        for j in range(n_in):
            rows = pl.ds(j * C, C)
            out_ref[rows, :] = _sgu_chunk(u_ref[rows, :], v_ref[rows, :], w_v, b_v, gain_v, avg, G).astype(bf16)

    return pl.pallas_call(
        body, name="sgu_fwd", grid=(T // R,),
        in_specs=[pl.BlockSpec((R, d_c), lambda i: (i, jb)), pl.BlockSpec((R, d_c), lambda i: (i, jb + 1)),
                  _full((G, C, C)), _full((C, d_c)), _full((1, d_c)), _full((d_c, d_c))],
        out_specs=pl.BlockSpec((R, d_c), lambda i: (i, 0)),
        out_shape=_sds((T, d_c), bf16),
        compiler_params=_cparams(1),
    )(z, z, w, bexp, gain, avg)


def _sgu_bwd(z, col0, d_c, dout, dz_buf, w, bexp, gain, avg):
    T = z.shape[0]
    C = C_CHUNK
    n_in = _sgu_inner(T)
    R = C * n_in
    nc = T // R
    jb = col0 // d_c
    G = w.shape[0]
    gd = d_c // G

    def body(u_ref, v_ref, do_ref, w_ref, b_ref, gain_ref, avg_ref, dzin_ref, dz_ref, dw_ref, db_ref, dgain_ref, dbexp_ref):
        step = pl.program_id(0)

        @pl.when(step == 0)
        def _():
            dw_ref[...] = jnp.zeros_like(dw_ref)
            dgain_ref[...] = jnp.zeros_like(dgain_ref)
            dbexp_ref[...] = jnp.zeros_like(dbexp_ref)

        avg, w_v, b_v, gain_v = avg_ref[...], w_ref[...], b_ref[...], gain_ref[...]
        dw = dbexp = dgain = None
        for j in range(n_in):
            rows = pl.ds(j * C, C)
            _, vjp = jax.vjp(lambda a, b, c, d, e: _sgu_chunk(a, b, c, d, e, avg, G),
                             u_ref[rows, :], v_ref[rows, :], w_v, b_v, gain_v)
            du, dv, dw_j, dbexp_j, dgain_j = vjp(do_ref[rows, :])
            dz_ref[rows, 0:d_c] = du.astype(bf16)
            dz_ref[rows, d_c:2 * d_c] = dv.astype(bf16)
            dw = dw_j if dw is None else dw + dw_j
            dbexp = dbexp_j if dbexp is None else dbexp + dbexp_j
            dgain = dgain_j if dgain is None else dgain + dgain_j
        dw_ref[...] += dw
        dbexp_ref[...] += dbexp
        dgain_ref[...] += dgain

        @pl.when(step == nc - 1)
        def _():
            lane = lax.broadcasted_iota(jnp.int32, (1, d_c), 1)
            acc = dbexp_ref[...]
            for g in range(G):
                sel = jnp.where((lane >= g * gd) & (lane < (g + 1) * gd), acc, 0.0)
                db_ref[:, g:g + 1] = jnp.sum(sel, axis=1, keepdims=True)

    return pl.pallas_call(
        body, name="sgu_bwd", grid=(nc,),
        in_specs=[pl.BlockSpec((R, d_c), lambda i: (i, jb)), pl.BlockSpec((R, d_c), lambda i: (i, jb + 1)),
                  pl.BlockSpec((R, d_c), lambda i: (i, 0)),
                  _full((G, C, C)), _full((C, d_c)), _full((1, d_c)), _full((d_c, d_c)), ANY_SPEC],
        out_specs=(pl.BlockSpec((R, 2 * d_c), lambda i: (i, col0 // (2 * d_c))), _full((G, C, C)), _full((C, G)), _full((1, d_c))),
        out_shape=(_sds(dz_buf.shape, bf16), _sds((G, C, C), f32), _sds((C, G), f32), _sds((1, d_c), f32)),
        scratch_shapes=[pltpu.VMEM((C, d_c), f32)],
        input_output_aliases={7: 0},
        compiler_params=_cparams(1),
    )(z, z, dout, w, bexp, gain, avg, dz_buf)


def _loss_head(h, gain, target, tm):
    T, D = h.shape
    nt = T // tm

    def body(h_ref, gain_ref, tgt_ref, dh_ref, loss_ref, dgain_ref):
        @pl.when(pl.program_id(0) == 0)
        def _():
            loss_ref[...] = jnp.zeros_like(loss_ref)
            dgain_ref[...] = jnp.zeros_like(dgain_ref)

        hv = h_ref[...]
        gain_v = gain_ref[...]
        r = lax.rsqrt(jnp.mean(hv * hv, axis=-1, keepdims=True) + EPS)
        xh = hv * r
        e = xh * gain_v - tgt_ref[...]
        loss_ref[...] += 0.5 * jnp.sum(jnp.mean(e * e, axis=-1, keepdims=True), axis=0, keepdims=True)
        dy = e * (1.0 / D)
        dgain_ref[...] += jnp.sum(dy * xh, axis=0, keepdims=True)
        dxh = dy * gain_v
        dh_ref[...] = r * (dxh - xh * jnp.mean(dxh * xh, axis=-1, keepdims=True))

    tok = pl.BlockSpec((tm, D), lambda i: (i, 0))
    return pl.pallas_call(
        body, name="loss_head", grid=(nt,),
        in_specs=[tok, _full((1, D)), tok],
        out_specs=(tok, _full((1, 128)), _full((1, D))),
        out_shape=(_sds((T, D), f32), _sds((1, 128), f32), _sds((1, D), f32)),
        compiler_params=_cparams(1),
    )(h, gain, target)


def _lower_bounds_fn(logits):
    n = logits.shape[0]
    mx = jnp.max(logits, axis=0, keepdims=True)
    ex = jnp.exp(logits - mx)
    soft = ex / jnp.sum(ex, axis=0, keepdims=True)
    rows = [jnp.zeros_like(soft[0:1])]
    for l in range(1, n):
        rows.append(rows[-1] + soft[l:l + 1])
    return jnp.concatenate(rows, axis=0)


def _lower_bounds(logits):
    def body(x_ref, o_ref):
        o_ref[...] = _lower_bounds_fn(x_ref[...])

    return pl.pallas_call(body, name="lower_bounds", out_shape=_sds(logits.shape, f32))(logits)


def _lower_bounds_bwd(logits, dlb):
    def body(x_ref, d_ref, o_ref):
        _, vjp = jax.vjp(_lower_bounds_fn, x_ref[...])
        o_ref[...] = vjp(d_ref[...])[0]

    return pl.pallas_call(body, name="lower_bounds_bwd", out_shape=_sds(logits.shape, f32))(logits, dlb)


def _adamw(w, g, m, v, rows_blk):
    R, Cc = w.shape
    rb = R if R <= rows_blk else math.gcd(R, rows_blk)

    def body(w_ref, g_ref, m_ref, v_ref, d_ref, nm_ref, nv_ref):
        gv = g_ref[...]
        m2 = ADAM_B1 * m_ref[...] + (1.0 - ADAM_B1) * gv
        v2 = ADAM_B2 * v_ref[...] + (1.0 - ADAM_B2) * (gv * gv)
        m_hat = m2 / (1.0 - ADAM_B1 ** ADAM_STEP)
        v_hat = v2 / (1.0 - ADAM_B2 ** ADAM_STEP)
        d_ref[...] = -ADAM_LR * (m_hat / (jnp.sqrt(v_hat) + ADAM_EPS) + ADAM_WD * w_ref[...])
        nm_ref[...] = m2
        nv_ref[...] = v2

    spec = pl.BlockSpec((rb, Cc), lambda i: (i, 0))
    return pl.pallas_call(
        body, name="adamw", grid=(R // rb,),
        in_specs=[spec] * 4, out_specs=(spec,) * 3, out_shape=(_sds((R, Cc), f32),) * 3,
        compiler_params=_cparams(1),
    )(w, g, m, v)


def _pair_sum(grads, recv, c_arr):
    n = len(grads)
    nsh = grads[0].shape[0]

    def body(c_ref, *refs):
        for a in range(n):
            refs[2 * n + a][...] = (refs[a][...].astype(f32) + refs[n + a][...].astype(f32)).astype(bf16)

    g_specs, r_specs, out_shape = [], [], []
    for g in grads:
        _, R, Cc = g.shape
        r2 = R // 2
        g_specs.append(pl.BlockSpec((None, r2, Cc), lambda s, c: (s, c[0], 0)))
        r_specs.append(pl.BlockSpec((None, r2, Cc), lambda s, c: (s, 0, 0)))
        out_shape.append(_sds((nsh, r2, Cc), bf16))
    gs = pltpu.PrefetchScalarGridSpec(num_scalar_prefetch=1, grid=(nsh,), in_specs=g_specs + r_specs, out_specs=tuple(r_specs))
    return list(pl.pallas_call(body, name="pair_sum", grid_spec=gs, out_shape=tuple(out_shape),
                               compiler_params=_cparams(1))(c_arr, *grads, *recv))


def _add(a, b):
    def body(a_ref, b_ref, o_ref):
        o_ref[...] = a_ref[...] + b_ref[...]

    return pl.pallas_call(body, name="pair_sum_small", out_shape=_sds(a.shape, f32))(a, b)


def _chip_sum(hsum, recv, bufs, slot_arr, c_arr, layer, n_layers):
    n = len(hsum)
    prev = list(bufs)
    steps = 2

    def body(s_ref, c_ref, *refs):
        outs = refs[len(refs) - n:]
        for a in range(n):
            acc = refs[a][...].astype(f32)
            for j in range(N_CHIPS - 1):
                acc = acc + refs[n + a][j].astype(f32)
            outs[a][...] = acc

    h_specs, r_specs, o_specs, out_shape = [], [], [], []
    for hh in hsum:
        _, r2, Cc = hh.shape
        rt = r2 // steps
        h_specs.append(pl.BlockSpec((None, rt, Cc), lambda i, s, c: (s[0], i, 0)))
        r_specs.append(pl.BlockSpec((N_CHIPS - 1, rt, Cc), lambda i, s, c: (0, i, 0)))
        o_specs.append(pl.BlockSpec((None, rt, Cc), lambda i, s, c: (layer, c[0] * steps + i, 0)))
        out_shape.append(_sds((n_layers, 2 * r2, Cc), f32))
    gs = pltpu.PrefetchScalarGridSpec(num_scalar_prefetch=2, grid=(steps,),
                                      in_specs=h_specs + r_specs + [ANY_SPEC] * len(prev), out_specs=tuple(o_specs))
    return list(pl.pallas_call(body, name="chip_sum", grid_spec=gs, out_shape=tuple(out_shape),
                               input_output_aliases={2 + 2 * n + a: a for a in range(len(prev))},
                               compiler_params=_cparams(1))(slot_arr, c_arr, *hsum, *recv, *prev))


def _sum_slots(x):
    def body(x_ref, o_ref):
        acc = x_ref[0]
        for j in range(1, x.shape[0]):
            acc = acc + x_ref[j]
        o_ref[...] = acc

    return pl.pallas_call(body, name="sum_slots", out_shape=_sds(x.shape[1:], f32))(x)


def _blockdiag(w):
    nb, bd, _ = w.shape
    eye = jnp.eye(nb, dtype=w.dtype)
    return (eye[:, None, :, None] * w[:, :, None, :]).reshape(nb * bd, nb * bd)


def _blockdiag_extract(dense, nb):
    bd = dense.shape[0] // nb
    d4 = dense.reshape(nb, bd, nb, bd)
    return jnp.stack([d4[i, :, i, :] for i in range(nb)])


def _pack(arrays, multiple):
    flat = jnp.concatenate([a.reshape(-1).astype(f32) for a in arrays])
    pad = (-flat.shape[0]) % multiple
    return jnp.pad(flat, (0, pad))


def _unpack(flat, shapes):
    out, off = [], 0
    for s in shapes:
        n = int(np.prod(s))
        out.append(flat[off:off + n].reshape(s))
        off += n
    return out


BIG = ("ffn1_wg", "ffn1_wu", "ffn1_wd", "w_in", "w_out", "ffn2_wg", "ffn2_wu", "ffn2_wd")
SMALL = ("ffn1_norm", "mix_norm", "hgrn_lb_logits", "hgrn_norm", "conv_w", "conv_b", "lru_wa", "lru_ba", "lru_wx",
         "lru_bx", "lru_lambda", "lru_norm", "sgu_w", "sgu_b", "sgu_norm", "ffn2_norm", "final_norm")
WEIGHTS = ("ffn1_norm", "ffn1_wg", "ffn1_wu", "ffn1_wd", "mix_norm", "w_in", "hgrn_lb_logits", "hgrn_norm", "conv_w",
           "conv_b", "lru_wa", "lru_ba", "lru_wx", "lru_bx", "lru_lambda", "lru_norm", "sgu_w", "sgu_b", "sgu_norm",
           "w_out", "ffn2_norm", "ffn2_wg", "ffn2_wu", "ffn2_wd", "final_norm")


def kernel(x, ffn1_norm, ffn1_wg, ffn1_wu, ffn1_wd, mix_norm, w_in, hgrn_lb_logits, hgrn_norm, conv_w, conv_b, lru_wa, lru_ba, lru_wx, lru_bx, lru_lambda, lru_norm, sgu_w, sgu_b, sgu_norm, w_out, ffn2_norm, ffn2_wg, ffn2_wu, ffn2_wd, final_norm, loss_target, m_ffn1_norm, m_ffn1_wg, m_ffn1_wu, m_ffn1_wd, m_mix_norm, m_w_in, m_hgrn_lb_logits, m_hgrn_norm, m_conv_w, m_conv_b, m_lru_wa, m_lru_ba, m_lru_wx, m_lru_bx, m_lru_lambda, m_lru_norm, m_sgu_w, m_sgu_b, m_sgu_norm, m_w_out, m_ffn2_norm, m_ffn2_wg, m_ffn2_wu, m_ffn2_wd, m_final_norm, v_ffn1_norm, v_ffn1_wg, v_ffn1_wu, v_ffn1_wd, v_mix_norm, v_w_in, v_hgrn_lb_logits, v_hgrn_norm, v_conv_w, v_conv_b, v_lru_wa, v_lru_ba, v_lru_wx, v_lru_bx, v_lru_lambda, v_lru_norm, v_sgu_w, v_sgu_b, v_sgu_norm, v_w_out, v_ffn2_norm, v_ffn2_wg, v_ffn2_wu, v_ffn2_wd, v_final_norm):
    args = dict(locals())
    W = {n: args[n] for n in WEIGHTS}
    M = {n: args["m_" + n] for n in WEIGHTS}
    V = {n: args["v_" + n] for n in WEIGHTS}

    T, D = x.shape[1], x.shape[2]
    L = ffn1_norm.shape[0]
    d_a, d_b, d_c = hgrn_norm.shape[1], lru_norm.shape[1], sgu_norm.shape[1]
    col_b, col_c = 4 * d_a, 4 * d_a + 2 * d_b
    tm = 512 if T % 512 == 0 else T
    tm_w = 1024 if T % 1024 == 0 else tm
    tm_d = 256 if T % 256 == 0 else tm
    my_c = lax.axis_index("c")
    my_slot = 2 * lax.axis_index("x") + lax.axis_index("y")
    c_arr = jnp.reshape(my_c, (1,)).astype(jnp.int32)
    slot_arr = jnp.reshape(my_slot, (1,)).astype(jnp.int32)

    nb = len(BIG)
    gplan = _gather_ici_plan(nb)
    place_steps = 4 if all(W[n].shape[1] % 64 == 0 for n in BIG) else 2

    def placed(l):
        return _cast_place([W[n] for n in BIG], l, slot_arr, place_steps)

    conv_land = lax.dynamic_update_slice_in_dim(jnp.zeros((N_CHIPS,) + conv_w.shape, f32), conv_w[None], my_slot, axis=0)
    lands0 = placed(0)
    n_first = 3
    first = lands0[:n_first] + [conv_land]
    got = _exchange("gather0_ici", first, [_sds(a.shape, a.dtype) for a in first], _gather_ici_plan(n_first + 1),
                    aliases={a: a for a in range(n_first + 1)})
    got = _gather_d2d("gather0_d2d", got)
    G = [None] * L
    G[0] = dict(zip(BIG[:n_first], got[:n_first]))
    conv_full = jnp.transpose(got[n_first], (1, 2, 0, 3)).reshape(L, CONV_WIDTH, d_b)
    rest_plan = _gather_ici_plan(nb - n_first)
    rest_pending = _start_copies("gather_start_0", [], lands0[n_first:], rest_plan, got[0])

    def start_gather(l, after):
        return _start_copies(f"gather_start_{l}", [], placed(l), gplan, after)

    d2d_plan = _gather_d2d_plan(nb)

    lb = _lower_bounds(hgrn_lb_logits)
    avg_b = _group_avg_matrix(d_b, d_b // B_BLOCKS)
    avg_c = _group_avg_matrix(d_c, d_c // C_GROUPS)
    wa_dense = [_blockdiag(lru_wa[l]) for l in range(L)]
    wx_dense = [_blockdiag(lru_wx[l]) for l in range(L)]
    bexp = [jnp.repeat(sgu_b[l].T, d_c // C_GROUPS, axis=1) for l in range(L)]

    def lru_params(l):
        return (conv_full[l], conv_b[l][None], wa_dense[l], lru_ba[l].reshape(1, d_b), wx_dense[l],
                lru_bx[l].reshape(1, d_b), lru_lambda[l][None], lru_norm[l][None], avg_b)

    h = x.reshape(T, D)
    saved = []
    for l in range(L):
        s = {"h0": h}
        gain1, gain_mix = ffn1_norm[l][None], mix_norm[l][None]
        pending = None
        if l == 0:
            gain1 = gain1 + rest_pending[4][0:1, 0:1]
        elif l + 1 < L:
            pending = start_gather(l + 1, h)
            gain1 = gain1 + pending[4][0:1, 0:1]
        g = G[l]
        h, s["g1"], s["u1"] = _ffn_fwd(h, gain1, g["ffn1_wg"], g["ffn1_wu"], g["ffn1_wd"], tm)
        s["h1"] = h
        if l == 0:
            send, recv, _, lands, _ = rest_pending
            lands = _wait_copies("gather_wait_0", send, recv, [], lands, rest_plan, h)
            g.update(zip(BIG[n_first:], _gather_d2d("gather_d2d", lands)))
            if L > 1:
                pending = start_gather(1, g["w_in"])
                gain_mix = gain_mix + pending[4][0:1, 0:1]
        z = _proj_in_fwd(h, gain_mix, g["w_in"], tm)
        s["z"] = z
        s["oa"], s["o_pre"], s["states"] = _hgrn_fwd(z, lb[l][None], hgrn_norm[l][None], d_a)
        s["ob"], s["hl"] = _lru_fwd(z, col_b, d_b, *lru_params(l))
        s["oc"] = _sgu_fwd(z, col_c, d_c, sgu_w[l], bexp[l], sgu_norm[l][None], avg_c)
        h = _proj_out_fwd(h, s["oa"], s["ob"], s["oc"], g["w_out"], tm)
        s["h2"] = h
        gain2 = ffn2_norm[l][None]
        forward = None
        if pending is not None:
            send, recv, _, lands, _ = pending
            lands = _wait_copies(f"gather_wait_{l + 1}", send, recv, [], lands, gplan, h)
            forward = _start_copies(f"gather_d2d_start_{l + 1}", [], lands, d2d_plan, lands[0])
            gain2 = gain2 + forward[4][0:1, 0:1]
        h, s["g2"], s["u2"] = _ffn_fwd(h, gain2, g["ffn2_wg"], g["ffn2_wu"], g["ffn2_wd"], tm)
        saved.append(s)
        if forward is not None:
            send, recv, _, lands, _ = forward
            G[l + 1] = dict(zip(BIG, _wait_copies(f"gather_d2d_wait_{l + 1}", send, recv, [], lands, d2d_plan, h)))

    dh, loss_part, d_final = _loss_head(h, final_norm[None], loss_target.reshape(T, D), tm)
    loss = lax.psum(loss_part[0, 0], ("x", "y", "c"))

    def pair_views(n_big):
        r = [(lambda i, o, p, a=a: i[a].at[:, pl.ds((1 - p.c) * (i[a].shape[1] // 2), i[a].shape[1] // 2)],
              lambda i, o, p, a=a: o[a], "sib") for a in range(n_big)]
        return r

    def chip_plan_for(n):
        return [(lambda s_, o, p, a=a, kind=kind: s_[a].at[p.peer_slot(kind)], lambda s_, o, p, a=a, j=j: o[a].at[j], kind)
                for a in range(n) for j, kind in enumerate(CHIP_KINDS)]

    chip_plan = chip_plan_for(nb)
    sbufs = {n: None for n in BIG}

    def pair_phase(arrs, extra=None):
        n = len(arrs)
        ins, remote = list(arrs), pair_views(n)
        outs = [_sds((N_CHIPS, a.shape[1] // 2, a.shape[2]), bf16) for a in arrs]
        if extra is not None:
            ins.append(extra)
            outs.append(_sds(extra.shape, f32))
            remote = remote + [(lambda i, o, p: i[n], lambda i, o, p: o[n], "sib")]
        recv = _exchange("grad_pair_d2d", ins, outs, remote)
        return _pair_sum(arrs, recv[:n], c_arr), (None if extra is None else _add(extra, recv[n]))

    def chip_sum_into(names, hs, lands, l):
        prev = [sbufs[n] for n in names] if sbufs[names[0]] is not None else []
        for n, buf in zip(names, _chip_sum(hs, lands, prev, slot_arr, c_arr, l, L)):
            sbufs[n] = buf

    def share(l, extra_in=(), extra_out=(), extra_remote=(), extra_local=()):
        remote = [(lambda i, o, p, a=a: _half(o[a].at[l], p.c), lambda i, o, p, a=a: _half(o[a].at[l], p.c), "sib")
                  for a in range(nb)]
        outs = [_sds(sbufs[n].shape, f32) for n in BIG] + list(extra_out)
        res = _exchange("grad_share_d2d", [sbufs[n] for n in BIG] + list(extra_in), outs, remote + list(extra_remote),
                        list(extra_local), aliases={a: a for a in range(nb)})
        for n, buf in zip(BIG, res[:nb]):
            sbufs[n] = buf
        return res[nb:]

    small = {n: [None] * L for n in SMALL if n != "final_norm"}
    chip_pending = pair_pending = early = None
    early_names = ("w_in", "w_out", "ffn2_wg", "ffn2_wu", "ffn2_wd")
    for l in reversed(range(L)):
        s, g = saved[l], G[l]
        gain2, gain_a = ffn2_norm[l][None], hgrn_norm[l][None]
        if pair_pending is not None:
            gain2 = gain2 + pair_pending[0][4][0:1, 0:1]
        dh, small["ffn2_norm"][l], dg, du, xn, dob = _ffn_bwd_dgrad(
            s["h2"], gain2, dh, s["g2"], s["u2"], g["ffn2_wg"], g["ffn2_wu"], g["ffn2_wd"], tm_d)
        if pair_pending is not None:
            (send, recv, grads_prev, lands, _), = pair_pending
            recv_a = _wait_copies(f"grad_pair_wait_{l + 1}", send, recv, grads_prev, lands, pair_views(nb), dh)
            hsum = _pair_sum(grads_prev, recv_a, c_arr)
            lands = [lax.empty((N_CHIPS - 1,) + hh.shape[1:], bf16) for hh in hsum]
            chip_pending = (_start_copies(f"grad_chip_start_{l + 1}", hsum, lands, chip_plan, hsum[0]), hsum)
            gain_a = gain_a + chip_pending[0][4][0:1, 0:1]
            pair_pending = None
        dwg2, dwu2, dwd2 = _ffn_bwd_wgrad(xn, dob, s["g2"], s["u2"], dg, du, tm_w)
        doa, dob_, doc, dwo = _proj_out_bwd(dh, s["oa"], s["ob"], s["oc"], g["w_out"], tm)
        dz, small["hgrn_lb_logits"][l], small["hgrn_norm"][l] = _hgrn_bwd(
            s["z"], lb[l][None], gain_a, s["o_pre"], s["states"], doa, d_a)
        (dz, small["conv_w"][l], small["conv_b"][l], dwa, small["lru_ba"][l], dwx, small["lru_bx"][l],
         small["lru_lambda"][l], small["lru_norm"][l]) = _lru_bwd(s["z"], col_b, d_b, s["hl"], dob_, dz, *lru_params(l))
        small["lru_wa"][l] = _blockdiag_extract(dwa, B_BLOCKS)
        small["lru_wx"][l] = _blockdiag_extract(dwx, B_BLOCKS)
        dz, small["sgu_w"][l], dsb, small["sgu_norm"][l] = _sgu_bwd(
            s["z"], col_c, d_c, doc, dz, sgu_w[l], bexp[l], sgu_norm[l][None], avg_c)
        small["sgu_b"][l] = dsb.T
        dh, small["mix_norm"][l], xn = _proj_in_bwd_dgrad(s["h1"], mix_norm[l][None], dh, dz, g["w_in"], tm)
        dwi = _proj_in_bwd_wgrad(xn, dz, N_CHIPS)
        gain1 = ffn1_norm[l][None]
        if l == 0:
            hs_e, _ = pair_phase([dwi, dwo, dwg2, dwu2, dwd2])
            lands = [lax.empty((N_CHIPS - 1,) + hh.shape[1:], bf16) for hh in hs_e]
            early = (_start_copies("grad_chip_start_0", hs_e, lands, chip_plan_for(len(hs_e)), hs_e[0]), hs_e)
            gain1 = gain1 + early[0][4][0:1, 0:1]
        dh, small["ffn1_norm"][l], dg, du, xn, dob = _ffn_bwd_dgrad(
            s["h0"], gain1, dh, s["g1"], s["u1"], g["ffn1_wg"], g["ffn1_wu"], g["ffn1_wd"], tm_d)
        dwg1, dwu1, dwd1 = _ffn_bwd_wgrad(xn, dob, s["g1"], s["u1"], dg, du, tm_w)
        layer_grads = [dwg1, dwu1, dwd1, dwi, dwo, dwg2, dwu2, dwd2]

        if chip_pending is not None:
            (send, recv, hs, lands, _), hsum_prev = chip_pending
            lands = _wait_copies(f"grad_chip_wait_{l + 1}", send, recv, hs, lands, chip_plan, dwg1)
            chip_sum_into(BIG, hsum_prev, lands, l + 1)
            share(l + 1)
            chip_pending = None
        if l > 0:
            lands = [lax.empty((N_CHIPS, gr.shape[1] // 2, gr.shape[2]), bf16) for gr in layer_grads]
            pair_pending = (_start_copies(f"grad_pair_start_{l}", layer_grads, lands, pair_views(nb), dwg1),)
    grad_x = dh.reshape(x.shape)

    (send, recv, hs, lands, _), hs_e = early
    lands = _wait_copies("grad_chip_wait_0", send, recv, hs, lands, chip_plan_for(len(hs_e)), dwg1)
    chip_sum_into(early_names, hs_e, lands, 0)
    small_names = [n for n in SMALL]
    small_parts = [jnp.stack([jnp.reshape(v, (-1,)) for v in small[n]]) if n != "final_norm" else d_final for n in small_names]
    small_shapes = [p.shape for p in small_parts]
    packed = _pack(small_parts, 2 * 8 * 128).reshape(2, -1, 128)
    n_rows = packed.shape[1]
    late = [dwg1, dwu1, dwd1]
    nl = len(late)
    hsum, small_pair = pair_phase(late, packed)
    remote = chip_plan_for(nl) + [(lambda i, o, p: i[nl].at[p.c], lambda i, o, p: o[nl].at[p.slot], kind) for kind in CHIP_KINDS]
    local = [(lambda i, o, p: i[nl].at[p.c], lambda i, o, p: o[nl].at[p.slot])]
    outs = [_sds((N_CHIPS - 1,) + hh.shape[1:], bf16) for hh in hsum] + [_sds((N_CHIPS, n_rows, 128), f32)]
    recv_b = _exchange("grad_chip_ici_last", hsum + [small_pair], outs, remote, local)
    chip_sum_into(BIG[:nl], hsum, recv_b[:nl], 0)
    small_half = _sum_slots(recv_b[nl])
    (small_all,) = share(0, extra_in=[small_half], extra_out=[_sds(packed.shape, f32)],
                         extra_remote=[(lambda i, o, p: i[nb], lambda i, o, p: o[nb].at[p.c], "sib")],
                         extra_local=[(lambda i, o, p: i[nb], lambda i, o, p: o[nb].at[p.c])])
    grads = {n: sbufs[n].reshape(W[n].shape) for n in BIG}
    small_tot = _unpack(small_all.reshape(-1), small_shapes)
    for n, val in zip(small_names, small_tot):
        grads[n] = val
    grads["hgrn_lb_logits"] = _lower_bounds_bwd(hgrn_lb_logits, grads["hgrn_lb_logits"])
    shard_cols = conv_w.shape[2]
    grads["conv_w"] = lax.dynamic_slice_in_dim(grads["conv_w"].reshape(L, CONV_WIDTH, d_b), my_slot * shard_cols, shard_cols, axis=2)
    for n in SMALL:
        grads[n] = grads[n].reshape(W[n].shape)

    delta, new_m, new_v = {}, {}, {}
    for n in BIG:
        cols = W[n].shape[-1]
        d2, m2, v2 = _adamw(W[n].reshape(-1, cols), grads[n].reshape(-1, cols), M[n].reshape(-1, cols), V[n].reshape(-1, cols), 512)
        delta[n], new_m[n], new_v[n] = d2.reshape(W[n].shape), m2.reshape(W[n].shape), v2.reshape(W[n].shape)
    shapes = [W[n].shape for n in SMALL]
    packs = [_pack([src[n] for n in SMALL], 8 * 128).reshape(-1, 128) for src in (W, grads, M, V)]
    d2, m2, v2 = _adamw(*packs, 4096)
    for dst, val in ((delta, d2), (new_m, m2), (new_v, v2)):
        for n, piece in zip(SMALL, _unpack(val.reshape(-1), shapes)):
            dst[n] = piece

    return (loss, grad_x, *[grads[n] for n in WEIGHTS], *[delta[n] for n in WEIGHTS],
            *[new_m[n] for n in WEIGHTS], *[new_v[n] for n in WEIGHTS])
```

```python
import math

import numpy as np
import jax
import jax.numpy as jnp
from jax import lax
from jax.experimental import pallas as pl
from jax.experimental.pallas import tpu as pltpu

f32 = jnp.float32
bf16 = jnp.bfloat16
HI = lax.Precision.HIGHEST
MESH = pl.DeviceIdType.MESH

EPS = 1e-6
HEAD = 128
A_CHUNK = 64
A_SUB = 16
A_INNER = 2
SUBLANES = 8
B_BLOCKS = 4
B_CHUNK = 256
CONV_WIDTH = 4
LRU_C = 8.0
C_GROUPS = 4
C_CHUNK = 128
C_INNER = 4
N_CHIPS = 4
ADAM_LR, ADAM_B1, ADAM_B2, ADAM_EPS, ADAM_WD, ADAM_STEP = 0.001, 0.9, 0.999, 1e-08, 0.01, 10
VMEM_LIMIT = 56 * 1024 * 1024


def _cparams(n_axes):
    return pltpu.CompilerParams(dimension_semantics=("arbitrary",) * n_axes, vmem_limit_bytes=VMEM_LIMIT)


def _sds(shape, dtype):
    return jax.ShapeDtypeStruct(tuple(shape), dtype)


def _full(shape):
    n = len(shape)
    return pl.BlockSpec(tuple(shape), lambda *_: (0,) * n)


def _resident(shape):
    n = len(shape)
    return pl.BlockSpec(tuple(shape), lambda *_: (0,) * n, pipeline_mode=pl.Buffered(1))


def _dot(a, b):
    return jnp.dot(a, b, preferred_element_type=f32)


def _dot_nt(a, b):
    return lax.dot_general(a, b, (((1,), (1,)), ((), ())), preferred_element_type=f32)


def _dot_tn(a, b):
    return lax.dot_general(a, b, (((0,), (0,)), ((), ())), preferred_element_type=f32)


def _silu(x):
    return x * jax.nn.sigmoid(x)


def _group_avg_matrix(n, group):
    idx = np.arange(n) // group
    return jnp.asarray((idx[:, None] == idx[None, :]).astype(np.float32) / group)


class _Place:
    def __init__(self):
        self.x, self.y, self.c = lax.axis_index("x"), lax.axis_index("y"), lax.axis_index("c")
        self.slot = 2 * self.x + self.y

    def peer(self, kind):
        x, y, c = self.x, self.y, self.c
        return {"sib": (x, y, 1 - c), "fx": (1 - x, y, c), "fy": (x, 1 - y, c), "fxy": (1 - x, 1 - y, c)}[kind]

    def peer_slot(self, kind):
        x, y = self.x, self.y
        return {"fx": 2 * (1 - x) + y, "fy": 2 * x + (1 - y), "fxy": 2 * (1 - x) + (1 - y)}[kind]


CHIP_KINDS = ("fx", "fy", "fxy")


def _exchange(name, ins, outs, remote, local=(), aliases=None):
    n_in, n_out, n_r, n_l = len(ins), len(outs), len(remote), len(local)

    def body(*refs):
        in_refs, out_refs = refs[:n_in], refs[n_in:n_in + n_out]
        send, recv, lsem = refs[n_in + n_out:]
        p = _Place()
        lcopies = []
        for t, (src, dst) in enumerate(local):
            cp = pltpu.make_async_copy(src(in_refs, out_refs, p), dst(in_refs, out_refs, p), lsem.at[t])
            cp.start()
            lcopies.append(cp)
        copies = []
        for t, (src, dst, kind) in enumerate(remote):
            cp = pltpu.make_async_remote_copy(
                src_ref=src(in_refs, out_refs, p), dst_ref=dst(in_refs, out_refs, p),
                send_sem=send.at[t], recv_sem=recv.at[t], device_id=p.peer(kind), device_id_type=MESH)
            cp.start()
            copies.append(cp)
        for cp in copies:
            cp.wait_recv()
        for cp in copies:
            cp.wait_send()
        for cp in lcopies:
            cp.wait()

    anyspec = pl.BlockSpec(memory_space=pl.ANY)
    res = pl.pallas_call(
        body, name=name, out_shape=tuple(outs),
        in_specs=[anyspec] * n_in, out_specs=tuple([anyspec] * n_out),
        scratch_shapes=[pltpu.SemaphoreType.DMA((n_r,)), pltpu.SemaphoreType.DMA((n_r,)),
                        pltpu.SemaphoreType.DMA((max(n_l, 1),))],
        input_output_aliases=aliases or {},
        compiler_params=pltpu.CompilerParams(has_side_effects=True),
    )(*ins)
    return list(res)


HBM_SPEC = pl.BlockSpec(memory_space=pltpu.HBM)
SEM_SPEC = pl.BlockSpec(memory_space=pltpu.SEMAPHORE)
ANY_SPEC = pl.BlockSpec(memory_space=pl.ANY)
DATAFLOW = pltpu.SideEffectType.DATAFLOW_SIDE_EFFECTING


def _in_hbm(a):
    return pltpu.with_memory_space_constraint(a, pltpu.HBM)


def _start_copies(name, srcs, lands, remote, after=None):
    n_s, n_l, n_r = len(srcs), len(lands), len(remote)
    extra = [] if after is None else [after]

    def body(*refs):
        src_refs, land_refs = refs[:n_s], refs[n_s:n_s + n_l]
        n_in = n_s + n_l + len(extra)
        send, recv = refs[n_in], refs[n_in + 1]
        token = refs[-1]
        p = _Place()
        for t, (src, dst, kind) in enumerate(remote):
            pltpu.make_async_remote_copy(
                src_ref=src(src_refs, land_refs, p), dst_ref=dst(src_refs, land_refs, p),
                send_sem=send.at[t], recv_sem=recv.at[t], device_id=p.peer(kind), device_id_type=MESH).start()
        token[...] = jnp.zeros_like(token)

    thru = [pltpu.HBM(a.shape, a.dtype) for a in lands]
    res = pl.pallas_call(
        body, name=name,
        out_shape=(pltpu.SemaphoreType.DMA((n_r,)), pltpu.SemaphoreType.DMA((n_r,)), *thru, _sds((8, 128), f32)),
        in_specs=[ANY_SPEC] * n_s + [HBM_SPEC] * n_l + [ANY_SPEC] * len(extra),
        out_specs=(SEM_SPEC, SEM_SPEC, *([HBM_SPEC] * n_l), pl.BlockSpec(memory_space=pltpu.VMEM)),
        input_output_aliases={n_s + i: 2 + i for i in range(n_l)},
        compiler_params=pltpu.CompilerParams(has_side_effects=DATAFLOW),
    )(*srcs, *[_in_hbm(a) for a in lands], *extra)
    return res[0], res[1], list(srcs), list(res[2:2 + n_l]), res[-1]


def _wait_copies(name, send, recv, srcs, lands, remote, after):
    n_s, n_l = len(srcs), len(lands)

    def body(*refs):
        src_refs, land_refs = refs[:n_s], refs[n_s:n_s + n_l]
        send_ref, recv_ref = refs[n_s + n_l], refs[n_s + n_l + 1]
        p = _Place()
        for t, (src, dst, kind) in enumerate(remote):
            cp = pltpu.make_async_remote_copy(
                src_ref=src(src_refs, land_refs, p), dst_ref=dst(src_refs, land_refs, p),
                send_sem=send_ref.at[t], recv_sem=recv_ref.at[t], device_id=p.peer(kind), device_id_type=MESH)
            cp.wait_send()
            cp.wait_recv()

    res = pl.pallas_call(
        body, name=name, out_shape=tuple(pltpu.HBM(a.shape, a.dtype) for a in lands),
        in_specs=[ANY_SPEC] * n_s + [HBM_SPEC] * n_l + [SEM_SPEC, SEM_SPEC, ANY_SPEC],
        out_specs=tuple([HBM_SPEC] * n_l),
        input_output_aliases={n_s + i: i for i in range(n_l)},
        compiler_params=pltpu.CompilerParams(has_side_effects=DATAFLOW),
    )(*srcs, *lands, send, recv, after)
    return list(res)


def _half(ref, c):
    n2 = ref.shape[0] // 2
    return ref.at[pl.ds(c * n2, n2)]


def _gather_ici_plan(n):
    def view(a):
        return lambda s, o, p: _half(o[a].at[p.slot], p.c)

    return [(view(a), view(a), kind) for a in range(n) for kind in CHIP_KINDS]


def _gather_d2d_plan(n):
    remote = []
    for a in range(n):
        for kind in CHIP_KINDS:
            view = lambda i, o, p, a=a, kind=kind: _half(o[a].at[p.peer_slot(kind)], p.c)
            remote.append((view, view, "sib"))
    return remote


def _gather_d2d(name, lands):
    n = len(lands)
    outs = [_sds(g.shape, g.dtype) for g in lands]
    return _exchange(name, list(lands), outs, _gather_d2d_plan(n), aliases={a: a for a in range(n)})


def _cast_place(weights, layer, slot_arr, n_steps=4):
    def body(s_ref, *refs):
        n = len(refs) // 2
        for a in range(n):
            refs[n + a][...] = refs[a][...].astype(bf16)

    in_specs, out_specs, out_shape = [], [], []
    for w in weights:
        _, R, Cc = w.shape
        rt = R // n_steps
        in_specs.append(pl.BlockSpec((None, rt, Cc), lambda i, s: (layer, i, 0)))
        out_specs.append(pl.BlockSpec((None, rt, Cc), lambda i, s: (s[0], i, 0)))
        out_shape.append(_sds((N_CHIPS, R, Cc), bf16))
    gs = pltpu.PrefetchScalarGridSpec(num_scalar_prefetch=1, grid=(n_steps,), in_specs=in_specs, out_specs=tuple(out_specs))
    return list(pl.pallas_call(body, name="cast_place", grid_spec=gs, out_shape=tuple(out_shape),
                               compiler_params=_cparams(1))(slot_arr, *weights))


def _ffn_fwd(h, gain, wg, wu, wd, tm):
    T, D = h.shape
    nsh, F = wg.shape[0], wg.shape[1]
    nt = T // tm

    def body(h_ref, gain_ref, wg_ref, wu_ref, wd_ref, out_ref, gs_ref, us_ref):
        hv = h_ref[...]
        r = lax.rsqrt(jnp.mean(hv * hv, axis=-1, keepdims=True) + EPS)
        xn = (hv * r * gain_ref[...]).astype(bf16)
        acc = None
        for k in range(nsh):
            g = _dot_nt(xn, wg_ref[k])
            u = _dot_nt(xn, wu_ref[k])
            gs_ref[k] = g.astype(bf16)
            us_ref[k] = u.astype(bf16)
            part = _dot((_silu(g) * u).astype(bf16), wd_ref[k])
            acc = part if acc is None else acc + part
        out_ref[...] = hv + 0.5 * acc

    sav = pl.BlockSpec((nsh, tm, F), lambda i: (0, i, 0))
    return pl.pallas_call(
        body, name="ffn_fwd", grid=(nt,),
        in_specs=[pl.BlockSpec((tm, D), lambda i: (i, 0)), _full((1, D)), _resident((nsh, F, D)), _resident((nsh, F, D)),
                  _resident((nsh, F, D))],
        out_specs=(pl.BlockSpec((tm, D), lambda i: (i, 0)), sav, sav),
        out_shape=(_sds((T, D), f32), _sds((nsh, T, F), bf16), _sds((nsh, T, F), bf16)),
        compiler_params=_cparams(1),
    )(h, gain, wg, wu, wd)


def _ffn_bwd_dgrad(h, gain, dout, gs, us, wg, wu, wd, tm):
    T, D = h.shape
    nsh, F = wg.shape[0], wg.shape[1]
    nt = T // tm

    def body(h_ref, gain_ref, dout_ref, gs_ref, us_ref, wg_ref, wu_ref, wd_ref,
             dh_ref, dgain_ref, dg_ref, du_ref, xn_ref, dob_ref):
        @pl.when(pl.program_id(0) == 0)
        def _():
            dgain_ref[...] = jnp.zeros_like(dgain_ref)

        hv = h_ref[...]
        r = lax.rsqrt(jnp.mean(hv * hv, axis=-1, keepdims=True) + EPS)
        xh = hv * r
        xn_ref[...] = (xh * gain_ref[...]).astype(bf16)
        dv = dout_ref[...]
        dob = (0.5 * dv).astype(bf16)
        dob_ref[...] = dob
        dxn = None
        for k in range(nsh):
            da = _dot_nt(dob, wd_ref[k])
            g = gs_ref[k].astype(f32)
            u = us_ref[k].astype(f32)
            sg = jax.nn.sigmoid(g)
            dg = (da * u * (sg * (1.0 + g * (1.0 - sg)))).astype(bf16)
            du = (da * (g * sg)).astype(bf16)
            dg_ref[k] = dg
            du_ref[k] = du
            part = _dot(dg, wg_ref[k]) + _dot(du, wu_ref[k])
            dxn = part if dxn is None else dxn + part
        dgain_ref[...] += jnp.sum(dxn * xh, axis=0, keepdims=True)
        dxh = dxn * gain_ref[...]
        dh_ref[...] = dv + r * (dxh - xh * jnp.mean(dxh * xh, axis=-1, keepdims=True))

    tok = pl.BlockSpec((tm, D), lambda i: (i, 0))
    sav = pl.BlockSpec((nsh, tm, F), lambda i: (0, i, 0))
    return pl.pallas_call(
        body, name="ffn_bwd_dgrad", grid=(nt,),
        in_specs=[tok, _full((1, D)), tok, sav, sav, _resident((nsh, F, D)), _resident((nsh, F, D)), _resident((nsh, F, D))],
        out_specs=(tok, _full((1, D)), sav, sav, tok, tok),
        out_shape=(_sds((T, D), f32), _sds((1, D), f32), _sds((nsh, T, F), bf16), _sds((nsh, T, F), bf16),
                   _sds((T, D), bf16), _sds((T, D), bf16)),
        compiler_params=_cparams(1),
    )(h, gain, dout, gs, us, wg, wu, wd)


def _ffn_bwd_wgrad(xn, dob, gs, us, dg, du, tm):
    T, D = xn.shape
    nsh, F = gs.shape[0], gs.shape[2]
    nt = T // tm

    def body(xn_ref, dob_ref, gs_ref, us_ref, dg_ref, du_ref, dwg_ref, dwu_ref, dwd_ref, ag_ref, au_ref, ad_ref):
        i = pl.program_id(1)

        @pl.when(i == 0)
        def _():
            ag_ref[...] = jnp.zeros_like(ag_ref)
            au_ref[...] = jnp.zeros_like(au_ref)
            ad_ref[...] = jnp.zeros_like(ad_ref)

        xn_v = xn_ref[...]
        ag_ref[...] += _dot_tn(dg_ref[...], xn_v)
        au_ref[...] += _dot_tn(du_ref[...], xn_v)
        g = gs_ref[...].astype(f32)
        a = (_silu(g) * us_ref[...].astype(f32)).astype(bf16)
        ad_ref[...] += _dot_tn(a, dob_ref[...])

        @pl.when(i == nt - 1)
        def _():
            dwg_ref[...] = ag_ref[...].astype(bf16)
            dwu_ref[...] = au_ref[...].astype(bf16)
            dwd_ref[...] = ad_ref[...].astype(bf16)

    tok = pl.BlockSpec((tm, D), lambda k, i: (i, 0))
    sav = pl.BlockSpec((None, tm, F), lambda k, i: (k, i, 0))
    wdspec = pl.BlockSpec((None, F, D), lambda k, i: (k, 0, 0))
    return pl.pallas_call(
        body, name="ffn_bwd_wgrad", grid=(nsh, nt),
        in_specs=[tok, tok, sav, sav, sav, sav],
        out_specs=(wdspec, wdspec, wdspec),
        out_shape=(_sds((nsh, F, D), bf16),) * 3,
        scratch_shapes=[pltpu.VMEM((F, D), f32)] * 3,
        compiler_params=_cparams(2),
    )(xn, dob, gs, us, dg, du)


def _proj_in_fwd(h, gain, w_in, tm):
    T, D = h.shape
    nsh, N = w_in.shape[0], w_in.shape[2]
    nt = T // tm

    def body(h_ref, gain_ref, w_ref, z_ref):
        hv = h_ref[...]
        r = lax.rsqrt(jnp.mean(hv * hv, axis=-1, keepdims=True) + EPS)
        xn = (hv * r * gain_ref[...]).astype(bf16)
        for k in range(nsh):
            z_ref[:, k * N:(k + 1) * N] = _dot(xn, w_ref[k])

    return pl.pallas_call(
        body, name="proj_in_fwd", grid=(nt,),
        in_specs=[pl.BlockSpec((tm, D), lambda i: (i, 0)), _full((1, D)), _full((nsh, D, N))],
        out_specs=pl.BlockSpec((tm, nsh * N), lambda i: (i, 0)),
        out_shape=_sds((T, nsh * N), f32),
        compiler_params=_cparams(1),
    )(h, gain, w_in)


def _proj_in_bwd_dgrad(h, gain, dres, dz, w_in, tm):
    T, D = h.shape
    nsh, N = w_in.shape[0], w_in.shape[2]
    nt = T // tm

    def body(h_ref, gain_ref, dres_ref, dz_ref, w_ref, dh_ref, dgain_ref, xn_ref):
        @pl.when(pl.program_id(0) == 0)
        def _():
            dgain_ref[...] = jnp.zeros_like(dgain_ref)

        dxn = _dot_nt(dz_ref[:, 0:N], w_ref[0])
        for k in range(1, nsh):
            dxn = dxn + _dot_nt(dz_ref[:, k * N:(k + 1) * N], w_ref[k])
        hv = h_ref[...]
        r = lax.rsqrt(jnp.mean(hv * hv, axis=-1, keepdims=True) + EPS)
        xh = hv * r
        xn_ref[...] = (xh * gain_ref[...]).astype(bf16)
        dgain_ref[...] += jnp.sum(dxn * xh, axis=0, keepdims=True)
        dxh = dxn * gain_ref[...]
        dh_ref[...] = dres_ref[...] + r * (dxh - xh * jnp.mean(dxh * xh, axis=-1, keepdims=True))

    tok = pl.BlockSpec((tm, D), lambda i: (i, 0))
    return pl.pallas_call(
        body, name="proj_in_bwd_dgrad", grid=(nt,),
        in_specs=[tok, _full((1, D)), tok, pl.BlockSpec((tm, nsh * N), lambda i: (i, 0)), _full((nsh, D, N))],
        out_specs=(tok, _full((1, D)), tok),
        out_shape=(_sds((T, D), f32), _sds((1, D), f32), _sds((T, D), bf16)),
        compiler_params=_cparams(1),
    )(h, gain, dres, dz, w_in)


def _proj_in_bwd_wgrad(xn, dz, nsh):
    T, D = xn.shape
    N = dz.shape[1] // nsh

    def body(xn_ref, dz_ref, dw_ref):
        dw_ref[...] = _dot_tn(xn_ref[...], dz_ref[...]).astype(bf16)

    return pl.pallas_call(
        body, name="proj_in_bwd_wgrad", grid=(nsh,),
        in_specs=[_full((T, D)), pl.BlockSpec((T, N), lambda k: (0, k))],
        out_specs=pl.BlockSpec((None, D, N), lambda k: (k, 0, 0)),
        out_shape=_sds((nsh, D, N), bf16),
        compiler_params=_cparams(1),
    )(xn, dz)


def _proj_out_fwd(h, oa, ob, oc, w_out, tm):
    T, D = h.shape
    nsh, R = w_out.shape[0], w_out.shape[1]
    da, db = oa.shape[1], ob.shape[1]
    nt = T // tm

    def body(h_ref, oa_ref, ob_ref, oc_ref, w_ref, out_ref):
        w = w_ref[...].reshape(nsh * R, D)
        out_ref[...] = (h_ref[...] + _dot(oa_ref[...], w[:da]) + _dot(ob_ref[...], w[da:da + db])
                        + _dot(oc_ref[...], w[da + db:]))

    def tok(n):
        return pl.BlockSpec((tm, n), lambda i: (i, 0))

    return pl.pallas_call(
        body, name="proj_out_fwd", grid=(nt,),
        in_specs=[tok(D), tok(da), tok(db), tok(oc.shape[1]), _full((nsh, R, D))],
        out_specs=tok(D), out_shape=_sds((T, D), f32),
        compiler_params=_cparams(1),
    )(h, oa, ob, oc, w_out)


def _proj_out_bwd(dh, oa, ob, oc, w_out, tm):
    T, D = dh.shape
    nsh, R = w_out.shape[0], w_out.shape[1]
    da, db, dc = oa.shape[1], ob.shape[1], oc.shape[1]
    nt = T // tm

    def body(dh_ref, oa_ref, ob_ref, oc_ref, w_ref, doa_ref, dob_ref, doc_ref, dw_ref, acc_ref):
        i = pl.program_id(0)

        @pl.when(i == 0)
        def _():
            acc_ref[...] = jnp.zeros_like(acc_ref)

        d = dh_ref[...].astype(bf16)
        w = w_ref[...].reshape(nsh * R, D)
        dm = _dot_nt(d, w)
        doa_ref[...] = dm[:, :da]
        dob_ref[...] = dm[:, da:da + db]
        doc_ref[...] = dm[:, da + db:]
        acc_ref[pl.ds(0, da), :] += _dot_tn(oa_ref[...], d)
        acc_ref[pl.ds(da, db), :] += _dot_tn(ob_ref[...], d)
        acc_ref[pl.ds(da + db, dc), :] += _dot_tn(oc_ref[...], d)

        @pl.when(i == nt - 1)
        def _():
            dw_ref[...] = acc_ref[...].astype(bf16).reshape(nsh, R, D)

    def tok(n):
        return pl.BlockSpec((tm, n), lambda i: (i, 0))

    wspec = _full((nsh, R, D))
    return pl.pallas_call(
        body, name="proj_out_bwd", grid=(nt,),
        in_specs=[tok(D), tok(da), tok(db), tok(dc), wspec],
        out_specs=(tok(da), tok(db), tok(dc), wspec),
        out_shape=(_sds((T, da), f32), _sds((T, db), f32), _sds((T, dc), f32), _sds((nsh, R, D), bf16)),
        scratch_shapes=[pltpu.VMEM((nsh * R, D), f32)],
        compiler_params=_cparams(1),
    )(dh, oa, ob, oc, w_out)


def _head_sum(m, n_heads):
    parts = []
    for hd in range(n_heads):
        s = jnp.sum(m[:, hd * HEAD:(hd + 1) * HEAD], axis=-1, keepdims=True)
        parts.append(jnp.broadcast_to(s, (m.shape[0], HEAD)))
    return parts[0] if n_heads == 1 else jnp.concatenate(parts, axis=1)


def _cat(parts, axis):
    return parts[0] if len(parts) == 1 else jnp.concatenate(parts, axis=axis)


def _hgrn_block(q, fl, iv, lb, states, tri, n_heads, n_inner):
    C = q.shape[0] // n_inner
    qs = _silu(q)
    forget = lb + (1.0 - lb) * jax.nn.sigmoid(fl)
    kk = 1.0 - forget
    logf = jnp.log(forget)
    b = jnp.dot(tri, logf, precision=HI, preferred_element_type=f32)
    vb = iv.astype(bf16)
    heads = [slice(hd * HEAD, (hd + 1) * HEAD) for hd in range(n_heads)]
    n_sub = C // A_SUB

    off, qe, kd, dec = {}, [], [], []
    for j in range(n_inner):
        c0 = j * C
        for blk in range(1, n_sub):
            lo = c0 + blk * A_SUB
            piv = b[lo:lo + 1]
            qt = (qs[lo:lo + A_SUB] * jnp.exp(b[lo:lo + A_SUB] - piv)).astype(bf16)
            kt = (kk[c0:lo] * jnp.exp(piv - b[c0:lo])).astype(bf16)
            parts = []
            for sl in heads:
                sc = _dot_nt(qt[:, sl], kt[:, sl])
                parts.append(_dot(sc.astype(bf16), vb[c0:lo, sl]))
            off[(j, blk)] = _cat(parts, 1)
        bj = b[c0:c0 + C]
        b_end = bj[C - 1:C]
        qe.append((qs[c0:c0 + C] * jnp.exp(bj)).astype(bf16))
        kd.append((kk[c0:c0 + C] * jnp.exp(b_end - bj)).astype(bf16))
        dec.append(jnp.exp(b_end))

    outs = []
    for j in range(n_inner):
        for blk in range(n_sub):
            lo = j * C + blk * A_SUB
            groups = [off[(j, blk)][r0:r0 + SUBLANES] if blk > 0 else None for r0 in range(0, A_SUB, SUBLANES)]
            for s in range(A_SUB):
                first = (s // SUBLANES) * SUBLANES
                n_rows = A_SUB - first
                row = lax.broadcasted_iota(jnp.int32, (n_rows, 1), 0) + first
                gate = jnp.where(row >= s, 0.0, -1e30)
                r = slice(lo + first, lo + A_SUB)
                m = qs[r] * jnp.exp((b[r] - b[lo + s:lo + s + 1]) + gate) * kk[lo + s:lo + s + 1]
                term = _head_sum(m, n_heads) * iv[lo + s:lo + s + 1]
                for gi in range(first // SUBLANES, A_SUB // SUBLANES):
                    piece = term[gi * SUBLANES - first:(gi + 1) * SUBLANES - first]
                    groups[gi] = piece if groups[gi] is None else groups[gi] + piece
            outs.extend(groups)
    o = jnp.concatenate(outs, axis=0)

    inter = []
    states = list(states)
    for j in range(n_inner):
        c0 = j * C
        parts = []
        for hd, sl in enumerate(heads):
            st = states[hd]
            parts.append(_dot_nt(qe[j][:, sl], st.astype(bf16)))
            states[hd] = dec[j][:, sl] * st + _dot_tn(vb[c0:c0 + C, sl], kd[j][:, sl])
        inter.append(_cat(parts, 1))
    return o + _cat(inter, 0), tuple(states)


def _hgrn_gate(o, g, gain, n_heads):
    ms = _head_sum(o * o, n_heads) * (1.0 / HEAD)
    return o * lax.rsqrt(ms + EPS) * gain * _silu(g)


def _tri_matrix(c, n_inner):
    idx = np.arange(c * n_inner)
    same = (idx[:, None] // c) == (idx[None, :] // c)
    return jnp.asarray((same & (idx[:, None] >= idx[None, :])).astype(np.float32))


def _hgrn_fwd(z, lb, gain, d_a):
    T = z.shape[0]
    C = A_CHUNK * A_INNER
    nc = T // C
    nh = d_a // HEAD
    tri = _tri_matrix(A_CHUNK, A_INNER)

    def body(q_ref, f_ref, i_ref, g_ref, lb_ref, gain_ref, tri_ref, out_ref, o_ref, st_ref, carry_ref):
        @pl.when(pl.program_id(0) == 0)
        def _():
            carry_ref[...] = jnp.zeros_like(carry_ref)

        states = tuple(carry_ref[hd] for hd in range(nh))
        st_ref[...] = carry_ref[...]
        o, new_states = _hgrn_block(q_ref[...], f_ref[...], i_ref[...], lb_ref[...], states, tri_ref[...], nh, A_INNER)
        o_ref[...] = o
        out_ref[...] = _hgrn_gate(o, g_ref[...], gain_ref[...], nh).astype(bf16)
        for hd in range(nh):
            carry_ref[hd] = new_states[hd]

    def col(j):
        return pl.BlockSpec((C, d_a), lambda c, j=j: (c, j))

    tok = pl.BlockSpec((C, d_a), lambda c: (c, 0))
    return pl.pallas_call(
        body, name="hgrn_fwd", grid=(nc,),
        in_specs=[col(0), col(1), col(2), col(3), _full((1, d_a)), _full((1, d_a)), _full((C, C))],
        out_specs=(tok, tok, pl.BlockSpec((None, nh, HEAD, HEAD), lambda c: (c, 0, 0, 0))),
        out_shape=(_sds((T, d_a), bf16), _sds((T, d_a), f32), _sds((nc, nh, HEAD, HEAD), f32)),
        scratch_shapes=[pltpu.VMEM((nh, HEAD, HEAD), f32)],
        compiler_params=_cparams(1),
    )(z, z, z, z, lb, gain, tri)


def _hgrn_bwd(z, lb, gain, o_pre, states, dout, d_a):
    T = z.shape[0]
    C = A_CHUNK * A_INNER
    nc = T // C
    nh = d_a // HEAD
    tri = _tri_matrix(A_CHUNK, A_INNER)

    def body(q_ref, f_ref, i_ref, g_ref, lb_ref, gain_ref, tri_ref, o_ref, st_ref, do_ref,
             dz_ref, dlb_ref, dgain_ref, carry_ref):
        @pl.when(pl.program_id(0) == 0)
        def _():
            carry_ref[...] = jnp.zeros_like(carry_ref)
            dlb_ref[...] = jnp.zeros_like(dlb_ref)
            dgain_ref[...] = jnp.zeros_like(dgain_ref)

        _, vjp_gate = jax.vjp(lambda o, g, gv: _hgrn_gate(o, g, gv, nh), o_ref[...], g_ref[...], gain_ref[...])
        d_o, dg, dgain = vjp_gate(do_ref[...])
        tri_v = tri_ref[...]

        def fn(q, fl, iv, lbv, sts):
            return _hgrn_block(q, fl, iv, lbv, sts, tri_v, nh, A_INNER)

        states_in = tuple(st_ref[hd] for hd in range(nh))
        _, vjp = jax.vjp(fn, q_ref[...], f_ref[...], i_ref[...], lb_ref[...], states_in)
        dstates = tuple(carry_ref[hd] for hd in range(nh))
        dq, df, di, dlb, dst = vjp((d_o, dstates))
        dz_ref[:, 0:d_a] = dq.astype(bf16)
        dz_ref[:, d_a:2 * d_a] = df.astype(bf16)
        dz_ref[:, 2 * d_a:3 * d_a] = di.astype(bf16)
        dz_ref[:, 3 * d_a:4 * d_a] = dg.astype(bf16)
        dlb_ref[...] += dlb
        dgain_ref[...] += dgain
        for hd in range(nh):
            carry_ref[hd] = dst[hd]

    def col(j):
        return pl.BlockSpec((C, d_a), lambda c, j=j: (nc - 1 - c, j))

    tok = pl.BlockSpec((C, d_a), lambda c: (nc - 1 - c, 0))
    return pl.pallas_call(
        body, name="hgrn_bwd", grid=(nc,),
        in_specs=[col(0), col(1), col(2), col(3), _full((1, d_a)), _full((1, d_a)), _full((C, C)), tok,
                  pl.BlockSpec((None, nh, HEAD, HEAD), lambda c: (nc - 1 - c, 0, 0, 0)), tok],
        out_specs=(pl.BlockSpec((C, 4 * d_a), lambda c: (nc - 1 - c, 0)), _full((1, d_a)), _full((1, d_a))),
        out_shape=(_sds(z.shape, bf16), _sds((1, d_a), f32), _sds((1, d_a), f32)),
        scratch_shapes=[pltpu.VMEM((nh, HEAD, HEAD), f32)],
        compiler_params=_cparams(1),
    )(z, z, z, z, lb, gain, tri, o_pre, states, dout)


def _one_minus_exp(x):
    series = -x * (1.0 + x * (0.5 + x * (1.0 / 6.0 + x * (1.0 / 24.0))))
    return jnp.where(x > -0.03, series, 1.0 - jnp.exp(x))


def _lru_pre(xc, wa, ba, wx, bx, lam):
    xb16 = xc.astype(bf16)
    r = jax.nn.sigmoid(_dot(xb16, wa.astype(bf16)) + ba)
    gi = jax.nn.sigmoid(_dot(xb16, wx.astype(bf16)) + bx)
    log_a = -LRU_C * r * jax.nn.softplus(-lam)
    a = jnp.exp(log_a)
    mult = jnp.sqrt(_one_minus_exp(2.0 * log_a))
    return a, mult * gi * xc


def _lru_post(h, gate, gain, avg):
    y = h * jax.nn.gelu(gate)
    ms = _group_mean(y * y, avg)
    return y * lax.rsqrt(ms + EPS) * gain


def _shift_down(x, d, prev):
    row = lax.broadcasted_iota(jnp.int32, x.shape, 0)
    return jnp.where(row >= d, pltpu.roll(x, d, 0), pltpu.roll(prev, d, 0))


def _shift_up(x, d, nxt):
    n = x.shape[0]
    row = lax.broadcasted_iota(jnp.int32, x.shape, 0)
    return jnp.where(row < n - d, pltpu.roll(x, n - d, 0), pltpu.roll(nxt, n - d, 0))


def _scan_rows(a, u, reverse):
    n = a.shape[0]
    row = lax.broadcasted_iota(jnp.int32, a.shape, 0)
    d = 1
    while d < n:
        shift, ok = (n - d, row < n - d) if reverse else (d, row >= d)
        su = jnp.where(ok, pltpu.roll(u, shift, 0), 0.0)
        sa = jnp.where(ok, pltpu.roll(a, shift, 0), 1.0)
        u = u + a * su
        a = a * sa
        d *= 2
    return a, u


def _conv(xb, xprev, cw, cb):
    xc = cb + cw[CONV_WIDTH - 1:CONV_WIDTH] * xb
    for d in range(1, CONV_WIDTH):
        xc = xc + cw[CONV_WIDTH - 1 - d:CONV_WIDTH - d] * _shift_down(xb, d, xprev)
    return xc


def _lru_fwd(z, col0, d_b, cw, cb, wa, ba, wx, bx, lam, gain, avg):
    T = z.shape[0]
    R = min(B_CHUNK, T)
    nr = T // R
    jb = col0 // d_b

    def body(xb_ref, gate_ref, cw_ref, cb_ref, wa_ref, ba_ref, wx_ref, bx_ref, lam_ref, gain_ref, avg_ref,
             out_ref, h_ref, xprev_ref, hprev_ref):
        @pl.when(pl.program_id(0) == 0)
        def _():
            xprev_ref[...] = jnp.zeros_like(xprev_ref)
            hprev_ref[...] = jnp.zeros_like(hprev_ref)

        xb = xb_ref[...]
        xc = _conv(xb, xprev_ref[...], cw_ref[...], cb_ref[...])
        a, u = _lru_pre(xc, wa_ref[...], ba_ref[...], wx_ref[...], bx_ref[...], lam_ref[...])
        acum, hl = _scan_rows(a, u, False)
        h = hl + acum * hprev_ref[R - 1:R, :]
        h_ref[...] = h
        out_ref[...] = _lru_post(h, gate_ref[...], gain_ref[...], avg_ref[...]).astype(bf16)
        xprev_ref[...] = xb
        hprev_ref[...] = h

    vec = _full((1, d_b))
    return pl.pallas_call(
        body, name="lru_fwd", grid=(nr,),
        in_specs=[pl.BlockSpec((R, d_b), lambda i: (i, jb)), pl.BlockSpec((R, d_b), lambda i: (i, jb + 1)),
                  _full((CONV_WIDTH, d_b)), vec, _full((d_b, d_b)), vec, _full((d_b, d_b)), vec, vec, vec, _full((d_b, d_b))],
        out_specs=(pl.BlockSpec((R, d_b), lambda i: (i, 0)), pl.BlockSpec((R, d_b), lambda i: (i, 0))),
        out_shape=(_sds((T, d_b), bf16), _sds((T, d_b), f32)),
        scratch_shapes=[pltpu.VMEM((R, d_b), f32), pltpu.VMEM((R, d_b), f32)],
        compiler_params=_cparams(1),
    )(z, z, cw, cb, wa, ba, wx, bx, lam, gain, avg)


def _lru_bwd(z, col0, d_b, hsave, dout, dz_buf, cw, cb, wa, ba, wx, bx, lam, gain, avg):
    T = z.shape[0]
    R = min(B_CHUNK, T)
    nr = T // R
    jb = col0 // d_b

    def body(xb_ref, xp_ref, gate_ref, h_ref, hp_ref, do_ref,
             cw_ref, cb_ref, wa_ref, ba_ref, wx_ref, bx_ref, lam_ref, gain_ref, avg_ref, dzin_ref,
             dz_ref, dcw_ref, dcb_ref, dwa_ref, dba_ref, dwx_ref, dbx_ref, dlam_ref, dgain_ref,
             gfirst_ref, afirst_ref, dxcn_ref):
        step = pl.program_id(0)
        first_in_time = step == nr - 1

        @pl.when(step == 0)
        def _():
            for r in (dcw_ref, dcb_ref, dwa_ref, dba_ref, dwx_ref, dbx_ref, dlam_ref, dgain_ref,
                      gfirst_ref, afirst_ref, dxcn_ref):
                r[...] = jnp.zeros_like(r)

        xb = xb_ref[...]
        keep = jnp.where(first_in_time, 0.0, 1.0)
        xprev = xp_ref[...] * keep
        hprev = hp_ref[...] * keep
        cw = cw_ref[...]
        xc = _conv(xb, xprev, cw, cb_ref[...])
        (a, _), vjp_pre = jax.vjp(_lru_pre, xc, wa_ref[...], ba_ref[...], wx_ref[...], bx_ref[...], lam_ref[...])
        h = h_ref[...]
        avg = avg_ref[...]
        _, vjp_post = jax.vjp(lambda hh, gg, gn: _lru_post(hh, gg, gn, avg), h, gate_ref[...], gain_ref[...])
        dh, dgate, dgain = vjp_post(do_ref[...])
        a_next = _shift_up(a, 1, jnp.broadcast_to(afirst_ref[0:1, :], a.shape))
        acum, gl = _scan_rows(a_next, dh, True)
        gtot = gl + acum * gfirst_ref[0:1, :]
        da = gtot * _shift_down(h, 1, hprev)
        dxc, dwa, dba, dwx, dbx, dlam = vjp_pre((da, gtot))
        dxcn = dxcn_ref[...]
        dxb = cw[CONV_WIDTH - 1:CONV_WIDTH] * dxc
        dcw_ref[CONV_WIDTH - 1:CONV_WIDTH, :] += jnp.sum(dxc * xb, axis=0, keepdims=True)
        for d in range(1, CONV_WIDTH):
            tap = CONV_WIDTH - 1 - d
            dxb = dxb + cw[tap:tap + 1] * _shift_up(dxc, d, dxcn)
            dcw_ref[tap:tap + 1, :] += jnp.sum(dxc * _shift_down(xb, d, xprev), axis=0, keepdims=True)
        dz_ref[:, 0:d_b] = dxb.astype(bf16)
        dz_ref[:, d_b:2 * d_b] = dgate.astype(bf16)
        dcb_ref[...] += jnp.sum(dxc, axis=0, keepdims=True)
        dwa_ref[...] += dwa
        dba_ref[...] += dba
        dwx_ref[...] += dwx
        dbx_ref[...] += dbx
        dlam_ref[...] += dlam
        dgain_ref[...] += dgain
        gfirst_ref[...] = jnp.broadcast_to(gtot[0:1, :], gfirst_ref.shape)
        afirst_ref[...] = jnp.broadcast_to(a[0:1, :], afirst_ref.shape)
        dxcn_ref[...] = dxc

    vec = _full((1, d_b))
    mat = _full((d_b, d_b))

    def cur(j):
        return pl.BlockSpec((R, d_b), lambda i, j=j: (nr - 1 - i, j))

    def prev(j):
        return pl.BlockSpec((R, d_b), lambda i, j=j: (jnp.maximum(nr - 2 - i, 0), j))

    return pl.pallas_call(
        body, name="lru_bwd", grid=(nr,),
        in_specs=[cur(jb), prev(jb), cur(jb + 1), cur(0), prev(0), cur(0),
                  _full((CONV_WIDTH, d_b)), vec, mat, vec, mat, vec, vec, vec, mat, ANY_SPEC],
        out_specs=(pl.BlockSpec((R, 2 * d_b), lambda i: (nr - 1 - i, col0 // (2 * d_b))), _full((CONV_WIDTH, d_b)), vec, mat, vec, mat, vec, vec, vec),
        out_shape=(_sds(dz_buf.shape, bf16), _sds((CONV_WIDTH, d_b), f32), _sds((1, d_b), f32), _sds((d_b, d_b), f32),
                   _sds((1, d_b), f32), _sds((d_b, d_b), f32), _sds((1, d_b), f32), _sds((1, d_b), f32), _sds((1, d_b), f32)),
        scratch_shapes=[pltpu.VMEM((8, d_b), f32), pltpu.VMEM((8, d_b), f32), pltpu.VMEM((R, d_b), f32)],
        input_output_aliases={15: 0},
        compiler_params=_cparams(1),
    )(z, z, z, hsave, hsave, dout, cw, cb, wa, ba, wx, bx, lam, gain, avg, dz_buf)


def _two_pass(x, m16):
    hi = x.astype(bf16)
    lo = (x - hi.astype(f32)).astype(bf16)
    return _dot(hi, m16) + _dot(lo, m16)


@jax.custom_vjp
def _group_mean(x, avg):
    return _two_pass(x, avg.astype(bf16))


def _group_mean_fwd(x, avg):
    return _group_mean(x, avg), avg


def _group_mean_bwd(avg, ct):
    return _two_pass(ct, avg.astype(bf16)), jnp.zeros_like(avg)


_group_mean.defvjp(_group_mean_fwd, _group_mean_bwd)


def _sgu_chunk(u_in, v_in, w, bexp, gain, avg, n_groups):
    C, d_c = u_in.shape
    gd = d_c // n_groups
    u = jax.nn.gelu(u_in)
    v = jax.nn.gelu(v_in)
    mu = _group_mean(v, avg)
    vc = v - mu
    var = _group_mean(vc * vc, avg)
    vh = (vc * lax.rsqrt(var + EPS)).astype(bf16)
    lane = lax.broadcasted_iota(jnp.int32, (1, d_c), 1)
    causal = lax.broadcasted_iota(jnp.int32, (C, C), 0) >= lax.broadcasted_iota(jnp.int32, (C, C), 1)
    zz = bexp
    for g in range(n_groups):
        wg = jnp.where(causal, w[g], 0.0).astype(bf16)
        zz = zz + jnp.where((lane >= g * gd) & (lane < (g + 1) * gd), _dot(wg, vh), 0.0)
    y = u * zz
    ms = _group_mean(y * y, avg)
    return y * lax.rsqrt(ms + EPS) * gain


def _sgu_inner(T):
    return C_INNER if T % (C_CHUNK * C_INNER) == 0 else 1


def _sgu_fwd(z, col0, d_c, w, bexp, gain, avg):
    T = z.shape[0]
    C = C_CHUNK
    n_in = _sgu_inner(T)
    R = C * n_in
    jb = col0 // d_c
    G = w.shape[0]

    def body(u_ref, v_ref, w_ref, b_ref, gain_ref, avg_ref, out_ref):
        w_v, b_v, gain_v, avg = w_ref[...], b_ref[...], gain_ref[...], avg_ref[...]
        for j in range(n_in):
            rows = pl.ds(j * C, C)
            out_ref[rows, :] = _sgu_chunk(u_ref[rows, :], v_ref[rows, :], w_v, b_v, gain_v, avg, G).astype(bf16)

    return pl.pallas_call(
        body, name="sgu_fwd", grid=(T // R,),
        in_specs=[pl.BlockSpec((R, d_c), lambda i: (i, jb)), pl.BlockSpec((R, d_c), lambda i: (i, jb + 1)),
                  _full((G, C, C)), _full((C, d_c)), _full((1, d_c)), _full((d_c, d_c))],
        out_specs=pl.BlockSpec((R, d_c), lambda i: (i, 0)),
        out_shape=_sds((T, d_c), bf16),
        compiler_params=_cparams(1),
    )(z, z, w, bexp, gain, avg)


def _sgu_bwd(z, col0, d_c, dout, dz_buf, w, bexp, gain, avg):
    T = z.shape[0]
    C = C_CHUNK
    n_in = _sgu_inner(T)
    R = C * n_in
    nc = T // R
    jb = col0 // d_c
    G = w.shape[0]
    gd = d_c // G

    def body(u_ref, v_ref, do_ref, w_ref, b_ref, gain_ref, avg_ref, dzin_ref, dz_ref, dw_ref, db_ref, dgain_ref, dbexp_ref):
        step = pl.program_id(0)

        @pl.when(step == 0)
        def _():
            dw_ref[...] = jnp.zeros_like(dw_ref)
            dgain_ref[...] = jnp.zeros_like(dgain_ref)
            dbexp_ref[...] = jnp.zeros_like(dbexp_ref)

        avg, w_v, b_v, gain_v = avg_ref[...], w_ref[...], b_ref[...], gain_ref[...]
        dw = dbexp = dgain = None
        for j in range(n_in):
            rows = pl.ds(j * C, C)
            _, vjp = jax.vjp(lambda a, b, c, d, e: _sgu_chunk(a, b, c, d, e, avg, G),
                             u_ref[rows, :], v_ref[rows, :], w_v, b_v, gain_v)
            du, dv, dw_j, dbexp_j, dgain_j = vjp(do_ref[rows, :])
            dz_ref[rows, 0:d_c] = du.astype(bf16)
            dz_ref[rows, d_c:2 * d_c] = dv.astype(bf16)
            dw = dw_j if dw is None else dw + dw_j
            dbexp = dbexp_j if dbexp is None else dbexp + dbexp_j
            dgain = dgain_j if dgain is None else dgain + dgain_j
        dw_ref[...] += dw
        dbexp_ref[...] += dbexp
        dgain_ref[...] += dgain

        @pl.when(step == nc - 1)
        def _():
            lane = lax.broadcasted_iota(jnp.int32, (1, d_c), 1)
            acc = dbexp_ref[...]
            for g in range(G):
                sel = jnp.where((lane >= g * gd) & (lane < (g + 1) * gd), acc, 0.0)
                db_ref[:, g:g + 1] = jnp.sum(sel, axis=1, keepdims=True)

    return pl.pallas_call(
        body, name="sgu_bwd", grid=(nc,),
        in_specs=[pl.BlockSpec((R, d_c), lambda i: (i, jb)), pl.BlockSpec((R, d_c), lambda i: (i, jb + 1)),
                  pl.BlockSpec((R, d_c), lambda i: (i, 0)),
                  _full((G, C, C)), _full((C, d_c)), _full((1, d_c)), _full((d_c, d_c)), ANY_SPEC],
        out_specs=(pl.BlockSpec((R, 2 * d_c), lambda i: (i, col0 // (2 * d_c))), _full((G, C, C)), _full((C, G)), _full((1, d_c))),
        out_shape=(_sds(dz_buf.shape, bf16), _sds((G, C, C), f32), _sds((C, G), f32), _sds((1, d_c), f32)),
        scratch_shapes=[pltpu.VMEM((C, d_c), f32)],
        input_output_aliases={7: 0},
        compiler_params=_cparams(1),
    )(z, z, dout, w, bexp, gain, avg, dz_buf)


def _loss_head(h, gain, target, tm):
    T, D = h.shape
    nt = T // tm

    def body(h_ref, gain_ref, tgt_ref, dh_ref, loss_ref, dgain_ref):
        @pl.when(pl.program_id(0) == 0)
        def _():
            loss_ref[...] = jnp.zeros_like(loss_ref)
            dgain_ref[...] = jnp.zeros_like(dgain_ref)

        hv = h_ref[...]
        gain_v = gain_ref[...]
        r = lax.rsqrt(jnp.mean(hv * hv, axis=-1, keepdims=True) + EPS)
        xh = hv * r
        e = xh * gain_v - tgt_ref[...]
        loss_ref[...] += 0.5 * jnp.sum(jnp.mean(e * e, axis=-1, keepdims=True), axis=0, keepdims=True)
        dy = e * (1.0 / D)
        dgain_ref[...] += jnp.sum(dy * xh, axis=0, keepdims=True)
        dxh = dy * gain_v
        dh_ref[...] = r * (dxh - xh * jnp.mean(dxh * xh, axis=-1, keepdims=True))

    tok = pl.BlockSpec((tm, D), lambda i: (i, 0))
    return pl.pallas_call(
        body, name="loss_head", grid=(nt,),
        in_specs=[tok, _full((1, D)), tok],
        out_specs=(tok, _full((1, 128)), _full((1, D))),
        out_shape=(_sds((T, D), f32), _sds((1, 128), f32), _sds((1, D), f32)),
        compiler_params=_cparams(1),
    )(h, gain, target)


def _lower_bounds_fn(logits):
    n = logits.shape[0]
    mx = jnp.max(logits, axis=0, keepdims=True)
    ex = jnp.exp(logits - mx)
    soft = ex / jnp.sum(ex, axis=0, keepdims=True)
    rows = [jnp.zeros_like(soft[0:1])]
    for l in range(1, n):
        rows.append(rows[-1] + soft[l:l + 1])
    return jnp.concatenate(rows, axis=0)


def _lower_bounds(logits):
    def body(x_ref, o_ref):
        o_ref[...] = _lower_bounds_fn(x_ref[...])

    return pl.pallas_call(body, name="lower_bounds", out_shape=_sds(logits.shape, f32))(logits)


def _lower_bounds_bwd(logits, dlb):
    def body(x_ref, d_ref, o_ref):
        _, vjp = jax.vjp(_lower_bounds_fn, x_ref[...])
        o_ref[...] = vjp(d_ref[...])[0]

    return pl.pallas_call(body, name="lower_bounds_bwd", out_shape=_sds(logits.shape, f32))(logits, dlb)


def _adamw(w, g, m, v, rows_blk):
    R, Cc = w.shape
    rb = R if R <= rows_blk else math.gcd(R, rows_blk)

    def body(w_ref, g_ref, m_ref, v_ref, d_ref, nm_ref, nv_ref, go_ref):
        gv = g_ref[...]
        m2 = ADAM_B1 * m_ref[...] + (1.0 - ADAM_B1) * gv
        v2 = ADAM_B2 * v_ref[...] + (1.0 - ADAM_B2) * (gv * gv)
        m_hat = m2 / (1.0 - ADAM_B1 ** ADAM_STEP)
        v_hat = v2 / (1.0 - ADAM_B2 ** ADAM_STEP)
        d_ref[...] = -ADAM_LR * (m_hat / (jnp.sqrt(v_hat) + ADAM_EPS) + ADAM_WD * w_ref[...])
        nm_ref[...] = m2
        nv_ref[...] = v2
        go_ref[...] = gv

    spec = pl.BlockSpec((rb, Cc), lambda i: (i, 0))
    return pl.pallas_call(
        body, name="adamw", grid=(R // rb,),
        in_specs=[spec] * 4, out_specs=(spec,) * 4, out_shape=(_sds((R, Cc), f32),) * 4,
        compiler_params=_cparams(1),
    )(w, g, m, v)


def _pair_sum(grads, recv, c_arr):
    n = len(grads)
    nsh = grads[0].shape[0]

    def body(c_ref, *refs):
        for a in range(n):
            refs[2 * n + a][...] = (refs[a][...].astype(f32) + refs[n + a][...].astype(f32)).astype(bf16)

    g_specs, r_specs, out_shape = [], [], []
    for g in grads:
        _, R, Cc = g.shape
        r2 = R // 2
        g_specs.append(pl.BlockSpec((None, r2, Cc), lambda s, c: (s, c[0], 0)))
        r_specs.append(pl.BlockSpec((None, r2, Cc), lambda s, c: (s, 0, 0)))
        out_shape.append(_sds((nsh, r2, Cc), bf16))
    gs = pltpu.PrefetchScalarGridSpec(num_scalar_prefetch=1, grid=(nsh,), in_specs=g_specs + r_specs, out_specs=tuple(r_specs))
    return list(pl.pallas_call(body, name="pair_sum", grid_spec=gs, out_shape=tuple(out_shape),
                               compiler_params=_cparams(1))(c_arr, *grads, *recv))


def _add(a, b):
    def body(a_ref, b_ref, o_ref):
        o_ref[...] = a_ref[...] + b_ref[...]

    return pl.pallas_call(body, name="pair_sum_small", out_shape=_sds(a.shape, f32))(a, b)


def _chip_sum(hsum, recv, bufs, slot_arr, c_arr, layer, n_layers):
    n = len(hsum)
    prev = list(bufs)
    steps = 2

    def body(s_ref, c_ref, *refs):
        outs = refs[len(refs) - n:]
        for a in range(n):
            acc = refs[a][...].astype(f32)
            for j in range(N_CHIPS - 1):
                acc = acc + refs[n + a][j].astype(f32)
            outs[a][...] = acc

    h_specs, r_specs, o_specs, out_shape = [], [], [], []
    for hh in hsum:
        _, r2, Cc = hh.shape
        rt = r2 // steps
        h_specs.append(pl.BlockSpec((None, rt, Cc), lambda i, s, c: (s[0], i, 0)))
        r_specs.append(pl.BlockSpec((N_CHIPS - 1, rt, Cc), lambda i, s, c: (0, i, 0)))
        o_specs.append(pl.BlockSpec((None, rt, Cc), lambda i, s, c: (layer, c[0] * steps + i, 0)))
        out_shape.append(_sds((n_layers, 2 * r2, Cc), f32))
    gs = pltpu.PrefetchScalarGridSpec(num_scalar_prefetch=2, grid=(steps,),
                                      in_specs=h_specs + r_specs + [ANY_SPEC] * len(prev), out_specs=tuple(o_specs))
    return list(pl.pallas_call(body, name="chip_sum", grid_spec=gs, out_shape=tuple(out_shape),
                               input_output_aliases={2 + 2 * n + a: a for a in range(len(prev))},
                               compiler_params=_cparams(1))(slot_arr, c_arr, *hsum, *recv, *prev))


def _sum_slots(x):
    def body(x_ref, o_ref):
        acc = x_ref[0]
        for j in range(1, x.shape[0]):
            acc = acc + x_ref[j]
        o_ref[...] = acc

    return pl.pallas_call(body, name="sum_slots", out_shape=_sds(x.shape[1:], f32))(x)


def _blockdiag(w):
    nb, bd, _ = w.shape
    eye = jnp.eye(nb, dtype=w.dtype)
    return (eye[:, None, :, None] * w[:, :, None, :]).reshape(nb * bd, nb * bd)


def _blockdiag_extract(dense, nb):
    bd = dense.shape[0] // nb
    d4 = dense.reshape(nb, bd, nb, bd)
    return jnp.stack([d4[i, :, i, :] for i in range(nb)])


def _pack(arrays, multiple):
    flat = jnp.concatenate([a.reshape(-1).astype(f32) for a in arrays])
    pad = (-flat.shape[0]) % multiple
    return jnp.pad(flat, (0, pad))


def _unpack(flat, shapes):
    out, off = [], 0
    for s in shapes:
        n = int(np.prod(s))
        out.append(flat[off:off + n].reshape(s))
        off += n
    return out


BIG = ("ffn1_wg", "ffn1_wu", "ffn1_wd", "w_in", "w_out", "ffn2_wg", "ffn2_wu", "ffn2_wd")
TRANSPOSED = ("ffn1_wg", "ffn1_wu", "ffn2_wg", "ffn2_wu")
SMALL = ("ffn1_norm", "mix_norm", "hgrn_lb_logits", "hgrn_norm", "conv_w", "conv_b", "lru_wa", "lru_ba", "lru_wx",
         "lru_bx", "lru_lambda", "lru_norm", "sgu_w", "sgu_b", "sgu_norm", "ffn2_norm", "final_norm")
WEIGHTS = ("ffn1_norm", "ffn1_wg", "ffn1_wu", "ffn1_wd", "mix_norm", "w_in", "hgrn_lb_logits", "hgrn_norm", "conv_w",
           "conv_b", "lru_wa", "lru_ba", "lru_wx", "lru_bx", "lru_lambda", "lru_norm", "sgu_w", "sgu_b", "sgu_norm",
           "w_out", "ffn2_norm", "ffn2_wg", "ffn2_wu", "ffn2_wd", "final_norm")


def kernel(x, ffn1_norm, ffn1_wg, ffn1_wu, ffn1_wd, mix_norm, w_in, hgrn_lb_logits, hgrn_norm, conv_w, conv_b, lru_wa, lru_ba, lru_wx, lru_bx, lru_lambda, lru_norm, sgu_w, sgu_b, sgu_norm, w_out, ffn2_norm, ffn2_wg, ffn2_wu, ffn2_wd, final_norm, loss_target, m_ffn1_norm, m_ffn1_wg, m_ffn1_wu, m_ffn1_wd, m_mix_norm, m_w_in, m_hgrn_lb_logits, m_hgrn_norm, m_conv_w, m_conv_b, m_lru_wa, m_lru_ba, m_lru_wx, m_lru_bx, m_lru_lambda, m_lru_norm, m_sgu_w, m_sgu_b, m_sgu_norm, m_w_out, m_ffn2_norm, m_ffn2_wg, m_ffn2_wu, m_ffn2_wd, m_final_norm, v_ffn1_norm, v_ffn1_wg, v_ffn1_wu, v_ffn1_wd, v_mix_norm, v_w_in, v_hgrn_lb_logits, v_hgrn_norm, v_conv_w, v_conv_b, v_lru_wa, v_lru_ba, v_lru_wx, v_lru_bx, v_lru_lambda, v_lru_norm, v_sgu_w, v_sgu_b, v_sgu_norm, v_w_out, v_ffn2_norm, v_ffn2_wg, v_ffn2_wu, v_ffn2_wd, v_final_norm):
    args = dict(locals())
    W = {n: args[n] for n in WEIGHTS}
    M = {n: args["m_" + n] for n in WEIGHTS}
    V = {n: args["v_" + n] for n in WEIGHTS}

    T, D = x.shape[1], x.shape[2]
    L = ffn1_norm.shape[0]
    d_a, d_b, d_c = hgrn_norm.shape[1], lru_norm.shape[1], sgu_norm.shape[1]
    col_b, col_c = 4 * d_a, 4 * d_a + 2 * d_b
    tm = 512 if T % 512 == 0 else T
    tm_w = 1024 if T % 1024 == 0 else tm
    tm_d = 256 if T % 256 == 0 else tm
    my_c = lax.axis_index("c")
    my_slot = 2 * lax.axis_index("x") + lax.axis_index("y")
    c_arr = jnp.reshape(my_c, (1,)).astype(jnp.int32)
    slot_arr = jnp.reshape(my_slot, (1,)).astype(jnp.int32)

    nb = len(BIG)
    gplan = _gather_ici_plan(nb)

    def kview(a, n):
        return jnp.swapaxes(a, 1, 2) if n in TRANSPOSED else a

    Wk = {n: kview(W[n], n) for n in BIG}
    place_steps = 4 if all(Wk[n].shape[1] % 64 == 0 for n in BIG) else 2

    def placed(l):
        return _cast_place([Wk[n] for n in BIG], l, slot_arr, place_steps)

    conv_land = lax.dynamic_update_slice_in_dim(jnp.zeros((N_CHIPS,) + conv_w.shape, f32), conv_w[None], my_slot, axis=0)
    lands0 = placed(0)
    n_first = 3
    first = lands0[:n_first] + [conv_land]
    got = _exchange("gather0_ici", first, [_sds(a.shape, a.dtype) for a in first], _gather_ici_plan(n_first + 1),
                    aliases={a: a for a in range(n_first + 1)})
    got = _gather_d2d("gather0_d2d", got)
    G = [None] * L
    G[0] = dict(zip(BIG[:n_first], got[:n_first]))
    conv_full = jnp.transpose(got[n_first], (1, 2, 0, 3)).reshape(L, CONV_WIDTH, d_b)
    rest_plan = _gather_ici_plan(nb - n_first)
    rest_pending = _start_copies("gather_start_0", [], lands0[n_first:], rest_plan, got[0])

    def start_gather(l, after):
        return _start_copies(f"gather_start_{l}", [], placed(l), gplan, after)

    d2d_plan = _gather_d2d_plan(nb)

    lb = _lower_bounds(hgrn_lb_logits)
    avg_b = _group_avg_matrix(d_b, d_b // B_BLOCKS)
    avg_c = _group_avg_matrix(d_c, d_c // C_GROUPS)
    wa_dense = [_blockdiag(lru_wa[l]) for l in range(L)]
    wx_dense = [_blockdiag(lru_wx[l]) for l in range(L)]
    bexp = [jnp.repeat(sgu_b[l].T, d_c // C_GROUPS, axis=1) for l in range(L)]

    def lru_params(l):
        return (conv_full[l], conv_b[l][None], wa_dense[l], lru_ba[l].reshape(1, d_b), wx_dense[l],
                lru_bx[l].reshape(1, d_b), lru_lambda[l][None], lru_norm[l][None], avg_b)

    h = x.reshape(T, D)
    saved = []
    for l in range(L):
        s = {"h0": h}
        gain1, gain_mix = ffn1_norm[l][None], mix_norm[l][None]
        pending = None
        if l == 0:
            gain1 = gain1 + rest_pending[4][0:1, 0:1]
        elif l + 1 < L:
            pending = start_gather(l + 1, h)
            gain1 = gain1 + pending[4][0:1, 0:1]
        g = G[l]
        h, s["g1"], s["u1"] = _ffn_fwd(h, gain1, g["ffn1_wg"], g["ffn1_wu"], g["ffn1_wd"], tm)
        s["h1"] = h
        if l == 0:
            send, recv, _, lands, _ = rest_pending
            lands = _wait_copies("gather_wait_0", send, recv, [], lands, rest_plan, h)
            g.update(zip(BIG[n_first:], _gather_d2d("gather_d2d", lands)))
            if L > 1:
                pending = start_gather(1, g["w_in"])
                gain_mix = gain_mix + pending[4][0:1, 0:1]
        z = _proj_in_fwd(h, gain_mix, g["w_in"], tm)
        s["z"] = z
        s["oa"], s["o_pre"], s["states"] = _hgrn_fwd(z, lb[l][None], hgrn_norm[l][None], d_a)
        s["ob"], s["hl"] = _lru_fwd(z, col_b, d_b, *lru_params(l))
        s["oc"] = _sgu_fwd(z, col_c, d_c, sgu_w[l], bexp[l], sgu_norm[l][None], avg_c)
        h = _proj_out_fwd(h, s["oa"], s["ob"], s["oc"], g["w_out"], tm)
        s["h2"] = h
        gain2 = ffn2_norm[l][None]
        forward = None
        if pending is not None:
            send, recv, _, lands, _ = pending
            lands = _wait_copies(f"gather_wait_{l + 1}", send, recv, [], lands, gplan, h)
            forward = _start_copies(f"gather_d2d_start_{l + 1}", [], lands, d2d_plan)
            gain2 = gain2 + forward[4][0:1, 0:1]
        h, s["g2"], s["u2"] = _ffn_fwd(h, gain2, g["ffn2_wg"], g["ffn2_wu"], g["ffn2_wd"], tm)
        saved.append(s)
        if forward is not None:
            send, recv, _, lands, _ = forward
            G[l + 1] = dict(zip(BIG, _wait_copies(f"gather_d2d_wait_{l + 1}", send, recv, [], lands, d2d_plan, h)))

    dh, loss_part, d_final = _loss_head(h, final_norm[None], loss_target.reshape(T, D), tm)
    loss = lax.psum(loss_part[0, 0], ("x", "y", "c"))

    def pair_views(n_big):
        r = [(lambda i, o, p, a=a: i[a].at[:, pl.ds((1 - p.c) * (i[a].shape[1] // 2), i[a].shape[1] // 2)],
              lambda i, o, p, a=a: o[a], "sib") for a in range(n_big)]
        return r

    def chip_plan_for(n):
        return [(lambda s_, o, p, a=a, kind=kind: s_[a].at[p.peer_slot(kind)], lambda s_, o, p, a=a, j=j: o[a].at[j], kind)
                for a in range(n) for j, kind in enumerate(CHIP_KINDS)]

    chip_plan = chip_plan_for(nb)
    sbufs = {n: None for n in BIG}

    def pair_phase(arrs, extra=None):
        n = len(arrs)
        ins, remote = list(arrs), pair_views(n)
        outs = [_sds((N_CHIPS, a.shape[1] // 2, a.shape[2]), bf16) for a in arrs]
        if extra is not None:
            ins.append(extra)
            outs.append(_sds(extra.shape, f32))
            remote = remote + [(lambda i, o, p: i[n], lambda i, o, p: o[n], "sib")]
        recv = _exchange("grad_pair_d2d", ins, outs, remote)
        return _pair_sum(arrs, recv[:n], c_arr), (None if extra is None else _add(extra, recv[n]))

    def chip_sum_into(names, hs, lands, l):
        prev = [sbufs[n] for n in names] if sbufs[names[0]] is not None else []
        for n, buf in zip(names, _chip_sum(hs, lands, prev, slot_arr, c_arr, l, L)):
            sbufs[n] = buf

    def share(l, extra_in=(), extra_out=(), extra_remote=(), extra_local=()):
        remote = [(lambda i, o, p, a=a: _half(o[a].at[l], p.c), lambda i, o, p, a=a: _half(o[a].at[l], p.c), "sib")
                  for a in range(nb)]
        outs = [_sds(sbufs[n].shape, f32) for n in BIG] + list(extra_out)
        res = _exchange("grad_share_d2d", [sbufs[n] for n in BIG] + list(extra_in), outs, remote + list(extra_remote),
                        list(extra_local), aliases={a: a for a in range(nb)})
        for n, buf in zip(BIG, res[:nb]):
            sbufs[n] = buf
        return res[nb:]

    small = {n: [None] * L for n in SMALL if n != "final_norm"}
    chip_pending = pair_pending = early = None
    early_names = ("w_in", "w_out", "ffn2_wg", "ffn2_wu", "ffn2_wd")
    for l in reversed(range(L)):
        s, g = saved[l], G[l]
        gain2, gain_a = ffn2_norm[l][None], hgrn_norm[l][None]
        if pair_pending is not None:
            gain2 = gain2 + pair_pending[0][4][0:1, 0:1]
        dh, small["ffn2_norm"][l], dg, du, xn, dob = _ffn_bwd_dgrad(
            s["h2"], gain2, dh, s["g2"], s["u2"], g["ffn2_wg"], g["ffn2_wu"], g["ffn2_wd"], tm_d)
        if pair_pending is not None:
            (send, recv, grads_prev, lands, _), = pair_pending
            recv_a = _wait_copies(f"grad_pair_wait_{l + 1}", send, recv, grads_prev, lands, pair_views(nb), dh)
            hsum = _pair_sum(grads_prev, recv_a, c_arr)
            lands = [lax.empty((N_CHIPS - 1,) + hh.shape[1:], bf16) for hh in hsum]
            chip_pending = (_start_copies(f"grad_chip_start_{l + 1}", hsum, lands, chip_plan), hsum)
            gain_a = gain_a + chip_pending[0][4][0:1, 0:1]
            pair_pending = None
        dwg2, dwu2, dwd2 = _ffn_bwd_wgrad(xn, dob, s["g2"], s["u2"], dg, du, tm_w)
        doa, dob_, doc, dwo = _proj_out_bwd(dh, s["oa"], s["ob"], s["oc"], g["w_out"], tm)
        dz, small["hgrn_lb_logits"][l], small["hgrn_norm"][l] = _hgrn_bwd(
            s["z"], lb[l][None], gain_a, s["o_pre"], s["states"], doa, d_a)
        (dz, small["conv_w"][l], small["conv_b"][l], dwa, small["lru_ba"][l], dwx, small["lru_bx"][l],
         small["lru_lambda"][l], small["lru_norm"][l]) = _lru_bwd(s["z"], col_b, d_b, s["hl"], dob_, dz, *lru_params(l))
        small["lru_wa"][l] = _blockdiag_extract(dwa, B_BLOCKS)
        small["lru_wx"][l] = _blockdiag_extract(dwx, B_BLOCKS)
        dz, small["sgu_w"][l], dsb, small["sgu_norm"][l] = _sgu_bwd(
            s["z"], col_c, d_c, doc, dz, sgu_w[l], bexp[l], sgu_norm[l][None], avg_c)
        small["sgu_b"][l] = dsb.T
        dh, small["mix_norm"][l], xn = _proj_in_bwd_dgrad(s["h1"], mix_norm[l][None], dh, dz, g["w_in"], tm)
        dwi = _proj_in_bwd_wgrad(xn, dz, N_CHIPS)
        gain1 = ffn1_norm[l][None]
        if l == 0:
            hs_e, _ = pair_phase([dwi, dwo, dwg2, dwu2, dwd2])
            lands = [lax.empty((N_CHIPS - 1,) + hh.shape[1:], bf16) for hh in hs_e]
            early = (_start_copies("grad_chip_start_0", hs_e, lands, chip_plan_for(len(hs_e))), hs_e)
            gain1 = gain1 + early[0][4][0:1, 0:1]
        dh, small["ffn1_norm"][l], dg, du, xn, dob = _ffn_bwd_dgrad(
            s["h0"], gain1, dh, s["g1"], s["u1"], g["ffn1_wg"], g["ffn1_wu"], g["ffn1_wd"], tm_d)
        dwg1, dwu1, dwd1 = _ffn_bwd_wgrad(xn, dob, s["g1"], s["u1"], dg, du, tm_w)
        layer_grads = [dwg1, dwu1, dwd1, dwi, dwo, dwg2, dwu2, dwd2]

        if chip_pending is not None:
            (send, recv, hs, lands, _), hsum_prev = chip_pending
            lands = _wait_copies(f"grad_chip_wait_{l + 1}", send, recv, hs, lands, chip_plan, dwg1)
            chip_sum_into(BIG, hsum_prev, lands, l + 1)
            share(l + 1)
            chip_pending = None
        if l > 0:
            lands = [lax.empty((N_CHIPS, gr.shape[1] // 2, gr.shape[2]), bf16) for gr in layer_grads]
            pair_pending = (_start_copies(f"grad_pair_start_{l}", layer_grads, lands, pair_views(nb)),)
    grad_x = dh.reshape(x.shape)

    (send, recv, hs, lands, _), hs_e = early
    lands = _wait_copies("grad_chip_wait_0", send, recv, hs, lands, chip_plan_for(len(hs_e)), dwg1)
    chip_sum_into(early_names, hs_e, lands, 0)
    small_names = [n for n in SMALL]
    small_parts = [jnp.stack([jnp.reshape(v, (-1,)) for v in small[n]]) if n != "final_norm" else d_final for n in small_names]
    small_shapes = [p.shape for p in small_parts]
    packed = _pack(small_parts, 2 * 8 * 128).reshape(2, -1, 128)
    n_rows = packed.shape[1]
    late = [dwg1, dwu1, dwd1]
    nl = len(late)
    hsum, small_pair = pair_phase(late, packed)
    remote = chip_plan_for(nl) + [(lambda i, o, p: i[nl].at[p.c], lambda i, o, p: o[nl].at[p.slot], kind) for kind in CHIP_KINDS]
    local = [(lambda i, o, p: i[nl].at[p.c], lambda i, o, p: o[nl].at[p.slot])]
    outs = [_sds((N_CHIPS - 1,) + hh.shape[1:], bf16) for hh in hsum] + [_sds((N_CHIPS, n_rows, 128), f32)]
    recv_b = _exchange("grad_chip_ici_last", hsum + [small_pair], outs, remote, local)
    chip_sum_into(BIG[:nl], hsum, recv_b[:nl], 0)
    small_half = _sum_slots(recv_b[nl])
    (small_all,) = share(0, extra_in=[small_half], extra_out=[_sds(packed.shape, f32)],
                         extra_remote=[(lambda i, o, p: i[nb], lambda i, o, p: o[nb].at[p.c], "sib")],
                         extra_local=[(lambda i, o, p: i[nb], lambda i, o, p: o[nb].at[p.c])])
    grads = {n: kview(sbufs[n], n) for n in BIG}
    small_tot = _unpack(small_all.reshape(-1), small_shapes)
    for n, val in zip(small_names, small_tot):
        grads[n] = val
    grads["hgrn_lb_logits"] = _lower_bounds_bwd(hgrn_lb_logits, grads["hgrn_lb_logits"])
    shard_cols = conv_w.shape[2]
    grads["conv_w"] = lax.dynamic_slice_in_dim(grads["conv_w"].reshape(L, CONV_WIDTH, d_b), my_slot * shard_cols, shard_cols, axis=2)
    for n in SMALL:
        grads[n] = grads[n].reshape(W[n].shape)

    delta, new_m, new_v = {}, {}, {}
    for n in BIG:
        shape = Wk[n].shape
        flat = [a.reshape(-1, shape[-1]) for a in (Wk[n], sbufs[n], kview(M[n], n), kview(V[n], n))]
        delta[n], new_m[n], new_v[n], grads[n] = [kview(o.reshape(shape), n) for o in _adamw(*flat, 512)]
    shapes = [W[n].shape for n in SMALL]
    packs = [_pack([src[n] for n in SMALL], 8 * 128).reshape(-1, 128) for src in (W, grads, M, V)]
    d2, m2, v2, _ = _adamw(*packs, 4096)
    for dst, val in ((delta, d2), (new_m, m2), (new_v, v2)):
        for n, piece in zip(SMALL, _unpack(val.reshape(-1), shapes)):
            dst[n] = piece

    return (loss, grad_x, *[grads[n] for n in WEIGHTS], *[delta[n] for n in WEIGHTS],
            *[new_m[n] for n in WEIGHTS], *[new_v[n] for n in WEIGHTS])
```

```python
import math

import numpy as np
import jax
import jax.numpy as jnp
from jax import lax
from jax.experimental import pallas as pl
from jax.experimental.pallas import tpu as pltpu

f32 = jnp.float32
bf16 = jnp.bfloat16
HI = lax.Precision.HIGHEST
MESH = pl.DeviceIdType.MESH

EPS = 1e-6
HEAD = 128
A_CHUNK = 64
A_SUB = 16
A_INNER = 2
SUBLANES = 8
B_BLOCKS = 4
B_CHUNK = 256
CONV_WIDTH = 4
LRU_C = 8.0
C_GROUPS = 4
C_CHUNK = 128
C_INNER = 4
N_CHIPS = 4
ADAM_LR, ADAM_B1, ADAM_B2, ADAM_EPS, ADAM_WD, ADAM_STEP = 0.001, 0.9, 0.999, 1e-08, 0.01, 10
VMEM_LIMIT = 56 * 1024 * 1024


def _cparams(n_axes):
    return pltpu.CompilerParams(dimension_semantics=("arbitrary",) * n_axes, vmem_limit_bytes=VMEM_LIMIT)


def _sds(shape, dtype):
    return jax.ShapeDtypeStruct(tuple(shape), dtype)


def _full(shape):
    n = len(shape)
    return pl.BlockSpec(tuple(shape), lambda *_: (0,) * n)


def _resident(shape):
    n = len(shape)
    return pl.BlockSpec(tuple(shape), lambda *_: (0,) * n, pipeline_mode=pl.Buffered(1))


def _dot(a, b):
    return jnp.dot(a, b, preferred_element_type=f32)


def _dot_nt(a, b):
    return lax.dot_general(a, b, (((1,), (1,)), ((), ())), preferred_element_type=f32)


def _dot_tn(a, b):
    return lax.dot_general(a, b, (((0,), (0,)), ((), ())), preferred_element_type=f32)


def _silu(x):
    return x * jax.nn.sigmoid(x)


def _group_avg_matrix(n, group):
    idx = np.arange(n) // group
    return jnp.asarray((idx[:, None] == idx[None, :]).astype(np.float32) / group)


class _Place:
    def __init__(self):
        self.x, self.y, self.c = lax.axis_index("x"), lax.axis_index("y"), lax.axis_index("c")
        self.slot = 2 * self.x + self.y

    def peer(self, kind):
        x, y, c = self.x, self.y, self.c
        return {"sib": (x, y, 1 - c), "fx": (1 - x, y, c), "fy": (x, 1 - y, c), "fxy": (1 - x, 1 - y, c)}[kind]

    def peer_slot(self, kind):
        x, y = self.x, self.y
        return {"fx": 2 * (1 - x) + y, "fy": 2 * x + (1 - y), "fxy": 2 * (1 - x) + (1 - y)}[kind]


CHIP_KINDS = ("fx", "fy", "fxy")


def _exchange(name, ins, outs, remote, local=(), aliases=None):
    n_in, n_out, n_r, n_l = len(ins), len(outs), len(remote), len(local)

    def body(*refs):
        in_refs, out_refs = refs[:n_in], refs[n_in:n_in + n_out]
        send, recv, lsem = refs[n_in + n_out:]
        p = _Place()
        lcopies = []
        for t, (src, dst) in enumerate(local):
            cp = pltpu.make_async_copy(src(in_refs, out_refs, p), dst(in_refs, out_refs, p), lsem.at[t])
            cp.start()
            lcopies.append(cp)
        copies = []
        for t, (src, dst, kind) in enumerate(remote):
            cp = pltpu.make_async_remote_copy(
                src_ref=src(in_refs, out_refs, p), dst_ref=dst(in_refs, out_refs, p),
                send_sem=send.at[t], recv_sem=recv.at[t], device_id=p.peer(kind), device_id_type=MESH)
            cp.start()
            copies.append(cp)
        for cp in copies:
            cp.wait_recv()
        for cp in copies:
            cp.wait_send()
        for cp in lcopies:
            cp.wait()

    anyspec = pl.BlockSpec(memory_space=pl.ANY)
    res = pl.pallas_call(
        body, name=name, out_shape=tuple(outs),
        in_specs=[anyspec] * n_in, out_specs=tuple([anyspec] * n_out),
        scratch_shapes=[pltpu.SemaphoreType.DMA((n_r,)), pltpu.SemaphoreType.DMA((n_r,)),
                        pltpu.SemaphoreType.DMA((max(n_l, 1),))],
        input_output_aliases=aliases or {},
        compiler_params=pltpu.CompilerParams(has_side_effects=True),
    )(*ins)
    return list(res)


HBM_SPEC = pl.BlockSpec(memory_space=pltpu.HBM)
SEM_SPEC = pl.BlockSpec(memory_space=pltpu.SEMAPHORE)
ANY_SPEC = pl.BlockSpec(memory_space=pl.ANY)
DATAFLOW = pltpu.SideEffectType.DATAFLOW_SIDE_EFFECTING


def _in_hbm(a):
    return pltpu.with_memory_space_constraint(a, pltpu.HBM)


def _start_copies(name, srcs, lands, remote, after=None):
    n_s, n_l, n_r = len(srcs), len(lands), len(remote)
    extra = [] if after is None else [after]

    def body(*refs):
        src_refs, land_refs = refs[:n_s], refs[n_s:n_s + n_l]
        n_in = n_s + n_l + len(extra)
        send, recv = refs[n_in], refs[n_in + 1]
        token = refs[-1]
        p = _Place()
        for t, (src, dst, kind) in enumerate(remote):
            pltpu.make_async_remote_copy(
                src_ref=src(src_refs, land_refs, p), dst_ref=dst(src_refs, land_refs, p),
                send_sem=send.at[t], recv_sem=recv.at[t], device_id=p.peer(kind), device_id_type=MESH).start()
        token[...] = jnp.zeros_like(token)

    thru = [pltpu.HBM(a.shape, a.dtype) for a in lands]
    res = pl.pallas_call(
        body, name=name,
        out_shape=(pltpu.SemaphoreType.DMA((n_r,)), pltpu.SemaphoreType.DMA((n_r,)), *thru, _sds((8, 128), f32)),
        in_specs=[ANY_SPEC] * n_s + [HBM_SPEC] * n_l + [ANY_SPEC] * len(extra),
        out_specs=(SEM_SPEC, SEM_SPEC, *([HBM_SPEC] * n_l), pl.BlockSpec(memory_space=pltpu.VMEM)),
        input_output_aliases={n_s + i: 2 + i for i in range(n_l)},
        compiler_params=pltpu.CompilerParams(has_side_effects=DATAFLOW),
    )(*srcs, *[_in_hbm(a) for a in lands], *extra)
    return res[0], res[1], list(srcs), list(res[2:2 + n_l]), res[-1]


def _wait_copies(name, send, recv, srcs, lands, remote, after):
    n_s, n_l = len(srcs), len(lands)

    def body(*refs):
        src_refs, land_refs = refs[:n_s], refs[n_s:n_s + n_l]
        send_ref, recv_ref = refs[n_s + n_l], refs[n_s + n_l + 1]
        p = _Place()
        for t, (src, dst, kind) in enumerate(remote):
            cp = pltpu.make_async_remote_copy(
                src_ref=src(src_refs, land_refs, p), dst_ref=dst(src_refs, land_refs, p),
                send_sem=send_ref.at[t], recv_sem=recv_ref.at[t], device_id=p.peer(kind), device_id_type=MESH)
            cp.wait_send()
            cp.wait_recv()

    res = pl.pallas_call(
        body, name=name, out_shape=tuple(pltpu.HBM(a.shape, a.dtype) for a in lands),
        in_specs=[ANY_SPEC] * n_s + [HBM_SPEC] * n_l + [SEM_SPEC, SEM_SPEC, ANY_SPEC],
        out_specs=tuple([HBM_SPEC] * n_l),
        input_output_aliases={n_s + i: i for i in range(n_l)},
        compiler_params=pltpu.CompilerParams(has_side_effects=DATAFLOW),
    )(*srcs, *lands, send, recv, after)
    return list(res)


def _half(ref, c):
    n2 = ref.shape[0] // 2
    return ref.at[pl.ds(c * n2, n2)]


def _gather_ici_plan(n):
    def view(a):
        return lambda s, o, p: _half(o[a].at[p.slot], p.c)

    return [(view(a), view(a), kind) for a in range(n) for kind in CHIP_KINDS]


def _gather_d2d_plan(n):
    remote = []
    for a in range(n):
        for kind in CHIP_KINDS:
            view = lambda i, o, p, a=a, kind=kind: _half(o[a].at[p.peer_slot(kind)], p.c)
            remote.append((view, view, "sib"))
    return remote


def _gather_d2d(name, lands):
    n = len(lands)
    outs = [_sds(g.shape, g.dtype) for g in lands]
    return _exchange(name, list(lands), outs, _gather_d2d_plan(n), aliases={a: a for a in range(n)})


def _cast_place(weights, layer, slot_arr, n_steps=4):
    def body(s_ref, *refs):
        n = len(refs) // 2
        for a in range(n):
            refs[n + a][...] = refs[a][...].astype(bf16)

    in_specs, out_specs, out_shape = [], [], []
    for w in weights:
        _, R, Cc = w.shape
        rt = R // n_steps
        in_specs.append(pl.BlockSpec((None, rt, Cc), lambda i, s: (layer, i, 0)))
        out_specs.append(pl.BlockSpec((None, rt, Cc), lambda i, s: (s[0], i, 0)))
        out_shape.append(_sds((N_CHIPS, R, Cc), bf16))
    gs = pltpu.PrefetchScalarGridSpec(num_scalar_prefetch=1, grid=(n_steps,), in_specs=in_specs, out_specs=tuple(out_specs))
    return list(pl.pallas_call(body, name="cast_place", grid_spec=gs, out_shape=tuple(out_shape),
                               compiler_params=_cparams(1))(slot_arr, *weights))


def _ffn_fwd(h, gain, wg, wu, wd, tm):
    T, D = h.shape
    nsh, F = wg.shape[0], wg.shape[1]
    nt = T // tm

    def body(h_ref, gain_ref, wg_ref, wu_ref, wd_ref, out_ref, gs_ref, us_ref):
        hv = h_ref[...]
        r = lax.rsqrt(jnp.mean(hv * hv, axis=-1, keepdims=True) + EPS)
        xn = (hv * r * gain_ref[...]).astype(bf16)
        acc = None
        for k in range(nsh):
            g = _dot_nt(xn, wg_ref[k])
            u = _dot_nt(xn, wu_ref[k])
            gs_ref[k] = g.astype(bf16)
            us_ref[k] = u.astype(bf16)
            part = _dot((_silu(g) * u).astype(bf16), wd_ref[k])
            acc = part if acc is None else acc + part
        out_ref[...] = hv + 0.5 * acc

    sav = pl.BlockSpec((nsh, tm, F), lambda i: (0, i, 0))
    return pl.pallas_call(
        body, name="ffn_fwd", grid=(nt,),
        in_specs=[pl.BlockSpec((tm, D), lambda i: (i, 0)), _full((1, D)), _resident((nsh, F, D)), _resident((nsh, F, D)),
                  _resident((nsh, F, D))],
        out_specs=(pl.BlockSpec((tm, D), lambda i: (i, 0)), sav, sav),
        out_shape=(_sds((T, D), f32), _sds((nsh, T, F), bf16), _sds((nsh, T, F), bf16)),
        compiler_params=_cparams(1),
    )(h, gain, wg, wu, wd)


def _ffn_bwd_dgrad(h, gain, dout, gs, us, wg, wu, wd, tm):
    T, D = h.shape
    nsh, F = wg.shape[0], wg.shape[1]
    nt = T // tm

    def body(h_ref, gain_ref, dout_ref, gs_ref, us_ref, wg_ref, wu_ref, wd_ref,
             dh_ref, dgain_ref, dg_ref, du_ref, xn_ref, dob_ref):
        @pl.when(pl.program_id(0) == 0)
        def _():
            dgain_ref[...] = jnp.zeros_like(dgain_ref)

        hv = h_ref[...]
        r = lax.rsqrt(jnp.mean(hv * hv, axis=-1, keepdims=True) + EPS)
        xh = hv * r
        xn_ref[...] = (xh * gain_ref[...]).astype(bf16)
        dv = dout_ref[...]
        dob = (0.5 * dv).astype(bf16)
        dob_ref[...] = dob
        dxn = None
        for k in range(nsh):
            da = _dot_nt(dob, wd_ref[k])
            g = gs_ref[k].astype(f32)
            u = us_ref[k].astype(f32)
            sg = jax.nn.sigmoid(g)
            dg = (da * u * (sg * (1.0 + g * (1.0 - sg)))).astype(bf16)
            du = (da * (g * sg)).astype(bf16)
            dg_ref[k] = dg
            du_ref[k] = du
            part = _dot(dg, wg_ref[k]) + _dot(du, wu_ref[k])
            dxn = part if dxn is None else dxn + part
        dgain_ref[...] += jnp.sum(dxn * xh, axis=0, keepdims=True)
        dxh = dxn * gain_ref[...]
        dh_ref[...] = dv + r * (dxh - xh * jnp.mean(dxh * xh, axis=-1, keepdims=True))

    tok = pl.BlockSpec((tm, D), lambda i: (i, 0))
    sav = pl.BlockSpec((nsh, tm, F), lambda i: (0, i, 0))
    return pl.pallas_call(
        body, name="ffn_bwd_dgrad", grid=(nt,),
        in_specs=[tok, _full((1, D)), tok, sav, sav, _resident((nsh, F, D)), _resident((nsh, F, D)), _resident((nsh, F, D))],
        out_specs=(tok, _full((1, D)), sav, sav, tok, tok),
        out_shape=(_sds((T, D), f32), _sds((1, D), f32), _sds((nsh, T, F), bf16), _sds((nsh, T, F), bf16),
                   _sds((T, D), bf16), _sds((T, D), bf16)),
        compiler_params=_cparams(1),
    )(h, gain, dout, gs, us, wg, wu, wd)


def _ffn_bwd_wgrad(xn, dob, gs, us, dg, du, tm):
    T, D = xn.shape
    nsh, F = gs.shape[0], gs.shape[2]
    nt = T // tm

    def body(xn_ref, dob_ref, gs_ref, us_ref, dg_ref, du_ref, dwg_ref, dwu_ref, dwd_ref, ag_ref, au_ref, ad_ref):
        i = pl.program_id(1)

        @pl.when(i == 0)
        def _():
            ag_ref[...] = jnp.zeros_like(ag_ref)
            au_ref[...] = jnp.zeros_like(au_ref)
            ad_ref[...] = jnp.zeros_like(ad_ref)

        xn_v = xn_ref[...]
        ag_ref[...] += _dot_tn(dg_ref[...], xn_v)
        au_ref[...] += _dot_tn(du_ref[...], xn_v)
        g = gs_ref[...].astype(f32)
        a = (_silu(g) * us_ref[...].astype(f32)).astype(bf16)
        ad_ref[...] += _dot_tn(a, dob_ref[...])

        @pl.when(i == nt - 1)
        def _():
            dwg_ref[...] = ag_ref[...].astype(bf16)
            dwu_ref[...] = au_ref[...].astype(bf16)
            dwd_ref[...] = ad_ref[...].astype(bf16)

    tok = pl.BlockSpec((tm, D), lambda k, i: (i, 0))
    sav = pl.BlockSpec((None, tm, F), lambda k, i: (k, i, 0))
    wdspec = pl.BlockSpec((None, F, D), lambda k, i: (k, 0, 0))
    return pl.pallas_call(
        body, name="ffn_bwd_wgrad", grid=(nsh, nt),
        in_specs=[tok, tok, sav, sav, sav, sav],
        out_specs=(wdspec, wdspec, wdspec),
        out_shape=(_sds((nsh, F, D), bf16),) * 3,
        scratch_shapes=[pltpu.VMEM((F, D), f32)] * 3,
        compiler_params=_cparams(2),
    )(xn, dob, gs, us, dg, du)


def _proj_in_fwd(h, gain, w_in, tm):
    T, D = h.shape
    nsh, N = w_in.shape[0], w_in.shape[2]
    nt = T // tm

    def body(h_ref, gain_ref, w_ref, z_ref):
        hv = h_ref[...]
        r = lax.rsqrt(jnp.mean(hv * hv, axis=-1, keepdims=True) + EPS)
        xn = (hv * r * gain_ref[...]).astype(bf16)
        for k in range(nsh):
            z_ref[:, k * N:(k + 1) * N] = _dot(xn, w_ref[k])

    return pl.pallas_call(
        body, name="proj_in_fwd", grid=(nt,),
        in_specs=[pl.BlockSpec((tm, D), lambda i: (i, 0)), _full((1, D)), _full((nsh, D, N))],
        out_specs=pl.BlockSpec((tm, nsh * N), lambda i: (i, 0)),
        out_shape=_sds((T, nsh * N), f32),
        compiler_params=_cparams(1),
    )(h, gain, w_in)


def _proj_in_bwd_dgrad(h, gain, dres, dz, w_in, tm):
    T, D = h.shape
    nsh, N = w_in.shape[0], w_in.shape[2]
    nt = T // tm

    def body(h_ref, gain_ref, dres_ref, dz_ref, w_ref, dh_ref, dgain_ref, xn_ref):
        @pl.when(pl.program_id(0) == 0)
        def _():
            dgain_ref[...] = jnp.zeros_like(dgain_ref)

        dxn = _dot_nt(dz_ref[:, 0:N], w_ref[0])
        for k in range(1, nsh):
            dxn = dxn + _dot_nt(dz_ref[:, k * N:(k + 1) * N], w_ref[k])
        hv = h_ref[...]
        r = lax.rsqrt(jnp.mean(hv * hv, axis=-1, keepdims=True) + EPS)
        xh = hv * r
        xn_ref[...] = (xh * gain_ref[...]).astype(bf16)
        dgain_ref[...] += jnp.sum(dxn * xh, axis=0, keepdims=True)
        dxh = dxn * gain_ref[...]
        dh_ref[...] = dres_ref[...] + r * (dxh - xh * jnp.mean(dxh * xh, axis=-1, keepdims=True))

    tok = pl.BlockSpec((tm, D), lambda i: (i, 0))
    return pl.pallas_call(
        body, name="proj_in_bwd_dgrad", grid=(nt,),
        in_specs=[tok, _full((1, D)), tok, pl.BlockSpec((tm, nsh * N), lambda i: (i, 0)), _full((nsh, D, N))],
        out_specs=(tok, _full((1, D)), tok),
        out_shape=(_sds((T, D), f32), _sds((1, D), f32), _sds((T, D), bf16)),
        compiler_params=_cparams(1),
    )(h, gain, dres, dz, w_in)


def _proj_in_bwd_wgrad(xn, dz, nsh):
    T, D = xn.shape
    N = dz.shape[1] // nsh

    def body(xn_ref, dz_ref, dw_ref):
        dw_ref[...] = _dot_tn(xn_ref[...], dz_ref[...]).astype(bf16)

    return pl.pallas_call(
        body, name="proj_in_bwd_wgrad", grid=(nsh,),
        in_specs=[_full((T, D)), pl.BlockSpec((T, N), lambda k: (0, k))],
        out_specs=pl.BlockSpec((None, D, N), lambda k: (k, 0, 0)),
        out_shape=_sds((nsh, D, N), bf16),
        compiler_params=_cparams(1),
    )(xn, dz)


def _proj_out_fwd(h, oa, ob, oc, w_out, tm):
    T, D = h.shape
    nsh, R = w_out.shape[0], w_out.shape[1]
    da, db = oa.shape[1], ob.shape[1]
    nt = T // tm

    def body(h_ref, oa_ref, ob_ref, oc_ref, w_ref, out_ref):
        w = w_ref[...].reshape(nsh * R, D)
        out_ref[...] = (h_ref[...] + _dot(oa_ref[...], w[:da]) + _dot(ob_ref[...], w[da:da + db])
                        + _dot(oc_ref[...], w[da + db:]))

    def tok(n):
        return pl.BlockSpec((tm, n), lambda i: (i, 0))

    return pl.pallas_call(
        body, name="proj_out_fwd", grid=(nt,),
        in_specs=[tok(D), tok(da), tok(db), tok(oc.shape[1]), _full((nsh, R, D))],
        out_specs=tok(D), out_shape=_sds((T, D), f32),
        compiler_params=_cparams(1),
    )(h, oa, ob, oc, w_out)


def _proj_out_bwd(dh, oa, ob, oc, w_out, tm):
    T, D = dh.shape
    nsh, R = w_out.shape[0], w_out.shape[1]
    da, db, dc = oa.shape[1], ob.shape[1], oc.shape[1]
    nt = T // tm

    def body(dh_ref, oa_ref, ob_ref, oc_ref, w_ref, doa_ref, dob_ref, doc_ref, dw_ref, acc_ref):
        i = pl.program_id(0)

        @pl.when(i == 0)
        def _():
            acc_ref[...] = jnp.zeros_like(acc_ref)

        d = dh_ref[...].astype(bf16)
        w = w_ref[...].reshape(nsh * R, D)
        dm = _dot_nt(d, w)
        doa_ref[...] = dm[:, :da]
        dob_ref[...] = dm[:, da:da + db]
        doc_ref[...] = dm[:, da + db:]
        acc_ref[pl.ds(0, da), :] += _dot_tn(oa_ref[...], d)
        acc_ref[pl.ds(da, db), :] += _dot_tn(ob_ref[...], d)
        acc_ref[pl.ds(da + db, dc), :] += _dot_tn(oc_ref[...], d)

        @pl.when(i == nt - 1)
        def _():
            dw_ref[...] = acc_ref[...].astype(bf16).reshape(nsh, R, D)

    def tok(n):
        return pl.BlockSpec((tm, n), lambda i: (i, 0))

    wspec = _full((nsh, R, D))
    return pl.pallas_call(
        body, name="proj_out_bwd", grid=(nt,),
        in_specs=[tok(D), tok(da), tok(db), tok(dc), wspec],
        out_specs=(tok(da), tok(db), tok(dc), wspec),
        out_shape=(_sds((T, da), f32), _sds((T, db), f32), _sds((T, dc), f32), _sds((nsh, R, D), bf16)),
        scratch_shapes=[pltpu.VMEM((nsh * R, D), f32)],
        compiler_params=_cparams(1),
    )(dh, oa, ob, oc, w_out)


def _head_sum(m, n_heads):
    parts = []
    for hd in range(n_heads):
        s = jnp.sum(m[:, hd * HEAD:(hd + 1) * HEAD], axis=-1, keepdims=True)
        parts.append(jnp.broadcast_to(s, (m.shape[0], HEAD)))
    return parts[0] if n_heads == 1 else jnp.concatenate(parts, axis=1)


def _cat(parts, axis):
    return parts[0] if len(parts) == 1 else jnp.concatenate(parts, axis=axis)


def _hgrn_block(q, fl, iv, lb, states, tri, n_heads, n_inner):
    C = q.shape[0] // n_inner
    qs = _silu(q)
    forget = lb + (1.0 - lb) * jax.nn.sigmoid(fl)
    kk = 1.0 - forget
    logf = jnp.log(forget)
    b = jnp.dot(tri, logf, precision=HI, preferred_element_type=f32)
    vb = iv.astype(bf16)
    heads = [slice(hd * HEAD, (hd + 1) * HEAD) for hd in range(n_heads)]
    n_sub = C // A_SUB

    off, qe, kd, dec = {}, [], [], []
    for j in range(n_inner):
        c0 = j * C
        for blk in range(1, n_sub):
            lo = c0 + blk * A_SUB
            piv = b[lo:lo + 1]
            qt = (qs[lo:lo + A_SUB] * jnp.exp(b[lo:lo + A_SUB] - piv)).astype(bf16)
            kt = (kk[c0:lo] * jnp.exp(piv - b[c0:lo])).astype(bf16)
            parts = []
            for sl in heads:
                sc = _dot_nt(qt[:, sl], kt[:, sl])
                parts.append(_dot(sc.astype(bf16), vb[c0:lo, sl]))
            off[(j, blk)] = _cat(parts, 1)
        bj = b[c0:c0 + C]
        b_end = bj[C - 1:C]
        qe.append((qs[c0:c0 + C] * jnp.exp(bj)).astype(bf16))
        kd.append((kk[c0:c0 + C] * jnp.exp(b_end - bj)).astype(bf16))
        dec.append(jnp.exp(b_end))

    outs = []
    for j in range(n_inner):
        for blk in range(n_sub):
            lo = j * C + blk * A_SUB
            groups = [off[(j, blk)][r0:r0 + SUBLANES] if blk > 0 else None for r0 in range(0, A_SUB, SUBLANES)]
            for s in range(A_SUB):
                first = (s // SUBLANES) * SUBLANES
                n_rows = A_SUB - first
                row = lax.broadcasted_iota(jnp.int32, (n_rows, 1), 0) + first
                gate = jnp.where(row >= s, 0.0, -1e30)
                r = slice(lo + first, lo + A_SUB)
                m = qs[r] * jnp.exp((b[r] - b[lo + s:lo + s + 1]) + gate) * kk[lo + s:lo + s + 1]
                term = _head_sum(m, n_heads) * iv[lo + s:lo + s + 1]
                for gi in range(first // SUBLANES, A_SUB // SUBLANES):
                    piece = term[gi * SUBLANES - first:(gi + 1) * SUBLANES - first]
                    groups[gi] = piece if groups[gi] is None else groups[gi] + piece
            outs.extend(groups)
    o = jnp.concatenate(outs, axis=0)

    inter = []
    states = list(states)
    for j in range(n_inner):
        c0 = j * C
        parts = []
        for hd, sl in enumerate(heads):
            st = states[hd]
            parts.append(_dot_nt(qe[j][:, sl], st.astype(bf16)))
            states[hd] = dec[j][:, sl] * st + _dot_tn(vb[c0:c0 + C, sl], kd[j][:, sl])
        inter.append(_cat(parts, 1))
    return o + _cat(inter, 0), tuple(states)


def _hgrn_gate(o, g, gain, n_heads):
    ms = _head_sum(o * o, n_heads) * (1.0 / HEAD)
    return o * lax.rsqrt(ms + EPS) * gain * _silu(g)


def _tri_matrix(c, n_inner):
    idx = np.arange(c * n_inner)
    same = (idx[:, None] // c) == (idx[None, :] // c)
    return jnp.asarray((same & (idx[:, None] >= idx[None, :])).astype(np.float32))


def _hgrn_fwd(z, lb, gain, d_a):
    T = z.shape[0]
    C = A_CHUNK * A_INNER
    nc = T // C
    nh = d_a // HEAD
    tri = _tri_matrix(A_CHUNK, A_INNER)

    def body(q_ref, f_ref, i_ref, g_ref, lb_ref, gain_ref, tri_ref, out_ref, o_ref, st_ref, carry_ref):
        @pl.when(pl.program_id(0) == 0)
        def _():
            carry_ref[...] = jnp.zeros_like(carry_ref)

        states = tuple(carry_ref[hd] for hd in range(nh))
        st_ref[...] = carry_ref[...]
        o, new_states = _hgrn_block(q_ref[...], f_ref[...], i_ref[...], lb_ref[...], states, tri_ref[...], nh, A_INNER)
        o_ref[...] = o
        out_ref[...] = _hgrn_gate(o, g_ref[...], gain_ref[...], nh).astype(bf16)
        for hd in range(nh):
            carry_ref[hd] = new_states[hd]

    def col(j):
        return pl.BlockSpec((C, d_a), lambda c, j=j: (c, j))

    tok = pl.BlockSpec((C, d_a), lambda c: (c, 0))
    return pl.pallas_call(
        body, name="hgrn_fwd", grid=(nc,),
        in_specs=[col(0), col(1), col(2), col(3), _full((1, d_a)), _full((1, d_a)), _full((C, C))],
        out_specs=(tok, tok, pl.BlockSpec((None, nh, HEAD, HEAD), lambda c: (c, 0, 0, 0))),
        out_shape=(_sds((T, d_a), bf16), _sds((T, d_a), f32), _sds((nc, nh, HEAD, HEAD), f32)),
        scratch_shapes=[pltpu.VMEM((nh, HEAD, HEAD), f32)],
        compiler_params=_cparams(1),
    )(z, z, z, z, lb, gain, tri)


def _hgrn_bwd(z, lb, gain, o_pre, states, dout, d_a):
    T = z.shape[0]
    C = A_CHUNK * A_INNER
    nc = T // C
    nh = d_a // HEAD
    tri = _tri_matrix(A_CHUNK, A_INNER)

    def body(q_ref, f_ref, i_ref, g_ref, lb_ref, gain_ref, tri_ref, o_ref, st_ref, do_ref,
             dz_ref, dlb_ref, dgain_ref, carry_ref):
        @pl.when(pl.program_id(0) == 0)
        def _():
            carry_ref[...] = jnp.zeros_like(carry_ref)
            dlb_ref[...] = jnp.zeros_like(dlb_ref)
            dgain_ref[...] = jnp.zeros_like(dgain_ref)

        _, vjp_gate = jax.vjp(lambda o, g, gv: _hgrn_gate(o, g, gv, nh), o_ref[...], g_ref[...], gain_ref[...])
        d_o, dg, dgain = vjp_gate(do_ref[...])
        tri_v = tri_ref[...]

        def fn(q, fl, iv, lbv, sts):
            return _hgrn_block(q, fl, iv, lbv, sts, tri_v, nh, A_INNER)

        states_in = tuple(st_ref[hd] for hd in range(nh))
        _, vjp = jax.vjp(fn, q_ref[...], f_ref[...], i_ref[...], lb_ref[...], states_in)
        dstates = tuple(carry_ref[hd] for hd in range(nh))
        dq, df, di, dlb, dst = vjp((d_o, dstates))
        dz_ref[:, 0:d_a] = dq.astype(bf16)
        dz_ref[:, d_a:2 * d_a] = df.astype(bf16)
        dz_ref[:, 2 * d_a:3 * d_a] = di.astype(bf16)
        dz_ref[:, 3 * d_a:4 * d_a] = dg.astype(bf16)
        dlb_ref[...] += dlb
        dgain_ref[...] += dgain
        for hd in range(nh):
            carry_ref[hd] = dst[hd]

    def col(j):
        return pl.BlockSpec((C, d_a), lambda c, j=j: (nc - 1 - c, j))

    tok = pl.BlockSpec((C, d_a), lambda c: (nc - 1 - c, 0))
    return pl.pallas_call(
        body, name="hgrn_bwd", grid=(nc,),
        in_specs=[col(0), col(1), col(2), col(3), _full((1, d_a)), _full((1, d_a)), _full((C, C)), tok,
                  pl.BlockSpec((None, nh, HEAD, HEAD), lambda c: (nc - 1 - c, 0, 0, 0)), tok],
        out_specs=(pl.BlockSpec((C, 4 * d_a), lambda c: (nc - 1 - c, 0)), _full((1, d_a)), _full((1, d_a))),
        out_shape=(_sds(z.shape, bf16), _sds((1, d_a), f32), _sds((1, d_a), f32)),
        scratch_shapes=[pltpu.VMEM((nh, HEAD, HEAD), f32)],
        compiler_params=_cparams(1),
    )(z, z, z, z, lb, gain, tri, o_pre, states, dout)


def _one_minus_exp(x):
    series = -x * (1.0 + x * (0.5 + x * (1.0 / 6.0 + x * (1.0 / 24.0))))
    return jnp.where(x > -0.03, series, 1.0 - jnp.exp(x))


def _lru_pre(xc, wa, ba, wx, bx, lam):
    xb16 = xc.astype(bf16)
    r = jax.nn.sigmoid(_dot(xb16, wa.astype(bf16)) + ba)
    gi = jax.nn.sigmoid(_dot(xb16, wx.astype(bf16)) + bx)
    log_a = -LRU_C * r * jax.nn.softplus(-lam)
    a = jnp.exp(log_a)
    mult = jnp.sqrt(_one_minus_exp(2.0 * log_a))
    return a, mult * gi * xc


def _lru_post(h, gate, gain, avg):
    y = h * jax.nn.gelu(gate)
    ms = _group_mean(y * y, avg)
    return y * lax.rsqrt(ms + EPS) * gain


def _shift_down(x, d, prev):
    row = lax.broadcasted_iota(jnp.int32, x.shape, 0)
    return jnp.where(row >= d, pltpu.roll(x, d, 0), pltpu.roll(prev, d, 0))


def _shift_up(x, d, nxt):
    n = x.shape[0]
    row = lax.broadcasted_iota(jnp.int32, x.shape, 0)
    return jnp.where(row < n - d, pltpu.roll(x, n - d, 0), pltpu.roll(nxt, n - d, 0))


def _scan_rows(a, u, reverse):
    n = a.shape[0]
    row = lax.broadcasted_iota(jnp.int32, a.shape, 0)
    d = 1
    while d < n:
        shift, ok = (n - d, row < n - d) if reverse else (d, row >= d)
        su = jnp.where(ok, pltpu.roll(u, shift, 0), 0.0)
        sa = jnp.where(ok, pltpu.roll(a, shift, 0), 1.0)
        u = u + a * su
        a = a * sa
        d *= 2
    return a, u


def _conv(xb, xprev, cw, cb):
    xc = cb + cw[CONV_WIDTH - 1:CONV_WIDTH] * xb
    for d in range(1, CONV_WIDTH):
        xc = xc + cw[CONV_WIDTH - 1 - d:CONV_WIDTH - d] * _shift_down(xb, d, xprev)
    return xc


def _lru_fwd(z, col0, d_b, cw, cb, wa, ba, wx, bx, lam, gain, avg):
    T = z.shape[0]
    R = min(B_CHUNK, T)
    nr = T // R
    jb = col0 // d_b

    def body(xb_ref, gate_ref, cw_ref, cb_ref, wa_ref, ba_ref, wx_ref, bx_ref, lam_ref, gain_ref, avg_ref,
             out_ref, h_ref, xprev_ref, hprev_ref):
        @pl.when(pl.program_id(0) == 0)
        def _():
            xprev_ref[...] = jnp.zeros_like(xprev_ref)
            hprev_ref[...] = jnp.zeros_like(hprev_ref)

        xb = xb_ref[...]
        xc = _conv(xb, xprev_ref[...], cw_ref[...], cb_ref[...])
        a, u = _lru_pre(xc, wa_ref[...], ba_ref[...], wx_ref[...], bx_ref[...], lam_ref[...])
        acum, hl = _scan_rows(a, u, False)
        h = hl + acum * hprev_ref[R - 1:R, :]
        h_ref[...] = h
        out_ref[...] = _lru_post(h, gate_ref[...], gain_ref[...], avg_ref[...]).astype(bf16)
        xprev_ref[...] = xb
        hprev_ref[...] = h

    vec = _full((1, d_b))
    return pl.pallas_call(
        body, name="lru_fwd", grid=(nr,),
        in_specs=[pl.BlockSpec((R, d_b), lambda i: (i, jb)), pl.BlockSpec((R, d_b), lambda i: (i, jb + 1)),
                  _full((CONV_WIDTH, d_b)), vec, _full((d_b, d_b)), vec, _full((d_b, d_b)), vec, vec, vec, _full((d_b, d_b))],
        out_specs=(pl.BlockSpec((R, d_b), lambda i: (i, 0)), pl.BlockSpec((R, d_b), lambda i: (i, 0))),
        out_shape=(_sds((T, d_b), bf16), _sds((T, d_b), f32)),
        scratch_shapes=[pltpu.VMEM((R, d_b), f32), pltpu.VMEM((R, d_b), f32)],
        compiler_params=_cparams(1),
    )(z, z, cw, cb, wa, ba, wx, bx, lam, gain, avg)


def _lru_bwd(z, col0, d_b, hsave, dout, dz_buf, cw, cb, wa, ba, wx, bx, lam, gain, avg):
    T = z.shape[0]
    R = min(B_CHUNK, T)
    nr = T // R
    jb = col0 // d_b

    def body(xb_ref, xp_ref, gate_ref, h_ref, hp_ref, do_ref,
             cw_ref, cb_ref, wa_ref, ba_ref, wx_ref, bx_ref, lam_ref, gain_ref, avg_ref, dzin_ref,
             dz_ref, dcw_ref, dcb_ref, dwa_ref, dba_ref, dwx_ref, dbx_ref, dlam_ref, dgain_ref,
             gfirst_ref, afirst_ref, dxcn_ref):
        step = pl.program_id(0)
        first_in_time = step == nr - 1

        @pl.when(step == 0)
        def _():
            for r in (dcw_ref, dcb_ref, dwa_ref, dba_ref, dwx_ref, dbx_ref, dlam_ref, dgain_ref,
                      gfirst_ref, afirst_ref, dxcn_ref):
                r[...] = jnp.zeros_like(r)

        xb = xb_ref[...]
        keep = jnp.where(first_in_time, 0.0, 1.0)
        xprev = xp_ref[...] * keep
        hprev = hp_ref[...] * keep
        cw = cw_ref[...]
        xc = _conv(xb, xprev, cw, cb_ref[...])
        (a, _), vjp_pre = jax.vjp(_lru_pre, xc, wa_ref[...], ba_ref[...], wx_ref[...], bx_ref[...], lam_ref[...])
        h = h_ref[...]
        avg = avg_ref[...]
        _, vjp_post = jax.vjp(lambda hh, gg, gn: _lru_post(hh, gg, gn, avg), h, gate_ref[...], gain_ref[...])
        dh, dgate, dgain = vjp_post(do_ref[...])
        a_next = _shift_up(a, 1, jnp.broadcast_to(afirst_ref[0:1, :], a.shape))
        acum, gl = _scan_rows(a_next, dh, True)
        gtot = gl + acum * gfirst_ref[0:1, :]
        da = gtot * _shift_down(h, 1, hprev)
        dxc, dwa, dba, dwx, dbx, dlam = vjp_pre((da, gtot))
        dxcn = dxcn_ref[...]
        dxb = cw[CONV_WIDTH - 1:CONV_WIDTH] * dxc
        dcw_ref[CONV_WIDTH - 1:CONV_WIDTH, :] += jnp.sum(dxc * xb, axis=0, keepdims=True)
        for d in range(1, CONV_WIDTH):
            tap = CONV_WIDTH - 1 - d
            dxb = dxb + cw[tap:tap + 1] * _shift_up(dxc, d, dxcn)
            dcw_ref[tap:tap + 1, :] += jnp.sum(dxc * _shift_down(xb, d, xprev), axis=0, keepdims=True)
        dz_ref[:, 0:d_b] = dxb.astype(bf16)
        dz_ref[:, d_b:2 * d_b] = dgate.astype(bf16)
        dcb_ref[...] += jnp.sum(dxc, axis=0, keepdims=True)
        dwa_ref[...] += dwa
        dba_ref[...] += dba
        dwx_ref[...] += dwx
        dbx_ref[...] += dbx
        dlam_ref[...] += dlam
        dgain_ref[...] += dgain
        gfirst_ref[...] = jnp.broadcast_to(gtot[0:1, :], gfirst_ref.shape)
        afirst_ref[...] = jnp.broadcast_to(a[0:1, :], afirst_ref.shape)
        dxcn_ref[...] = dxc

    vec = _full((1, d_b))
    mat = _full((d_b, d_b))

    def cur(j):
        return pl.BlockSpec((R, d_b), lambda i, j=j: (nr - 1 - i, j))

    def prev(j):
        return pl.BlockSpec((R, d_b), lambda i, j=j: (jnp.maximum(nr - 2 - i, 0), j))

    return pl.pallas_call(
        body, name="lru_bwd", grid=(nr,),
        in_specs=[cur(jb), prev(jb), cur(jb + 1), cur(0), prev(0), cur(0),
                  _full((CONV_WIDTH, d_b)), vec, mat, vec, mat, vec, vec, vec, mat, ANY_SPEC],
        out_specs=(pl.BlockSpec((R, 2 * d_b), lambda i: (nr - 1 - i, col0 // (2 * d_b))), _full((CONV_WIDTH, d_b)), vec, mat, vec, mat, vec, vec, vec),
        out_shape=(_sds(dz_buf.shape, bf16), _sds((CONV_WIDTH, d_b), f32), _sds((1, d_b), f32), _sds((d_b, d_b), f32),
                   _sds((1, d_b), f32), _sds((d_b, d_b), f32), _sds((1, d_b), f32), _sds((1, d_b), f32), _sds((1, d_b), f32)),
        scratch_shapes=[pltpu.VMEM((8, d_b), f32), pltpu.VMEM((8, d_b), f32), pltpu.VMEM((R, d_b), f32)],
        input_output_aliases={15: 0},
        compiler_params=_cparams(1),
    )(z, z, z, hsave, hsave, dout, cw, cb, wa, ba, wx, bx, lam, gain, avg, dz_buf)


def _two_pass(x, m16):
    hi = x.astype(bf16)
    lo = (x - hi.astype(f32)).astype(bf16)
    return _dot(hi, m16) + _dot(lo, m16)


@jax.custom_vjp
def _group_mean(x, avg):
    return _two_pass(x, avg.astype(bf16))


def _group_mean_fwd(x, avg):
    return _group_mean(x, avg), avg


def _group_mean_bwd(avg, ct):
    return _two_pass(ct, avg.astype(bf16)), jnp.zeros_like(avg)


_group_mean.defvjp(_group_mean_fwd, _group_mean_bwd)


def _sgu_chunk(u_in, v_in, w, bexp, gain, avg, n_groups):
    C, d_c = u_in.shape
    gd = d_c // n_groups
    u = jax.nn.gelu(u_in)
    v = jax.nn.gelu(v_in)
    mu = _group_mean(v, avg)
    vc = v - mu
    var = _group_mean(vc * vc, avg)
    vh = (vc * lax.rsqrt(var + EPS)).astype(bf16)
    lane = lax.broadcasted_iota(jnp.int32, (1, d_c), 1)
    causal = lax.broadcasted_iota(jnp.int32, (C, C), 0) >= lax.broadcasted_iota(jnp.int32, (C, C), 1)
    zz = bexp
    for g in range(n_groups):
        wg = jnp.where(causal, w[g], 0.0).astype(bf16)
        zz = zz + jnp.where((lane >= g * gd) & (lane < (g + 1) * gd), _dot(wg, vh), 0.0)
    y = u * zz
    ms = _group_mean(y * y, avg)
    return y * lax.rsqrt(ms + EPS) * gain


def _sgu_inner(T):
    return C_INNER if T % (C_CHUNK * C_INNER) == 0 else 1


def _sgu_fwd(z, col0, d_c, w, bexp, gain, avg):
    T = z.shape[0]
    C = C_CHUNK
    n_in = _sgu_inner(T)
    R = C * n_in
    jb = col0 // d_c
    G = w.shape[0]

    def body(u_ref, v_ref, w_ref, b_ref, gain_ref, avg_ref, out_ref):
        w_v, b_v, gain_v, avg = w_ref[...], b_ref[...], gain_ref[...], avg_ref[...]
        for j in range(n_in):
            rows = pl.ds(j * C, C)
            out_ref[rows, :] = _sgu_chunk(u_ref[rows, :], v_ref[rows, :], w_v, b_v, gain_v, avg, G).astype(bf16)

    return pl.pallas_call(
        body, name="sgu_fwd", grid=(T // R,),
        in_specs=[pl.BlockSpec((R, d_c), lambda i: (i, jb)), pl.BlockSpec((R, d_c), lambda i: (i, jb + 1)),
                  _full((G, C, C)), _full((C, d_c)), _full((1, d_c)), _full((d_c, d_c))],
        out_specs=pl.BlockSpec((R, d_c), lambda i: (i, 0)),
        out_shape=_sds((T, d_c), bf16),
        compiler_params=_cparams(1),
    )(z, z, w, bexp, gain, avg)


def _sgu_bwd(z, col0, d_c, dout, dz_buf, w, bexp, gain, avg):
    T = z.shape[0]
    C = C_CHUNK
    n_in = _sgu_inner(T)
    R = C * n_in
    nc = T // R
    jb = col0 // d_c
    G = w.shape[0]
    gd = d_c // G

    def body(u_ref, v_ref, do_ref, w_ref, b_ref, gain_ref, avg_ref, dzin_ref, dz_ref, dw_ref, db_ref, dgain_ref, dbexp_ref):
        step = pl.program_id(0)

        @pl.when(step == 0)
        def _():
            dw_ref[...] = jnp.zeros_like(dw_ref)
            dgain_ref[...] = jnp.zeros_like(dgain_ref)
            dbexp_ref[...] = jnp.zeros_like(dbexp_ref)

        avg, w_v, b_v, gain_v = avg_ref[...], w_ref[...], b_ref[...], gain_ref[...]
        dw = dbexp = dgain = None
        for j in range(n_in):
            rows = pl.ds(j * C, C)
            _, vjp = jax.vjp(lambda a, b, c, d, e: _sgu_chunk(a, b, c, d, e, avg, G),
                             u_ref[rows, :], v_ref[rows, :], w_v, b_v, gain_v)
            du, dv, dw_j, dbexp_j, dgain_j = vjp(do_ref[rows, :])
            dz_ref[rows, 0:d_c] = du.astype(bf16)
            dz_ref[rows, d_c:2 * d_c] = dv.astype(bf16)
            dw = dw_j if dw is None else dw + dw_j
            dbexp = dbexp_j if dbexp is None else dbexp + dbexp_j
            dgain = dgain_j if dgain is None else dgain + dgain_j
        dw_ref[...] += dw
        dbexp_ref[...] += dbexp
        dgain_ref[...] += dgain

        @pl.when(step == nc - 1)
        def _():
            lane = lax.broadcasted_iota(jnp.int32, (1, d_c), 1)
            acc = dbexp_ref[...]
            for g in range(G):
                sel = jnp.where((lane >= g * gd) & (lane < (g + 1) * gd), acc, 0.0)
                db_ref[:, g:g + 1] = jnp.sum(sel, axis=1, keepdims=True)

    return pl.pallas_call(
        body, name="sgu_bwd", grid=(nc,),
        in_specs=[pl.BlockSpec((R, d_c), lambda i: (i, jb)), pl.BlockSpec((R, d_c), lambda i: (i, jb + 1)),
                  pl.BlockSpec((R, d_c), lambda i: (i, 0)),
                  _full((G, C, C)), _full((C, d_c)), _full((1, d_c)), _full((d_c, d_c)), ANY_SPEC],
        out_specs=(pl.BlockSpec((R, 2 * d_c), lambda i: (i, col0 // (2 * d_c))), _full((G, C, C)), _full((C, G)), _full((1, d_c))),
        out_shape=(_sds(dz_buf.shape, bf16), _sds((G, C, C), f32), _sds((C, G), f32), _sds((1, d_c), f32)),
        scratch_shapes=[pltpu.VMEM((C, d_c), f32)],
        input_output_aliases={7: 0},
        compiler_params=_cparams(1),
    )(z, z, dout, w, bexp, gain, avg, dz_buf)


def _loss_head(h, gain, target, tm):
    T, D = h.shape
    nt = T // tm

    def body(h_ref, gain_ref, tgt_ref, dh_ref, loss_ref, dgain_ref):
        @pl.when(pl.program_id(0) == 0)
        def _():
            loss_ref[...] = jnp.zeros_like(loss_ref)
            dgain_ref[...] = jnp.zeros_like(dgain_ref)

        hv = h_ref[...]
        gain_v = gain_ref[...]
        r = lax.rsqrt(jnp.mean(hv * hv, axis=-1, keepdims=True) + EPS)
        xh = hv * r
        e = xh * gain_v - tgt_ref[...]
        loss_ref[...] += 0.5 * jnp.sum(jnp.mean(e * e, axis=-1, keepdims=True), axis=0, keepdims=True)
        dy = e * (1.0 / D)
        dgain_ref[...] += jnp.sum(dy * xh, axis=0, keepdims=True)
        dxh = dy * gain_v
        dh_ref[...] = r * (dxh - xh * jnp.mean(dxh * xh, axis=-1, keepdims=True))

    tok = pl.BlockSpec((tm, D), lambda i: (i, 0))
    return pl.pallas_call(
        body, name="loss_head", grid=(nt,),
        in_specs=[tok, _full((1, D)), tok],
        out_specs=(tok, _full((1, 128)), _full((1, D))),
        out_shape=(_sds((T, D), f32), _sds((1, 128), f32), _sds((1, D), f32)),
        compiler_params=_cparams(1),
    )(h, gain, target)


def _lower_bounds_fn(logits):
    n = logits.shape[0]
    mx = jnp.max(logits, axis=0, keepdims=True)
    ex = jnp.exp(logits - mx)
    soft = ex / jnp.sum(ex, axis=0, keepdims=True)
    rows = [jnp.zeros_like(soft[0:1])]
    for l in range(1, n):
        rows.append(rows[-1] + soft[l:l + 1])
    return jnp.concatenate(rows, axis=0)


def _lower_bounds(logits):
    def body(x_ref, o_ref):
        o_ref[...] = _lower_bounds_fn(x_ref[...])

    return pl.pallas_call(body, name="lower_bounds", out_shape=_sds(logits.shape, f32))(logits)


def _lower_bounds_bwd(logits, dlb):
    def body(x_ref, d_ref, o_ref):
        _, vjp = jax.vjp(_lower_bounds_fn, x_ref[...])
        o_ref[...] = vjp(d_ref[...])[0]

    return pl.pallas_call(body, name="lower_bounds_bwd", out_shape=_sds(logits.shape, f32))(logits, dlb)


def _adamw(w, g, m, v, rows_blk, row_range=None, prev=(), after=None):
    R, Cc = w.shape
    lo, hi = (0, R) if row_range is None else row_range
    span = hi - lo
    rb = span if (span <= rows_blk and lo % span == 0) else math.gcd(math.gcd(span, lo), rows_blk)
    extra = list(prev) + ([] if after is None else [after])

    def body(w_ref, g_ref, m_ref, v_ref, *rest):
        d_ref, nm_ref, nv_ref, go_ref = rest[len(extra):]
        gv = g_ref[...]
        m2 = ADAM_B1 * m_ref[...] + (1.0 - ADAM_B1) * gv
        v2 = ADAM_B2 * v_ref[...] + (1.0 - ADAM_B2) * (gv * gv)
        m_hat = m2 / (1.0 - ADAM_B1 ** ADAM_STEP)
        v_hat = v2 / (1.0 - ADAM_B2 ** ADAM_STEP)
        d_ref[...] = -ADAM_LR * (m_hat / (jnp.sqrt(v_hat) + ADAM_EPS) + ADAM_WD * w_ref[...])
        nm_ref[...] = m2
        nv_ref[...] = v2
        go_ref[...] = gv

    first = lo // rb
    spec = pl.BlockSpec((rb, Cc), lambda i: (i + first, 0))
    return pl.pallas_call(
        body, name="adamw", grid=((hi - lo) // rb,),
        in_specs=[spec] * 4 + [ANY_SPEC] * len(extra), out_specs=(spec,) * 4, out_shape=(_sds((R, Cc), f32),) * 4,
        input_output_aliases={4 + j: j for j in range(len(prev))},
        compiler_params=_cparams(1),
    )(w, g, m, v, *extra)


def _pair_sum(grads, recv, c_arr):
    n = len(grads)
    nsh = grads[0].shape[0]

    def body(c_ref, *refs):
        for a in range(n):
            refs[2 * n + a][...] = (refs[a][...].astype(f32) + refs[n + a][...].astype(f32)).astype(bf16)

    g_specs, r_specs, out_shape = [], [], []
    for g in grads:
        _, R, Cc = g.shape
        r2 = R // 2
        g_specs.append(pl.BlockSpec((None, r2, Cc), lambda s, c: (s, c[0], 0)))
        r_specs.append(pl.BlockSpec((None, r2, Cc), lambda s, c: (s, 0, 0)))
        out_shape.append(_sds((nsh, r2, Cc), bf16))
    gs = pltpu.PrefetchScalarGridSpec(num_scalar_prefetch=1, grid=(nsh,), in_specs=g_specs + r_specs, out_specs=tuple(r_specs))
    return list(pl.pallas_call(body, name="pair_sum", grid_spec=gs, out_shape=tuple(out_shape),
                               compiler_params=_cparams(1))(c_arr, *grads, *recv))


def _add(a, b):
    def body(a_ref, b_ref, o_ref):
        o_ref[...] = a_ref[...] + b_ref[...]

    return pl.pallas_call(body, name="pair_sum_small", out_shape=_sds(a.shape, f32))(a, b)


def _chip_sum(hsum, recv, bufs, slot_arr, c_arr, layer, n_layers):
    n = len(hsum)
    prev = list(bufs)
    steps = 2

    def body(s_ref, c_ref, *refs):
        outs = refs[len(refs) - n:]
        for a in range(n):
            acc = refs[a][...].astype(f32)
            for j in range(N_CHIPS - 1):
                acc = acc + refs[n + a][j].astype(f32)
            outs[a][...] = acc

    h_specs, r_specs, o_specs, out_shape = [], [], [], []
    for hh in hsum:
        _, r2, Cc = hh.shape
        rt = r2 // steps
        h_specs.append(pl.BlockSpec((None, rt, Cc), lambda i, s, c: (s[0], i, 0)))
        r_specs.append(pl.BlockSpec((N_CHIPS - 1, rt, Cc), lambda i, s, c: (0, i, 0)))
        o_specs.append(pl.BlockSpec((None, rt, Cc), lambda i, s, c: (layer, c[0] * steps + i, 0)))
        out_shape.append(_sds((n_layers, 2 * r2, Cc), f32))
    gs = pltpu.PrefetchScalarGridSpec(num_scalar_prefetch=2, grid=(steps,),
                                      in_specs=h_specs + r_specs + [ANY_SPEC] * len(prev), out_specs=tuple(o_specs))
    return list(pl.pallas_call(body, name="chip_sum", grid_spec=gs, out_shape=tuple(out_shape),
                               input_output_aliases={2 + 2 * n + a: a for a in range(len(prev))},
                               compiler_params=_cparams(1))(slot_arr, c_arr, *hsum, *recv, *prev))


def _sum_slots(x):
    def body(x_ref, o_ref):
        acc = x_ref[0]
        for j in range(1, x.shape[0]):
            acc = acc + x_ref[j]
        o_ref[...] = acc

    return pl.pallas_call(body, name="sum_slots", out_shape=_sds(x.shape[1:], f32))(x)


def _blockdiag(w):
    nb, bd, _ = w.shape
    eye = jnp.eye(nb, dtype=w.dtype)
    return (eye[:, None, :, None] * w[:, :, None, :]).reshape(nb * bd, nb * bd)


def _blockdiag_extract(dense, nb):
    bd = dense.shape[0] // nb
    d4 = dense.reshape(nb, bd, nb, bd)
    return jnp.stack([d4[i, :, i, :] for i in range(nb)])


def _pack(arrays, multiple):
    flat = jnp.concatenate([a.reshape(-1).astype(f32) for a in arrays])
    pad = (-flat.shape[0]) % multiple
    return jnp.pad(flat, (0, pad))


def _unpack(flat, shapes):
    out, off = [], 0
    for s in shapes:
        n = int(np.prod(s))
        out.append(flat[off:off + n].reshape(s))
        off += n
    return out


BIG = ("ffn1_wg", "ffn1_wu", "ffn1_wd", "w_in", "w_out", "ffn2_wg", "ffn2_wu", "ffn2_wd")
TRANSPOSED = ("ffn1_wg", "ffn1_wu", "ffn2_wg", "ffn2_wu")
SMALL = ("ffn1_norm", "mix_norm", "hgrn_lb_logits", "hgrn_norm", "conv_w", "conv_b", "lru_wa", "lru_ba", "lru_wx",
         "lru_bx", "lru_lambda", "lru_norm", "sgu_w", "sgu_b", "sgu_norm", "ffn2_norm", "final_norm")
WEIGHTS = ("ffn1_norm", "ffn1_wg", "ffn1_wu", "ffn1_wd", "mix_norm", "w_in", "hgrn_lb_logits", "hgrn_norm", "conv_w",
           "conv_b", "lru_wa", "lru_ba", "lru_wx", "lru_bx", "lru_lambda", "lru_norm", "sgu_w", "sgu_b", "sgu_norm",
           "w_out", "ffn2_norm", "ffn2_wg", "ffn2_wu", "ffn2_wd", "final_norm")


def kernel(x, ffn1_norm, ffn1_wg, ffn1_wu, ffn1_wd, mix_norm, w_in, hgrn_lb_logits, hgrn_norm, conv_w, conv_b, lru_wa, lru_ba, lru_wx, lru_bx, lru_lambda, lru_norm, sgu_w, sgu_b, sgu_norm, w_out, ffn2_norm, ffn2_wg, ffn2_wu, ffn2_wd, final_norm, loss_target, m_ffn1_norm, m_ffn1_wg, m_ffn1_wu, m_ffn1_wd, m_mix_norm, m_w_in, m_hgrn_lb_logits, m_hgrn_norm, m_conv_w, m_conv_b, m_lru_wa, m_lru_ba, m_lru_wx, m_lru_bx, m_lru_lambda, m_lru_norm, m_sgu_w, m_sgu_b, m_sgu_norm, m_w_out, m_ffn2_norm, m_ffn2_wg, m_ffn2_wu, m_ffn2_wd, m_final_norm, v_ffn1_norm, v_ffn1_wg, v_ffn1_wu, v_ffn1_wd, v_mix_norm, v_w_in, v_hgrn_lb_logits, v_hgrn_norm, v_conv_w, v_conv_b, v_lru_wa, v_lru_ba, v_lru_wx, v_lru_bx, v_lru_lambda, v_lru_norm, v_sgu_w, v_sgu_b, v_sgu_norm, v_w_out, v_ffn2_norm, v_ffn2_wg, v_ffn2_wu, v_ffn2_wd, v_final_norm):
    args = dict(locals())
    W = {n: args[n] for n in WEIGHTS}
    M = {n: args["m_" + n] for n in WEIGHTS}
    V = {n: args["v_" + n] for n in WEIGHTS}

    T, D = x.shape[1], x.shape[2]
    L = ffn1_norm.shape[0]
    d_a, d_b, d_c = hgrn_norm.shape[1], lru_norm.shape[1], sgu_norm.shape[1]
    col_b, col_c = 4 * d_a, 4 * d_a + 2 * d_b
    tm = 512 if T % 512 == 0 else T
    tm_w = 1024 if T % 1024 == 0 else tm
    tm_d = 256 if T % 256 == 0 else tm
    my_c = lax.axis_index("c")
    my_slot = 2 * lax.axis_index("x") + lax.axis_index("y")
    c_arr = jnp.reshape(my_c, (1,)).astype(jnp.int32)
    slot_arr = jnp.reshape(my_slot, (1,)).astype(jnp.int32)

    nb = len(BIG)
    gplan = _gather_ici_plan(nb)

    def kview(a, n):
        return jnp.swapaxes(a, 1, 2) if n in TRANSPOSED else a

    Wk = {n: kview(W[n], n) for n in BIG}
    place_steps = 4 if all(Wk[n].shape[1] % 64 == 0 for n in BIG) else 2

    def placed(l):
        return _cast_place([Wk[n] for n in BIG], l, slot_arr, place_steps)

    conv_land = lax.dynamic_update_slice_in_dim(jnp.zeros((N_CHIPS,) + conv_w.shape, f32), conv_w[None], my_slot, axis=0)
    lands0 = placed(0)
    n_first = 3
    first = lands0[:n_first] + [conv_land]
    got = _exchange("gather0_ici", first, [_sds(a.shape, a.dtype) for a in first], _gather_ici_plan(n_first + 1),
                    aliases={a: a for a in range(n_first + 1)})
    got = _gather_d2d("gather0_d2d", got)
    G = [None] * L
    G[0] = dict(zip(BIG[:n_first], got[:n_first]))
    conv_full = jnp.transpose(got[n_first], (1, 2, 0, 3)).reshape(L, CONV_WIDTH, d_b)
    rest_plan = _gather_ici_plan(nb - n_first)
    rest_pending = _start_copies("gather_start_0", [], lands0[n_first:], rest_plan, got[0])

    def start_gather(l, after):
        return _start_copies(f"gather_start_{l}", [], placed(l), gplan, after)

    d2d_plan = _gather_d2d_plan(nb)

    lb = _lower_bounds(hgrn_lb_logits)
    avg_b = _group_avg_matrix(d_b, d_b // B_BLOCKS)
    avg_c = _group_avg_matrix(d_c, d_c // C_GROUPS)
    wa_dense = [_blockdiag(lru_wa[l]) for l in range(L)]
    wx_dense = [_blockdiag(lru_wx[l]) for l in range(L)]
    bexp = [jnp.repeat(sgu_b[l].T, d_c // C_GROUPS, axis=1) for l in range(L)]

    def lru_params(l):
        return (conv_full[l], conv_b[l][None], wa_dense[l], lru_ba[l].reshape(1, d_b), wx_dense[l],
                lru_bx[l].reshape(1, d_b), lru_lambda[l][None], lru_norm[l][None], avg_b)

    h = x.reshape(T, D)
    saved = []
    for l in range(L):
        s = {"h0": h}
        gain1, gain_mix = ffn1_norm[l][None], mix_norm[l][None]
        pending = None
        if l == 0:
            gain1 = gain1 + rest_pending[4][0:1, 0:1]
        elif l + 1 < L:
            pending = start_gather(l + 1, h)
            gain1 = gain1 + pending[4][0:1, 0:1]
        g = G[l]
        h, s["g1"], s["u1"] = _ffn_fwd(h, gain1, g["ffn1_wg"], g["ffn1_wu"], g["ffn1_wd"], tm)
        s["h1"] = h
        if l == 0:
            send, recv, _, lands, _ = rest_pending
            lands = _wait_copies("gather_wait_0", send, recv, [], lands, rest_plan, h)
            g.update(zip(BIG[n_first:], _gather_d2d("gather_d2d", lands)))
            if L > 1:
                pending = start_gather(1, g["w_in"])
                gain_mix = gain_mix + pending[4][0:1, 0:1]
        z = _proj_in_fwd(h, gain_mix, g["w_in"], tm)
        s["z"] = z
        s["oa"], s["o_pre"], s["states"] = _hgrn_fwd(z, lb[l][None], hgrn_norm[l][None], d_a)
        s["ob"], s["hl"] = _lru_fwd(z, col_b, d_b, *lru_params(l))
        s["oc"] = _sgu_fwd(z, col_c, d_c, sgu_w[l], bexp[l], sgu_norm[l][None], avg_c)
        h = _proj_out_fwd(h, s["oa"], s["ob"], s["oc"], g["w_out"], tm)
        s["h2"] = h
        gain2 = ffn2_norm[l][None]
        forward = None
        if pending is not None:
            send, recv, _, lands, _ = pending
            lands = _wait_copies(f"gather_wait_{l + 1}", send, recv, [], lands, gplan, h)
            forward = _start_copies(f"gather_d2d_start_{l + 1}", [], lands, d2d_plan)
            gain2 = gain2 + forward[4][0:1, 0:1]
        h, s["g2"], s["u2"] = _ffn_fwd(h, gain2, g["ffn2_wg"], g["ffn2_wu"], g["ffn2_wd"], tm)
        saved.append(s)
        if forward is not None:
            send, recv, _, lands, _ = forward
            G[l + 1] = dict(zip(BIG, _wait_copies(f"gather_d2d_wait_{l + 1}", send, recv, [], lands, d2d_plan, h)))

    dh, loss_part, d_final = _loss_head(h, final_norm[None], loss_target.reshape(T, D), tm)
    loss = lax.psum(loss_part[0, 0], ("x", "y", "c"))

    def pair_views(n_big):
        r = [(lambda i, o, p, a=a: i[a].at[:, pl.ds((1 - p.c) * (i[a].shape[1] // 2), i[a].shape[1] // 2)],
              lambda i, o, p, a=a: o[a], "sib") for a in range(n_big)]
        return r

    def chip_plan_for(n):
        return [(lambda s_, o, p, a=a, kind=kind: s_[a].at[p.peer_slot(kind)], lambda s_, o, p, a=a, j=j: o[a].at[j], kind)
                for a in range(n) for j, kind in enumerate(CHIP_KINDS)]

    chip_plan = chip_plan_for(nb)
    sbufs = {n: None for n in BIG}

    def pair_phase(arrs, extra=None):
        n = len(arrs)
        ins, remote = list(arrs), pair_views(n)
        outs = [_sds((N_CHIPS, a.shape[1] // 2, a.shape[2]), bf16) for a in arrs]
        if extra is not None:
            ins.append(extra)
            outs.append(_sds(extra.shape, f32))
            remote = remote + [(lambda i, o, p: i[n], lambda i, o, p: o[n], "sib")]
        recv = _exchange("grad_pair_d2d", ins, outs, remote)
        return _pair_sum(arrs, recv[:n], c_arr), (None if extra is None else _add(extra, recv[n]))

    def chip_sum_into(names, hs, lands, l):
        prev = [sbufs[n] for n in names] if sbufs[names[0]] is not None else []
        for n, buf in zip(names, _chip_sum(hs, lands, prev, slot_arr, c_arr, l, L)):
            sbufs[n] = buf

    def share(l, extra_in=(), extra_out=(), extra_remote=(), extra_local=()):
        remote = [(lambda i, o, p, a=a: _half(o[a].at[l], p.c), lambda i, o, p, a=a: _half(o[a].at[l], p.c), "sib")
                  for a in range(nb)]
        outs = [_sds(sbufs[n].shape, f32) for n in BIG] + list(extra_out)
        res = _exchange("grad_share_d2d", [sbufs[n] for n in BIG] + list(extra_in), outs, remote + list(extra_remote),
                        list(extra_local), aliases={a: a for a in range(nb)})
        for n, buf in zip(BIG, res[:nb]):
            sbufs[n] = buf
        return res[nb:]

    small = {n: [None] * L for n in SMALL if n != "final_norm"}
    chip_pending = pair_pending = early = None
    early_names = ("w_in", "w_out", "ffn2_wg", "ffn2_wu", "ffn2_wd")
    for l in reversed(range(L)):
        s, g = saved[l], G[l]
        gain2, gain_a = ffn2_norm[l][None], hgrn_norm[l][None]
        if pair_pending is not None:
            gain2 = gain2 + pair_pending[0][4][0:1, 0:1]
        dh, small["ffn2_norm"][l], dg, du, xn, dob = _ffn_bwd_dgrad(
            s["h2"], gain2, dh, s["g2"], s["u2"], g["ffn2_wg"], g["ffn2_wu"], g["ffn2_wd"], tm_d)
        if pair_pending is not None:
            (send, recv, grads_prev, lands, _), = pair_pending
            recv_a = _wait_copies(f"grad_pair_wait_{l + 1}", send, recv, grads_prev, lands, pair_views(nb), dh)
            hsum = _pair_sum(grads_prev, recv_a, c_arr)
            lands = [lax.empty((N_CHIPS - 1,) + hh.shape[1:], bf16) for hh in hsum]
            chip_pending = (_start_copies(f"grad_chip_start_{l + 1}", hsum, lands, chip_plan), hsum)
            gain_a = gain_a + chip_pending[0][4][0:1, 0:1]
            pair_pending = None
        dwg2, dwu2, dwd2 = _ffn_bwd_wgrad(xn, dob, s["g2"], s["u2"], dg, du, tm_w)
        doa, dob_, doc, dwo = _proj_out_bwd(dh, s["oa"], s["ob"], s["oc"], g["w_out"], tm)
        dz, small["hgrn_lb_logits"][l], small["hgrn_norm"][l] = _hgrn_bwd(
            s["z"], lb[l][None], gain_a, s["o_pre"], s["states"], doa, d_a)
        (dz, small["conv_w"][l], small["conv_b"][l], dwa, small["lru_ba"][l], dwx, small["lru_bx"][l],
         small["lru_lambda"][l], small["lru_norm"][l]) = _lru_bwd(s["z"], col_b, d_b, s["hl"], dob_, dz, *lru_params(l))
        small["lru_wa"][l] = _blockdiag_extract(dwa, B_BLOCKS)
        small["lru_wx"][l] = _blockdiag_extract(dwx, B_BLOCKS)
        dz, small["sgu_w"][l], dsb, small["sgu_norm"][l] = _sgu_bwd(
            s["z"], col_c, d_c, doc, dz, sgu_w[l], bexp[l], sgu_norm[l][None], avg_c)
        small["sgu_b"][l] = dsb.T
        dh, small["mix_norm"][l], xn = _proj_in_bwd_dgrad(s["h1"], mix_norm[l][None], dh, dz, g["w_in"], tm)
        dwi = _proj_in_bwd_wgrad(xn, dz, N_CHIPS)
        gain1 = ffn1_norm[l][None]
        if l == 0:
            hs_e, _ = pair_phase([dwi, dwo, dwg2, dwu2, dwd2])
            lands = [lax.empty((N_CHIPS - 1,) + hh.shape[1:], bf16) for hh in hs_e]
            early = (_start_copies("grad_chip_start_0", hs_e, lands, chip_plan_for(len(hs_e))), hs_e)
            gain1 = gain1 + early[0][4][0:1, 0:1]
        dh, small["ffn1_norm"][l], dg, du, xn, dob = _ffn_bwd_dgrad(
            s["h0"], gain1, dh, s["g1"], s["u1"], g["ffn1_wg"], g["ffn1_wu"], g["ffn1_wd"], tm_d)
        dwg1, dwu1, dwd1 = _ffn_bwd_wgrad(xn, dob, s["g1"], s["u1"], dg, du, tm_w)
        layer_grads = [dwg1, dwu1, dwd1, dwi, dwo, dwg2, dwu2, dwd2]

        if chip_pending is not None:
            (send, recv, hs, lands, _), hsum_prev = chip_pending
            lands = _wait_copies(f"grad_chip_wait_{l + 1}", send, recv, hs, lands, chip_plan, dwg1)
            chip_sum_into(BIG, hsum_prev, lands, l + 1)
            share(l + 1)
            chip_pending = None
        if l > 0:
            lands = [lax.empty((N_CHIPS, gr.shape[1] // 2, gr.shape[2]), bf16) for gr in layer_grads]
            pair_pending = (_start_copies(f"grad_pair_start_{l}", layer_grads, lands, pair_views(nb)),)
    grad_x = dh.reshape(x.shape)

    (send, recv, hs, lands, _), hs_e = early
    lands = _wait_copies("grad_chip_wait_0", send, recv, hs, lands, chip_plan_for(len(hs_e)), dwg1)
    chip_sum_into(early_names, hs_e, lands, 0)
    small_names = [n for n in SMALL]
    small_parts = [jnp.stack([jnp.reshape(v, (-1,)) for v in small[n]]) if n != "final_norm" else d_final for n in small_names]
    small_shapes = [p.shape for p in small_parts]
    packed = _pack(small_parts, 2 * 8 * 128).reshape(2, -1, 128)
    n_rows = packed.shape[1]
    late = [dwg1, dwu1, dwd1]
    nl = len(late)
    hsum, small_pair = pair_phase(late, packed)
    own_half = lax.dynamic_index_in_dim(small_pair, my_c, 0, keepdims=True)
    small_land = lax.dynamic_update_slice_in_dim(jnp.zeros((N_CHIPS, n_rows, 128), f32), own_half, my_slot, axis=0)
    late_plan = chip_plan_for(nl) + [(lambda s_, o, p: s_[nl].at[p.c], lambda s_, o, p: o[nl].at[p.slot], kind)
                                     for kind in CHIP_KINDS]
    lands = [lax.empty((N_CHIPS - 1,) + hh.shape[1:], bf16) for hh in hsum] + [small_land]
    send, recv, srcs, lands, token = _start_copies("grad_chip_start_last", hsum + [small_pair], lands, late_plan)

    def adam_operands(n):
        shape = Wk[n].shape
        rows_blk = 512 if shape[1] % 512 == 0 else (shape[1] // 2 if shape[1] > 512 else shape[1])
        return [a.reshape(-1, shape[-1]) for a in (Wk[n], sbufs[n], kview(M[n], n), kview(V[n], n))], rows_blk, shape

    partial = {}
    if L > 1:
        for n in BIG:
            flat, rows_blk, shape = adam_operands(n)
            partial[n] = _adamw(*flat, rows_blk, row_range=(shape[1], L * shape[1]), after=token)
    recv_b = _wait_copies("grad_chip_wait_last", send, recv, srcs, lands, late_plan,
                          partial[BIG[-1]][0] if partial else hsum[0])
    chip_sum_into(BIG[:nl], hsum, recv_b[:nl], 0)
    small_half = _sum_slots(recv_b[nl])
    (small_all,) = share(0, extra_in=[small_half], extra_out=[_sds(packed.shape, f32)],
                         extra_remote=[(lambda i, o, p: i[nb], lambda i, o, p: o[nb].at[p.c], "sib")],
                         extra_local=[(lambda i, o, p: i[nb], lambda i, o, p: o[nb].at[p.c])])
    grads = {n: kview(sbufs[n], n) for n in BIG}
    small_tot = _unpack(small_all.reshape(-1), small_shapes)
    for n, val in zip(small_names, small_tot):
        grads[n] = val
    grads["hgrn_lb_logits"] = _lower_bounds_bwd(hgrn_lb_logits, grads["hgrn_lb_logits"])
    shard_cols = conv_w.shape[2]
    grads["conv_w"] = lax.dynamic_slice_in_dim(grads["conv_w"].reshape(L, CONV_WIDTH, d_b), my_slot * shard_cols, shard_cols, axis=2)
    for n in SMALL:
        grads[n] = grads[n].reshape(W[n].shape)

    delta, new_m, new_v = {}, {}, {}
    for n in BIG:
        flat, rows_blk, shape = adam_operands(n)
        outs = _adamw(*flat, rows_blk, row_range=(0, shape[1]), prev=partial[n]) if partial else _adamw(*flat, rows_blk)
        delta[n], new_m[n], new_v[n], grads[n] = [kview(o.reshape(shape), n) for o in outs]
    shapes = [W[n].shape for n in SMALL]
    packs = [_pack([src[n] for n in SMALL], 8 * 128).reshape(-1, 128) for src in (W, grads, M, V)]
    d2, m2, v2, _ = _adamw(*packs, 4096)
    for dst, val in ((delta, d2), (new_m, m2), (new_v, v2)):
        for n, piece in zip(SMALL, _unpack(val.reshape(-1), shapes)):
            dst[n] = piece

    return (loss, grad_x, *[grads[n] for n in WEIGHTS], *[delta[n] for n in WEIGHTS],
            *[new_m[n] for n in WEIGHTS], *[new_v[n] for n in WEIGHTS])
```

```python
import math

import numpy as np
import jax
import jax.numpy as jnp
from jax import lax
from jax.experimental import pallas as pl
from jax.experimental.pallas import tpu as pltpu

f32 = jnp.float32
bf16 = jnp.bfloat16
HI = lax.Precision.HIGHEST
MESH = pl.DeviceIdType.MESH

EPS = 1e-6
HEAD = 128
A_CHUNK = 64
A_SUB = 16
A_INNER = 2
SUBLANES = 8
B_BLOCKS = 4
B_CHUNK = 256
CONV_WIDTH = 4
LRU_C = 8.0
C_GROUPS = 4
C_CHUNK = 128
C_INNER = 4
N_CHIPS = 4
ADAM_LR, ADAM_B1, ADAM_B2, ADAM_EPS, ADAM_WD, ADAM_STEP = 0.001, 0.9, 0.999, 1e-08, 0.01, 10
VMEM_LIMIT = 56 * 1024 * 1024


def _cparams(n_axes):
    return pltpu.CompilerParams(dimension_semantics=("arbitrary",) * n_axes, vmem_limit_bytes=VMEM_LIMIT)


def _sds(shape, dtype):
    return jax.ShapeDtypeStruct(tuple(shape), dtype)


def _full(shape):
    n = len(shape)
    return pl.BlockSpec(tuple(shape), lambda *_: (0,) * n)


def _resident(shape):
    n = len(shape)
    return pl.BlockSpec(tuple(shape), lambda *_: (0,) * n, pipeline_mode=pl.Buffered(1))


def _dot(a, b):
    return jnp.dot(a, b, preferred_element_type=f32)


def _dot_nt(a, b):
    return lax.dot_general(a, b, (((1,), (1,)), ((), ())), preferred_element_type=f32)


def _dot_tn(a, b):
    return lax.dot_general(a, b, (((0,), (0,)), ((), ())), preferred_element_type=f32)


def _silu(x):
    return x * jax.nn.sigmoid(x)


def _group_avg_matrix(n, group):
    idx = np.arange(n) // group
    return jnp.asarray((idx[:, None] == idx[None, :]).astype(np.float32) / group)


class _Place:
    def __init__(self):
        self.x, self.y, self.c = lax.axis_index("x"), lax.axis_index("y"), lax.axis_index("c")
        self.slot = 2 * self.x + self.y

    def peer(self, kind):
        x, y, c = self.x, self.y, self.c
        return {"sib": (x, y, 1 - c), "fx": (1 - x, y, c), "fy": (x, 1 - y, c), "fxy": (1 - x, 1 - y, c)}[kind]

    def peer_slot(self, kind):
        x, y = self.x, self.y
        return {"fx": 2 * (1 - x) + y, "fy": 2 * x + (1 - y), "fxy": 2 * (1 - x) + (1 - y)}[kind]


CHIP_KINDS = ("fx", "fy", "fxy")


def _exchange(name, ins, outs, remote, local=(), aliases=None):
    n_in, n_out, n_r, n_l = len(ins), len(outs), len(remote), len(local)

    def body(*refs):
        in_refs, out_refs = refs[:n_in], refs[n_in:n_in + n_out]
        send, recv, lsem = refs[n_in + n_out:]
        p = _Place()
        lcopies = []
        for t, (src, dst) in enumerate(local):
            cp = pltpu.make_async_copy(src(in_refs, out_refs, p), dst(in_refs, out_refs, p), lsem.at[t])
            cp.start()
            lcopies.append(cp)
        copies = []
        for t, (src, dst, kind) in enumerate(remote):
            cp = pltpu.make_async_remote_copy(
                src_ref=src(in_refs, out_refs, p), dst_ref=dst(in_refs, out_refs, p),
                send_sem=send.at[t], recv_sem=recv.at[t], device_id=p.peer(kind), device_id_type=MESH)
            cp.start()
            copies.append(cp)
        for cp in copies:
            cp.wait_recv()
        for cp in copies:
            cp.wait_send()
        for cp in lcopies:
            cp.wait()

    anyspec = pl.BlockSpec(memory_space=pl.ANY)
    res = pl.pallas_call(
        body, name=name, out_shape=tuple(outs),
        in_specs=[anyspec] * n_in, out_specs=tuple([anyspec] * n_out),
        scratch_shapes=[pltpu.SemaphoreType.DMA((n_r,)), pltpu.SemaphoreType.DMA((n_r,)),
                        pltpu.SemaphoreType.DMA((max(n_l, 1),))],
        input_output_aliases=aliases or {},
        compiler_params=pltpu.CompilerParams(has_side_effects=True),
    )(*ins)
    return list(res)


HBM_SPEC = pl.BlockSpec(memory_space=pltpu.HBM)
SEM_SPEC = pl.BlockSpec(memory_space=pltpu.SEMAPHORE)
ANY_SPEC = pl.BlockSpec(memory_space=pl.ANY)
DATAFLOW = pltpu.SideEffectType.DATAFLOW_SIDE_EFFECTING


def _in_hbm(a):
    return pltpu.with_memory_space_constraint(a, pltpu.HBM)


def _start_copies(name, srcs, lands, remote, after=None):
    n_s, n_l, n_r = len(srcs), len(lands), len(remote)
    extra = [] if after is None else [after]

    def body(*refs):
        src_refs, land_refs = refs[:n_s], refs[n_s:n_s + n_l]
        n_in = n_s + n_l + len(extra)
        send, recv = refs[n_in], refs[n_in + 1]
        token = refs[-1]
        p = _Place()
        for t, (src, dst, kind) in enumerate(remote):
            pltpu.make_async_remote_copy(
                src_ref=src(src_refs, land_refs, p), dst_ref=dst(src_refs, land_refs, p),
                send_sem=send.at[t], recv_sem=recv.at[t], device_id=p.peer(kind), device_id_type=MESH).start()
        token[...] = jnp.zeros_like(token)

    thru = [pltpu.HBM(a.shape, a.dtype) for a in lands]
    res = pl.pallas_call(
        body, name=name,
        out_shape=(pltpu.SemaphoreType.DMA((n_r,)), pltpu.SemaphoreType.DMA((n_r,)), *thru, _sds((8, 128), f32)),
        in_specs=[ANY_SPEC] * n_s + [HBM_SPEC] * n_l + [ANY_SPEC] * len(extra),
        out_specs=(SEM_SPEC, SEM_SPEC, *([HBM_SPEC] * n_l), pl.BlockSpec(memory_space=pltpu.VMEM)),
        input_output_aliases={n_s + i: 2 + i for i in range(n_l)},
        compiler_params=pltpu.CompilerParams(has_side_effects=DATAFLOW),
    )(*srcs, *[_in_hbm(a) for a in lands], *extra)
    return res[0], res[1], list(srcs), list(res[2:2 + n_l]), res[-1]


def _wait_copies(name, send, recv, srcs, lands, remote, after):
    n_s, n_l = len(srcs), len(lands)

    def body(*refs):
        src_refs, land_refs = refs[:n_s], refs[n_s:n_s + n_l]
        send_ref, recv_ref = refs[n_s + n_l], refs[n_s + n_l + 1]
        p = _Place()
        for t, (src, dst, kind) in enumerate(remote):
            cp = pltpu.make_async_remote_copy(
                src_ref=src(src_refs, land_refs, p), dst_ref=dst(src_refs, land_refs, p),
                send_sem=send_ref.at[t], recv_sem=recv_ref.at[t], device_id=p.peer(kind), device_id_type=MESH)
            cp.wait_send()
            cp.wait_recv()

    res = pl.pallas_call(
        body, name=name, out_shape=tuple(pltpu.HBM(a.shape, a.dtype) for a in lands),
        in_specs=[ANY_SPEC] * n_s + [HBM_SPEC] * n_l + [SEM_SPEC, SEM_SPEC, ANY_SPEC],
        out_specs=tuple([HBM_SPEC] * n_l),
        input_output_aliases={n_s + i: i for i in range(n_l)},
        compiler_params=pltpu.CompilerParams(has_side_effects=DATAFLOW),
    )(*srcs, *lands, send, recv, after)
    return list(res)


def _half(ref, c):
    n2 = ref.shape[0] // 2
    return ref.at[pl.ds(c * n2, n2)]


def _gather_ici_plan(n):
    def view(a):
        return lambda s, o, p: _half(o[a].at[p.slot], p.c)

    return [(view(a), view(a), kind) for a in range(n) for kind in CHIP_KINDS]


def _gather_d2d_plan(n):
    remote = []
    for a in range(n):
        for kind in CHIP_KINDS:
            view = lambda i, o, p, a=a, kind=kind: _half(o[a].at[p.peer_slot(kind)], p.c)
            remote.append((view, view, "sib"))
    return remote


def _gather_d2d(name, lands):
    n = len(lands)
    outs = [_sds(g.shape, g.dtype) for g in lands]
    return _exchange(name, list(lands), outs, _gather_d2d_plan(n), aliases={a: a for a in range(n)})


def _cast_place(weights, layer, slot_arr, n_steps=4):
    def body(s_ref, *refs):
        n = len(refs) // 2
        for a in range(n):
            refs[n + a][...] = refs[a][...].astype(bf16)

    in_specs, out_specs, out_shape = [], [], []
    for w in weights:
        _, R, Cc = w.shape
        rt = R // n_steps
        in_specs.append(pl.BlockSpec((None, rt, Cc), lambda i, s: (layer, i, 0)))
        out_specs.append(pl.BlockSpec((None, rt, Cc), lambda i, s: (s[0], i, 0)))
        out_shape.append(_sds((N_CHIPS, R, Cc), bf16))
    gs = pltpu.PrefetchScalarGridSpec(num_scalar_prefetch=1, grid=(n_steps,), in_specs=in_specs, out_specs=tuple(out_specs))
    return list(pl.pallas_call(body, name="cast_place", grid_spec=gs, out_shape=tuple(out_shape),
                               compiler_params=_cparams(1))(slot_arr, *weights))


def _ffn_fwd(h, gain, wg, wu, wd, tm):
    T, D = h.shape
    nsh, F = wg.shape[0], wg.shape[1]
    nt = T // tm

    def body(h_ref, gain_ref, wg_ref, wu_ref, wd_ref, out_ref, gs_ref, us_ref):
        hv = h_ref[...]
        r = lax.rsqrt(jnp.mean(hv * hv, axis=-1, keepdims=True) + EPS)
        xn = (hv * r * gain_ref[...]).astype(bf16)
        acc = None
        for k in range(nsh):
            g = _dot_nt(xn, wg_ref[k])
            u = _dot_nt(xn, wu_ref[k])
            gs_ref[k] = g.astype(bf16)
            us_ref[k] = u.astype(bf16)
            part = _dot((_silu(g) * u).astype(bf16), wd_ref[k])
            acc = part if acc is None else acc + part
        out_ref[...] = hv + 0.5 * acc

    sav = pl.BlockSpec((nsh, tm, F), lambda i: (0, i, 0))
    return pl.pallas_call(
        body, name="ffn_fwd", grid=(nt,),
        in_specs=[pl.BlockSpec((tm, D), lambda i: (i, 0)), _full((1, D)), _resident((nsh, F, D)), _resident((nsh, F, D)),
                  _resident((nsh, F, D))],
        out_specs=(pl.BlockSpec((tm, D), lambda i: (i, 0)), sav, sav),
        out_shape=(_sds((T, D), f32), _sds((nsh, T, F), bf16), _sds((nsh, T, F), bf16)),
        compiler_params=_cparams(1),
    )(h, gain, wg, wu, wd)


def _ffn_bwd_dgrad(h, gain, dout, gs, us, wg, wu, wd, tm):
    T, D = h.shape
    nsh, F = wg.shape[0], wg.shape[1]
    nt = T // tm

    def body(h_ref, gain_ref, dout_ref, gs_ref, us_ref, wg_ref, wu_ref, wd_ref,
             dh_ref, dgain_ref, dg_ref, du_ref, xn_ref, dob_ref):
        @pl.when(pl.program_id(0) == 0)
        def _():
            dgain_ref[...] = jnp.zeros_like(dgain_ref)

        hv = h_ref[...]
        r = lax.rsqrt(jnp.mean(hv * hv, axis=-1, keepdims=True) + EPS)
        xh = hv * r
        xn_ref[...] = (xh * gain_ref[...]).astype(bf16)
        dv = dout_ref[...]
        dob = (0.5 * dv).astype(bf16)
        dob_ref[...] = dob
        dxn = None
        for k in range(nsh):
            da = _dot_nt(dob, wd_ref[k])
            g = gs_ref[k].astype(f32)
            u = us_ref[k].astype(f32)
            sg = jax.nn.sigmoid(g)
            dg = (da * u * (sg * (1.0 + g * (1.0 - sg)))).astype(bf16)
            du = (da * (g * sg)).astype(bf16)
            dg_ref[k] = dg
            du_ref[k] = du
            part = _dot(dg, wg_ref[k]) + _dot(du, wu_ref[k])
            dxn = part if dxn is None else dxn + part
        dgain_ref[...] += jnp.sum(dxn * xh, axis=0, keepdims=True)
        dxh = dxn * gain_ref[...]
        dh_ref[...] = dv + r * (dxh - xh * jnp.mean(dxh * xh, axis=-1, keepdims=True))

    tok = pl.BlockSpec((tm, D), lambda i: (i, 0))
    sav = pl.BlockSpec((nsh, tm, F), lambda i: (0, i, 0))
    return pl.pallas_call(
        body, name="ffn_bwd_dgrad", grid=(nt,),
        in_specs=[tok, _full((1, D)), tok, sav, sav, _resident((nsh, F, D)), _resident((nsh, F, D)), _resident((nsh, F, D))],
        out_specs=(tok, _full((1, D)), sav, sav, tok, tok),
        out_shape=(_sds((T, D), f32), _sds((1, D), f32), _sds((nsh, T, F), bf16), _sds((nsh, T, F), bf16),
                   _sds((T, D), bf16), _sds((T, D), bf16)),
        compiler_params=_cparams(1),
    )(h, gain, dout, gs, us, wg, wu, wd)


def _ffn_bwd_wgrad(xn, dob, gs, us, dg, du, tm):
    T, D = xn.shape
    nsh, F = gs.shape[0], gs.shape[2]
    nt = T // tm

    def body(xn_ref, dob_ref, gs_ref, us_ref, dg_ref, du_ref, dwg_ref, dwu_ref, dwd_ref, ag_ref, au_ref, ad_ref):
        i = pl.program_id(1)

        @pl.when(i == 0)
        def _():
            ag_ref[...] = jnp.zeros_like(ag_ref)
            au_ref[...] = jnp.zeros_like(au_ref)
            ad_ref[...] = jnp.zeros_like(ad_ref)

        xn_v = xn_ref[...]
        ag_ref[...] += _dot_tn(dg_ref[...], xn_v)
        au_ref[...] += _dot_tn(du_ref[...], xn_v)
        g = gs_ref[...].astype(f32)
        a = (_silu(g) * us_ref[...].astype(f32)).astype(bf16)
        ad_ref[...] += _dot_tn(a, dob_ref[...])

        @pl.when(i == nt - 1)
        def _():
            dwg_ref[...] = ag_ref[...].astype(bf16)
            dwu_ref[...] = au_ref[...].astype(bf16)
            dwd_ref[...] = ad_ref[...].astype(bf16)

    tok = pl.BlockSpec((tm, D), lambda k, i: (i, 0))
    sav = pl.BlockSpec((None, tm, F), lambda k, i: (k, i, 0))
    wdspec = pl.BlockSpec((None, F, D), lambda k, i: (k, 0, 0))
    return pl.pallas_call(
        body, name="ffn_bwd_wgrad", grid=(nsh, nt),
        in_specs=[tok, tok, sav, sav, sav, sav],
        out_specs=(wdspec, wdspec, wdspec),
        out_shape=(_sds((nsh, F, D), bf16),) * 3,
        scratch_shapes=[pltpu.VMEM((F, D), f32)] * 3,
        compiler_params=_cparams(2),
    )(xn, dob, gs, us, dg, du)


def _proj_in_fwd(h, gain, w_in, tm):
    T, D = h.shape
    nsh, N = w_in.shape[0], w_in.shape[2]
    nt = T // tm

    def body(h_ref, gain_ref, w_ref, z_ref):
        hv = h_ref[...]
        r = lax.rsqrt(jnp.mean(hv * hv, axis=-1, keepdims=True) + EPS)
        xn = (hv * r * gain_ref[...]).astype(bf16)
        for k in range(nsh):
            z_ref[:, k * N:(k + 1) * N] = _dot(xn, w_ref[k])

    return pl.pallas_call(
        body, name="proj_in_fwd", grid=(nt,),
        in_specs=[pl.BlockSpec((tm, D), lambda i: (i, 0)), _full((1, D)), _full((nsh, D, N))],
        out_specs=pl.BlockSpec((tm, nsh * N), lambda i: (i, 0)),
        out_shape=_sds((T, nsh * N), f32),
        compiler_params=_cparams(1),
    )(h, gain, w_in)


def _proj_in_bwd_dgrad(h, gain, dres, dz, w_in, tm):
    T, D = h.shape
    nsh, N = w_in.shape[0], w_in.shape[2]
    nt = T // tm

    def body(h_ref, gain_ref, dres_ref, dz_ref, w_ref, dh_ref, dgain_ref, xn_ref):
        @pl.when(pl.program_id(0) == 0)
        def _():
            dgain_ref[...] = jnp.zeros_like(dgain_ref)

        dxn = _dot_nt(dz_ref[:, 0:N], w_ref[0])
        for k in range(1, nsh):
            dxn = dxn + _dot_nt(dz_ref[:, k * N:(k + 1) * N], w_ref[k])
        hv = h_ref[...]
        r = lax.rsqrt(jnp.mean(hv * hv, axis=-1, keepdims=True) + EPS)
        xh = hv * r
        xn_ref[...] = (xh * gain_ref[...]).astype(bf16)
        dgain_ref[...] += jnp.sum(dxn * xh, axis=0, keepdims=True)
        dxh = dxn * gain_ref[...]
        dh_ref[...] = dres_ref[...] + r * (dxh - xh * jnp.mean(dxh * xh, axis=-1, keepdims=True))

    tok = pl.BlockSpec((tm, D), lambda i: (i, 0))
    return pl.pallas_call(
        body, name="proj_in_bwd_dgrad", grid=(nt,),
        in_specs=[tok, _full((1, D)), tok, pl.BlockSpec((tm, nsh * N), lambda i: (i, 0)), _full((nsh, D, N))],
        out_specs=(tok, _full((1, D)), tok),
        out_shape=(_sds((T, D), f32), _sds((1, D), f32), _sds((T, D), bf16)),
        compiler_params=_cparams(1),
    )(h, gain, dres, dz, w_in)


def _proj_in_bwd_wgrad(xn, dz, nsh):
    T, D = xn.shape
    N = dz.shape[1] // nsh

    def body(xn_ref, dz_ref, dw_ref):
        dw_ref[...] = _dot_tn(xn_ref[...], dz_ref[...]).astype(bf16)

    return pl.pallas_call(
        body, name="proj_in_bwd_wgrad", grid=(nsh,),
        in_specs=[_full((T, D)), pl.BlockSpec((T, N), lambda k: (0, k))],
        out_specs=pl.BlockSpec((None, D, N), lambda k: (k, 0, 0)),
        out_shape=_sds((nsh, D, N), bf16),
        compiler_params=_cparams(1),
    )(xn, dz)


def _proj_out_fwd(h, oa, ob, oc, w_out, tm):
    T, D = h.shape
    nsh, R = w_out.shape[0], w_out.shape[1]
    da, db = oa.shape[1], ob.shape[1]
    nt = T // tm

    def body(h_ref, oa_ref, ob_ref, oc_ref, w_ref, out_ref):
        w = w_ref[...].reshape(nsh * R, D)
        out_ref[...] = (h_ref[...] + _dot(oa_ref[...], w[:da]) + _dot(ob_ref[...], w[da:da + db])
                        + _dot(oc_ref[...], w[da + db:]))

    def tok(n):
        return pl.BlockSpec((tm, n), lambda i: (i, 0))

    return pl.pallas_call(
        body, name="proj_out_fwd", grid=(nt,),
        in_specs=[tok(D), tok(da), tok(db), tok(oc.shape[1]), _full((nsh, R, D))],
        out_specs=tok(D), out_shape=_sds((T, D), f32),
        compiler_params=_cparams(1),
    )(h, oa, ob, oc, w_out)


def _proj_out_bwd(dh, oa, ob, oc, w_out, tm):
    T, D = dh.shape
    nsh, R = w_out.shape[0], w_out.shape[1]
    da, db, dc = oa.shape[1], ob.shape[1], oc.shape[1]
    nt = T // tm

    def body(dh_ref, oa_ref, ob_ref, oc_ref, w_ref, doa_ref, dob_ref, doc_ref, dw_ref, acc_ref):
        i = pl.program_id(0)

        @pl.when(i == 0)
        def _():
            acc_ref[...] = jnp.zeros_like(acc_ref)

        d = dh_ref[...].astype(bf16)
        w = w_ref[...].reshape(nsh * R, D)
        dm = _dot_nt(d, w)
        doa_ref[...] = dm[:, :da]
        dob_ref[...] = dm[:, da:da + db]
        doc_ref[...] = dm[:, da + db:]
        acc_ref[pl.ds(0, da), :] += _dot_tn(oa_ref[...], d)
        acc_ref[pl.ds(da, db), :] += _dot_tn(ob_ref[...], d)
        acc_ref[pl.ds(da + db, dc), :] += _dot_tn(oc_ref[...], d)

        @pl.when(i == nt - 1)
        def _():
            dw_ref[...] = acc_ref[...].astype(bf16).reshape(nsh, R, D)

    def tok(n):
        return pl.BlockSpec((tm, n), lambda i: (i, 0))

    wspec = _full((nsh, R, D))
    return pl.pallas_call(
        body, name="proj_out_bwd", grid=(nt,),
        in_specs=[tok(D), tok(da), tok(db), tok(dc), wspec],
        out_specs=(tok(da), tok(db), tok(dc), wspec),
        out_shape=(_sds((T, da), f32), _sds((T, db), f32), _sds((T, dc), f32), _sds((nsh, R, D), bf16)),
        scratch_shapes=[pltpu.VMEM((nsh * R, D), f32)],
        compiler_params=_cparams(1),
    )(dh, oa, ob, oc, w_out)


def _head_sum(m, n_heads):
    parts = []
    for hd in range(n_heads):
        s = jnp.sum(m[:, hd * HEAD:(hd + 1) * HEAD], axis=-1, keepdims=True)
        parts.append(jnp.broadcast_to(s, (m.shape[0], HEAD)))
    return parts[0] if n_heads == 1 else jnp.concatenate(parts, axis=1)


def _cat(parts, axis):
    return parts[0] if len(parts) == 1 else jnp.concatenate(parts, axis=axis)


def _hgrn_block(q, fl, iv, lb, states, tri, n_heads, n_inner):
    C = q.shape[0] // n_inner
    qs = _silu(q)
    forget = lb + (1.0 - lb) * jax.nn.sigmoid(fl)
    kk = 1.0 - forget
    logf = jnp.log(forget)
    b = jnp.dot(tri, logf, precision=HI, preferred_element_type=f32)
    vb = iv.astype(bf16)
    heads = [slice(hd * HEAD, (hd + 1) * HEAD) for hd in range(n_heads)]
    n_sub = C // A_SUB

    off, qe, kd, dec = {}, [], [], []
    for j in range(n_inner):
        c0 = j * C
        for blk in range(1, n_sub):
            lo = c0 + blk * A_SUB
            piv = b[lo:lo + 1]
            qt = (qs[lo:lo + A_SUB] * jnp.exp(b[lo:lo + A_SUB] - piv)).astype(bf16)
            kt = (kk[c0:lo] * jnp.exp(piv - b[c0:lo])).astype(bf16)
            parts = []
            for sl in heads:
                sc = _dot_nt(qt[:, sl], kt[:, sl])
                parts.append(_dot(sc.astype(bf16), vb[c0:lo, sl]))
            off[(j, blk)] = _cat(parts, 1)
        bj = b[c0:c0 + C]
        b_end = bj[C - 1:C]
        qe.append((qs[c0:c0 + C] * jnp.exp(bj)).astype(bf16))
        kd.append((kk[c0:c0 + C] * jnp.exp(b_end - bj)).astype(bf16))
        dec.append(jnp.exp(b_end))

    outs = []
    for j in range(n_inner):
        for blk in range(n_sub):
            lo = j * C + blk * A_SUB
            groups = [off[(j, blk)][r0:r0 + SUBLANES] if blk > 0 else None for r0 in range(0, A_SUB, SUBLANES)]
            for s in range(A_SUB):
                first = (s // SUBLANES) * SUBLANES
                n_rows = A_SUB - first
                row = lax.broadcasted_iota(jnp.int32, (n_rows, 1), 0) + first
                gate = jnp.where(row >= s, 0.0, -1e30)
                r = slice(lo + first, lo + A_SUB)
                m = qs[r] * jnp.exp((b[r] - b[lo + s:lo + s + 1]) + gate) * kk[lo + s:lo + s + 1]
                term = _head_sum(m, n_heads) * iv[lo + s:lo + s + 1]
                for gi in range(first // SUBLANES, A_SUB // SUBLANES):
                    piece = term[gi * SUBLANES - first:(gi + 1) * SUBLANES - first]
                    groups[gi] = piece if groups[gi] is None else groups[gi] + piece
            outs.extend(groups)
    o = jnp.concatenate(outs, axis=0)

    inter = []
    states = list(states)
    for j in range(n_inner):
        c0 = j * C
        parts = []
        for hd, sl in enumerate(heads):
            st = states[hd]
            parts.append(_dot_nt(qe[j][:, sl], st.astype(bf16)))
            states[hd] = dec[j][:, sl] * st + _dot_tn(vb[c0:c0 + C, sl], kd[j][:, sl])
        inter.append(_cat(parts, 1))
    return o + _cat(inter, 0), tuple(states)


def _hgrn_gate(o, g, gain, n_heads):
    ms = _head_sum(o * o, n_heads) * (1.0 / HEAD)
    return o * lax.rsqrt(ms + EPS) * gain * _silu(g)


def _tri_matrix(c, n_inner):
    idx = np.arange(c * n_inner)
    same = (idx[:, None] // c) == (idx[None, :] // c)
    return jnp.asarray((same & (idx[:, None] >= idx[None, :])).astype(np.float32))


def _hgrn_fwd(z, lb, gain, d_a):
    T = z.shape[0]
    C = A_CHUNK * A_INNER
    nc = T // C
    nh = d_a // HEAD
    tri = _tri_matrix(A_CHUNK, A_INNER)

    def body(q_ref, f_ref, i_ref, g_ref, lb_ref, gain_ref, tri_ref, out_ref, o_ref, st_ref, carry_ref):
        @pl.when(pl.program_id(0) == 0)
        def _():
            carry_ref[...] = jnp.zeros_like(carry_ref)

        states = tuple(carry_ref[hd] for hd in range(nh))
        st_ref[...] = carry_ref[...]
        o, new_states = _hgrn_block(q_ref[...], f_ref[...], i_ref[...], lb_ref[...], states, tri_ref[...], nh, A_INNER)
        o_ref[...] = o
        out_ref[...] = _hgrn_gate(o, g_ref[...], gain_ref[...], nh).astype(bf16)
        for hd in range(nh):
            carry_ref[hd] = new_states[hd]

    def col(j):
        return pl.BlockSpec((C, d_a), lambda c, j=j: (c, j))

    tok = pl.BlockSpec((C, d_a), lambda c: (c, 0))
    return pl.pallas_call(
        body, name="hgrn_fwd", grid=(nc,),
        in_specs=[col(0), col(1), col(2), col(3), _full((1, d_a)), _full((1, d_a)), _full((C, C))],
        out_specs=(tok, tok, pl.BlockSpec((None, nh, HEAD, HEAD), lambda c: (c, 0, 0, 0))),
        out_shape=(_sds((T, d_a), bf16), _sds((T, d_a), f32), _sds((nc, nh, HEAD, HEAD), f32)),
        scratch_shapes=[pltpu.VMEM((nh, HEAD, HEAD), f32)],
        compiler_params=_cparams(1),
    )(z, z, z, z, lb, gain, tri)


def _hgrn_bwd(z, lb, gain, o_pre, states, dout, d_a):
    T = z.shape[0]
    C = A_CHUNK * A_INNER
    nc = T // C
    nh = d_a // HEAD
    tri = _tri_matrix(A_CHUNK, A_INNER)

    def body(q_ref, f_ref, i_ref, g_ref, lb_ref, gain_ref, tri_ref, o_ref, st_ref, do_ref,
             dz_ref, dlb_ref, dgain_ref, carry_ref):
        @pl.when(pl.program_id(0) == 0)
        def _():
            carry_ref[...] = jnp.zeros_like(carry_ref)
            dlb_ref[...] = jnp.zeros_like(dlb_ref)
            dgain_ref[...] = jnp.zeros_like(dgain_ref)

        _, vjp_gate = jax.vjp(lambda o, g, gv: _hgrn_gate(o, g, gv, nh), o_ref[...], g_ref[...], gain_ref[...])
        d_o, dg, dgain = vjp_gate(do_ref[...])
        tri_v = tri_ref[...]

        def fn(q, fl, iv, lbv, sts):
            return _hgrn_block(q, fl, iv, lbv, sts, tri_v, nh, A_INNER)

        states_in = tuple(st_ref[hd] for hd in range(nh))
        _, vjp = jax.vjp(fn, q_ref[...], f_ref[...], i_ref[...], lb_ref[...], states_in)
        dstates = tuple(carry_ref[hd] for hd in range(nh))
        dq, df, di, dlb, dst = vjp((d_o, dstates))
        dz_ref[:, 0:d_a] = dq.astype(bf16)
        dz_ref[:, d_a:2 * d_a] = df.astype(bf16)
        dz_ref[:, 2 * d_a:3 * d_a] = di.astype(bf16)
        dz_ref[:, 3 * d_a:4 * d_a] = dg.astype(bf16)
        dlb_ref[...] += dlb
        dgain_ref[...] += dgain
        for hd in range(nh):
            carry_ref[hd] = dst[hd]

    def col(j):
        return pl.BlockSpec((C, d_a), lambda c, j=j: (nc - 1 - c, j))

    tok = pl.BlockSpec((C, d_a), lambda c: (nc - 1 - c, 0))
    return pl.pallas_call(
        body, name="hgrn_bwd", grid=(nc,),
        in_specs=[col(0), col(1), col(2), col(3), _full((1, d_a)), _full((1, d_a)), _full((C, C)), tok,
                  pl.BlockSpec((None, nh, HEAD, HEAD), lambda c: (nc - 1 - c, 0, 0, 0)), tok],
        out_specs=(pl.BlockSpec((C, 4 * d_a), lambda c: (nc - 1 - c, 0)), _full((1, d_a)), _full((1, d_a))),
        out_shape=(_sds(z.shape, bf16), _sds((1, d_a), f32), _sds((1, d_a), f32)),
        scratch_shapes=[pltpu.VMEM((nh, HEAD, HEAD), f32)],
        compiler_params=_cparams(1),
    )(z, z, z, z, lb, gain, tri, o_pre, states, dout)


def _one_minus_exp(x):
    series = -x * (1.0 + x * (0.5 + x * (1.0 / 6.0 + x * (1.0 / 24.0))))
    return jnp.where(x > -0.03, series, 1.0 - jnp.exp(x))


def _lru_pre(xc, wa, ba, wx, bx, lam):
    xb16 = xc.astype(bf16)
    r = jax.nn.sigmoid(_dot(xb16, wa.astype(bf16)) + ba)
    gi = jax.nn.sigmoid(_dot(xb16, wx.astype(bf16)) + bx)
    log_a = -LRU_C * r * jax.nn.softplus(-lam)
    a = jnp.exp(log_a)
    mult = jnp.sqrt(_one_minus_exp(2.0 * log_a))
    return a, mult * gi * xc


def _lru_post(h, gate, gain, avg):
    y = h * jax.nn.gelu(gate)
    ms = _group_mean(y * y, avg)
    return y * lax.rsqrt(ms + EPS) * gain


def _shift_down(x, d, prev):
    row = lax.broadcasted_iota(jnp.int32, x.shape, 0)
    return jnp.where(row >= d, pltpu.roll(x, d, 0), pltpu.roll(prev, d, 0))


def _shift_up(x, d, nxt):
    n = x.shape[0]
    row = lax.broadcasted_iota(jnp.int32, x.shape, 0)
    return jnp.where(row < n - d, pltpu.roll(x, n - d, 0), pltpu.roll(nxt, n - d, 0))


def _scan_rows(a, u, reverse):
    n = a.shape[0]
    row = lax.broadcasted_iota(jnp.int32, a.shape, 0)
    d = 1
    while d < n:
        shift, ok = (n - d, row < n - d) if reverse else (d, row >= d)
        su = jnp.where(ok, pltpu.roll(u, shift, 0), 0.0)
        sa = jnp.where(ok, pltpu.roll(a, shift, 0), 1.0)
        u = u + a * su
        a = a * sa
        d *= 2
    return a, u


def _conv(xb, xprev, cw, cb):
    xc = cb + cw[CONV_WIDTH - 1:CONV_WIDTH] * xb
    for d in range(1, CONV_WIDTH):
        xc = xc + cw[CONV_WIDTH - 1 - d:CONV_WIDTH - d] * _shift_down(xb, d, xprev)
    return xc


def _lru_fwd(z, col0, d_b, cw, cb, wa, ba, wx, bx, lam, gain, avg):
    T = z.shape[0]
    R = min(B_CHUNK, T)
    nr = T // R
    jb = col0 // d_b

    def body(xb_ref, gate_ref, cw_ref, cb_ref, wa_ref, ba_ref, wx_ref, bx_ref, lam_ref, gain_ref, avg_ref,
             out_ref, h_ref, xprev_ref, hprev_ref):
        @pl.when(pl.program_id(0) == 0)
        def _():
            xprev_ref[...] = jnp.zeros_like(xprev_ref)
            hprev_ref[...] = jnp.zeros_like(hprev_ref)

        xb = xb_ref[...]
        xc = _conv(xb, xprev_ref[...], cw_ref[...], cb_ref[...])
        a, u = _lru_pre(xc, wa_ref[...], ba_ref[...], wx_ref[...], bx_ref[...], lam_ref[...])
        acum, hl = _scan_rows(a, u, False)
        h = hl + acum * hprev_ref[R - 1:R, :]
        h_ref[...] = h
        out_ref[...] = _lru_post(h, gate_ref[...], gain_ref[...], avg_ref[...]).astype(bf16)
        xprev_ref[...] = xb
        hprev_ref[...] = h

    vec = _full((1, d_b))
    return pl.pallas_call(
        body, name="lru_fwd", grid=(nr,),
        in_specs=[pl.BlockSpec((R, d_b), lambda i: (i, jb)), pl.BlockSpec((R, d_b), lambda i: (i, jb + 1)),
                  _full((CONV_WIDTH, d_b)), vec, _full((d_b, d_b)), vec, _full((d_b, d_b)), vec, vec, vec, _full((d_b, d_b))],
        out_specs=(pl.BlockSpec((R, d_b), lambda i: (i, 0)), pl.BlockSpec((R, d_b), lambda i: (i, 0))),
        out_shape=(_sds((T, d_b), bf16), _sds((T, d_b), f32)),
        scratch_shapes=[pltpu.VMEM((R, d_b), f32), pltpu.VMEM((R, d_b), f32)],
        compiler_params=_cparams(1),
    )(z, z, cw, cb, wa, ba, wx, bx, lam, gain, avg)


def _lru_bwd(z, col0, d_b, hsave, dout, dz_buf, cw, cb, wa, ba, wx, bx, lam, gain, avg):
    T = z.shape[0]
    R = min(B_CHUNK, T)
    nr = T // R
    jb = col0 // d_b

    def body(xb_ref, xp_ref, gate_ref, h_ref, hp_ref, do_ref,
             cw_ref, cb_ref, wa_ref, ba_ref, wx_ref, bx_ref, lam_ref, gain_ref, avg_ref, dzin_ref,
             dz_ref, dcw_ref, dcb_ref, dwa_ref, dba_ref, dwx_ref, dbx_ref, dlam_ref, dgain_ref,
             gfirst_ref, afirst_ref, dxcn_ref):
        step = pl.program_id(0)
        first_in_time = step == nr - 1

        @pl.when(step == 0)
        def _():
            for r in (dcw_ref, dcb_ref, dwa_ref, dba_ref, dwx_ref, dbx_ref, dlam_ref, dgain_ref,
                      gfirst_ref, afirst_ref, dxcn_ref):
                r[...] = jnp.zeros_like(r)

        xb = xb_ref[...]
        keep = jnp.where(first_in_time, 0.0, 1.0)
        xprev = xp_ref[...] * keep
        hprev = hp_ref[...] * keep
        cw = cw_ref[...]
        xc = _conv(xb, xprev, cw, cb_ref[...])
        (a, _), vjp_pre = jax.vjp(_lru_pre, xc, wa_ref[...], ba_ref[...], wx_ref[...], bx_ref[...], lam_ref[...])
        h = h_ref[...]
        avg = avg_ref[...]
        _, vjp_post = jax.vjp(lambda hh, gg, gn: _lru_post(hh, gg, gn, avg), h, gate_ref[...], gain_ref[...])
        dh, dgate, dgain = vjp_post(do_ref[...])
        a_next = _shift_up(a, 1, jnp.broadcast_to(afirst_ref[0:1, :], a.shape))
        acum, gl = _scan_rows(a_next, dh, True)
        gtot = gl + acum * gfirst_ref[0:1, :]
        da = gtot * _shift_down(h, 1, hprev)
        dxc, dwa, dba, dwx, dbx, dlam = vjp_pre((da, gtot))
        dxcn = dxcn_ref[...]
        dxb = cw[CONV_WIDTH - 1:CONV_WIDTH] * dxc
        dcw_ref[CONV_WIDTH - 1:CONV_WIDTH, :] += jnp.sum(dxc * xb, axis=0, keepdims=True)
        for d in range(1, CONV_WIDTH):
            tap = CONV_WIDTH - 1 - d
            dxb = dxb + cw[tap:tap + 1] * _shift_up(dxc, d, dxcn)
            dcw_ref[tap:tap + 1, :] += jnp.sum(dxc * _shift_down(xb, d, xprev), axis=0, keepdims=True)
        dz_ref[:, 0:d_b] = dxb.astype(bf16)
        dz_ref[:, d_b:2 * d_b] = dgate.astype(bf16)
        dcb_ref[...] += jnp.sum(dxc, axis=0, keepdims=True)
        dwa_ref[...] += dwa
        dba_ref[...] += dba
        dwx_ref[...] += dwx
        dbx_ref[...] += dbx
        dlam_ref[...] += dlam
        dgain_ref[...] += dgain
        gfirst_ref[...] = jnp.broadcast_to(gtot[0:1, :], gfirst_ref.shape)
        afirst_ref[...] = jnp.broadcast_to(a[0:1, :], afirst_ref.shape)
        dxcn_ref[...] = dxc

    vec = _full((1, d_b))
    mat = _full((d_b, d_b))

    def cur(j):
        return pl.BlockSpec((R, d_b), lambda i, j=j: (nr - 1 - i, j))

    def prev(j):
        return pl.BlockSpec((R, d_b), lambda i, j=j: (jnp.maximum(nr - 2 - i, 0), j))

    return pl.pallas_call(
        body, name="lru_bwd", grid=(nr,),
        in_specs=[cur(jb), prev(jb), cur(jb + 1), cur(0), prev(0), cur(0),
                  _full((CONV_WIDTH, d_b)), vec, mat, vec, mat, vec, vec, vec, mat, ANY_SPEC],
        out_specs=(pl.BlockSpec((R, 2 * d_b), lambda i: (nr - 1 - i, col0 // (2 * d_b))), _full((CONV_WIDTH, d_b)), vec, mat, vec, mat, vec, vec, vec),
        out_shape=(_sds(dz_buf.shape, bf16), _sds((CONV_WIDTH, d_b), f32), _sds((1, d_b), f32), _sds((d_b, d_b), f32),
                   _sds((1, d_b), f32), _sds((d_b, d_b), f32), _sds((1, d_b), f32), _sds((1, d_b), f32), _sds((1, d_b), f32)),
        scratch_shapes=[pltpu.VMEM((8, d_b), f32), pltpu.VMEM((8, d_b), f32), pltpu.VMEM((R, d_b), f32)],
        input_output_aliases={15: 0},
        compiler_params=_cparams(1),
    )(z, z, z, hsave, hsave, dout, cw, cb, wa, ba, wx, bx, lam, gain, avg, dz_buf)


def _two_pass(x, m16):
    hi = x.astype(bf16)
    lo = (x - hi.astype(f32)).astype(bf16)
    return _dot(hi, m16) + _dot(lo, m16)


@jax.custom_vjp
def _group_mean(x, avg):
    return _two_pass(x, avg.astype(bf16))


def _group_mean_fwd(x, avg):
    return _group_mean(x, avg), avg


def _group_mean_bwd(avg, ct):
    return _two_pass(ct, avg.astype(bf16)), jnp.zeros_like(avg)


_group_mean.defvjp(_group_mean_fwd, _group_mean_bwd)


def _sgu_chunk(u_in, v_in, w, bexp, gain, avg, n_groups):
    C, d_c = u_in.shape
    gd = d_c // n_groups
    u = jax.nn.gelu(u_in)
    v = jax.nn.gelu(v_in)
    mu = _group_mean(v, avg)
    vc = v - mu
    var = _group_mean(vc * vc, avg)
    vh = (vc * lax.rsqrt(var + EPS)).astype(bf16)
    lane = lax.broadcasted_iota(jnp.int32, (1, d_c), 1)
    causal = lax.broadcasted_iota(jnp.int32, (C, C), 0) >= lax.broadcasted_iota(jnp.int32, (C, C), 1)
    zz = bexp
    for g in range(n_groups):
        wg = jnp.where(causal, w[g], 0.0).astype(bf16)
        zz = zz + jnp.where((lane >= g * gd) & (lane < (g + 1) * gd), _dot(wg, vh), 0.0)
    y = u * zz
    ms = _group_mean(y * y, avg)
    return y * lax.rsqrt(ms + EPS) * gain


def _sgu_inner(T):
    return C_INNER if T % (C_CHUNK * C_INNER) == 0 else 1


def _sgu_fwd(z, col0, d_c, w, bexp, gain, avg):
    T = z.shape[0]
    C = C_CHUNK
    n_in = _sgu_inner(T)
    R = C * n_in
    jb = col0 // d_c
    G = w.shape[0]

    def body(u_ref, v_ref, w_ref, b_ref, gain_ref, avg_ref, out_ref):
        w_v, b_v, gain_v, avg = w_ref[...], b_ref[...], gain_ref[...], avg_ref[...]
        for j in range(n_in):
            rows = pl.ds(j * C, C)
            out_ref[rows, :] = _sgu_chunk(u_ref[rows, :], v_ref[rows, :], w_v, b_v, gain_v, avg, G).astype(bf16)

    return pl.pallas_call(
        body, name="sgu_fwd", grid=(T // R,),
        in_specs=[pl.BlockSpec((R, d_c), lambda i: (i, jb)), pl.BlockSpec((R, d_c), lambda i: (i, jb + 1)),
                  _full((G, C, C)), _full((C, d_c)), _full((1, d_c)), _full((d_c, d_c))],
        out_specs=pl.BlockSpec((R, d_c), lambda i: (i, 0)),
        out_shape=_sds((T, d_c), bf16),
        compiler_params=_cparams(1),
    )(z, z, w, bexp, gain, avg)


def _sgu_bwd(z, col0, d_c, dout, dz_buf, w, bexp, gain, avg):
    T = z.shape[0]
    C = C_CHUNK
    n_in = _sgu_inner(T)
    R = C * n_in
    nc = T // R
    jb = col0 // d_c
    G = w.shape[0]
    gd = d_c // G

    def body(u_ref, v_ref, do_ref, w_ref, b_ref, gain_ref, avg_ref, dzin_ref, dz_ref, dw_ref, db_ref, dgain_ref, dbexp_ref):
        step = pl.program_id(0)

        @pl.when(step == 0)
        def _():
            dw_ref[...] = jnp.zeros_like(dw_ref)
            dgain_ref[...] = jnp.zeros_like(dgain_ref)
            dbexp_ref[...] = jnp.zeros_like(dbexp_ref)

        avg, w_v, b_v, gain_v = avg_ref[...], w_ref[...], b_ref[...], gain_ref[...]
        dw = dbexp = dgain = None
        for j in range(n_in):
            rows = pl.ds(j * C, C)
            _, vjp = jax.vjp(lambda a, b, c, d, e: _sgu_chunk(a, b, c, d, e, avg, G),
                             u_ref[rows, :], v_ref[rows, :], w_v, b_v, gain_v)
            du, dv, dw_j, dbexp_j, dgain_j = vjp(do_ref[rows, :])
            dz_ref[rows, 0:d_c] = du.astype(bf16)
            dz_ref[rows, d_c:2 * d_c] = dv.astype(bf16)
            dw = dw_j if dw is None else dw + dw_j
            dbexp = dbexp_j if dbexp is None else dbexp + dbexp_j
            dgain = dgain_j if dgain is None else dgain + dgain_j
        dw_ref[...] += dw
        dbexp_ref[...] += dbexp
        dgain_ref[...] += dgain

        @pl.when(step == nc - 1)
        def _():
            lane = lax.broadcasted_iota(jnp.int32, (1, d_c), 1)
            acc = dbexp_ref[...]
            for g in range(G):
                sel = jnp.where((lane >= g * gd) & (lane < (g + 1) * gd), acc, 0.0)
                db_ref[:, g:g + 1] = jnp.sum(sel, axis=1, keepdims=True)

    return pl.pallas_call(
        body, name="sgu_bwd", grid=(nc,),
        in_specs=[pl.BlockSpec((R, d_c), lambda i: (i, jb)), pl.BlockSpec((R, d_c), lambda i: (i, jb + 1)),
                  pl.BlockSpec((R, d_c), lambda i: (i, 0)),
                  _full((G, C, C)), _full((C, d_c)), _full((1, d_c)), _full((d_c, d_c)), ANY_SPEC],
        out_specs=(pl.BlockSpec((R, 2 * d_c), lambda i: (i, col0 // (2 * d_c))), _full((G, C, C)), _full((C, G)), _full((1, d_c))),
        out_shape=(_sds(dz_buf.shape, bf16), _sds((G, C, C), f32), _sds((C, G), f32), _sds((1, d_c), f32)),
        scratch_shapes=[pltpu.VMEM((C, d_c), f32)],
        input_output_aliases={7: 0},
        compiler_params=_cparams(1),
    )(z, z, dout, w, bexp, gain, avg, dz_buf)


def _loss_head(h, gain, target, tm):
    T, D = h.shape
    nt = T // tm

    def body(h_ref, gain_ref, tgt_ref, dh_ref, loss_ref, dgain_ref):
        @pl.when(pl.program_id(0) == 0)
        def _():
            loss_ref[...] = jnp.zeros_like(loss_ref)
            dgain_ref[...] = jnp.zeros_like(dgain_ref)

        hv = h_ref[...]
        gain_v = gain_ref[...]
        r = lax.rsqrt(jnp.mean(hv * hv, axis=-1, keepdims=True) + EPS)
        xh = hv * r
        e = xh * gain_v - tgt_ref[...]
        loss_ref[...] += 0.5 * jnp.sum(jnp.mean(e * e, axis=-1, keepdims=True), axis=0, keepdims=True)
        dy = e * (1.0 / D)
        dgain_ref[...] += jnp.sum(dy * xh, axis=0, keepdims=True)
        dxh = dy * gain_v
        dh_ref[...] = r * (dxh - xh * jnp.mean(dxh * xh, axis=-1, keepdims=True))

    tok = pl.BlockSpec((tm, D), lambda i: (i, 0))
    return pl.pallas_call(
        body, name="loss_head", grid=(nt,),
        in_specs=[tok, _full((1, D)), tok],
        out_specs=(tok, _full((1, 128)), _full((1, D))),
        out_shape=(_sds((T, D), f32), _sds((1, 128), f32), _sds((1, D), f32)),
        compiler_params=_cparams(1),
    )(h, gain, target)


def _lower_bounds_fn(logits):
    n = logits.shape[0]
    mx = jnp.max(logits, axis=0, keepdims=True)
    ex = jnp.exp(logits - mx)
    soft = ex / jnp.sum(ex, axis=0, keepdims=True)
    rows = [jnp.zeros_like(soft[0:1])]
    for l in range(1, n):
        rows.append(rows[-1] + soft[l:l + 1])
    return jnp.concatenate(rows, axis=0)


def _lower_bounds(logits):
    def body(x_ref, o_ref):
        o_ref[...] = _lower_bounds_fn(x_ref[...])

    return pl.pallas_call(body, name="lower_bounds", out_shape=_sds(logits.shape, f32))(logits)


def _lower_bounds_bwd(logits, dlb):
    def body(x_ref, d_ref, o_ref):
        _, vjp = jax.vjp(_lower_bounds_fn, x_ref[...])
        o_ref[...] = vjp(d_ref[...])[0]

    return pl.pallas_call(body, name="lower_bounds_bwd", out_shape=_sds(logits.shape, f32))(logits, dlb)


def _adamw(w, g, m, v, rows_blk, row_range=None, prev=(), after=None):
    R, Cc = w.shape
    lo, hi = (0, R) if row_range is None else row_range
    span = hi - lo
    rb = span if (span <= rows_blk and lo % span == 0) else math.gcd(math.gcd(span, lo), rows_blk)
    extra = list(prev) + ([] if after is None else [after])

    def body(w_ref, g_ref, m_ref, v_ref, *rest):
        d_ref, nm_ref, nv_ref, go_ref = rest[len(extra):]
        gv = g_ref[...]
        m2 = ADAM_B1 * m_ref[...] + (1.0 - ADAM_B1) * gv
        v2 = ADAM_B2 * v_ref[...] + (1.0 - ADAM_B2) * (gv * gv)
        m_hat = m2 / (1.0 - ADAM_B1 ** ADAM_STEP)
        v_hat = v2 / (1.0 - ADAM_B2 ** ADAM_STEP)
        d_ref[...] = -ADAM_LR * (m_hat / (jnp.sqrt(v_hat) + ADAM_EPS) + ADAM_WD * w_ref[...])
        nm_ref[...] = m2
        nv_ref[...] = v2
        go_ref[...] = gv

    first = lo // rb
    spec = pl.BlockSpec((rb, Cc), lambda i: (i + first, 0))
    return pl.pallas_call(
        body, name="adamw", grid=((hi - lo) // rb,),
        in_specs=[spec] * 4 + [ANY_SPEC] * len(extra), out_specs=(spec,) * 4, out_shape=(_sds((R, Cc), f32),) * 4,
        input_output_aliases={4 + j: j for j in range(len(prev))},
        compiler_params=_cparams(1),
    )(w, g, m, v, *extra)


def _pair_sum(grads, recv, c_arr):
    n = len(grads)
    nsh = grads[0].shape[0]

    def body(c_ref, *refs):
        for a in range(n):
            refs[2 * n + a][...] = (refs[a][...].astype(f32) + refs[n + a][...].astype(f32)).astype(bf16)

    g_specs, r_specs, out_shape = [], [], []
    for g in grads:
        _, R, Cc = g.shape
        r2 = R // 2
        g_specs.append(pl.BlockSpec((None, r2, Cc), lambda s, c: (s, c[0], 0)))
        r_specs.append(pl.BlockSpec((None, r2, Cc), lambda s, c: (s, 0, 0)))
        out_shape.append(_sds((nsh, r2, Cc), bf16))
    gs = pltpu.PrefetchScalarGridSpec(num_scalar_prefetch=1, grid=(nsh,), in_specs=g_specs + r_specs, out_specs=tuple(r_specs))
    return list(pl.pallas_call(body, name="pair_sum", grid_spec=gs, out_shape=tuple(out_shape),
                               compiler_params=_cparams(1))(c_arr, *grads, *recv))


def _add(a, b):
    def body(a_ref, b_ref, o_ref):
        o_ref[...] = a_ref[...] + b_ref[...]

    return pl.pallas_call(body, name="pair_sum_small", out_shape=_sds(a.shape, f32))(a, b)


def _chip_sum(hsum, recv, bufs, slot_arr, c_arr, layer, n_layers):
    n = len(hsum)
    prev = list(bufs)
    steps = 2

    def body(s_ref, c_ref, *refs):
        outs = refs[len(refs) - n:]
        for a in range(n):
            acc = refs[a][...].astype(f32)
            for j in range(N_CHIPS - 1):
                acc = acc + refs[n + a][j].astype(f32)
            outs[a][...] = acc

    h_specs, r_specs, o_specs, out_shape = [], [], [], []
    for hh in hsum:
        _, r2, Cc = hh.shape
        rt = r2 // steps
        h_specs.append(pl.BlockSpec((None, rt, Cc), lambda i, s, c: (s[0], i, 0)))
        r_specs.append(pl.BlockSpec((N_CHIPS - 1, rt, Cc), lambda i, s, c: (0, i, 0)))
        o_specs.append(pl.BlockSpec((None, rt, Cc), lambda i, s, c: (layer, c[0] * steps + i, 0)))
        out_shape.append(_sds((n_layers, 2 * r2, Cc), f32))
    gs = pltpu.PrefetchScalarGridSpec(num_scalar_prefetch=2, grid=(steps,),
                                      in_specs=h_specs + r_specs + [ANY_SPEC] * len(prev), out_specs=tuple(o_specs))
    return list(pl.pallas_call(body, name="chip_sum", grid_spec=gs, out_shape=tuple(out_shape),
                               input_output_aliases={2 + 2 * n + a: a for a in range(len(prev))},
                               compiler_params=_cparams(1))(slot_arr, c_arr, *hsum, *recv, *prev))


def _sum_slots(x):
    def body(x_ref, o_ref):
        acc = x_ref[0]
        for j in range(1, x.shape[0]):
            acc = acc + x_ref[j]
        o_ref[...] = acc

    return pl.pallas_call(body, name="sum_slots", out_shape=_sds(x.shape[1:], f32))(x)


def _blockdiag(w):
    nb, bd, _ = w.shape
    eye = jnp.eye(nb, dtype=w.dtype)
    return (eye[:, None, :, None] * w[:, :, None, :]).reshape(nb * bd, nb * bd)


def _blockdiag_extract(dense, nb):
    bd = dense.shape[0] // nb
    d4 = dense.reshape(nb, bd, nb, bd)
    return jnp.stack([d4[i, :, i, :] for i in range(nb)])


def _pack(arrays, multiple):
    flat = jnp.concatenate([a.reshape(-1).astype(f32) for a in arrays])
    pad = (-flat.shape[0]) % multiple
    return jnp.pad(flat, (0, pad))


def _unpack(flat, shapes):
    out, off = [], 0
    for s in shapes:
        n = int(np.prod(s))
        out.append(flat[off:off + n].reshape(s))
        off += n
    return out


BIG = ("ffn1_wg", "ffn1_wu", "ffn1_wd", "w_in", "w_out", "ffn2_wg", "ffn2_wu", "ffn2_wd")
TRANSPOSED = ("ffn1_wg", "ffn1_wu", "ffn2_wg", "ffn2_wu")
SMALL = ("ffn1_norm", "mix_norm", "hgrn_lb_logits", "hgrn_norm", "conv_w", "conv_b", "lru_wa", "lru_ba", "lru_wx",
         "lru_bx", "lru_lambda", "lru_norm", "sgu_w", "sgu_b", "sgu_norm", "ffn2_norm", "final_norm")
WEIGHTS = ("ffn1_norm", "ffn1_wg", "ffn1_wu", "ffn1_wd", "mix_norm", "w_in", "hgrn_lb_logits", "hgrn_norm", "conv_w",
           "conv_b", "lru_wa", "lru_ba", "lru_wx", "lru_bx", "lru_lambda", "lru_norm", "sgu_w", "sgu_b", "sgu_norm",
           "w_out", "ffn2_norm", "ffn2_wg", "ffn2_wu", "ffn2_wd", "final_norm")


def kernel(x, ffn1_norm, ffn1_wg, ffn1_wu, ffn1_wd, mix_norm, w_in, hgrn_lb_logits, hgrn_norm, conv_w, conv_b, lru_wa, lru_ba, lru_wx, lru_bx, lru_lambda, lru_norm, sgu_w, sgu_b, sgu_norm, w_out, ffn2_norm, ffn2_wg, ffn2_wu, ffn2_wd, final_norm, loss_target, m_ffn1_norm, m_ffn1_wg, m_ffn1_wu, m_ffn1_wd, m_mix_norm, m_w_in, m_hgrn_lb_logits, m_hgrn_norm, m_conv_w, m_conv_b, m_lru_wa, m_lru_ba, m_lru_wx, m_lru_bx, m_lru_lambda, m_lru_norm, m_sgu_w, m_sgu_b, m_sgu_norm, m_w_out, m_ffn2_norm, m_ffn2_wg, m_ffn2_wu, m_ffn2_wd, m_final_norm, v_ffn1_norm, v_ffn1_wg, v_ffn1_wu, v_ffn1_wd, v_mix_norm, v_w_in, v_hgrn_lb_logits, v_hgrn_norm, v_conv_w, v_conv_b, v_lru_wa, v_lru_ba, v_lru_wx, v_lru_bx, v_lru_lambda, v_lru_norm, v_sgu_w, v_sgu_b, v_sgu_norm, v_w_out, v_ffn2_norm, v_ffn2_wg, v_ffn2_wu, v_ffn2_wd, v_final_norm):
    args = dict(locals())
    W = {n: args[n] for n in WEIGHTS}
    M = {n: args["m_" + n] for n in WEIGHTS}
    V = {n: args["v_" + n] for n in WEIGHTS}

    T, D = x.shape[1], x.shape[2]
    L = ffn1_norm.shape[0]
    d_a, d_b, d_c = hgrn_norm.shape[1], lru_norm.shape[1], sgu_norm.shape[1]
    col_b, col_c = 4 * d_a, 4 * d_a + 2 * d_b
    tm = 512 if T % 512 == 0 else T
    tm_w = 1024 if T % 1024 == 0 else tm
    tm_d = 256 if T % 256 == 0 else tm
    my_c = lax.axis_index("c")
    my_slot = 2 * lax.axis_index("x") + lax.axis_index("y")
    c_arr = jnp.reshape(my_c, (1,)).astype(jnp.int32)
    slot_arr = jnp.reshape(my_slot, (1,)).astype(jnp.int32)

    nb = len(BIG)
    gplan = _gather_ici_plan(nb)

    def kview(a, n):
        return jnp.swapaxes(a, 1, 2) if n in TRANSPOSED else a

    Wk = {n: kview(W[n], n) for n in BIG}
    place_steps = 4 if all(Wk[n].shape[1] % 64 == 0 for n in BIG) else 2

    conv_land = lax.dynamic_update_slice_in_dim(jnp.zeros((N_CHIPS,) + conv_w.shape, f32), conv_w[None], my_slot, axis=0)
    lands0 = _cast_place([Wk[n] for n in BIG], 0, slot_arr, place_steps)
    n_first = 3
    first_plan = _gather_ici_plan(n_first + 1)
    send, recv, _, first, token = _start_copies("gather_start_first", [], lands0[:n_first] + [conv_land], first_plan)
    slot_after = slot_arr + token[0, 0].astype(jnp.int32)
    later = {l: _cast_place([Wk[n] for n in BIG], l, slot_after, place_steps) for l in range(1, L)}
    got = _wait_copies("gather_wait_first", send, recv, [], first, first_plan, later[L - 1][0] if later else lands0[-1])
    got = _gather_d2d("gather0_d2d", got)

    def placed(l):
        return later[l]
    G = [None] * L
    G[0] = dict(zip(BIG[:n_first], got[:n_first]))
    conv_full = jnp.transpose(got[n_first], (1, 2, 0, 3)).reshape(L, CONV_WIDTH, d_b)
    rest_plan = _gather_ici_plan(nb - n_first)
    rest_pending = _start_copies("gather_start_0", [], lands0[n_first:], rest_plan, got[0])

    def start_gather(l, after):
        return _start_copies(f"gather_start_{l}", [], placed(l), gplan, after)

    d2d_plan = _gather_d2d_plan(nb)

    lb = _lower_bounds(hgrn_lb_logits)
    avg_b = _group_avg_matrix(d_b, d_b // B_BLOCKS)
    avg_c = _group_avg_matrix(d_c, d_c // C_GROUPS)
    wa_dense = [_blockdiag(lru_wa[l]) for l in range(L)]
    wx_dense = [_blockdiag(lru_wx[l]) for l in range(L)]
    bexp = [jnp.repeat(sgu_b[l].T, d_c // C_GROUPS, axis=1) for l in range(L)]

    def lru_params(l):
        return (conv_full[l], conv_b[l][None], wa_dense[l], lru_ba[l].reshape(1, d_b), wx_dense[l],
                lru_bx[l].reshape(1, d_b), lru_lambda[l][None], lru_norm[l][None], avg_b)

    h = x.reshape(T, D)
    saved = []
    for l in range(L):
        s = {"h0": h}
        gain1, gain_mix = ffn1_norm[l][None], mix_norm[l][None]
        pending = None
        if l == 0:
            gain1 = gain1 + rest_pending[4][0:1, 0:1]
        elif l + 1 < L:
            pending = start_gather(l + 1, h)
            gain1 = gain1 + pending[4][0:1, 0:1]
        g = G[l]
        h, s["g1"], s["u1"] = _ffn_fwd(h, gain1, g["ffn1_wg"], g["ffn1_wu"], g["ffn1_wd"], tm)
        s["h1"] = h
        if l == 0:
            send, recv, _, lands, _ = rest_pending
            lands = _wait_copies("gather_wait_0", send, recv, [], lands, rest_plan, h)
            g.update(zip(BIG[n_first:], _gather_d2d("gather_d2d", lands)))
            if L > 1:
                pending = start_gather(1, g["w_in"])
                gain_mix = gain_mix + pending[4][0:1, 0:1]
        z = _proj_in_fwd(h, gain_mix, g["w_in"], tm)
        s["z"] = z
        s["oa"], s["o_pre"], s["states"] = _hgrn_fwd(z, lb[l][None], hgrn_norm[l][None], d_a)
        s["ob"], s["hl"] = _lru_fwd(z, col_b, d_b, *lru_params(l))
        s["oc"] = _sgu_fwd(z, col_c, d_c, sgu_w[l], bexp[l], sgu_norm[l][None], avg_c)
        h = _proj_out_fwd(h, s["oa"], s["ob"], s["oc"], g["w_out"], tm)
        s["h2"] = h
        gain2 = ffn2_norm[l][None]
        forward = None
        if pending is not None:
            send, recv, _, lands, _ = pending
            lands = _wait_copies(f"gather_wait_{l + 1}", send, recv, [], lands, gplan, h)
            forward = _start_copies(f"gather_d2d_start_{l + 1}", [], lands, d2d_plan)
            gain2 = gain2 + forward[4][0:1, 0:1]
        h, s["g2"], s["u2"] = _ffn_fwd(h, gain2, g["ffn2_wg"], g["ffn2_wu"], g["ffn2_wd"], tm)
        saved.append(s)
        if forward is not None:
            send, recv, _, lands, _ = forward
            G[l + 1] = dict(zip(BIG, _wait_copies(f"gather_d2d_wait_{l + 1}", send, recv, [], lands, d2d_plan, h)))

    dh, loss_part, d_final = _loss_head(h, final_norm[None], loss_target.reshape(T, D), tm)
    loss = lax.psum(loss_part[0, 0], ("x", "y", "c"))

    def pair_views(n_big):
        r = [(lambda i, o, p, a=a: i[a].at[:, pl.ds((1 - p.c) * (i[a].shape[1] // 2), i[a].shape[1] // 2)],
              lambda i, o, p, a=a: o[a], "sib") for a in range(n_big)]
        return r

    def chip_plan_for(n):
        return [(lambda s_, o, p, a=a, kind=kind: s_[a].at[p.peer_slot(kind)], lambda s_, o, p, a=a, j=j: o[a].at[j], kind)
                for a in range(n) for j, kind in enumerate(CHIP_KINDS)]

    chip_plan = chip_plan_for(nb)
    sbufs = {n: None for n in BIG}

    def pair_phase(arrs, extra=None):
        n = len(arrs)
        ins, remote = list(arrs), pair_views(n)
        outs = [_sds((N_CHIPS, a.shape[1] // 2, a.shape[2]), bf16) for a in arrs]
        if extra is not None:
            ins.append(extra)
            outs.append(_sds(extra.shape, f32))
            remote = remote + [(lambda i, o, p: i[n], lambda i, o, p: o[n], "sib")]
        recv = _exchange("grad_pair_d2d", ins, outs, remote)
        return _pair_sum(arrs, recv[:n], c_arr), (None if extra is None else _add(extra, recv[n]))

    def chip_sum_into(names, hs, lands, l):
        prev = [sbufs[n] for n in names] if sbufs[names[0]] is not None else []
        for n, buf in zip(names, _chip_sum(hs, lands, prev, slot_arr, c_arr, l, L)):
            sbufs[n] = buf

    def share_plan(l):
        view = lambda a: (lambda s_, o, p: _half(o[a].at[l], p.c))
        return [(view(a), view(a), "sib") for a in range(nb)]

    def share_start(l):
        send, recv, _, bufs, token = _start_copies(f"grad_share_start_{l}", [], [sbufs[n] for n in BIG], share_plan(l))
        for n, buf in zip(BIG, bufs):
            sbufs[n] = buf
        return (l, send, recv), token

    def share_wait(pending, after):
        l, send, recv = pending
        bufs = _wait_copies(f"grad_share_wait_{l}", send, recv, [], [sbufs[n] for n in BIG], share_plan(l), after)
        for n, buf in zip(BIG, bufs):
            sbufs[n] = buf

    def share(l, extra_in=(), extra_out=(), extra_remote=(), extra_local=()):
        remote = [(lambda i, o, p, a=a: _half(o[a].at[l], p.c), lambda i, o, p, a=a: _half(o[a].at[l], p.c), "sib")
                  for a in range(nb)]
        outs = [_sds(sbufs[n].shape, f32) for n in BIG] + list(extra_out)
        res = _exchange("grad_share_d2d", [sbufs[n] for n in BIG] + list(extra_in), outs, remote + list(extra_remote),
                        list(extra_local), aliases={a: a for a in range(nb)})
        for n, buf in zip(BIG, res[:nb]):
            sbufs[n] = buf
        return res[nb:]

    small = {n: [None] * L for n in SMALL if n != "final_norm"}
    chip_pending = pair_pending = early = share_token = None
    shares = []
    early_names = ("w_in", "w_out", "ffn2_wg", "ffn2_wu", "ffn2_wd")
    for l in reversed(range(L)):
        s, g = saved[l], G[l]
        gain2, gain_a = ffn2_norm[l][None], hgrn_norm[l][None]
        if pair_pending is not None:
            gain2 = gain2 + pair_pending[0][4][0:1, 0:1]
        if share_token is not None:
            gain2 = gain2 + share_token[0:1, 0:1]
            share_token = None
        dh, small["ffn2_norm"][l], dg, du, xn, dob = _ffn_bwd_dgrad(
            s["h2"], gain2, dh, s["g2"], s["u2"], g["ffn2_wg"], g["ffn2_wu"], g["ffn2_wd"], tm_d)
        if pair_pending is not None:
            (send, recv, grads_prev, lands, _), = pair_pending
            recv_a = _wait_copies(f"grad_pair_wait_{l + 1}", send, recv, grads_prev, lands, pair_views(nb), dh)
            hsum = _pair_sum(grads_prev, recv_a, c_arr)
            lands = [lax.empty((N_CHIPS - 1,) + hh.shape[1:], bf16) for hh in hsum]
            chip_pending = (_start_copies(f"grad_chip_start_{l + 1}", hsum, lands, chip_plan), hsum)
            gain_a = gain_a + chip_pending[0][4][0:1, 0:1]
            pair_pending = None
        dwg2, dwu2, dwd2 = _ffn_bwd_wgrad(xn, dob, s["g2"], s["u2"], dg, du, tm_w)
        doa, dob_, doc, dwo = _proj_out_bwd(dh, s["oa"], s["ob"], s["oc"], g["w_out"], tm)
        dz, small["hgrn_lb_logits"][l], small["hgrn_norm"][l] = _hgrn_bwd(
            s["z"], lb[l][None], gain_a, s["o_pre"], s["states"], doa, d_a)
        (dz, small["conv_w"][l], small["conv_b"][l], dwa, small["lru_ba"][l], dwx, small["lru_bx"][l],
         small["lru_lambda"][l], small["lru_norm"][l]) = _lru_bwd(s["z"], col_b, d_b, s["hl"], dob_, dz, *lru_params(l))
        small["lru_wa"][l] = _blockdiag_extract(dwa, B_BLOCKS)
        small["lru_wx"][l] = _blockdiag_extract(dwx, B_BLOCKS)
        dz, small["sgu_w"][l], dsb, small["sgu_norm"][l] = _sgu_bwd(
            s["z"], col_c, d_c, doc, dz, sgu_w[l], bexp[l], sgu_norm[l][None], avg_c)
        small["sgu_b"][l] = dsb.T
        dh, small["mix_norm"][l], xn = _proj_in_bwd_dgrad(s["h1"], mix_norm[l][None], dh, dz, g["w_in"], tm)
        dwi = _proj_in_bwd_wgrad(xn, dz, N_CHIPS)
        gain1 = ffn1_norm[l][None]
        if l == 0:
            hs_e, _ = pair_phase([dwi, dwo, dwg2, dwu2, dwd2])
            lands = [lax.empty((N_CHIPS - 1,) + hh.shape[1:], bf16) for hh in hs_e]
            early = (_start_copies("grad_chip_start_0", hs_e, lands, chip_plan_for(len(hs_e))), hs_e)
            gain1 = gain1 + early[0][4][0:1, 0:1]
        dh, small["ffn1_norm"][l], dg, du, xn, dob = _ffn_bwd_dgrad(
            s["h0"], gain1, dh, s["g1"], s["u1"], g["ffn1_wg"], g["ffn1_wu"], g["ffn1_wd"], tm_d)
        dwg1, dwu1, dwd1 = _ffn_bwd_wgrad(xn, dob, s["g1"], s["u1"], dg, du, tm_w)
        layer_grads = [dwg1, dwu1, dwd1, dwi, dwo, dwg2, dwu2, dwd2]

        if chip_pending is not None:
            (send, recv, hs, lands, _), hsum_prev = chip_pending
            lands = _wait_copies(f"grad_chip_wait_{l + 1}", send, recv, hs, lands, chip_plan, dwg1)
            chip_sum_into(BIG, hsum_prev, lands, l + 1)
            pending_share, share_token = share_start(l + 1)
            shares.append(pending_share)
            chip_pending = None
        if l > 0:
            lands = [lax.empty((N_CHIPS, gr.shape[1] // 2, gr.shape[2]), bf16) for gr in layer_grads]
            pair_pending = (_start_copies(f"grad_pair_start_{l}", layer_grads, lands, pair_views(nb)),)
    grad_x = dh.reshape(x.shape)

    (send, recv, hs, lands, _), hs_e = early
    lands = _wait_copies("grad_chip_wait_0", send, recv, hs, lands, chip_plan_for(len(hs_e)), dwg1)
    chip_sum_into(early_names, hs_e, lands, 0)
    small_names = [n for n in SMALL]
    small_parts = [jnp.stack([jnp.reshape(v, (-1,)) for v in small[n]]) if n != "final_norm" else d_final for n in small_names]
    small_shapes = [p.shape for p in small_parts]
    packed = _pack(small_parts, 2 * 8 * 128).reshape(2, -1, 128)
    n_rows = packed.shape[1]
    late = [dwg1, dwu1, dwd1]
    nl = len(late)
    hsum, small_pair = pair_phase(late, packed)
    own_half = lax.dynamic_index_in_dim(small_pair, my_c, 0, keepdims=True)
    small_land = lax.dynamic_update_slice_in_dim(jnp.zeros((N_CHIPS, n_rows, 128), f32), own_half, my_slot, axis=0)
    late_plan = chip_plan_for(nl) + [(lambda s_, o, p: s_[nl].at[p.c], lambda s_, o, p: o[nl].at[p.slot], kind)
                                     for kind in CHIP_KINDS]
    lands = [lax.empty((N_CHIPS - 1,) + hh.shape[1:], bf16) for hh in hsum] + [small_land]
    send, recv, srcs, lands, token = _start_copies("grad_chip_start_last", hsum + [small_pair], lands, late_plan)

    def adam_operands(n):
        shape = Wk[n].shape
        rows_blk = 512 if shape[1] % 512 == 0 else (shape[1] // 2 if shape[1] > 512 else shape[1])
        return [a.reshape(-1, shape[-1]) for a in (Wk[n], sbufs[n], kview(M[n], n), kview(V[n], n))], rows_blk, shape

    for pending_share in shares:
        share_wait(pending_share, token)
    partial = {}
    if L > 1:
        for n in BIG:
            flat, rows_blk, shape = adam_operands(n)
            partial[n] = _adamw(*flat, rows_blk, row_range=(shape[1], L * shape[1]), after=token)
    recv_b = _wait_copies("grad_chip_wait_last", send, recv, srcs, lands, late_plan,
                          partial[BIG[-1]][0] if partial else hsum[0])
    chip_sum_into(BIG[:nl], hsum, recv_b[:nl], 0)
    small_half = _sum_slots(recv_b[nl])
    (small_all,) = share(0, extra_in=[small_half], extra_out=[_sds(packed.shape, f32)],
                         extra_remote=[(lambda i, o, p: i[nb], lambda i, o, p: o[nb].at[p.c], "sib")],
                         extra_local=[(lambda i, o, p: i[nb], lambda i, o, p: o[nb].at[p.c])])
    grads = {n: kview(sbufs[n], n) for n in BIG}
    small_tot = _unpack(small_all.reshape(-1), small_shapes)
    for n, val in zip(small_names, small_tot):
        grads[n] = val
    grads["hgrn_lb_logits"] = _lower_bounds_bwd(hgrn_lb_logits, grads["hgrn_lb_logits"])
    shard_cols = conv_w.shape[2]
    grads["conv_w"] = lax.dynamic_slice_in_dim(grads["conv_w"].reshape(L, CONV_WIDTH, d_b), my_slot * shard_cols, shard_cols, axis=2)
    for n in SMALL:
        grads[n] = grads[n].reshape(W[n].shape)

    delta, new_m, new_v = {}, {}, {}
    for n in BIG:
        flat, rows_blk, shape = adam_operands(n)
        outs = _adamw(*flat, rows_blk, row_range=(0, shape[1]), prev=partial[n]) if partial else _adamw(*flat, rows_blk)
        delta[n], new_m[n], new_v[n], grads[n] = [kview(o.reshape(shape), n) for o in outs]
    shapes = [W[n].shape for n in SMALL]
    packs = [_pack([src[n] for n in SMALL], 8 * 128).reshape(-1, 128) for src in (W, grads, M, V)]
    d2, m2, v2, _ = _adamw(*packs, 4096)
    for dst, val in ((delta, d2), (new_m, m2), (new_v, v2)):
        for n, piece in zip(SMALL, _unpack(val.reshape(-1), shapes)):
            dst[n] = piece

    return (loss, grad_x, *[grads[n] for n in WEIGHTS], *[delta[n] for n in WEIGHTS],
            *[new_m[n] for n in WEIGHTS], *[new_v[n] for n in WEIGHTS])
```

```python
import math

import numpy as np
import jax
import jax.numpy as jnp
from jax import lax
from jax.experimental import pallas as pl
from jax.experimental.pallas import tpu as pltpu

f32 = jnp.float32
bf16 = jnp.bfloat16
HI = lax.Precision.HIGHEST
MESH = pl.DeviceIdType.MESH

EPS = 1e-6
HEAD = 128
A_CHUNK = 64
A_SUB = 16
A_INNER = 2
SUBLANES = 8
B_BLOCKS = 4
B_CHUNK = 256
CONV_WIDTH = 4
LRU_C = 8.0
C_GROUPS = 4
C_CHUNK = 128
C_INNER = 4
N_CHIPS = 4
ADAM_LR, ADAM_B1, ADAM_B2, ADAM_EPS, ADAM_WD, ADAM_STEP = 0.001, 0.9, 0.999, 1e-08, 0.01, 10
VMEM_LIMIT = 56 * 1024 * 1024


def _cparams(n_axes):
    return pltpu.CompilerParams(dimension_semantics=("arbitrary",) * n_axes, vmem_limit_bytes=VMEM_LIMIT)


def _sds(shape, dtype):
    return jax.ShapeDtypeStruct(tuple(shape), dtype)


def _full(shape):
    n = len(shape)
    return pl.BlockSpec(tuple(shape), lambda *_: (0,) * n)


def _resident(shape):
    n = len(shape)
    return pl.BlockSpec(tuple(shape), lambda *_: (0,) * n, pipeline_mode=pl.Buffered(1))


def _dot(a, b):
    return jnp.dot(a, b, preferred_element_type=f32)


def _dot_nt(a, b):
    return lax.dot_general(a, b, (((1,), (1,)), ((), ())), preferred_element_type=f32)


def _dot_tn(a, b):
    return lax.dot_general(a, b, (((0,), (0,)), ((), ())), preferred_element_type=f32)


def _silu(x):
    return x * jax.nn.sigmoid(x)


def _group_avg_matrix(n, group):
    idx = np.arange(n) // group
    return jnp.asarray((idx[:, None] == idx[None, :]).astype(np.float32) / group)


class _Place:
    def __init__(self):
        self.x, self.y, self.c = lax.axis_index("x"), lax.axis_index("y"), lax.axis_index("c")
        self.slot = 2 * self.x + self.y

    def peer(self, kind):
        x, y, c = self.x, self.y, self.c
        return {"sib": (x, y, 1 - c), "fx": (1 - x, y, c), "fy": (x, 1 - y, c), "fxy": (1 - x, 1 - y, c)}[kind]

    def peer_slot(self, kind):
        x, y = self.x, self.y
        return {"fx": 2 * (1 - x) + y, "fy": 2 * x + (1 - y), "fxy": 2 * (1 - x) + (1 - y)}[kind]


CHIP_KINDS = ("fx", "fy", "fxy")


def _exchange(name, ins, outs, remote, local=(), aliases=None):
    n_in, n_out, n_r, n_l = len(ins), len(outs), len(remote), len(local)

    def body(*refs):
        in_refs, out_refs = refs[:n_in], refs[n_in:n_in + n_out]
        send, recv, lsem = refs[n_in + n_out:]
        p = _Place()
        lcopies = []
        for t, (src, dst) in enumerate(local):
            cp = pltpu.make_async_copy(src(in_refs, out_refs, p), dst(in_refs, out_refs, p), lsem.at[t])
            cp.start()
            lcopies.append(cp)
        copies = []
        for t, (src, dst, kind) in enumerate(remote):
            cp = pltpu.make_async_remote_copy(
                src_ref=src(in_refs, out_refs, p), dst_ref=dst(in_refs, out_refs, p),
                send_sem=send.at[t], recv_sem=recv.at[t], device_id=p.peer(kind), device_id_type=MESH)
            cp.start()
            copies.append(cp)
        for cp in copies:
            cp.wait_recv()
        for cp in copies:
            cp.wait_send()
        for cp in lcopies:
            cp.wait()

    anyspec = pl.BlockSpec(memory_space=pl.ANY)
    res = pl.pallas_call(
        body, name=name, out_shape=tuple(outs),
        in_specs=[anyspec] * n_in, out_specs=tuple([anyspec] * n_out),
        scratch_shapes=[pltpu.SemaphoreType.DMA((n_r,)), pltpu.SemaphoreType.DMA((n_r,)),
                        pltpu.SemaphoreType.DMA((max(n_l, 1),))],
        input_output_aliases=aliases or {},
        compiler_params=pltpu.CompilerParams(has_side_effects=True),
    )(*ins)
    return list(res)


HBM_SPEC = pl.BlockSpec(memory_space=pltpu.HBM)
SEM_SPEC = pl.BlockSpec(memory_space=pltpu.SEMAPHORE)
ANY_SPEC = pl.BlockSpec(memory_space=pl.ANY)
DATAFLOW = pltpu.SideEffectType.DATAFLOW_SIDE_EFFECTING


def _in_hbm(a):
    return pltpu.with_memory_space_constraint(a, pltpu.HBM)


def _start_copies(name, srcs, lands, remote, after=None):
    n_s, n_l, n_r = len(srcs), len(lands), len(remote)
    extra = [] if after is None else [after]

    def body(*refs):
        src_refs, land_refs = refs[:n_s], refs[n_s:n_s + n_l]
        n_in = n_s + n_l + len(extra)
        send, recv = refs[n_in], refs[n_in + 1]
        token = refs[-1]
        p = _Place()
        for t, (src, dst, kind) in enumerate(remote):
            pltpu.make_async_remote_copy(
                src_ref=src(src_refs, land_refs, p), dst_ref=dst(src_refs, land_refs, p),
                send_sem=send.at[t], recv_sem=recv.at[t], device_id=p.peer(kind), device_id_type=MESH).start()
        token[...] = jnp.zeros_like(token)

    thru = [pltpu.HBM(a.shape, a.dtype) for a in lands]
    res = pl.pallas_call(
        body, name=name,
        out_shape=(pltpu.SemaphoreType.DMA((n_r,)), pltpu.SemaphoreType.DMA((n_r,)), *thru, _sds((8, 128), f32)),
        in_specs=[ANY_SPEC] * n_s + [HBM_SPEC] * n_l + [ANY_SPEC] * len(extra),
        out_specs=(SEM_SPEC, SEM_SPEC, *([HBM_SPEC] * n_l), pl.BlockSpec(memory_space=pltpu.VMEM)),
        input_output_aliases={n_s + i: 2 + i for i in range(n_l)},
        compiler_params=pltpu.CompilerParams(has_side_effects=DATAFLOW),
    )(*srcs, *[_in_hbm(a) for a in lands], *extra)
    return res[0], res[1], list(srcs), list(res[2:2 + n_l]), res[-1]


def _wait_copies(name, send, recv, srcs, lands, remote, after):
    n_s, n_l = len(srcs), len(lands)

    def body(*refs):
        src_refs, land_refs = refs[:n_s], refs[n_s:n_s + n_l]
        send_ref, recv_ref = refs[n_s + n_l], refs[n_s + n_l + 1]
        p = _Place()
        for t, (src, dst, kind) in enumerate(remote):
            cp = pltpu.make_async_remote_copy(
                src_ref=src(src_refs, land_refs, p), dst_ref=dst(src_refs, land_refs, p),
                send_sem=send_ref.at[t], recv_sem=recv_ref.at[t], device_id=p.peer(kind), device_id_type=MESH)
            cp.wait_send()
            cp.wait_recv()

    res = pl.pallas_call(
        body, name=name, out_shape=tuple(pltpu.HBM(a.shape, a.dtype) for a in lands),
        in_specs=[ANY_SPEC] * n_s + [HBM_SPEC] * n_l + [SEM_SPEC, SEM_SPEC, ANY_SPEC],
        out_specs=tuple([HBM_SPEC] * n_l),
        input_output_aliases={n_s + i: i for i in range(n_l)},
        compiler_params=pltpu.CompilerParams(has_side_effects=DATAFLOW),
    )(*srcs, *lands, send, recv, after)
    return list(res)


def _half(ref, c):
    n2 = ref.shape[0] // 2
    return ref.at[pl.ds(c * n2, n2)]


def _gather_ici_plan(n):
    def view(a):
        return lambda s, o, p: _half(o[a].at[p.slot], p.c)

    return [(view(a), view(a), kind) for a in range(n) for kind in CHIP_KINDS]


def _gather_d2d_plan(n):
    remote = []
    for a in range(n):
        for kind in CHIP_KINDS:
            view = lambda i, o, p, a=a, kind=kind: _half(o[a].at[p.peer_slot(kind)], p.c)
            remote.append((view, view, "sib"))
    return remote


def _gather_d2d(name, lands):
    n = len(lands)
    outs = [_sds(g.shape, g.dtype) for g in lands]
    return _exchange(name, list(lands), outs, _gather_d2d_plan(n), aliases={a: a for a in range(n)})


def _cast_place(weights, layer, slot_arr, n_steps=4, after=None):
    n = len(weights)
    extra = [] if after is None else [after]

    def body(s_ref, *refs):
        outs = refs[n + len(extra):]
        for a in range(n):
            outs[a][...] = refs[a][...].astype(bf16)

    in_specs, out_specs, out_shape = [], [], []
    for w in weights:
        _, R, Cc = w.shape
        rt = R // n_steps
        in_specs.append(pl.BlockSpec((None, rt, Cc), lambda i, s: (layer, i, 0)))
        out_specs.append(pl.BlockSpec((None, rt, Cc), lambda i, s: (s[0], i, 0)))
        out_shape.append(_sds((N_CHIPS, R, Cc), bf16))
    gs = pltpu.PrefetchScalarGridSpec(num_scalar_prefetch=1, grid=(n_steps,), in_specs=in_specs + [ANY_SPEC] * len(extra),
                                      out_specs=tuple(out_specs))
    return list(pl.pallas_call(body, name="cast_place", grid_spec=gs, out_shape=tuple(out_shape),
                               compiler_params=_cparams(1))(slot_arr, *weights, *extra))


def _ffn_fwd(h, gain, wg, wu, wd, tm):
    T, D = h.shape
    nsh, F = wg.shape[0], wg.shape[1]
    nt = T // tm

    def body(h_ref, gain_ref, wg_ref, wu_ref, wd_ref, out_ref, gs_ref, us_ref):
        hv = h_ref[...]
        r = lax.rsqrt(jnp.mean(hv * hv, axis=-1, keepdims=True) + EPS)
        xn = (hv * r * gain_ref[...]).astype(bf16)
        acc = None
        for k in range(nsh):
            g = _dot_nt(xn, wg_ref[k])
            u = _dot_nt(xn, wu_ref[k])
            gs_ref[k] = g.astype(bf16)
            us_ref[k] = u.astype(bf16)
            part = _dot((_silu(g) * u).astype(bf16), wd_ref[k])
            acc = part if acc is None else acc + part
        out_ref[...] = hv + 0.5 * acc

    sav = pl.BlockSpec((nsh, tm, F), lambda i: (0, i, 0))
    return pl.pallas_call(
        body, name="ffn_fwd", grid=(nt,),
        in_specs=[pl.BlockSpec((tm, D), lambda i: (i, 0)), _full((1, D)), _resident((nsh, F, D)), _resident((nsh, F, D)),
                  _resident((nsh, F, D))],
        out_specs=(pl.BlockSpec((tm, D), lambda i: (i, 0)), sav, sav),
        out_shape=(_sds((T, D), f32), _sds((nsh, T, F), bf16), _sds((nsh, T, F), bf16)),
        compiler_params=_cparams(1),
    )(h, gain, wg, wu, wd)


def _ffn_bwd_dgrad(h, gain, dout, gs, us, wg, wu, wd, tm):
    T, D = h.shape
    nsh, F = wg.shape[0], wg.shape[1]
    nt = T // tm

    def body(h_ref, gain_ref, dout_ref, gs_ref, us_ref, wg_ref, wu_ref, wd_ref,
             dh_ref, dgain_ref, dg_ref, du_ref, xn_ref, dob_ref):
        @pl.when(pl.program_id(0) == 0)
        def _():
            dgain_ref[...] = jnp.zeros_like(dgain_ref)

        hv = h_ref[...]
        r = lax.rsqrt(jnp.mean(hv * hv, axis=-1, keepdims=True) + EPS)
        xh = hv * r
        xn_ref[...] = (xh * gain_ref[...]).astype(bf16)
        dv = dout_ref[...]
        dob = (0.5 * dv).astype(bf16)
        dob_ref[...] = dob
        dxn = None
        for k in range(nsh):
            da = _dot_nt(dob, wd_ref[k])
            g = gs_ref[k].astype(f32)
            u = us_ref[k].astype(f32)
            sg = jax.nn.sigmoid(g)
            dg = (da * u * (sg * (1.0 + g * (1.0 - sg)))).astype(bf16)
            du = (da * (g * sg)).astype(bf16)
            dg_ref[k] = dg
            du_ref[k] = du
            part = _dot(dg, wg_ref[k]) + _dot(du, wu_ref[k])
            dxn = part if dxn is None else dxn + part
        dgain_ref[...] += jnp.sum(dxn * xh, axis=0, keepdims=True)
        dxh = dxn * gain_ref[...]
        dh_ref[...] = dv + r * (dxh - xh * jnp.mean(dxh * xh, axis=-1, keepdims=True))

    tok = pl.BlockSpec((tm, D), lambda i: (i, 0))
    sav = pl.BlockSpec((nsh, tm, F), lambda i: (0, i, 0))
    return pl.pallas_call(
        body, name="ffn_bwd_dgrad", grid=(nt,),
        in_specs=[tok, _full((1, D)), tok, sav, sav, _resident((nsh, F, D)), _resident((nsh, F, D)), _resident((nsh, F, D))],
        out_specs=(tok, _full((1, D)), sav, sav, tok, tok),
        out_shape=(_sds((T, D), f32), _sds((1, D), f32), _sds((nsh, T, F), bf16), _sds((nsh, T, F), bf16),
                   _sds((T, D), bf16), _sds((T, D), bf16)),
        compiler_params=_cparams(1),
    )(h, gain, dout, gs, us, wg, wu, wd)


def _ffn_bwd_wgrad(xn, dob, gs, us, dg, du, tm):
    T, D = xn.shape
    nsh, F = gs.shape[0], gs.shape[2]
    nt = T // tm

    def body(xn_ref, dob_ref, gs_ref, us_ref, dg_ref, du_ref, dwg_ref, dwu_ref, dwd_ref, ag_ref, au_ref, ad_ref):
        i = pl.program_id(1)

        @pl.when(i == 0)
        def _():
            ag_ref[...] = jnp.zeros_like(ag_ref)
            au_ref[...] = jnp.zeros_like(au_ref)
            ad_ref[...] = jnp.zeros_like(ad_ref)

        xn_v = xn_ref[...]
        ag_ref[...] += _dot_tn(dg_ref[...], xn_v)
        au_ref[...] += _dot_tn(du_ref[...], xn_v)
        g = gs_ref[...].astype(f32)
        a = (_silu(g) * us_ref[...].astype(f32)).astype(bf16)
        ad_ref[...] += _dot_tn(a, dob_ref[...])

        @pl.when(i == nt - 1)
        def _():
            dwg_ref[...] = ag_ref[...].astype(bf16)
            dwu_ref[...] = au_ref[...].astype(bf16)
            dwd_ref[...] = ad_ref[...].astype(bf16)

    tok = pl.BlockSpec((tm, D), lambda k, i: (i, 0))
    sav = pl.BlockSpec((None, tm, F), lambda k, i: (k, i, 0))
    wdspec = pl.BlockSpec((None, F, D), lambda k, i: (k, 0, 0))
    return pl.pallas_call(
        body, name="ffn_bwd_wgrad", grid=(nsh, nt),
        in_specs=[tok, tok, sav, sav, sav, sav],
        out_specs=(wdspec, wdspec, wdspec),
        out_shape=(_sds((nsh, F, D), bf16),) * 3,
        scratch_shapes=[pltpu.VMEM((F, D), f32)] * 3,
        compiler_params=_cparams(2),
    )(xn, dob, gs, us, dg, du)


def _proj_in_fwd(h, gain, w_in, tm):
    T, D = h.shape
    nsh, N = w_in.shape[0], w_in.shape[2]
    nt = T // tm

    def body(h_ref, gain_ref, w_ref, z_ref):
        hv = h_ref[...]
        r = lax.rsqrt(jnp.mean(hv * hv, axis=-1, keepdims=True) + EPS)
        xn = (hv * r * gain_ref[...]).astype(bf16)
        for k in range(nsh):
            z_ref[:, k * N:(k + 1) * N] = _dot(xn, w_ref[k])

    return pl.pallas_call(
        body, name="proj_in_fwd", grid=(nt,),
        in_specs=[pl.BlockSpec((tm, D), lambda i: (i, 0)), _full((1, D)), _full((nsh, D, N))],
        out_specs=pl.BlockSpec((tm, nsh * N), lambda i: (i, 0)),
        out_shape=_sds((T, nsh * N), f32),
        compiler_params=_cparams(1),
    )(h, gain, w_in)


def _proj_in_bwd_dgrad(h, gain, dres, dz, w_in, tm):
    T, D = h.shape
    nsh, N = w_in.shape[0], w_in.shape[2]
    nt = T // tm

    def body(h_ref, gain_ref, dres_ref, dz_ref, w_ref, dh_ref, dgain_ref, xn_ref):
        @pl.when(pl.program_id(0) == 0)
        def _():
            dgain_ref[...] = jnp.zeros_like(dgain_ref)

        dxn = _dot_nt(dz_ref[:, 0:N], w_ref[0])
        for k in range(1, nsh):
            dxn = dxn + _dot_nt(dz_ref[:, k * N:(k + 1) * N], w_ref[k])
        hv = h_ref[...]
        r = lax.rsqrt(jnp.mean(hv * hv, axis=-1, keepdims=True) + EPS)
        xh = hv * r
        xn_ref[...] = (xh * gain_ref[...]).astype(bf16)
        dgain_ref[...] += jnp.sum(dxn * xh, axis=0, keepdims=True)
        dxh = dxn * gain_ref[...]
        dh_ref[...] = dres_ref[...] + r * (dxh - xh * jnp.mean(dxh * xh, axis=-1, keepdims=True))

    tok = pl.BlockSpec((tm, D), lambda i: (i, 0))
    return pl.pallas_call(
        body, name="proj_in_bwd_dgrad", grid=(nt,),
        in_specs=[tok, _full((1, D)), tok, pl.BlockSpec((tm, nsh * N), lambda i: (i, 0)), _full((nsh, D, N))],
        out_specs=(tok, _full((1, D)), tok),
        out_shape=(_sds((T, D), f32), _sds((1, D), f32), _sds((T, D), bf16)),
        compiler_params=_cparams(1),
    )(h, gain, dres, dz, w_in)


def _proj_in_bwd_wgrad(xn, dz, nsh):
    T, D = xn.shape
    N = dz.shape[1] // nsh

    def body(xn_ref, dz_ref, dw_ref):
        dw_ref[...] = _dot_tn(xn_ref[...], dz_ref[...]).astype(bf16)

    return pl.pallas_call(
        body, name="proj_in_bwd_wgrad", grid=(nsh,),
        in_specs=[_full((T, D)), pl.BlockSpec((T, N), lambda k: (0, k))],
        out_specs=pl.BlockSpec((None, D, N), lambda k: (k, 0, 0)),
        out_shape=_sds((nsh, D, N), bf16),
        compiler_params=_cparams(1),
    )(xn, dz)


def _proj_out_fwd(h, oa, ob, oc, w_out, tm):
    T, D = h.shape
    nsh, R = w_out.shape[0], w_out.shape[1]
    da, db = oa.shape[1], ob.shape[1]
    nt = T // tm

    def body(h_ref, oa_ref, ob_ref, oc_ref, w_ref, out_ref):
        w = w_ref[...].reshape(nsh * R, D)
        out_ref[...] = (h_ref[...] + _dot(oa_ref[...], w[:da]) + _dot(ob_ref[...], w[da:da + db])
                        + _dot(oc_ref[...], w[da + db:]))

    def tok(n):
        return pl.BlockSpec((tm, n), lambda i: (i, 0))

    return pl.pallas_call(
        body, name="proj_out_fwd", grid=(nt,),
        in_specs=[tok(D), tok(da), tok(db), tok(oc.shape[1]), _full((nsh, R, D))],
        out_specs=tok(D), out_shape=_sds((T, D), f32),
        compiler_params=_cparams(1),
    )(h, oa, ob, oc, w_out)


def _proj_out_bwd(dh, oa, ob, oc, w_out, tm):
    T, D = dh.shape
    nsh, R = w_out.shape[0], w_out.shape[1]
    da, db, dc = oa.shape[1], ob.shape[1], oc.shape[1]
    nt = T // tm

    def body(dh_ref, oa_ref, ob_ref, oc_ref, w_ref, doa_ref, dob_ref, doc_ref, dw_ref, acc_ref):
        i = pl.program_id(0)

        @pl.when(i == 0)
        def _():
            acc_ref[...] = jnp.zeros_like(acc_ref)

        d = dh_ref[...].astype(bf16)
        w = w_ref[...].reshape(nsh * R, D)
        dm = _dot_nt(d, w)
        doa_ref[...] = dm[:, :da]
        dob_ref[...] = dm[:, da:da + db]
        doc_ref[...] = dm[:, da + db:]
        acc_ref[pl.ds(0, da), :] += _dot_tn(oa_ref[...], d)
        acc_ref[pl.ds(da, db), :] += _dot_tn(ob_ref[...], d)
        acc_ref[pl.ds(da + db, dc), :] += _dot_tn(oc_ref[...], d)

        @pl.when(i == nt - 1)
        def _():
            dw_ref[...] = acc_ref[...].astype(bf16).reshape(nsh, R, D)

    def tok(n):
        return pl.BlockSpec((tm, n), lambda i: (i, 0))

    wspec = _full((nsh, R, D))
    return pl.pallas_call(
        body, name="proj_out_bwd", grid=(nt,),
        in_specs=[tok(D), tok(da), tok(db), tok(dc), wspec],
        out_specs=(tok(da), tok(db), tok(dc), wspec),
        out_shape=(_sds((T, da), f32), _sds((T, db), f32), _sds((T, dc), f32), _sds((nsh, R, D), bf16)),
        scratch_shapes=[pltpu.VMEM((nsh * R, D), f32)],
        compiler_params=_cparams(1),
    )(dh, oa, ob, oc, w_out)


def _head_sum(m, n_heads):
    parts = []
    for hd in range(n_heads):
        s = jnp.sum(m[:, hd * HEAD:(hd + 1) * HEAD], axis=-1, keepdims=True)
        parts.append(jnp.broadcast_to(s, (m.shape[0], HEAD)))
    return parts[0] if n_heads == 1 else jnp.concatenate(parts, axis=1)


def _cat(parts, axis):
    return parts[0] if len(parts) == 1 else jnp.concatenate(parts, axis=axis)


def _hgrn_block(q, fl, iv, lb, states, tri, n_heads, n_inner):
    C = q.shape[0] // n_inner
    qs = _silu(q)
    forget = lb + (1.0 - lb) * jax.nn.sigmoid(fl)
    kk = 1.0 - forget
    logf = jnp.log(forget)
    b = jnp.dot(tri, logf, precision=HI, preferred_element_type=f32)
    vb = iv.astype(bf16)
    heads = [slice(hd * HEAD, (hd + 1) * HEAD) for hd in range(n_heads)]
    n_sub = C // A_SUB

    off, qe, kd, dec = {}, [], [], []
    for j in range(n_inner):
        c0 = j * C
        for blk in range(1, n_sub):
            lo = c0 + blk * A_SUB
            piv = b[lo:lo + 1]
            qt = (qs[lo:lo + A_SUB] * jnp.exp(b[lo:lo + A_SUB] - piv)).astype(bf16)
            kt = (kk[c0:lo] * jnp.exp(piv - b[c0:lo])).astype(bf16)
            parts = []
            for sl in heads:
                sc = _dot_nt(qt[:, sl], kt[:, sl])
                parts.append(_dot(sc.astype(bf16), vb[c0:lo, sl]))
            off[(j, blk)] = _cat(parts, 1)
        bj = b[c0:c0 + C]
        b_end = bj[C - 1:C]
        qe.append((qs[c0:c0 + C] * jnp.exp(bj)).astype(bf16))
        kd.append((kk[c0:c0 + C] * jnp.exp(b_end - bj)).astype(bf16))
        dec.append(jnp.exp(b_end))

    outs = []
    for j in range(n_inner):
        for blk in range(n_sub):
            lo = j * C + blk * A_SUB
            groups = [off[(j, blk)][r0:r0 + SUBLANES] if blk > 0 else None for r0 in range(0, A_SUB, SUBLANES)]
            for s in range(A_SUB):
                first = (s // SUBLANES) * SUBLANES
                n_rows = A_SUB - first
                row = lax.broadcasted_iota(jnp.int32, (n_rows, 1), 0) + first
                gate = jnp.where(row >= s, 0.0, -1e30)
                r = slice(lo + first, lo + A_SUB)
                m = qs[r] * jnp.exp((b[r] - b[lo + s:lo + s + 1]) + gate) * kk[lo + s:lo + s + 1]
                term = _head_sum(m, n_heads) * iv[lo + s:lo + s + 1]
                for gi in range(first // SUBLANES, A_SUB // SUBLANES):
                    piece = term[gi * SUBLANES - first:(gi + 1) * SUBLANES - first]
                    groups[gi] = piece if groups[gi] is None else groups[gi] + piece
            outs.extend(groups)
    o = jnp.concatenate(outs, axis=0)

    inter = []
    states = list(states)
    for j in range(n_inner):
        c0 = j * C
        parts = []
        for hd, sl in enumerate(heads):
            st = states[hd]
            parts.append(_dot_nt(qe[j][:, sl], st.astype(bf16)))
            states[hd] = dec[j][:, sl] * st + _dot_tn(vb[c0:c0 + C, sl], kd[j][:, sl])
        inter.append(_cat(parts, 1))
    return o + _cat(inter, 0), tuple(states)


def _hgrn_gate(o, g, gain, n_heads):
    ms = _head_sum(o * o, n_heads) * (1.0 / HEAD)
    return o * lax.rsqrt(ms + EPS) * gain * _silu(g)


def _tri_matrix(c, n_inner):
    idx = np.arange(c * n_inner)
    same = (idx[:, None] // c) == (idx[None, :] // c)
    return jnp.asarray((same & (idx[:, None] >= idx[None, :])).astype(np.float32))


def _hgrn_fwd(z, lb, gain, d_a):
    T = z.shape[0]
    C = A_CHUNK * A_INNER
    nc = T // C
    nh = d_a // HEAD
    tri = _tri_matrix(A_CHUNK, A_INNER)

    def body(q_ref, f_ref, i_ref, g_ref, lb_ref, gain_ref, tri_ref, out_ref, o_ref, st_ref, carry_ref):
        @pl.when(pl.program_id(0) == 0)
        def _():
            carry_ref[...] = jnp.zeros_like(carry_ref)

        states = tuple(carry_ref[hd] for hd in range(nh))
        st_ref[...] = carry_ref[...]
        o, new_states = _hgrn_block(q_ref[...], f_ref[...], i_ref[...], lb_ref[...], states, tri_ref[...], nh, A_INNER)
        o_ref[...] = o
        out_ref[...] = _hgrn_gate(o, g_ref[...], gain_ref[...], nh).astype(bf16)
        for hd in range(nh):
            carry_ref[hd] = new_states[hd]

    def col(j):
        return pl.BlockSpec((C, d_a), lambda c, j=j: (c, j))

    tok = pl.BlockSpec((C, d_a), lambda c: (c, 0))
    return pl.pallas_call(
        body, name="hgrn_fwd", grid=(nc,),
        in_specs=[col(0), col(1), col(2), col(3), _full((1, d_a)), _full((1, d_a)), _full((C, C))],
        out_specs=(tok, tok, pl.BlockSpec((None, nh, HEAD, HEAD), lambda c: (c, 0, 0, 0))),
        out_shape=(_sds((T, d_a), bf16), _sds((T, d_a), f32), _sds((nc, nh, HEAD, HEAD), f32)),
        scratch_shapes=[pltpu.VMEM((nh, HEAD, HEAD), f32)],
        compiler_params=_cparams(1),
    )(z, z, z, z, lb, gain, tri)


def _hgrn_bwd(z, lb, gain, o_pre, states, dout, d_a):
    T = z.shape[0]
    C = A_CHUNK * A_INNER
    nc = T // C
    nh = d_a // HEAD
    tri = _tri_matrix(A_CHUNK, A_INNER)

    def body(q_ref, f_ref, i_ref, g_ref, lb_ref, gain_ref, tri_ref, o_ref, st_ref, do_ref,
             dz_ref, dlb_ref, dgain_ref, carry_ref):
        @pl.when(pl.program_id(0) == 0)
        def _():
            carry_ref[...] = jnp.zeros_like(carry_ref)
            dlb_ref[...] = jnp.zeros_like(dlb_ref)
            dgain_ref[...] = jnp.zeros_like(dgain_ref)

        _, vjp_gate = jax.vjp(lambda o, g, gv: _hgrn_gate(o, g, gv, nh), o_ref[...], g_ref[...], gain_ref[...])
        d_o, dg, dgain = vjp_gate(do_ref[...])
        tri_v = tri_ref[...]

        def fn(q, fl, iv, lbv, sts):
            return _hgrn_block(q, fl, iv, lbv, sts, tri_v, nh, A_INNER)

        states_in = tuple(st_ref[hd] for hd in range(nh))
        _, vjp = jax.vjp(fn, q_ref[...], f_ref[...], i_ref[...], lb_ref[...], states_in)
        dstates = tuple(carry_ref[hd] for hd in range(nh))
        dq, df, di, dlb, dst = vjp((d_o, dstates))
        dz_ref[:, 0:d_a] = dq.astype(bf16)
        dz_ref[:, d_a:2 * d_a] = df.astype(bf16)
        dz_ref[:, 2 * d_a:3 * d_a] = di.astype(bf16)
        dz_ref[:, 3 * d_a:4 * d_a] = dg.astype(bf16)
        dlb_ref[...] += dlb
        dgain_ref[...] += dgain
        for hd in range(nh):
            carry_ref[hd] = dst[hd]

    def col(j):
        return pl.BlockSpec((C, d_a), lambda c, j=j: (nc - 1 - c, j))

    tok = pl.BlockSpec((C, d_a), lambda c: (nc - 1 - c, 0))
    return pl.pallas_call(
        body, name="hgrn_bwd", grid=(nc,),
        in_specs=[col(0), col(1), col(2), col(3), _full((1, d_a)), _full((1, d_a)), _full((C, C)), tok,
                  pl.BlockSpec((None, nh, HEAD, HEAD), lambda c: (nc - 1 - c, 0, 0, 0)), tok],
        out_specs=(pl.BlockSpec((C, 4 * d_a), lambda c: (nc - 1 - c, 0)), _full((1, d_a)), _full((1, d_a))),
        out_shape=(_sds(z.shape, bf16), _sds((1, d_a), f32), _sds((1, d_a), f32)),
        scratch_shapes=[pltpu.VMEM((nh, HEAD, HEAD), f32)],
        compiler_params=_cparams(1),
    )(z, z, z, z, lb, gain, tri, o_pre, states, dout)


def _one_minus_exp(x):
    series = -x * (1.0 + x * (0.5 + x * (1.0 / 6.0 + x * (1.0 / 24.0))))
    return jnp.where(x > -0.03, series, 1.0 - jnp.exp(x))


def _lru_pre(xc, wa, ba, wx, bx, lam):
    xb16 = xc.astype(bf16)
    r = jax.nn.sigmoid(_dot(xb16, wa.astype(bf16)) + ba)
    gi = jax.nn.sigmoid(_dot(xb16, wx.astype(bf16)) + bx)
    log_a = -LRU_C * r * jax.nn.softplus(-lam)
    a = jnp.exp(log_a)
    mult = jnp.sqrt(_one_minus_exp(2.0 * log_a))
    return a, mult * gi * xc


def _lru_post(h, gate, gain, avg):
    y = h * jax.nn.gelu(gate)
    ms = _group_mean(y * y, avg)
    return y * lax.rsqrt(ms + EPS) * gain


def _shift_down(x, d, prev):
    row = lax.broadcasted_iota(jnp.int32, x.shape, 0)
    return jnp.where(row >= d, pltpu.roll(x, d, 0), pltpu.roll(prev, d, 0))


def _shift_up(x, d, nxt):
    n = x.shape[0]
    row = lax.broadcasted_iota(jnp.int32, x.shape, 0)
    return jnp.where(row < n - d, pltpu.roll(x, n - d, 0), pltpu.roll(nxt, n - d, 0))


def _scan_rows(a, u, reverse):
    n = a.shape[0]
    row = lax.broadcasted_iota(jnp.int32, a.shape, 0)
    d = 1
    while d < n:
        shift, ok = (n - d, row < n - d) if reverse else (d, row >= d)
        su = jnp.where(ok, pltpu.roll(u, shift, 0), 0.0)
        sa = jnp.where(ok, pltpu.roll(a, shift, 0), 1.0)
        u = u + a * su
        a = a * sa
        d *= 2
    return a, u


def _conv(xb, xprev, cw, cb):
    xc = cb + cw[CONV_WIDTH - 1:CONV_WIDTH] * xb
    for d in range(1, CONV_WIDTH):
        xc = xc + cw[CONV_WIDTH - 1 - d:CONV_WIDTH - d] * _shift_down(xb, d, xprev)
    return xc


def _lru_fwd(z, col0, d_b, cw, cb, wa, ba, wx, bx, lam, gain, avg):
    T = z.shape[0]
    R = min(B_CHUNK, T)
    nr = T // R
    jb = col0 // d_b

    def body(xb_ref, gate_ref, cw_ref, cb_ref, wa_ref, ba_ref, wx_ref, bx_ref, lam_ref, gain_ref, avg_ref,
             out_ref, h_ref, xprev_ref, hprev_ref):
        @pl.when(pl.program_id(0) == 0)
        def _():
            xprev_ref[...] = jnp.zeros_like(xprev_ref)
            hprev_ref[...] = jnp.zeros_like(hprev_ref)

        xb = xb_ref[...]
        xc = _conv(xb, xprev_ref[...], cw_ref[...], cb_ref[...])
        a, u = _lru_pre(xc, wa_ref[...], ba_ref[...], wx_ref[...], bx_ref[...], lam_ref[...])
        acum, hl = _scan_rows(a, u, False)
        h = hl + acum * hprev_ref[R - 1:R, :]
        h_ref[...] = h
        out_ref[...] = _lru_post(h, gate_ref[...], gain_ref[...], avg_ref[...]).astype(bf16)
        xprev_ref[...] = xb
        hprev_ref[...] = h

    vec = _full((1, d_b))
    return pl.pallas_call(
        body, name="lru_fwd", grid=(nr,),
        in_specs=[pl.BlockSpec((R, d_b), lambda i: (i, jb)), pl.BlockSpec((R, d_b), lambda i: (i, jb + 1)),
                  _full((CONV_WIDTH, d_b)), vec, _full((d_b, d_b)), vec, _full((d_b, d_b)), vec, vec, vec, _full((d_b, d_b))],
        out_specs=(pl.BlockSpec((R, d_b), lambda i: (i, 0)), pl.BlockSpec((R, d_b), lambda i: (i, 0))),
        out_shape=(_sds((T, d_b), bf16), _sds((T, d_b), f32)),
        scratch_shapes=[pltpu.VMEM((R, d_b), f32), pltpu.VMEM((R, d_b), f32)],
        compiler_params=_cparams(1),
    )(z, z, cw, cb, wa, ba, wx, bx, lam, gain, avg)


def _lru_bwd(z, col0, d_b, hsave, dout, dz_buf, cw, cb, wa, ba, wx, bx, lam, gain, avg):
    T = z.shape[0]
    R = min(B_CHUNK, T)
    nr = T // R
    jb = col0 // d_b

    def body(xb_ref, xp_ref, gate_ref, h_ref, hp_ref, do_ref,
             cw_ref, cb_ref, wa_ref, ba_ref, wx_ref, bx_ref, lam_ref, gain_ref, avg_ref, dzin_ref,
             dz_ref, dcw_ref, dcb_ref, dwa_ref, dba_ref, dwx_ref, dbx_ref, dlam_ref, dgain_ref,
             gfirst_ref, afirst_ref, dxcn_ref):
        step = pl.program_id(0)
        first_in_time = step == nr - 1

        @pl.when(step == 0)
        def _():
            for r in (dcw_ref, dcb_ref, dwa_ref, dba_ref, dwx_ref, dbx_ref, dlam_ref, dgain_ref,
                      gfirst_ref, afirst_ref, dxcn_ref):
                r[...] = jnp.zeros_like(r)

        xb = xb_ref[...]
        keep = jnp.where(first_in_time, 0.0, 1.0)
        xprev = xp_ref[...] * keep
        hprev = hp_ref[...] * keep
        cw = cw_ref[...]
        xc = _conv(xb, xprev, cw, cb_ref[...])
        (a, _), vjp_pre = jax.vjp(_lru_pre, xc, wa_ref[...], ba_ref[...], wx_ref[...], bx_ref[...], lam_ref[...])
        h = h_ref[...]
        avg = avg_ref[...]
        _, vjp_post = jax.vjp(lambda hh, gg, gn: _lru_post(hh, gg, gn, avg), h, gate_ref[...], gain_ref[...])
        dh, dgate, dgain = vjp_post(do_ref[...])
        a_next = _shift_up(a, 1, jnp.broadcast_to(afirst_ref[0:1, :], a.shape))
        acum, gl = _scan_rows(a_next, dh, True)
        gtot = gl + acum * gfirst_ref[0:1, :]
        da = gtot * _shift_down(h, 1, hprev)
        dxc, dwa, dba, dwx, dbx, dlam = vjp_pre((da, gtot))
        dxcn = dxcn_ref[...]
        dxb = cw[CONV_WIDTH - 1:CONV_WIDTH] * dxc
        dcw_ref[CONV_WIDTH - 1:CONV_WIDTH, :] += jnp.sum(dxc * xb, axis=0, keepdims=True)
        for d in range(1, CONV_WIDTH):
            tap = CONV_WIDTH - 1 - d
            dxb = dxb + cw[tap:tap + 1] * _shift_up(dxc, d, dxcn)
            dcw_ref[tap:tap + 1, :] += jnp.sum(dxc * _shift_down(xb, d, xprev), axis=0, keepdims=True)
        dz_ref[:, 0:d_b] = dxb.astype(bf16)
        dz_ref[:, d_b:2 * d_b] = dgate.astype(bf16)
        dcb_ref[...] += jnp.sum(dxc, axis=0, keepdims=True)
        dwa_ref[...] += dwa
        dba_ref[...] += dba
        dwx_ref[...] += dwx
        dbx_ref[...] += dbx
        dlam_ref[...] += dlam
        dgain_ref[...] += dgain
        gfirst_ref[...] = jnp.broadcast_to(gtot[0:1, :], gfirst_ref.shape)
        afirst_ref[...] = jnp.broadcast_to(a[0:1, :], afirst_ref.shape)
        dxcn_ref[...] = dxc

    vec = _full((1, d_b))
    mat = _full((d_b, d_b))

    def cur(j):
        return pl.BlockSpec((R, d_b), lambda i, j=j: (nr - 1 - i, j))

    def prev(j):
        return pl.BlockSpec((R, d_b), lambda i, j=j: (jnp.maximum(nr - 2 - i, 0), j))

    return pl.pallas_call(
        body, name="lru_bwd", grid=(nr,),
        in_specs=[cur(jb), prev(jb), cur(jb + 1), cur(0), prev(0), cur(0),
                  _full((CONV_WIDTH, d_b)), vec, mat, vec, mat, vec, vec, vec, mat, ANY_SPEC],
        out_specs=(pl.BlockSpec((R, 2 * d_b), lambda i: (nr - 1 - i, col0 // (2 * d_b))), _full((CONV_WIDTH, d_b)), vec, mat, vec, mat, vec, vec, vec),
        out_shape=(_sds(dz_buf.shape, bf16), _sds((CONV_WIDTH, d_b), f32), _sds((1, d_b), f32), _sds((d_b, d_b), f32),
                   _sds((1, d_b), f32), _sds((d_b, d_b), f32), _sds((1, d_b), f32), _sds((1, d_b), f32), _sds((1, d_b), f32)),
        scratch_shapes=[pltpu.VMEM((8, d_b), f32), pltpu.VMEM((8, d_b), f32), pltpu.VMEM((R, d_b), f32)],
        input_output_aliases={15: 0},
        compiler_params=_cparams(1),
    )(z, z, z, hsave, hsave, dout, cw, cb, wa, ba, wx, bx, lam, gain, avg, dz_buf)


def _two_pass(x, m16):
    hi = x.astype(bf16)
    lo = (x - hi.astype(f32)).astype(bf16)
    return _dot(hi, m16) + _dot(lo, m16)


@jax.custom_vjp
def _group_mean(x, avg):
    return _two_pass(x, avg.astype(bf16))


def _group_mean_fwd(x, avg):
    return _group_mean(x, avg), avg


def _group_mean_bwd(avg, ct):
    return _two_pass(ct, avg.astype(bf16)), jnp.zeros_like(avg)


_group_mean.defvjp(_group_mean_fwd, _group_mean_bwd)


def _sgu_chunk(u_in, v_in, w, bexp, gain, avg, n_groups):
    C, d_c = u_in.shape
    gd = d_c // n_groups
    u = jax.nn.gelu(u_in)
    v = jax.nn.gelu(v_in)
    mu = _group_mean(v, avg)
    vc = v - mu
    var = _group_mean(vc * vc, avg)
    vh = (vc * lax.rsqrt(var + EPS)).astype(bf16)
    lane = lax.broadcasted_iota(jnp.int32, (1, d_c), 1)
    causal = lax.broadcasted_iota(jnp.int32, (C, C), 0) >= lax.broadcasted_iota(jnp.int32, (C, C), 1)
    zz = bexp
    for g in range(n_groups):
        wg = jnp.where(causal, w[g], 0.0).astype(bf16)
        zz = zz + jnp.where((lane >= g * gd) & (lane < (g + 1) * gd), _dot(wg, vh), 0.0)
    y = u * zz
    ms = _group_mean(y * y, avg)
    return y * lax.rsqrt(ms + EPS) * gain


def _sgu_inner(T):
    return C_INNER if T % (C_CHUNK * C_INNER) == 0 else 1


def _sgu_fwd(z, col0, d_c, w, bexp, gain, avg):
    T = z.shape[0]
    C = C_CHUNK
    n_in = _sgu_inner(T)
    R = C * n_in
    jb = col0 // d_c
    G = w.shape[0]

    def body(u_ref, v_ref, w_ref, b_ref, gain_ref, avg_ref, out_ref):
        w_v, b_v, gain_v, avg = w_ref[...], b_ref[...], gain_ref[...], avg_ref[...]
        for j in range(n_in):
            rows = pl.ds(j * C, C)
            out_ref[rows, :] = _sgu_chunk(u_ref[rows, :], v_ref[rows, :], w_v, b_v, gain_v, avg, G).astype(bf16)

    return pl.pallas_call(
        body, name="sgu_fwd", grid=(T // R,),
        in_specs=[pl.BlockSpec((R, d_c), lambda i: (i, jb)), pl.BlockSpec((R, d_c), lambda i: (i, jb + 1)),
                  _full((G, C, C)), _full((C, d_c)), _full((1, d_c)), _full((d_c, d_c))],
        out_specs=pl.BlockSpec((R, d_c), lambda i: (i, 0)),
        out_shape=_sds((T, d_c), bf16),
        compiler_params=_cparams(1),
    )(z, z, w, bexp, gain, avg)


def _sgu_bwd(z, col0, d_c, dout, dz_buf, w, bexp, gain, avg):
    T = z.shape[0]
    C = C_CHUNK
    n_in = _sgu_inner(T)
    R = C * n_in
    nc = T // R
    jb = col0 // d_c
    G = w.shape[0]
    gd = d_c // G

    def body(u_ref, v_ref, do_ref, w_ref, b_ref, gain_ref, avg_ref, dzin_ref, dz_ref, dw_ref, db_ref, dgain_ref, dbexp_ref):
        step = pl.program_id(0)

        @pl.when(step == 0)
        def _():
            dw_ref[...] = jnp.zeros_like(dw_ref)
            dgain_ref[...] = jnp.zeros_like(dgain_ref)
            dbexp_ref[...] = jnp.zeros_like(dbexp_ref)

        avg, w_v, b_v, gain_v = avg_ref[...], w_ref[...], b_ref[...], gain_ref[...]
        dw = dbexp = dgain = None
        for j in range(n_in):
            rows = pl.ds(j * C, C)
            _, vjp = jax.vjp(lambda a, b, c, d, e: _sgu_chunk(a, b, c, d, e, avg, G),
                             u_ref[rows, :], v_ref[rows, :], w_v, b_v, gain_v)
            du, dv, dw_j, dbexp_j, dgain_j = vjp(do_ref[rows, :])
            dz_ref[rows, 0:d_c] = du.astype(bf16)
            dz_ref[rows, d_c:2 * d_c] = dv.astype(bf16)
            dw = dw_j if dw is None else dw + dw_j
            dbexp = dbexp_j if dbexp is None else dbexp + dbexp_j
            dgain = dgain_j if dgain is None else dgain + dgain_j
        dw_ref[...] += dw
        dbexp_ref[...] += dbexp
        dgain_ref[...] += dgain

        @pl.when(step == nc - 1)
        def _():
            lane = lax.broadcasted_iota(jnp.int32, (1, d_c), 1)
            acc = dbexp_ref[...]
            for g in range(G):
                sel = jnp.where((lane >= g * gd) & (lane < (g + 1) * gd), acc, 0.0)
                db_ref[:, g:g + 1] = jnp.sum(sel, axis=1, keepdims=True)

    return pl.pallas_call(
        body, name="sgu_bwd", grid=(nc,),
        in_specs=[pl.BlockSpec((R, d_c), lambda i: (i, jb)), pl.BlockSpec((R, d_c), lambda i: (i, jb + 1)),
                  pl.BlockSpec((R, d_c), lambda i: (i, 0)),
                  _full((G, C, C)), _full((C, d_c)), _full((1, d_c)), _full((d_c, d_c)), ANY_SPEC],
        out_specs=(pl.BlockSpec((R, 2 * d_c), lambda i: (i, col0 // (2 * d_c))), _full((G, C, C)), _full((C, G)), _full((1, d_c))),
        out_shape=(_sds(dz_buf.shape, bf16), _sds((G, C, C), f32), _sds((C, G), f32), _sds((1, d_c), f32)),
        scratch_shapes=[pltpu.VMEM((C, d_c), f32)],
        input_output_aliases={7: 0},
        compiler_params=_cparams(1),
    )(z, z, dout, w, bexp, gain, avg, dz_buf)


def _loss_head(h, gain, target, tm):
    T, D = h.shape
    nt = T // tm

    def body(h_ref, gain_ref, tgt_ref, dh_ref, loss_ref, dgain_ref):
        @pl.when(pl.program_id(0) == 0)
        def _():
            loss_ref[...] = jnp.zeros_like(loss_ref)
            dgain_ref[...] = jnp.zeros_like(dgain_ref)

        hv = h_ref[...]
        gain_v = gain_ref[...]
        r = lax.rsqrt(jnp.mean(hv * hv, axis=-1, keepdims=True) + EPS)
        xh = hv * r
        e = xh * gain_v - tgt_ref[...]
        loss_ref[...] += 0.5 * jnp.sum(jnp.mean(e * e, axis=-1, keepdims=True), axis=0, keepdims=True)
        dy = e * (1.0 / D)
        dgain_ref[...] += jnp.sum(dy * xh, axis=0, keepdims=True)
        dxh = dy * gain_v
        dh_ref[...] = r * (dxh - xh * jnp.mean(dxh * xh, axis=-1, keepdims=True))

    tok = pl.BlockSpec((tm, D), lambda i: (i, 0))
    return pl.pallas_call(
        body, name="loss_head", grid=(nt,),
        in_specs=[tok, _full((1, D)), tok],
        out_specs=(tok, _full((1, 128)), _full((1, D))),
        out_shape=(_sds((T, D), f32), _sds((1, 128), f32), _sds((1, D), f32)),
        compiler_params=_cparams(1),
    )(h, gain, target)


def _lower_bounds_fn(logits):
    n = logits.shape[0]
    mx = jnp.max(logits, axis=0, keepdims=True)
    ex = jnp.exp(logits - mx)
    soft = ex / jnp.sum(ex, axis=0, keepdims=True)
    rows = [jnp.zeros_like(soft[0:1])]
    for l in range(1, n):
        rows.append(rows[-1] + soft[l:l + 1])
    return jnp.concatenate(rows, axis=0)


def _lower_bounds(logits):
    def body(x_ref, o_ref):
        o_ref[...] = _lower_bounds_fn(x_ref[...])

    return pl.pallas_call(body, name="lower_bounds", out_shape=_sds(logits.shape, f32))(logits)


def _lower_bounds_bwd(logits, dlb):
    def body(x_ref, d_ref, o_ref):
        _, vjp = jax.vjp(_lower_bounds_fn, x_ref[...])
        o_ref[...] = vjp(d_ref[...])[0]

    return pl.pallas_call(body, name="lower_bounds_bwd", out_shape=_sds(logits.shape, f32))(logits, dlb)


def _adamw(w, g, m, v, rows_blk, row_range=None, prev=(), after=None):
    R, Cc = w.shape
    lo, hi = (0, R) if row_range is None else row_range
    span = hi - lo
    rb = span if (span <= rows_blk and lo % span == 0) else math.gcd(math.gcd(span, lo), rows_blk)
    extra = list(prev) + ([] if after is None else [after])

    def body(w_ref, g_ref, m_ref, v_ref, *rest):
        d_ref, nm_ref, nv_ref, go_ref = rest[len(extra):]
        gv = g_ref[...]
        m2 = ADAM_B1 * m_ref[...] + (1.0 - ADAM_B1) * gv
        v2 = ADAM_B2 * v_ref[...] + (1.0 - ADAM_B2) * (gv * gv)
        m_hat = m2 / (1.0 - ADAM_B1 ** ADAM_STEP)
        v_hat = v2 / (1.0 - ADAM_B2 ** ADAM_STEP)
        d_ref[...] = -ADAM_LR * (m_hat / (jnp.sqrt(v_hat) + ADAM_EPS) + ADAM_WD * w_ref[...])
        nm_ref[...] = m2
        nv_ref[...] = v2
        go_ref[...] = gv

    first = lo // rb
    spec = pl.BlockSpec((rb, Cc), lambda i: (i + first, 0))
    return pl.pallas_call(
        body, name="adamw", grid=((hi - lo) // rb,),
        in_specs=[spec] * 4 + [ANY_SPEC] * len(extra), out_specs=(spec,) * 4, out_shape=(_sds((R, Cc), f32),) * 4,
        input_output_aliases={4 + j: j for j in range(len(prev))},
        compiler_params=_cparams(1),
    )(w, g, m, v, *extra)


def _pair_sum(grads, recv, c_arr):
    n = len(grads)
    nsh = grads[0].shape[0]

    def body(c_ref, *refs):
        for a in range(n):
            refs[2 * n + a][...] = (refs[a][...].astype(f32) + refs[n + a][...].astype(f32)).astype(bf16)

    g_specs, r_specs, out_shape = [], [], []
    for g in grads:
        _, R, Cc = g.shape
        r2 = R // 2
        g_specs.append(pl.BlockSpec((None, r2, Cc), lambda s, c: (s, c[0], 0)))
        r_specs.append(pl.BlockSpec((None, r2, Cc), lambda s, c: (s, 0, 0)))
        out_shape.append(_sds((nsh, r2, Cc), bf16))
    gs = pltpu.PrefetchScalarGridSpec(num_scalar_prefetch=1, grid=(nsh,), in_specs=g_specs + r_specs, out_specs=tuple(r_specs))
    return list(pl.pallas_call(body, name="pair_sum", grid_spec=gs, out_shape=tuple(out_shape),
                               compiler_params=_cparams(1))(c_arr, *grads, *recv))


def _add(a, b):
    def body(a_ref, b_ref, o_ref):
        o_ref[...] = a_ref[...] + b_ref[...]

    return pl.pallas_call(body, name="pair_sum_small", out_shape=_sds(a.shape, f32))(a, b)


def _chip_sum(hsum, recv, bufs, slot_arr, c_arr, layer, n_layers):
    n = len(hsum)
    prev = list(bufs)
    steps = 2

    def body(s_ref, c_ref, *refs):
        outs = refs[len(refs) - n:]
        for a in range(n):
            acc = refs[a][...].astype(f32)
            for j in range(N_CHIPS - 1):
                acc = acc + refs[n + a][j].astype(f32)
            outs[a][...] = acc

    h_specs, r_specs, o_specs, out_shape = [], [], [], []
    for hh in hsum:
        _, r2, Cc = hh.shape
        rt = r2 // steps
        h_specs.append(pl.BlockSpec((None, rt, Cc), lambda i, s, c: (s[0], i, 0)))
        r_specs.append(pl.BlockSpec((N_CHIPS - 1, rt, Cc), lambda i, s, c: (0, i, 0)))
        o_specs.append(pl.BlockSpec((None, rt, Cc), lambda i, s, c: (layer, c[0] * steps + i, 0)))
        out_shape.append(_sds((n_layers, 2 * r2, Cc), f32))
    gs = pltpu.PrefetchScalarGridSpec(num_scalar_prefetch=2, grid=(steps,),
                                      in_specs=h_specs + r_specs + [ANY_SPEC] * len(prev), out_specs=tuple(o_specs))
    return list(pl.pallas_call(body, name="chip_sum", grid_spec=gs, out_shape=tuple(out_shape),
                               input_output_aliases={2 + 2 * n + a: a for a in range(len(prev))},
                               compiler_params=_cparams(1))(slot_arr, c_arr, *hsum, *recv, *prev))


def _sum_slots(x):
    def body(x_ref, o_ref):
        acc = x_ref[0]
        for j in range(1, x.shape[0]):
            acc = acc + x_ref[j]
        o_ref[...] = acc

    return pl.pallas_call(body, name="sum_slots", out_shape=_sds(x.shape[1:], f32))(x)


def _blockdiag(w):
    nb, bd, _ = w.shape
    eye = jnp.eye(nb, dtype=w.dtype)
    return (eye[:, None, :, None] * w[:, :, None, :]).reshape(nb * bd, nb * bd)


def _blockdiag_extract(dense, nb):
    bd = dense.shape[0] // nb
    d4 = dense.reshape(nb, bd, nb, bd)
    return jnp.stack([d4[i, :, i, :] for i in range(nb)])


def _pack(arrays, multiple):
    flat = jnp.concatenate([a.reshape(-1).astype(f32) for a in arrays])
    pad = (-flat.shape[0]) % multiple
    return jnp.pad(flat, (0, pad))


def _unpack(flat, shapes):
    out, off = [], 0
    for s in shapes:
        n = int(np.prod(s))
        out.append(flat[off:off + n].reshape(s))
        off += n
    return out


BIG = ("ffn1_wg", "ffn1_wu", "ffn1_wd", "w_in", "w_out", "ffn2_wg", "ffn2_wu", "ffn2_wd")
TRANSPOSED = ("ffn1_wg", "ffn1_wu", "ffn2_wg", "ffn2_wu")
SMALL = ("ffn1_norm", "mix_norm", "hgrn_lb_logits", "hgrn_norm", "conv_w", "conv_b", "lru_wa", "lru_ba", "lru_wx",
         "lru_bx", "lru_lambda", "lru_norm", "sgu_w", "sgu_b", "sgu_norm", "ffn2_norm", "final_norm")
WEIGHTS = ("ffn1_norm", "ffn1_wg", "ffn1_wu", "ffn1_wd", "mix_norm", "w_in", "hgrn_lb_logits", "hgrn_norm", "conv_w",
           "conv_b", "lru_wa", "lru_ba", "lru_wx", "lru_bx", "lru_lambda", "lru_norm", "sgu_w", "sgu_b", "sgu_norm",
           "w_out", "ffn2_norm", "ffn2_wg", "ffn2_wu", "ffn2_wd", "final_norm")


def kernel(x, ffn1_norm, ffn1_wg, ffn1_wu, ffn1_wd, mix_norm, w_in, hgrn_lb_logits, hgrn_norm, conv_w, conv_b, lru_wa, lru_ba, lru_wx, lru_bx, lru_lambda, lru_norm, sgu_w, sgu_b, sgu_norm, w_out, ffn2_norm, ffn2_wg, ffn2_wu, ffn2_wd, final_norm, loss_target, m_ffn1_norm, m_ffn1_wg, m_ffn1_wu, m_ffn1_wd, m_mix_norm, m_w_in, m_hgrn_lb_logits, m_hgrn_norm, m_conv_w, m_conv_b, m_lru_wa, m_lru_ba, m_lru_wx, m_lru_bx, m_lru_lambda, m_lru_norm, m_sgu_w, m_sgu_b, m_sgu_norm, m_w_out, m_ffn2_norm, m_ffn2_wg, m_ffn2_wu, m_ffn2_wd, m_final_norm, v_ffn1_norm, v_ffn1_wg, v_ffn1_wu, v_ffn1_wd, v_mix_norm, v_w_in, v_hgrn_lb_logits, v_hgrn_norm, v_conv_w, v_conv_b, v_lru_wa, v_lru_ba, v_lru_wx, v_lru_bx, v_lru_lambda, v_lru_norm, v_sgu_w, v_sgu_b, v_sgu_norm, v_w_out, v_ffn2_norm, v_ffn2_wg, v_ffn2_wu, v_ffn2_wd, v_final_norm):
    args = dict(locals())
    W = {n: args[n] for n in WEIGHTS}
    M = {n: args["m_" + n] for n in WEIGHTS}
    V = {n: args["v_" + n] for n in WEIGHTS}

    T, D = x.shape[1], x.shape[2]
    L = ffn1_norm.shape[0]
    d_a, d_b, d_c = hgrn_norm.shape[1], lru_norm.shape[1], sgu_norm.shape[1]
    col_b, col_c = 4 * d_a, 4 * d_a + 2 * d_b
    tm = 512 if T % 512 == 0 else T
    tm_w = 1024 if T % 1024 == 0 else tm
    tm_d = 256 if T % 256 == 0 else tm
    my_c = lax.axis_index("c")
    my_slot = 2 * lax.axis_index("x") + lax.axis_index("y")
    c_arr = jnp.reshape(my_c, (1,)).astype(jnp.int32)
    slot_arr = jnp.reshape(my_slot, (1,)).astype(jnp.int32)

    nb = len(BIG)
    gplan = _gather_ici_plan(nb)

    def kview(a, n):
        return jnp.swapaxes(a, 1, 2) if n in TRANSPOSED else a

    Wk = {n: kview(W[n], n) for n in BIG}
    place_steps = 4 if all(Wk[n].shape[1] % 64 == 0 for n in BIG) else 2

    conv_land = lax.dynamic_update_slice_in_dim(jnp.zeros((N_CHIPS,) + conv_w.shape, f32), conv_w[None], my_slot, axis=0)
    lands0 = _cast_place([Wk[n] for n in BIG], 0, slot_arr, place_steps)
    n_first = 3
    first_plan = _gather_ici_plan(n_first + 1)
    send, recv, _, first, token = _start_copies("gather_start_first", [], lands0[:n_first] + [conv_land], first_plan)
    later = {l: _cast_place([Wk[n] for n in BIG], l, slot_arr, place_steps, after=token) for l in range(1, L)}
    got = _wait_copies("gather_wait_first", send, recv, [], first, first_plan, later[L - 1][0] if later else lands0[-1])
    got = _gather_d2d("gather0_d2d", got)

    def placed(l):
        return later[l]
    G = [None] * L
    G[0] = dict(zip(BIG[:n_first], got[:n_first]))
    conv_full = jnp.transpose(got[n_first], (1, 2, 0, 3)).reshape(L, CONV_WIDTH, d_b)
    rest_plan = _gather_ici_plan(nb - n_first)
    rest_pending = _start_copies("gather_start_0", [], lands0[n_first:], rest_plan, got[0])

    def start_gather(l, after):
        return _start_copies(f"gather_start_{l}", [], placed(l), gplan, after)

    d2d_plan = _gather_d2d_plan(nb)

    lb = _lower_bounds(hgrn_lb_logits)
    avg_b = _group_avg_matrix(d_b, d_b // B_BLOCKS)
    avg_c = _group_avg_matrix(d_c, d_c // C_GROUPS)
    wa_dense = [_blockdiag(lru_wa[l]) for l in range(L)]
    wx_dense = [_blockdiag(lru_wx[l]) for l in range(L)]
    bexp = [jnp.repeat(sgu_b[l].T, d_c // C_GROUPS, axis=1) for l in range(L)]

    def lru_params(l):
        return (conv_full[l], conv_b[l][None], wa_dense[l], lru_ba[l].reshape(1, d_b), wx_dense[l],
                lru_bx[l].reshape(1, d_b), lru_lambda[l][None], lru_norm[l][None], avg_b)

    h = x.reshape(T, D)
    saved = []
    for l in range(L):
        s = {"h0": h}
        gain1, gain_mix = ffn1_norm[l][None], mix_norm[l][None]
        pending = None
        if l == 0:
            gain1 = gain1 + rest_pending[4][0:1, 0:1]
        elif l + 1 < L:
            pending = start_gather(l + 1, h)
            gain1 = gain1 + pending[4][0:1, 0:1]
        g = G[l]
        h, s["g1"], s["u1"] = _ffn_fwd(h, gain1, g["ffn1_wg"], g["ffn1_wu"], g["ffn1_wd"], tm)
        s["h1"] = h
        if l == 0:
            send, recv, _, lands, _ = rest_pending
            lands = _wait_copies("gather_wait_0", send, recv, [], lands, rest_plan, h)
            g.update(zip(BIG[n_first:], _gather_d2d("gather_d2d", lands)))
            if L > 1:
                pending = start_gather(1, g["w_in"])
                gain_mix = gain_mix + pending[4][0:1, 0:1]
        z = _proj_in_fwd(h, gain_mix, g["w_in"], tm)
        s["z"] = z
        s["oa"], s["o_pre"], s["states"] = _hgrn_fwd(z, lb[l][None], hgrn_norm[l][None], d_a)
        s["ob"], s["hl"] = _lru_fwd(z, col_b, d_b, *lru_params(l))
        s["oc"] = _sgu_fwd(z, col_c, d_c, sgu_w[l], bexp[l], sgu_norm[l][None], avg_c)
        h = _proj_out_fwd(h, s["oa"], s["ob"], s["oc"], g["w_out"], tm)
        s["h2"] = h
        gain2 = ffn2_norm[l][None]
        forward = None
        if pending is not None:
            send, recv, _, lands, _ = pending
            lands = _wait_copies(f"gather_wait_{l + 1}", send, recv, [], lands, gplan, h)
            forward = _start_copies(f"gather_d2d_start_{l + 1}", [], lands, d2d_plan)
            gain2 = gain2 + forward[4][0:1, 0:1]
        h, s["g2"], s["u2"] = _ffn_fwd(h, gain2, g["ffn2_wg"], g["ffn2_wu"], g["ffn2_wd"], tm)
        saved.append(s)
        if forward is not None:
            send, recv, _, lands, _ = forward
            G[l + 1] = dict(zip(BIG, _wait_copies(f"gather_d2d_wait_{l + 1}", send, recv, [], lands, d2d_plan, h)))

    dh, loss_part, d_final = _loss_head(h, final_norm[None], loss_target.reshape(T, D), tm)
    loss = lax.psum(loss_part[0, 0], ("x", "y", "c"))

    def pair_views(n_big):
        r = [(lambda i, o, p, a=a: i[a].at[:, pl.ds((1 - p.c) * (i[a].shape[1] // 2), i[a].shape[1] // 2)],
              lambda i, o, p, a=a: o[a], "sib") for a in range(n_big)]
        return r

    def chip_plan_for(n):
        return [(lambda s_, o, p, a=a, kind=kind: s_[a].at[p.peer_slot(kind)], lambda s_, o, p, a=a, j=j: o[a].at[j], kind)
                for a in range(n) for j, kind in enumerate(CHIP_KINDS)]

    chip_plan = chip_plan_for(nb)
    sbufs = {n: None for n in BIG}

    def pair_phase(arrs, extra=None):
        n = len(arrs)
        ins, remote = list(arrs), pair_views(n)
        outs = [_sds((N_CHIPS, a.shape[1] // 2, a.shape[2]), bf16) for a in arrs]
        if extra is not None:
            ins.append(extra)
            outs.append(_sds(extra.shape, f32))
            remote = remote + [(lambda i, o, p: i[n], lambda i, o, p: o[n], "sib")]
        recv = _exchange("grad_pair_d2d", ins, outs, remote)
        return _pair_sum(arrs, recv[:n], c_arr), (None if extra is None else _add(extra, recv[n]))

    def chip_sum_into(names, hs, lands, l):
        prev = [sbufs[n] for n in names] if sbufs[names[0]] is not None else []
        for n, buf in zip(names, _chip_sum(hs, lands, prev, slot_arr, c_arr, l, L)):
            sbufs[n] = buf

    def share_plan(l):
        view = lambda a: (lambda s_, o, p: _half(o[a].at[l], p.c))
        return [(view(a), view(a), "sib") for a in range(nb)]

    def share_start(l):
        send, recv, _, bufs, token = _start_copies(f"grad_share_start_{l}", [], [sbufs[n] for n in BIG], share_plan(l))
        for n, buf in zip(BIG, bufs):
            sbufs[n] = buf
        return (l, send, recv), token

    def share_wait(pending, after):
        l, send, recv = pending
        bufs = _wait_copies(f"grad_share_wait_{l}", send, recv, [], [sbufs[n] for n in BIG], share_plan(l), after)
        for n, buf in zip(BIG, bufs):
            sbufs[n] = buf

    def share(l, extra_in=(), extra_out=(), extra_remote=(), extra_local=()):
        remote = [(lambda i, o, p, a=a: _half(o[a].at[l], p.c), lambda i, o, p, a=a: _half(o[a].at[l], p.c), "sib")
                  for a in range(nb)]
        outs = [_sds(sbufs[n].shape, f32) for n in BIG] + list(extra_out)
        res = _exchange("grad_share_d2d", [sbufs[n] for n in BIG] + list(extra_in), outs, remote + list(extra_remote),
                        list(extra_local), aliases={a: a for a in range(nb)})
        for n, buf in zip(BIG, res[:nb]):
            sbufs[n] = buf
        return res[nb:]

    small = {n: [None] * L for n in SMALL if n != "final_norm"}
    chip_pending = pair_pending = early = share_token = None
    shares = []
    early_names = ("w_in", "w_out", "ffn2_wg", "ffn2_wu", "ffn2_wd")
    for l in reversed(range(L)):
        s, g = saved[l], G[l]
        gain2, gain_a = ffn2_norm[l][None], hgrn_norm[l][None]
        if pair_pending is not None:
            gain2 = gain2 + pair_pending[0][4][0:1, 0:1]
        if share_token is not None:
            gain2 = gain2 + share_token[0:1, 0:1]
            share_token = None
        dh, small["ffn2_norm"][l], dg, du, xn, dob = _ffn_bwd_dgrad(
            s["h2"], gain2, dh, s["g2"], s["u2"], g["ffn2_wg"], g["ffn2_wu"], g["ffn2_wd"], tm_d)
        if pair_pending is not None:
            (send, recv, grads_prev, lands, _), = pair_pending
            recv_a = _wait_copies(f"grad_pair_wait_{l + 1}", send, recv, grads_prev, lands, pair_views(nb), dh)
            hsum = _pair_sum(grads_prev, recv_a, c_arr)
            lands = [lax.empty((N_CHIPS - 1,) + hh.shape[1:], bf16) for hh in hsum]
            chip_pending = (_start_copies(f"grad_chip_start_{l + 1}", hsum, lands, chip_plan), hsum)
            gain_a = gain_a + chip_pending[0][4][0:1, 0:1]
            pair_pending = None
        dwg2, dwu2, dwd2 = _ffn_bwd_wgrad(xn, dob, s["g2"], s["u2"], dg, du, tm_w)
        doa, dob_, doc, dwo = _proj_out_bwd(dh, s["oa"], s["ob"], s["oc"], g["w_out"], tm)
        dz, small["hgrn_lb_logits"][l], small["hgrn_norm"][l] = _hgrn_bwd(
            s["z"], lb[l][None], gain_a, s["o_pre"], s["states"], doa, d_a)
        (dz, small["conv_w"][l], small["conv_b"][l], dwa, small["lru_ba"][l], dwx, small["lru_bx"][l],
         small["lru_lambda"][l], small["lru_norm"][l]) = _lru_bwd(s["z"], col_b, d_b, s["hl"], dob_, dz, *lru_params(l))
        small["lru_wa"][l] = _blockdiag_extract(dwa, B_BLOCKS)
        small["lru_wx"][l] = _blockdiag_extract(dwx, B_BLOCKS)
        dz, small["sgu_w"][l], dsb, small["sgu_norm"][l] = _sgu_bwd(
            s["z"], col_c, d_c, doc, dz, sgu_w[l], bexp[l], sgu_norm[l][None], avg_c)
        small["sgu_b"][l] = dsb.T
        dh, small["mix_norm"][l], xn = _proj_in_bwd_dgrad(s["h1"], mix_norm[l][None], dh, dz, g["w_in"], tm)
        dwi = _proj_in_bwd_wgrad(xn, dz, N_CHIPS)
        gain1 = ffn1_norm[l][None]
        if l == 0:
            hs_e, _ = pair_phase([dwi, dwo, dwg2, dwu2, dwd2])
            lands = [lax.empty((N_CHIPS - 1,) + hh.shape[1:], bf16) for hh in hs_e]
            early = (_start_copies("grad_chip_start_0", hs_e, lands, chip_plan_for(len(hs_e))), hs_e)
            gain1 = gain1 + early[0][4][0:1, 0:1]
        dh, small["ffn1_norm"][l], dg, du, xn, dob = _ffn_bwd_dgrad(
            s["h0"], gain1, dh, s["g1"], s["u1"], g["ffn1_wg"], g["ffn1_wu"], g["ffn1_wd"], tm_d)
        dwg1, dwu1, dwd1 = _ffn_bwd_wgrad(xn, dob, s["g1"], s["u1"], dg, du, tm_w)
        layer_grads = [dwg1, dwu1, dwd1, dwi, dwo, dwg2, dwu2, dwd2]

        if chip_pending is not None:
            (send, recv, hs, lands, _), hsum_prev = chip_pending
            lands = _wait_copies(f"grad_chip_wait_{l + 1}", send, recv, hs, lands, chip_plan, dwg1)
            chip_sum_into(BIG, hsum_prev, lands, l + 1)
            pending_share, share_token = share_start(l + 1)
            shares.append(pending_share)
            chip_pending = None
        if l > 0:
            lands = [lax.empty((N_CHIPS, gr.shape[1] // 2, gr.shape[2]), bf16) for gr in layer_grads]
            pair_pending = (_start_copies(f"grad_pair_start_{l}", layer_grads, lands, pair_views(nb)),)
    grad_x = dh.reshape(x.shape)

    (send, recv, hs, lands, _), hs_e = early
    lands = _wait_copies("grad_chip_wait_0", send, recv, hs, lands, chip_plan_for(len(hs_e)), dwg1)
    chip_sum_into(early_names, hs_e, lands, 0)
    small_names = [n for n in SMALL]
    small_parts = [jnp.stack([jnp.reshape(v, (-1,)) for v in small[n]]) if n != "final_norm" else d_final for n in small_names]
    small_shapes = [p.shape for p in small_parts]
    packed = _pack(small_parts, 2 * 8 * 128).reshape(2, -1, 128)
    n_rows = packed.shape[1]
    late = [dwg1, dwu1, dwd1]
    nl = len(late)
    hsum, small_pair = pair_phase(late, packed)
    own_half = lax.dynamic_index_in_dim(small_pair, my_c, 0, keepdims=True)
    small_land = lax.dynamic_update_slice_in_dim(jnp.zeros((N_CHIPS, n_rows, 128), f32), own_half, my_slot, axis=0)
    late_plan = chip_plan_for(nl) + [(lambda s_, o, p: s_[nl].at[p.c], lambda s_, o, p: o[nl].at[p.slot], kind)
                                     for kind in CHIP_KINDS]
    lands = [lax.empty((N_CHIPS - 1,) + hh.shape[1:], bf16) for hh in hsum] + [small_land]
    send, recv, srcs, lands, token = _start_copies("grad_chip_start_last", hsum + [small_pair], lands, late_plan)

    def adam_operands(n):
        shape = Wk[n].shape
        rows_blk = 512 if shape[1] % 512 == 0 else (shape[1] // 2 if shape[1] > 512 else shape[1])
        return [a.reshape(-1, shape[-1]) for a in (Wk[n], sbufs[n], kview(M[n], n), kview(V[n], n))], rows_blk, shape

    for pending_share in shares:
        share_wait(pending_share, token)
    partial = {}
    if L > 1:
        for n in BIG:
            flat, rows_blk, shape = adam_operands(n)
            partial[n] = _adamw(*flat, rows_blk, row_range=(shape[1], L * shape[1]), after=token)
    recv_b = _wait_copies("grad_chip_wait_last", send, recv, srcs, lands, late_plan,
                          partial[BIG[-1]][0] if partial else hsum[0])
    chip_sum_into(BIG[:nl], hsum, recv_b[:nl], 0)
    small_half = _sum_slots(recv_b[nl])
    (small_all,) = share(0, extra_in=[small_half], extra_out=[_sds(packed.shape, f32)],
                         extra_remote=[(lambda i, o, p: i[nb], lambda i, o, p: o[nb].at[p.c], "sib")],
                         extra_local=[(lambda i, o, p: i[nb], lambda i, o, p: o[nb].at[p.c])])
    grads = {n: kview(sbufs[n], n) for n in BIG}
    small_tot = _unpack(small_all.reshape(-1), small_shapes)
    for n, val in zip(small_names, small_tot):
        grads[n] = val
    grads["hgrn_lb_logits"] = _lower_bounds_bwd(hgrn_lb_logits, grads["hgrn_lb_logits"])
    shard_cols = conv_w.shape[2]
    grads["conv_w"] = lax.dynamic_slice_in_dim(grads["conv_w"].reshape(L, CONV_WIDTH, d_b), my_slot * shard_cols, shard_cols, axis=2)
    for n in SMALL:
        grads[n] = grads[n].reshape(W[n].shape)

    delta, new_m, new_v = {}, {}, {}
    for n in BIG:
        flat, rows_blk, shape = adam_operands(n)
        outs = _adamw(*flat, rows_blk, row_range=(0, shape[1]), prev=partial[n]) if partial else _adamw(*flat, rows_blk)
        delta[n], new_m[n], new_v[n], grads[n] = [kview(o.reshape(shape), n) for o in outs]
    shapes = [W[n].shape for n in SMALL]
    packs = [_pack([src[n] for n in SMALL], 8 * 128).reshape(-1, 128) for src in (W, grads, M, V)]
    d2, m2, v2, _ = _adamw(*packs, 4096)
    for dst, val in ((delta, d2), (new_m, m2), (new_v, v2)):
        for n, piece in zip(SMALL, _unpack(val.reshape(-1), shapes)):
            dst[n] = piece

    return (loss, grad_x, *[grads[n] for n in WEIGHTS], *[delta[n] for n in WEIGHTS],
            *[new_m[n] for n in WEIGHTS], *[new_v[n] for n in WEIGHTS])
```

```python
import math

import numpy as np
import jax
import jax.numpy as jnp
from jax import lax
from jax.experimental import pallas as pl
from jax.experimental.pallas import tpu as pltpu

f32 = jnp.float32
bf16 = jnp.bfloat16
HI = lax.Precision.HIGHEST
MESH = pl.DeviceIdType.MESH

EPS = 1e-6
HEAD = 128
A_CHUNK = 64
A_SUB = 16
A_INNER = 2
SUBLANES = 8
B_BLOCKS = 4
B_CHUNK = 256
CONV_WIDTH = 4
LRU_C = 8.0
C_GROUPS = 4
C_CHUNK = 128
C_INNER = 8
N_CHIPS = 4
ADAM_LR, ADAM_B1, ADAM_B2, ADAM_EPS, ADAM_WD, ADAM_STEP = 0.001, 0.9, 0.999, 1e-08, 0.01, 10
VMEM_LIMIT = 56 * 1024 * 1024


def _cparams(n_axes):
    return pltpu.CompilerParams(dimension_semantics=("arbitrary",) * n_axes, vmem_limit_bytes=VMEM_LIMIT)


def _sds(shape, dtype):
    return jax.ShapeDtypeStruct(tuple(shape), dtype)


def _full(shape):
    n = len(shape)
    return pl.BlockSpec(tuple(shape), lambda *_: (0,) * n)


def _resident(shape):
    n = len(shape)
    return pl.BlockSpec(tuple(shape), lambda *_: (0,) * n, pipeline_mode=pl.Buffered(1))


def _dot(a, b):
    return jnp.dot(a, b, preferred_element_type=f32)


def _dot_nt(a, b):
    return lax.dot_general(a, b, (((1,), (1,)), ((), ())), preferred_element_type=f32)


def _dot_tn(a, b):
    return lax.dot_general(a, b, (((0,), (0,)), ((), ())), preferred_element_type=f32)


def _silu(x):
    return x * jax.nn.sigmoid(x)


def _group_avg_matrix(n, group):
    idx = np.arange(n) // group
    return jnp.asarray((idx[:, None] == idx[None, :]).astype(np.float32) / group)


class _Place:
    def __init__(self):
        self.x, self.y, self.c = lax.axis_index("x"), lax.axis_index("y"), lax.axis_index("c")
        self.slot = 2 * self.x + self.y

    def peer(self, kind):
        x, y, c = self.x, self.y, self.c
        return {"sib": (x, y, 1 - c), "fx": (1 - x, y, c), "fy": (x, 1 - y, c), "fxy": (1 - x, 1 - y, c)}[kind]

    def peer_slot(self, kind):
        x, y = self.x, self.y
        return {"fx": 2 * (1 - x) + y, "fy": 2 * x + (1 - y), "fxy": 2 * (1 - x) + (1 - y)}[kind]


CHIP_KINDS = ("fx", "fy", "fxy")


def _exchange(name, ins, outs, remote, local=(), aliases=None):
    n_in, n_out, n_r, n_l = len(ins), len(outs), len(remote), len(local)

    def body(*refs):
        in_refs, out_refs = refs[:n_in], refs[n_in:n_in + n_out]
        send, recv, lsem = refs[n_in + n_out:]
        p = _Place()
        lcopies = []
        for t, (src, dst) in enumerate(local):
            cp = pltpu.make_async_copy(src(in_refs, out_refs, p), dst(in_refs, out_refs, p), lsem.at[t])
            cp.start()
            lcopies.append(cp)
        copies = []
        for t, (src, dst, kind) in enumerate(remote):
            cp = pltpu.make_async_remote_copy(
                src_ref=src(in_refs, out_refs, p), dst_ref=dst(in_refs, out_refs, p),
                send_sem=send.at[t], recv_sem=recv.at[t], device_id=p.peer(kind), device_id_type=MESH)
            cp.start()
            copies.append(cp)
        for cp in copies:
            cp.wait_recv()
        for cp in copies:
            cp.wait_send()
        for cp in lcopies:
            cp.wait()

    anyspec = pl.BlockSpec(memory_space=pl.ANY)
    res = pl.pallas_call(
        body, name=name, out_shape=tuple(outs),
        in_specs=[anyspec] * n_in, out_specs=tuple([anyspec] * n_out),
        scratch_shapes=[pltpu.SemaphoreType.DMA((n_r,)), pltpu.SemaphoreType.DMA((n_r,)),
                        pltpu.SemaphoreType.DMA((max(n_l, 1),))],
        input_output_aliases=aliases or {},
        compiler_params=pltpu.CompilerParams(has_side_effects=True),
    )(*ins)
    return list(res)


HBM_SPEC = pl.BlockSpec(memory_space=pltpu.HBM)
SEM_SPEC = pl.BlockSpec(memory_space=pltpu.SEMAPHORE)
ANY_SPEC = pl.BlockSpec(memory_space=pl.ANY)
DATAFLOW = pltpu.SideEffectType.DATAFLOW_SIDE_EFFECTING


def _in_hbm(a):
    return pltpu.with_memory_space_constraint(a, pltpu.HBM)


def _start_copies(name, srcs, lands, remote, after=None):
    n_s, n_l, n_r = len(srcs), len(lands), len(remote)
    extra = [] if after is None else [after]

    def body(*refs):
        src_refs, land_refs = refs[:n_s], refs[n_s:n_s + n_l]
        n_in = n_s + n_l + len(extra)
        send, recv = refs[n_in], refs[n_in + 1]
        token = refs[-1]
        p = _Place()
        for t, (src, dst, kind) in enumerate(remote):
            pltpu.make_async_remote_copy(
                src_ref=src(src_refs, land_refs, p), dst_ref=dst(src_refs, land_refs, p),
                send_sem=send.at[t], recv_sem=recv.at[t], device_id=p.peer(kind), device_id_type=MESH).start()
        token[...] = jnp.zeros_like(token)

    thru = [pltpu.HBM(a.shape, a.dtype) for a in lands]
    res = pl.pallas_call(
        body, name=name,
        out_shape=(pltpu.SemaphoreType.DMA((n_r,)), pltpu.SemaphoreType.DMA((n_r,)), *thru, _sds((8, 128), f32)),
        in_specs=[ANY_SPEC] * n_s + [HBM_SPEC] * n_l + [ANY_SPEC] * len(extra),
        out_specs=(SEM_SPEC, SEM_SPEC, *([HBM_SPEC] * n_l), pl.BlockSpec(memory_space=pltpu.VMEM)),
        input_output_aliases={n_s + i: 2 + i for i in range(n_l)},
        compiler_params=pltpu.CompilerParams(has_side_effects=DATAFLOW),
    )(*srcs, *[_in_hbm(a) for a in lands], *extra)
    return res[0], res[1], list(srcs), list(res[2:2 + n_l]), res[-1]


def _wait_copies(name, send, recv, srcs, lands, remote, after):
    n_s, n_l = len(srcs), len(lands)

    def body(*refs):
        src_refs, land_refs = refs[:n_s], refs[n_s:n_s + n_l]
        send_ref, recv_ref = refs[n_s + n_l], refs[n_s + n_l + 1]
        p = _Place()
        for t, (src, dst, kind) in enumerate(remote):
            cp = pltpu.make_async_remote_copy(
                src_ref=src(src_refs, land_refs, p), dst_ref=dst(src_refs, land_refs, p),
                send_sem=send_ref.at[t], recv_sem=recv_ref.at[t], device_id=p.peer(kind), device_id_type=MESH)
            cp.wait_send()
            cp.wait_recv()

    res = pl.pallas_call(
        body, name=name, out_shape=tuple(pltpu.HBM(a.shape, a.dtype) for a in lands),
        in_specs=[ANY_SPEC] * n_s + [HBM_SPEC] * n_l + [SEM_SPEC, SEM_SPEC, ANY_SPEC],
        out_specs=tuple([HBM_SPEC] * n_l),
        input_output_aliases={n_s + i: i for i in range(n_l)},
        compiler_params=pltpu.CompilerParams(has_side_effects=DATAFLOW),
    )(*srcs, *lands, send, recv, after)
    return list(res)


def _half(ref, c):
    n2 = ref.shape[0] // 2
    return ref.at[pl.ds(c * n2, n2)]


def _gather_ici_plan(n):
    def view(a):
        return lambda s, o, p: _half(o[a].at[p.slot], p.c)

    return [(view(a), view(a), kind) for a in range(n) for kind in CHIP_KINDS]


def _gather_d2d_plan(n):
    remote = []
    for a in range(n):
        for kind in CHIP_KINDS:
            view = lambda i, o, p, a=a, kind=kind: _half(o[a].at[p.peer_slot(kind)], p.c)
            remote.append((view, view, "sib"))
    return remote


def _gather_d2d(name, lands):
    n = len(lands)
    outs = [_sds(g.shape, g.dtype) for g in lands]
    return _exchange(name, list(lands), outs, _gather_d2d_plan(n), aliases={a: a for a in range(n)})


def _cast_place(weights, layer, slot_arr, n_steps=4, after=None):
    n = len(weights)
    extra = [] if after is None else [after]

    def body(s_ref, *refs):
        outs = refs[n + len(extra):]
        for a in range(n):
            outs[a][...] = refs[a][...].astype(bf16)

    in_specs, out_specs, out_shape = [], [], []
    for w in weights:
        _, R, Cc = w.shape
        rt = R // n_steps
        in_specs.append(pl.BlockSpec((None, rt, Cc), lambda i, s: (layer, i, 0)))
        out_specs.append(pl.BlockSpec((None, rt, Cc), lambda i, s: (s[0], i, 0)))
        out_shape.append(_sds((N_CHIPS, R, Cc), bf16))
    gs = pltpu.PrefetchScalarGridSpec(num_scalar_prefetch=1, grid=(n_steps,), in_specs=in_specs + [ANY_SPEC] * len(extra),
                                      out_specs=tuple(out_specs))
    return list(pl.pallas_call(body, name="cast_place", grid_spec=gs, out_shape=tuple(out_shape),
                               compiler_params=_cparams(1))(slot_arr, *weights, *extra))


def _ffn_fwd(h, gain, wg, wu, wd, tm):
    T, D = h.shape
    nsh, F = wg.shape[0], wg.shape[1]
    nt = T // tm

    def body(h_ref, gain_ref, wg_ref, wu_ref, wd_ref, out_ref, gs_ref, us_ref):
        hv = h_ref[...]
        r = lax.rsqrt(jnp.mean(hv * hv, axis=-1, keepdims=True) + EPS)
        xn = (hv * r * gain_ref[...]).astype(bf16)
        acc = None
        for k in range(nsh):
            g = _dot_nt(xn, wg_ref[k])
            u = _dot_nt(xn, wu_ref[k])
            gs_ref[k] = g.astype(bf16)
            us_ref[k] = u.astype(bf16)
            part = _dot((_silu(g) * u).astype(bf16), wd_ref[k])
            acc = part if acc is None else acc + part
        out_ref[...] = hv + 0.5 * acc

    sav = pl.BlockSpec((nsh, tm, F), lambda i: (0, i, 0))
    return pl.pallas_call(
        body, name="ffn_fwd", grid=(nt,),
        in_specs=[pl.BlockSpec((tm, D), lambda i: (i, 0)), _full((1, D)), _resident((nsh, F, D)), _resident((nsh, F, D)),
                  _resident((nsh, F, D))],
        out_specs=(pl.BlockSpec((tm, D), lambda i: (i, 0)), sav, sav),
        out_shape=(_sds((T, D), f32), _sds((nsh, T, F), bf16), _sds((nsh, T, F), bf16)),
        compiler_params=_cparams(1),
    )(h, gain, wg, wu, wd)


def _ffn_bwd_dgrad(h, gain, dout, gs, us, wg, wu, wd, tm):
    T, D = h.shape
    nsh, F = wg.shape[0], wg.shape[1]
    nt = T // tm

    def body(h_ref, gain_ref, dout_ref, gs_ref, us_ref, wg_ref, wu_ref, wd_ref,
             dh_ref, dgain_ref, dg_ref, du_ref, xn_ref, dob_ref):
        @pl.when(pl.program_id(0) == 0)
        def _():
            dgain_ref[...] = jnp.zeros_like(dgain_ref)

        hv = h_ref[...]
        r = lax.rsqrt(jnp.mean(hv * hv, axis=-1, keepdims=True) + EPS)
        xh = hv * r
        xn_ref[...] = (xh * gain_ref[...]).astype(bf16)
        dv = dout_ref[...]
        dob = (0.5 * dv).astype(bf16)
        dob_ref[...] = dob
        dxn = None
        for k in range(nsh):
            da = _dot_nt(dob, wd_ref[k])
            g = gs_ref[k].astype(f32)
            u = us_ref[k].astype(f32)
            sg = jax.nn.sigmoid(g)
            dg = (da * u * (sg * (1.0 + g * (1.0 - sg)))).astype(bf16)
            du = (da * (g * sg)).astype(bf16)
            dg_ref[k] = dg
            du_ref[k] = du
            part = _dot(dg, wg_ref[k]) + _dot(du, wu_ref[k])
            dxn = part if dxn is None else dxn + part
        dgain_ref[...] += jnp.sum(dxn * xh, axis=0, keepdims=True)
        dxh = dxn * gain_ref[...]
        dh_ref[...] = dv + r * (dxh - xh * jnp.mean(dxh * xh, axis=-1, keepdims=True))

    tok = pl.BlockSpec((tm, D), lambda i: (i, 0))
    sav = pl.BlockSpec((nsh, tm, F), lambda i: (0, i, 0))
    return pl.pallas_call(
        body, name="ffn_bwd_dgrad", grid=(nt,),
        in_specs=[tok, _full((1, D)), tok, sav, sav, _resident((nsh, F, D)), _resident((nsh, F, D)), _resident((nsh, F, D))],
        out_specs=(tok, _full((1, D)), sav, sav, tok, tok),
        out_shape=(_sds((T, D), f32), _sds((1, D), f32), _sds((nsh, T, F), bf16), _sds((nsh, T, F), bf16),
                   _sds((T, D), bf16), _sds((T, D), bf16)),
        compiler_params=_cparams(1),
    )(h, gain, dout, gs, us, wg, wu, wd)


def _ffn_bwd_wgrad(xn, dob, gs, us, dg, du, tm):
    T, D = xn.shape
    nsh, F = gs.shape[0], gs.shape[2]
    nt = T // tm

    def body(xn_ref, dob_ref, gs_ref, us_ref, dg_ref, du_ref, dwg_ref, dwu_ref, dwd_ref, ag_ref, au_ref, ad_ref):
        i = pl.program_id(1)

        @pl.when(i == 0)
        def _():
            ag_ref[...] = jnp.zeros_like(ag_ref)
            au_ref[...] = jnp.zeros_like(au_ref)
            ad_ref[...] = jnp.zeros_like(ad_ref)

        xn_v = xn_ref[...]
        ag_ref[...] += _dot_tn(dg_ref[...], xn_v)
        au_ref[...] += _dot_tn(du_ref[...], xn_v)
        g = gs_ref[...].astype(f32)
        a = (_silu(g) * us_ref[...].astype(f32)).astype(bf16)
        ad_ref[...] += _dot_tn(a, dob_ref[...])

        @pl.when(i == nt - 1)
        def _():
            dwg_ref[...] = ag_ref[...].astype(bf16)
            dwu_ref[...] = au_ref[...].astype(bf16)
            dwd_ref[...] = ad_ref[...].astype(bf16)

    tok = pl.BlockSpec((tm, D), lambda k, i: (i, 0))
    sav = pl.BlockSpec((None, tm, F), lambda k, i: (k, i, 0))
    wdspec = pl.BlockSpec((None, F, D), lambda k, i: (k, 0, 0))
    return pl.pallas_call(
        body, name="ffn_bwd_wgrad", grid=(nsh, nt),
        in_specs=[tok, tok, sav, sav, sav, sav],
        out_specs=(wdspec, wdspec, wdspec),
        out_shape=(_sds((nsh, F, D), bf16),) * 3,
        scratch_shapes=[pltpu.VMEM((F, D), f32)] * 3,
        compiler_params=_cparams(2),
    )(xn, dob, gs, us, dg, du)


def _proj_in_fwd(h, gain, w_in, tm):
    T, D = h.shape
    nsh, N = w_in.shape[0], w_in.shape[2]
    nt = T // tm

    def body(h_ref, gain_ref, w_ref, z_ref):
        hv = h_ref[...]
        r = lax.rsqrt(jnp.mean(hv * hv, axis=-1, keepdims=True) + EPS)
        xn = (hv * r * gain_ref[...]).astype(bf16)
        for k in range(nsh):
            z_ref[:, k * N:(k + 1) * N] = _dot(xn, w_ref[k])

    return pl.pallas_call(
        body, name="proj_in_fwd", grid=(nt,),
        in_specs=[pl.BlockSpec((tm, D), lambda i: (i, 0)), _full((1, D)), _full((nsh, D, N))],
        out_specs=pl.BlockSpec((tm, nsh * N), lambda i: (i, 0)),
        out_shape=_sds((T, nsh * N), f32),
        compiler_params=_cparams(1),
    )(h, gain, w_in)


def _proj_in_bwd_dgrad(h, gain, dres, dz, w_in, tm):
    T, D = h.shape
    nsh, N = w_in.shape[0], w_in.shape[2]
    nt = T // tm

    def body(h_ref, gain_ref, dres_ref, dz_ref, w_ref, dh_ref, dgain_ref, xn_ref):
        @pl.when(pl.program_id(0) == 0)
        def _():
            dgain_ref[...] = jnp.zeros_like(dgain_ref)

        dxn = _dot_nt(dz_ref[:, 0:N], w_ref[0])
        for k in range(1, nsh):
            dxn = dxn + _dot_nt(dz_ref[:, k * N:(k + 1) * N], w_ref[k])
        hv = h_ref[...]
        r = lax.rsqrt(jnp.mean(hv * hv, axis=-1, keepdims=True) + EPS)
        xh = hv * r
        xn_ref[...] = (xh * gain_ref[...]).astype(bf16)
        dgain_ref[...] += jnp.sum(dxn * xh, axis=0, keepdims=True)
        dxh = dxn * gain_ref[...]
        dh_ref[...] = dres_ref[...] + r * (dxh - xh * jnp.mean(dxh * xh, axis=-1, keepdims=True))

    tok = pl.BlockSpec((tm, D), lambda i: (i, 0))
    return pl.pallas_call(
        body, name="proj_in_bwd_dgrad", grid=(nt,),
        in_specs=[tok, _full((1, D)), tok, pl.BlockSpec((tm, nsh * N), lambda i: (i, 0)), _full((nsh, D, N))],
        out_specs=(tok, _full((1, D)), tok),
        out_shape=(_sds((T, D), f32), _sds((1, D), f32), _sds((T, D), bf16)),
        compiler_params=_cparams(1),
    )(h, gain, dres, dz, w_in)


def _proj_in_bwd_wgrad(xn, dz, nsh):
    T, D = xn.shape
    N = dz.shape[1] // nsh

    def body(xn_ref, dz_ref, dw_ref):
        dw_ref[...] = _dot_tn(xn_ref[...], dz_ref[...]).astype(bf16)

    return pl.pallas_call(
        body, name="proj_in_bwd_wgrad", grid=(nsh,),
        in_specs=[_full((T, D)), pl.BlockSpec((T, N), lambda k: (0, k))],
        out_specs=pl.BlockSpec((None, D, N), lambda k: (k, 0, 0)),
        out_shape=_sds((nsh, D, N), bf16),
        compiler_params=_cparams(1),
    )(xn, dz)


def _proj_out_fwd(h, oa, ob, oc, w_out, tm):
    T, D = h.shape
    nsh, R = w_out.shape[0], w_out.shape[1]
    da, db = oa.shape[1], ob.shape[1]
    nt = T // tm

    def body(h_ref, oa_ref, ob_ref, oc_ref, w_ref, out_ref):
        w = w_ref[...].reshape(nsh * R, D)
        out_ref[...] = (h_ref[...] + _dot(oa_ref[...], w[:da]) + _dot(ob_ref[...], w[da:da + db])
                        + _dot(oc_ref[...], w[da + db:]))

    def tok(n):
        return pl.BlockSpec((tm, n), lambda i: (i, 0))

    return pl.pallas_call(
        body, name="proj_out_fwd", grid=(nt,),
        in_specs=[tok(D), tok(da), tok(db), tok(oc.shape[1]), _full((nsh, R, D))],
        out_specs=tok(D), out_shape=_sds((T, D), f32),
        compiler_params=_cparams(1),
    )(h, oa, ob, oc, w_out)


def _proj_out_bwd(dh, oa, ob, oc, w_out, tm):
    T, D = dh.shape
    nsh, R = w_out.shape[0], w_out.shape[1]
    da, db, dc = oa.shape[1], ob.shape[1], oc.shape[1]
    nt = T // tm

    def body(dh_ref, oa_ref, ob_ref, oc_ref, w_ref, doa_ref, dob_ref, doc_ref, dw_ref, acc_ref):
        i = pl.program_id(0)

        @pl.when(i == 0)
        def _():
            acc_ref[...] = jnp.zeros_like(acc_ref)

        d = dh_ref[...].astype(bf16)
        w = w_ref[...].reshape(nsh * R, D)
        dm = _dot_nt(d, w)
        doa_ref[...] = dm[:, :da]
        dob_ref[...] = dm[:, da:da + db]
        doc_ref[...] = dm[:, da + db:]
        acc_ref[pl.ds(0, da), :] += _dot_tn(oa_ref[...], d)
        acc_ref[pl.ds(da, db), :] += _dot_tn(ob_ref[...], d)
        acc_ref[pl.ds(da + db, dc), :] += _dot_tn(oc_ref[...], d)

        @pl.when(i == nt - 1)
        def _():
            dw_ref[...] = acc_ref[...].astype(bf16).reshape(nsh, R, D)

    def tok(n):
        return pl.BlockSpec((tm, n), lambda i: (i, 0))

    wspec = _full((nsh, R, D))
    return pl.pallas_call(
        body, name="proj_out_bwd", grid=(nt,),
        in_specs=[tok(D), tok(da), tok(db), tok(dc), wspec],
        out_specs=(tok(da), tok(db), tok(dc), wspec),
        out_shape=(_sds((T, da), f32), _sds((T, db), f32), _sds((T, dc), f32), _sds((nsh, R, D), bf16)),
        scratch_shapes=[pltpu.VMEM((nsh * R, D), f32)],
        compiler_params=_cparams(1),
    )(dh, oa, ob, oc, w_out)


def _head_sum(m, n_heads):
    parts = []
    for hd in range(n_heads):
        s = jnp.sum(m[:, hd * HEAD:(hd + 1) * HEAD], axis=-1, keepdims=True)
        parts.append(jnp.broadcast_to(s, (m.shape[0], HEAD)))
    return parts[0] if n_heads == 1 else jnp.concatenate(parts, axis=1)


def _cat(parts, axis):
    return parts[0] if len(parts) == 1 else jnp.concatenate(parts, axis=axis)


def _three_parts(x):
    hi = x.astype(bf16)
    r1 = x - hi.astype(f32)
    mid = r1.astype(bf16)
    lo = (r1 - mid.astype(f32)).astype(bf16)
    return hi, mid, lo


@jax.custom_vjp
def _chunk_cumsum(tri, x):
    t16 = tri.astype(bf16)
    hi, mid, lo = _three_parts(x)
    return _dot(t16, hi) + _dot(t16, mid) + _dot(t16, lo)


def _chunk_cumsum_fwd(tri, x):
    return _chunk_cumsum(tri, x), tri


def _chunk_cumsum_bwd(tri, ct):
    t16 = tri.astype(bf16)
    hi, mid, lo = _three_parts(ct)
    return jnp.zeros_like(tri), _dot_tn(t16, hi) + _dot_tn(t16, mid) + _dot_tn(t16, lo)


_chunk_cumsum.defvjp(_chunk_cumsum_fwd, _chunk_cumsum_bwd)


def _hgrn_block(q, fl, iv, lb, states, tri, n_heads, n_inner):
    C = q.shape[0] // n_inner
    qs = _silu(q)
    forget = lb + (1.0 - lb) * jax.nn.sigmoid(fl)
    kk = 1.0 - forget
    logf = jnp.log(forget)
    b = _chunk_cumsum(tri, logf)
    vb = iv.astype(bf16)
    heads = [slice(hd * HEAD, (hd + 1) * HEAD) for hd in range(n_heads)]
    n_sub = C // A_SUB

    off, qe, kd, dec = {}, [], [], []
    for j in range(n_inner):
        c0 = j * C
        for blk in range(1, n_sub):
            lo = c0 + blk * A_SUB
            piv = b[lo:lo + 1]
            qt = (qs[lo:lo + A_SUB] * jnp.exp(b[lo:lo + A_SUB] - piv)).astype(bf16)
            kt = (kk[c0:lo] * jnp.exp(piv - b[c0:lo])).astype(bf16)
            parts = []
            for sl in heads:
                sc = _dot_nt(qt[:, sl], kt[:, sl])
                parts.append(_dot(sc.astype(bf16), vb[c0:lo, sl]))
            off[(j, blk)] = _cat(parts, 1)
        bj = b[c0:c0 + C]
        b_end = bj[C - 1:C]
        qe.append((qs[c0:c0 + C] * jnp.exp(bj)).astype(bf16))
        kd.append((kk[c0:c0 + C] * jnp.exp(b_end - bj)).astype(bf16))
        dec.append(jnp.exp(b_end))

    outs = []
    for j in range(n_inner):
        for blk in range(n_sub):
            lo = j * C + blk * A_SUB
            groups = [off[(j, blk)][r0:r0 + SUBLANES] if blk > 0 else None for r0 in range(0, A_SUB, SUBLANES)]
            for s in range(A_SUB):
                first = (s // SUBLANES) * SUBLANES
                n_rows = A_SUB - first
                row = lax.broadcasted_iota(jnp.int32, (n_rows, 1), 0) + first
                gate = jnp.where(row >= s, 0.0, -1e30)
                r = slice(lo + first, lo + A_SUB)
                m = qs[r] * jnp.exp((b[r] - b[lo + s:lo + s + 1]) + gate) * kk[lo + s:lo + s + 1]
                term = _head_sum(m, n_heads) * iv[lo + s:lo + s + 1]
                for gi in range(first // SUBLANES, A_SUB // SUBLANES):
                    piece = term[gi * SUBLANES - first:(gi + 1) * SUBLANES - first]
                    groups[gi] = piece if groups[gi] is None else groups[gi] + piece
            outs.extend(groups)
    o = jnp.concatenate(outs, axis=0)

    inter = []
    states = list(states)
    for j in range(n_inner):
        c0 = j * C
        parts = []
        for hd, sl in enumerate(heads):
            st = states[hd]
            parts.append(_dot_nt(qe[j][:, sl], st.astype(bf16)))
            states[hd] = dec[j][:, sl] * st + _dot_tn(vb[c0:c0 + C, sl], kd[j][:, sl])
        inter.append(_cat(parts, 1))
    return o + _cat(inter, 0), tuple(states)


def _hgrn_gate(o, g, gain, n_heads):
    ms = _head_sum(o * o, n_heads) * (1.0 / HEAD)
    return o * lax.rsqrt(ms + EPS) * gain * _silu(g)


def _tri_matrix(c, n_inner):
    idx = np.arange(c * n_inner)
    same = (idx[:, None] // c) == (idx[None, :] // c)
    return jnp.asarray((same & (idx[:, None] >= idx[None, :])).astype(np.float32))


def _hgrn_fwd(z, lb, gain, d_a):
    T = z.shape[0]
    C = A_CHUNK * A_INNER
    nc = T // C
    nh = d_a // HEAD
    tri = _tri_matrix(A_CHUNK, A_INNER)

    def body(q_ref, f_ref, i_ref, g_ref, lb_ref, gain_ref, tri_ref, out_ref, o_ref, st_ref, carry_ref):
        @pl.when(pl.program_id(0) == 0)
        def _():
            carry_ref[...] = jnp.zeros_like(carry_ref)

        states = tuple(carry_ref[hd] for hd in range(nh))
        st_ref[...] = carry_ref[...]
        o, new_states = _hgrn_block(q_ref[...], f_ref[...], i_ref[...], lb_ref[...], states, tri_ref[...], nh, A_INNER)
        o_ref[...] = o
        out_ref[...] = _hgrn_gate(o, g_ref[...], gain_ref[...], nh).astype(bf16)
        for hd in range(nh):
            carry_ref[hd] = new_states[hd]

    def col(j):
        return pl.BlockSpec((C, d_a), lambda c, j=j: (c, j))

    tok = pl.BlockSpec((C, d_a), lambda c: (c, 0))
    return pl.pallas_call(
        body, name="hgrn_fwd", grid=(nc,),
        in_specs=[col(0), col(1), col(2), col(3), _full((1, d_a)), _full((1, d_a)), _full((C, C))],
        out_specs=(tok, tok, pl.BlockSpec((None, nh, HEAD, HEAD), lambda c: (c, 0, 0, 0))),
        out_shape=(_sds((T, d_a), bf16), _sds((T, d_a), f32), _sds((nc, nh, HEAD, HEAD), f32)),
        scratch_shapes=[pltpu.VMEM((nh, HEAD, HEAD), f32)],
        compiler_params=_cparams(1),
    )(z, z, z, z, lb, gain, tri)


def _hgrn_bwd(z, lb, gain, o_pre, states, dout, d_a):
    T = z.shape[0]
    C = A_CHUNK * A_INNER
    nc = T // C
    nh = d_a // HEAD
    tri = _tri_matrix(A_CHUNK, A_INNER)

    def body(q_ref, f_ref, i_ref, g_ref, lb_ref, gain_ref, tri_ref, o_ref, st_ref, do_ref,
             dz_ref, dlb_ref, dgain_ref, carry_ref):
        @pl.when(pl.program_id(0) == 0)
        def _():
            carry_ref[...] = jnp.zeros_like(carry_ref)
            dlb_ref[...] = jnp.zeros_like(dlb_ref)
            dgain_ref[...] = jnp.zeros_like(dgain_ref)

        _, vjp_gate = jax.vjp(lambda o, g, gv: _hgrn_gate(o, g, gv, nh), o_ref[...], g_ref[...], gain_ref[...])
        d_o, dg, dgain = vjp_gate(do_ref[...])
        tri_v = tri_ref[...]

        def fn(q, fl, iv, lbv, sts):
            return _hgrn_block(q, fl, iv, lbv, sts, tri_v, nh, A_INNER)

        states_in = tuple(st_ref[hd] for hd in range(nh))
        _, vjp = jax.vjp(fn, q_ref[...], f_ref[...], i_ref[...], lb_ref[...], states_in)
        dstates = tuple(carry_ref[hd] for hd in range(nh))
        dq, df, di, dlb, dst = vjp((d_o, dstates))
        dz_ref[:, 0:d_a] = dq.astype(bf16)
        dz_ref[:, d_a:2 * d_a] = df.astype(bf16)
        dz_ref[:, 2 * d_a:3 * d_a] = di.astype(bf16)
        dz_ref[:, 3 * d_a:4 * d_a] = dg.astype(bf16)
        dlb_ref[...] += dlb
        dgain_ref[...] += dgain
        for hd in range(nh):
            carry_ref[hd] = dst[hd]

    def col(j):
        return pl.BlockSpec((C, d_a), lambda c, j=j: (nc - 1 - c, j))

    tok = pl.BlockSpec((C, d_a), lambda c: (nc - 1 - c, 0))
    return pl.pallas_call(
        body, name="hgrn_bwd", grid=(nc,),
        in_specs=[col(0), col(1), col(2), col(3), _full((1, d_a)), _full((1, d_a)), _full((C, C)), tok,
                  pl.BlockSpec((None, nh, HEAD, HEAD), lambda c: (nc - 1 - c, 0, 0, 0)), tok],
        out_specs=(pl.BlockSpec((C, 4 * d_a), lambda c: (nc - 1 - c, 0)), _full((1, d_a)), _full((1, d_a))),
        out_shape=(_sds(z.shape, bf16), _sds((1, d_a), f32), _sds((1, d_a), f32)),
        scratch_shapes=[pltpu.VMEM((nh, HEAD, HEAD), f32)],
        compiler_params=_cparams(1),
    )(z, z, z, z, lb, gain, tri, o_pre, states, dout)


def _one_minus_exp(x):
    series = -x * (1.0 + x * (0.5 + x * (1.0 / 6.0 + x * (1.0 / 24.0))))
    return jnp.where(x > -0.03, series, 1.0 - jnp.exp(x))


def _lru_pre(xc, wa, ba, wx, bx, lam):
    xb16 = xc.astype(bf16)
    r = jax.nn.sigmoid(_dot(xb16, wa.astype(bf16)) + ba)
    gi = jax.nn.sigmoid(_dot(xb16, wx.astype(bf16)) + bx)
    log_a = -LRU_C * r * jax.nn.softplus(-lam)
    a = jnp.exp(log_a)
    mult = jnp.sqrt(_one_minus_exp(2.0 * log_a))
    return a, mult * gi * xc


def _lru_post(h, gate, gain, avg):
    y = h * jax.nn.gelu(gate)
    ms = _group_mean(y * y, avg)
    return y * lax.rsqrt(ms + EPS) * gain


def _shift_down(x, d, prev):
    row = lax.broadcasted_iota(jnp.int32, x.shape, 0)
    return jnp.where(row >= d, pltpu.roll(x, d, 0), pltpu.roll(prev, d, 0))


def _shift_up(x, d, nxt):
    n = x.shape[0]
    row = lax.broadcasted_iota(jnp.int32, x.shape, 0)
    return jnp.where(row < n - d, pltpu.roll(x, n - d, 0), pltpu.roll(nxt, n - d, 0))


def _scan_rows(a, u, reverse):
    n = a.shape[0]
    row = lax.broadcasted_iota(jnp.int32, a.shape, 0)
    d = 1
    while d < n:
        shift, ok = (n - d, row < n - d) if reverse else (d, row >= d)
        su = jnp.where(ok, pltpu.roll(u, shift, 0), 0.0)
        sa = jnp.where(ok, pltpu.roll(a, shift, 0), 1.0)
        u = u + a * su
        a = a * sa
        d *= 2
    return a, u


def _conv(xb, xprev, cw, cb):
    xc = cb + cw[CONV_WIDTH - 1:CONV_WIDTH] * xb
    for d in range(1, CONV_WIDTH):
        xc = xc + cw[CONV_WIDTH - 1 - d:CONV_WIDTH - d] * _shift_down(xb, d, xprev)
    return xc


def _lru_fwd(z, col0, d_b, cw, cb, wa, ba, wx, bx, lam, gain, avg):
    T = z.shape[0]
    R = min(B_CHUNK, T)
    nr = T // R
    jb = col0 // d_b

    def body(xb_ref, gate_ref, cw_ref, cb_ref, wa_ref, ba_ref, wx_ref, bx_ref, lam_ref, gain_ref, avg_ref,
             out_ref, h_ref, xprev_ref, hprev_ref):
        @pl.when(pl.program_id(0) == 0)
        def _():
            xprev_ref[...] = jnp.zeros_like(xprev_ref)
            hprev_ref[...] = jnp.zeros_like(hprev_ref)

        xb = xb_ref[...]
        xc = _conv(xb, xprev_ref[...], cw_ref[...], cb_ref[...])
        a, u = _lru_pre(xc, wa_ref[...], ba_ref[...], wx_ref[...], bx_ref[...], lam_ref[...])
        acum, hl = _scan_rows(a, u, False)
        h = hl + acum * hprev_ref[R - 1:R, :]
        h_ref[...] = h
        out_ref[...] = _lru_post(h, gate_ref[...], gain_ref[...], avg_ref[...]).astype(bf16)
        xprev_ref[...] = xb
        hprev_ref[...] = h

    vec = _full((1, d_b))
    return pl.pallas_call(
        body, name="lru_fwd", grid=(nr,),
        in_specs=[pl.BlockSpec((R, d_b), lambda i: (i, jb)), pl.BlockSpec((R, d_b), lambda i: (i, jb + 1)),
                  _full((CONV_WIDTH, d_b)), vec, _full((d_b, d_b)), vec, _full((d_b, d_b)), vec, vec, vec, _full((d_b, d_b))],
        out_specs=(pl.BlockSpec((R, d_b), lambda i: (i, 0)), pl.BlockSpec((R, d_b), lambda i: (i, 0))),
        out_shape=(_sds((T, d_b), bf16), _sds((T, d_b), f32)),
        scratch_shapes=[pltpu.VMEM((R, d_b), f32), pltpu.VMEM((R, d_b), f32)],
        compiler_params=_cparams(1),
    )(z, z, cw, cb, wa, ba, wx, bx, lam, gain, avg)


def _lru_bwd(z, col0, d_b, hsave, dout, dz_buf, cw, cb, wa, ba, wx, bx, lam, gain, avg):
    T = z.shape[0]
    R = min(B_CHUNK, T)
    nr = T // R
    jb = col0 // d_b

    def body(xb_ref, xp_ref, gate_ref, h_ref, hp_ref, do_ref,
             cw_ref, cb_ref, wa_ref, ba_ref, wx_ref, bx_ref, lam_ref, gain_ref, avg_ref, dzin_ref,
             dz_ref, dcw_ref, dcb_ref, dwa_ref, dba_ref, dwx_ref, dbx_ref, dlam_ref, dgain_ref,
             gfirst_ref, afirst_ref, dxcn_ref):
        step = pl.program_id(0)
        first_in_time = step == nr - 1

        @pl.when(step == 0)
        def _():
            for r in (dcw_ref, dcb_ref, dwa_ref, dba_ref, dwx_ref, dbx_ref, dlam_ref, dgain_ref,
                      gfirst_ref, afirst_ref, dxcn_ref):
                r[...] = jnp.zeros_like(r)

        xb = xb_ref[...]
        keep = jnp.where(first_in_time, 0.0, 1.0)
        xprev = xp_ref[...] * keep
        hprev = hp_ref[...] * keep
        cw = cw_ref[...]
        xc = _conv(xb, xprev, cw, cb_ref[...])
        (a, _), vjp_pre = jax.vjp(_lru_pre, xc, wa_ref[...], ba_ref[...], wx_ref[...], bx_ref[...], lam_ref[...])
        h = h_ref[...]
        avg = avg_ref[...]
        _, vjp_post = jax.vjp(lambda hh, gg, gn: _lru_post(hh, gg, gn, avg), h, gate_ref[...], gain_ref[...])
        dh, dgate, dgain = vjp_post(do_ref[...])
        a_next = _shift_up(a, 1, jnp.broadcast_to(afirst_ref[0:1, :], a.shape))
        acum, gl = _scan_rows(a_next, dh, True)
        gtot = gl + acum * gfirst_ref[0:1, :]
        da = gtot * _shift_down(h, 1, hprev)
        dxc, dwa, dba, dwx, dbx, dlam = vjp_pre((da, gtot))
        dxcn = dxcn_ref[...]
        dxb = cw[CONV_WIDTH - 1:CONV_WIDTH] * dxc
        dcw_ref[CONV_WIDTH - 1:CONV_WIDTH, :] += jnp.sum(dxc * xb, axis=0, keepdims=True)
        for d in range(1, CONV_WIDTH):
            tap = CONV_WIDTH - 1 - d
            dxb = dxb + cw[tap:tap + 1] * _shift_up(dxc, d, dxcn)
            dcw_ref[tap:tap + 1, :] += jnp.sum(dxc * _shift_down(xb, d, xprev), axis=0, keepdims=True)
        dz_ref[:, 0:d_b] = dxb.astype(bf16)
        dz_ref[:, d_b:2 * d_b] = dgate.astype(bf16)
        dcb_ref[...] += jnp.sum(dxc, axis=0, keepdims=True)
        dwa_ref[...] += dwa
        dba_ref[...] += dba
        dwx_ref[...] += dwx
        dbx_ref[...] += dbx
        dlam_ref[...] += dlam
        dgain_ref[...] += dgain
        gfirst_ref[...] = jnp.broadcast_to(gtot[0:1, :], gfirst_ref.shape)
        afirst_ref[...] = jnp.broadcast_to(a[0:1, :], afirst_ref.shape)
        dxcn_ref[...] = dxc

    vec = _full((1, d_b))
    mat = _full((d_b, d_b))

    def cur(j):
        return pl.BlockSpec((R, d_b), lambda i, j=j: (nr - 1 - i, j))

    def prev(j):
        return pl.BlockSpec((R, d_b), lambda i, j=j: (jnp.maximum(nr - 2 - i, 0), j))

    return pl.pallas_call(
        body, name="lru_bwd", grid=(nr,),
        in_specs=[cur(jb), prev(jb), cur(jb + 1), cur(0), prev(0), cur(0),
                  _full((CONV_WIDTH, d_b)), vec, mat, vec, mat, vec, vec, vec, mat, ANY_SPEC],
        out_specs=(pl.BlockSpec((R, 2 * d_b), lambda i: (nr - 1 - i, col0 // (2 * d_b))), _full((CONV_WIDTH, d_b)), vec, mat, vec, mat, vec, vec, vec),
        out_shape=(_sds(dz_buf.shape, bf16), _sds((CONV_WIDTH, d_b), f32), _sds((1, d_b), f32), _sds((d_b, d_b), f32),
                   _sds((1, d_b), f32), _sds((d_b, d_b), f32), _sds((1, d_b), f32), _sds((1, d_b), f32), _sds((1, d_b), f32)),
        scratch_shapes=[pltpu.VMEM((8, d_b), f32), pltpu.VMEM((8, d_b), f32), pltpu.VMEM((R, d_b), f32)],
        input_output_aliases={15: 0},
        compiler_params=_cparams(1),
    )(z, z, z, hsave, hsave, dout, cw, cb, wa, ba, wx, bx, lam, gain, avg, dz_buf)


def _two_pass(x, m16):
    hi = x.astype(bf16)
    lo = (x - hi.astype(f32)).astype(bf16)
    return _dot(hi, m16) + _dot(lo, m16)


@jax.custom_vjp
def _group_mean(x, avg):
    return _two_pass(x, avg.astype(bf16))


def _group_mean_fwd(x, avg):
    return _group_mean(x, avg), avg


def _group_mean_bwd(avg, ct):
    return _two_pass(ct, avg.astype(bf16)), jnp.zeros_like(avg)


_group_mean.defvjp(_group_mean_fwd, _group_mean_bwd)


def _sgu_chunk(u_in, v_in, w, bexp, gain, avg, n_groups):
    C, d_c = u_in.shape
    gd = d_c // n_groups
    u = jax.nn.gelu(u_in)
    v = jax.nn.gelu(v_in)
    mu = _group_mean(v, avg)
    vc = v - mu
    var = _group_mean(vc * vc, avg)
    vh = (vc * lax.rsqrt(var + EPS)).astype(bf16)
    lane = lax.broadcasted_iota(jnp.int32, (1, d_c), 1)
    causal = lax.broadcasted_iota(jnp.int32, (C, C), 0) >= lax.broadcasted_iota(jnp.int32, (C, C), 1)
    zz = bexp
    for g in range(n_groups):
        wg = jnp.where(causal, w[g], 0.0).astype(bf16)
        zz = zz + jnp.where((lane >= g * gd) & (lane < (g + 1) * gd), _dot(wg, vh), 0.0)
    y = u * zz
    ms = _group_mean(y * y, avg)
    return y * lax.rsqrt(ms + EPS) * gain


def _sgu_inner(T):
    return C_INNER if T % (C_CHUNK * C_INNER) == 0 else 1


def _sgu_fwd(z, col0, d_c, w, bexp, gain, avg):
    T = z.shape[0]
    C = C_CHUNK
    n_in = _sgu_inner(T)
    R = C * n_in
    jb = col0 // d_c
    G = w.shape[0]

    def body(u_ref, v_ref, w_ref, b_ref, gain_ref, avg_ref, out_ref):
        w_v, b_v, gain_v, avg = w_ref[...], b_ref[...], gain_ref[...], avg_ref[...]
        for j in range(n_in):
            rows = pl.ds(j * C, C)
            out_ref[rows, :] = _sgu_chunk(u_ref[rows, :], v_ref[rows, :], w_v, b_v, gain_v, avg, G).astype(bf16)

    return pl.pallas_call(
        body, name="sgu_fwd", grid=(T // R,),
        in_specs=[pl.BlockSpec((R, d_c), lambda i: (i, jb)), pl.BlockSpec((R, d_c), lambda i: (i, jb + 1)),
                  _full((G, C, C)), _full((C, d_c)), _full((1, d_c)), _full((d_c, d_c))],
        out_specs=pl.BlockSpec((R, d_c), lambda i: (i, 0)),
        out_shape=_sds((T, d_c), bf16),
        compiler_params=_cparams(1),
    )(z, z, w, bexp, gain, avg)


def _sgu_bwd(z, col0, d_c, dout, dz_buf, w, bexp, gain, avg):
    T = z.shape[0]
    C = C_CHUNK
    n_in = _sgu_inner(T)
    R = C * n_in
    nc = T // R
    jb = col0 // d_c
    G = w.shape[0]
    gd = d_c // G

    def body(u_ref, v_ref, do_ref, w_ref, b_ref, gain_ref, avg_ref, dzin_ref, dz_ref, dw_ref, db_ref, dgain_ref, dbexp_ref):
        step = pl.program_id(0)

        @pl.when(step == 0)
        def _():
            dw_ref[...] = jnp.zeros_like(dw_ref)
            dgain_ref[...] = jnp.zeros_like(dgain_ref)
            dbexp_ref[...] = jnp.zeros_like(dbexp_ref)

        avg, w_v, b_v, gain_v = avg_ref[...], w_ref[...], b_ref[...], gain_ref[...]
        dw = dbexp = dgain = None
        for j in range(n_in):
            rows = pl.ds(j * C, C)
            _, vjp = jax.vjp(lambda a, b, c, d, e: _sgu_chunk(a, b, c, d, e, avg, G),
                             u_ref[rows, :], v_ref[rows, :], w_v, b_v, gain_v)
            du, dv, dw_j, dbexp_j, dgain_j = vjp(do_ref[rows, :])
            dz_ref[rows, 0:d_c] = du.astype(bf16)
            dz_ref[rows, d_c:2 * d_c] = dv.astype(bf16)
            dw = dw_j if dw is None else dw + dw_j
            dbexp = dbexp_j if dbexp is None else dbexp + dbexp_j
            dgain = dgain_j if dgain is None else dgain + dgain_j
        dw_ref[...] += dw
        dbexp_ref[...] += dbexp
        dgain_ref[...] += dgain

        @pl.when(step == nc - 1)
        def _():
            lane = lax.broadcasted_iota(jnp.int32, (1, d_c), 1)
            acc = dbexp_ref[...]
            for g in range(G):
                sel = jnp.where((lane >= g * gd) & (lane < (g + 1) * gd), acc, 0.0)
                db_ref[:, g:g + 1] = jnp.sum(sel, axis=1, keepdims=True)

    return pl.pallas_call(
        body, name="sgu_bwd", grid=(nc,),
        in_specs=[pl.BlockSpec((R, d_c), lambda i: (i, jb)), pl.BlockSpec((R, d_c), lambda i: (i, jb + 1)),
                  pl.BlockSpec((R, d_c), lambda i: (i, 0)),
                  _full((G, C, C)), _full((C, d_c)), _full((1, d_c)), _full((d_c, d_c)), ANY_SPEC],
        out_specs=(pl.BlockSpec((R, 2 * d_c), lambda i: (i, col0 // (2 * d_c))), _full((G, C, C)), _full((C, G)), _full((1, d_c))),
        out_shape=(_sds(dz_buf.shape, bf16), _sds((G, C, C), f32), _sds((C, G), f32), _sds((1, d_c), f32)),
        scratch_shapes=[pltpu.VMEM((C, d_c), f32)],
        input_output_aliases={7: 0},
        compiler_params=_cparams(1),
    )(z, z, dout, w, bexp, gain, avg, dz_buf)


def _loss_head(h, gain, target, tm):
    T, D = h.shape
    nt = T // tm

    def body(h_ref, gain_ref, tgt_ref, dh_ref, loss_ref, dgain_ref):
        @pl.when(pl.program_id(0) == 0)
        def _():
            loss_ref[...] = jnp.zeros_like(loss_ref)
            dgain_ref[...] = jnp.zeros_like(dgain_ref)

        hv = h_ref[...]
        gain_v = gain_ref[...]
        r = lax.rsqrt(jnp.mean(hv * hv, axis=-1, keepdims=True) + EPS)
        xh = hv * r
        e = xh * gain_v - tgt_ref[...]
        loss_ref[...] += 0.5 * jnp.sum(jnp.mean(e * e, axis=-1, keepdims=True), axis=0, keepdims=True)
        dy = e * (1.0 / D)
        dgain_ref[...] += jnp.sum(dy * xh, axis=0, keepdims=True)
        dxh = dy * gain_v
        dh_ref[...] = r * (dxh - xh * jnp.mean(dxh * xh, axis=-1, keepdims=True))

    tok = pl.BlockSpec((tm, D), lambda i: (i, 0))
    return pl.pallas_call(
        body, name="loss_head", grid=(nt,),
        in_specs=[tok, _full((1, D)), tok],
        out_specs=(tok, _full((1, 128)), _full((1, D))),
        out_shape=(_sds((T, D), f32), _sds((1, 128), f32), _sds((1, D), f32)),
        compiler_params=_cparams(1),
    )(h, gain, target)


def _lower_bounds_fn(logits):
    n = logits.shape[0]
    mx = jnp.max(logits, axis=0, keepdims=True)
    ex = jnp.exp(logits - mx)
    soft = ex / jnp.sum(ex, axis=0, keepdims=True)
    rows = [jnp.zeros_like(soft[0:1])]
    for l in range(1, n):
        rows.append(rows[-1] + soft[l:l + 1])
    return jnp.concatenate(rows, axis=0)


def _lower_bounds(logits):
    def body(x_ref, o_ref):
        o_ref[...] = _lower_bounds_fn(x_ref[...])

    return pl.pallas_call(body, name="lower_bounds", out_shape=_sds(logits.shape, f32))(logits)


def _lower_bounds_bwd(logits, dlb):
    def body(x_ref, d_ref, o_ref):
        _, vjp = jax.vjp(_lower_bounds_fn, x_ref[...])
        o_ref[...] = vjp(d_ref[...])[0]

    return pl.pallas_call(body, name="lower_bounds_bwd", out_shape=_sds(logits.shape, f32))(logits, dlb)


def _adamw(w, g, m, v, rows_blk, row_range=None, prev=(), after=None):
    R, Cc = w.shape
    lo, hi = (0, R) if row_range is None else row_range
    span = hi - lo
    rb = span if (span <= rows_blk and lo % span == 0) else math.gcd(math.gcd(span, lo), rows_blk)
    extra = list(prev) + ([] if after is None else [after])

    def body(w_ref, g_ref, m_ref, v_ref, *rest):
        d_ref, nm_ref, nv_ref, go_ref = rest[len(extra):]
        gv = g_ref[...]
        m2 = ADAM_B1 * m_ref[...] + (1.0 - ADAM_B1) * gv
        v2 = ADAM_B2 * v_ref[...] + (1.0 - ADAM_B2) * (gv * gv)
        m_hat = m2 / (1.0 - ADAM_B1 ** ADAM_STEP)
        v_hat = v2 / (1.0 - ADAM_B2 ** ADAM_STEP)
        d_ref[...] = -ADAM_LR * (m_hat / (jnp.sqrt(v_hat) + ADAM_EPS) + ADAM_WD * w_ref[...])
        nm_ref[...] = m2
        nv_ref[...] = v2
        go_ref[...] = gv

    first = lo // rb
    spec = pl.BlockSpec((rb, Cc), lambda i: (i + first, 0))
    return pl.pallas_call(
        body, name="adamw", grid=((hi - lo) // rb,),
        in_specs=[spec] * 4 + [ANY_SPEC] * len(extra), out_specs=(spec,) * 4, out_shape=(_sds((R, Cc), f32),) * 4,
        input_output_aliases={4 + j: j for j in range(len(prev))},
        compiler_params=_cparams(1),
    )(w, g, m, v, *extra)


def _pair_sum(grads, recv, c_arr):
    n = len(grads)
    nsh = grads[0].shape[0]

    def body(c_ref, *refs):
        for a in range(n):
            refs[2 * n + a][...] = (refs[a][...].astype(f32) + refs[n + a][...].astype(f32)).astype(bf16)

    g_specs, r_specs, out_shape = [], [], []
    for g in grads:
        _, R, Cc = g.shape
        r2 = R // 2
        g_specs.append(pl.BlockSpec((None, r2, Cc), lambda s, c: (s, c[0], 0)))
        r_specs.append(pl.BlockSpec((None, r2, Cc), lambda s, c: (s, 0, 0)))
        out_shape.append(_sds((nsh, r2, Cc), bf16))
    gs = pltpu.PrefetchScalarGridSpec(num_scalar_prefetch=1, grid=(nsh,), in_specs=g_specs + r_specs, out_specs=tuple(r_specs))
    return list(pl.pallas_call(body, name="pair_sum", grid_spec=gs, out_shape=tuple(out_shape),
                               compiler_params=_cparams(1))(c_arr, *grads, *recv))


def _add(a, b):
    def body(a_ref, b_ref, o_ref):
        o_ref[...] = a_ref[...] + b_ref[...]

    return pl.pallas_call(body, name="pair_sum_small", out_shape=_sds(a.shape, f32))(a, b)


def _chip_sum(hsum, recv, bufs, slot_arr, c_arr, layer, n_layers):
    n = len(hsum)
    prev = list(bufs)
    steps = 2

    def body(s_ref, c_ref, *refs):
        outs = refs[len(refs) - n:]
        for a in range(n):
            acc = refs[a][...].astype(f32)
            for j in range(N_CHIPS - 1):
                acc = acc + refs[n + a][j].astype(f32)
            outs[a][...] = acc

    h_specs, r_specs, o_specs, out_shape = [], [], [], []
    for hh in hsum:
        _, r2, Cc = hh.shape
        rt = r2 // steps
        h_specs.append(pl.BlockSpec((None, rt, Cc), lambda i, s, c: (s[0], i, 0)))
        r_specs.append(pl.BlockSpec((N_CHIPS - 1, rt, Cc), lambda i, s, c: (0, i, 0)))
        o_specs.append(pl.BlockSpec((None, rt, Cc), lambda i, s, c: (layer, c[0] * steps + i, 0)))
        out_shape.append(_sds((n_layers, 2 * r2, Cc), f32))
    gs = pltpu.PrefetchScalarGridSpec(num_scalar_prefetch=2, grid=(steps,),
                                      in_specs=h_specs + r_specs + [ANY_SPEC] * len(prev), out_specs=tuple(o_specs))
    return list(pl.pallas_call(body, name="chip_sum", grid_spec=gs, out_shape=tuple(out_shape),
                               input_output_aliases={2 + 2 * n + a: a for a in range(len(prev))},
                               compiler_params=_cparams(1))(slot_arr, c_arr, *hsum, *recv, *prev))


def _sum_slots(x):
    def body(x_ref, o_ref):
        acc = x_ref[0]
        for j in range(1, x.shape[0]):
            acc = acc + x_ref[j]
        o_ref[...] = acc

    return pl.pallas_call(body, name="sum_slots", out_shape=_sds(x.shape[1:], f32))(x)


def _blockdiag(w):
    nb, bd, _ = w.shape
    eye = jnp.eye(nb, dtype=w.dtype)
    return (eye[:, None, :, None] * w[:, :, None, :]).reshape(nb * bd, nb * bd)


def _blockdiag_extract(dense, nb):
    bd = dense.shape[0] // nb
    d4 = dense.reshape(nb, bd, nb, bd)
    return jnp.stack([d4[i, :, i, :] for i in range(nb)])


def _pack(arrays, multiple):
    flat = jnp.concatenate([a.reshape(-1).astype(f32) for a in arrays])
    pad = (-flat.shape[0]) % multiple
    return jnp.pad(flat, (0, pad))


def _unpack(flat, shapes):
    out, off = [], 0
    for s in shapes:
        n = int(np.prod(s))
        out.append(flat[off:off + n].reshape(s))
        off += n
    return out


BIG = ("ffn1_wg", "ffn1_wu", "ffn1_wd", "w_in", "w_out", "ffn2_wg", "ffn2_wu", "ffn2_wd")
TRANSPOSED = ("ffn1_wg", "ffn1_wu", "ffn2_wg", "ffn2_wu")
SMALL = ("ffn1_norm", "mix_norm", "hgrn_lb_logits", "hgrn_norm", "conv_w", "conv_b", "lru_wa", "lru_ba", "lru_wx",
         "lru_bx", "lru_lambda", "lru_norm", "sgu_w", "sgu_b", "sgu_norm", "ffn2_norm", "final_norm")
WEIGHTS = ("ffn1_norm", "ffn1_wg", "ffn1_wu", "ffn1_wd", "mix_norm", "w_in", "hgrn_lb_logits", "hgrn_norm", "conv_w",
           "conv_b", "lru_wa", "lru_ba", "lru_wx", "lru_bx", "lru_lambda", "lru_norm", "sgu_w", "sgu_b", "sgu_norm",
           "w_out", "ffn2_norm", "ffn2_wg", "ffn2_wu", "ffn2_wd", "final_norm")


def kernel(x, ffn1_norm, ffn1_wg, ffn1_wu, ffn1_wd, mix_norm, w_in, hgrn_lb_logits, hgrn_norm, conv_w, conv_b, lru_wa, lru_ba, lru_wx, lru_bx, lru_lambda, lru_norm, sgu_w, sgu_b, sgu_norm, w_out, ffn2_norm, ffn2_wg, ffn2_wu, ffn2_wd, final_norm, loss_target, m_ffn1_norm, m_ffn1_wg, m_ffn1_wu, m_ffn1_wd, m_mix_norm, m_w_in, m_hgrn_lb_logits, m_hgrn_norm, m_conv_w, m_conv_b, m_lru_wa, m_lru_ba, m_lru_wx, m_lru_bx, m_lru_lambda, m_lru_norm, m_sgu_w, m_sgu_b, m_sgu_norm, m_w_out, m_ffn2_norm, m_ffn2_wg, m_ffn2_wu, m_ffn2_wd, m_final_norm, v_ffn1_norm, v_ffn1_wg, v_ffn1_wu, v_ffn1_wd, v_mix_norm, v_w_in, v_hgrn_lb_logits, v_hgrn_norm, v_conv_w, v_conv_b, v_lru_wa, v_lru_ba, v_lru_wx, v_lru_bx, v_lru_lambda, v_lru_norm, v_sgu_w, v_sgu_b, v_sgu_norm, v_w_out, v_ffn2_norm, v_ffn2_wg, v_ffn2_wu, v_ffn2_wd, v_final_norm):
    args = dict(locals())
    W = {n: args[n] for n in WEIGHTS}
    M = {n: args["m_" + n] for n in WEIGHTS}
    V = {n: args["v_" + n] for n in WEIGHTS}

    T, D = x.shape[1], x.shape[2]
    L = ffn1_norm.shape[0]
    d_a, d_b, d_c = hgrn_norm.shape[1], lru_norm.shape[1], sgu_norm.shape[1]
    col_b, col_c = 4 * d_a, 4 * d_a + 2 * d_b
    tm = 512 if T % 512 == 0 else T
    tm_w = 1024 if T % 1024 == 0 else tm
    tm_d = 256 if T % 256 == 0 else tm
    my_c = lax.axis_index("c")
    my_slot = 2 * lax.axis_index("x") + lax.axis_index("y")
    c_arr = jnp.reshape(my_c, (1,)).astype(jnp.int32)
    slot_arr = jnp.reshape(my_slot, (1,)).astype(jnp.int32)

    nb = len(BIG)
    gplan = _gather_ici_plan(nb)

    def kview(a, n):
        return jnp.swapaxes(a, 1, 2) if n in TRANSPOSED else a

    Wk = {n: kview(W[n], n) for n in BIG}
    place_steps = 4 if all(Wk[n].shape[1] % 64 == 0 for n in BIG) else 2

    conv_land = lax.dynamic_update_slice_in_dim(jnp.zeros((N_CHIPS,) + conv_w.shape, f32), conv_w[None], my_slot, axis=0)
    lands0 = _cast_place([Wk[n] for n in BIG], 0, slot_arr, place_steps)
    n_first = 3
    first_plan = _gather_ici_plan(n_first + 1)
    send, recv, _, first, token = _start_copies("gather_start_first", [], lands0[:n_first] + [conv_land], first_plan)
    later = {l: _cast_place([Wk[n] for n in BIG], l, slot_arr, place_steps, after=token) for l in range(1, L)}
    got = _wait_copies("gather_wait_first", send, recv, [], first, first_plan, later[L - 1][0] if later else lands0[-1])
    got = _gather_d2d("gather0_d2d", got)

    def placed(l):
        return later[l]
    G = [None] * L
    G[0] = dict(zip(BIG[:n_first], got[:n_first]))
    conv_full = jnp.transpose(got[n_first], (1, 2, 0, 3)).reshape(L, CONV_WIDTH, d_b)
    rest_plan = _gather_ici_plan(nb - n_first)
    rest_pending = _start_copies("gather_start_0", [], lands0[n_first:], rest_plan, got[0])

    def start_gather(l, after):
        return _start_copies(f"gather_start_{l}", [], placed(l), gplan, after)

    d2d_plan = _gather_d2d_plan(nb)

    lb = _lower_bounds(hgrn_lb_logits)
    avg_b = _group_avg_matrix(d_b, d_b // B_BLOCKS)
    avg_c = _group_avg_matrix(d_c, d_c // C_GROUPS)
    wa_dense = [_blockdiag(lru_wa[l]) for l in range(L)]
    wx_dense = [_blockdiag(lru_wx[l]) for l in range(L)]
    bexp = [jnp.repeat(sgu_b[l].T, d_c // C_GROUPS, axis=1) for l in range(L)]

    def lru_params(l):
        return (conv_full[l], conv_b[l][None], wa_dense[l], lru_ba[l].reshape(1, d_b), wx_dense[l],
                lru_bx[l].reshape(1, d_b), lru_lambda[l][None], lru_norm[l][None], avg_b)

    h = x.reshape(T, D)
    saved = []
    for l in range(L):
        s = {"h0": h}
        gain1, gain_mix = ffn1_norm[l][None], mix_norm[l][None]
        pending = None
        if l == 0:
            gain1 = gain1 + rest_pending[4][0:1, 0:1]
        elif l + 1 < L:
            pending = start_gather(l + 1, h)
            gain1 = gain1 + pending[4][0:1, 0:1]
        g = G[l]
        h, s["g1"], s["u1"] = _ffn_fwd(h, gain1, g["ffn1_wg"], g["ffn1_wu"], g["ffn1_wd"], tm)
        s["h1"] = h
        if l == 0:
            send, recv, _, lands, _ = rest_pending
            lands = _wait_copies("gather_wait_0", send, recv, [], lands, rest_plan, h)
            g.update(zip(BIG[n_first:], _gather_d2d("gather_d2d", lands)))
            if L > 1:
                pending = start_gather(1, g["w_in"])
                gain_mix = gain_mix + pending[4][0:1, 0:1]
        z = _proj_in_fwd(h, gain_mix, g["w_in"], tm)
        s["z"] = z
        s["oa"], s["o_pre"], s["states"] = _hgrn_fwd(z, lb[l][None], hgrn_norm[l][None], d_a)
        s["ob"], s["hl"] = _lru_fwd(z, col_b, d_b, *lru_params(l))
        s["oc"] = _sgu_fwd(z, col_c, d_c, sgu_w[l], bexp[l], sgu_norm[l][None], avg_c)
        h = _proj_out_fwd(h, s["oa"], s["ob"], s["oc"], g["w_out"], tm)
        s["h2"] = h
        gain2 = ffn2_norm[l][None]
        forward = None
        if pending is not None:
            send, recv, _, lands, _ = pending
            lands = _wait_copies(f"gather_wait_{l + 1}", send, recv, [], lands, gplan, h)
            forward = _start_copies(f"gather_d2d_start_{l + 1}", [], lands, d2d_plan)
            gain2 = gain2 + forward[4][0:1, 0:1]
        h, s["g2"], s["u2"] = _ffn_fwd(h, gain2, g["ffn2_wg"], g["ffn2_wu"], g["ffn2_wd"], tm)
        saved.append(s)
        if forward is not None:
            send, recv, _, lands, _ = forward
            G[l + 1] = dict(zip(BIG, _wait_copies(f"gather_d2d_wait_{l + 1}", send, recv, [], lands, d2d_plan, h)))

    dh, loss_part, d_final = _loss_head(h, final_norm[None], loss_target.reshape(T, D), tm)
    loss = lax.psum(loss_part[0, 0], ("x", "y", "c"))

    def pair_views(n_big):
        r = [(lambda i, o, p, a=a: i[a].at[:, pl.ds((1 - p.c) * (i[a].shape[1] // 2), i[a].shape[1] // 2)],
              lambda i, o, p, a=a: o[a], "sib") for a in range(n_big)]
        return r

    def chip_plan_for(n):
        return [(lambda s_, o, p, a=a, kind=kind: s_[a].at[p.peer_slot(kind)], lambda s_, o, p, a=a, j=j: o[a].at[j], kind)
                for a in range(n) for j, kind in enumerate(CHIP_KINDS)]

    chip_plan = chip_plan_for(nb)
    sbufs = {n: None for n in BIG}

    def pair_phase(arrs, extra=None):
        n = len(arrs)
        ins, remote = list(arrs), pair_views(n)
        outs = [_sds((N_CHIPS, a.shape[1] // 2, a.shape[2]), bf16) for a in arrs]
        if extra is not None:
            ins.append(extra)
            outs.append(_sds(extra.shape, f32))
            remote = remote + [(lambda i, o, p: i[n], lambda i, o, p: o[n], "sib")]
        recv = _exchange("grad_pair_d2d", ins, outs, remote)
        return _pair_sum(arrs, recv[:n], c_arr), (None if extra is None else _add(extra, recv[n]))

    def chip_sum_into(names, hs, lands, l):
        prev = [sbufs[n] for n in names] if sbufs[names[0]] is not None else []
        for n, buf in zip(names, _chip_sum(hs, lands, prev, slot_arr, c_arr, l, L)):
            sbufs[n] = buf

    def share_plan(l):
        view = lambda a: (lambda s_, o, p: _half(o[a].at[l], p.c))
        return [(view(a), view(a), "sib") for a in range(nb)]

    def share_start(l):
        send, recv, _, bufs, token = _start_copies(f"grad_share_start_{l}", [], [sbufs[n] for n in BIG], share_plan(l))
        for n, buf in zip(BIG, bufs):
            sbufs[n] = buf
        return (l, send, recv), token

    def share_wait(pending, after):
        l, send, recv = pending
        bufs = _wait_copies(f"grad_share_wait_{l}", send, recv, [], [sbufs[n] for n in BIG], share_plan(l), after)
        for n, buf in zip(BIG, bufs):
            sbufs[n] = buf

    def share(l, extra_in=(), extra_out=(), extra_remote=(), extra_local=()):
        remote = [(lambda i, o, p, a=a: _half(o[a].at[l], p.c), lambda i, o, p, a=a: _half(o[a].at[l], p.c), "sib")
                  for a in range(nb)]
        outs = [_sds(sbufs[n].shape, f32) for n in BIG] + list(extra_out)
        res = _exchange("grad_share_d2d", [sbufs[n] for n in BIG] + list(extra_in), outs, remote + list(extra_remote),
                        list(extra_local), aliases={a: a for a in range(nb)})
        for n, buf in zip(BIG, res[:nb]):
            sbufs[n] = buf
        return res[nb:]

    small = {n: [None] * L for n in SMALL if n != "final_norm"}
    chip_pending = pair_pending = early = share_token = None
    shares = []
    early_names = ("w_in", "w_out", "ffn2_wg", "ffn2_wu", "ffn2_wd")
    for l in reversed(range(L)):
        s, g = saved[l], G[l]
        gain2, gain_a = ffn2_norm[l][None], hgrn_norm[l][None]
        if pair_pending is not None:
            gain2 = gain2 + pair_pending[0][4][0:1, 0:1]
        if share_token is not None:
            gain2 = gain2 + share_token[0:1, 0:1]
            share_token = None
        dh, small["ffn2_norm"][l], dg, du, xn, dob = _ffn_bwd_dgrad(
            s["h2"], gain2, dh, s["g2"], s["u2"], g["ffn2_wg"], g["ffn2_wu"], g["ffn2_wd"], tm_d)
        if pair_pending is not None:
            (send, recv, grads_prev, lands, _), = pair_pending
            recv_a = _wait_copies(f"grad_pair_wait_{l + 1}", send, recv, grads_prev, lands, pair_views(nb), dh)
            hsum = _pair_sum(grads_prev, recv_a, c_arr)
            lands = [lax.empty((N_CHIPS - 1,) + hh.shape[1:], bf16) for hh in hsum]
            chip_pending = (_start_copies(f"grad_chip_start_{l + 1}", hsum, lands, chip_plan), hsum)
            gain_a = gain_a + chip_pending[0][4][0:1, 0:1]
            pair_pending = None
        dwg2, dwu2, dwd2 = _ffn_bwd_wgrad(xn, dob, s["g2"], s["u2"], dg, du, tm_w)
        doa, dob_, doc, dwo = _proj_out_bwd(dh, s["oa"], s["ob"], s["oc"], g["w_out"], tm)
        dz, small["hgrn_lb_logits"][l], small["hgrn_norm"][l] = _hgrn_bwd(
            s["z"], lb[l][None], gain_a, s["o_pre"], s["states"], doa, d_a)
        (dz, small["conv_w"][l], small["conv_b"][l], dwa, small["lru_ba"][l], dwx, small["lru_bx"][l],
         small["lru_lambda"][l], small["lru_norm"][l]) = _lru_bwd(s["z"], col_b, d_b, s["hl"], dob_, dz, *lru_params(l))
        small["lru_wa"][l] = _blockdiag_extract(dwa, B_BLOCKS)
        small["lru_wx"][l] = _blockdiag_extract(dwx, B_BLOCKS)
        dz, small["sgu_w"][l], dsb, small["sgu_norm"][l] = _sgu_bwd(
            s["z"], col_c, d_c, doc, dz, sgu_w[l], bexp[l], sgu_norm[l][None], avg_c)
        small["sgu_b"][l] = dsb.T
        dh, small["mix_norm"][l], xn = _proj_in_bwd_dgrad(s["h1"], mix_norm[l][None], dh, dz, g["w_in"], tm)
        dwi = _proj_in_bwd_wgrad(xn, dz, N_CHIPS)
        gain1 = ffn1_norm[l][None]
        if l == 0:
            hs_e, _ = pair_phase([dwi, dwo, dwg2, dwu2, dwd2])
            lands = [lax.empty((N_CHIPS - 1,) + hh.shape[1:], bf16) for hh in hs_e]
            early = (_start_copies("grad_chip_start_0", hs_e, lands, chip_plan_for(len(hs_e))), hs_e)
            gain1 = gain1 + early[0][4][0:1, 0:1]
        dh, small["ffn1_norm"][l], dg, du, xn, dob = _ffn_bwd_dgrad(
            s["h0"], gain1, dh, s["g1"], s["u1"], g["ffn1_wg"], g["ffn1_wu"], g["ffn1_wd"], tm_d)
        dwg1, dwu1, dwd1 = _ffn_bwd_wgrad(xn, dob, s["g1"], s["u1"], dg, du, tm_w)
        layer_grads = [dwg1, dwu1, dwd1, dwi, dwo, dwg2, dwu2, dwd2]

        if chip_pending is not None:
            (send, recv, hs, lands, _), hsum_prev = chip_pending
            lands = _wait_copies(f"grad_chip_wait_{l + 1}", send, recv, hs, lands, chip_plan, dwg1)
            chip_sum_into(BIG, hsum_prev, lands, l + 1)
            pending_share, share_token = share_start(l + 1)
            shares.append(pending_share)
            chip_pending = None
        if l > 0:
            lands = [lax.empty((N_CHIPS, gr.shape[1] // 2, gr.shape[2]), bf16) for gr in layer_grads]
            pair_pending = (_start_copies(f"grad_pair_start_{l}", layer_grads, lands, pair_views(nb)),)
    grad_x = dh.reshape(x.shape)

    (send, recv, hs, lands, _), hs_e = early
    lands = _wait_copies("grad_chip_wait_0", send, recv, hs, lands, chip_plan_for(len(hs_e)), dwg1)
    chip_sum_into(early_names, hs_e, lands, 0)
    small_names = [n for n in SMALL]
    small_parts = [jnp.stack([jnp.reshape(v, (-1,)) for v in small[n]]) if n != "final_norm" else d_final for n in small_names]
    small_shapes = [p.shape for p in small_parts]
    packed = _pack(small_parts, 2 * 8 * 128).reshape(2, -1, 128)
    n_rows = packed.shape[1]
    late = [dwg1, dwu1, dwd1]
    nl = len(late)
    hsum, small_pair = pair_phase(late, packed)
    own_half = lax.dynamic_index_in_dim(small_pair, my_c, 0, keepdims=True)
    small_land = lax.dynamic_update_slice_in_dim(jnp.zeros((N_CHIPS, n_rows, 128), f32), own_half, my_slot, axis=0)
    late_plan = chip_plan_for(nl) + [(lambda s_, o, p: s_[nl].at[p.c], lambda s_, o, p: o[nl].at[p.slot], kind)
                                     for kind in CHIP_KINDS]
    lands = [lax.empty((N_CHIPS - 1,) + hh.shape[1:], bf16) for hh in hsum] + [small_land]
    send, recv, srcs, lands, token = _start_copies("grad_chip_start_last", hsum + [small_pair], lands, late_plan)

    def adam_operands(n):
        shape = Wk[n].shape
        rows_blk = 512 if shape[1] % 512 == 0 else (shape[1] // 2 if shape[1] > 512 else shape[1])
        return [a.reshape(-1, shape[-1]) for a in (Wk[n], sbufs[n], kview(M[n], n), kview(V[n], n))], rows_blk, shape

    for pending_share in shares:
        share_wait(pending_share, token)
    partial = {}
    if L > 1:
        for n in BIG:
            flat, rows_blk, shape = adam_operands(n)
            partial[n] = _adamw(*flat, rows_blk, row_range=(shape[1], L * shape[1]), after=token)
    recv_b = _wait_copies("grad_chip_wait_last", send, recv, srcs, lands, late_plan,
                          partial[BIG[-1]][0] if partial else hsum[0])
    chip_sum_into(BIG[:nl], hsum, recv_b[:nl], 0)
    small_half = _sum_slots(recv_b[nl])
    (small_all,) = share(0, extra_in=[small_half], extra_out=[_sds(packed.shape, f32)],
                         extra_remote=[(lambda i, o, p: i[nb], lambda i, o, p: o[nb].at[p.c], "sib")],
                         extra_local=[(lambda i, o, p: i[nb], lambda i, o, p: o[nb].at[p.c])])
    grads = {n: kview(sbufs[n], n) for n in BIG}
    small_tot = _unpack(small_all.reshape(-1), small_shapes)
    for n, val in zip(small_names, small_tot):
        grads[n] = val
    grads["hgrn_lb_logits"] = _lower_bounds_bwd(hgrn_lb_logits, grads["hgrn_lb_logits"])
    shard_cols = conv_w.shape[2]
    grads["conv_w"] = lax.dynamic_slice_in_dim(grads["conv_w"].reshape(L, CONV_WIDTH, d_b), my_slot * shard_cols, shard_cols, axis=2)
    for n in SMALL:
        grads[n] = grads[n].reshape(W[n].shape)

    delta, new_m, new_v = {}, {}, {}
    for n in BIG:
        flat, rows_blk, shape = adam_operands(n)
        outs = _adamw(*flat, rows_blk, row_range=(0, shape[1]), prev=partial[n]) if partial else _adamw(*flat, rows_blk)
        delta[n], new_m[n], new_v[n], grads[n] = [kview(o.reshape(shape), n) for o in outs]
    shapes = [W[n].shape for n in SMALL]
    packs = [_pack([src[n] for n in SMALL], 8 * 128).reshape(-1, 128) for src in (W, grads, M, V)]
    d2, m2, v2, _ = _adamw(*packs, 4096)
    for dst, val in ((delta, d2), (new_m, m2), (new_v, v2)):
        for n, piece in zip(SMALL, _unpack(val.reshape(-1), shapes)):
            dst[n] = piece

    return (loss, grad_x, *[grads[n] for n in WEIGHTS], *[delta[n] for n in WEIGHTS],
            *[new_m[n] for n in WEIGHTS], *[new_v[n] for n in WEIGHTS])
```

```python
import math

import numpy as np
import jax
import jax.numpy as jnp
from jax import lax
from jax.experimental import pallas as pl
from jax.experimental.pallas import tpu as pltpu

f32 = jnp.float32
bf16 = jnp.bfloat16
HI = lax.Precision.HIGHEST
MESH = pl.DeviceIdType.MESH

EPS = 1e-6
HEAD = 128
A_CHUNK = 64
A_SUB = 16
A_INNER = 2
A_CHUNK_FWD = 32
A_INNER_FWD = 4
SUBLANES = 8
B_BLOCKS = 4
B_CHUNK = 256
CONV_WIDTH = 4
LRU_C = 8.0
C_GROUPS = 4
C_CHUNK = 128
C_INNER = 8
N_CHIPS = 4
ADAM_LR, ADAM_B1, ADAM_B2, ADAM_EPS, ADAM_WD, ADAM_STEP = 0.001, 0.9, 0.999, 1e-08, 0.01, 10
VMEM_LIMIT = 56 * 1024 * 1024


def _cparams(n_axes):
    return pltpu.CompilerParams(dimension_semantics=("arbitrary",) * n_axes, vmem_limit_bytes=VMEM_LIMIT)


def _sds(shape, dtype):
    return jax.ShapeDtypeStruct(tuple(shape), dtype)


def _full(shape):
    n = len(shape)
    return pl.BlockSpec(tuple(shape), lambda *_: (0,) * n)


def _resident(shape):
    n = len(shape)
    return pl.BlockSpec(tuple(shape), lambda *_: (0,) * n, pipeline_mode=pl.Buffered(1))


def _dot(a, b):
    return jnp.dot(a, b, preferred_element_type=f32)


def _dot_nt(a, b):
    return lax.dot_general(a, b, (((1,), (1,)), ((), ())), preferred_element_type=f32)


def _dot_tn(a, b):
    return lax.dot_general(a, b, (((0,), (0,)), ((), ())), preferred_element_type=f32)


def _silu(x):
    return x * jax.nn.sigmoid(x)


def _group_avg_matrix(n, group):
    idx = np.arange(n) // group
    return jnp.asarray((idx[:, None] == idx[None, :]).astype(np.float32) / group)


class _Place:
    def __init__(self):
        self.x, self.y, self.c = lax.axis_index("x"), lax.axis_index("y"), lax.axis_index("c")
        self.slot = 2 * self.x + self.y

    def peer(self, kind):
        x, y, c = self.x, self.y, self.c
        return {"sib": (x, y, 1 - c), "fx": (1 - x, y, c), "fy": (x, 1 - y, c), "fxy": (1 - x, 1 - y, c)}[kind]

    def peer_slot(self, kind):
        x, y = self.x, self.y
        return {"fx": 2 * (1 - x) + y, "fy": 2 * x + (1 - y), "fxy": 2 * (1 - x) + (1 - y)}[kind]


CHIP_KINDS = ("fx", "fy", "fxy")


def _exchange(name, ins, outs, remote, local=(), aliases=None):
    n_in, n_out, n_r, n_l = len(ins), len(outs), len(remote), len(local)

    def body(*refs):
        in_refs, out_refs = refs[:n_in], refs[n_in:n_in + n_out]
        send, recv, lsem = refs[n_in + n_out:]
        p = _Place()
        lcopies = []
        for t, (src, dst) in enumerate(local):
            cp = pltpu.make_async_copy(src(in_refs, out_refs, p), dst(in_refs, out_refs, p), lsem.at[t])
            cp.start()
            lcopies.append(cp)
        copies = []
        for t, (src, dst, kind) in enumerate(remote):
            cp = pltpu.make_async_remote_copy(
                src_ref=src(in_refs, out_refs, p), dst_ref=dst(in_refs, out_refs, p),
                send_sem=send.at[t], recv_sem=recv.at[t], device_id=p.peer(kind), device_id_type=MESH)
            cp.start()
            copies.append(cp)
        for cp in copies:
            cp.wait_recv()
        for cp in copies:
            cp.wait_send()
        for cp in lcopies:
            cp.wait()

    anyspec = pl.BlockSpec(memory_space=pl.ANY)
    res = pl.pallas_call(
        body, name=name, out_shape=tuple(outs),
        in_specs=[anyspec] * n_in, out_specs=tuple([anyspec] * n_out),
        scratch_shapes=[pltpu.SemaphoreType.DMA((n_r,)), pltpu.SemaphoreType.DMA((n_r,)),
                        pltpu.SemaphoreType.DMA((max(n_l, 1),))],
        input_output_aliases=aliases or {},
        compiler_params=pltpu.CompilerParams(has_side_effects=True),
    )(*ins)
    return list(res)


HBM_SPEC = pl.BlockSpec(memory_space=pltpu.HBM)
SEM_SPEC = pl.BlockSpec(memory_space=pltpu.SEMAPHORE)
ANY_SPEC = pl.BlockSpec(memory_space=pl.ANY)
DATAFLOW = pltpu.SideEffectType.DATAFLOW_SIDE_EFFECTING


def _in_hbm(a):
    return pltpu.with_memory_space_constraint(a, pltpu.HBM)


def _start_copies(name, srcs, lands, remote, after=None):
    n_s, n_l, n_r = len(srcs), len(lands), len(remote)
    extra = [] if after is None else [after]

    def body(*refs):
        src_refs, land_refs = refs[:n_s], refs[n_s:n_s + n_l]
        n_in = n_s + n_l + len(extra)
        send, recv = refs[n_in], refs[n_in + 1]
        token = refs[-1]
        p = _Place()
        for t, (src, dst, kind) in enumerate(remote):
            pltpu.make_async_remote_copy(
                src_ref=src(src_refs, land_refs, p), dst_ref=dst(src_refs, land_refs, p),
                send_sem=send.at[t], recv_sem=recv.at[t], device_id=p.peer(kind), device_id_type=MESH).start()
        token[...] = jnp.zeros_like(token)

    thru = [pltpu.HBM(a.shape, a.dtype) for a in lands]
    res = pl.pallas_call(
        body, name=name,
        out_shape=(pltpu.SemaphoreType.DMA((n_r,)), pltpu.SemaphoreType.DMA((n_r,)), *thru, _sds((8, 128), f32)),
        in_specs=[ANY_SPEC] * n_s + [HBM_SPEC] * n_l + [ANY_SPEC] * len(extra),
        out_specs=(SEM_SPEC, SEM_SPEC, *([HBM_SPEC] * n_l), pl.BlockSpec(memory_space=pltpu.VMEM)),
        input_output_aliases={n_s + i: 2 + i for i in range(n_l)},
        compiler_params=pltpu.CompilerParams(has_side_effects=DATAFLOW),
    )(*srcs, *[_in_hbm(a) for a in lands], *extra)
    return res[0], res[1], list(srcs), list(res[2:2 + n_l]), res[-1]


def _wait_copies(name, send, recv, srcs, lands, remote, after):
    n_s, n_l = len(srcs), len(lands)

    def body(*refs):
        src_refs, land_refs = refs[:n_s], refs[n_s:n_s + n_l]
        send_ref, recv_ref = refs[n_s + n_l], refs[n_s + n_l + 1]
        p = _Place()
        for t, (src, dst, kind) in enumerate(remote):
            cp = pltpu.make_async_remote_copy(
                src_ref=src(src_refs, land_refs, p), dst_ref=dst(src_refs, land_refs, p),
                send_sem=send_ref.at[t], recv_sem=recv_ref.at[t], device_id=p.peer(kind), device_id_type=MESH)
            cp.wait_send()
            cp.wait_recv()

    res = pl.pallas_call(
        body, name=name, out_shape=tuple(pltpu.HBM(a.shape, a.dtype) for a in lands),
        in_specs=[ANY_SPEC] * n_s + [HBM_SPEC] * n_l + [SEM_SPEC, SEM_SPEC, ANY_SPEC],
        out_specs=tuple([HBM_SPEC] * n_l),
        input_output_aliases={n_s + i: i for i in range(n_l)},
        compiler_params=pltpu.CompilerParams(has_side_effects=DATAFLOW),
    )(*srcs, *lands, send, recv, after)
    return list(res)


def _half(ref, c):
    n2 = ref.shape[0] // 2
    return ref.at[pl.ds(c * n2, n2)]


def _gather_ici_plan(n):
    def view(a):
        return lambda s, o, p: _half(o[a].at[p.slot], p.c)

    return [(view(a), view(a), kind) for a in range(n) for kind in CHIP_KINDS]


def _gather_d2d_plan(n):
    remote = []
    for a in range(n):
        for kind in CHIP_KINDS:
            view = lambda i, o, p, a=a, kind=kind: _half(o[a].at[p.peer_slot(kind)], p.c)
            remote.append((view, view, "sib"))
    return remote


def _gather_d2d(name, lands):
    n = len(lands)
    outs = [_sds(g.shape, g.dtype) for g in lands]
    return _exchange(name, list(lands), outs, _gather_d2d_plan(n), aliases={a: a for a in range(n)})


def _cast_place(weights, layer, slot_arr, n_steps=4, after=None):
    n = len(weights)
    extra = [] if after is None else [after]

    def body(s_ref, *refs):
        outs = refs[n + len(extra):]
        for a in range(n):
            outs[a][...] = refs[a][...].astype(bf16)

    in_specs, out_specs, out_shape = [], [], []
    for w in weights:
        _, R, Cc = w.shape
        rt = R // n_steps
        in_specs.append(pl.BlockSpec((None, rt, Cc), lambda i, s: (layer, i, 0)))
        out_specs.append(pl.BlockSpec((None, rt, Cc), lambda i, s: (s[0], i, 0)))
        out_shape.append(_sds((N_CHIPS, R, Cc), bf16))
    gs = pltpu.PrefetchScalarGridSpec(num_scalar_prefetch=1, grid=(n_steps,), in_specs=in_specs + [ANY_SPEC] * len(extra),
                                      out_specs=tuple(out_specs))
    return list(pl.pallas_call(body, name="cast_place", grid_spec=gs, out_shape=tuple(out_shape),
                               compiler_params=_cparams(1))(slot_arr, *weights, *extra))


def _ffn_fwd(h, gain, wg, wu, wd, tm):
    T, D = h.shape
    nsh, F = wg.shape[0], wg.shape[1]
    nt = T // tm

    def body(h_ref, gain_ref, wg_ref, wu_ref, wd_ref, out_ref, gs_ref, us_ref):
        hv = h_ref[...]
        r = lax.rsqrt(jnp.mean(hv * hv, axis=-1, keepdims=True) + EPS)
        xn = (hv * r * gain_ref[...]).astype(bf16)
        acc = None
        for k in range(nsh):
            g = _dot_nt(xn, wg_ref[k])
            u = _dot_nt(xn, wu_ref[k])
            gs_ref[k] = g.astype(bf16)
            us_ref[k] = u.astype(bf16)
            part = _dot((_silu(g) * u).astype(bf16), wd_ref[k])
            acc = part if acc is None else acc + part
        out_ref[...] = hv + 0.5 * acc

    sav = pl.BlockSpec((nsh, tm, F), lambda i: (0, i, 0))
    return pl.pallas_call(
        body, name="ffn_fwd", grid=(nt,),
        in_specs=[pl.BlockSpec((tm, D), lambda i: (i, 0)), _full((1, D)), _resident((nsh, F, D)), _resident((nsh, F, D)),
                  _resident((nsh, F, D))],
        out_specs=(pl.BlockSpec((tm, D), lambda i: (i, 0)), sav, sav),
        out_shape=(_sds((T, D), f32), _sds((nsh, T, F), bf16), _sds((nsh, T, F), bf16)),
        compiler_params=_cparams(1),
    )(h, gain, wg, wu, wd)


def _ffn_bwd_dgrad(h, gain, dout, gs, us, wg, wu, wd, tm):
    T, D = h.shape
    nsh, F = wg.shape[0], wg.shape[1]
    nt = T // tm

    def body(h_ref, gain_ref, dout_ref, gs_ref, us_ref, wg_ref, wu_ref, wd_ref,
             dh_ref, dgain_ref, dg_ref, du_ref, xn_ref, dob_ref):
        @pl.when(pl.program_id(0) == 0)
        def _():
            dgain_ref[...] = jnp.zeros_like(dgain_ref)

        hv = h_ref[...]
        r = lax.rsqrt(jnp.mean(hv * hv, axis=-1, keepdims=True) + EPS)
        xh = hv * r
        xn_ref[...] = (xh * gain_ref[...]).astype(bf16)
        dv = dout_ref[...]
        dob = (0.5 * dv).astype(bf16)
        dob_ref[...] = dob
        dxn = None
        for k in range(nsh):
            da = _dot_nt(dob, wd_ref[k])
            g = gs_ref[k].astype(f32)
            u = us_ref[k].astype(f32)
            sg = jax.nn.sigmoid(g)
            dg = (da * u * (sg * (1.0 + g * (1.0 - sg)))).astype(bf16)
            du = (da * (g * sg)).astype(bf16)
            dg_ref[k] = dg
            du_ref[k] = du
            part = _dot(dg, wg_ref[k]) + _dot(du, wu_ref[k])
            dxn = part if dxn is None else dxn + part
        dgain_ref[...] += jnp.sum(dxn * xh, axis=0, keepdims=True)
        dxh = dxn * gain_ref[...]
        dh_ref[...] = dv + r * (dxh - xh * jnp.mean(dxh * xh, axis=-1, keepdims=True))

    tok = pl.BlockSpec((tm, D), lambda i: (i, 0))
    sav = pl.BlockSpec((nsh, tm, F), lambda i: (0, i, 0))
    return pl.pallas_call(
        body, name="ffn_bwd_dgrad", grid=(nt,),
        in_specs=[tok, _full((1, D)), tok, sav, sav, _resident((nsh, F, D)), _resident((nsh, F, D)), _resident((nsh, F, D))],
        out_specs=(tok, _full((1, D)), sav, sav, tok, tok),
        out_shape=(_sds((T, D), f32), _sds((1, D), f32), _sds((nsh, T, F), bf16), _sds((nsh, T, F), bf16),
                   _sds((T, D), bf16), _sds((T, D), bf16)),
        compiler_params=_cparams(1),
    )(h, gain, dout, gs, us, wg, wu, wd)


def _ffn_bwd_wgrad(xn, dob, gs, us, dg, du, tm):
    T, D = xn.shape
    nsh, F = gs.shape[0], gs.shape[2]
    nt = T // tm

    def body(xn_ref, dob_ref, gs_ref, us_ref, dg_ref, du_ref, dwg_ref, dwu_ref, dwd_ref, ag_ref, au_ref, ad_ref):
        i = pl.program_id(1)

        @pl.when(i == 0)
        def _():
            ag_ref[...] = jnp.zeros_like(ag_ref)
            au_ref[...] = jnp.zeros_like(au_ref)
            ad_ref[...] = jnp.zeros_like(ad_ref)

        xn_v = xn_ref[...]
        ag_ref[...] += _dot_tn(dg_ref[...], xn_v)
        au_ref[...] += _dot_tn(du_ref[...], xn_v)
        g = gs_ref[...].astype(f32)
        a = (_silu(g) * us_ref[...].astype(f32)).astype(bf16)
        ad_ref[...] += _dot_tn(a, dob_ref[...])

        @pl.when(i == nt - 1)
        def _():
            dwg_ref[...] = ag_ref[...].astype(bf16)
            dwu_ref[...] = au_ref[...].astype(bf16)
            dwd_ref[...] = ad_ref[...].astype(bf16)

    tok = pl.BlockSpec((tm, D), lambda k, i: (i, 0))
    sav = pl.BlockSpec((None, tm, F), lambda k, i: (k, i, 0))
    wdspec = pl.BlockSpec((None, F, D), lambda k, i: (k, 0, 0))
    return pl.pallas_call(
        body, name="ffn_bwd_wgrad", grid=(nsh, nt),
        in_specs=[tok, tok, sav, sav, sav, sav],
        out_specs=(wdspec, wdspec, wdspec),
        out_shape=(_sds((nsh, F, D), bf16),) * 3,
        scratch_shapes=[pltpu.VMEM((F, D), f32)] * 3,
        compiler_params=_cparams(2),
    )(xn, dob, gs, us, dg, du)


def _proj_in_fwd(h, gain, w_in, tm):
    T, D = h.shape
    nsh, N = w_in.shape[0], w_in.shape[2]
    nt = T // tm

    def body(h_ref, gain_ref, w_ref, z_ref):
        hv = h_ref[...]
        r = lax.rsqrt(jnp.mean(hv * hv, axis=-1, keepdims=True) + EPS)
        xn = (hv * r * gain_ref[...]).astype(bf16)
        for k in range(nsh):
            z_ref[:, k * N:(k + 1) * N] = _dot(xn, w_ref[k])

    return pl.pallas_call(
        body, name="proj_in_fwd", grid=(nt,),
        in_specs=[pl.BlockSpec((tm, D), lambda i: (i, 0)), _full((1, D)), _full((nsh, D, N))],
        out_specs=pl.BlockSpec((tm, nsh * N), lambda i: (i, 0)),
        out_shape=_sds((T, nsh * N), f32),
        compiler_params=_cparams(1),
    )(h, gain, w_in)


def _proj_in_bwd_dgrad(h, gain, dres, dz, w_in, tm):
    T, D = h.shape
    nsh, N = w_in.shape[0], w_in.shape[2]
    nt = T // tm

    def body(h_ref, gain_ref, dres_ref, dz_ref, w_ref, dh_ref, dgain_ref, xn_ref):
        @pl.when(pl.program_id(0) == 0)
        def _():
            dgain_ref[...] = jnp.zeros_like(dgain_ref)

        dxn = _dot_nt(dz_ref[:, 0:N], w_ref[0])
        for k in range(1, nsh):
            dxn = dxn + _dot_nt(dz_ref[:, k * N:(k + 1) * N], w_ref[k])
        hv = h_ref[...]
        r = lax.rsqrt(jnp.mean(hv * hv, axis=-1, keepdims=True) + EPS)
        xh = hv * r
        xn_ref[...] = (xh * gain_ref[...]).astype(bf16)
        dgain_ref[...] += jnp.sum(dxn * xh, axis=0, keepdims=True)
        dxh = dxn * gain_ref[...]
        dh_ref[...] = dres_ref[...] + r * (dxh - xh * jnp.mean(dxh * xh, axis=-1, keepdims=True))

    tok = pl.BlockSpec((tm, D), lambda i: (i, 0))
    return pl.pallas_call(
        body, name="proj_in_bwd_dgrad", grid=(nt,),
        in_specs=[tok, _full((1, D)), tok, pl.BlockSpec((tm, nsh * N), lambda i: (i, 0)), _full((nsh, D, N))],
        out_specs=(tok, _full((1, D)), tok),
        out_shape=(_sds((T, D), f32), _sds((1, D), f32), _sds((T, D), bf16)),
        compiler_params=_cparams(1),
    )(h, gain, dres, dz, w_in)


def _proj_in_bwd_wgrad(xn, dz, nsh):
    T, D = xn.shape
    N = dz.shape[1] // nsh

    def body(xn_ref, dz_ref, dw_ref):
        dw_ref[...] = _dot_tn(xn_ref[...], dz_ref[...]).astype(bf16)

    return pl.pallas_call(
        body, name="proj_in_bwd_wgrad", grid=(nsh,),
        in_specs=[_full((T, D)), pl.BlockSpec((T, N), lambda k: (0, k))],
        out_specs=pl.BlockSpec((None, D, N), lambda k: (k, 0, 0)),
        out_shape=_sds((nsh, D, N), bf16),
        compiler_params=_cparams(1),
    )(xn, dz)


def _proj_out_fwd(h, oa, ob, oc, w_out, tm):
    T, D = h.shape
    nsh, R = w_out.shape[0], w_out.shape[1]
    da, db = oa.shape[1], ob.shape[1]
    nt = T // tm

    def body(h_ref, oa_ref, ob_ref, oc_ref, w_ref, out_ref):
        w = w_ref[...].reshape(nsh * R, D)
        out_ref[...] = (h_ref[...] + _dot(oa_ref[...], w[:da]) + _dot(ob_ref[...], w[da:da + db])
                        + _dot(oc_ref[...], w[da + db:]))

    def tok(n):
        return pl.BlockSpec((tm, n), lambda i: (i, 0))

    return pl.pallas_call(
        body, name="proj_out_fwd", grid=(nt,),
        in_specs=[tok(D), tok(da), tok(db), tok(oc.shape[1]), _full((nsh, R, D))],
        out_specs=tok(D), out_shape=_sds((T, D), f32),
        compiler_params=_cparams(1),
    )(h, oa, ob, oc, w_out)


def _proj_out_bwd(dh, oa, ob, oc, w_out, tm):
    T, D = dh.shape
    nsh, R = w_out.shape[0], w_out.shape[1]
    da, db, dc = oa.shape[1], ob.shape[1], oc.shape[1]
    nt = T // tm

    def body(dh_ref, oa_ref, ob_ref, oc_ref, w_ref, doa_ref, dob_ref, doc_ref, dw_ref, acc_ref):
        i = pl.program_id(0)

        @pl.when(i == 0)
        def _():
            acc_ref[...] = jnp.zeros_like(acc_ref)

        d = dh_ref[...].astype(bf16)
        w = w_ref[...].reshape(nsh * R, D)
        dm = _dot_nt(d, w)
        doa_ref[...] = dm[:, :da]
        dob_ref[...] = dm[:, da:da + db]
        doc_ref[...] = dm[:, da + db:]
        acc_ref[pl.ds(0, da), :] += _dot_tn(oa_ref[...], d)
        acc_ref[pl.ds(da, db), :] += _dot_tn(ob_ref[...], d)
        acc_ref[pl.ds(da + db, dc), :] += _dot_tn(oc_ref[...], d)

        @pl.when(i == nt - 1)
        def _():
            dw_ref[...] = acc_ref[...].astype(bf16).reshape(nsh, R, D)

    def tok(n):
        return pl.BlockSpec((tm, n), lambda i: (i, 0))

    wspec = _full((nsh, R, D))
    return pl.pallas_call(
        body, name="proj_out_bwd", grid=(nt,),
        in_specs=[tok(D), tok(da), tok(db), tok(dc), wspec],
        out_specs=(tok(da), tok(db), tok(dc), wspec),
        out_shape=(_sds((T, da), f32), _sds((T, db), f32), _sds((T, dc), f32), _sds((nsh, R, D), bf16)),
        scratch_shapes=[pltpu.VMEM((nsh * R, D), f32)],
        compiler_params=_cparams(1),
    )(dh, oa, ob, oc, w_out)


def _head_sum(m, n_heads):
    parts = []
    for hd in range(n_heads):
        s = jnp.sum(m[:, hd * HEAD:(hd + 1) * HEAD], axis=-1, keepdims=True)
        parts.append(jnp.broadcast_to(s, (m.shape[0], HEAD)))
    return parts[0] if n_heads == 1 else jnp.concatenate(parts, axis=1)


def _cat(parts, axis):
    return parts[0] if len(parts) == 1 else jnp.concatenate(parts, axis=axis)


def _three_parts(x):
    hi = x.astype(bf16)
    r1 = x - hi.astype(f32)
    mid = r1.astype(bf16)
    lo = (r1 - mid.astype(f32)).astype(bf16)
    return hi, mid, lo


@jax.custom_vjp
def _chunk_cumsum(tri, x):
    t16 = tri.astype(bf16)
    hi, mid, lo = _three_parts(x)
    return _dot(t16, hi) + _dot(t16, mid) + _dot(t16, lo)


def _chunk_cumsum_fwd(tri, x):
    return _chunk_cumsum(tri, x), tri


def _chunk_cumsum_bwd(tri, ct):
    t16 = tri.astype(bf16)
    hi, mid, lo = _three_parts(ct)
    return jnp.zeros_like(tri), _dot_tn(t16, hi) + _dot_tn(t16, mid) + _dot_tn(t16, lo)


_chunk_cumsum.defvjp(_chunk_cumsum_fwd, _chunk_cumsum_bwd)


def _hgrn_block(q, fl, iv, lb, states, tri, n_heads, n_inner):
    C = q.shape[0] // n_inner
    qs = _silu(q)
    forget = lb + (1.0 - lb) * jax.nn.sigmoid(fl)
    kk = 1.0 - forget
    logf = jnp.log(forget)
    b = _chunk_cumsum(tri, logf)
    vb = iv.astype(bf16)
    heads = [slice(hd * HEAD, (hd + 1) * HEAD) for hd in range(n_heads)]
    n_sub = C // A_SUB

    off, qe, kd, dec = {}, [], [], []
    for j in range(n_inner):
        c0 = j * C
        for blk in range(1, n_sub):
            lo = c0 + blk * A_SUB
            piv = b[lo:lo + 1]
            qt = (qs[lo:lo + A_SUB] * jnp.exp(b[lo:lo + A_SUB] - piv)).astype(bf16)
            kt = (kk[c0:lo] * jnp.exp(piv - b[c0:lo])).astype(bf16)
            parts = []
            for sl in heads:
                sc = _dot_nt(qt[:, sl], kt[:, sl])
                parts.append(_dot(sc.astype(bf16), vb[c0:lo, sl]))
            off[(j, blk)] = _cat(parts, 1)
        bj = b[c0:c0 + C]
        b_end = bj[C - 1:C]
        qe.append((qs[c0:c0 + C] * jnp.exp(bj)).astype(bf16))
        kd.append((kk[c0:c0 + C] * jnp.exp(b_end - bj)).astype(bf16))
        dec.append(jnp.exp(b_end))

    outs = []
    for j in range(n_inner):
        for blk in range(n_sub):
            lo = j * C + blk * A_SUB
            groups = [off[(j, blk)][r0:r0 + SUBLANES] if blk > 0 else None for r0 in range(0, A_SUB, SUBLANES)]
            for s in range(A_SUB):
                first = (s // SUBLANES) * SUBLANES
                n_rows = A_SUB - first
                row = lax.broadcasted_iota(jnp.int32, (n_rows, 1), 0) + first
                gate = jnp.where(row >= s, 0.0, -1e30)
                r = slice(lo + first, lo + A_SUB)
                m = qs[r] * jnp.exp((b[r] - b[lo + s:lo + s + 1]) + gate) * kk[lo + s:lo + s + 1]
                term = _head_sum(m, n_heads) * iv[lo + s:lo + s + 1]
                for gi in range(first // SUBLANES, A_SUB // SUBLANES):
                    piece = term[gi * SUBLANES - first:(gi + 1) * SUBLANES - first]
                    groups[gi] = piece if groups[gi] is None else groups[gi] + piece
            outs.extend(groups)
    o = jnp.concatenate(outs, axis=0)

    inter = []
    states = list(states)
    for j in range(n_inner):
        c0 = j * C
        parts = []
        for hd, sl in enumerate(heads):
            st = states[hd]
            parts.append(_dot_nt(qe[j][:, sl], st.astype(bf16)))
            states[hd] = dec[j][:, sl] * st + _dot_tn(vb[c0:c0 + C, sl], kd[j][:, sl])
        inter.append(_cat(parts, 1))
    return o + _cat(inter, 0), tuple(states)


def _hgrn_gate(o, g, gain, n_heads):
    ms = _head_sum(o * o, n_heads) * (1.0 / HEAD)
    return o * lax.rsqrt(ms + EPS) * gain * _silu(g)


def _tri_matrix(c, n_inner):
    idx = np.arange(c * n_inner)
    same = (idx[:, None] // c) == (idx[None, :] // c)
    return jnp.asarray((same & (idx[:, None] >= idx[None, :])).astype(np.float32))


def _hgrn_fwd(z, lb, gain, d_a):
    T = z.shape[0]
    C = A_CHUNK_FWD * A_INNER_FWD
    assert C == A_CHUNK * A_INNER
    nc = T // C
    nh = d_a // HEAD
    tri = _tri_matrix(A_CHUNK_FWD, A_INNER_FWD)

    def body(q_ref, f_ref, i_ref, g_ref, lb_ref, gain_ref, tri_ref, out_ref, o_ref, st_ref, carry_ref):
        @pl.when(pl.program_id(0) == 0)
        def _():
            carry_ref[...] = jnp.zeros_like(carry_ref)

        states = tuple(carry_ref[hd] for hd in range(nh))
        st_ref[...] = carry_ref[...]
        o, new_states = _hgrn_block(q_ref[...], f_ref[...], i_ref[...], lb_ref[...], states, tri_ref[...], nh, A_INNER_FWD)
        o_ref[...] = o
        out_ref[...] = _hgrn_gate(o, g_ref[...], gain_ref[...], nh).astype(bf16)
        for hd in range(nh):
            carry_ref[hd] = new_states[hd]

    def col(j):
        return pl.BlockSpec((C, d_a), lambda c, j=j: (c, j))

    tok = pl.BlockSpec((C, d_a), lambda c: (c, 0))
    return pl.pallas_call(
        body, name="hgrn_fwd", grid=(nc,),
        in_specs=[col(0), col(1), col(2), col(3), _full((1, d_a)), _full((1, d_a)), _full((C, C))],
        out_specs=(tok, tok, pl.BlockSpec((None, nh, HEAD, HEAD), lambda c: (c, 0, 0, 0))),
        out_shape=(_sds((T, d_a), bf16), _sds((T, d_a), f32), _sds((nc, nh, HEAD, HEAD), f32)),
        scratch_shapes=[pltpu.VMEM((nh, HEAD, HEAD), f32)],
        compiler_params=_cparams(1),
    )(z, z, z, z, lb, gain, tri)


def _hgrn_bwd(z, lb, gain, o_pre, states, dout, d_a):
    T = z.shape[0]
    C = A_CHUNK * A_INNER
    nc = T // C
    nh = d_a // HEAD
    tri = _tri_matrix(A_CHUNK, A_INNER)

    def body(q_ref, f_ref, i_ref, g_ref, lb_ref, gain_ref, tri_ref, o_ref, st_ref, do_ref,
             dz_ref, dlb_ref, dgain_ref, carry_ref):
        @pl.when(pl.program_id(0) == 0)
        def _():
            carry_ref[...] = jnp.zeros_like(carry_ref)
            dlb_ref[...] = jnp.zeros_like(dlb_ref)
            dgain_ref[...] = jnp.zeros_like(dgain_ref)

        _, vjp_gate = jax.vjp(lambda o, g, gv: _hgrn_gate(o, g, gv, nh), o_ref[...], g_ref[...], gain_ref[...])
        d_o, dg, dgain = vjp_gate(do_ref[...])
        tri_v = tri_ref[...]

        def fn(q, fl, iv, lbv, sts):
            return _hgrn_block(q, fl, iv, lbv, sts, tri_v, nh, A_INNER)

        states_in = tuple(st_ref[hd] for hd in range(nh))
        _, vjp = jax.vjp(fn, q_ref[...], f_ref[...], i_ref[...], lb_ref[...], states_in)
        dstates = tuple(carry_ref[hd] for hd in range(nh))
        dq, df, di, dlb, dst = vjp((d_o, dstates))
        dz_ref[:, 0:d_a] = dq.astype(bf16)
        dz_ref[:, d_a:2 * d_a] = df.astype(bf16)
        dz_ref[:, 2 * d_a:3 * d_a] = di.astype(bf16)
        dz_ref[:, 3 * d_a:4 * d_a] = dg.astype(bf16)
        dlb_ref[...] += dlb
        dgain_ref[...] += dgain
        for hd in range(nh):
            carry_ref[hd] = dst[hd]

    def col(j):
        return pl.BlockSpec((C, d_a), lambda c, j=j: (nc - 1 - c, j))

    tok = pl.BlockSpec((C, d_a), lambda c: (nc - 1 - c, 0))
    return pl.pallas_call(
        body, name="hgrn_bwd", grid=(nc,),
        in_specs=[col(0), col(1), col(2), col(3), _full((1, d_a)), _full((1, d_a)), _full((C, C)), tok,
                  pl.BlockSpec((None, nh, HEAD, HEAD), lambda c: (nc - 1 - c, 0, 0, 0)), tok],
        out_specs=(pl.BlockSpec((C, 4 * d_a), lambda c: (nc - 1 - c, 0)), _full((1, d_a)), _full((1, d_a))),
        out_shape=(_sds(z.shape, bf16), _sds((1, d_a), f32), _sds((1, d_a), f32)),
        scratch_shapes=[pltpu.VMEM((nh, HEAD, HEAD), f32)],
        compiler_params=_cparams(1),
    )(z, z, z, z, lb, gain, tri, o_pre, states, dout)


def _one_minus_exp(x):
    series = -x * (1.0 + x * (0.5 + x * (1.0 / 6.0 + x * (1.0 / 24.0))))
    return jnp.where(x > -0.03, series, 1.0 - jnp.exp(x))


def _lru_pre(xc, wa, ba, wx, bx, lam):
    xb16 = xc.astype(bf16)
    r = jax.nn.sigmoid(_dot(xb16, wa.astype(bf16)) + ba)
    gi = jax.nn.sigmoid(_dot(xb16, wx.astype(bf16)) + bx)
    log_a = -LRU_C * r * jax.nn.softplus(-lam)
    a = jnp.exp(log_a)
    mult = jnp.sqrt(_one_minus_exp(2.0 * log_a))
    return a, mult * gi * xc


def _lru_post(h, gate, gain, avg):
    y = h * jax.nn.gelu(gate)
    ms = _group_mean(y * y, avg)
    return y * lax.rsqrt(ms + EPS) * gain


def _shift_down(x, d, prev):
    row = lax.broadcasted_iota(jnp.int32, x.shape, 0)
    return jnp.where(row >= d, pltpu.roll(x, d, 0), pltpu.roll(prev, d, 0))


def _shift_up(x, d, nxt):
    n = x.shape[0]
    row = lax.broadcasted_iota(jnp.int32, x.shape, 0)
    return jnp.where(row < n - d, pltpu.roll(x, n - d, 0), pltpu.roll(nxt, n - d, 0))


def _scan_rows(a, u, reverse):
    n = a.shape[0]
    row = lax.broadcasted_iota(jnp.int32, a.shape, 0)
    d = 1
    while d < n:
        shift, ok = (n - d, row < n - d) if reverse else (d, row >= d)
        su = jnp.where(ok, pltpu.roll(u, shift, 0), 0.0)
        sa = jnp.where(ok, pltpu.roll(a, shift, 0), 1.0)
        u = u + a * su
        a = a * sa
        d *= 2
    return a, u


def _conv(xb, xprev, cw, cb):
    xc = cb + cw[CONV_WIDTH - 1:CONV_WIDTH] * xb
    for d in range(1, CONV_WIDTH):
        xc = xc + cw[CONV_WIDTH - 1 - d:CONV_WIDTH - d] * _shift_down(xb, d, xprev)
    return xc


def _lru_fwd(z, col0, d_b, cw, cb, wa, ba, wx, bx, lam, gain, avg):
    T = z.shape[0]
    R = min(B_CHUNK, T)
    nr = T // R
    jb = col0 // d_b

    def body(xb_ref, gate_ref, cw_ref, cb_ref, wa_ref, ba_ref, wx_ref, bx_ref, lam_ref, gain_ref, avg_ref,
             out_ref, h_ref, xprev_ref, hprev_ref):
        @pl.when(pl.program_id(0) == 0)
        def _():
            xprev_ref[...] = jnp.zeros_like(xprev_ref)
            hprev_ref[...] = jnp.zeros_like(hprev_ref)

        xb = xb_ref[...]
        xc = _conv(xb, xprev_ref[...], cw_ref[...], cb_ref[...])
        a, u = _lru_pre(xc, wa_ref[...], ba_ref[...], wx_ref[...], bx_ref[...], lam_ref[...])
        acum, hl = _scan_rows(a, u, False)
        h = hl + acum * hprev_ref[R - 1:R, :]
        h_ref[...] = h
        out_ref[...] = _lru_post(h, gate_ref[...], gain_ref[...], avg_ref[...]).astype(bf16)
        xprev_ref[...] = xb
        hprev_ref[...] = h

    vec = _full((1, d_b))
    return pl.pallas_call(
        body, name="lru_fwd", grid=(nr,),
        in_specs=[pl.BlockSpec((R, d_b), lambda i: (i, jb)), pl.BlockSpec((R, d_b), lambda i: (i, jb + 1)),
                  _full((CONV_WIDTH, d_b)), vec, _full((d_b, d_b)), vec, _full((d_b, d_b)), vec, vec, vec, _full((d_b, d_b))],
        out_specs=(pl.BlockSpec((R, d_b), lambda i: (i, 0)), pl.BlockSpec((R, d_b), lambda i: (i, 0))),
        out_shape=(_sds((T, d_b), bf16), _sds((T, d_b), f32)),
        scratch_shapes=[pltpu.VMEM((R, d_b), f32), pltpu.VMEM((R, d_b), f32)],
        compiler_params=_cparams(1),
    )(z, z, cw, cb, wa, ba, wx, bx, lam, gain, avg)


def _lru_bwd(z, col0, d_b, hsave, dout, dz_buf, cw, cb, wa, ba, wx, bx, lam, gain, avg):
    T = z.shape[0]
    R = min(B_CHUNK, T)
    nr = T // R
    jb = col0 // d_b

    def body(xb_ref, xp_ref, gate_ref, h_ref, hp_ref, do_ref,
             cw_ref, cb_ref, wa_ref, ba_ref, wx_ref, bx_ref, lam_ref, gain_ref, avg_ref, dzin_ref,
             dz_ref, dcw_ref, dcb_ref, dwa_ref, dba_ref, dwx_ref, dbx_ref, dlam_ref, dgain_ref,
             gfirst_ref, afirst_ref, dxcn_ref):
        step = pl.program_id(0)
        first_in_time = step == nr - 1

        @pl.when(step == 0)
        def _():
            for r in (dcw_ref, dcb_ref, dwa_ref, dba_ref, dwx_ref, dbx_ref, dlam_ref, dgain_ref,
                      gfirst_ref, afirst_ref, dxcn_ref):
                r[...] = jnp.zeros_like(r)

        xb = xb_ref[...]
        keep = jnp.where(first_in_time, 0.0, 1.0)
        xprev = xp_ref[...] * keep
        hprev = hp_ref[...] * keep
        cw = cw_ref[...]
        xc = _conv(xb, xprev, cw, cb_ref[...])
        (a, _), vjp_pre = jax.vjp(_lru_pre, xc, wa_ref[...], ba_ref[...], wx_ref[...], bx_ref[...], lam_ref[...])
        h = h_ref[...]
        avg = avg_ref[...]
        _, vjp_post = jax.vjp(lambda hh, gg, gn: _lru_post(hh, gg, gn, avg), h, gate_ref[...], gain_ref[...])
        dh, dgate, dgain = vjp_post(do_ref[...])
        a_next = _shift_up(a, 1, jnp.broadcast_to(afirst_ref[0:1, :], a.shape))
        acum, gl = _scan_rows(a_next, dh, True)
        gtot = gl + acum * gfirst_ref[0:1, :]
        da = gtot * _shift_down(h, 1, hprev)
        dxc, dwa, dba, dwx, dbx, dlam = vjp_pre((da, gtot))
        dxcn = dxcn_ref[...]
        dxb = cw[CONV_WIDTH - 1:CONV_WIDTH] * dxc
        dcw_ref[CONV_WIDTH - 1:CONV_WIDTH, :] += jnp.sum(dxc * xb, axis=0, keepdims=True)
        for d in range(1, CONV_WIDTH):
            tap = CONV_WIDTH - 1 - d
            dxb = dxb + cw[tap:tap + 1] * _shift_up(dxc, d, dxcn)
            dcw_ref[tap:tap + 1, :] += jnp.sum(dxc * _shift_down(xb, d, xprev), axis=0, keepdims=True)
        dz_ref[:, 0:d_b] = dxb.astype(bf16)
        dz_ref[:, d_b:2 * d_b] = dgate.astype(bf16)
        dcb_ref[...] += jnp.sum(dxc, axis=0, keepdims=True)
        dwa_ref[...] += dwa
        dba_ref[...] += dba
        dwx_ref[...] += dwx
        dbx_ref[...] += dbx
        dlam_ref[...] += dlam
        dgain_ref[...] += dgain
        gfirst_ref[...] = jnp.broadcast_to(gtot[0:1, :], gfirst_ref.shape)
        afirst_ref[...] = jnp.broadcast_to(a[0:1, :], afirst_ref.shape)
        dxcn_ref[...] = dxc

    vec = _full((1, d_b))
    mat = _full((d_b, d_b))

    def cur(j):
        return pl.BlockSpec((R, d_b), lambda i, j=j: (nr - 1 - i, j))

    def prev(j):
        return pl.BlockSpec((R, d_b), lambda i, j=j: (jnp.maximum(nr - 2 - i, 0), j))

    return pl.pallas_call(
        body, name="lru_bwd", grid=(nr,),
        in_specs=[cur(jb), prev(jb), cur(jb + 1), cur(0), prev(0), cur(0),
                  _full((CONV_WIDTH, d_b)), vec, mat, vec, mat, vec, vec, vec, mat, ANY_SPEC],
        out_specs=(pl.BlockSpec((R, 2 * d_b), lambda i: (nr - 1 - i, col0 // (2 * d_b))), _full((CONV_WIDTH, d_b)), vec, mat, vec, mat, vec, vec, vec),
        out_shape=(_sds(dz_buf.shape, bf16), _sds((CONV_WIDTH, d_b), f32), _sds((1, d_b), f32), _sds((d_b, d_b), f32),
                   _sds((1, d_b), f32), _sds((d_b, d_b), f32), _sds((1, d_b), f32), _sds((1, d_b), f32), _sds((1, d_b), f32)),
        scratch_shapes=[pltpu.VMEM((8, d_b), f32), pltpu.VMEM((8, d_b), f32), pltpu.VMEM((R, d_b), f32)],
        input_output_aliases={15: 0},
        compiler_params=_cparams(1),
    )(z, z, z, hsave, hsave, dout, cw, cb, wa, ba, wx, bx, lam, gain, avg, dz_buf)


def _two_pass(x, m16):
    hi = x.astype(bf16)
    lo = (x - hi.astype(f32)).astype(bf16)
    return _dot(hi, m16) + _dot(lo, m16)


@jax.custom_vjp
def _group_mean(x, avg):
    return _two_pass(x, avg.astype(bf16))


def _group_mean_fwd(x, avg):
    return _group_mean(x, avg), avg


def _group_mean_bwd(avg, ct):
    return _two_pass(ct, avg.astype(bf16)), jnp.zeros_like(avg)


_group_mean.defvjp(_group_mean_fwd, _group_mean_bwd)


def _sgu_chunk(u_in, v_in, w, bexp, gain, avg, n_groups):
    C, d_c = u_in.shape
    gd = d_c // n_groups
    u = jax.nn.gelu(u_in)
    v = jax.nn.gelu(v_in)
    mu = _group_mean(v, avg)
    vc = v - mu
    var = _group_mean(vc * vc, avg)
    vh = (vc * lax.rsqrt(var + EPS)).astype(bf16)
    lane = lax.broadcasted_iota(jnp.int32, (1, d_c), 1)
    causal = lax.broadcasted_iota(jnp.int32, (C, C), 0) >= lax.broadcasted_iota(jnp.int32, (C, C), 1)
    zz = bexp
    for g in range(n_groups):
        wg = jnp.where(causal, w[g], 0.0).astype(bf16)
        zz = zz + jnp.where((lane >= g * gd) & (lane < (g + 1) * gd), _dot(wg, vh), 0.0)
    y = u * zz
    ms = _group_mean(y * y, avg)
    return y * lax.rsqrt(ms + EPS) * gain


def _sgu_inner(T):
    return C_INNER if T % (C_CHUNK * C_INNER) == 0 else 1


def _sgu_fwd(z, col0, d_c, w, bexp, gain, avg):
    T = z.shape[0]
    C = C_CHUNK
    n_in = _sgu_inner(T)
    R = C * n_in
    jb = col0 // d_c
    G = w.shape[0]

    def body(u_ref, v_ref, w_ref, b_ref, gain_ref, avg_ref, out_ref):
        w_v, b_v, gain_v, avg = w_ref[...], b_ref[...], gain_ref[...], avg_ref[...]
        for j in range(n_in):
            rows = pl.ds(j * C, C)
            out_ref[rows, :] = _sgu_chunk(u_ref[rows, :], v_ref[rows, :], w_v, b_v, gain_v, avg, G).astype(bf16)

    return pl.pallas_call(
        body, name="sgu_fwd", grid=(T // R,),
        in_specs=[pl.BlockSpec((R, d_c), lambda i: (i, jb)), pl.BlockSpec((R, d_c), lambda i: (i, jb + 1)),
                  _full((G, C, C)), _full((C, d_c)), _full((1, d_c)), _full((d_c, d_c))],
        out_specs=pl.BlockSpec((R, d_c), lambda i: (i, 0)),
        out_shape=_sds((T, d_c), bf16),
        compiler_params=_cparams(1),
    )(z, z, w, bexp, gain, avg)


def _sgu_bwd(z, col0, d_c, dout, dz_buf, w, bexp, gain, avg):
    T = z.shape[0]
    C = C_CHUNK
    n_in = _sgu_inner(T)
    R = C * n_in
    nc = T // R
    jb = col0 // d_c
    G = w.shape[0]
    gd = d_c // G

    def body(u_ref, v_ref, do_ref, w_ref, b_ref, gain_ref, avg_ref, dzin_ref, dz_ref, dw_ref, db_ref, dgain_ref, dbexp_ref):
        step = pl.program_id(0)

        @pl.when(step == 0)
        def _():
            dw_ref[...] = jnp.zeros_like(dw_ref)
            dgain_ref[...] = jnp.zeros_like(dgain_ref)
            dbexp_ref[...] = jnp.zeros_like(dbexp_ref)

        avg, w_v, b_v, gain_v = avg_ref[...], w_ref[...], b_ref[...], gain_ref[...]
        dw = dbexp = dgain = None
        for j in range(n_in):
            rows = pl.ds(j * C, C)
            _, vjp = jax.vjp(lambda a, b, c, d, e: _sgu_chunk(a, b, c, d, e, avg, G),
                             u_ref[rows, :], v_ref[rows, :], w_v, b_v, gain_v)
            du, dv, dw_j, dbexp_j, dgain_j = vjp(do_ref[rows, :])
            dz_ref[rows, 0:d_c] = du.astype(bf16)
            dz_ref[rows, d_c:2 * d_c] = dv.astype(bf16)
            dw = dw_j if dw is None else dw + dw_j
            dbexp = dbexp_j if dbexp is None else dbexp + dbexp_j
            dgain = dgain_j if dgain is None else dgain + dgain_j
        dw_ref[...] += dw
        dbexp_ref[...] += dbexp
        dgain_ref[...] += dgain

        @pl.when(step == nc - 1)
        def _():
            lane = lax.broadcasted_iota(jnp.int32, (1, d_c), 1)
            acc = dbexp_ref[...]
            for g in range(G):
                sel = jnp.where((lane >= g * gd) & (lane < (g + 1) * gd), acc, 0.0)
                db_ref[:, g:g + 1] = jnp.sum(sel, axis=1, keepdims=True)

    return pl.pallas_call(
        body, name="sgu_bwd", grid=(nc,),
        in_specs=[pl.BlockSpec((R, d_c), lambda i: (i, jb)), pl.BlockSpec((R, d_c), lambda i: (i, jb + 1)),
                  pl.BlockSpec((R, d_c), lambda i: (i, 0)),
                  _full((G, C, C)), _full((C, d_c)), _full((1, d_c)), _full((d_c, d_c)), ANY_SPEC],
        out_specs=(pl.BlockSpec((R, 2 * d_c), lambda i: (i, col0 // (2 * d_c))), _full((G, C, C)), _full((C, G)), _full((1, d_c))),
        out_shape=(_sds(dz_buf.shape, bf16), _sds((G, C, C), f32), _sds((C, G), f32), _sds((1, d_c), f32)),
        scratch_shapes=[pltpu.VMEM((C, d_c), f32)],
        input_output_aliases={7: 0},
        compiler_params=_cparams(1),
    )(z, z, dout, w, bexp, gain, avg, dz_buf)


def _loss_head(h, gain, target, tm):
    T, D = h.shape
    nt = T // tm

    def body(h_ref, gain_ref, tgt_ref, dh_ref, loss_ref, dgain_ref):
        @pl.when(pl.program_id(0) == 0)
        def _():
            loss_ref[...] = jnp.zeros_like(loss_ref)
            dgain_ref[...] = jnp.zeros_like(dgain_ref)

        hv = h_ref[...]
        gain_v = gain_ref[...]
        r = lax.rsqrt(jnp.mean(hv * hv, axis=-1, keepdims=True) + EPS)
        xh = hv * r
        e = xh * gain_v - tgt_ref[...]
        loss_ref[...] += 0.5 * jnp.sum(jnp.mean(e * e, axis=-1, keepdims=True), axis=0, keepdims=True)
        dy = e * (1.0 / D)
        dgain_ref[...] += jnp.sum(dy * xh, axis=0, keepdims=True)
        dxh = dy * gain_v
        dh_ref[...] = r * (dxh - xh * jnp.mean(dxh * xh, axis=-1, keepdims=True))

    tok = pl.BlockSpec((tm, D), lambda i: (i, 0))
    return pl.pallas_call(
        body, name="loss_head", grid=(nt,),
        in_specs=[tok, _full((1, D)), tok],
        out_specs=(tok, _full((1, 128)), _full((1, D))),
        out_shape=(_sds((T, D), f32), _sds((1, 128), f32), _sds((1, D), f32)),
        compiler_params=_cparams(1),
    )(h, gain, target)


def _lower_bounds_fn(logits):
    n = logits.shape[0]
    mx = jnp.max(logits, axis=0, keepdims=True)
    ex = jnp.exp(logits - mx)
    soft = ex / jnp.sum(ex, axis=0, keepdims=True)
    rows = [jnp.zeros_like(soft[0:1])]
    for l in range(1, n):
        rows.append(rows[-1] + soft[l:l + 1])
    return jnp.concatenate(rows, axis=0)


def _lower_bounds(logits):
    def body(x_ref, o_ref):
        o_ref[...] = _lower_bounds_fn(x_ref[...])

    return pl.pallas_call(body, name="lower_bounds", out_shape=_sds(logits.shape, f32))(logits)


def _lower_bounds_bwd(logits, dlb):
    def body(x_ref, d_ref, o_ref):
        _, vjp = jax.vjp(_lower_bounds_fn, x_ref[...])
        o_ref[...] = vjp(d_ref[...])[0]

    return pl.pallas_call(body, name="lower_bounds_bwd", out_shape=_sds(logits.shape, f32))(logits, dlb)


def _adamw(w, g, m, v, rows_blk, row_range=None, prev=(), after=None):
    R, Cc = w.shape
    lo, hi = (0, R) if row_range is None else row_range
    span = hi - lo
    rb = span if (span <= rows_blk and lo % span == 0) else math.gcd(math.gcd(span, lo), rows_blk)
    extra = list(prev) + ([] if after is None else [after])

    def body(w_ref, g_ref, m_ref, v_ref, *rest):
        d_ref, nm_ref, nv_ref, go_ref = rest[len(extra):]
        gv = g_ref[...]
        m2 = ADAM_B1 * m_ref[...] + (1.0 - ADAM_B1) * gv
        v2 = ADAM_B2 * v_ref[...] + (1.0 - ADAM_B2) * (gv * gv)
        m_hat = m2 / (1.0 - ADAM_B1 ** ADAM_STEP)
        v_hat = v2 / (1.0 - ADAM_B2 ** ADAM_STEP)
        d_ref[...] = -ADAM_LR * (m_hat / (jnp.sqrt(v_hat) + ADAM_EPS) + ADAM_WD * w_ref[...])
        nm_ref[...] = m2
        nv_ref[...] = v2
        go_ref[...] = gv

    first = lo // rb
    spec = pl.BlockSpec((rb, Cc), lambda i: (i + first, 0))
    return pl.pallas_call(
        body, name="adamw", grid=((hi - lo) // rb,),
        in_specs=[spec] * 4 + [ANY_SPEC] * len(extra), out_specs=(spec,) * 4, out_shape=(_sds((R, Cc), f32),) * 4,
        input_output_aliases={4 + j: j for j in range(len(prev))},
        compiler_params=_cparams(1),
    )(w, g, m, v, *extra)


def _pair_sum(grads, recv, c_arr):
    n = len(grads)
    nsh = grads[0].shape[0]

    def body(c_ref, *refs):
        for a in range(n):
            refs[2 * n + a][...] = (refs[a][...].astype(f32) + refs[n + a][...].astype(f32)).astype(bf16)

    g_specs, r_specs, out_shape = [], [], []
    for g in grads:
        _, R, Cc = g.shape
        r2 = R // 2
        g_specs.append(pl.BlockSpec((None, r2, Cc), lambda s, c: (s, c[0], 0)))
        r_specs.append(pl.BlockSpec((None, r2, Cc), lambda s, c: (s, 0, 0)))
        out_shape.append(_sds((nsh, r2, Cc), bf16))
    gs = pltpu.PrefetchScalarGridSpec(num_scalar_prefetch=1, grid=(nsh,), in_specs=g_specs + r_specs, out_specs=tuple(r_specs))
    return list(pl.pallas_call(body, name="pair_sum", grid_spec=gs, out_shape=tuple(out_shape),
                               compiler_params=_cparams(1))(c_arr, *grads, *recv))


def _add(a, b):
    def body(a_ref, b_ref, o_ref):
        o_ref[...] = a_ref[...] + b_ref[...]

    return pl.pallas_call(body, name="pair_sum_small", out_shape=_sds(a.shape, f32))(a, b)


def _chip_sum(hsum, recv, bufs, slot_arr, c_arr, layer, n_layers):
    n = len(hsum)
    prev = list(bufs)
    steps = 2

    def body(s_ref, c_ref, *refs):
        outs = refs[len(refs) - n:]
        for a in range(n):
            acc = refs[a][...].astype(f32)
            for j in range(N_CHIPS - 1):
                acc = acc + refs[n + a][j].astype(f32)
            outs[a][...] = acc

    h_specs, r_specs, o_specs, out_shape = [], [], [], []
    for hh in hsum:
        _, r2, Cc = hh.shape
        rt = r2 // steps
        h_specs.append(pl.BlockSpec((None, rt, Cc), lambda i, s, c: (s[0], i, 0)))
        r_specs.append(pl.BlockSpec((N_CHIPS - 1, rt, Cc), lambda i, s, c: (0, i, 0)))
        o_specs.append(pl.BlockSpec((None, rt, Cc), lambda i, s, c: (layer, c[0] * steps + i, 0)))
        out_shape.append(_sds((n_layers, 2 * r2, Cc), f32))
    gs = pltpu.PrefetchScalarGridSpec(num_scalar_prefetch=2, grid=(steps,),
                                      in_specs=h_specs + r_specs + [ANY_SPEC] * len(prev), out_specs=tuple(o_specs))
    return list(pl.pallas_call(body, name="chip_sum", grid_spec=gs, out_shape=tuple(out_shape),
                               input_output_aliases={2 + 2 * n + a: a for a in range(len(prev))},
                               compiler_params=_cparams(1))(slot_arr, c_arr, *hsum, *recv, *prev))


def _sum_slots(x):
    def body(x_ref, o_ref):
        acc = x_ref[0]
        for j in range(1, x.shape[0]):
            acc = acc + x_ref[j]
        o_ref[...] = acc

    return pl.pallas_call(body, name="sum_slots", out_shape=_sds(x.shape[1:], f32))(x)


def _blockdiag(w):
    nb, bd, _ = w.shape
    eye = jnp.eye(nb, dtype=w.dtype)
    return (eye[:, None, :, None] * w[:, :, None, :]).reshape(nb * bd, nb * bd)


def _blockdiag_extract(dense, nb):
    bd = dense.shape[0] // nb
    d4 = dense.reshape(nb, bd, nb, bd)
    return jnp.stack([d4[i, :, i, :] for i in range(nb)])


def _pack(arrays, multiple):
    flat = jnp.concatenate([a.reshape(-1).astype(f32) for a in arrays])
    pad = (-flat.shape[0]) % multiple
    return jnp.pad(flat, (0, pad))


def _unpack(flat, shapes):
    out, off = [], 0
    for s in shapes:
        n = int(np.prod(s))
        out.append(flat[off:off + n].reshape(s))
        off += n
    return out


BIG = ("ffn1_wg", "ffn1_wu", "ffn1_wd", "w_in", "w_out", "ffn2_wg", "ffn2_wu", "ffn2_wd")
TRANSPOSED = ("ffn1_wg", "ffn1_wu", "ffn2_wg", "ffn2_wu")
SMALL = ("ffn1_norm", "mix_norm", "hgrn_lb_logits", "hgrn_norm", "conv_w", "conv_b", "lru_wa", "lru_ba", "lru_wx",
         "lru_bx", "lru_lambda", "lru_norm", "sgu_w", "sgu_b", "sgu_norm", "ffn2_norm", "final_norm")
WEIGHTS = ("ffn1_norm", "ffn1_wg", "ffn1_wu", "ffn1_wd", "mix_norm", "w_in", "hgrn_lb_logits", "hgrn_norm", "conv_w",
           "conv_b", "lru_wa", "lru_ba", "lru_wx", "lru_bx", "lru_lambda", "lru_norm", "sgu_w", "sgu_b", "sgu_norm",
           "w_out", "ffn2_norm", "ffn2_wg", "ffn2_wu", "ffn2_wd", "final_norm")


def kernel(x, ffn1_norm, ffn1_wg, ffn1_wu, ffn1_wd, mix_norm, w_in, hgrn_lb_logits, hgrn_norm, conv_w, conv_b, lru_wa, lru_ba, lru_wx, lru_bx, lru_lambda, lru_norm, sgu_w, sgu_b, sgu_norm, w_out, ffn2_norm, ffn2_wg, ffn2_wu, ffn2_wd, final_norm, loss_target, m_ffn1_norm, m_ffn1_wg, m_ffn1_wu, m_ffn1_wd, m_mix_norm, m_w_in, m_hgrn_lb_logits, m_hgrn_norm, m_conv_w, m_conv_b, m_lru_wa, m_lru_ba, m_lru_wx, m_lru_bx, m_lru_lambda, m_lru_norm, m_sgu_w, m_sgu_b, m_sgu_norm, m_w_out, m_ffn2_norm, m_ffn2_wg, m_ffn2_wu, m_ffn2_wd, m_final_norm, v_ffn1_norm, v_ffn1_wg, v_ffn1_wu, v_ffn1_wd, v_mix_norm, v_w_in, v_hgrn_lb_logits, v_hgrn_norm, v_conv_w, v_conv_b, v_lru_wa, v_lru_ba, v_lru_wx, v_lru_bx, v_lru_lambda, v_lru_norm, v_sgu_w, v_sgu_b, v_sgu_norm, v_w_out, v_ffn2_norm, v_ffn2_wg, v_ffn2_wu, v_ffn2_wd, v_final_norm):
    args = dict(locals())
    W = {n: args[n] for n in WEIGHTS}
    M = {n: args["m_" + n] for n in WEIGHTS}
    V = {n: args["v_" + n] for n in WEIGHTS}

    T, D = x.shape[1], x.shape[2]
    L = ffn1_norm.shape[0]
    d_a, d_b, d_c = hgrn_norm.shape[1], lru_norm.shape[1], sgu_norm.shape[1]
    col_b, col_c = 4 * d_a, 4 * d_a + 2 * d_b
    tm = 512 if T % 512 == 0 else T
    tm_w = 1024 if T % 1024 == 0 else tm
    tm_d = 256 if T % 256 == 0 else tm
    my_c = lax.axis_index("c")
    my_slot = 2 * lax.axis_index("x") + lax.axis_index("y")
    c_arr = jnp.reshape(my_c, (1,)).astype(jnp.int32)
    slot_arr = jnp.reshape(my_slot, (1,)).astype(jnp.int32)

    nb = len(BIG)
    gplan = _gather_ici_plan(nb)

    def kview(a, n):
        return jnp.swapaxes(a, 1, 2) if n in TRANSPOSED else a

    Wk = {n: kview(W[n], n) for n in BIG}
    place_steps = 4 if all(Wk[n].shape[1] % 64 == 0 for n in BIG) else 2

    conv_land = lax.dynamic_update_slice_in_dim(jnp.zeros((N_CHIPS,) + conv_w.shape, f32), conv_w[None], my_slot, axis=0)
    lands0 = _cast_place([Wk[n] for n in BIG], 0, slot_arr, place_steps)
    n_first = 3
    first_plan = _gather_ici_plan(n_first + 1)
    send, recv, _, first, token = _start_copies("gather_start_first", [], lands0[:n_first] + [conv_land], first_plan)
    later = {l: _cast_place([Wk[n] for n in BIG], l, slot_arr, place_steps, after=token) for l in range(1, L)}
    got = _wait_copies("gather_wait_first", send, recv, [], first, first_plan, later[L - 1][0] if later else lands0[-1])
    got = _gather_d2d("gather0_d2d", got)

    def placed(l):
        return later[l]
    G = [None] * L
    G[0] = dict(zip(BIG[:n_first], got[:n_first]))
    conv_full = jnp.transpose(got[n_first], (1, 2, 0, 3)).reshape(L, CONV_WIDTH, d_b)
    rest_plan = _gather_ici_plan(nb - n_first)
    rest_pending = _start_copies("gather_start_0", [], lands0[n_first:], rest_plan, got[0])

    def start_gather(l, after):
        return _start_copies(f"gather_start_{l}", [], placed(l), gplan, after)

    d2d_plan = _gather_d2d_plan(nb)

    lb = _lower_bounds(hgrn_lb_logits)
    avg_b = _group_avg_matrix(d_b, d_b // B_BLOCKS)
    avg_c = _group_avg_matrix(d_c, d_c // C_GROUPS)
    wa_dense = [_blockdiag(lru_wa[l]) for l in range(L)]
    wx_dense = [_blockdiag(lru_wx[l]) for l in range(L)]
    bexp = [jnp.repeat(sgu_b[l].T, d_c // C_GROUPS, axis=1) for l in range(L)]

    def lru_params(l):
        return (conv_full[l], conv_b[l][None], wa_dense[l], lru_ba[l].reshape(1, d_b), wx_dense[l],
                lru_bx[l].reshape(1, d_b), lru_lambda[l][None], lru_norm[l][None], avg_b)

    h = x.reshape(T, D)
    saved = []
    for l in range(L):
        s = {"h0": h}
        gain1, gain_mix = ffn1_norm[l][None], mix_norm[l][None]
        pending = None
        if l == 0:
            gain1 = gain1 + rest_pending[4][0:1, 0:1]
        elif l + 1 < L:
            pending = start_gather(l + 1, h)
            gain1 = gain1 + pending[4][0:1, 0:1]
        g = G[l]
        h, s["g1"], s["u1"] = _ffn_fwd(h, gain1, g["ffn1_wg"], g["ffn1_wu"], g["ffn1_wd"], tm)
        s["h1"] = h
        if l == 0:
            send, recv, _, lands, _ = rest_pending
            lands = _wait_copies("gather_wait_0", send, recv, [], lands, rest_plan, h)
            g.update(zip(BIG[n_first:], _gather_d2d("gather_d2d", lands)))
            if L > 1:
                pending = start_gather(1, g["w_in"])
                gain_mix = gain_mix + pending[4][0:1, 0:1]
        z = _proj_in_fwd(h, gain_mix, g["w_in"], tm)
        s["z"] = z
        s["oa"], s["o_pre"], s["states"] = _hgrn_fwd(z, lb[l][None], hgrn_norm[l][None], d_a)
        s["ob"], s["hl"] = _lru_fwd(z, col_b, d_b, *lru_params(l))
        s["oc"] = _sgu_fwd(z, col_c, d_c, sgu_w[l], bexp[l], sgu_norm[l][None], avg_c)
        h = _proj_out_fwd(h, s["oa"], s["ob"], s["oc"], g["w_out"], tm)
        s["h2"] = h
        gain2 = ffn2_norm[l][None]
        forward = None
        if pending is not None:
            send, recv, _, lands, _ = pending
            lands = _wait_copies(f"gather_wait_{l + 1}", send, recv, [], lands, gplan, h)
            forward = _start_copies(f"gather_d2d_start_{l + 1}", [], lands, d2d_plan)
            gain2 = gain2 + forward[4][0:1, 0:1]
        h, s["g2"], s["u2"] = _ffn_fwd(h, gain2, g["ffn2_wg"], g["ffn2_wu"], g["ffn2_wd"], tm)
        saved.append(s)
        if forward is not None:
            send, recv, _, lands, _ = forward
            G[l + 1] = dict(zip(BIG, _wait_copies(f"gather_d2d_wait_{l + 1}", send, recv, [], lands, d2d_plan, h)))

    dh, loss_part, d_final = _loss_head(h, final_norm[None], loss_target.reshape(T, D), tm)
    loss = lax.psum(loss_part[0, 0], ("x", "y", "c"))

    def pair_views(n_big):
        r = [(lambda i, o, p, a=a: i[a].at[:, pl.ds((1 - p.c) * (i[a].shape[1] // 2), i[a].shape[1] // 2)],
              lambda i, o, p, a=a: o[a], "sib") for a in range(n_big)]
        return r

    def chip_plan_for(n):
        return [(lambda s_, o, p, a=a, kind=kind: s_[a].at[p.peer_slot(kind)], lambda s_, o, p, a=a, j=j: o[a].at[j], kind)
                for a in range(n) for j, kind in enumerate(CHIP_KINDS)]

    chip_plan = chip_plan_for(nb)
    sbufs = {n: None for n in BIG}

    def pair_phase(arrs, extra=None):
        n = len(arrs)
        ins, remote = list(arrs), pair_views(n)
        outs = [_sds((N_CHIPS, a.shape[1] // 2, a.shape[2]), bf16) for a in arrs]
        if extra is not None:
            ins.append(extra)
            outs.append(_sds(extra.shape, f32))
            remote = remote + [(lambda i, o, p: i[n], lambda i, o, p: o[n], "sib")]
        recv = _exchange("grad_pair_d2d", ins, outs, remote)
        return _pair_sum(arrs, recv[:n], c_arr), (None if extra is None else _add(extra, recv[n]))

    def chip_sum_into(names, hs, lands, l):
        prev = [sbufs[n] for n in names] if sbufs[names[0]] is not None else []
        for n, buf in zip(names, _chip_sum(hs, lands, prev, slot_arr, c_arr, l, L)):
            sbufs[n] = buf

    def share_plan(l):
        view = lambda a: (lambda s_, o, p: _half(o[a].at[l], p.c))
        return [(view(a), view(a), "sib") for a in range(nb)]

    def share_start(l):
        send, recv, _, bufs, token = _start_copies(f"grad_share_start_{l}", [], [sbufs[n] for n in BIG], share_plan(l))
        for n, buf in zip(BIG, bufs):
            sbufs[n] = buf
        return (l, send, recv), token

    def share_wait(pending, after):
        l, send, recv = pending
        bufs = _wait_copies(f"grad_share_wait_{l}", send, recv, [], [sbufs[n] for n in BIG], share_plan(l), after)
        for n, buf in zip(BIG, bufs):
            sbufs[n] = buf

    def share(l, extra_in=(), extra_out=(), extra_remote=(), extra_local=()):
        remote = [(lambda i, o, p, a=a: _half(o[a].at[l], p.c), lambda i, o, p, a=a: _half(o[a].at[l], p.c), "sib")
                  for a in range(nb)]
        outs = [_sds(sbufs[n].shape, f32) for n in BIG] + list(extra_out)
        res = _exchange("grad_share_d2d", [sbufs[n] for n in BIG] + list(extra_in), outs, remote + list(extra_remote),
                        list(extra_local), aliases={a: a for a in range(nb)})
        for n, buf in zip(BIG, res[:nb]):
            sbufs[n] = buf
        return res[nb:]

    small = {n: [None] * L for n in SMALL if n != "final_norm"}
    chip_pending = pair_pending = early = share_token = None
    shares = []
    early_names = ("w_in", "w_out", "ffn2_wg", "ffn2_wu", "ffn2_wd")
    for l in reversed(range(L)):
        s, g = saved[l], G[l]
        gain2, gain_a = ffn2_norm[l][None], hgrn_norm[l][None]
        if pair_pending is not None:
            gain2 = gain2 + pair_pending[0][4][0:1, 0:1]
        if share_token is not None:
            gain2 = gain2 + share_token[0:1, 0:1]
            share_token = None
        dh, small["ffn2_norm"][l], dg, du, xn, dob = _ffn_bwd_dgrad(
            s["h2"], gain2, dh, s["g2"], s["u2"], g["ffn2_wg"], g["ffn2_wu"], g["ffn2_wd"], tm_d)
        if pair_pending is not None:
            (send, recv, grads_prev, lands, _), = pair_pending
            recv_a = _wait_copies(f"grad_pair_wait_{l + 1}", send, recv, grads_prev, lands, pair_views(nb), dh)
            hsum = _pair_sum(grads_prev, recv_a, c_arr)
            lands = [lax.empty((N_CHIPS - 1,) + hh.shape[1:], bf16) for hh in hsum]
            chip_pending = (_start_copies(f"grad_chip_start_{l + 1}", hsum, lands, chip_plan), hsum)
            gain_a = gain_a + chip_pending[0][4][0:1, 0:1]
            pair_pending = None
        dwg2, dwu2, dwd2 = _ffn_bwd_wgrad(xn, dob, s["g2"], s["u2"], dg, du, tm_w)
        doa, dob_, doc, dwo = _proj_out_bwd(dh, s["oa"], s["ob"], s["oc"], g["w_out"], tm)
        dz, small["hgrn_lb_logits"][l], small["hgrn_norm"][l] = _hgrn_bwd(
            s["z"], lb[l][None], gain_a, s["o_pre"], s["states"], doa, d_a)
        (dz, small["conv_w"][l], small["conv_b"][l], dwa, small["lru_ba"][l], dwx, small["lru_bx"][l],
         small["lru_lambda"][l], small["lru_norm"][l]) = _lru_bwd(s["z"], col_b, d_b, s["hl"], dob_, dz, *lru_params(l))
        small["lru_wa"][l] = _blockdiag_extract(dwa, B_BLOCKS)
        small["lru_wx"][l] = _blockdiag_extract(dwx, B_BLOCKS)
        dz, small["sgu_w"][l], dsb, small["sgu_norm"][l] = _sgu_bwd(
            s["z"], col_c, d_c, doc, dz, sgu_w[l], bexp[l], sgu_norm[l][None], avg_c)
        small["sgu_b"][l] = dsb.T
        dh, small["mix_norm"][l], xn = _proj_in_bwd_dgrad(s["h1"], mix_norm[l][None], dh, dz, g["w_in"], tm)
        dwi = _proj_in_bwd_wgrad(xn, dz, N_CHIPS)
        gain1 = ffn1_norm[l][None]
        if l == 0:
            hs_e, _ = pair_phase([dwi, dwo, dwg2, dwu2, dwd2])
            lands = [lax.empty((N_CHIPS - 1,) + hh.shape[1:], bf16) for hh in hs_e]
            early = (_start_copies("grad_chip_start_0", hs_e, lands, chip_plan_for(len(hs_e))), hs_e)
            gain1 = gain1 + early[0][4][0:1, 0:1]
        dh, small["ffn1_norm"][l], dg, du, xn, dob = _ffn_bwd_dgrad(
            s["h0"], gain1, dh, s["g1"], s["u1"], g["ffn1_wg"], g["ffn1_wu"], g["ffn1_wd"], tm_d)
        dwg1, dwu1, dwd1 = _ffn_bwd_wgrad(xn, dob, s["g1"], s["u1"], dg, du, tm_w)
        layer_grads = [dwg1, dwu1, dwd1, dwi, dwo, dwg2, dwu2, dwd2]

        if chip_pending is not None:
            (send, recv, hs, lands, _), hsum_prev = chip_pending
            lands = _wait_copies(f"grad_chip_wait_{l + 1}", send, recv, hs, lands, chip_plan, dwg1)
            chip_sum_into(BIG, hsum_prev, lands, l + 1)
            pending_share, share_token = share_start(l + 1)
            shares.append(pending_share)
            chip_pending = None
        if l > 0:
            lands = [lax.empty((N_CHIPS, gr.shape[1] // 2, gr.shape[2]), bf16) for gr in layer_grads]
            pair_pending = (_start_copies(f"grad_pair_start_{l}", layer_grads, lands, pair_views(nb)),)
    grad_x = dh.reshape(x.shape)

    (send, recv, hs, lands, _), hs_e = early
    lands = _wait_copies("grad_chip_wait_0", send, recv, hs, lands, chip_plan_for(len(hs_e)), dwg1)
    chip_sum_into(early_names, hs_e, lands, 0)
    small_names = [n for n in SMALL]
    small_parts = [jnp.stack([jnp.reshape(v, (-1,)) for v in small[n]]) if n != "final_norm" else d_final for n in small_names]
    small_shapes = [p.shape for p in small_parts]
    packed = _pack(small_parts, 2 * 8 * 128).reshape(2, -1, 128)
    n_rows = packed.shape[1]
    late = [dwg1, dwu1, dwd1]
    nl = len(late)
    hsum, small_pair = pair_phase(late, packed)
    own_half = lax.dynamic_index_in_dim(small_pair, my_c, 0, keepdims=True)
    small_land = lax.dynamic_update_slice_in_dim(jnp.zeros((N_CHIPS, n_rows, 128), f32), own_half, my_slot, axis=0)
    late_plan = chip_plan_for(nl) + [(lambda s_, o, p: s_[nl].at[p.c], lambda s_, o, p: o[nl].at[p.slot], kind)
                                     for kind in CHIP_KINDS]
    lands = [lax.empty((N_CHIPS - 1,) + hh.shape[1:], bf16) for hh in hsum] + [small_land]
    send, recv, srcs, lands, token = _start_copies("grad_chip_start_last", hsum + [small_pair], lands, late_plan)

    def adam_operands(n):
        shape = Wk[n].shape
        rows_blk = 512 if shape[1] % 512 == 0 else (shape[1] // 2 if shape[1] > 512 else shape[1])
        return [a.reshape(-1, shape[-1]) for a in (Wk[n], sbufs[n], kview(M[n], n), kview(V[n], n))], rows_blk, shape

    for pending_share in shares:
        share_wait(pending_share, token)
    partial = {}
    if L > 1:
        for n in BIG:
            flat, rows_blk, shape = adam_operands(n)
            partial[n] = _adamw(*flat, rows_blk, row_range=(shape[1], L * shape[1]), after=token)
    recv_b = _wait_copies("grad_chip_wait_last", send, recv, srcs, lands, late_plan,
                          partial[BIG[-1]][0] if partial else hsum[0])
    chip_sum_into(BIG[:nl], hsum, recv_b[:nl], 0)
    small_half = _sum_slots(recv_b[nl])
    (small_all,) = share(0, extra_in=[small_half], extra_out=[_sds(packed.shape, f32)],
                         extra_remote=[(lambda i, o, p: i[nb], lambda i, o, p: o[nb].at[p.c], "sib")],
                         extra_local=[(lambda i, o, p: i[nb], lambda i, o, p: o[nb].at[p.c])])
    grads = {n: kview(sbufs[n], n) for n in BIG}
    small_tot = _unpack(small_all.reshape(-1), small_shapes)
    for n, val in zip(small_names, small_tot):
        grads[n] = val
    grads["hgrn_lb_logits"] = _lower_bounds_bwd(hgrn_lb_logits, grads["hgrn_lb_logits"])
    shard_cols = conv_w.shape[2]
    grads["conv_w"] = lax.dynamic_slice_in_dim(grads["conv_w"].reshape(L, CONV_WIDTH, d_b), my_slot * shard_cols, shard_cols, axis=2)
    for n in SMALL:
        grads[n] = grads[n].reshape(W[n].shape)

    delta, new_m, new_v = {}, {}, {}
    for n in BIG:
        flat, rows_blk, shape = adam_operands(n)
        outs = _adamw(*flat, rows_blk, row_range=(0, shape[1]), prev=partial[n]) if partial else _adamw(*flat, rows_blk)
        delta[n], new_m[n], new_v[n], grads[n] = [kview(o.reshape(shape), n) for o in outs]
    shapes = [W[n].shape for n in SMALL]
    packs = [_pack([src[n] for n in SMALL], 8 * 128).reshape(-1, 128) for src in (W, grads, M, V)]
    d2, m2, v2, _ = _adamw(*packs, 4096)
    for dst, val in ((delta, d2), (new_m, m2), (new_v, v2)):
        for n, piece in zip(SMALL, _unpack(val.reshape(-1), shapes)):
            dst[n] = piece

    return (loss, grad_x, *[grads[n] for n in WEIGHTS], *[delta[n] for n in WEIGHTS],
            *[new_m[n] for n in WEIGHTS], *[new_v[n] for n in WEIGHTS])
```

```python
import math

import numpy as np
import jax
import jax.numpy as jnp
from jax import lax
from jax.experimental import pallas as pl
from jax.experimental.pallas import tpu as pltpu

f32 = jnp.float32
bf16 = jnp.bfloat16
HI = lax.Precision.HIGHEST
MESH = pl.DeviceIdType.MESH

EPS = 1e-6
HEAD = 128
A_CHUNK = 64
A_SUB = 16
A_INNER = 2
A_CHUNK_FWD = 32
A_INNER_FWD = 4
SUBLANES = 8
B_BLOCKS = 4
B_CHUNK = 512
CONV_WIDTH = 4
LRU_C = 8.0
C_GROUPS = 4
C_CHUNK = 128
C_INNER = 8
N_CHIPS = 4
ADAM_LR, ADAM_B1, ADAM_B2, ADAM_EPS, ADAM_WD, ADAM_STEP = 0.001, 0.9, 0.999, 1e-08, 0.01, 10
VMEM_LIMIT = 56 * 1024 * 1024


def _cparams(n_axes):
    return pltpu.CompilerParams(dimension_semantics=("arbitrary",) * n_axes, vmem_limit_bytes=VMEM_LIMIT)


def _sds(shape, dtype):
    return jax.ShapeDtypeStruct(tuple(shape), dtype)


def _full(shape):
    n = len(shape)
    return pl.BlockSpec(tuple(shape), lambda *_: (0,) * n)


def _resident(shape):
    n = len(shape)
    return pl.BlockSpec(tuple(shape), lambda *_: (0,) * n, pipeline_mode=pl.Buffered(1))


def _dot(a, b):
    return jnp.dot(a, b, preferred_element_type=f32)


def _dot_nt(a, b):
    return lax.dot_general(a, b, (((1,), (1,)), ((), ())), preferred_element_type=f32)


def _dot_tn(a, b):
    return lax.dot_general(a, b, (((0,), (0,)), ((), ())), preferred_element_type=f32)


def _silu(x):
    return x * jax.nn.sigmoid(x)


def _group_avg_matrix(n, group):
    idx = np.arange(n) // group
    return jnp.asarray((idx[:, None] == idx[None, :]).astype(np.float32) / group)


class _Place:
    def __init__(self):
        self.x, self.y, self.c = lax.axis_index("x"), lax.axis_index("y"), lax.axis_index("c")
        self.slot = 2 * self.x + self.y

    def peer(self, kind):
        x, y, c = self.x, self.y, self.c
        return {"sib": (x, y, 1 - c), "fx": (1 - x, y, c), "fy": (x, 1 - y, c), "fxy": (1 - x, 1 - y, c)}[kind]

    def peer_slot(self, kind):
        x, y = self.x, self.y
        return {"fx": 2 * (1 - x) + y, "fy": 2 * x + (1 - y), "fxy": 2 * (1 - x) + (1 - y)}[kind]


CHIP_KINDS = ("fx", "fy", "fxy")


def _exchange(name, ins, outs, remote, local=(), aliases=None):
    n_in, n_out, n_r, n_l = len(ins), len(outs), len(remote), len(local)

    def body(*refs):
        in_refs, out_refs = refs[:n_in], refs[n_in:n_in + n_out]
        send, recv, lsem = refs[n_in + n_out:]
        p = _Place()
        lcopies = []
        for t, (src, dst) in enumerate(local):
            cp = pltpu.make_async_copy(src(in_refs, out_refs, p), dst(in_refs, out_refs, p), lsem.at[t])
            cp.start()
            lcopies.append(cp)
        copies = []
        for t, (src, dst, kind) in enumerate(remote):
            cp = pltpu.make_async_remote_copy(
                src_ref=src(in_refs, out_refs, p), dst_ref=dst(in_refs, out_refs, p),
                send_sem=send.at[t], recv_sem=recv.at[t], device_id=p.peer(kind), device_id_type=MESH)
            cp.start()
            copies.append(cp)
        for cp in copies:
            cp.wait_recv()
        for cp in copies:
            cp.wait_send()
        for cp in lcopies:
            cp.wait()

    anyspec = pl.BlockSpec(memory_space=pl.ANY)
    res = pl.pallas_call(
        body, name=name, out_shape=tuple(outs),
        in_specs=[anyspec] * n_in, out_specs=tuple([anyspec] * n_out),
        scratch_shapes=[pltpu.SemaphoreType.DMA((n_r,)), pltpu.SemaphoreType.DMA((n_r,)),
                        pltpu.SemaphoreType.DMA((max(n_l, 1),))],
        input_output_aliases=aliases or {},
        compiler_params=pltpu.CompilerParams(has_side_effects=True),
    )(*ins)
    return list(res)


HBM_SPEC = pl.BlockSpec(memory_space=pltpu.HBM)
SEM_SPEC = pl.BlockSpec(memory_space=pltpu.SEMAPHORE)
ANY_SPEC = pl.BlockSpec(memory_space=pl.ANY)
DATAFLOW = pltpu.SideEffectType.DATAFLOW_SIDE_EFFECTING


def _in_hbm(a):
    return pltpu.with_memory_space_constraint(a, pltpu.HBM)


def _start_copies(name, srcs, lands, remote, after=None):
    n_s, n_l, n_r = len(srcs), len(lands), len(remote)
    extra = [] if after is None else [after]

    def body(*refs):
        src_refs, land_refs = refs[:n_s], refs[n_s:n_s + n_l]
        n_in = n_s + n_l + len(extra)
        send, recv = refs[n_in], refs[n_in + 1]
        token = refs[-1]
        p = _Place()
        for t, (src, dst, kind) in enumerate(remote):
            pltpu.make_async_remote_copy(
                src_ref=src(src_refs, land_refs, p), dst_ref=dst(src_refs, land_refs, p),
                send_sem=send.at[t], recv_sem=recv.at[t], device_id=p.peer(kind), device_id_type=MESH).start()
        token[...] = jnp.zeros_like(token)

    thru = [pltpu.HBM(a.shape, a.dtype) for a in lands]
    res = pl.pallas_call(
        body, name=name,
        out_shape=(pltpu.SemaphoreType.DMA((n_r,)), pltpu.SemaphoreType.DMA((n_r,)), *thru, _sds((8, 128), f32)),
        in_specs=[ANY_SPEC] * n_s + [HBM_SPEC] * n_l + [ANY_SPEC] * len(extra),
        out_specs=(SEM_SPEC, SEM_SPEC, *([HBM_SPEC] * n_l), pl.BlockSpec(memory_space=pltpu.VMEM)),
        input_output_aliases={n_s + i: 2 + i for i in range(n_l)},
        compiler_params=pltpu.CompilerParams(has_side_effects=DATAFLOW),
    )(*srcs, *[_in_hbm(a) for a in lands], *extra)
    return res[0], res[1], list(srcs), list(res[2:2 + n_l]), res[-1]


def _wait_copies(name, send, recv, srcs, lands, remote, after):
    n_s, n_l = len(srcs), len(lands)

    def body(*refs):
        src_refs, land_refs = refs[:n_s], refs[n_s:n_s + n_l]
        send_ref, recv_ref = refs[n_s + n_l], refs[n_s + n_l + 1]
        p = _Place()
        for t, (src, dst, kind) in enumerate(remote):
            cp = pltpu.make_async_remote_copy(
                src_ref=src(src_refs, land_refs, p), dst_ref=dst(src_refs, land_refs, p),
                send_sem=send_ref.at[t], recv_sem=recv_ref.at[t], device_id=p.peer(kind), device_id_type=MESH)
            cp.wait_send()
            cp.wait_recv()

    res = pl.pallas_call(
        body, name=name, out_shape=tuple(pltpu.HBM(a.shape, a.dtype) for a in lands),
        in_specs=[ANY_SPEC] * n_s + [HBM_SPEC] * n_l + [SEM_SPEC, SEM_SPEC, ANY_SPEC],
        out_specs=tuple([HBM_SPEC] * n_l),
        input_output_aliases={n_s + i: i for i in range(n_l)},
        compiler_params=pltpu.CompilerParams(has_side_effects=DATAFLOW),
    )(*srcs, *lands, send, recv, after)
    return list(res)


def _half(ref, c):
    n2 = ref.shape[0] // 2
    return ref.at[pl.ds(c * n2, n2)]


def _gather_ici_plan(n):
    def view(a):
        return lambda s, o, p: _half(o[a].at[p.slot], p.c)

    return [(view(a), view(a), kind) for a in range(n) for kind in CHIP_KINDS]


def _gather_d2d_plan(n):
    remote = []
    for a in range(n):
        for kind in CHIP_KINDS:
            view = lambda i, o, p, a=a, kind=kind: _half(o[a].at[p.peer_slot(kind)], p.c)
            remote.append((view, view, "sib"))
    return remote


def _gather_d2d(name, lands):
    n = len(lands)
    outs = [_sds(g.shape, g.dtype) for g in lands]
    return _exchange(name, list(lands), outs, _gather_d2d_plan(n), aliases={a: a for a in range(n)})


def _cast_place(weights, layer, slot_arr, n_steps=4, after=None):
    n = len(weights)
    extra = [] if after is None else [after]

    def body(s_ref, *refs):
        outs = refs[n + len(extra):]
        for a in range(n):
            outs[a][...] = refs[a][...].astype(bf16)

    in_specs, out_specs, out_shape = [], [], []
    for w in weights:
        _, R, Cc = w.shape
        rt = R // n_steps
        in_specs.append(pl.BlockSpec((None, rt, Cc), lambda i, s: (layer, i, 0)))
        out_specs.append(pl.BlockSpec((None, rt, Cc), lambda i, s: (s[0], i, 0)))
        out_shape.append(_sds((N_CHIPS, R, Cc), bf16))
    gs = pltpu.PrefetchScalarGridSpec(num_scalar_prefetch=1, grid=(n_steps,), in_specs=in_specs + [ANY_SPEC] * len(extra),
                                      out_specs=tuple(out_specs))
    return list(pl.pallas_call(body, name="cast_place", grid_spec=gs, out_shape=tuple(out_shape),
                               compiler_params=_cparams(1))(slot_arr, *weights, *extra))


def _ffn_fwd(h, gain, wg, wu, wd, tm):
    T, D = h.shape
    nsh, F = wg.shape[0], wg.shape[1]
    nt = T // tm

    def body(h_ref, gain_ref, wg_ref, wu_ref, wd_ref, out_ref, gs_ref, us_ref):
        hv = h_ref[...]
        r = lax.rsqrt(jnp.mean(hv * hv, axis=-1, keepdims=True) + EPS)
        xn = (hv * r * gain_ref[...]).astype(bf16)
        acc = None
        for k in range(nsh):
            g = _dot_nt(xn, wg_ref[k])
            u = _dot_nt(xn, wu_ref[k])
            gs_ref[k] = g.astype(bf16)
            us_ref[k] = u.astype(bf16)
            part = _dot((_silu(g) * u).astype(bf16), wd_ref[k])
            acc = part if acc is None else acc + part
        out_ref[...] = hv + 0.5 * acc

    sav = pl.BlockSpec((nsh, tm, F), lambda i: (0, i, 0))
    return pl.pallas_call(
        body, name="ffn_fwd", grid=(nt,),
        in_specs=[pl.BlockSpec((tm, D), lambda i: (i, 0)), _full((1, D)), _resident((nsh, F, D)), _resident((nsh, F, D)),
                  _resident((nsh, F, D))],
        out_specs=(pl.BlockSpec((tm, D), lambda i: (i, 0)), sav, sav),
        out_shape=(_sds((T, D), f32), _sds((nsh, T, F), bf16), _sds((nsh, T, F), bf16)),
        compiler_params=_cparams(1),
    )(h, gain, wg, wu, wd)


def _ffn_bwd_dgrad(h, gain, dout, gs, us, wg, wu, wd, tm):
    T, D = h.shape
    nsh, F = wg.shape[0], wg.shape[1]
    nt = T // tm

    def body(h_ref, gain_ref, dout_ref, gs_ref, us_ref, wg_ref, wu_ref, wd_ref,
             dh_ref, dgain_ref, dg_ref, du_ref, xn_ref, dob_ref):
        @pl.when(pl.program_id(0) == 0)
        def _():
            dgain_ref[...] = jnp.zeros_like(dgain_ref)

        hv = h_ref[...]
        r = lax.rsqrt(jnp.mean(hv * hv, axis=-1, keepdims=True) + EPS)
        xh = hv * r
        xn_ref[...] = (xh * gain_ref[...]).astype(bf16)
        dv = dout_ref[...]
        dob = (0.5 * dv).astype(bf16)
        dob_ref[...] = dob
        dxn = None
        for k in range(nsh):
            da = _dot_nt(dob, wd_ref[k])
            g = gs_ref[k].astype(f32)
            u = us_ref[k].astype(f32)
            sg = jax.nn.sigmoid(g)
            dg = (da * u * (sg * (1.0 + g * (1.0 - sg)))).astype(bf16)
            du = (da * (g * sg)).astype(bf16)
            dg_ref[k] = dg
            du_ref[k] = du
            part = _dot(dg, wg_ref[k]) + _dot(du, wu_ref[k])
            dxn = part if dxn is None else dxn + part
        dgain_ref[...] += jnp.sum(dxn * xh, axis=0, keepdims=True)
        dxh = dxn * gain_ref[...]
        dh_ref[...] = dv + r * (dxh - xh * jnp.mean(dxh * xh, axis=-1, keepdims=True))

    tok = pl.BlockSpec((tm, D), lambda i: (i, 0))
    sav = pl.BlockSpec((nsh, tm, F), lambda i: (0, i, 0))
    return pl.pallas_call(
        body, name="ffn_bwd_dgrad", grid=(nt,),
        in_specs=[tok, _full((1, D)), tok, sav, sav, _resident((nsh, F, D)), _resident((nsh, F, D)), _resident((nsh, F, D))],
        out_specs=(tok, _full((1, D)), sav, sav, tok, tok),
        out_shape=(_sds((T, D), f32), _sds((1, D), f32), _sds((nsh, T, F), bf16), _sds((nsh, T, F), bf16),
                   _sds((T, D), bf16), _sds((T, D), bf16)),
        compiler_params=_cparams(1),
    )(h, gain, dout, gs, us, wg, wu, wd)


def _ffn_bwd_wgrad(xn, dob, gs, us, dg, du, tm):
    T, D = xn.shape
    nsh, F = gs.shape[0], gs.shape[2]
    nt = T // tm

    def body(xn_ref, dob_ref, gs_ref, us_ref, dg_ref, du_ref, dwg_ref, dwu_ref, dwd_ref, ag_ref, au_ref, ad_ref):
        i = pl.program_id(1)

        @pl.when(i == 0)
        def _():
            ag_ref[...] = jnp.zeros_like(ag_ref)
            au_ref[...] = jnp.zeros_like(au_ref)
            ad_ref[...] = jnp.zeros_like(ad_ref)

        xn_v = xn_ref[...]
        ag_ref[...] += _dot_tn(dg_ref[...], xn_v)
        au_ref[...] += _dot_tn(du_ref[...], xn_v)
        g = gs_ref[...].astype(f32)
        a = (_silu(g) * us_ref[...].astype(f32)).astype(bf16)
        ad_ref[...] += _dot_tn(a, dob_ref[...])

        @pl.when(i == nt - 1)
        def _():
            dwg_ref[...] = ag_ref[...].astype(bf16)
            dwu_ref[...] = au_ref[...].astype(bf16)
            dwd_ref[...] = ad_ref[...].astype(bf16)

    tok = pl.BlockSpec((tm, D), lambda k, i: (i, 0))
    sav = pl.BlockSpec((None, tm, F), lambda k, i: (k, i, 0))
    wdspec = pl.BlockSpec((None, F, D), lambda k, i: (k, 0, 0))
    return pl.pallas_call(
        body, name="ffn_bwd_wgrad", grid=(nsh, nt),
        in_specs=[tok, tok, sav, sav, sav, sav],
        out_specs=(wdspec, wdspec, wdspec),
        out_shape=(_sds((nsh, F, D), bf16),) * 3,
        scratch_shapes=[pltpu.VMEM((F, D), f32)] * 3,
        compiler_params=_cparams(2),
    )(xn, dob, gs, us, dg, du)


def _proj_in_fwd(h, gain, w_in, tm):
    T, D = h.shape
    nsh, N = w_in.shape[0], w_in.shape[2]
    nt = T // tm

    def body(h_ref, gain_ref, w_ref, z_ref):
        hv = h_ref[...]
        r = lax.rsqrt(jnp.mean(hv * hv, axis=-1, keepdims=True) + EPS)
        xn = (hv * r * gain_ref[...]).astype(bf16)
        for k in range(nsh):
            z_ref[:, k * N:(k + 1) * N] = _dot(xn, w_ref[k])

    return pl.pallas_call(
        body, name="proj_in_fwd", grid=(nt,),
        in_specs=[pl.BlockSpec((tm, D), lambda i: (i, 0)), _full((1, D)), _full((nsh, D, N))],
        out_specs=pl.BlockSpec((tm, nsh * N), lambda i: (i, 0)),
        out_shape=_sds((T, nsh * N), f32),
        compiler_params=_cparams(1),
    )(h, gain, w_in)


def _proj_in_bwd_dgrad(h, gain, dres, dz, w_in, tm):
    T, D = h.shape
    nsh, N = w_in.shape[0], w_in.shape[2]
    nt = T // tm

    def body(h_ref, gain_ref, dres_ref, dz_ref, w_ref, dh_ref, dgain_ref, xn_ref):
        @pl.when(pl.program_id(0) == 0)
        def _():
            dgain_ref[...] = jnp.zeros_like(dgain_ref)

        dxn = _dot_nt(dz_ref[:, 0:N], w_ref[0])
        for k in range(1, nsh):
            dxn = dxn + _dot_nt(dz_ref[:, k * N:(k + 1) * N], w_ref[k])
        hv = h_ref[...]
        r = lax.rsqrt(jnp.mean(hv * hv, axis=-1, keepdims=True) + EPS)
        xh = hv * r
        xn_ref[...] = (xh * gain_ref[...]).astype(bf16)
        dgain_ref[...] += jnp.sum(dxn * xh, axis=0, keepdims=True)
        dxh = dxn * gain_ref[...]
        dh_ref[...] = dres_ref[...] + r * (dxh - xh * jnp.mean(dxh * xh, axis=-1, keepdims=True))

    tok = pl.BlockSpec((tm, D), lambda i: (i, 0))
    return pl.pallas_call(
        body, name="proj_in_bwd_dgrad", grid=(nt,),
        in_specs=[tok, _full((1, D)), tok, pl.BlockSpec((tm, nsh * N), lambda i: (i, 0)), _full((nsh, D, N))],
        out_specs=(tok, _full((1, D)), tok),
        out_shape=(_sds((T, D), f32), _sds((1, D), f32), _sds((T, D), bf16)),
        compiler_params=_cparams(1),
    )(h, gain, dres, dz, w_in)


def _proj_in_bwd_wgrad(xn, dz, nsh):
    T, D = xn.shape
    N = dz.shape[1] // nsh

    def body(xn_ref, dz_ref, dw_ref):
        dw_ref[...] = _dot_tn(xn_ref[...], dz_ref[...]).astype(bf16)

    return pl.pallas_call(
        body, name="proj_in_bwd_wgrad", grid=(nsh,),
        in_specs=[_full((T, D)), pl.BlockSpec((T, N), lambda k: (0, k))],
        out_specs=pl.BlockSpec((None, D, N), lambda k: (k, 0, 0)),
        out_shape=_sds((nsh, D, N), bf16),
        compiler_params=_cparams(1),
    )(xn, dz)


def _proj_out_fwd(h, oa, ob, oc, w_out, tm):
    T, D = h.shape
    nsh, R = w_out.shape[0], w_out.shape[1]
    da, db = oa.shape[1], ob.shape[1]
    nt = T // tm

    def body(h_ref, oa_ref, ob_ref, oc_ref, w_ref, out_ref):
        w = w_ref[...].reshape(nsh * R, D)
        out_ref[...] = (h_ref[...] + _dot(oa_ref[...], w[:da]) + _dot(ob_ref[...], w[da:da + db])
                        + _dot(oc_ref[...], w[da + db:]))

    def tok(n):
        return pl.BlockSpec((tm, n), lambda i: (i, 0))

    return pl.pallas_call(
        body, name="proj_out_fwd", grid=(nt,),
        in_specs=[tok(D), tok(da), tok(db), tok(oc.shape[1]), _full((nsh, R, D))],
        out_specs=tok(D), out_shape=_sds((T, D), f32),
        compiler_params=_cparams(1),
    )(h, oa, ob, oc, w_out)


def _proj_out_bwd(dh, oa, ob, oc, w_out, tm):
    T, D = dh.shape
    nsh, R = w_out.shape[0], w_out.shape[1]
    da, db, dc = oa.shape[1], ob.shape[1], oc.shape[1]
    nt = T // tm

    def body(dh_ref, oa_ref, ob_ref, oc_ref, w_ref, doa_ref, dob_ref, doc_ref, dw_ref, acc_ref):
        i = pl.program_id(0)

        @pl.when(i == 0)
        def _():
            acc_ref[...] = jnp.zeros_like(acc_ref)

        d = dh_ref[...].astype(bf16)
        w = w_ref[...].reshape(nsh * R, D)
        dm = _dot_nt(d, w)
        doa_ref[...] = dm[:, :da]
        dob_ref[...] = dm[:, da:da + db]
        doc_ref[...] = dm[:, da + db:]
        acc_ref[pl.ds(0, da), :] += _dot_tn(oa_ref[...], d)
        acc_ref[pl.ds(da, db), :] += _dot_tn(ob_ref[...], d)
        acc_ref[pl.ds(da + db, dc), :] += _dot_tn(oc_ref[...], d)

        @pl.when(i == nt - 1)
        def _():
            dw_ref[...] = acc_ref[...].astype(bf16).reshape(nsh, R, D)

    def tok(n):
        return pl.BlockSpec((tm, n), lambda i: (i, 0))

    wspec = _full((nsh, R, D))
    return pl.pallas_call(
        body, name="proj_out_bwd", grid=(nt,),
        in_specs=[tok(D), tok(da), tok(db), tok(dc), wspec],
        out_specs=(tok(da), tok(db), tok(dc), wspec),
        out_shape=(_sds((T, da), f32), _sds((T, db), f32), _sds((T, dc), f32), _sds((nsh, R, D), bf16)),
        scratch_shapes=[pltpu.VMEM((nsh * R, D), f32)],
        compiler_params=_cparams(1),
    )(dh, oa, ob, oc, w_out)


def _head_sum(m, n_heads):
    parts = []
    for hd in range(n_heads):
        s = jnp.sum(m[:, hd * HEAD:(hd + 1) * HEAD], axis=-1, keepdims=True)
        parts.append(jnp.broadcast_to(s, (m.shape[0], HEAD)))
    return parts[0] if n_heads == 1 else jnp.concatenate(parts, axis=1)


def _cat(parts, axis):
    return parts[0] if len(parts) == 1 else jnp.concatenate(parts, axis=axis)


def _three_parts(x):
    hi = x.astype(bf16)
    r1 = x - hi.astype(f32)
    mid = r1.astype(bf16)
    lo = (r1 - mid.astype(f32)).astype(bf16)
    return hi, mid, lo


@jax.custom_vjp
def _chunk_cumsum(tri, x):
    t16 = tri.astype(bf16)
    hi, mid, lo = _three_parts(x)
    return _dot(t16, hi) + _dot(t16, mid) + _dot(t16, lo)


def _chunk_cumsum_fwd(tri, x):
    return _chunk_cumsum(tri, x), tri


def _chunk_cumsum_bwd(tri, ct):
    t16 = tri.astype(bf16)
    hi, mid, lo = _three_parts(ct)
    return jnp.zeros_like(tri), _dot_tn(t16, hi) + _dot_tn(t16, mid) + _dot_tn(t16, lo)


_chunk_cumsum.defvjp(_chunk_cumsum_fwd, _chunk_cumsum_bwd)


def _hgrn_block(q, fl, iv, lb, states, tri, n_heads, n_inner):
    C = q.shape[0] // n_inner
    qs = _silu(q)
    forget = lb + (1.0 - lb) * jax.nn.sigmoid(fl)
    kk = 1.0 - forget
    logf = jnp.log(forget)
    b = _chunk_cumsum(tri, logf)
    vb = iv.astype(bf16)
    heads = [slice(hd * HEAD, (hd + 1) * HEAD) for hd in range(n_heads)]
    n_sub = C // A_SUB

    off, qe, kd, dec = {}, [], [], []
    for j in range(n_inner):
        c0 = j * C
        for blk in range(1, n_sub):
            lo = c0 + blk * A_SUB
            piv = b[lo:lo + 1]
            qt = (qs[lo:lo + A_SUB] * jnp.exp(b[lo:lo + A_SUB] - piv)).astype(bf16)
            kt = (kk[c0:lo] * jnp.exp(piv - b[c0:lo])).astype(bf16)
            parts = []
            for sl in heads:
                sc = _dot_nt(qt[:, sl], kt[:, sl])
                parts.append(_dot(sc.astype(bf16), vb[c0:lo, sl]))
            off[(j, blk)] = _cat(parts, 1)
        bj = b[c0:c0 + C]
        b_end = bj[C - 1:C]
        qe.append((qs[c0:c0 + C] * jnp.exp(bj)).astype(bf16))
        kd.append((kk[c0:c0 + C] * jnp.exp(b_end - bj)).astype(bf16))
        dec.append(jnp.exp(b_end))

    outs = []
    for j in range(n_inner):
        for blk in range(n_sub):
            lo = j * C + blk * A_SUB
            groups = [off[(j, blk)][r0:r0 + SUBLANES] if blk > 0 else None for r0 in range(0, A_SUB, SUBLANES)]
            for s in range(A_SUB):
                first = (s // SUBLANES) * SUBLANES
                n_rows = A_SUB - first
                row = lax.broadcasted_iota(jnp.int32, (n_rows, 1), 0) + first
                gate = jnp.where(row >= s, 0.0, -1e30)
                r = slice(lo + first, lo + A_SUB)
                m = qs[r] * jnp.exp((b[r] - b[lo + s:lo + s + 1]) + gate) * kk[lo + s:lo + s + 1]
                term = _head_sum(m, n_heads) * iv[lo + s:lo + s + 1]
                for gi in range(first // SUBLANES, A_SUB // SUBLANES):
                    piece = term[gi * SUBLANES - first:(gi + 1) * SUBLANES - first]
                    groups[gi] = piece if groups[gi] is None else groups[gi] + piece
            outs.extend(groups)
    o = jnp.concatenate(outs, axis=0)

    inter = []
    states = list(states)
    for j in range(n_inner):
        c0 = j * C
        parts = []
        for hd, sl in enumerate(heads):
            st = states[hd]
            parts.append(_dot_nt(qe[j][:, sl], st.astype(bf16)))
            states[hd] = dec[j][:, sl] * st + _dot_tn(vb[c0:c0 + C, sl], kd[j][:, sl])
        inter.append(_cat(parts, 1))
    return o + _cat(inter, 0), tuple(states)


def _hgrn_gate(o, g, gain, n_heads):
    ms = _head_sum(o * o, n_heads) * (1.0 / HEAD)
    return o * lax.rsqrt(ms + EPS) * gain * _silu(g)


def _tri_matrix(c, n_inner):
    idx = np.arange(c * n_inner)
    same = (idx[:, None] // c) == (idx[None, :] // c)
    return jnp.asarray((same & (idx[:, None] >= idx[None, :])).astype(np.float32))


def _hgrn_fwd(z, lb, gain, d_a):
    T = z.shape[0]
    C = A_CHUNK_FWD * A_INNER_FWD
    assert C == A_CHUNK * A_INNER
    nc = T // C
    nh = d_a // HEAD
    tri = _tri_matrix(A_CHUNK_FWD, A_INNER_FWD)

    def body(q_ref, f_ref, i_ref, g_ref, lb_ref, gain_ref, tri_ref, out_ref, o_ref, st_ref, carry_ref):
        @pl.when(pl.program_id(0) == 0)
        def _():
            carry_ref[...] = jnp.zeros_like(carry_ref)

        states = tuple(carry_ref[hd] for hd in range(nh))
        st_ref[...] = carry_ref[...]
        o, new_states = _hgrn_block(q_ref[...], f_ref[...], i_ref[...], lb_ref[...], states, tri_ref[...], nh, A_INNER_FWD)
        o_ref[...] = o
        out_ref[...] = _hgrn_gate(o, g_ref[...], gain_ref[...], nh).astype(bf16)
        for hd in range(nh):
            carry_ref[hd] = new_states[hd]

    def col(j):
        return pl.BlockSpec((C, d_a), lambda c, j=j: (c, j))

    tok = pl.BlockSpec((C, d_a), lambda c: (c, 0))
    return pl.pallas_call(
        body, name="hgrn_fwd", grid=(nc,),
        in_specs=[col(0), col(1), col(2), col(3), _full((1, d_a)), _full((1, d_a)), _full((C, C))],
        out_specs=(tok, tok, pl.BlockSpec((None, nh, HEAD, HEAD), lambda c: (c, 0, 0, 0))),
        out_shape=(_sds((T, d_a), bf16), _sds((T, d_a), f32), _sds((nc, nh, HEAD, HEAD), f32)),
        scratch_shapes=[pltpu.VMEM((nh, HEAD, HEAD), f32)],
        compiler_params=_cparams(1),
    )(z, z, z, z, lb, gain, tri)


def _hgrn_bwd(z, lb, gain, o_pre, states, dout, d_a):
    T = z.shape[0]
    C = A_CHUNK * A_INNER
    nc = T // C
    nh = d_a // HEAD
    tri = _tri_matrix(A_CHUNK, A_INNER)

    def body(q_ref, f_ref, i_ref, g_ref, lb_ref, gain_ref, tri_ref, o_ref, st_ref, do_ref,
             dz_ref, dlb_ref, dgain_ref, carry_ref):
        @pl.when(pl.program_id(0) == 0)
        def _():
            carry_ref[...] = jnp.zeros_like(carry_ref)
            dlb_ref[...] = jnp.zeros_like(dlb_ref)
            dgain_ref[...] = jnp.zeros_like(dgain_ref)

        _, vjp_gate = jax.vjp(lambda o, g, gv: _hgrn_gate(o, g, gv, nh), o_ref[...], g_ref[...], gain_ref[...])
        d_o, dg, dgain = vjp_gate(do_ref[...])
        tri_v = tri_ref[...]

        def fn(q, fl, iv, lbv, sts):
            return _hgrn_block(q, fl, iv, lbv, sts, tri_v, nh, A_INNER)

        states_in = tuple(st_ref[hd] for hd in range(nh))
        _, vjp = jax.vjp(fn, q_ref[...], f_ref[...], i_ref[...], lb_ref[...], states_in)
        dstates = tuple(carry_ref[hd] for hd in range(nh))
        dq, df, di, dlb, dst = vjp((d_o, dstates))
        dz_ref[:, 0:d_a] = dq.astype(bf16)
        dz_ref[:, d_a:2 * d_a] = df.astype(bf16)
        dz_ref[:, 2 * d_a:3 * d_a] = di.astype(bf16)
        dz_ref[:, 3 * d_a:4 * d_a] = dg.astype(bf16)
        dlb_ref[...] += dlb
        dgain_ref[...] += dgain
        for hd in range(nh):
            carry_ref[hd] = dst[hd]

    def col(j):
        return pl.BlockSpec((C, d_a), lambda c, j=j: (nc - 1 - c, j))

    tok = pl.BlockSpec((C, d_a), lambda c: (nc - 1 - c, 0))
    return pl.pallas_call(
        body, name="hgrn_bwd", grid=(nc,),
        in_specs=[col(0), col(1), col(2), col(3), _full((1, d_a)), _full((1, d_a)), _full((C, C)), tok,
                  pl.BlockSpec((None, nh, HEAD, HEAD), lambda c: (nc - 1 - c, 0, 0, 0)), tok],
        out_specs=(pl.BlockSpec((C, 4 * d_a), lambda c: (nc - 1 - c, 0)), _full((1, d_a)), _full((1, d_a))),
        out_shape=(_sds(z.shape, bf16), _sds((1, d_a), f32), _sds((1, d_a), f32)),
        scratch_shapes=[pltpu.VMEM((nh, HEAD, HEAD), f32)],
        compiler_params=_cparams(1),
    )(z, z, z, z, lb, gain, tri, o_pre, states, dout)


def _one_minus_exp(x):
    series = -x * (1.0 + x * (0.5 + x * (1.0 / 6.0 + x * (1.0 / 24.0))))
    return jnp.where(x > -0.03, series, 1.0 - jnp.exp(x))


def _lru_pre(xc, wa, ba, wx, bx, lam):
    xb16 = xc.astype(bf16)
    r = jax.nn.sigmoid(_dot(xb16, wa.astype(bf16)) + ba)
    gi = jax.nn.sigmoid(_dot(xb16, wx.astype(bf16)) + bx)
    log_a = -LRU_C * r * jax.nn.softplus(-lam)
    a = jnp.exp(log_a)
    mult = jnp.sqrt(_one_minus_exp(2.0 * log_a))
    return a, mult * gi * xc


def _lru_post(h, gate, gain, avg):
    y = h * jax.nn.gelu(gate)
    ms = _group_mean(y * y, avg)
    return y * lax.rsqrt(ms + EPS) * gain


def _shift_down(x, d, prev):
    row = lax.broadcasted_iota(jnp.int32, x.shape, 0)
    return jnp.where(row >= d, pltpu.roll(x, d, 0), pltpu.roll(prev, d, 0))


def _shift_up(x, d, nxt):
    n = x.shape[0]
    row = lax.broadcasted_iota(jnp.int32, x.shape, 0)
    return jnp.where(row < n - d, pltpu.roll(x, n - d, 0), pltpu.roll(nxt, n - d, 0))


def _scan_rows(a, u, reverse):
    n = a.shape[0]
    row = lax.broadcasted_iota(jnp.int32, a.shape, 0)
    d = 1
    while d < n:
        shift, ok = (n - d, row < n - d) if reverse else (d, row >= d)
        su = jnp.where(ok, pltpu.roll(u, shift, 0), 0.0)
        sa = jnp.where(ok, pltpu.roll(a, shift, 0), 1.0)
        u = u + a * su
        a = a * sa
        d *= 2
    return a, u


def _conv(xb, xprev, cw, cb):
    xc = cb + cw[CONV_WIDTH - 1:CONV_WIDTH] * xb
    for d in range(1, CONV_WIDTH):
        xc = xc + cw[CONV_WIDTH - 1 - d:CONV_WIDTH - d] * _shift_down(xb, d, xprev)
    return xc


def _lru_fwd(z, col0, d_b, cw, cb, wa, ba, wx, bx, lam, gain, avg):
    T = z.shape[0]
    R = min(B_CHUNK, T)
    nr = T // R
    jb = col0 // d_b

    def body(xb_ref, gate_ref, cw_ref, cb_ref, wa_ref, ba_ref, wx_ref, bx_ref, lam_ref, gain_ref, avg_ref,
             out_ref, h_ref, xprev_ref, hprev_ref):
        @pl.when(pl.program_id(0) == 0)
        def _():
            xprev_ref[...] = jnp.zeros_like(xprev_ref)
            hprev_ref[...] = jnp.zeros_like(hprev_ref)

        xb = xb_ref[...]
        xc = _conv(xb, xprev_ref[...], cw_ref[...], cb_ref[...])
        a, u = _lru_pre(xc, wa_ref[...], ba_ref[...], wx_ref[...], bx_ref[...], lam_ref[...])
        acum, hl = _scan_rows(a, u, False)
        h = hl + acum * hprev_ref[R - 1:R, :]
        h_ref[...] = h
        out_ref[...] = _lru_post(h, gate_ref[...], gain_ref[...], avg_ref[...]).astype(bf16)
        xprev_ref[...] = xb
        hprev_ref[...] = h

    vec = _full((1, d_b))
    return pl.pallas_call(
        body, name="lru_fwd", grid=(nr,),
        in_specs=[pl.BlockSpec((R, d_b), lambda i: (i, jb)), pl.BlockSpec((R, d_b), lambda i: (i, jb + 1)),
                  _full((CONV_WIDTH, d_b)), vec, _full((d_b, d_b)), vec, _full((d_b, d_b)), vec, vec, vec, _full((d_b, d_b))],
        out_specs=(pl.BlockSpec((R, d_b), lambda i: (i, 0)), pl.BlockSpec((R, d_b), lambda i: (i, 0))),
        out_shape=(_sds((T, d_b), bf16), _sds((T, d_b), f32)),
        scratch_shapes=[pltpu.VMEM((R, d_b), f32), pltpu.VMEM((R, d_b), f32)],
        compiler_params=_cparams(1),
    )(z, z, cw, cb, wa, ba, wx, bx, lam, gain, avg)


def _lru_bwd(z, col0, d_b, hsave, dout, dz_buf, cw, cb, wa, ba, wx, bx, lam, gain, avg):
    T = z.shape[0]
    R = min(B_CHUNK, T)
    nr = T // R
    jb = col0 // d_b

    def body(xb_ref, xp_ref, gate_ref, h_ref, hp_ref, do_ref,
             cw_ref, cb_ref, wa_ref, ba_ref, wx_ref, bx_ref, lam_ref, gain_ref, avg_ref, dzin_ref,
             dz_ref, dcw_ref, dcb_ref, dwa_ref, dba_ref, dwx_ref, dbx_ref, dlam_ref, dgain_ref,
             gfirst_ref, afirst_ref, dxcn_ref):
        step = pl.program_id(0)
        first_in_time = step == nr - 1

        @pl.when(step == 0)
        def _():
            for r in (dcw_ref, dcb_ref, dwa_ref, dba_ref, dwx_ref, dbx_ref, dlam_ref, dgain_ref,
                      gfirst_ref, afirst_ref, dxcn_ref):
                r[...] = jnp.zeros_like(r)

        xb = xb_ref[...]
        keep = jnp.where(first_in_time, 0.0, 1.0)
        xprev = xp_ref[...] * keep
        hprev = hp_ref[...] * keep
        cw = cw_ref[...]
        xc = _conv(xb, xprev, cw, cb_ref[...])
        (a, _), vjp_pre = jax.vjp(_lru_pre, xc, wa_ref[...], ba_ref[...], wx_ref[...], bx_ref[...], lam_ref[...])
        h = h_ref[...]
        avg = avg_ref[...]
        _, vjp_post = jax.vjp(lambda hh, gg, gn: _lru_post(hh, gg, gn, avg), h, gate_ref[...], gain_ref[...])
        dh, dgate, dgain = vjp_post(do_ref[...])
        a_next = _shift_up(a, 1, jnp.broadcast_to(afirst_ref[0:1, :], a.shape))
        acum, gl = _scan_rows(a_next, dh, True)
        gtot = gl + acum * gfirst_ref[0:1, :]
        da = gtot * _shift_down(h, 1, hprev)
        dxc, dwa, dba, dwx, dbx, dlam = vjp_pre((da, gtot))
        dxcn = dxcn_ref[...]
        dxb = cw[CONV_WIDTH - 1:CONV_WIDTH] * dxc
        dcw_ref[CONV_WIDTH - 1:CONV_WIDTH, :] += jnp.sum(dxc * xb, axis=0, keepdims=True)
        for d in range(1, CONV_WIDTH):
            tap = CONV_WIDTH - 1 - d
            dxb = dxb + cw[tap:tap + 1] * _shift_up(dxc, d, dxcn)
            dcw_ref[tap:tap + 1, :] += jnp.sum(dxc * _shift_down(xb, d, xprev), axis=0, keepdims=True)
        dz_ref[:, 0:d_b] = dxb.astype(bf16)
        dz_ref[:, d_b:2 * d_b] = dgate.astype(bf16)
        dcb_ref[...] += jnp.sum(dxc, axis=0, keepdims=True)
        dwa_ref[...] += dwa
        dba_ref[...] += dba
        dwx_ref[...] += dwx
        dbx_ref[...] += dbx
        dlam_ref[...] += dlam
        dgain_ref[...] += dgain
        gfirst_ref[...] = jnp.broadcast_to(gtot[0:1, :], gfirst_ref.shape)
        afirst_ref[...] = jnp.broadcast_to(a[0:1, :], afirst_ref.shape)
        dxcn_ref[...] = dxc

    vec = _full((1, d_b))
    mat = _full((d_b, d_b))

    def cur(j):
        return pl.BlockSpec((R, d_b), lambda i, j=j: (nr - 1 - i, j))

    def prev(j):
        return pl.BlockSpec((R, d_b), lambda i, j=j: (jnp.maximum(nr - 2 - i, 0), j))

    return pl.pallas_call(
        body, name="lru_bwd", grid=(nr,),
        in_specs=[cur(jb), prev(jb), cur(jb + 1), cur(0), prev(0), cur(0),
                  _full((CONV_WIDTH, d_b)), vec, mat, vec, mat, vec, vec, vec, mat, ANY_SPEC],
        out_specs=(pl.BlockSpec((R, 2 * d_b), lambda i: (nr - 1 - i, col0 // (2 * d_b))), _full((CONV_WIDTH, d_b)), vec, mat, vec, mat, vec, vec, vec),
        out_shape=(_sds(dz_buf.shape, bf16), _sds((CONV_WIDTH, d_b), f32), _sds((1, d_b), f32), _sds((d_b, d_b), f32),
                   _sds((1, d_b), f32), _sds((d_b, d_b), f32), _sds((1, d_b), f32), _sds((1, d_b), f32), _sds((1, d_b), f32)),
        scratch_shapes=[pltpu.VMEM((8, d_b), f32), pltpu.VMEM((8, d_b), f32), pltpu.VMEM((R, d_b), f32)],
        input_output_aliases={15: 0},
        compiler_params=_cparams(1),
    )(z, z, z, hsave, hsave, dout, cw, cb, wa, ba, wx, bx, lam, gain, avg, dz_buf)


def _two_pass(x, m16):
    hi = x.astype(bf16)
    lo = (x - hi.astype(f32)).astype(bf16)
    return _dot(hi, m16) + _dot(lo, m16)


@jax.custom_vjp
def _group_mean(x, avg):
    return _two_pass(x, avg.astype(bf16))


def _group_mean_fwd(x, avg):
    return _group_mean(x, avg), avg


def _group_mean_bwd(avg, ct):
    return _two_pass(ct, avg.astype(bf16)), jnp.zeros_like(avg)


_group_mean.defvjp(_group_mean_fwd, _group_mean_bwd)


def _sgu_chunk(u_in, v_in, w, bexp, gain, avg, n_groups):
    C, d_c = u_in.shape
    gd = d_c // n_groups
    u = jax.nn.gelu(u_in)
    v = jax.nn.gelu(v_in)
    mu = _group_mean(v, avg)
    vc = v - mu
    var = _group_mean(vc * vc, avg)
    vh = (vc * lax.rsqrt(var + EPS)).astype(bf16)
    lane = lax.broadcasted_iota(jnp.int32, (1, d_c), 1)
    causal = lax.broadcasted_iota(jnp.int32, (C, C), 0) >= lax.broadcasted_iota(jnp.int32, (C, C), 1)
    zz = bexp
    for g in range(n_groups):
        wg = jnp.where(causal, w[g], 0.0).astype(bf16)
        zz = zz + jnp.where((lane >= g * gd) & (lane < (g + 1) * gd), _dot(wg, vh), 0.0)
    y = u * zz
    ms = _group_mean(y * y, avg)
    return y * lax.rsqrt(ms + EPS) * gain


def _sgu_inner(T):
    return C_INNER if T % (C_CHUNK * C_INNER) == 0 else 1


def _sgu_fwd(z, col0, d_c, w, bexp, gain, avg):
    T = z.shape[0]
    C = C_CHUNK
    n_in = _sgu_inner(T)
    R = C * n_in
    jb = col0 // d_c
    G = w.shape[0]

    def body(u_ref, v_ref, w_ref, b_ref, gain_ref, avg_ref, out_ref):
        w_v, b_v, gain_v, avg = w_ref[...], b_ref[...], gain_ref[...], avg_ref[...]
        for j in range(n_in):
            rows = pl.ds(j * C, C)
            out_ref[rows, :] = _sgu_chunk(u_ref[rows, :], v_ref[rows, :], w_v, b_v, gain_v, avg, G).astype(bf16)

    return pl.pallas_call(
        body, name="sgu_fwd", grid=(T // R,),
        in_specs=[pl.BlockSpec((R, d_c), lambda i: (i, jb)), pl.BlockSpec((R, d_c), lambda i: (i, jb + 1)),
                  _full((G, C, C)), _full((C, d_c)), _full((1, d_c)), _full((d_c, d_c))],
        out_specs=pl.BlockSpec((R, d_c), lambda i: (i, 0)),
        out_shape=_sds((T, d_c), bf16),
        compiler_params=_cparams(1),
    )(z, z, w, bexp, gain, avg)


def _sgu_bwd(z, col0, d_c, dout, dz_buf, w, bexp, gain, avg):
    T = z.shape[0]
    C = C_CHUNK
    n_in = _sgu_inner(T)
    R = C * n_in
    nc = T // R
    jb = col0 // d_c
    G = w.shape[0]
    gd = d_c // G

    def body(u_ref, v_ref, do_ref, w_ref, b_ref, gain_ref, avg_ref, dzin_ref, dz_ref, dw_ref, db_ref, dgain_ref, dbexp_ref):
        step = pl.program_id(0)

        @pl.when(step == 0)
        def _():
            dw_ref[...] = jnp.zeros_like(dw_ref)
            dgain_ref[...] = jnp.zeros_like(dgain_ref)
            dbexp_ref[...] = jnp.zeros_like(dbexp_ref)

        avg, w_v, b_v, gain_v = avg_ref[...], w_ref[...], b_ref[...], gain_ref[...]
        dw = dbexp = dgain = None
        for j in range(n_in):
            rows = pl.ds(j * C, C)
            _, vjp = jax.vjp(lambda a, b, c, d, e: _sgu_chunk(a, b, c, d, e, avg, G),
                             u_ref[rows, :], v_ref[rows, :], w_v, b_v, gain_v)
            du, dv, dw_j, dbexp_j, dgain_j = vjp(do_ref[rows, :])
            dz_ref[rows, 0:d_c] = du.astype(bf16)
            dz_ref[rows, d_c:2 * d_c] = dv.astype(bf16)
            dw = dw_j if dw is None else dw + dw_j
            dbexp = dbexp_j if dbexp is None else dbexp + dbexp_j
            dgain = dgain_j if dgain is None else dgain + dgain_j
        dw_ref[...] += dw
        dbexp_ref[...] += dbexp
        dgain_ref[...] += dgain

        @pl.when(step == nc - 1)
        def _():
            lane = lax.broadcasted_iota(jnp.int32, (1, d_c), 1)
            acc = dbexp_ref[...]
            for g in range(G):
                sel = jnp.where((lane >= g * gd) & (lane < (g + 1) * gd), acc, 0.0)
                db_ref[:, g:g + 1] = jnp.sum(sel, axis=1, keepdims=True)

    return pl.pallas_call(
        body, name="sgu_bwd", grid=(nc,),
        in_specs=[pl.BlockSpec((R, d_c), lambda i: (i, jb)), pl.BlockSpec((R, d_c), lambda i: (i, jb + 1)),
                  pl.BlockSpec((R, d_c), lambda i: (i, 0)),
                  _full((G, C, C)), _full((C, d_c)), _full((1, d_c)), _full((d_c, d_c)), ANY_SPEC],
        out_specs=(pl.BlockSpec((R, 2 * d_c), lambda i: (i, col0 // (2 * d_c))), _full((G, C, C)), _full((C, G)), _full((1, d_c))),
        out_shape=(_sds(dz_buf.shape, bf16), _sds((G, C, C), f32), _sds((C, G), f32), _sds((1, d_c), f32)),
        scratch_shapes=[pltpu.VMEM((C, d_c), f32)],
        input_output_aliases={7: 0},
        compiler_params=_cparams(1),
    )(z, z, dout, w, bexp, gain, avg, dz_buf)


def _loss_head(h, gain, target, tm):
    T, D = h.shape
    nt = T // tm

    def body(h_ref, gain_ref, tgt_ref, dh_ref, loss_ref, dgain_ref):
        @pl.when(pl.program_id(0) == 0)
        def _():
            loss_ref[...] = jnp.zeros_like(loss_ref)
            dgain_ref[...] = jnp.zeros_like(dgain_ref)

        hv = h_ref[...]
        gain_v = gain_ref[...]
        r = lax.rsqrt(jnp.mean(hv * hv, axis=-1, keepdims=True) + EPS)
        xh = hv * r
        e = xh * gain_v - tgt_ref[...]
        loss_ref[...] += 0.5 * jnp.sum(jnp.mean(e * e, axis=-1, keepdims=True), axis=0, keepdims=True)
        dy = e * (1.0 / D)
        dgain_ref[...] += jnp.sum(dy * xh, axis=0, keepdims=True)
        dxh = dy * gain_v
        dh_ref[...] = r * (dxh - xh * jnp.mean(dxh * xh, axis=-1, keepdims=True))

    tok = pl.BlockSpec((tm, D), lambda i: (i, 0))
    return pl.pallas_call(
        body, name="loss_head", grid=(nt,),
        in_specs=[tok, _full((1, D)), tok],
        out_specs=(tok, _full((1, 128)), _full((1, D))),
        out_shape=(_sds((T, D), f32), _sds((1, 128), f32), _sds((1, D), f32)),
        compiler_params=_cparams(1),
    )(h, gain, target)


def _lower_bounds_fn(logits):
    n = logits.shape[0]
    mx = jnp.max(logits, axis=0, keepdims=True)
    ex = jnp.exp(logits - mx)
    soft = ex / jnp.sum(ex, axis=0, keepdims=True)
    rows = [jnp.zeros_like(soft[0:1])]
    for l in range(1, n):
        rows.append(rows[-1] + soft[l:l + 1])
    return jnp.concatenate(rows, axis=0)


def _lower_bounds(logits):
    def body(x_ref, o_ref):
        o_ref[...] = _lower_bounds_fn(x_ref[...])

    return pl.pallas_call(body, name="lower_bounds", out_shape=_sds(logits.shape, f32))(logits)


def _lower_bounds_bwd(logits, dlb):
    def body(x_ref, d_ref, o_ref):
        _, vjp = jax.vjp(_lower_bounds_fn, x_ref[...])
        o_ref[...] = vjp(d_ref[...])[0]

    return pl.pallas_call(body, name="lower_bounds_bwd", out_shape=_sds(logits.shape, f32))(logits, dlb)


def _adamw(w, g, m, v, rows_blk, row_range=None, prev=(), after=None):
    R, Cc = w.shape
    lo, hi = (0, R) if row_range is None else row_range
    span = hi - lo
    rb = span if (span <= rows_blk and lo % span == 0) else math.gcd(math.gcd(span, lo), rows_blk)
    extra = list(prev) + ([] if after is None else [after])

    def body(w_ref, g_ref, m_ref, v_ref, *rest):
        d_ref, nm_ref, nv_ref, go_ref = rest[len(extra):]
        gv = g_ref[...]
        m2 = ADAM_B1 * m_ref[...] + (1.0 - ADAM_B1) * gv
        v2 = ADAM_B2 * v_ref[...] + (1.0 - ADAM_B2) * (gv * gv)
        m_hat = m2 / (1.0 - ADAM_B1 ** ADAM_STEP)
        v_hat = v2 / (1.0 - ADAM_B2 ** ADAM_STEP)
        d_ref[...] = -ADAM_LR * (m_hat / (jnp.sqrt(v_hat) + ADAM_EPS) + ADAM_WD * w_ref[...])
        nm_ref[...] = m2
        nv_ref[...] = v2
        go_ref[...] = gv

    first = lo // rb
    spec = pl.BlockSpec((rb, Cc), lambda i: (i + first, 0))
    return pl.pallas_call(
        body, name="adamw", grid=((hi - lo) // rb,),
        in_specs=[spec] * 4 + [ANY_SPEC] * len(extra), out_specs=(spec,) * 4, out_shape=(_sds((R, Cc), f32),) * 4,
        input_output_aliases={4 + j: j for j in range(len(prev))},
        compiler_params=_cparams(1),
    )(w, g, m, v, *extra)


def _pair_sum(grads, recv, c_arr):
    n = len(grads)
    nsh = grads[0].shape[0]

    def body(c_ref, *refs):
        for a in range(n):
            refs[2 * n + a][...] = (refs[a][...].astype(f32) + refs[n + a][...].astype(f32)).astype(bf16)

    g_specs, r_specs, out_shape = [], [], []
    for g in grads:
        _, R, Cc = g.shape
        r2 = R // 2
        g_specs.append(pl.BlockSpec((None, r2, Cc), lambda s, c: (s, c[0], 0)))
        r_specs.append(pl.BlockSpec((None, r2, Cc), lambda s, c: (s, 0, 0)))
        out_shape.append(_sds((nsh, r2, Cc), bf16))
    gs = pltpu.PrefetchScalarGridSpec(num_scalar_prefetch=1, grid=(nsh,), in_specs=g_specs + r_specs, out_specs=tuple(r_specs))
    return list(pl.pallas_call(body, name="pair_sum", grid_spec=gs, out_shape=tuple(out_shape),
                               compiler_params=_cparams(1))(c_arr, *grads, *recv))


def _add(a, b):
    def body(a_ref, b_ref, o_ref):
        o_ref[...] = a_ref[...] + b_ref[...]

    return pl.pallas_call(body, name="pair_sum_small", out_shape=_sds(a.shape, f32))(a, b)


def _chip_sum(hsum, recv, bufs, slot_arr, c_arr, layer, n_layers):
    n = len(hsum)
    prev = list(bufs)
    steps = 2

    def body(s_ref, c_ref, *refs):
        outs = refs[len(refs) - n:]
        for a in range(n):
            acc = refs[a][...].astype(f32)
            for j in range(N_CHIPS - 1):
                acc = acc + refs[n + a][j].astype(f32)
            outs[a][...] = acc

    h_specs, r_specs, o_specs, out_shape = [], [], [], []
    for hh in hsum:
        _, r2, Cc = hh.shape
        rt = r2 // steps
        h_specs.append(pl.BlockSpec((None, rt, Cc), lambda i, s, c: (s[0], i, 0)))
        r_specs.append(pl.BlockSpec((N_CHIPS - 1, rt, Cc), lambda i, s, c: (0, i, 0)))
        o_specs.append(pl.BlockSpec((None, rt, Cc), lambda i, s, c: (layer, c[0] * steps + i, 0)))
        out_shape.append(_sds((n_layers, 2 * r2, Cc), f32))
    gs = pltpu.PrefetchScalarGridSpec(num_scalar_prefetch=2, grid=(steps,),
                                      in_specs=h_specs + r_specs + [ANY_SPEC] * len(prev), out_specs=tuple(o_specs))
    return list(pl.pallas_call(body, name="chip_sum", grid_spec=gs, out_shape=tuple(out_shape),
                               input_output_aliases={2 + 2 * n + a: a for a in range(len(prev))},
                               compiler_params=_cparams(1))(slot_arr, c_arr, *hsum, *recv, *prev))


def _sum_slots(x):
    def body(x_ref, o_ref):
        acc = x_ref[0]
        for j in range(1, x.shape[0]):
            acc = acc + x_ref[j]
        o_ref[...] = acc

    return pl.pallas_call(body, name="sum_slots", out_shape=_sds(x.shape[1:], f32))(x)


def _blockdiag(w):
    nb, bd, _ = w.shape
    eye = jnp.eye(nb, dtype=w.dtype)
    return (eye[:, None, :, None] * w[:, :, None, :]).reshape(nb * bd, nb * bd)


def _blockdiag_extract(dense, nb):
    bd = dense.shape[0] // nb
    d4 = dense.reshape(nb, bd, nb, bd)
    return jnp.stack([d4[i, :, i, :] for i in range(nb)])


def _pack(arrays, multiple):
    flat = jnp.concatenate([a.reshape(-1).astype(f32) for a in arrays])
    pad = (-flat.shape[0]) % multiple
    return jnp.pad(flat, (0, pad))


def _unpack(flat, shapes):
    out, off = [], 0
    for s in shapes:
        n = int(np.prod(s))
        out.append(flat[off:off + n].reshape(s))
        off += n
    return out


BIG = ("ffn1_wg", "ffn1_wu", "ffn1_wd", "w_in", "w_out", "ffn2_wg", "ffn2_wu", "ffn2_wd")
TRANSPOSED = ("ffn1_wg", "ffn1_wu", "ffn2_wg", "ffn2_wu")
SMALL = ("ffn1_norm", "mix_norm", "hgrn_lb_logits", "hgrn_norm", "conv_w", "conv_b", "lru_wa", "lru_ba", "lru_wx",
         "lru_bx", "lru_lambda", "lru_norm", "sgu_w", "sgu_b", "sgu_norm", "ffn2_norm", "final_norm")
WEIGHTS = ("ffn1_norm", "ffn1_wg", "ffn1_wu", "ffn1_wd", "mix_norm", "w_in", "hgrn_lb_logits", "hgrn_norm", "conv_w",
           "conv_b", "lru_wa", "lru_ba", "lru_wx", "lru_bx", "lru_lambda", "lru_norm", "sgu_w", "sgu_b", "sgu_norm",
           "w_out", "ffn2_norm", "ffn2_wg", "ffn2_wu", "ffn2_wd", "final_norm")


def kernel(x, ffn1_norm, ffn1_wg, ffn1_wu, ffn1_wd, mix_norm, w_in, hgrn_lb_logits, hgrn_norm, conv_w, conv_b, lru_wa, lru_ba, lru_wx, lru_bx, lru_lambda, lru_norm, sgu_w, sgu_b, sgu_norm, w_out, ffn2_norm, ffn2_wg, ffn2_wu, ffn2_wd, final_norm, loss_target, m_ffn1_norm, m_ffn1_wg, m_ffn1_wu, m_ffn1_wd, m_mix_norm, m_w_in, m_hgrn_lb_logits, m_hgrn_norm, m_conv_w, m_conv_b, m_lru_wa, m_lru_ba, m_lru_wx, m_lru_bx, m_lru_lambda, m_lru_norm, m_sgu_w, m_sgu_b, m_sgu_norm, m_w_out, m_ffn2_norm, m_ffn2_wg, m_ffn2_wu, m_ffn2_wd, m_final_norm, v_ffn1_norm, v_ffn1_wg, v_ffn1_wu, v_ffn1_wd, v_mix_norm, v_w_in, v_hgrn_lb_logits, v_hgrn_norm, v_conv_w, v_conv_b, v_lru_wa, v_lru_ba, v_lru_wx, v_lru_bx, v_lru_lambda, v_lru_norm, v_sgu_w, v_sgu_b, v_sgu_norm, v_w_out, v_ffn2_norm, v_ffn2_wg, v_ffn2_wu, v_ffn2_wd, v_final_norm):
    args = dict(locals())
    W = {n: args[n] for n in WEIGHTS}
    M = {n: args["m_" + n] for n in WEIGHTS}
    V = {n: args["v_" + n] for n in WEIGHTS}

    T, D = x.shape[1], x.shape[2]
    L = ffn1_norm.shape[0]
    d_a, d_b, d_c = hgrn_norm.shape[1], lru_norm.shape[1], sgu_norm.shape[1]
    col_b, col_c = 4 * d_a, 4 * d_a + 2 * d_b
    tm = 512 if T % 512 == 0 else T
    tm_w = 1024 if T % 1024 == 0 else tm
    tm_d = 256 if T % 256 == 0 else tm
    my_c = lax.axis_index("c")
    my_slot = 2 * lax.axis_index("x") + lax.axis_index("y")
    c_arr = jnp.reshape(my_c, (1,)).astype(jnp.int32)
    slot_arr = jnp.reshape(my_slot, (1,)).astype(jnp.int32)

    nb = len(BIG)
    gplan = _gather_ici_plan(nb)

    def kview(a, n):
        return jnp.swapaxes(a, 1, 2) if n in TRANSPOSED else a

    Wk = {n: kview(W[n], n) for n in BIG}
    place_steps = 4 if all(Wk[n].shape[1] % 64 == 0 for n in BIG) else 2

    conv_land = lax.dynamic_update_slice_in_dim(jnp.zeros((N_CHIPS,) + conv_w.shape, f32), conv_w[None], my_slot, axis=0)
    lands0 = _cast_place([Wk[n] for n in BIG], 0, slot_arr, place_steps)
    n_first = 3
    first_plan = _gather_ici_plan(n_first + 1)
    send, recv, _, first, token = _start_copies("gather_start_first", [], lands0[:n_first] + [conv_land], first_plan)
    later = {l: _cast_place([Wk[n] for n in BIG], l, slot_arr, place_steps, after=token) for l in range(1, L)}
    got = _wait_copies("gather_wait_first", send, recv, [], first, first_plan, later[L - 1][0] if later else lands0[-1])
    got = _gather_d2d("gather0_d2d", got)

    def placed(l):
        return later[l]
    G = [None] * L
    G[0] = dict(zip(BIG[:n_first], got[:n_first]))
    conv_full = jnp.transpose(got[n_first], (1, 2, 0, 3)).reshape(L, CONV_WIDTH, d_b)
    rest_plan = _gather_ici_plan(nb - n_first)
    rest_pending = _start_copies("gather_start_0", [], lands0[n_first:], rest_plan, got[0])

    def start_gather(l, after):
        return _start_copies(f"gather_start_{l}", [], placed(l), gplan, after)

    d2d_plan = _gather_d2d_plan(nb)

    lb = _lower_bounds(hgrn_lb_logits)
    avg_b = _group_avg_matrix(d_b, d_b // B_BLOCKS)
    avg_c = _group_avg_matrix(d_c, d_c // C_GROUPS)
    wa_dense = [_blockdiag(lru_wa[l]) for l in range(L)]
    wx_dense = [_blockdiag(lru_wx[l]) for l in range(L)]
    bexp = [jnp.repeat(sgu_b[l].T, d_c // C_GROUPS, axis=1) for l in range(L)]

    def lru_params(l):
        return (conv_full[l], conv_b[l][None], wa_dense[l], lru_ba[l].reshape(1, d_b), wx_dense[l],
                lru_bx[l].reshape(1, d_b), lru_lambda[l][None], lru_norm[l][None], avg_b)

    h = x.reshape(T, D)
    saved = []
    for l in range(L):
        s = {"h0": h}
        gain1, gain_mix = ffn1_norm[l][None], mix_norm[l][None]
        pending = None
        if l == 0:
            gain1 = gain1 + rest_pending[4][0:1, 0:1]
        elif l + 1 < L:
            pending = start_gather(l + 1, h)
            gain1 = gain1 + pending[4][0:1, 0:1]
        g = G[l]
        h, s["g1"], s["u1"] = _ffn_fwd(h, gain1, g["ffn1_wg"], g["ffn1_wu"], g["ffn1_wd"], tm)
        s["h1"] = h
        if l == 0:
            send, recv, _, lands, _ = rest_pending
            lands = _wait_copies("gather_wait_0", send, recv, [], lands, rest_plan, h)
            g.update(zip(BIG[n_first:], _gather_d2d("gather_d2d", lands)))
            if L > 1:
                pending = start_gather(1, g["w_in"])
                gain_mix = gain_mix + pending[4][0:1, 0:1]
        z = _proj_in_fwd(h, gain_mix, g["w_in"], tm)
        s["z"] = z
        s["oa"], s["o_pre"], s["states"] = _hgrn_fwd(z, lb[l][None], hgrn_norm[l][None], d_a)
        s["ob"], s["hl"] = _lru_fwd(z, col_b, d_b, *lru_params(l))
        s["oc"] = _sgu_fwd(z, col_c, d_c, sgu_w[l], bexp[l], sgu_norm[l][None], avg_c)
        h = _proj_out_fwd(h, s["oa"], s["ob"], s["oc"], g["w_out"], tm)
        s["h2"] = h
        gain2 = ffn2_norm[l][None]
        forward = None
        if pending is not None:
            send, recv, _, lands, _ = pending
            lands = _wait_copies(f"gather_wait_{l + 1}", send, recv, [], lands, gplan, h)
            forward = _start_copies(f"gather_d2d_start_{l + 1}", [], lands, d2d_plan)
            gain2 = gain2 + forward[4][0:1, 0:1]
        h, s["g2"], s["u2"] = _ffn_fwd(h, gain2, g["ffn2_wg"], g["ffn2_wu"], g["ffn2_wd"], tm)
        saved.append(s)
        if forward is not None:
            send, recv, _, lands, _ = forward
            G[l + 1] = dict(zip(BIG, _wait_copies(f"gather_d2d_wait_{l + 1}", send, recv, [], lands, d2d_plan, h)))

    dh, loss_part, d_final = _loss_head(h, final_norm[None], loss_target.reshape(T, D), tm)
    loss = lax.psum(loss_part[0, 0], ("x", "y", "c"))

    def pair_views(n_big):
        r = [(lambda i, o, p, a=a: i[a].at[:, pl.ds((1 - p.c) * (i[a].shape[1] // 2), i[a].shape[1] // 2)],
              lambda i, o, p, a=a: o[a], "sib") for a in range(n_big)]
        return r

    def chip_plan_for(n):
        return [(lambda s_, o, p, a=a, kind=kind: s_[a].at[p.peer_slot(kind)], lambda s_, o, p, a=a, j=j: o[a].at[j], kind)
                for a in range(n) for j, kind in enumerate(CHIP_KINDS)]

    chip_plan = chip_plan_for(nb)
    sbufs = {n: None for n in BIG}

    def pair_phase(arrs, extra=None):
        n = len(arrs)
        ins, remote = list(arrs), pair_views(n)
        outs = [_sds((N_CHIPS, a.shape[1] // 2, a.shape[2]), bf16) for a in arrs]
        if extra is not None:
            ins.append(extra)
            outs.append(_sds(extra.shape, f32))
            remote = remote + [(lambda i, o, p: i[n], lambda i, o, p: o[n], "sib")]
        recv = _exchange("grad_pair_d2d", ins, outs, remote)
        return _pair_sum(arrs, recv[:n], c_arr), (None if extra is None else _add(extra, recv[n]))

    def chip_sum_into(names, hs, lands, l):
        prev = [sbufs[n] for n in names] if sbufs[names[0]] is not None else []
        for n, buf in zip(names, _chip_sum(hs, lands, prev, slot_arr, c_arr, l, L)):
            sbufs[n] = buf

    def share_plan(l):
        view = lambda a: (lambda s_, o, p: _half(o[a].at[l], p.c))
        return [(view(a), view(a), "sib") for a in range(nb)]

    def share_start(l):
        send, recv, _, bufs, token = _start_copies(f"grad_share_start_{l}", [], [sbufs[n] for n in BIG], share_plan(l))
        for n, buf in zip(BIG, bufs):
            sbufs[n] = buf
        return (l, send, recv), token

    def share_wait(pending, after):
        l, send, recv = pending
        bufs = _wait_copies(f"grad_share_wait_{l}", send, recv, [], [sbufs[n] for n in BIG], share_plan(l), after)
        for n, buf in zip(BIG, bufs):
            sbufs[n] = buf

    def share(l, extra_in=(), extra_out=(), extra_remote=(), extra_local=()):
        remote = [(lambda i, o, p, a=a: _half(o[a].at[l], p.c), lambda i, o, p, a=a: _half(o[a].at[l], p.c), "sib")
                  for a in range(nb)]
        outs = [_sds(sbufs[n].shape, f32) for n in BIG] + list(extra_out)
        res = _exchange("grad_share_d2d", [sbufs[n] for n in BIG] + list(extra_in), outs, remote + list(extra_remote),
                        list(extra_local), aliases={a: a for a in range(nb)})
        for n, buf in zip(BIG, res[:nb]):
            sbufs[n] = buf
        return res[nb:]

    small = {n: [None] * L for n in SMALL if n != "final_norm"}
    chip_pending = pair_pending = early = share_token = None
    shares = []
    early_names = ("w_in", "w_out", "ffn2_wg", "ffn2_wu", "ffn2_wd")
    for l in reversed(range(L)):
        s, g = saved[l], G[l]
        gain2, gain_a = ffn2_norm[l][None], hgrn_norm[l][None]
        if pair_pending is not None:
            gain2 = gain2 + pair_pending[0][4][0:1, 0:1]
        if share_token is not None:
            gain2 = gain2 + share_token[0:1, 0:1]
            share_token = None
        dh, small["ffn2_norm"][l], dg, du, xn, dob = _ffn_bwd_dgrad(
            s["h2"], gain2, dh, s["g2"], s["u2"], g["ffn2_wg"], g["ffn2_wu"], g["ffn2_wd"], tm_d)
        if pair_pending is not None:
            (send, recv, grads_prev, lands, _), = pair_pending
            recv_a = _wait_copies(f"grad_pair_wait_{l + 1}", send, recv, grads_prev, lands, pair_views(nb), dh)
            hsum = _pair_sum(grads_prev, recv_a, c_arr)
            lands = [lax.empty((N_CHIPS - 1,) + hh.shape[1:], bf16) for hh in hsum]
            chip_pending = (_start_copies(f"grad_chip_start_{l + 1}", hsum, lands, chip_plan), hsum)
            gain_a = gain_a + chip_pending[0][4][0:1, 0:1]
            pair_pending = None
        dwg2, dwu2, dwd2 = _ffn_bwd_wgrad(xn, dob, s["g2"], s["u2"], dg, du, tm_w)
        doa, dob_, doc, dwo = _proj_out_bwd(dh, s["oa"], s["ob"], s["oc"], g["w_out"], tm)
        dz, small["hgrn_lb_logits"][l], small["hgrn_norm"][l] = _hgrn_bwd(
            s["z"], lb[l][None], gain_a, s["o_pre"], s["states"], doa, d_a)
        (dz, small["conv_w"][l], small["conv_b"][l], dwa, small["lru_ba"][l], dwx, small["lru_bx"][l],
         small["lru_lambda"][l], small["lru_norm"][l]) = _lru_bwd(s["z"], col_b, d_b, s["hl"], dob_, dz, *lru_params(l))
        small["lru_wa"][l] = _blockdiag_extract(dwa, B_BLOCKS)
        small["lru_wx"][l] = _blockdiag_extract(dwx, B_BLOCKS)
        dz, small["sgu_w"][l], dsb, small["sgu_norm"][l] = _sgu_bwd(
            s["z"], col_c, d_c, doc, dz, sgu_w[l], bexp[l], sgu_norm[l][None], avg_c)
        small["sgu_b"][l] = dsb.T
        dh, small["mix_norm"][l], xn = _proj_in_bwd_dgrad(s["h1"], mix_norm[l][None], dh, dz, g["w_in"], tm)
        dwi = _proj_in_bwd_wgrad(xn, dz, N_CHIPS)
        gain1 = ffn1_norm[l][None]
        if l == 0:
            hs_e, _ = pair_phase([dwi, dwo, dwg2, dwu2, dwd2])
            lands = [lax.empty((N_CHIPS - 1,) + hh.shape[1:], bf16) for hh in hs_e]
            early = (_start_copies("grad_chip_start_0", hs_e, lands, chip_plan_for(len(hs_e))), hs_e)
            gain1 = gain1 + early[0][4][0:1, 0:1]
        dh, small["ffn1_norm"][l], dg, du, xn, dob = _ffn_bwd_dgrad(
            s["h0"], gain1, dh, s["g1"], s["u1"], g["ffn1_wg"], g["ffn1_wu"], g["ffn1_wd"], tm_d)
        dwg1, dwu1, dwd1 = _ffn_bwd_wgrad(xn, dob, s["g1"], s["u1"], dg, du, tm_w)
        layer_grads = [dwg1, dwu1, dwd1, dwi, dwo, dwg2, dwu2, dwd2]

        if chip_pending is not None:
            (send, recv, hs, lands, _), hsum_prev = chip_pending
            lands = _wait_copies(f"grad_chip_wait_{l + 1}", send, recv, hs, lands, chip_plan, dwg1)
            chip_sum_into(BIG, hsum_prev, lands, l + 1)
            pending_share, share_token = share_start(l + 1)
            shares.append(pending_share)
            chip_pending = None
        if l > 0:
            lands = [lax.empty((N_CHIPS, gr.shape[1] // 2, gr.shape[2]), bf16) for gr in layer_grads]
            pair_pending = (_start_copies(f"grad_pair_start_{l}", layer_grads, lands, pair_views(nb)),)
    grad_x = dh.reshape(x.shape)

    (send, recv, hs, lands, _), hs_e = early
    lands = _wait_copies("grad_chip_wait_0", send, recv, hs, lands, chip_plan_for(len(hs_e)), dwg1)
    chip_sum_into(early_names, hs_e, lands, 0)
    small_names = [n for n in SMALL]
    small_parts = [jnp.stack([jnp.reshape(v, (-1,)) for v in small[n]]) if n != "final_norm" else d_final for n in small_names]
    small_shapes = [p.shape for p in small_parts]
    packed = _pack(small_parts, 2 * 8 * 128).reshape(2, -1, 128)
    n_rows = packed.shape[1]
    late = [dwg1, dwu1, dwd1]
    nl = len(late)
    hsum, small_pair = pair_phase(late, packed)
    own_half = lax.dynamic_index_in_dim(small_pair, my_c, 0, keepdims=True)
    small_land = lax.dynamic_update_slice_in_dim(jnp.zeros((N_CHIPS, n_rows, 128), f32), own_half, my_slot, axis=0)
    late_plan = chip_plan_for(nl) + [(lambda s_, o, p: s_[nl].at[p.c], lambda s_, o, p: o[nl].at[p.slot], kind)
                                     for kind in CHIP_KINDS]
    lands = [lax.empty((N_CHIPS - 1,) + hh.shape[1:], bf16) for hh in hsum] + [small_land]
    send, recv, srcs, lands, token = _start_copies("grad_chip_start_last", hsum + [small_pair], lands, late_plan)

    def adam_operands(n):
        shape = Wk[n].shape
        rows_blk = 512 if shape[1] % 512 == 0 else (shape[1] // 2 if shape[1] > 512 else shape[1])
        return [a.reshape(-1, shape[-1]) for a in (Wk[n], sbufs[n], kview(M[n], n), kview(V[n], n))], rows_blk, shape

    for pending_share in shares:
        share_wait(pending_share, token)
    partial = {}
    if L > 1:
        for n in BIG:
            flat, rows_blk, shape = adam_operands(n)
            partial[n] = _adamw(*flat, rows_blk, row_range=(shape[1], L * shape[1]), after=token)
    recv_b = _wait_copies("grad_chip_wait_last", send, recv, srcs, lands, late_plan,
                          partial[BIG[-1]][0] if partial else hsum[0])
    chip_sum_into(BIG[:nl], hsum, recv_b[:nl], 0)
    small_half = _sum_slots(recv_b[nl])
    (small_all,) = share(0, extra_in=[small_half], extra_out=[_sds(packed.shape, f32)],
                         extra_remote=[(lambda i, o, p: i[nb], lambda i, o, p: o[nb].at[p.c], "sib")],
                         extra_local=[(lambda i, o, p: i[nb], lambda i, o, p: o[nb].at[p.c])])
    grads = {n: kview(sbufs[n], n) for n in BIG}
    small_tot = _unpack(small_all.reshape(-1), small_shapes)
    for n, val in zip(small_names, small_tot):
        grads[n] = val
    grads["hgrn_lb_logits"] = _lower_bounds_bwd(hgrn_lb_logits, grads["hgrn_lb_logits"])
    shard_cols = conv_w.shape[2]
    grads["conv_w"] = lax.dynamic_slice_in_dim(grads["conv_w"].reshape(L, CONV_WIDTH, d_b), my_slot * shard_cols, shard_cols, axis=2)
    for n in SMALL:
        grads[n] = grads[n].reshape(W[n].shape)

    delta, new_m, new_v = {}, {}, {}
    for n in BIG:
        flat, rows_blk, shape = adam_operands(n)
        outs = _adamw(*flat, rows_blk, row_range=(0, shape[1]), prev=partial[n]) if partial else _adamw(*flat, rows_blk)
        delta[n], new_m[n], new_v[n], grads[n] = [kview(o.reshape(shape), n) for o in outs]
    shapes = [W[n].shape for n in SMALL]
    packs = [_pack([src[n] for n in SMALL], 8 * 128).reshape(-1, 128) for src in (W, grads, M, V)]
    d2, m2, v2, _ = _adamw(*packs, 4096)
    for dst, val in ((delta, d2), (new_m, m2), (new_v, v2)):
        for n, piece in zip(SMALL, _unpack(val.reshape(-1), shapes)):
            dst[n] = piece

    return (loss, grad_x, *[grads[n] for n in WEIGHTS], *[delta[n] for n in WEIGHTS],
            *[new_m[n] for n in WEIGHTS], *[new_v[n] for n in WEIGHTS])
```
